```python
import jax, jax.numpy as jnp
from jax import lax
import numpy as np

D_MODEL = 1024
BATCH = 8
SEQ = 8192
DEPTH = 4

HEAD_DIM = 64
A_HEADS = 8
A_KV_HEADS = 2
B_HEADS = 8
B_KV_HEADS = 2
WINDOW = 128
BLOCK = 128
ROPE_THETA = 10000.0
GRID_W = 64
SGU_WIDTH = D_MODEL
SGU_GROUPS = 8
SGU_CHUNK = 128
D_FF = 4 * D_MODEL
EPS = 1e-6
N_ATT_LAYERS = (DEPTH + 1) // 2
N_SGU_LAYERS = DEPTH // 2

A_Q = A_HEADS * HEAD_DIM
A_KV = A_KV_HEADS * HEAD_DIM
B_Q = B_HEADS * HEAD_DIM
B_KV = B_KV_HEADS * HEAD_DIM
ATT_IN = A_Q + 2 * A_KV + B_Q + 2 * B_KV
ATT_OUT_IN = A_Q + B_Q

kernel_name = "hybrid_window_grid_attn_sgu_encoder"


def _rmsnorm(x, g):
    xf = x.astype(jnp.float32)
    y = xf * lax.rsqrt(jnp.mean(xf * xf, axis=-1, keepdims=True) + EPS)
    return (y * g.astype(jnp.float32)).astype(x.dtype)


def _layernorm(x, g, b):
    xf = x.astype(jnp.float32)
    mu = jnp.mean(xf, axis=-1, keepdims=True)
    var = jnp.mean(jnp.square(xf - mu), axis=-1, keepdims=True)
    y = (xf - mu) * lax.rsqrt(var + EPS)
    return (y * g.astype(jnp.float32) + b.astype(jnp.float32)).astype(x.dtype)


def _rope_angles(pos, dim):
    freqs = ROPE_THETA ** (-jnp.arange(0, dim, 2, dtype=jnp.float32) / dim)
    ang = pos.astype(jnp.float32)[:, None] * freqs[None, :]
    return jnp.cos(ang), jnp.sin(ang)


def _apply_rope(x, cos, sin):
    xf = x.astype(jnp.float32)
    half = xf.shape[-1] // 2
    x1, x2 = xf[..., :half], xf[..., half:]
    c, s = cos[None, :, None, :], sin[None, :, None, :]
    return jnp.concatenate([x1 * c - x2 * s, x2 * c + x1 * s], axis=-1).astype(x.dtype)


def _apply_axial_rope(x, cos_r, sin_r, cos_c, sin_c):
    half = x.shape[-1] // 2
    return jnp.concatenate([_apply_rope(x[..., :half], cos_r, sin_r),
                            _apply_rope(x[..., half:], cos_c, sin_c)], axis=-1)


def _window_attention(q, k, v, sink):
    bsz, s_len = q.shape[0], q.shape[1]
    nb = s_len // BLOCK
    g = A_HEADS // A_KV_HEADS
    scale = HEAD_DIM ** -0.5
    qb = q.reshape(bsz, nb, BLOCK, A_KV_HEADS, g, HEAD_DIM).astype(jnp.float32)
    pad = ((0, 0), (BLOCK, BLOCK), (0, 0), (0, 0))
    kp = jnp.pad(k, pad).reshape(bsz, nb + 2, BLOCK, A_KV_HEADS, HEAD_DIM)
    vp = jnp.pad(v, pad).reshape(bsz, nb + 2, BLOCK, A_KV_HEADS, HEAD_DIM)
    kband = jnp.concatenate([kp[:, :-2], kp[:, 1:-1], kp[:, 2:]], axis=2).astype(jnp.float32)
    vband = jnp.concatenate([vp[:, :-2], vp[:, 1:-1], vp[:, 2:]], axis=2).astype(jnp.float32)
    s = jnp.einsum('bnqhgd,bnjhd->bnhgqj', qb, kband) * scale
    qi = jnp.arange(BLOCK)
    kj = jnp.arange(3 * BLOCK)
    rel = kj[None, :] - BLOCK - qi[:, None]
    kpos = jnp.arange(nb)[:, None] * BLOCK - BLOCK + kj[None, :]
    mask = (jnp.abs(rel) <= WINDOW)[None, :, :] & ((kpos >= 0) & (kpos < s_len))[:, None, :]
    s = jnp.where(mask[None, :, None, None, :, :], s, -1e30)
    sink_b = sink.astype(jnp.float32).reshape(A_KV_HEADS, g)[None, None, :, :, None, None]
    m = jnp.maximum(jnp.max(s, axis=-1, keepdims=True), sink_b)
    p = jnp.exp(s - m)
    denom = jnp.sum(p, axis=-1, keepdims=True) + jnp.exp(sink_b - m)
    o = jnp.einsum('bnhgqj,bnjhd->bnqhgd', p / denom, vband)
    return o.reshape(bsz, s_len, A_Q).astype(q.dtype)


def _grid_attention(q, k, v):
    bsz, s_len = q.shape[0], q.shape[1]
    nb = s_len // BLOCK
    g = B_HEADS // B_KV_HEADS
    scale = HEAD_DIM ** -0.5
    qb = q.reshape(bsz, nb, BLOCK, B_KV_HEADS, g, HEAD_DIM).transpose(1, 0, 2, 3, 4, 5)
    kf = k.astype(jnp.float32)
    vf = v.astype(jnp.float32)

    def one_block(qblk):
        s = jnp.einsum('bqhgd,bkhd->bhgqk', qblk.astype(jnp.float32), kf) * scale
        p = jax.nn.softmax(s, axis=-1)
        return jnp.einsum('bhgqk,bkhd->bqhgd', p, vf)

    o = lax.map(one_block, qb)
    return o.transpose(1, 0, 2, 3, 4, 5).reshape(bsz, s_len, B_Q).astype(q.dtype)


def _attention_layer(x, norm_g, w_in, sink, qn_g, kn_g, w_out,
                     cos1, sin1, cos_r, sin_r, cos_c, sin_c):
    bsz, s_len = x.shape[0], x.shape[1]
    h = _rmsnorm(x, norm_g)
    proj = h @ w_in
    offs = [A_Q, A_Q + A_KV, A_Q + 2 * A_KV, A_Q + 2 * A_KV + B_Q, A_Q + 2 * A_KV + B_Q + B_KV]
    qa, ka, va, qb, kb, vb = jnp.split(proj, offs, axis=-1)
    qa = qa.reshape(bsz, s_len, A_HEADS, HEAD_DIM)
    ka = ka.reshape(bsz, s_len, A_KV_HEADS, HEAD_DIM)
    va = va.reshape(bsz, s_len, A_KV_HEADS, HEAD_DIM)
    qb = qb.reshape(bsz, s_len, B_HEADS, HEAD_DIM)
    kb = kb.reshape(bsz, s_len, B_KV_HEADS, HEAD_DIM)
    vb = vb.reshape(bsz, s_len, B_KV_HEADS, HEAD_DIM)
    qa = _apply_rope(qa, cos1, sin1)
    ka = _apply_rope(ka, cos1, sin1)
    oa = _window_attention(qa, ka, va, sink)
    qb = _apply_axial_rope(_rmsnorm(qb, qn_g), cos_r, sin_r, cos_c, sin_c)
    kb = _apply_axial_rope(_rmsnorm(kb, kn_g), cos_r, sin_r, cos_c, sin_c)
    ob = _grid_attention(qb, kb, vb)
    return x + jnp.concatenate([oa, ob], axis=-1) @ w_out


def _sgu_layer(x, norm_g, w_in, ln_g, ln_b, w_s, b_s, w_out):
    bsz, s_len = x.shape[0], x.shape[1]
    nc = s_len // SGU_CHUNK
    dg = SGU_WIDTH // SGU_GROUPS
    h = _rmsnorm(x, norm_g)
    z = jax.nn.gelu(h @ w_in)
    u, v = jnp.split(z, 2, axis=-1)
    v = _layernorm(v, ln_g, ln_b)
    vb = v.reshape(bsz, nc, SGU_CHUNK, SGU_GROUPS, dg)
    mixed = jnp.einsum('gpq,bnqgd->bnpgd', w_s, vb) + b_s.T[None, None, :, :, None]
    y = u * mixed.reshape(bsz, s_len, SGU_WIDTH)
    return x + y @ w_out


def _mlp(x, norm_g, w1, w2):
    h = _rmsnorm(x, norm_g)
    return x + jnp.square(jax.nn.relu(h @ w1)) @ w2


def _fwd_setup_inputs(seed: int = 0) -> dict:
    key = jax.random.key(seed)
    ks = jax.random.split(key, 20)
    f32 = jnp.float32
    nrm = lambda k, shape, s: jax.random.normal(k, shape, f32) * s
    return {
        "x": jax.random.normal(ks[0], (BATCH, SEQ, D_MODEL), f32),
        "att_norm": 1.0 + nrm(ks[1], (N_ATT_LAYERS, D_MODEL), 0.02),
        "att_w_in": nrm(ks[2], (N_ATT_LAYERS, D_MODEL, ATT_IN), D_MODEL ** -0.5),
        "att_sink": nrm(ks[3], (N_ATT_LAYERS, A_HEADS), 0.5),
        "att_qnorm": 1.0 + nrm(ks[4], (N_ATT_LAYERS, HEAD_DIM), 0.02),
        "att_knorm": 1.0 + nrm(ks[5], (N_ATT_LAYERS, HEAD_DIM), 0.02),
        "att_w_out": nrm(ks[6], (N_ATT_LAYERS, ATT_OUT_IN, D_MODEL), ATT_OUT_IN ** -0.5),
        "sgu_norm": 1.0 + nrm(ks[7], (N_SGU_LAYERS, D_MODEL), 0.02),
        "sgu_w_in": nrm(ks[8], (N_SGU_LAYERS, D_MODEL, 2 * SGU_WIDTH), D_MODEL ** -0.5),
        "sgu_ln_g": 1.0 + nrm(ks[9], (N_SGU_LAYERS, SGU_WIDTH), 0.02),
        "sgu_ln_b": nrm(ks[10], (N_SGU_LAYERS, SGU_WIDTH), 0.02),
        "sgu_w_s": nrm(ks[11], (N_SGU_LAYERS, SGU_GROUPS, SGU_CHUNK, SGU_CHUNK), SGU_CHUNK ** -0.5),
        "sgu_b_s": 1.0 + nrm(ks[12], (N_SGU_LAYERS, SGU_GROUPS, SGU_CHUNK), 0.1),
        "sgu_w_out": nrm(ks[13], (N_SGU_LAYERS, SGU_WIDTH, D_MODEL), SGU_WIDTH ** -0.5),
        "mlp_norm": 1.0 + nrm(ks[14], (DEPTH, D_MODEL), 0.02),
        "mlp_w1": nrm(ks[15], (DEPTH, D_MODEL, D_FF), D_MODEL ** -0.5),
        "mlp_w2": nrm(ks[16], (DEPTH, D_FF, D_MODEL), D_FF ** -0.5),
        "final_norm": 1.0 + nrm(ks[17], (D_MODEL,), 0.02),
    }


def _fwd_reference(x, att_norm, att_w_in, att_sink, att_qnorm, att_knorm, att_w_out,
              sgu_norm, sgu_w_in, sgu_ln_g, sgu_ln_b, sgu_w_s, sgu_b_s, sgu_w_out,
              mlp_norm, mlp_w1, mlp_w2, final_norm):
    s_len = x.shape[1]
    pos = jnp.arange(s_len)
    rows = s_len // GRID_W
    row_idx = jnp.repeat(jnp.arange(rows), GRID_W)
    col_idx = jnp.tile(jnp.arange(GRID_W), rows)
    cos1, sin1 = _rope_angles(pos, HEAD_DIM)
    cos_r, sin_r = _rope_angles(row_idx, HEAD_DIM // 2)
    cos_c, sin_c = _rope_angles(col_idx, HEAD_DIM // 2)
    h = x
    for layer in range(DEPTH):
        i = layer // 2
        if layer % 2 == 0:
            h = _attention_layer(h, att_norm[i], att_w_in[i], att_sink[i], att_qnorm[i],
                                 att_knorm[i], att_w_out[i],
                                 cos1, sin1, cos_r, sin_r, cos_c, sin_c)
        else:
            h = _sgu_layer(h, sgu_norm[i], sgu_w_in[i], sgu_ln_g[i], sgu_ln_b[i],
                           sgu_w_s[i], sgu_b_s[i], sgu_w_out[i])
        h = _mlp(h, mlp_norm[layer], mlp_w1[layer], mlp_w2[layer])
    return _rmsnorm(h, final_norm)


import jax as _jax
import jax.numpy as _jnp

TWIN_FORMAT = 'train_step'
FWD_PARAMS = ['x', 'att_norm', 'att_w_in', 'att_sink', 'att_qnorm', 'att_knorm', 'att_w_out', 'sgu_norm', 'sgu_w_in', 'sgu_ln_g', 'sgu_ln_b', 'sgu_w_s', 'sgu_b_s', 'sgu_w_out', 'mlp_norm', 'mlp_w1', 'mlp_w2', 'final_norm']
TWIN_WEIGHTS = ['att_norm', 'att_w_in', 'att_sink', 'att_qnorm', 'att_knorm', 'att_w_out', 'sgu_norm', 'sgu_w_in', 'sgu_ln_g', 'sgu_ln_b', 'sgu_w_s', 'sgu_b_s', 'sgu_w_out', 'mlp_norm', 'mlp_w1', 'mlp_w2', 'final_norm']
TWIN_DIFF_INPUT = 'x'
TWIN_INPUTS = ['x', 'att_norm', 'att_w_in', 'att_sink', 'att_qnorm', 'att_knorm', 'att_w_out', 'sgu_norm', 'sgu_w_in', 'sgu_ln_g', 'sgu_ln_b', 'sgu_w_s', 'sgu_b_s', 'sgu_w_out', 'mlp_norm', 'mlp_w1', 'mlp_w2', 'final_norm', 'loss_target', 'm_att_norm', 'm_att_w_in', 'm_att_sink', 'm_att_qnorm', 'm_att_knorm', 'm_att_w_out', 'm_sgu_norm', 'm_sgu_w_in', 'm_sgu_ln_g', 'm_sgu_ln_b', 'm_sgu_w_s', 'm_sgu_b_s', 'm_sgu_w_out', 'm_mlp_norm', 'm_mlp_w1', 'm_mlp_w2', 'm_final_norm', 'v_att_norm', 'v_att_w_in', 'v_att_sink', 'v_att_qnorm', 'v_att_knorm', 'v_att_w_out', 'v_sgu_norm', 'v_sgu_w_in', 'v_sgu_ln_g', 'v_sgu_ln_b', 'v_sgu_w_s', 'v_sgu_b_s', 'v_sgu_w_out', 'v_mlp_norm', 'v_mlp_w1', 'v_mlp_w2', 'v_final_norm']
TWIN_OUTPUTS = ['loss', 'grad_x', 'grad_att_norm', 'grad_att_w_in', 'grad_att_sink', 'grad_att_qnorm', 'grad_att_knorm', 'grad_att_w_out', 'grad_sgu_norm', 'grad_sgu_w_in', 'grad_sgu_ln_g', 'grad_sgu_ln_b', 'grad_sgu_w_s', 'grad_sgu_b_s', 'grad_sgu_w_out', 'grad_mlp_norm', 'grad_mlp_w1', 'grad_mlp_w2', 'grad_final_norm', 'delta_att_norm', 'delta_att_w_in', 'delta_att_sink', 'delta_att_qnorm', 'delta_att_knorm', 'delta_att_w_out', 'delta_sgu_norm', 'delta_sgu_w_in', 'delta_sgu_ln_g', 'delta_sgu_ln_b', 'delta_sgu_w_s', 'delta_sgu_b_s', 'delta_sgu_w_out', 'delta_mlp_norm', 'delta_mlp_w1', 'delta_mlp_w2', 'delta_final_norm', 'new_m_att_norm', 'new_m_att_w_in', 'new_m_att_sink', 'new_m_att_qnorm', 'new_m_att_knorm', 'new_m_att_w_out', 'new_m_sgu_norm', 'new_m_sgu_w_in', 'new_m_sgu_ln_g', 'new_m_sgu_ln_b', 'new_m_sgu_w_s', 'new_m_sgu_b_s', 'new_m_sgu_w_out', 'new_m_mlp_norm', 'new_m_mlp_w1', 'new_m_mlp_w2', 'new_m_final_norm', 'new_v_att_norm', 'new_v_att_w_in', 'new_v_att_sink', 'new_v_att_qnorm', 'new_v_att_knorm', 'new_v_att_w_out', 'new_v_sgu_norm', 'new_v_sgu_w_in', 'new_v_sgu_ln_g', 'new_v_sgu_ln_b', 'new_v_sgu_w_s', 'new_v_sgu_b_s', 'new_v_sgu_w_out', 'new_v_mlp_norm', 'new_v_mlp_w1', 'new_v_mlp_w2', 'new_v_final_norm']
TWIN_LEAF_KINDS = {'loss': 'loss', 'grad_x': 'grad_x', 'grad_att_norm': 'grad_w', 'grad_att_w_in': 'grad_w', 'grad_att_sink': 'grad_w', 'grad_att_qnorm': 'grad_w', 'grad_att_knorm': 'grad_w', 'grad_att_w_out': 'grad_w', 'grad_sgu_norm': 'grad_w', 'grad_sgu_w_in': 'grad_w', 'grad_sgu_ln_g': 'grad_w', 'grad_sgu_ln_b': 'grad_w', 'grad_sgu_w_s': 'grad_w', 'grad_sgu_b_s': 'grad_w', 'grad_sgu_w_out': 'grad_w', 'grad_mlp_norm': 'grad_w', 'grad_mlp_w1': 'grad_w', 'grad_mlp_w2': 'grad_w', 'grad_final_norm': 'grad_w', 'delta_att_norm': 'delta_w', 'delta_att_w_in': 'delta_w', 'delta_att_sink': 'delta_w', 'delta_att_qnorm': 'delta_w', 'delta_att_knorm': 'delta_w', 'delta_att_w_out': 'delta_w', 'delta_sgu_norm': 'delta_w', 'delta_sgu_w_in': 'delta_w', 'delta_sgu_ln_g': 'delta_w', 'delta_sgu_ln_b': 'delta_w', 'delta_sgu_w_s': 'delta_w', 'delta_sgu_b_s': 'delta_w', 'delta_sgu_w_out': 'delta_w', 'delta_mlp_norm': 'delta_w', 'delta_mlp_w1': 'delta_w', 'delta_mlp_w2': 'delta_w', 'delta_final_norm': 'delta_w', 'new_m_att_norm': 'new_m', 'new_m_att_w_in': 'new_m', 'new_m_att_sink': 'new_m', 'new_m_att_qnorm': 'new_m', 'new_m_att_knorm': 'new_m', 'new_m_att_w_out': 'new_m', 'new_m_sgu_norm': 'new_m', 'new_m_sgu_w_in': 'new_m', 'new_m_sgu_ln_g': 'new_m', 'new_m_sgu_ln_b': 'new_m', 'new_m_sgu_w_s': 'new_m', 'new_m_sgu_b_s': 'new_m', 'new_m_sgu_w_out': 'new_m', 'new_m_mlp_norm': 'new_m', 'new_m_mlp_w1': 'new_m', 'new_m_mlp_w2': 'new_m', 'new_m_final_norm': 'new_m', 'new_v_att_norm': 'new_v', 'new_v_att_w_in': 'new_v', 'new_v_att_sink': 'new_v', 'new_v_att_qnorm': 'new_v', 'new_v_att_knorm': 'new_v', 'new_v_att_w_out': 'new_v', 'new_v_sgu_norm': 'new_v', 'new_v_sgu_w_in': 'new_v', 'new_v_sgu_ln_g': 'new_v', 'new_v_sgu_ln_b': 'new_v', 'new_v_sgu_w_s': 'new_v', 'new_v_sgu_b_s': 'new_v', 'new_v_sgu_w_out': 'new_v', 'new_v_mlp_norm': 'new_v', 'new_v_mlp_w1': 'new_v', 'new_v_mlp_w2': 'new_v', 'new_v_final_norm': 'new_v'}


def _forward(args):
    return _fwd_reference(*[args[k] for k in FWD_PARAMS])


def _output_shape():
    def fwd():
        inp = _fwd_setup_inputs(0)
        return _fwd_reference(*[inp[k] for k in FWD_PARAMS])
    out = _jax.eval_shape(fwd)
    return out.shape, out.dtype

N_MICROBATCH = 1
ADAM_LR = 0.001
ADAM_B1 = 0.9
ADAM_B2 = 0.999
ADAM_EPS = 1e-08
ADAM_WD = 0.01
ADAM_STEP = 10
PER_EXAMPLE_BATCH_AXIS = {'x': 0, 'loss_target': 0}
SHARED_INPUTS = []
_WEIGHT_DTYPES = {'att_norm': _jnp.float32, 'att_w_in': _jnp.float32, 'att_sink': _jnp.float32, 'att_qnorm': _jnp.float32, 'att_knorm': _jnp.float32, 'att_w_out': _jnp.float32, 'sgu_norm': _jnp.float32, 'sgu_w_in': _jnp.float32, 'sgu_ln_g': _jnp.float32, 'sgu_ln_b': _jnp.float32, 'sgu_w_s': _jnp.float32, 'sgu_b_s': _jnp.float32, 'sgu_w_out': _jnp.float32, 'mlp_norm': _jnp.float32, 'mlp_w1': _jnp.float32, 'mlp_w2': _jnp.float32, 'final_norm': _jnp.float32}
MOMENT_SCALE = {'att_norm': 6.242349e-02, 'att_w_in': 5.267342e-02, 'att_sink': 2.375090e-03, 'att_qnorm': 7.816701e-02, 'att_knorm': 7.285481e-02, 'att_w_out': 7.630066e-02, 'sgu_norm': 1.807983e-01, 'sgu_w_in': 1.161994e-01, 'sgu_ln_g': 8.510725e-02, 'sgu_ln_b': 8.252403e-02, 'sgu_w_s': 8.484940e-02, 'sgu_b_s': 8.402921e-02, 'sgu_w_out': 1.537174e-01, 'mlp_norm': 2.045353e-01, 'mlp_w1': 1.017697e-01, 'mlp_w2': 2.280654e-01, 'final_norm': 6.648483e+01}


def _to_microbatches(a, axis):
    t = _jnp.moveaxis(a, axis, 0)
    t = t.reshape((N_MICROBATCH, t.shape[0] // N_MICROBATCH) + t.shape[1:])
    return _jnp.moveaxis(t, 1, axis + 1)


def setup_inputs(seed: int = 0) -> dict:
    inp = _fwd_setup_inputs(seed)
    key = _jax.random.fold_in(_jax.random.key(seed), 7919)
    shape, _ = _output_shape()
    out = dict(inp)
    out["loss_target"] = _jax.random.normal(_jax.random.fold_in(key, 0), shape, _jnp.float32)
    for i, name in enumerate(TWIN_WEIGHTS):
        w = inp[name].astype(_jnp.float32)
        if MOMENT_SCALE is None:
            s = _jnp.sqrt(_jnp.mean(_jnp.square(w)) + 1e-30)
        else:
            s = MOMENT_SCALE[name]
        km, kv = _jax.random.split(_jax.random.fold_in(key, i + 1))
        out[name] = w
        out["m_" + name] = s * _jax.random.normal(km, w.shape, _jnp.float32)
        out["v_" + name] = (s * s) * _jax.random.uniform(kv, w.shape, _jnp.float32, 0.5, 1.5)
    if N_MICROBATCH > 1:
        for name, axis in PER_EXAMPLE_BATCH_AXIS.items():
            out[name] = _to_microbatches(out[name], axis)
    return {'x': out['x'], 'att_norm': out['att_norm'], 'att_w_in': out['att_w_in'], 'att_sink': out['att_sink'], 'att_qnorm': out['att_qnorm'], 'att_knorm': out['att_knorm'], 'att_w_out': out['att_w_out'], 'sgu_norm': out['sgu_norm'], 'sgu_w_in': out['sgu_w_in'], 'sgu_ln_g': out['sgu_ln_g'], 'sgu_ln_b': out['sgu_ln_b'], 'sgu_w_s': out['sgu_w_s'], 'sgu_b_s': out['sgu_b_s'], 'sgu_w_out': out['sgu_w_out'], 'mlp_norm': out['mlp_norm'], 'mlp_w1': out['mlp_w1'], 'mlp_w2': out['mlp_w2'], 'final_norm': out['final_norm'], 'loss_target': out['loss_target'], 'm_att_norm': out['m_att_norm'], 'm_att_w_in': out['m_att_w_in'], 'm_att_sink': out['m_att_sink'], 'm_att_qnorm': out['m_att_qnorm'], 'm_att_knorm': out['m_att_knorm'], 'm_att_w_out': out['m_att_w_out'], 'm_sgu_norm': out['m_sgu_norm'], 'm_sgu_w_in': out['m_sgu_w_in'], 'm_sgu_ln_g': out['m_sgu_ln_g'], 'm_sgu_ln_b': out['m_sgu_ln_b'], 'm_sgu_w_s': out['m_sgu_w_s'], 'm_sgu_b_s': out['m_sgu_b_s'], 'm_sgu_w_out': out['m_sgu_w_out'], 'm_mlp_norm': out['m_mlp_norm'], 'm_mlp_w1': out['m_mlp_w1'], 'm_mlp_w2': out['m_mlp_w2'], 'm_final_norm': out['m_final_norm'], 'v_att_norm': out['v_att_norm'], 'v_att_w_in': out['v_att_w_in'], 'v_att_sink': out['v_att_sink'], 'v_att_qnorm': out['v_att_qnorm'], 'v_att_knorm': out['v_att_knorm'], 'v_att_w_out': out['v_att_w_out'], 'v_sgu_norm': out['v_sgu_norm'], 'v_sgu_w_in': out['v_sgu_w_in'], 'v_sgu_ln_g': out['v_sgu_ln_g'], 'v_sgu_ln_b': out['v_sgu_ln_b'], 'v_sgu_w_s': out['v_sgu_w_s'], 'v_sgu_b_s': out['v_sgu_b_s'], 'v_sgu_w_out': out['v_sgu_w_out'], 'v_mlp_norm': out['v_mlp_norm'], 'v_mlp_w1': out['v_mlp_w1'], 'v_mlp_w2': out['v_mlp_w2'], 'v_final_norm': out['v_final_norm']}


def _loss(weights, diff, rest, loss_target):
    with _jax.named_scope("forward"):
        args = {**rest, TWIN_DIFF_INPUT: diff, **{k: w.astype(_WEIGHT_DTYPES[k]) for k, w in weights.items()}}
        y = _forward(args)
    with _jax.named_scope("loss_head"):
        err = _jnp.square(y.astype(_jnp.float32) - loss_target)
        return 0.5 * _jnp.sum(_jnp.mean(err, axis=-1)) if err.ndim else 0.5 * err


def _adamw(w, g, m, v):
    m = ADAM_B1 * m + (1.0 - ADAM_B1) * g
    v = ADAM_B2 * v + (1.0 - ADAM_B2) * _jnp.square(g)
    m_hat = m / (1.0 - ADAM_B1 ** ADAM_STEP)
    v_hat = v / (1.0 - ADAM_B2 ** ADAM_STEP)
    delta = -ADAM_LR * (m_hat / (_jnp.sqrt(v_hat) + ADAM_EPS) + ADAM_WD * w)
    return delta, m, v


def reference(x, att_norm, att_w_in, att_sink, att_qnorm, att_knorm, att_w_out, sgu_norm, sgu_w_in, sgu_ln_g, sgu_ln_b, sgu_w_s, sgu_b_s, sgu_w_out, mlp_norm, mlp_w1, mlp_w2, final_norm, loss_target, m_att_norm, m_att_w_in, m_att_sink, m_att_qnorm, m_att_knorm, m_att_w_out, m_sgu_norm, m_sgu_w_in, m_sgu_ln_g, m_sgu_ln_b, m_sgu_w_s, m_sgu_b_s, m_sgu_w_out, m_mlp_norm, m_mlp_w1, m_mlp_w2, m_final_norm, v_att_norm, v_att_w_in, v_att_sink, v_att_qnorm, v_att_knorm, v_att_w_out, v_sgu_norm, v_sgu_w_in, v_sgu_ln_g, v_sgu_ln_b, v_sgu_w_s, v_sgu_b_s, v_sgu_w_out, v_mlp_norm, v_mlp_w1, v_mlp_w2, v_final_norm):
    given = dict(x=x, att_norm=att_norm, att_w_in=att_w_in, att_sink=att_sink, att_qnorm=att_qnorm, att_knorm=att_knorm, att_w_out=att_w_out, sgu_norm=sgu_norm, sgu_w_in=sgu_w_in, sgu_ln_g=sgu_ln_g, sgu_ln_b=sgu_ln_b, sgu_w_s=sgu_w_s, sgu_b_s=sgu_b_s, sgu_w_out=sgu_w_out, mlp_norm=mlp_norm, mlp_w1=mlp_w1, mlp_w2=mlp_w2, final_norm=final_norm, loss_target=loss_target, m_att_norm=m_att_norm, m_att_w_in=m_att_w_in, m_att_sink=m_att_sink, m_att_qnorm=m_att_qnorm, m_att_knorm=m_att_knorm, m_att_w_out=m_att_w_out, m_sgu_norm=m_sgu_norm, m_sgu_w_in=m_sgu_w_in, m_sgu_ln_g=m_sgu_ln_g, m_sgu_ln_b=m_sgu_ln_b, m_sgu_w_s=m_sgu_w_s, m_sgu_b_s=m_sgu_b_s, m_sgu_w_out=m_sgu_w_out, m_mlp_norm=m_mlp_norm, m_mlp_w1=m_mlp_w1, m_mlp_w2=m_mlp_w2, m_final_norm=m_final_norm, v_att_norm=v_att_norm, v_att_w_in=v_att_w_in, v_att_sink=v_att_sink, v_att_qnorm=v_att_qnorm, v_att_knorm=v_att_knorm, v_att_w_out=v_att_w_out, v_sgu_norm=v_sgu_norm, v_sgu_w_in=v_sgu_w_in, v_sgu_ln_g=v_sgu_ln_g, v_sgu_ln_b=v_sgu_ln_b, v_sgu_w_s=v_sgu_w_s, v_sgu_b_s=v_sgu_b_s, v_sgu_w_out=v_sgu_w_out, v_mlp_norm=v_mlp_norm, v_mlp_w1=v_mlp_w1, v_mlp_w2=v_mlp_w2, v_final_norm=v_final_norm)
    weights = {n: given[n] for n in TWIN_WEIGHTS}
    shared = {n: given[n] for n in SHARED_INPUTS}
    per_example = {n: given[n] for n in ['x']}
    grad_fn = _jax.value_and_grad(_loss, argnums=(0, 1))

    def one_microbatch(ex, loss_target):
        ex = dict(ex)
        diff = ex.pop(TWIN_DIFF_INPUT)
        return grad_fn(weights, diff, {**shared, **ex}, loss_target)

    if N_MICROBATCH == 1:
        loss, (grad_w, grad_x) = one_microbatch(per_example, given["loss_target"])
    else:
        def body(carry, xs):
            loss_sum, grad_sum = carry
            l_k, (gw_k, gx_k) = one_microbatch(xs[0], xs[1])
            with _jax.named_scope("update"):
                return (loss_sum + l_k, _jax.tree.map(_jnp.add, grad_sum, gw_k)), gx_k

        init = (_jnp.zeros((), _jnp.float32), _jax.tree.map(_jnp.zeros_like, weights))
        (loss, grad_w), grad_x = _jax.lax.scan(body, init, (per_example, given["loss_target"]))
    with _jax.named_scope("update"):
        delta_w, new_m, new_v = {}, {}, {}
        for n in TWIN_WEIGHTS:
            delta_w[n], new_m[n], new_v[n] = _adamw(weights[n], grad_w[n], given["m_" + n], given["v_" + n])
    return (loss, grad_x, *[grad_w[n] for n in TWIN_WEIGHTS], *[delta_w[n] for n in TWIN_WEIGHTS],
            *[new_m[n] for n in TWIN_WEIGHTS], *[new_v[n] for n in TWIN_WEIGHTS])
```

```python
import numpy as np
import jax
import jax.numpy as jnp
from jax import lax
from jax.experimental import pallas as pl
from jax.experimental.pallas import tpu as pltpu

F32 = jnp.float32
BF16 = jnp.bfloat16
MESH = pl.DeviceIdType.MESH

EPS = 1e-6
HEAD_DIM = 64
BLOCK = 128
GRID_W = 64
ROPE_THETA = 10000.0
N_CHIPS = 4
LANES = 128
V7X_VMEM_BYTES = 64 * 1024 * 1024
VMEM_LIMIT = V7X_VMEM_BYTES - 8 * 1024 * 1024

ADAM_LR = 0.001
ADAM_B1 = 0.9
ADAM_B2 = 0.999
ADAM_EPS = 1e-08
ADAM_WD = 0.01
ADAM_STEP = 10

NT_DIMS = (((1,), (1,)), ((), ()))
TN_DIMS = (((0,), (0,)), ((), ()))


def _params(*sem):
    return pltpu.CompilerParams(dimension_semantics=sem, vmem_limit_bytes=VMEM_LIMIT)


def _sds(shape, dtype):
    return jax.ShapeDtypeStruct(tuple(shape), dtype)


def _row_tile(rows, want):
    t = min(rows, want)
    assert rows % t == 0, (rows, want)
    return t


def norm_mm(x, g, w4, layer, out_dtype, name):
    s_len, d = x.shape
    ns = w4.shape[-1]
    tm = _row_tile(s_len, 512)

    def body(x_ref, g_ref, w_ref, h_ref, y_ref):
        @pl.when(pl.program_id(1) == 0)
        def _():
            xf = x_ref[...]
            r = lax.rsqrt(jnp.mean(xf * xf, axis=-1, keepdims=True) + EPS)
            h_ref[...] = ((xf * r) * g_ref[...]).astype(BF16)

        y_ref[...] = jnp.dot(h_ref[...], w_ref[...], preferred_element_type=F32).astype(y_ref.dtype)

    return pl.pallas_call(
        body, name=name, grid=(s_len // tm, N_CHIPS),
        in_specs=[pl.BlockSpec((tm, d), lambda i, j: (i, 0)),
                  pl.BlockSpec((1, d), lambda i, j: (0, 0)),
                  pl.BlockSpec((None, None, d, ns), lambda i, j: (j, layer, 0, 0))],
        out_specs=[pl.BlockSpec((tm, d), lambda i, j: (i, 0)),
                   pl.BlockSpec((tm, ns), lambda i, j: (i, j))],
        out_shape=[_sds((s_len, d), BF16), _sds((s_len, N_CHIPS * ns), out_dtype)],
        compiler_params=_params("arbitrary", "arbitrary"),
    )(x, g, w4)


def mm_res(a, w4, layer, res, name, relu2=False):
    s_len, k = a.shape
    kq, n = w4.shape[-2:]
    assert kq * N_CHIPS == k
    tm = _row_tile(s_len, 256 if k > 1024 else 512)

    def body(a_ref, w0, w1, w2, w3, r_ref, o_ref):
        acc = r_ref[...]
        for s, w_ref in enumerate((w0, w1, w2, w3)):
            av = a_ref[:, s * kq:(s + 1) * kq]
            if relu2:
                t = jnp.maximum(av.astype(F32), 0.0)
                av = (t * t).astype(BF16)
            acc = acc + jnp.dot(av, w_ref[...], preferred_element_type=F32)
        o_ref[...] = acc

    def wspec(s):
        return pl.BlockSpec((None, None, kq, n), lambda i: (s, layer, 0, 0))

    return pl.pallas_call(
        body, name=name, grid=(s_len // tm,),
        in_specs=[pl.BlockSpec((tm, k), lambda i: (i, 0)), wspec(0), wspec(1), wspec(2), wspec(3),
                  pl.BlockSpec((tm, n), lambda i: (i, 0))],
        out_specs=pl.BlockSpec((tm, n), lambda i: (i, 0)),
        out_shape=_sds((s_len, n), F32),
        compiler_params=_params("arbitrary"),
    )(a, w4, w4, w4, w4, res)


def mm_nt(dy, w4, layer, name):
    s_len, n = dy.shape
    mq = w4.shape[-2]
    tm = _row_tile(s_len, 512)

    def body(d_ref, w0, w1, w2, w3, o_ref):
        dv = d_ref[...]
        for s, w_ref in enumerate((w0, w1, w2, w3)):
            o_ref[:, s * mq:(s + 1) * mq] = lax.dot_general(
                dv, w_ref[...], NT_DIMS, preferred_element_type=F32).astype(BF16)

    def wspec(s):
        return pl.BlockSpec((None, None, mq, n), lambda i: (s, layer, 0, 0))

    return pl.pallas_call(
        body, name=name, grid=(s_len // tm,),
        in_specs=[pl.BlockSpec((tm, n), lambda i: (i, 0)), wspec(0), wspec(1), wspec(2), wspec(3)],
        out_specs=pl.BlockSpec((tm, N_CHIPS * mq), lambda i: (i, 0)),
        out_shape=_sds((s_len, N_CHIPS * mq), BF16),
        compiler_params=_params("arbitrary"),
    )(dy, w4, w4, w4, w4)


def mm_nt_relu2_bwd(dy, w4, layer, a, name):
    s_len, n = dy.shape
    mq = w4.shape[-2]
    tm = _row_tile(s_len, 512)

    def body(d_ref, w_ref, a_ref, o_ref):
        dz = lax.dot_general(d_ref[...], w_ref[...], NT_DIMS, preferred_element_type=F32)
        o_ref[...] = (dz * (2.0 * jnp.maximum(a_ref[...].astype(F32), 0.0))).astype(BF16)

    return pl.pallas_call(
        body, name=name, grid=(s_len // tm, N_CHIPS),
        in_specs=[pl.BlockSpec((tm, n), lambda i, j: (i, 0)),
                  pl.BlockSpec((None, None, mq, n), lambda i, j: (j, layer, 0, 0)),
                  pl.BlockSpec((tm, mq), lambda i, j: (i, j))],
        out_specs=pl.BlockSpec((tm, mq), lambda i, j: (i, j)),
        out_shape=_sds((s_len, N_CHIPS * mq), BF16),
        compiler_params=_params("arbitrary", "arbitrary"),
    )(dy, w4, a)


def dx_norm(dy, w4, layer, x, g, dres, name):
    s_len, d = x.shape
    ns = w4.shape[-1]
    tm = _row_tile(s_len, 512)
    last = N_CHIPS - 1

    def body(dy_ref, w_ref, x_ref, g_ref, dr_ref, dx_ref, dxb_ref, dg_ref, acc_ref):
        i, j = pl.program_id(0), pl.program_id(1)
        part = lax.dot_general(dy_ref[...], w_ref[...], NT_DIMS, preferred_element_type=F32)

        @pl.when(j == 0)
        def _():
            acc_ref[...] = part

        @pl.when(j > 0)
        def _():
            acc_ref[...] += part

        @pl.when(j == last)
        def _():
            dh = acc_ref[...]
            xf = x_ref[...]
            r = lax.rsqrt(jnp.mean(xf * xf, axis=-1, keepdims=True) + EPS)
            xhat = xf * r
            dg_part = jnp.sum(dh * xhat, axis=0, keepdims=True)

            @pl.when(i == 0)
            def _():
                dg_ref[...] = dg_part

            @pl.when(i > 0)
            def _():
                dg_ref[...] += dg_part

            dxh = dh * g_ref[...]
            dx = dr_ref[...] + r * (dxh - xhat * jnp.mean(dxh * xhat, axis=-1, keepdims=True))
            dx_ref[...] = dx
            dxb_ref[...] = dx.astype(BF16)

    return pl.pallas_call(
        body, name=name, grid=(s_len // tm, N_CHIPS),
        in_specs=[pl.BlockSpec((tm, ns), lambda i, j: (i, j)),
                  pl.BlockSpec((None, None, d, ns), lambda i, j: (j, layer, 0, 0)),
                  pl.BlockSpec((tm, d), lambda i, j: (i, 0)),
                  pl.BlockSpec((1, d), lambda i, j: (0, 0)),
                  pl.BlockSpec((tm, d), lambda i, j: (i, 0))],
        out_specs=[pl.BlockSpec((tm, d), lambda i, j: (i, 0)),
                   pl.BlockSpec((tm, d), lambda i, j: (i, 0)),
                   pl.BlockSpec((1, d), lambda i, j: (0, 0))],
        out_shape=[_sds((s_len, d), F32), _sds((s_len, d), BF16), _sds((1, d), F32)],
        scratch_shapes=[pltpu.VMEM((tm, d), F32)],
        compiler_params=_params("arbitrary", "arbitrary"),
    )(dy, w4, x, g, dres)


def dw_mm(a, b, name, col_sharded, relu2=False):
    s_len, k = a.shape
    n = b.shape[1]
    ts = _row_tile(s_len, 512)
    tk = min(k, 1024)
    tn = n // N_CHIPS if col_sharded else min(n, 1024)
    n_s = s_len // ts

    def body(a_ref, b_ref, o_ref, acc_ref):
        s = pl.program_id(2)
        av = a_ref[...]
        if relu2:
            t = jnp.maximum(av.astype(F32), 0.0)
            av = (t * t).astype(BF16)
        part = lax.dot_general(av, b_ref[...], TN_DIMS, preferred_element_type=F32)

        @pl.when(s == 0)
        def _():
            acc_ref[...] = part

        @pl.when(s > 0)
        def _():
            acc_ref[...] += part

        @pl.when(s == n_s - 1)
        def _():
            o_ref[...] = acc_ref[...].astype(BF16)

    if col_sharded:
        out_shape = _sds((N_CHIPS, k, tn), BF16)
        out_spec = pl.BlockSpec((None, tk, tn), lambda i, j, s: (j, i, 0))
    else:
        out_shape = _sds((N_CHIPS, k // N_CHIPS, n), BF16)
        rows_per = k // N_CHIPS
        assert tk % rows_per == 0 or rows_per % tk == 0
        if tk >= rows_per:
            out_shape = _sds((k, n), BF16)
            out_spec = pl.BlockSpec((tk, tn), lambda i, j, s: (i, j))
        else:
            per = rows_per // tk
            out_spec = pl.BlockSpec((None, tk, tn), lambda i, j, s: (i // per, i % per, j))

    out = pl.pallas_call(
        body, name=name, grid=(k // tk, n // tn, n_s),
        in_specs=[pl.BlockSpec((ts, tk), lambda i, j, s: (s, i)),
                  pl.BlockSpec((ts, tn), lambda i, j, s: (s, j))],
        out_specs=out_spec, out_shape=out_shape,
        scratch_shapes=[pltpu.VMEM((tk, tn), F32)],
        compiler_params=_params("arbitrary", "arbitrary", "arbitrary"),
    )(a, b)
    if not col_sharded:
        out = out.reshape(N_CHIPS, k // N_CHIPS, n)
    return out


def ew(fn, ins, out_dtypes, name, tile_rows=256):
    rows, cols = ins[0].shape
    for a in ins:
        assert a.shape == (rows, cols), (name, a.shape, rows, cols)
    tr = rows if (rows <= tile_rows or rows % tile_rows) else tile_rows
    n_in = len(ins)

    def body(*refs):
        outs = fn(*[r[...] for r in refs[:n_in]])
        for o_ref, val in zip(refs[n_in:], outs):
            o_ref[...] = val.astype(o_ref.dtype)

    spec = pl.BlockSpec((tr, cols), lambda i: (i, 0))
    return pl.pallas_call(
        body, name=name, grid=(rows // tr,),
        in_specs=[spec] * n_in, out_specs=[spec] * len(out_dtypes),
        out_shape=[_sds((rows, cols), dt) for dt in out_dtypes],
        compiler_params=_params("arbitrary"),
    )(*ins)


def adamw(w, g, m, v, name):
    shape = w.shape
    cols = shape[-1]
    two_d = lambda a: a.reshape(-1, cols)

    def fn(wv, gv, mv, vv):
        m_new = ADAM_B1 * mv + (1.0 - ADAM_B1) * gv
        v_new = ADAM_B2 * vv + (1.0 - ADAM_B2) * (gv * gv)
        m_hat = m_new / (1.0 - ADAM_B1 ** ADAM_STEP)
        v_hat = v_new / (1.0 - ADAM_B2 ** ADAM_STEP)
        delta = -ADAM_LR * (m_hat / (jnp.sqrt(v_hat) + ADAM_EPS) + ADAM_WD * wv)
        return delta, m_new, v_new

    d, mn, vn = ew(fn, [two_d(w), two_d(g), two_d(m), two_d(v)], [F32, F32, F32], name)
    return d.reshape(shape), mn.reshape(shape), vn.reshape(shape)


def rope_tables(s_len):
    def angles(pos, dim):
        freqs = ROPE_THETA ** (-jnp.arange(0, dim, 2, dtype=F32) / dim)
        ang = pos.astype(F32)[:, None] * freqs[None, :]
        return jnp.cos(ang), jnp.sin(ang)

    pos = jnp.arange(s_len)
    rows = s_len // GRID_W
    row_idx = jnp.repeat(jnp.arange(rows), GRID_W)
    col_idx = jnp.tile(jnp.arange(GRID_W), rows)
    c1, s1 = angles(pos, HEAD_DIM)
    cr, sr = angles(row_idx, HEAD_DIM // 2)
    cc, sc = angles(col_idx, HEAD_DIM // 2)
    cos1 = jnp.tile(jnp.concatenate([c1, c1], -1), (1, 2))
    sin1 = jnp.tile(jnp.concatenate([-s1, s1], -1), (1, 2))
    cos2 = jnp.tile(jnp.concatenate([cr, cr, cc, cc], -1), (1, 2))
    sin2 = jnp.tile(jnp.concatenate([-sr, sr, -sc, sc], -1), (1, 2))
    return cos1, sin1, cos2, sin2


def _lane_iota(rows):
    return lax.broadcasted_iota(jnp.int32, (rows, LANES), 1)


def _swap(x, dist, lane):
    return jnp.where((lane & dist) != 0, pltpu.roll(x, dist, 1), pltpu.roll(x, LANES - dist, 1))


def _head_sum(t, lane):
    for dist in (32, 16, 8, 4, 2, 1):
        t = t + _swap(t, dist, lane)
    return t


Q_SCALE = HEAD_DIM ** -0.5
CHUNK_KIND = ["qa"] * 4 + ["ka", "va"] + ["qb"] * 4 + ["kb", "vb"]


def prep_fwd(proj, tabs, qn_g, kn_g, name):
    s_len, width = proj.shape
    ts = _row_tile(s_len, 512)
    cos1, sin1, cos2, sin2 = tabs

    def body(p_ref, c1_ref, s1_ref, c2_ref, s2_ref, qg_ref, kg_ref, o_ref):
        lane = _lane_iota(ts)
        c1, s1, c2, s2 = c1_ref[...], s1_ref[...], c2_ref[...], s2_ref[...]
        for cb, kind in enumerate(CHUNK_KIND):
            x = p_ref[:, cb * LANES:(cb + 1) * LANES]
            if kind in ("qa", "ka"):
                y = x * c1 + _swap(x, 32, lane) * s1
            elif kind in ("qb", "kb"):
                gain = qg_ref[...] if kind == "qb" else kg_ref[...]
                ms = _head_sum(x * x, lane) * (1.0 / HEAD_DIM)
                xn = (x * lax.rsqrt(ms + EPS)) * gain
                y = xn * c2 + _swap(xn, 16, lane) * s2
            else:
                y = x
            if kind in ("qa", "qb"):
                y = y * Q_SCALE
            o_ref[:, cb * LANES:(cb + 1) * LANES] = y.astype(BF16)

    tab = pl.BlockSpec((ts, LANES), lambda i: (i, 0))
    vec = pl.BlockSpec((1, LANES), lambda i: (0, 0))
    return pl.pallas_call(
        body, name=name, grid=(s_len // ts,),
        in_specs=[pl.BlockSpec((ts, width), lambda i: (i, 0)), tab, tab, tab, tab, vec, vec],
        out_specs=pl.BlockSpec((ts, width), lambda i: (i, 0)),
        out_shape=_sds((s_len, width), BF16),
        compiler_params=_params("arbitrary"),
    )(proj, cos1, sin1, cos2, sin2, qn_g, kn_g)


def prep_bwd(proj, dqa, dka, dva, dqb, dkb, dvb, tabs, qn_g, kn_g, name):
    s_len, width = proj.shape
    ts = _row_tile(s_len, 256)
    cos1, sin1, cos2, sin2 = tabs

    def body(p_ref, dqa_ref, dka_ref, dva_ref, dqb_ref, dkb_ref, dvb_ref,
             c1_ref, s1_ref, c2_ref, s2_ref, qg_ref, kg_ref, o_ref, dqg_ref, dkg_ref):
        i = pl.program_id(0)
        lane = _lane_iota(ts)
        c1, s1, c2, s2 = c1_ref[...], s1_ref[...], c2_ref[...], s2_ref[...]

        def group_sum(ref):
            g0 = ref[:, 0:LANES] + ref[:, LANES:2 * LANES]
            g1 = ref[:, 2 * LANES:3 * LANES] + ref[:, 3 * LANES:4 * LANES]
            g0 = g0 + pltpu.roll(g0, HEAD_DIM, 1)
            g1 = g1 + pltpu.roll(g1, HEAD_DIM, 1)
            return jnp.where(lane < HEAD_DIM, g0, g1)

        def rope_t(dy, cos, sin, dist):
            return dy * cos + _swap(dy * sin, dist, lane)

        def norm_bwd(dy, x, gain):
            r = lax.rsqrt(_head_sum(x * x, lane) * (1.0 / HEAD_DIM) + EPS)
            xhat = x * r
            dgain = jnp.sum(dy * xhat, axis=0, keepdims=True)
            dxh = dy * gain
            dx = r * (dxh - xhat * (_head_sum(dxh * xhat, lane) * (1.0 / HEAD_DIM)))
            return dx, dgain

        dqg = jnp.zeros((1, LANES), F32)
        dkg = jnp.zeros((1, LANES), F32)
        for cb, kind in enumerate(CHUNK_KIND):
            cols = slice(cb * LANES, (cb + 1) * LANES)
            if kind == "qa":
                dx = rope_t(dqa_ref[:, cols] * Q_SCALE, c1, s1, 32)
            elif kind == "ka":
                dx = rope_t(group_sum(dka_ref), c1, s1, 32)
            elif kind == "va":
                dx = group_sum(dva_ref)
            elif kind == "qb":
                qcols = slice((cb - 6) * LANES, (cb - 5) * LANES)
                dy = rope_t(dqb_ref[:, qcols] * Q_SCALE, c2, s2, 16)
                dx, dgain = norm_bwd(dy, p_ref[:, cols], qg_ref[...])
                dqg = dqg + dgain
            elif kind == "kb":
                dy = rope_t(group_sum(dkb_ref), c2, s2, 16)
                dx, dgain = norm_bwd(dy, p_ref[:, cols], kg_ref[...])
                dkg = dkg + dgain
            else:
                dx = group_sum(dvb_ref)
            o_ref[:, cols] = dx.astype(BF16)

        @pl.when(i == 0)
        def _():
            dqg_ref[...] = dqg
            dkg_ref[...] = dkg

        @pl.when(i > 0)
        def _():
            dqg_ref[...] += dqg
            dkg_ref[...] += dkg

    tab = pl.BlockSpec((ts, LANES), lambda i: (i, 0))
    vec = pl.BlockSpec((1, LANES), lambda i: (0, 0))
    grad = pl.BlockSpec((ts, 4 * LANES), lambda i: (i, 0))
    return pl.pallas_call(
        body, name=name, grid=(s_len // ts,),
        in_specs=[pl.BlockSpec((ts, width), lambda i: (i, 0))] + [grad] * 6 + [tab] * 4 + [vec, vec],
        out_specs=[pl.BlockSpec((ts, width), lambda i: (i, 0)), vec, vec],
        out_shape=[_sds((s_len, width), BF16), _sds((1, LANES), F32), _sds((1, LANES), F32)],
        compiler_params=_params("arbitrary"),
    )(proj, dqa, dka, dva, dqb, dkb, dvb, cos1, sin1, cos2, sin2, qn_g, kn_g)


NEG = -1e30
Q_HEADS = 8
GROUP = 4


def _band_mask(q_blk, k_blk, s_len, tq, tk):
    qpos = q_blk * tq + lax.broadcasted_iota(jnp.int32, (tq, tk), 0)
    kpos = k_blk * tk + lax.broadcasted_iota(jnp.int32, (tq, tk), 1)
    return (jnp.abs(kpos - qpos) <= BLOCK) & (kpos >= 0) & (kpos < s_len)


def flash_fwd(qkvh, q0, k0, v0, sink, window, name):
    s_len = qkvh.shape[1]
    if window:
        tq = tk = BLOCK
        n_kv = 3
    else:
        tq = tk = _row_tile(s_len, 512)
        n_kv = s_len // tk
    n_blk = s_len // tk

    def body(*refs):
        if window:
            sink_ref, q_ref, k_ref, v_ref, o_ref, lse_ref, m_sc, l_sc, acc_sc = refs
        else:
            q_ref, k_ref, v_ref, o_ref, lse_ref, m_sc, l_sc, acc_sc = refs
        h, i, t = pl.program_id(0), pl.program_id(1), pl.program_id(2)

        @pl.when(t == 0)
        def _():
            if window:
                m_sc[...] = jnp.full((tq, 1), sink_ref[h], F32)
                l_sc[...] = jnp.ones((tq, 1), F32)
            else:
                m_sc[...] = jnp.full((tq, 1), NEG, F32)
                l_sc[...] = jnp.zeros((tq, 1), F32)
            acc_sc[...] = jnp.zeros((tq, HEAD_DIM), F32)

        s = lax.dot_general(q_ref[...], k_ref[...], NT_DIMS, preferred_element_type=F32)
        if window:
            s = jnp.where(_band_mask(i, i - 1 + t, s_len, tq, tk), s, NEG)
        m_prev = m_sc[...]
        m_new = jnp.maximum(m_prev, jnp.max(s, axis=1, keepdims=True))
        alpha = jnp.exp(m_prev - m_new)
        p = jnp.exp(s - m_new)
        l_sc[...] = alpha * l_sc[...] + jnp.sum(p, axis=1, keepdims=True)
        acc_sc[...] = alpha * acc_sc[...] + jnp.dot(p.astype(BF16), v_ref[...], preferred_element_type=F32)
        m_sc[...] = m_new

        @pl.when(t == n_kv - 1)
        def _():
            l = l_sc[...]
            o_ref[...] = (acc_sc[...] / l).astype(BF16)
            lse_ref[...] = jnp.broadcast_to(m_sc[...] + jnp.log(l), (tq, LANES))

    if window:
        kv_blk = lambda i, t: jnp.clip(i - 1 + t, 0, n_blk - 1)
    else:
        kv_blk = lambda i, t: t
    in_specs = [pl.BlockSpec((None, tq, HEAD_DIM), lambda h, i, t: (q0 + h, i, 0)),
                pl.BlockSpec((None, tk, HEAD_DIM), lambda h, i, t: (k0 + h // GROUP, kv_blk(i, t), 0)),
                pl.BlockSpec((None, tk, HEAD_DIM), lambda h, i, t: (v0 + h // GROUP, kv_blk(i, t), 0))]
    args = [qkvh, qkvh, qkvh]
    if window:
        in_specs = [pl.BlockSpec(memory_space=pltpu.SMEM)] + in_specs
        args = [sink] + args
    return pl.pallas_call(
        body, name=name, grid=(Q_HEADS, s_len // tq, n_kv),
        in_specs=in_specs,
        out_specs=[pl.BlockSpec((None, tq, HEAD_DIM), lambda h, i, t: (h, i, 0)),
                   pl.BlockSpec((None, tq, LANES), lambda h, i, t: (h, i, 0))],
        out_shape=[_sds((Q_HEADS, s_len, HEAD_DIM), BF16), _sds((Q_HEADS, s_len, LANES), F32)],
        scratch_shapes=[pltpu.VMEM((tq, 1), F32), pltpu.VMEM((tq, 1), F32), pltpu.VMEM((tq, HEAD_DIM), F32)],
        compiler_params=_params("arbitrary", "arbitrary", "arbitrary"),
    )(*args)


def flash_bwd(qkvh, q0, k0, v0, oh, doh, o0, lse, sink, window, name):
    s_len = qkvh.shape[1]
    if window:
        tq = tk = BLOCK
        n_q = 3
    else:
        tq = tk = _row_tile(s_len, 512)
        n_q = s_len // tq
    n_blk = s_len // tq

    def body(*refs):
        if window:
            (sink_ref, q_ref, k_ref, v_ref, o_ref, do_ref, lse_ref,
             dq_ref, dk_ref, dv_ref, dsink_ref, dk_sc, dv_sc) = refs
        else:
            (q_ref, k_ref, v_ref, o_ref, do_ref, lse_ref, dq_ref, dk_ref, dv_ref, dk_sc, dv_sc) = refs
        h, j, t = pl.program_id(0), pl.program_id(1), pl.program_id(2)
        q_blk = j - 1 + t if window else t

        @pl.when((j == 0) & (t == 0))
        def _():
            dq_ref[...] = jnp.zeros((s_len, HEAD_DIM), F32)
            if window:
                dsink_ref[...] = jnp.zeros((8, LANES), F32)

        @pl.when(t == 0)
        def _():
            dk_sc[...] = jnp.zeros((tk, HEAD_DIM), F32)
            dv_sc[...] = jnp.zeros((tk, HEAD_DIM), F32)

        def step():
            q, k, v, do = q_ref[...], k_ref[...], v_ref[...], do_ref[...]
            s = lax.dot_general(q, k, NT_DIMS, preferred_element_type=F32)
            if window:
                qpos = q_blk * tq + lax.broadcasted_iota(jnp.int32, (tq, tk), 0)
                kpos = j * tk + lax.broadcasted_iota(jnp.int32, (tq, tk), 1)
                s = jnp.where(jnp.abs(kpos - qpos) <= BLOCK, s, NEG)
            lse_col = lse_ref[:, 0:1]
            p = jnp.exp(s - lse_col)
            delta = jnp.sum(do.astype(F32) * o_ref[...].astype(F32), axis=1, keepdims=True)
            dp = lax.dot_general(do, v, NT_DIMS, preferred_element_type=F32)
            ds = (p * (dp - delta)).astype(BF16)
            dv_sc[...] += lax.dot_general(p.astype(BF16), do, TN_DIMS, preferred_element_type=F32)
            dk_sc[...] += lax.dot_general(ds, q, TN_DIMS, preferred_element_type=F32)
            rows = pl.ds(pl.multiple_of(q_blk * tq, tq), tq)
            dq_ref[rows, :] += jnp.dot(ds, k, preferred_element_type=F32)
            if window:
                @pl.when(t == 1)
                def _():
                    p_sink = jnp.exp(sink_ref[h] - lse_col)
                    dsink_ref[...] += jnp.broadcast_to(-jnp.sum(p_sink * delta, axis=0, keepdims=True), (8, LANES))

        if window:
            pl.when((q_blk >= 0) & (q_blk < n_blk))(step)
        else:
            step()

        @pl.when(t == n_q - 1)
        def _():
            dk_ref[...] = dk_sc[...]
            dv_ref[...] = dv_sc[...]

    if window:
        qb = lambda j, t: jnp.clip(j - 1 + t, 0, n_blk - 1)
    else:
        qb = lambda j, t: t
    in_specs = [pl.BlockSpec((None, tq, HEAD_DIM), lambda h, j, t: (q0 + h, qb(j, t), 0)),
                pl.BlockSpec((None, tk, HEAD_DIM), lambda h, j, t: (k0 + h // GROUP, j, 0)),
                pl.BlockSpec((None, tk, HEAD_DIM), lambda h, j, t: (v0 + h // GROUP, j, 0)),
                pl.BlockSpec((None, tq, HEAD_DIM), lambda h, j, t: (o0 + h, qb(j, t), 0)),
                pl.BlockSpec((None, tq, HEAD_DIM), lambda h, j, t: (o0 + h, qb(j, t), 0)),
                pl.BlockSpec((None, tq, LANES), lambda h, j, t: (h, qb(j, t), 0))]
    args = [qkvh, qkvh, qkvh, oh, doh, lse]
    head_out = _sds((Q_HEADS, s_len, HEAD_DIM), F32)
    out_specs = [pl.BlockSpec((None, s_len, HEAD_DIM), lambda h, j, t: (h, 0, 0)),
                 pl.BlockSpec((None, tk, HEAD_DIM), lambda h, j, t: (h, j, 0)),
                 pl.BlockSpec((None, tk, HEAD_DIM), lambda h, j, t: (h, j, 0))]
    out_shape = [head_out, head_out, head_out]
    if window:
        in_specs = [pl.BlockSpec(memory_space=pltpu.SMEM)] + in_specs
        args = [sink] + args
        out_specs.append(pl.BlockSpec((None, 8, LANES), lambda h, j, t: (h, 0, 0)))
        out_shape.append(_sds((Q_HEADS, 8, LANES), F32))
    return pl.pallas_call(
        body, name=name, grid=(Q_HEADS, s_len // tk, n_q),
        in_specs=in_specs, out_specs=out_specs, out_shape=out_shape,
        scratch_shapes=[pltpu.VMEM((tk, HEAD_DIM), F32), pltpu.VMEM((tk, HEAD_DIM), F32)],
        compiler_params=_params("arbitrary", "arbitrary", "arbitrary"),
    )(*args)


SGU_GROUPS = 8
SGU_CHUNK = 128
GELU_C = float(np.sqrt(2.0 / np.pi))
GELU_A = 0.044715


def _gelu(x):
    return x * (0.5 * (1.0 + jnp.tanh(GELU_C * (x + GELU_A * (x * x * x)))))


def _gelu_grad(x):
    t = jnp.tanh(GELU_C * (x + GELU_A * (x * x * x)))
    return 0.5 * (1.0 + t) + 0.5 * x * (1.0 - t * t) * (GELU_C * (1.0 + 3.0 * GELU_A * (x * x)))


def _layernorm_stats(v):
    mu = jnp.mean(v, axis=-1, keepdims=True)
    var = jnp.mean(jnp.square(v - mu), axis=-1, keepdims=True)
    rstd = lax.rsqrt(var + EPS)
    return (v - mu) * rstd, rstd


def sgu_mid_fwd(zpre, ln_g, ln_b, ws, bsb, name):
    s_len, width = zpre.shape
    d = width // 2
    ts = _row_tile(s_len, 256)

    def body(z_ref, g_ref, b_ref, ws_ref, bs_ref, y_ref):
        z = _gelu(z_ref[...])
        u, v = z[:, :d], z[:, d:]
        vhat, _ = _layernorm_stats(v)
        vn = (vhat * g_ref[...] + b_ref[...]).astype(BF16)
        for n in range(ts // SGU_CHUNK):
            rows = slice(n * SGU_CHUNK, (n + 1) * SGU_CHUNK)
            for g in range(SGU_GROUPS):
                cols = slice(g * LANES, (g + 1) * LANES)
                mixed = jnp.dot(ws_ref[g], vn[rows, cols], preferred_element_type=F32) + bs_ref[g]
                y_ref[rows, cols] = (u[rows, cols] * mixed).astype(BF16)

    vec = pl.BlockSpec((1, d), lambda i: (0, 0))
    cube = pl.BlockSpec((SGU_GROUPS, SGU_CHUNK, SGU_CHUNK), lambda i: (0, 0, 0))
    return pl.pallas_call(
        body, name=name, grid=(s_len // ts,),
        in_specs=[pl.BlockSpec((ts, width), lambda i: (i, 0)), vec, vec, cube, cube],
        out_specs=pl.BlockSpec((ts, d), lambda i: (i, 0)),
        out_shape=_sds((s_len, d), BF16),
        compiler_params=_params("arbitrary"),
    )(zpre, ln_g, ln_b, ws, bsb)


def sgu_mid_bwd(zpre, dy, ln_g, ln_b, ws, wst, bsb, name):
    s_len, width = zpre.shape
    d = width // 2
    ts = _row_tile(s_len, 256)
    n_steps = s_len // ts

    def body(z_ref, dy_ref, g_ref, b_ref, ws_ref, wst_ref, bs_ref,
             dz_ref, dws_ref, dbs_ref, dg_ref, db_ref, du_sc, dvn_sc):
        i = pl.program_id(0)

        @pl.when(i == 0)
        def _():
            dws_ref[...] = jnp.zeros(dws_ref.shape, F32)
            dbs_ref[...] = jnp.zeros(dbs_ref.shape, F32)
            dg_ref[...] = jnp.zeros(dg_ref.shape, F32)
            db_ref[...] = jnp.zeros(db_ref.shape, F32)

        zp = z_ref[...]
        z = _gelu(zp)
        u, v = z[:, :d], z[:, d:]
        vhat, rstd = _layernorm_stats(v)
        gain = g_ref[...]
        vn = (vhat * gain + b_ref[...]).astype(BF16)
        dyf = dy_ref[...].astype(F32)
        for n in range(ts // SGU_CHUNK):
            rows = slice(n * SGU_CHUNK, (n + 1) * SGU_CHUNK)
            for g in range(SGU_GROUPS):
                cols = slice(g * LANES, (g + 1) * LANES)
                vt = vn[rows, cols]
                mixed = jnp.dot(ws_ref[g], vt, preferred_element_type=F32) + bs_ref[g]
                dyt = dyf[rows, cols]
                du_sc[rows, cols] = dyt * mixed
                dmixed = dyt * u[rows, cols]
                dmb = dmixed.astype(BF16)
                dvn_sc[rows, cols] = jnp.dot(wst_ref[g], dmb, preferred_element_type=F32)
                dws_ref[g] += lax.dot_general(dmb, vt, NT_DIMS, preferred_element_type=F32)
                dbs_ref[g] += dmixed
        dvn = dvn_sc[...]
        dg_ref[...] += jnp.sum(dvn * vhat, axis=0, keepdims=True)
        db_ref[...] += jnp.sum(dvn, axis=0, keepdims=True)
        dvh = dvn * gain
        dv = rstd * (dvh - jnp.mean(dvh, axis=-1, keepdims=True)
                     - vhat * jnp.mean(dvh * vhat, axis=-1, keepdims=True))
        gp = _gelu_grad(zp)
        dz_ref[:, :d] = (du_sc[...] * gp[:, :d]).astype(BF16)
        dz_ref[:, d:] = (dv * gp[:, d:]).astype(BF16)

        @pl.when(i == n_steps - 1)
        def _():
            for g in range(SGU_GROUPS):
                tot = jnp.sum(dbs_ref[g], axis=1, keepdims=True)
                dbs_ref[g] = jnp.broadcast_to(tot, (SGU_CHUNK, LANES))

    vec = pl.BlockSpec((1, d), lambda i: (0, 0))
    cube = pl.BlockSpec((SGU_GROUPS, SGU_CHUNK, SGU_CHUNK), lambda i: (0, 0, 0))
    cube_shape = _sds((SGU_GROUPS, SGU_CHUNK, SGU_CHUNK), F32)
    return pl.pallas_call(
        body, name=name, grid=(n_steps,),
        in_specs=[pl.BlockSpec((ts, width), lambda i: (i, 0)), pl.BlockSpec((ts, d), lambda i: (i, 0)),
                  vec, vec, cube, cube, cube],
        out_specs=[pl.BlockSpec((ts, width), lambda i: (i, 0)), cube, cube, vec, vec],
        out_shape=[_sds((s_len, width), BF16), cube_shape, cube_shape, _sds((1, d), F32), _sds((1, d), F32)],
        scratch_shapes=[pltpu.VMEM((ts, d), F32), pltpu.VMEM((ts, d), F32)],
        compiler_params=_params("arbitrary"),
    )(zpre, dy, ln_g, ln_b, ws, wst, bsb)


def loss_head(x, g, target, name):
    s_len, d = x.shape
    tm = _row_tile(s_len, 512)

    def body(x_ref, g_ref, t_ref, dx_ref, dxb_ref, dg_ref, loss_ref):
        i = pl.program_id(0)
        xf = x_ref[...]
        gain = g_ref[...]
        r = lax.rsqrt(jnp.mean(xf * xf, axis=-1, keepdims=True) + EPS)
        xhat = xf * r
        err = xhat * gain - t_ref[...]
        row = jnp.mean(err * err, axis=-1, keepdims=True)
        part = 0.5 * jnp.sum(row, axis=0, keepdims=True)
        dy = err * (1.0 / d)
        dg_part = jnp.sum(dy * xhat, axis=0, keepdims=True)

        @pl.when(i == 0)
        def _():
            dg_ref[...] = dg_part
            loss_ref[...] = jnp.broadcast_to(part, (8, LANES))

        @pl.when(i > 0)
        def _():
            dg_ref[...] += dg_part
            loss_ref[...] += jnp.broadcast_to(part, (8, LANES))

        dxh = dy * gain
        dx = r * (dxh - xhat * jnp.mean(dxh * xhat, axis=-1, keepdims=True))
        dx_ref[...] = dx
        dxb_ref[...] = dx.astype(BF16)

    row_spec = pl.BlockSpec((tm, d), lambda i: (i, 0))
    vec = pl.BlockSpec((1, d), lambda i: (0, 0))
    return pl.pallas_call(
        body, name=name, grid=(s_len // tm,),
        in_specs=[row_spec, vec, row_spec],
        out_specs=[row_spec, row_spec, vec, pl.BlockSpec((8, LANES), lambda i: (0, 0))],
        out_shape=[_sds((s_len, d), F32), _sds((s_len, d), BF16), _sds((1, d), F32), _sds((8, LANES), F32)],
        compiler_params=_params("arbitrary"),
    )(x, g, target)


FLIP_BITS = {"c": (0, 0, 1), "x": (1, 0, 0), "y": (0, 1, 0), "xy": (1, 1, 0),
             "xc": (1, 0, 1), "yc": (0, 1, 1), "xyc": (1, 1, 1)}
CHIP_FLIPS = ("x", "y", "xy")


def _flip(pos, name):
    return tuple(1 - p if bit else p for p, bit in zip(pos, FLIP_BITS[name]))


def _chip(pos):
    return 2 * pos[0] + pos[1]


def exchange(name, ins, out_shapes, remote, local):
    n_in, n_out = len(ins), len(out_shapes)

    def body(*refs):
        in_refs, out_refs = refs[:n_in], refs[n_in:n_in + n_out]
        send_sems, recv_sems, local_sems = refs[n_in + n_out:]
        me = (lax.axis_index("x"), lax.axis_index("y"), lax.axis_index("c"))

        def copy(k, sender, receiver):
            ii, src_fn, oi, dst_fn, flip = remote[k]
            return pltpu.make_async_remote_copy(
                src_ref=src_fn(in_refs[ii], sender, receiver), dst_ref=dst_fn(out_refs[oi], sender),
                send_sem=send_sems.at[k], recv_sem=recv_sems.at[k],
                device_id=receiver, device_id_type=MESH)

        stays = []
        for k, (ii, src_fn, oi, dst_fn) in enumerate(local):
            cp = pltpu.make_async_copy(src_fn(in_refs[ii], me), dst_fn(out_refs[oi], me), local_sems.at[k])
            cp.start()
            stays.append(cp)
        sends = []
        for k in range(len(remote)):
            cp = copy(k, me, _flip(me, remote[k][4]))
            cp.start()
            sends.append(cp)
        for k in range(len(remote)):
            copy(k, _flip(me, remote[k][4]), me).wait_recv()
        for cp in sends:
            cp.wait_send()
        for cp in stays:
            cp.wait()

    hbm = pl.BlockSpec(memory_space=pl.ANY)
    return pl.pallas_call(
        body, name=name,
        in_specs=[hbm] * n_in, out_specs=[hbm] * n_out, out_shape=list(out_shapes),
        scratch_shapes=[pltpu.SemaphoreType.DMA((max(len(remote), 1),)),
                        pltpu.SemaphoreType.DMA((max(len(remote), 1),)),
                        pltpu.SemaphoreType.DMA((max(len(local), 1),))],
        compiler_params=pltpu.CompilerParams(has_side_effects=True),
    )(*ins)


def gather_chips(shards, name):
    remote, local = [], []
    for t in range(len(shards)):
        whole = lambda ref, sender, receiver=None: ref
        slot = lambda ref, sender: ref.at[_chip(sender)]
        local.append((t, whole, t, slot))
        for flip in CHIP_FLIPS:
            remote.append((t, whole, t, slot, flip))
    outs = [_sds((N_CHIPS,) + a.shape, a.dtype) for a in shards]
    return exchange(name, shards, outs, remote, local)


def gather_all(buf, name):
    whole = lambda ref, sender, receiver=None: ref
    slot = lambda ref, sender: ref.at[4 * sender[0] + 2 * sender[1] + sender[2]]
    remote = [(0, whole, 0, slot, flip) for flip in FLIP_BITS]
    local = [(0, whole, 0, slot)]
    return exchange(name, [buf], [_sds((8,) + buf.shape, buf.dtype)], remote, local)[0]


def _half(ref, core, axis):
    rows = ref.shape[axis] // 2
    idx = (slice(None),) * axis + (pl.ds(core * rows, rows),)
    return ref.at[idx]


def reduce_to_owner_chips(grads, stacks, name):
    n_t = len(grads)
    remote, local, outs = [], [], []
    for t, g in enumerate(grads):
        half_shape = (N_CHIPS, g.shape[1] // 2, g.shape[2])
        outs += [_sds(half_shape, BF16), _sds(half_shape, BF16)]
        local.append((t, lambda ref, me: _half(ref, me[2], 1), 2 * t, lambda ref, me: ref))
        remote.append((t, lambda ref, sender, receiver: _half(ref, receiver[2], 1),
                       2 * t + 1, lambda ref, sender: ref, "c"))
    got = exchange(name + "_swap", grads, outs, remote, local)
    add2 = lambda a, b: ((a.astype(F32) + b.astype(F32)),)
    chip_sums = []
    for t in range(n_t):
        mine, theirs = got[2 * t], got[2 * t + 1]
        flat = lambda a: a.reshape(-1, a.shape[-1])
        chip_sums.append(ew(add2, [flat(mine), flat(theirs)], [BF16], f"{name}_add{t}")[0].reshape(mine.shape))
    remote, local, outs = [], [], []
    for t, p in enumerate(chip_sums):
        outs += [_sds(p.shape[1:], BF16), _sds((len(CHIP_FLIPS),) + p.shape[1:], BF16)]
        local.append((t, lambda ref, me: ref.at[_chip(me)], 2 * t, lambda ref, me: ref))
        for f, flip in enumerate(CHIP_FLIPS):
            remote.append((t, lambda ref, sender, receiver: ref.at[_chip(receiver)],
                           2 * t + 1, lambda ref, sender, f=f: ref.at[f], flip))
    got = exchange(name + "_scatter", chip_sums, outs, remote, local)
    add4 = lambda a, b, c, d: (((a.astype(F32) + b.astype(F32)) + c.astype(F32)) + d.astype(F32),)
    totals = []
    for t in range(n_t):
        mine, theirs = got[2 * t], got[2 * t + 1]
        totals.append(ew(add4, [mine, theirs[0], theirs[1], theirs[2]], [F32], f"{name}_sum{t}")[0])
    names = []
    for out_name, _ in stacks:
        if out_name not in names:
            names.append(out_name)
    outs = []
    for out_name in names:
        layers = 1 + max(l for n, l in stacks if n == out_name)
        t0 = [t for t, (n, _) in enumerate(stacks) if n == out_name][0]
        outs.append(_sds((layers, 2 * totals[t0].shape[0], totals[t0].shape[1]), F32))
    remote, local = [], []
    for t, (out_name, layer) in enumerate(stacks):
        oi = names.index(out_name)
        place = lambda ref, sender, layer=layer: _half(ref.at[layer], sender[2], 0)
        local.append((t, lambda ref, me: ref, oi, place))
        remote.append((t, lambda ref, sender, receiver: ref, oi, place, "c"))
    full = exchange(name + "_share", totals, outs, remote, local)
    return dict(zip(names, full))


def _tok(t):
    return t.transpose(1, 0, 2).reshape(t.shape[1], t.shape[0] * t.shape[2])


def _heads(t):
    return t.reshape(t.shape[0], t.shape[1] // HEAD_DIM, HEAD_DIM).transpose(1, 0, 2)


def _tile2(vec):
    return jnp.tile(vec.reshape(1, HEAD_DIM), (1, 2))


def local_step(x, target, gw, rep):
    s_len, d = x.shape
    tabs = rope_tables(s_len)
    depth = rep["mlp_norm"].shape[0]
    row = lambda a: a.reshape(1, -1)
    saved = []
    h = x
    for layer in range(depth):
        i = layer // 2
        tag = f"l{layer}"
        if layer % 2 == 0:
            hn, proj = norm_mm(h, row(rep["att_norm"][i]), gw["att_w_in"], i, F32, tag + "_att_proj")
            qkv = prep_fwd(proj, tabs, _tile2(rep["att_qnorm"][i]), _tile2(rep["att_knorm"][i]), tag + "_att_prep")
            qkvh = _heads(qkv)
            oa, lse_a = flash_fwd(qkvh, 0, 8, 10, rep["att_sink"][i], True, tag + "_win_fwd")
            ob, lse_b = flash_fwd(qkvh, 12, 20, 22, None, False, tag + "_grid_fwd")
            oh = jnp.concatenate([oa, ob], axis=0)
            ocat = _tok(oh)
            out = mm_res(ocat, gw["att_w_out"], i, h, tag + "_att_out")
            mix_saved = (h, hn, proj, qkvh, oh, ocat, lse_a, lse_b)
        else:
            hn, zpre = norm_mm(h, row(rep["sgu_norm"][i]), gw["sgu_w_in"], i, F32, tag + "_sgu_in")
            ws = rep["sgu_w_s"][i].astype(BF16)
            bsb = jnp.broadcast_to(rep["sgu_b_s"][i][:, :, None], (SGU_GROUPS, SGU_CHUNK, LANES))
            y = sgu_mid_fwd(zpre, row(rep["sgu_ln_g"][i]), row(rep["sgu_ln_b"][i]), ws, bsb, tag + "_sgu_mid")
            out = mm_res(y, gw["sgu_w_out"], i, h, tag + "_sgu_out")
            mix_saved = (h, hn, zpre, y, ws, bsb)
        hm, a = norm_mm(out, row(rep["mlp_norm"][layer]), gw["mlp_w1"], layer, BF16, tag + "_mlp_up")
        nxt = mm_res(a, gw["mlp_w2"], layer, out, tag + "_mlp_down", relu2=True)
        saved.append((mix_saved, (out, hm, a)))
        h = nxt
    dh, dhb, d_final, loss_tile = loss_head(h, row(rep["final_norm"]), target, "loss_head")
    big, tags = [], []
    small = {k: [None] * v.shape[0] for k, v in rep.items() if k != "final_norm"}
    small["final_norm"] = d_final.reshape(-1)
    for layer in reversed(range(depth)):
        i = layer // 2
        tag = f"l{layer}"
        mix_saved, (xin, hm, a) = saved[layer]
        da = mm_nt_relu2_bwd(dhb, gw["mlp_w2"], layer, a, tag + "_mlp_down_bwd")
        big.append(dw_mm(a, dhb, tag + "_mlp_dw2", col_sharded=False, relu2=True))
        tags.append(("mlp_w2", layer))
        big.append(dw_mm(hm, da, tag + "_mlp_dw1", col_sharded=True))
        tags.append(("mlp_w1", layer))
        dh, dhb, dg = dx_norm(da, gw["mlp_w1"], layer, xin, row(rep["mlp_norm"][layer]), dh, tag + "_mlp_up_bwd")
        small["mlp_norm"][layer] = dg.reshape(-1)
        if layer % 2 == 0:
            xin, hn, proj, qkvh, oh, ocat, lse_a, lse_b = mix_saved
            docat = mm_nt(dhb, gw["att_w_out"], i, tag + "_att_out_bwd")
            big.append(dw_mm(ocat, dhb, tag + "_att_dwout", col_sharded=False))
            tags.append(("att_w_out", i))
            doh = _heads(docat)
            dqa, dka, dva, dsink = flash_bwd(qkvh, 0, 8, 10, oh, doh, 0, lse_a, rep["att_sink"][i], True,
                                             tag + "_win_bwd")
            dqb, dkb, dvb = flash_bwd(qkvh, 12, 20, 22, oh, doh, 8, lse_b, None, False, tag + "_grid_bwd")
            qg, kg = _tile2(rep["att_qnorm"][i]), _tile2(rep["att_knorm"][i])
            dproj, dqg, dkg = prep_bwd(proj, _tok(dqa), _tok(dka), _tok(dva), _tok(dqb), _tok(dkb), _tok(dvb),
                                       tabs, qg, kg, tag + "_att_prep_bwd")
            big.append(dw_mm(hn, dproj, tag + "_att_dwin", col_sharded=True))
            tags.append(("att_w_in", i))
            dh, dhb, dg = dx_norm(dproj, gw["att_w_in"], i, xin, row(rep["att_norm"][i]), dh, tag + "_att_proj_bwd")
            small["att_norm"][i] = dg.reshape(-1)
            small["att_sink"][i] = dsink[:, 0, 0]
            small["att_qnorm"][i] = dqg[0, :HEAD_DIM] + dqg[0, HEAD_DIM:]
            small["att_knorm"][i] = dkg[0, :HEAD_DIM] + dkg[0, HEAD_DIM:]
        else:
            xin, hn, zpre, y, ws, bsb = mix_saved
            dy = mm_nt(dhb, gw["sgu_w_out"], i, tag + "_sgu_out_bwd")
            big.append(dw_mm(y, dhb, tag + "_sgu_dwout", col_sharded=False))
            tags.append(("sgu_w_out", i))
            wst = ws.transpose(0, 2, 1)
            dz, dws, dbs, dlg, dlb = sgu_mid_bwd(zpre, dy, row(rep["sgu_ln_g"][i]), row(rep["sgu_ln_b"][i]),
                                                 ws, wst, bsb, tag + "_sgu_mid_bwd")
            big.append(dw_mm(hn, dz, tag + "_sgu_dwin", col_sharded=True))
            tags.append(("sgu_w_in", i))
            dh, dhb, dg = dx_norm(dz, gw["sgu_w_in"], i, xin, row(rep["sgu_norm"][i]), dh, tag + "_sgu_in_bwd")
            small["sgu_norm"][i] = dg.reshape(-1)
            small["sgu_ln_g"][i] = dlg.reshape(-1)
            small["sgu_ln_b"][i] = dlb.reshape(-1)
            small["sgu_w_s"][i] = dws
            small["sgu_b_s"][i] = dbs[:, :, 0]
    small = {k: (v if k == "final_norm" else jnp.stack(v)) for k, v in small.items()}
    return loss_tile, dh, big, tags, small


BIG = ("att_w_in", "att_w_out", "sgu_w_in", "sgu_w_out", "mlp_w1", "mlp_w2")
SHARDED_VEC = ("sgu_norm", "sgu_ln_g", "sgu_ln_b")
REPLICATED = ("att_norm", "att_sink", "att_qnorm", "att_knorm", "sgu_w_s", "sgu_b_s", "mlp_norm", "final_norm")
WEIGHTS = ("att_norm", "att_w_in", "att_sink", "att_qnorm", "att_knorm", "att_w_out", "sgu_norm", "sgu_w_in",
           "sgu_ln_g", "sgu_ln_b", "sgu_w_s", "sgu_b_s", "sgu_w_out", "mlp_norm", "mlp_w1", "mlp_w2", "final_norm")
SMALL = tuple(n for n in WEIGHTS if n not in BIG)
PACK_ALIGN = 8 * LANES


def _pack_small(small, loss_tile):
    parts = [small[n].reshape(-1) for n in SMALL] + [loss_tile[0, :1]]
    flat = jnp.concatenate(parts)
    pad = -flat.shape[0] % PACK_ALIGN
    return jnp.pad(flat, (0, pad)).reshape(-1, LANES)


def _unpack_small(flat2d, shapes):
    flat = flat2d.reshape(-1)
    out, off = {}, 0
    for n in SMALL:
        size = int(np.prod(shapes[n]))
        out[n] = flat[off:off + size].reshape(shapes[n])
        off += size
    return out, flat[off]


def kernel(x, att_norm, att_w_in, att_sink, att_qnorm, att_knorm, att_w_out, sgu_norm, sgu_w_in, sgu_ln_g, sgu_ln_b, sgu_w_s, sgu_b_s, sgu_w_out, mlp_norm, mlp_w1, mlp_w2, final_norm, loss_target, m_att_norm, m_att_w_in, m_att_sink, m_att_qnorm, m_att_knorm, m_att_w_out, m_sgu_norm, m_sgu_w_in, m_sgu_ln_g, m_sgu_ln_b, m_sgu_w_s, m_sgu_b_s, m_sgu_w_out, m_mlp_norm, m_mlp_w1, m_mlp_w2, m_final_norm, v_att_norm, v_att_w_in, v_att_sink, v_att_qnorm, v_att_knorm, v_att_w_out, v_sgu_norm, v_sgu_w_in, v_sgu_ln_g, v_sgu_ln_b, v_sgu_w_s, v_sgu_b_s, v_sgu_w_out, v_mlp_norm, v_mlp_w1, v_mlp_w2, v_final_norm):
    w = dict(att_norm=att_norm, att_w_in=att_w_in, att_sink=att_sink, att_qnorm=att_qnorm, att_knorm=att_knorm,
             att_w_out=att_w_out, sgu_norm=sgu_norm, sgu_w_in=sgu_w_in, sgu_ln_g=sgu_ln_g, sgu_ln_b=sgu_ln_b,
             sgu_w_s=sgu_w_s, sgu_b_s=sgu_b_s, sgu_w_out=sgu_w_out, mlp_norm=mlp_norm, mlp_w1=mlp_w1,
             mlp_w2=mlp_w2, final_norm=final_norm)
    m = dict(att_norm=m_att_norm, att_w_in=m_att_w_in, att_sink=m_att_sink, att_qnorm=m_att_qnorm,
             att_knorm=m_att_knorm, att_w_out=m_att_w_out, sgu_norm=m_sgu_norm, sgu_w_in=m_sgu_w_in,
             sgu_ln_g=m_sgu_ln_g, sgu_ln_b=m_sgu_ln_b, sgu_w_s=m_sgu_w_s, sgu_b_s=m_sgu_b_s,
             sgu_w_out=m_sgu_w_out, mlp_norm=m_mlp_norm, mlp_w1=m_mlp_w1, mlp_w2=m_mlp_w2,
             final_norm=m_final_norm)
    v = dict(att_norm=v_att_norm, att_w_in=v_att_w_in, att_sink=v_att_sink, att_qnorm=v_att_qnorm,
             att_knorm=v_att_knorm, att_w_out=v_att_w_out, sgu_norm=v_sgu_norm, sgu_w_in=v_sgu_w_in,
             sgu_ln_g=v_sgu_ln_g, sgu_ln_b=v_sgu_ln_b, sgu_w_s=v_sgu_w_s, sgu_b_s=v_sgu_b_s,
             sgu_w_out=v_sgu_w_out, mlp_norm=v_mlp_norm, mlp_w1=v_mlp_w1, mlp_w2=v_mlp_w2,
             final_norm=v_final_norm)
    chip = 2 * lax.axis_index("x") + lax.axis_index("y")

    vecs = jnp.stack([w[n] for n in SHARDED_VEC])
    gathered = gather_chips([w[n].astype(BF16) for n in BIG] + [vecs], "gather_weights")
    gw = dict(zip(BIG, gathered[:len(BIG)]))
    vec_full = gathered[-1].transpose(1, 2, 0, 3).reshape(vecs.shape[0], vecs.shape[1], -1)
    rep = {n: w[n] for n in REPLICATED}
    rep.update({n: vec_full[k] for k, n in enumerate(SHARDED_VEC)})

    loss_tile, grad_x, big, tags, small = local_step(x[0], loss_target[0], gw, rep)

    grads = reduce_to_owner_chips(big, tags, "grads")
    packed = _pack_small(small, loss_tile)
    everyone = gather_all(packed, "gather_small")
    add8 = lambda *a: (((a[0] + a[1]) + (a[2] + a[3])) + ((a[4] + a[5]) + (a[6] + a[7])),)
    total = ew(add8, [everyone[k] for k in range(8)], [F32], "sum_small")[0]
    small_g, loss = _unpack_small(total, {n: small[n].shape for n in SMALL})
    width = w["sgu_norm"].shape[1]
    for n in SHARDED_VEC:
        small_g[n] = lax.dynamic_slice_in_dim(small_g[n], chip * width, width, axis=1)
    grads.update(small_g)
    for n in BIG:
        grads[n] = grads[n].reshape(w[n].shape)

    delta, new_m, new_v = {}, {}, {}
    for n in WEIGHTS:
        shape = w[n].shape
        two_d = (lambda a: a.reshape(1, -1)) if len(shape) == 1 else (lambda a: a)
        dn, mn, vn = adamw(two_d(w[n]), two_d(grads[n]), two_d(m[n]), two_d(v[n]), "adamw_" + n)
        delta[n], new_m[n], new_v[n] = dn.reshape(shape), mn.reshape(shape), vn.reshape(shape)
    return (loss, grad_x[None], *[grads[n] for n in WEIGHTS], *[delta[n] for n in WEIGHTS],
            *[new_m[n] for n in WEIGHTS], *[new_v[n] for n in WEIGHTS])
```

```python
import numpy as np
import jax
import jax.numpy as jnp
from jax import lax
from jax.experimental import pallas as pl
from jax.experimental.pallas import tpu as pltpu

F32 = jnp.float32
BF16 = jnp.bfloat16
MESH = pl.DeviceIdType.MESH

EPS = 1e-6
HEAD_DIM = 64
BLOCK = 128
GRID_W = 64
ROPE_THETA = 10000.0
N_CHIPS = 4
LANES = 128
V7X_VMEM_BYTES = 64 * 1024 * 1024
VMEM_LIMIT = V7X_VMEM_BYTES - 8 * 1024 * 1024

ADAM_LR = 0.001
ADAM_B1 = 0.9
ADAM_B2 = 0.999
ADAM_EPS = 1e-08
ADAM_WD = 0.01
ADAM_STEP = 10

NT_DIMS = (((1,), (1,)), ((), ()))
TN_DIMS = (((0,), (0,)), ((), ()))


def _params(*sem):
    return pltpu.CompilerParams(dimension_semantics=sem, vmem_limit_bytes=VMEM_LIMIT)


def _sds(shape, dtype):
    return jax.ShapeDtypeStruct(tuple(shape), dtype)


def _row_tile(rows, want):
    t = min(rows, want)
    assert rows % t == 0, (rows, want)
    return t


def norm_mm(x, g, w4, layer, out_dtype, name):
    s_len, d = x.shape
    ns = w4.shape[-1]
    tm = _row_tile(s_len, 512)

    def body(x_ref, g_ref, w_ref, h_ref, y_ref):
        @pl.when(pl.program_id(1) == 0)
        def _():
            xf = x_ref[...]
            r = lax.rsqrt(jnp.mean(xf * xf, axis=-1, keepdims=True) + EPS)
            h_ref[...] = ((xf * r) * g_ref[...]).astype(BF16)

        y_ref[...] = jnp.dot(h_ref[...], w_ref[...], preferred_element_type=F32).astype(y_ref.dtype)

    return pl.pallas_call(
        body, name=name, grid=(s_len // tm, N_CHIPS),
        in_specs=[pl.BlockSpec((tm, d), lambda i, j: (i, 0)),
                  pl.BlockSpec((1, d), lambda i, j: (0, 0)),
                  pl.BlockSpec((None, None, d, ns), lambda i, j: (j, layer, 0, 0))],
        out_specs=[pl.BlockSpec((tm, d), lambda i, j: (i, 0)),
                   pl.BlockSpec((tm, ns), lambda i, j: (i, j))],
        out_shape=[_sds((s_len, d), BF16), _sds((s_len, N_CHIPS * ns), out_dtype)],
        compiler_params=_params("arbitrary", "arbitrary"),
    )(x, g, w4)


def mm_res(a, w4, layer, res, name, relu2=False):
    s_len, k = a.shape
    kq, n = w4.shape[-2:]
    assert kq * N_CHIPS == k
    tm = _row_tile(s_len, 256 if k > 1024 else 512)

    def body(a_ref, w0, w1, w2, w3, r_ref, o_ref):
        acc = r_ref[...]
        for s, w_ref in enumerate((w0, w1, w2, w3)):
            av = a_ref[:, s * kq:(s + 1) * kq]
            if relu2:
                t = jnp.maximum(av.astype(F32), 0.0)
                av = (t * t).astype(BF16)
            acc = acc + jnp.dot(av, w_ref[...], preferred_element_type=F32)
        o_ref[...] = acc

    def wspec(s):
        return pl.BlockSpec((None, None, kq, n), lambda i: (s, layer, 0, 0))

    return pl.pallas_call(
        body, name=name, grid=(s_len // tm,),
        in_specs=[pl.BlockSpec((tm, k), lambda i: (i, 0)), wspec(0), wspec(1), wspec(2), wspec(3),
                  pl.BlockSpec((tm, n), lambda i: (i, 0))],
        out_specs=pl.BlockSpec((tm, n), lambda i: (i, 0)),
        out_shape=_sds((s_len, n), F32),
        compiler_params=_params("arbitrary"),
    )(a, w4, w4, w4, w4, res)


def mm_nt(dy, w4, layer, name):
    s_len, n = dy.shape
    mq = w4.shape[-2]
    tm = _row_tile(s_len, 512)

    def body(d_ref, w0, w1, w2, w3, o_ref):
        dv = d_ref[...]
        for s, w_ref in enumerate((w0, w1, w2, w3)):
            o_ref[:, s * mq:(s + 1) * mq] = lax.dot_general(
                dv, w_ref[...], NT_DIMS, preferred_element_type=F32).astype(BF16)

    def wspec(s):
        return pl.BlockSpec((None, None, mq, n), lambda i: (s, layer, 0, 0))

    return pl.pallas_call(
        body, name=name, grid=(s_len // tm,),
        in_specs=[pl.BlockSpec((tm, n), lambda i: (i, 0)), wspec(0), wspec(1), wspec(2), wspec(3)],
        out_specs=pl.BlockSpec((tm, N_CHIPS * mq), lambda i: (i, 0)),
        out_shape=_sds((s_len, N_CHIPS * mq), BF16),
        compiler_params=_params("arbitrary"),
    )(dy, w4, w4, w4, w4)


def mm_nt_relu2_bwd(dy, w4, layer, a, name):
    s_len, n = dy.shape
    mq = w4.shape[-2]
    tm = _row_tile(s_len, 512)

    def body(d_ref, w_ref, a_ref, o_ref):
        dz = lax.dot_general(d_ref[...], w_ref[...], NT_DIMS, preferred_element_type=F32)
        o_ref[...] = (dz * (2.0 * jnp.maximum(a_ref[...].astype(F32), 0.0))).astype(BF16)

    return pl.pallas_call(
        body, name=name, grid=(s_len // tm, N_CHIPS),
        in_specs=[pl.BlockSpec((tm, n), lambda i, j: (i, 0)),
                  pl.BlockSpec((None, None, mq, n), lambda i, j: (j, layer, 0, 0)),
                  pl.BlockSpec((tm, mq), lambda i, j: (i, j))],
        out_specs=pl.BlockSpec((tm, mq), lambda i, j: (i, j)),
        out_shape=_sds((s_len, N_CHIPS * mq), BF16),
        compiler_params=_params("arbitrary", "arbitrary"),
    )(dy, w4, a)


def dx_norm(dy, w4, layer, x, g, dres, name):
    s_len, d = x.shape
    ns = w4.shape[-1]
    tm = _row_tile(s_len, 512)
    last = N_CHIPS - 1

    def body(dy_ref, w_ref, x_ref, g_ref, dr_ref, dx_ref, dxb_ref, dg_ref, acc_ref):
        i, j = pl.program_id(0), pl.program_id(1)
        part = lax.dot_general(dy_ref[...], w_ref[...], NT_DIMS, preferred_element_type=F32)

        @pl.when(j == 0)
        def _():
            acc_ref[...] = part

        @pl.when(j > 0)
        def _():
            acc_ref[...] += part

        @pl.when(j == last)
        def _():
            dh = acc_ref[...]
            xf = x_ref[...]
            r = lax.rsqrt(jnp.mean(xf * xf, axis=-1, keepdims=True) + EPS)
            xhat = xf * r
            dg_part = jnp.sum(dh * xhat, axis=0, keepdims=True)

            @pl.when(i == 0)
            def _():
                dg_ref[...] = dg_part

            @pl.when(i > 0)
            def _():
                dg_ref[...] += dg_part

            dxh = dh * g_ref[...]
            dx = dr_ref[...] + r * (dxh - xhat * jnp.mean(dxh * xhat, axis=-1, keepdims=True))
            dx_ref[...] = dx
            dxb_ref[...] = dx.astype(BF16)

    return pl.pallas_call(
        body, name=name, grid=(s_len // tm, N_CHIPS),
        in_specs=[pl.BlockSpec((tm, ns), lambda i, j: (i, j)),
                  pl.BlockSpec((None, None, d, ns), lambda i, j: (j, layer, 0, 0)),
                  pl.BlockSpec((tm, d), lambda i, j: (i, 0)),
                  pl.BlockSpec((1, d), lambda i, j: (0, 0)),
                  pl.BlockSpec((tm, d), lambda i, j: (i, 0))],
        out_specs=[pl.BlockSpec((tm, d), lambda i, j: (i, 0)),
                   pl.BlockSpec((tm, d), lambda i, j: (i, 0)),
                   pl.BlockSpec((1, d), lambda i, j: (0, 0))],
        out_shape=[_sds((s_len, d), F32), _sds((s_len, d), BF16), _sds((1, d), F32)],
        scratch_shapes=[pltpu.VMEM((tm, d), F32)],
        compiler_params=_params("arbitrary", "arbitrary"),
    )(dy, w4, x, g, dres)


def dw_mm(a, b, name, col_sharded, relu2=False):
    s_len, k = a.shape
    n = b.shape[1]
    ts = _row_tile(s_len, 512)
    tk = min(k, 1024)
    tn = n // N_CHIPS if col_sharded else min(n, 1024)
    n_s = s_len // ts

    def body(a_ref, b_ref, o_ref, acc_ref):
        s = pl.program_id(2)
        av = a_ref[...]
        if relu2:
            t = jnp.maximum(av.astype(F32), 0.0)
            av = (t * t).astype(BF16)
        part = lax.dot_general(av, b_ref[...], TN_DIMS, preferred_element_type=F32)

        @pl.when(s == 0)
        def _():
            acc_ref[...] = part

        @pl.when(s > 0)
        def _():
            acc_ref[...] += part

        @pl.when(s == n_s - 1)
        def _():
            o_ref[...] = acc_ref[...].astype(BF16)

    if col_sharded:
        out_shape = _sds((N_CHIPS, k, tn), BF16)
        out_spec = pl.BlockSpec((None, tk, tn), lambda i, j, s: (j, i, 0))
    else:
        out_shape = _sds((N_CHIPS, k // N_CHIPS, n), BF16)
        rows_per = k // N_CHIPS
        assert tk % rows_per == 0 or rows_per % tk == 0
        if tk >= rows_per:
            out_shape = _sds((k, n), BF16)
            out_spec = pl.BlockSpec((tk, tn), lambda i, j, s: (i, j))
        else:
            per = rows_per // tk
            out_spec = pl.BlockSpec((None, tk, tn), lambda i, j, s: (i // per, i % per, j))

    out = pl.pallas_call(
        body, name=name, grid=(k // tk, n // tn, n_s),
        in_specs=[pl.BlockSpec((ts, tk), lambda i, j, s: (s, i)),
                  pl.BlockSpec((ts, tn), lambda i, j, s: (s, j))],
        out_specs=out_spec, out_shape=out_shape,
        scratch_shapes=[pltpu.VMEM((tk, tn), F32)],
        compiler_params=_params("arbitrary", "arbitrary", "arbitrary"),
    )(a, b)
    if not col_sharded:
        out = out.reshape(N_CHIPS, k // N_CHIPS, n)
    return out


def ew(fn, ins, out_dtypes, name, tile_rows=256):
    rows, cols = ins[0].shape
    for a in ins:
        assert a.shape == (rows, cols), (name, a.shape, rows, cols)
    tr = rows if (rows <= tile_rows or rows % tile_rows) else tile_rows
    n_in = len(ins)

    def body(*refs):
        outs = fn(*[r[...] for r in refs[:n_in]])
        for o_ref, val in zip(refs[n_in:], outs):
            o_ref[...] = val.astype(o_ref.dtype)

    spec = pl.BlockSpec((tr, cols), lambda i: (i, 0))
    return pl.pallas_call(
        body, name=name, grid=(rows // tr,),
        in_specs=[spec] * n_in, out_specs=[spec] * len(out_dtypes),
        out_shape=[_sds((rows, cols), dt) for dt in out_dtypes],
        compiler_params=_params("arbitrary"),
    )(*ins)


def adamw(w, g, m, v, name):
    shape = w.shape
    cols = shape[-1]
    two_d = lambda a: a.reshape(-1, cols)

    def fn(wv, gv, mv, vv):
        m_new = ADAM_B1 * mv + (1.0 - ADAM_B1) * gv
        v_new = ADAM_B2 * vv + (1.0 - ADAM_B2) * (gv * gv)
        m_hat = m_new / (1.0 - ADAM_B1 ** ADAM_STEP)
        v_hat = v_new / (1.0 - ADAM_B2 ** ADAM_STEP)
        delta = -ADAM_LR * (m_hat / (jnp.sqrt(v_hat) + ADAM_EPS) + ADAM_WD * wv)
        return delta, m_new, v_new

    d, mn, vn = ew(fn, [two_d(w), two_d(g), two_d(m), two_d(v)], [F32, F32, F32], name)
    return d.reshape(shape), mn.reshape(shape), vn.reshape(shape)


def rope_tables(s_len):
    def angles(pos, dim):
        freqs = ROPE_THETA ** (-jnp.arange(0, dim, 2, dtype=F32) / dim)
        ang = pos.astype(F32)[:, None] * freqs[None, :]
        return jnp.cos(ang), jnp.sin(ang)

    pos = jnp.arange(s_len)
    rows = s_len // GRID_W
    row_idx = jnp.repeat(jnp.arange(rows), GRID_W)
    col_idx = jnp.tile(jnp.arange(GRID_W), rows)
    c1, s1 = angles(pos, HEAD_DIM)
    cr, sr = angles(row_idx, HEAD_DIM // 2)
    cc, sc = angles(col_idx, HEAD_DIM // 2)
    cos1 = jnp.tile(jnp.concatenate([c1, c1], -1), (1, 2))
    sin1 = jnp.tile(jnp.concatenate([-s1, s1], -1), (1, 2))
    cos2 = jnp.tile(jnp.concatenate([cr, cr, cc, cc], -1), (1, 2))
    sin2 = jnp.tile(jnp.concatenate([-sr, sr, -sc, sc], -1), (1, 2))
    return cos1, sin1, cos2, sin2


def _lane_iota(rows):
    return lax.broadcasted_iota(jnp.int32, (rows, LANES), 1)


def _swap(x, dist, lane):
    return jnp.where((lane & dist) != 0, pltpu.roll(x, dist, 1), pltpu.roll(x, LANES - dist, 1))


def _head_sum(t, lane):
    for dist in (32, 16, 8, 4, 2, 1):
        t = t + _swap(t, dist, lane)
    return t


Q_SCALE = HEAD_DIM ** -0.5
CHUNK_KIND = ["qa"] * 4 + ["ka", "va"] + ["qb"] * 4 + ["kb", "vb"]
QA_COL, KA_COL, QB_COL, KB_COL = 0, 512, 768, 1280


def prep_fwd(proj, tabs, qn_g, kn_g, name):
    s_len, width = proj.shape
    ts = _row_tile(s_len, 512)
    cos1, sin1, cos2, sin2 = tabs

    def body(p_ref, c1_ref, s1_ref, c2_ref, s2_ref, qg_ref, kg_ref, o_ref):
        lane = _lane_iota(ts)
        c1, s1, c2, s2 = c1_ref[...], s1_ref[...], c2_ref[...], s2_ref[...]
        for cb, kind in enumerate(CHUNK_KIND):
            x = p_ref[:, cb * LANES:(cb + 1) * LANES]
            if kind in ("qa", "ka"):
                y = x * c1 + _swap(x, 32, lane) * s1
            elif kind in ("qb", "kb"):
                gain = qg_ref[...] if kind == "qb" else kg_ref[...]
                ms = _head_sum(x * x, lane) * (1.0 / HEAD_DIM)
                xn = (x * lax.rsqrt(ms + EPS)) * gain
                y = xn * c2 + _swap(xn, 16, lane) * s2
            else:
                y = x
            if kind in ("qa", "qb"):
                y = y * Q_SCALE
            o_ref[:, cb * LANES:(cb + 1) * LANES] = y.astype(BF16)

    tab = pl.BlockSpec((ts, LANES), lambda i: (i, 0))
    vec = pl.BlockSpec((1, LANES), lambda i: (0, 0))
    return pl.pallas_call(
        body, name=name, grid=(s_len // ts,),
        in_specs=[pl.BlockSpec((ts, width), lambda i: (i, 0)), tab, tab, tab, tab, vec, vec],
        out_specs=pl.BlockSpec((ts, width), lambda i: (i, 0)),
        out_shape=_sds((s_len, width), BF16),
        compiler_params=_params("arbitrary"),
    )(proj, cos1, sin1, cos2, sin2, qn_g, kn_g)


def prep_bwd(proj, dqa, dka, dva, dqb, dkb, dvb, tabs, qn_g, kn_g, name):
    s_len, width = proj.shape
    ts = _row_tile(s_len, 256)
    cos1, sin1, cos2, sin2 = tabs

    def body(p_ref, dqa_ref, dka_ref, dva_ref, dqb_ref, dkb_ref, dvb_ref,
             c1_ref, s1_ref, c2_ref, s2_ref, qg_ref, kg_ref, o_ref, dqg_ref, dkg_ref):
        i = pl.program_id(0)
        lane = _lane_iota(ts)
        c1, s1, c2, s2 = c1_ref[...], s1_ref[...], c2_ref[...], s2_ref[...]

        def rope_t(dy, cos, sin, dist):
            return dy * cos + _swap(dy * sin, dist, lane)

        def norm_bwd(dy, x, gain):
            r = lax.rsqrt(_head_sum(x * x, lane) * (1.0 / HEAD_DIM) + EPS)
            xhat = x * r
            dgain = jnp.sum(dy * xhat, axis=0, keepdims=True)
            dxh = dy * gain
            dx = r * (dxh - xhat * (_head_sum(dxh * xhat, lane) * (1.0 / HEAD_DIM)))
            return dx, dgain

        dqg = jnp.zeros((1, LANES), F32)
        dkg = jnp.zeros((1, LANES), F32)
        for cb, kind in enumerate(CHUNK_KIND):
            cols = slice(cb * LANES, (cb + 1) * LANES)
            if kind == "qa":
                dx = rope_t(dqa_ref[:, cols] * Q_SCALE, c1, s1, 32)
            elif kind == "ka":
                dx = rope_t(dka_ref[...], c1, s1, 32)
            elif kind == "va":
                dx = dva_ref[...]
            elif kind == "qb":
                qcols = slice((cb - 6) * LANES, (cb - 5) * LANES)
                dy = rope_t(dqb_ref[:, qcols] * Q_SCALE, c2, s2, 16)
                dx, dgain = norm_bwd(dy, p_ref[:, cols], qg_ref[...])
                dqg = dqg + dgain
            elif kind == "kb":
                dy = rope_t(dkb_ref[...], c2, s2, 16)
                dx, dgain = norm_bwd(dy, p_ref[:, cols], kg_ref[...])
                dkg = dkg + dgain
            else:
                dx = dvb_ref[...]
            o_ref[:, cols] = dx.astype(BF16)

        @pl.when(i == 0)
        def _():
            dqg_ref[...] = dqg
            dkg_ref[...] = dkg

        @pl.when(i > 0)
        def _():
            dqg_ref[...] += dqg
            dkg_ref[...] += dkg

    tab = pl.BlockSpec((ts, LANES), lambda i: (i, 0))
    vec = pl.BlockSpec((1, LANES), lambda i: (0, 0))
    dq_spec = pl.BlockSpec((ts, 4 * LANES), lambda i: (i, 0))
    return pl.pallas_call(
        body, name=name, grid=(s_len // ts,),
        in_specs=([pl.BlockSpec((ts, width), lambda i: (i, 0)), dq_spec, tab, tab, dq_spec, tab, tab]
                  + [tab] * 4 + [vec, vec]),
        out_specs=[pl.BlockSpec((ts, width), lambda i: (i, 0)), vec, vec],
        out_shape=[_sds((s_len, width), BF16), _sds((1, LANES), F32), _sds((1, LANES), F32)],
        compiler_params=_params("arbitrary"),
    )(proj, dqa, dka, dva, dqb, dkb, dvb, cos1, sin1, cos2, sin2, qn_g, kn_g)


NEG = -1e30
GROUP = 4
KV_HEADS = 2
GROUP_W = GROUP * HEAD_DIM
LSE_ROWS = 8


def _pos_mask_t(k_start, q_start, s_len, tk, tq):
    kpos = k_start + lax.broadcasted_iota(jnp.int32, (tk, tq), 0)
    qpos = q_start + lax.broadcasted_iota(jnp.int32, (tk, tq), 1)
    return (jnp.abs(kpos - qpos) <= BLOCK) & (kpos >= 0) & (kpos < s_len)


def flash_fwd_t(qkv_t, kv_tok, q_rb, k_i, v_rb, sink, window, name):
    s_len = qkv_t.shape[1]
    tq = _row_tile(s_len, 512)
    if window:
        tk = 256
        per = tq // tk
        n_kv = per + 2
    else:
        tk = _row_tile(s_len, 512)
        n_kv = s_len // tk
    n_kb = s_len // tk

    def body(*refs):
        if window:
            sink_ref, q_ref, k_ref, v_ref, o_ref, lse_ref, m_sc, l_sc, acc_sc = refs
        else:
            q_ref, k_ref, v_ref, o_ref, lse_ref, m_sc, l_sc, acc_sc = refs
        h, i, t = pl.program_id(0), pl.program_id(1), pl.program_id(2)

        @pl.when(t == 0)
        def _():
            for g in range(GROUP):
                if window:
                    m_sc[g] = jnp.full((1, tq), sink_ref[h * GROUP + g], F32)
                    l_sc[g] = jnp.ones((1, tq), F32)
                else:
                    m_sc[g] = jnp.full((1, tq), NEG, F32)
                    l_sc[g] = jnp.zeros((1, tq), F32)
                acc_sc[g] = jnp.zeros((HEAD_DIM, tq), F32)

        k = k_ref[...]
        v_t = v_ref[...]
        if window:
            mask = _pos_mask_t((i * per - 1 + t) * tk, i * tq, s_len, tk, tq)
        for g in range(GROUP):
            q_t = q_ref[g * HEAD_DIM:(g + 1) * HEAD_DIM, :]
            s_t = jnp.dot(k, q_t, preferred_element_type=F32)
            if window:
                s_t = jnp.where(mask, s_t, NEG)
            m_prev = m_sc[g]
            m_new = jnp.maximum(m_prev, jnp.max(s_t, axis=0, keepdims=True))
            alpha = jnp.exp(m_prev - m_new)
            p_t = jnp.exp(s_t - m_new)
            l_sc[g] = alpha * l_sc[g] + jnp.sum(p_t, axis=0, keepdims=True)
            acc_sc[g] = alpha * acc_sc[g] + jnp.dot(v_t, p_t.astype(BF16), preferred_element_type=F32)
            m_sc[g] = m_new

        @pl.when(t == n_kv - 1)
        def _():
            for g in range(GROUP):
                l = l_sc[g]
                o_ref[g * HEAD_DIM:(g + 1) * HEAD_DIM, :] = (acc_sc[g] / l).astype(BF16)
                lse_ref[g * LSE_ROWS:(g + 1) * LSE_ROWS, :] = jnp.broadcast_to(m_sc[g] + jnp.log(l), (LSE_ROWS, tq))

    if window:
        kv_blk = lambda i, t: jnp.clip(i * per - 1 + t, 0, n_kb - 1)
    else:
        kv_blk = lambda i, t: t
    in_specs = [pl.BlockSpec((GROUP_W, tq), lambda h, i, t: (q_rb + h, i)),
                pl.BlockSpec((None, tk, HEAD_DIM), lambda h, i, t: (k_i + h, kv_blk(i, t), 0)),
                pl.BlockSpec((HEAD_DIM, tk), lambda h, i, t: (v_rb + h, kv_blk(i, t)))]
    args = [qkv_t, kv_tok, qkv_t]
    if window:
        in_specs = [pl.BlockSpec(memory_space=pltpu.SMEM)] + in_specs
        args = [sink] + args
    return pl.pallas_call(
        body, name=name, grid=(KV_HEADS, s_len // tq, n_kv),
        in_specs=in_specs,
        out_specs=[pl.BlockSpec((GROUP_W, tq), lambda h, i, t: (h, i)),
                   pl.BlockSpec((GROUP * LSE_ROWS, tq), lambda h, i, t: (h, i))],
        out_shape=[_sds((KV_HEADS * GROUP_W, s_len), BF16), _sds((KV_HEADS * GROUP * LSE_ROWS, s_len), F32)],
        scratch_shapes=[pltpu.VMEM((GROUP, 1, tq), F32), pltpu.VMEM((GROUP, 1, tq), F32),
                        pltpu.VMEM((GROUP, HEAD_DIM, tq), F32)],
        compiler_params=_params("arbitrary", "arbitrary", "arbitrary"),
    )(*args)


def flash_bwd_t(qkv_t, kv_tok, o_t, do_t, lse, q_rb, k_i, v_i, k_rb, o_rb, sink, window, name):
    s_len = qkv_t.shape[1]
    tq = _row_tile(s_len, 512)
    if window:
        tk = 256
        n_q = 2
    else:
        tk = _row_tile(s_len, 512)
        n_q = s_len // tq
    n_qb = s_len // tq

    def body(*refs):
        if window:
            (sink_ref, q_ref, k_ref, v_ref, kt_ref, o_ref, do_ref, lse_ref,
             dq_ref, dk_ref, dv_ref, dsink_ref, dk_sc, dv_sc) = refs
        else:
            (q_ref, k_ref, v_ref, kt_ref, o_ref, do_ref, lse_ref, dq_ref, dk_ref, dv_ref, dk_sc, dv_sc) = refs
        h, j, t = pl.program_id(0), pl.program_id(1), pl.program_id(2)
        q_blk = (j + 1) // 2 - 1 + t if window else t

        @pl.when((j == 0) & (t == 0))
        def _():
            dq_ref[...] = jnp.zeros(dq_ref.shape, F32)
            if window:
                dsink_ref[...] = jnp.zeros((8, LANES), F32)

        @pl.when(t == 0)
        def _():
            dk_sc[...] = jnp.zeros((tk, HEAD_DIM), F32)
            dv_sc[...] = jnp.zeros((tk, HEAD_DIM), F32)

        def step():
            k, v, k_t = k_ref[...], v_ref[...], kt_ref[...]
            if window:
                mask = _pos_mask_t(j * tk, q_blk * tq, s_len, tk, tq)
                lane = lax.broadcasted_iota(jnp.int32, (8, LANES), 1)
                sink_tile = jnp.zeros((8, LANES), F32)
            dk_acc = dk_sc[...]
            dv_acc = dv_sc[...]
            for g in range(GROUP):
                rows = slice(g * HEAD_DIM, (g + 1) * HEAD_DIM)
                q_t, o_g, do_g = q_ref[rows, :], o_ref[rows, :], do_ref[rows, :]
                s_t = jnp.dot(k, q_t, preferred_element_type=F32)
                if window:
                    s_t = jnp.where(mask, s_t, NEG)
                lse_row = lse_ref[g * LSE_ROWS:g * LSE_ROWS + 1, :]
                p_t = jnp.exp(s_t - lse_row)
                delta = jnp.sum(do_g.astype(F32) * o_g.astype(F32), axis=0, keepdims=True)
                dp_t = jnp.dot(v, do_g, preferred_element_type=F32)
                ds_t = (p_t * (dp_t - delta)).astype(BF16)
                dv_acc = dv_acc + lax.dot_general(p_t.astype(BF16), do_g, NT_DIMS, preferred_element_type=F32)
                dk_acc = dk_acc + lax.dot_general(ds_t, q_t, NT_DIMS, preferred_element_type=F32)
                dq_ref[q_blk, rows, :] += jnp.dot(k_t, ds_t, preferred_element_type=F32)
                if window:
                    p_sink = jnp.exp(sink_ref[h * GROUP + g] - lse_row)
                    term = -jnp.sum(p_sink * delta, axis=1, keepdims=True)
                    sink_tile = jnp.where(lane == g, term, sink_tile)
            dk_sc[...] = dk_acc
            dv_sc[...] = dv_acc
            if window:
                @pl.when((j % 2 == 0) & (t == 1))
                def _():
                    dsink_ref[...] += sink_tile

        if window:
            pl.when((q_blk >= 0) & (q_blk < n_qb))(step)
        else:
            step()

        @pl.when(t == n_q - 1)
        def _():
            dk_ref[...] = dk_sc[...]
            dv_ref[...] = dv_sc[...]

    if window:
        qb = lambda j, t: jnp.clip((j + 1) // 2 - 1 + t, 0, n_qb - 1)
    else:
        qb = lambda j, t: t
    in_specs = [pl.BlockSpec((GROUP_W, tq), lambda h, j, t: (q_rb + h, qb(j, t))),
                pl.BlockSpec((None, tk, HEAD_DIM), lambda h, j, t: (k_i + h, j, 0)),
                pl.BlockSpec((None, tk, HEAD_DIM), lambda h, j, t: (v_i + h, j, 0)),
                pl.BlockSpec((HEAD_DIM, tk), lambda h, j, t: (k_rb + h, j)),
                pl.BlockSpec((GROUP_W, tq), lambda h, j, t: (o_rb + h, qb(j, t))),
                pl.BlockSpec((GROUP_W, tq), lambda h, j, t: (o_rb + h, qb(j, t))),
                pl.BlockSpec((GROUP * LSE_ROWS, tq), lambda h, j, t: (h, qb(j, t)))]
    args = [qkv_t, kv_tok, kv_tok, qkv_t, o_t, do_t, lse]
    kv_out = _sds((KV_HEADS, s_len, HEAD_DIM), F32)
    out_specs = [pl.BlockSpec((n_qb, GROUP_W, tq), lambda h, j, t: (0, h, 0)),
                 pl.BlockSpec((None, tk, HEAD_DIM), lambda h, j, t: (h, j, 0)),
                 pl.BlockSpec((None, tk, HEAD_DIM), lambda h, j, t: (h, j, 0))]
    out_shape = [_sds((n_qb, KV_HEADS * GROUP_W, tq), F32), kv_out, kv_out]
    if window:
        in_specs = [pl.BlockSpec(memory_space=pltpu.SMEM)] + in_specs
        args = [sink] + args
        out_specs.append(pl.BlockSpec((None, 8, LANES), lambda h, j, t: (h, 0, 0)))
        out_shape.append(_sds((KV_HEADS, 8, LANES), F32))
    return pl.pallas_call(
        body, name=name, grid=(KV_HEADS, s_len // tk, n_q),
        in_specs=in_specs, out_specs=out_specs, out_shape=out_shape,
        scratch_shapes=[pltpu.VMEM((tk, HEAD_DIM), F32), pltpu.VMEM((tk, HEAD_DIM), F32)],
        compiler_params=_params("arbitrary", "arbitrary", "arbitrary"),
    )(*args)


SGU_GROUPS = 8
SGU_CHUNK = 128
GELU_C = float(np.sqrt(2.0 / np.pi))
GELU_A = 0.044715


def _gelu(x):
    return x * (0.5 * (1.0 + jnp.tanh(GELU_C * (x + GELU_A * (x * x * x)))))


def _gelu_grad(x):
    t = jnp.tanh(GELU_C * (x + GELU_A * (x * x * x)))
    return 0.5 * (1.0 + t) + 0.5 * x * (1.0 - t * t) * (GELU_C * (1.0 + 3.0 * GELU_A * (x * x)))


def _layernorm_stats(v):
    mu = jnp.mean(v, axis=-1, keepdims=True)
    var = jnp.mean(jnp.square(v - mu), axis=-1, keepdims=True)
    rstd = lax.rsqrt(var + EPS)
    return (v - mu) * rstd, rstd


def sgu_mid_fwd(zpre, ln_g, ln_b, ws, bsb, name):
    s_len, width = zpre.shape
    d = width // 2
    ts = _row_tile(s_len, 256)

    def body(z_ref, g_ref, b_ref, ws_ref, bs_ref, y_ref):
        z = _gelu(z_ref[...])
        u, v = z[:, :d], z[:, d:]
        vhat, _ = _layernorm_stats(v)
        vn = (vhat * g_ref[...] + b_ref[...]).astype(BF16)
        for n in range(ts // SGU_CHUNK):
            rows = slice(n * SGU_CHUNK, (n + 1) * SGU_CHUNK)
            for g in range(SGU_GROUPS):
                cols = slice(g * LANES, (g + 1) * LANES)
                mixed = jnp.dot(ws_ref[g], vn[rows, cols], preferred_element_type=F32) + bs_ref[g]
                y_ref[rows, cols] = (u[rows, cols] * mixed).astype(BF16)

    vec = pl.BlockSpec((1, d), lambda i: (0, 0))
    cube = pl.BlockSpec((SGU_GROUPS, SGU_CHUNK, SGU_CHUNK), lambda i: (0, 0, 0))
    return pl.pallas_call(
        body, name=name, grid=(s_len // ts,),
        in_specs=[pl.BlockSpec((ts, width), lambda i: (i, 0)), vec, vec, cube, cube],
        out_specs=pl.BlockSpec((ts, d), lambda i: (i, 0)),
        out_shape=_sds((s_len, d), BF16),
        compiler_params=_params("arbitrary"),
    )(zpre, ln_g, ln_b, ws, bsb)


def sgu_mid_bwd(zpre, dy, ln_g, ln_b, ws, wst, bsb, name):
    s_len, width = zpre.shape
    d = width // 2
    ts = _row_tile(s_len, 256)
    n_steps = s_len // ts

    def body(z_ref, dy_ref, g_ref, b_ref, ws_ref, wst_ref, bs_ref,
             dz_ref, dws_ref, dbs_ref, dg_ref, db_ref, du_sc, dvn_sc):
        i = pl.program_id(0)

        @pl.when(i == 0)
        def _():
            dws_ref[...] = jnp.zeros(dws_ref.shape, F32)
            dbs_ref[...] = jnp.zeros(dbs_ref.shape, F32)
            dg_ref[...] = jnp.zeros(dg_ref.shape, F32)
            db_ref[...] = jnp.zeros(db_ref.shape, F32)

        zp = z_ref[...]
        z = _gelu(zp)
        u, v = z[:, :d], z[:, d:]
        vhat, rstd = _layernorm_stats(v)
        gain = g_ref[...]
        vn = (vhat * gain + b_ref[...]).astype(BF16)
        dyf = dy_ref[...].astype(F32)
        for n in range(ts // SGU_CHUNK):
            rows = slice(n * SGU_CHUNK, (n + 1) * SGU_CHUNK)
            for g in range(SGU_GROUPS):
                cols = slice(g * LANES, (g + 1) * LANES)
                vt = vn[rows, cols]
                mixed = jnp.dot(ws_ref[g], vt, preferred_element_type=F32) + bs_ref[g]
                dyt = dyf[rows, cols]
                du_sc[rows, cols] = dyt * mixed
                dmixed = dyt * u[rows, cols]
                dmb = dmixed.astype(BF16)
                dvn_sc[rows, cols] = jnp.dot(wst_ref[g], dmb, preferred_element_type=F32)
                dws_ref[g] += lax.dot_general(dmb, vt, NT_DIMS, preferred_element_type=F32)
                dbs_ref[g] += dmixed
        dvn = dvn_sc[...]
        dg_ref[...] += jnp.sum(dvn * vhat, axis=0, keepdims=True)
        db_ref[...] += jnp.sum(dvn, axis=0, keepdims=True)
        dvh = dvn * gain
        dv = rstd * (dvh - jnp.mean(dvh, axis=-1, keepdims=True)
                     - vhat * jnp.mean(dvh * vhat, axis=-1, keepdims=True))
        gp = _gelu_grad(zp)
        dz_ref[:, :d] = (du_sc[...] * gp[:, :d]).astype(BF16)
        dz_ref[:, d:] = (dv * gp[:, d:]).astype(BF16)

        @pl.when(i == n_steps - 1)
        def _():
            for g in range(SGU_GROUPS):
                tot = jnp.sum(dbs_ref[g], axis=1, keepdims=True)
                dbs_ref[g] = jnp.broadcast_to(tot, (SGU_CHUNK, LANES))

    vec = pl.BlockSpec((1, d), lambda i: (0, 0))
    cube = pl.BlockSpec((SGU_GROUPS, SGU_CHUNK, SGU_CHUNK), lambda i: (0, 0, 0))
    cube_shape = _sds((SGU_GROUPS, SGU_CHUNK, SGU_CHUNK), F32)
    return pl.pallas_call(
        body, name=name, grid=(n_steps,),
        in_specs=[pl.BlockSpec((ts, width), lambda i: (i, 0)), pl.BlockSpec((ts, d), lambda i: (i, 0)),
                  vec, vec, cube, cube, cube],
        out_specs=[pl.BlockSpec((ts, width), lambda i: (i, 0)), cube, cube, vec, vec],
        out_shape=[_sds((s_len, width), BF16), cube_shape, cube_shape, _sds((1, d), F32), _sds((1, d), F32)],
        scratch_shapes=[pltpu.VMEM((ts, d), F32), pltpu.VMEM((ts, d), F32)],
        compiler_params=_params("arbitrary"),
    )(zpre, dy, ln_g, ln_b, ws, wst, bsb)


def loss_head(x, g, target, name):
    s_len, d = x.shape
    tm = _row_tile(s_len, 512)

    def body(x_ref, g_ref, t_ref, dx_ref, dxb_ref, dg_ref, loss_ref):
        i = pl.program_id(0)
        xf = x_ref[...]
        gain = g_ref[...]
        r = lax.rsqrt(jnp.mean(xf * xf, axis=-1, keepdims=True) + EPS)
        xhat = xf * r
        err = xhat * gain - t_ref[...]
        row = jnp.mean(err * err, axis=-1, keepdims=True)
        part = 0.5 * jnp.sum(row, axis=0, keepdims=True)
        dy = err * (1.0 / d)
        dg_part = jnp.sum(dy * xhat, axis=0, keepdims=True)

        @pl.when(i == 0)
        def _():
            dg_ref[...] = dg_part
            loss_ref[...] = jnp.broadcast_to(part, (8, LANES))

        @pl.when(i > 0)
        def _():
            dg_ref[...] += dg_part
            loss_ref[...] += jnp.broadcast_to(part, (8, LANES))

        dxh = dy * gain
        dx = r * (dxh - xhat * jnp.mean(dxh * xhat, axis=-1, keepdims=True))
        dx_ref[...] = dx
        dxb_ref[...] = dx.astype(BF16)

    row_spec = pl.BlockSpec((tm, d), lambda i: (i, 0))
    vec = pl.BlockSpec((1, d), lambda i: (0, 0))
    return pl.pallas_call(
        body, name=name, grid=(s_len // tm,),
        in_specs=[row_spec, vec, row_spec],
        out_specs=[row_spec, row_spec, vec, pl.BlockSpec((8, LANES), lambda i: (0, 0))],
        out_shape=[_sds((s_len, d), F32), _sds((s_len, d), BF16), _sds((1, d), F32), _sds((8, LANES), F32)],
        compiler_params=_params("arbitrary"),
    )(x, g, target)


FLIP_BITS = {"c": (0, 0, 1), "x": (1, 0, 0), "y": (0, 1, 0), "xy": (1, 1, 0),
             "xc": (1, 0, 1), "yc": (0, 1, 1), "xyc": (1, 1, 1)}
CHIP_FLIPS = ("x", "y", "xy")


def _flip(pos, name):
    return tuple(1 - p if bit else p for p, bit in zip(pos, FLIP_BITS[name]))


def _chip(pos):
    return 2 * pos[0] + pos[1]


def exchange(name, ins, out_shapes, remote, local):
    n_in, n_out = len(ins), len(out_shapes)

    def body(*refs):
        in_refs, out_refs = refs[:n_in], refs[n_in:n_in + n_out]
        send_sems, recv_sems, local_sems = refs[n_in + n_out:]
        me = (lax.axis_index("x"), lax.axis_index("y"), lax.axis_index("c"))

        def copy(k, sender, receiver):
            ii, src_fn, oi, dst_fn, flip = remote[k]
            return pltpu.make_async_remote_copy(
                src_ref=src_fn(in_refs[ii], sender, receiver), dst_ref=dst_fn(out_refs[oi], sender),
                send_sem=send_sems.at[k], recv_sem=recv_sems.at[k],
                device_id=receiver, device_id_type=MESH)

        stays = []
        for k, (ii, src_fn, oi, dst_fn) in enumerate(local):
            cp = pltpu.make_async_copy(src_fn(in_refs[ii], me), dst_fn(out_refs[oi], me), local_sems.at[k])
            cp.start()
            stays.append(cp)
        sends = []
        for k in range(len(remote)):
            cp = copy(k, me, _flip(me, remote[k][4]))
            cp.start()
            sends.append(cp)
        for k in range(len(remote)):
            copy(k, _flip(me, remote[k][4]), me).wait_recv()
        for cp in sends:
            cp.wait_send()
        for cp in stays:
            cp.wait()

    hbm = pl.BlockSpec(memory_space=pl.ANY)
    return pl.pallas_call(
        body, name=name,
        in_specs=[hbm] * n_in, out_specs=[hbm] * n_out, out_shape=list(out_shapes),
        scratch_shapes=[pltpu.SemaphoreType.DMA((max(len(remote), 1),)),
                        pltpu.SemaphoreType.DMA((max(len(remote), 1),)),
                        pltpu.SemaphoreType.DMA((max(len(local), 1),))],
        compiler_params=pltpu.CompilerParams(has_side_effects=True),
    )(*ins)


def gather_chips(shards, name):
    remote, local = [], []
    for t in range(len(shards)):
        whole = lambda ref, sender, receiver=None: ref
        slot = lambda ref, sender: ref.at[_chip(sender)]
        local.append((t, whole, t, slot))
        for flip in CHIP_FLIPS:
            remote.append((t, whole, t, slot, flip))
    outs = [_sds((N_CHIPS,) + a.shape, a.dtype) for a in shards]
    return exchange(name, shards, outs, remote, local)


def gather_all(buf, name):
    whole = lambda ref, sender, receiver=None: ref
    slot = lambda ref, sender: ref.at[4 * sender[0] + 2 * sender[1] + sender[2]]
    remote = [(0, whole, 0, slot, flip) for flip in FLIP_BITS]
    local = [(0, whole, 0, slot)]
    return exchange(name, [buf], [_sds((8,) + buf.shape, buf.dtype)], remote, local)[0]


def _half(ref, core, axis):
    rows = ref.shape[axis] // 2
    idx = (slice(None),) * axis + (pl.ds(core * rows, rows),)
    return ref.at[idx]


def reduce_to_owner_chips(grads, stacks, name):
    n_t = len(grads)
    remote, local, outs = [], [], []
    for t, g in enumerate(grads):
        half_shape = (N_CHIPS, g.shape[1] // 2, g.shape[2])
        outs += [_sds(half_shape, BF16), _sds(half_shape, BF16)]
        local.append((t, lambda ref, me: _half(ref, me[2], 1), 2 * t, lambda ref, me: ref))
        remote.append((t, lambda ref, sender, receiver: _half(ref, receiver[2], 1),
                       2 * t + 1, lambda ref, sender: ref, "c"))
    got = exchange(name + "_swap", grads, outs, remote, local)
    add2 = lambda a, b: ((a.astype(F32) + b.astype(F32)),)
    chip_sums = []
    for t in range(n_t):
        mine, theirs = got[2 * t], got[2 * t + 1]
        flat = lambda a: a.reshape(-1, a.shape[-1])
        chip_sums.append(ew(add2, [flat(mine), flat(theirs)], [BF16], f"{name}_add{t}")[0].reshape(mine.shape))
    remote, local, outs = [], [], []
    for t, p in enumerate(chip_sums):
        outs += [_sds(p.shape[1:], BF16), _sds((len(CHIP_FLIPS),) + p.shape[1:], BF16)]
        local.append((t, lambda ref, me: ref.at[_chip(me)], 2 * t, lambda ref, me: ref))
        for f, flip in enumerate(CHIP_FLIPS):
            remote.append((t, lambda ref, sender, receiver: ref.at[_chip(receiver)],
                           2 * t + 1, lambda ref, sender, f=f: ref.at[f], flip))
    got = exchange(name + "_scatter", chip_sums, outs, remote, local)
    add4 = lambda a, b, c, d: (((a.astype(F32) + b.astype(F32)) + c.astype(F32)) + d.astype(F32),)
    totals = []
    for t in range(n_t):
        mine, theirs = got[2 * t], got[2 * t + 1]
        totals.append(ew(add4, [mine, theirs[0], theirs[1], theirs[2]], [F32], f"{name}_sum{t}")[0])
    names = []
    for out_name, _ in stacks:
        if out_name not in names:
            names.append(out_name)
    outs = []
    for out_name in names:
        layers = 1 + max(l for n, l in stacks if n == out_name)
        t0 = [t for t, (n, _) in enumerate(stacks) if n == out_name][0]
        outs.append(_sds((layers, 2 * totals[t0].shape[0], totals[t0].shape[1]), F32))
    remote, local = [], []
    for t, (out_name, layer) in enumerate(stacks):
        oi = names.index(out_name)
        place = lambda ref, sender, layer=layer: _half(ref.at[layer], sender[2], 0)
        local.append((t, lambda ref, me: ref, oi, place))
        remote.append((t, lambda ref, sender, receiver: ref, oi, place, "c"))
    full = exchange(name + "_share", totals, outs, remote, local)
    return dict(zip(names, full))


def _tok(t):
    return t.transpose(1, 0, 2).reshape(t.shape[1], t.shape[0] * t.shape[2])


def _heads(t):
    return t.reshape(t.shape[0], t.shape[1] // HEAD_DIM, HEAD_DIM).transpose(1, 0, 2)


def _tile2(vec):
    return jnp.tile(vec.reshape(1, HEAD_DIM), (1, 2))


def local_step(x, target, gw, rep):
    s_len, d = x.shape
    tabs = rope_tables(s_len)
    depth = rep["mlp_norm"].shape[0]
    row = lambda a: a.reshape(1, -1)
    saved = []
    h = x
    for layer in range(depth):
        i = layer // 2
        tag = f"l{layer}"
        if layer % 2 == 0:
            hn, proj = norm_mm(h, row(rep["att_norm"][i]), gw["att_w_in"], i, F32, tag + "_att_proj")
            qkv = prep_fwd(proj, tabs, _tile2(rep["att_qnorm"][i]), _tile2(rep["att_knorm"][i]), tag + "_att_prep")
            qkv_t = qkv.T
            kv_tok = _heads(jnp.concatenate([qkv[:, KA_COL:KA_COL + 2 * LANES], qkv[:, KB_COL:KB_COL + 2 * LANES]], 1))
            oa, lse_a = flash_fwd_t(qkv_t, kv_tok, QA_COL // GROUP_W, 0, (KA_COL + LANES) // HEAD_DIM,
                                    rep["att_sink"][i], True, tag + "_win_fwd")
            ob, lse_b = flash_fwd_t(qkv_t, kv_tok, QB_COL // GROUP_W, 4, (KB_COL + LANES) // HEAD_DIM,
                                    None, False, tag + "_grid_fwd")
            o_t = jnp.concatenate([oa, ob], axis=0)
            ocat = o_t.T
            out = mm_res(ocat, gw["att_w_out"], i, h, tag + "_att_out")
            mix_saved = (h, hn, proj, qkv_t, kv_tok, o_t, ocat, lse_a, lse_b)
        else:
            hn, zpre = norm_mm(h, row(rep["sgu_norm"][i]), gw["sgu_w_in"], i, F32, tag + "_sgu_in")
            ws = rep["sgu_w_s"][i].astype(BF16)
            bsb = jnp.broadcast_to(rep["sgu_b_s"][i][:, :, None], (SGU_GROUPS, SGU_CHUNK, LANES))
            y = sgu_mid_fwd(zpre, row(rep["sgu_ln_g"][i]), row(rep["sgu_ln_b"][i]), ws, bsb, tag + "_sgu_mid")
            out = mm_res(y, gw["sgu_w_out"], i, h, tag + "_sgu_out")
            mix_saved = (h, hn, zpre, y, ws, bsb)
        hm, a = norm_mm(out, row(rep["mlp_norm"][layer]), gw["mlp_w1"], layer, BF16, tag + "_mlp_up")
        nxt = mm_res(a, gw["mlp_w2"], layer, out, tag + "_mlp_down", relu2=True)
        saved.append((mix_saved, (out, hm, a)))
        h = nxt
    dh, dhb, d_final, loss_tile = loss_head(h, row(rep["final_norm"]), target, "loss_head")
    big, tags = [], []
    small = {k: [None] * v.shape[0] for k, v in rep.items() if k != "final_norm"}
    small["final_norm"] = d_final.reshape(-1)
    for layer in reversed(range(depth)):
        i = layer // 2
        tag = f"l{layer}"
        mix_saved, (xin, hm, a) = saved[layer]
        da = mm_nt_relu2_bwd(dhb, gw["mlp_w2"], layer, a, tag + "_mlp_down_bwd")
        big.append(dw_mm(a, dhb, tag + "_mlp_dw2", col_sharded=False, relu2=True))
        tags.append(("mlp_w2", layer))
        big.append(dw_mm(hm, da, tag + "_mlp_dw1", col_sharded=True))
        tags.append(("mlp_w1", layer))
        dh, dhb, dg = dx_norm(da, gw["mlp_w1"], layer, xin, row(rep["mlp_norm"][layer]), dh, tag + "_mlp_up_bwd")
        small["mlp_norm"][layer] = dg.reshape(-1)
        if layer % 2 == 0:
            xin, hn, proj, qkv_t, kv_tok, o_t, ocat, lse_a, lse_b = mix_saved
            docat = mm_nt(dhb, gw["att_w_out"], i, tag + "_att_out_bwd")
            big.append(dw_mm(ocat, dhb, tag + "_att_dwout", col_sharded=False))
            tags.append(("att_w_out", i))
            do_t = docat.T
            dqa, dka, dva, dsink = flash_bwd_t(qkv_t, kv_tok, o_t, do_t, lse_a, QA_COL // GROUP_W, 0, 2,
                                               KA_COL // HEAD_DIM, 0, rep["att_sink"][i], True, tag + "_win_bwd")
            dqb, dkb, dvb = flash_bwd_t(qkv_t, kv_tok, o_t, do_t, lse_b, QB_COL // GROUP_W, 4, 6,
                                        KB_COL // HEAD_DIM, 2, None, False, tag + "_grid_bwd")
            qg, kg = _tile2(rep["att_qnorm"][i]), _tile2(rep["att_knorm"][i])
            q_tok = lambda t: t.transpose(0, 2, 1).reshape(s_len, t.shape[1])
            dproj, dqg, dkg = prep_bwd(proj, q_tok(dqa), _tok(dka), _tok(dva), q_tok(dqb), _tok(dkb), _tok(dvb),
                                       tabs, qg, kg, tag + "_att_prep_bwd")
            big.append(dw_mm(hn, dproj, tag + "_att_dwin", col_sharded=True))
            tags.append(("att_w_in", i))
            dh, dhb, dg = dx_norm(dproj, gw["att_w_in"], i, xin, row(rep["att_norm"][i]), dh, tag + "_att_proj_bwd")
            small["att_norm"][i] = dg.reshape(-1)
            small["att_sink"][i] = dsink[:, 0, :GROUP].reshape(-1)
            small["att_qnorm"][i] = dqg[0, :HEAD_DIM] + dqg[0, HEAD_DIM:]
            small["att_knorm"][i] = dkg[0, :HEAD_DIM] + dkg[0, HEAD_DIM:]
        else:
            xin, hn, zpre, y, ws, bsb = mix_saved
            dy = mm_nt(dhb, gw["sgu_w_out"], i, tag + "_sgu_out_bwd")
            big.append(dw_mm(y, dhb, tag + "_sgu_dwout", col_sharded=False))
            tags.append(("sgu_w_out", i))
            wst = ws.transpose(0, 2, 1)
            dz, dws, dbs, dlg, dlb = sgu_mid_bwd(zpre, dy, row(rep["sgu_ln_g"][i]), row(rep["sgu_ln_b"][i]),
                                                 ws, wst, bsb, tag + "_sgu_mid_bwd")
            big.append(dw_mm(hn, dz, tag + "_sgu_dwin", col_sharded=True))
            tags.append(("sgu_w_in", i))
            dh, dhb, dg = dx_norm(dz, gw["sgu_w_in"], i, xin, row(rep["sgu_norm"][i]), dh, tag + "_sgu_in_bwd")
            small["sgu_norm"][i] = dg.reshape(-1)
            small["sgu_ln_g"][i] = dlg.reshape(-1)
            small["sgu_ln_b"][i] = dlb.reshape(-1)
            small["sgu_w_s"][i] = dws
            small["sgu_b_s"][i] = dbs[:, :, 0]
    small = {k: (v if k == "final_norm" else jnp.stack(v)) for k, v in small.items()}
    return loss_tile, dh, big, tags, small


BIG = ("att_w_in", "att_w_out", "sgu_w_in", "sgu_w_out", "mlp_w1", "mlp_w2")
SHARDED_VEC = ("sgu_norm", "sgu_ln_g", "sgu_ln_b")
REPLICATED = ("att_norm", "att_sink", "att_qnorm", "att_knorm", "sgu_w_s", "sgu_b_s", "mlp_norm", "final_norm")
WEIGHTS = ("att_norm", "att_w_in", "att_sink", "att_qnorm", "att_knorm", "att_w_out", "sgu_norm", "sgu_w_in",
           "sgu_ln_g", "sgu_ln_b", "sgu_w_s", "sgu_b_s", "sgu_w_out", "mlp_norm", "mlp_w1", "mlp_w2", "final_norm")
SMALL = tuple(n for n in WEIGHTS if n not in BIG)
PACK_ALIGN = 8 * LANES


def _pack_small(small, loss_tile):
    parts = [small[n].reshape(-1) for n in SMALL] + [loss_tile[0, :1]]
    flat = jnp.concatenate(parts)
    pad = -flat.shape[0] % PACK_ALIGN
    return jnp.pad(flat, (0, pad)).reshape(-1, LANES)


def _unpack_small(flat2d, shapes):
    flat = flat2d.reshape(-1)
    out, off = {}, 0
    for n in SMALL:
        size = int(np.prod(shapes[n]))
        out[n] = flat[off:off + size].reshape(shapes[n])
        off += size
    return out, flat[off]


def kernel(x, att_norm, att_w_in, att_sink, att_qnorm, att_knorm, att_w_out, sgu_norm, sgu_w_in, sgu_ln_g, sgu_ln_b, sgu_w_s, sgu_b_s, sgu_w_out, mlp_norm, mlp_w1, mlp_w2, final_norm, loss_target, m_att_norm, m_att_w_in, m_att_sink, m_att_qnorm, m_att_knorm, m_att_w_out, m_sgu_norm, m_sgu_w_in, m_sgu_ln_g, m_sgu_ln_b, m_sgu_w_s, m_sgu_b_s, m_sgu_w_out, m_mlp_norm, m_mlp_w1, m_mlp_w2, m_final_norm, v_att_norm, v_att_w_in, v_att_sink, v_att_qnorm, v_att_knorm, v_att_w_out, v_sgu_norm, v_sgu_w_in, v_sgu_ln_g, v_sgu_ln_b, v_sgu_w_s, v_sgu_b_s, v_sgu_w_out, v_mlp_norm, v_mlp_w1, v_mlp_w2, v_final_norm):
    w = dict(att_norm=att_norm, att_w_in=att_w_in, att_sink=att_sink, att_qnorm=att_qnorm, att_knorm=att_knorm,
             att_w_out=att_w_out, sgu_norm=sgu_norm, sgu_w_in=sgu_w_in, sgu_ln_g=sgu_ln_g, sgu_ln_b=sgu_ln_b,
             sgu_w_s=sgu_w_s, sgu_b_s=sgu_b_s, sgu_w_out=sgu_w_out, mlp_norm=mlp_norm, mlp_w1=mlp_w1,
             mlp_w2=mlp_w2, final_norm=final_norm)
    m = dict(att_norm=m_att_norm, att_w_in=m_att_w_in, att_sink=m_att_sink, att_qnorm=m_att_qnorm,
             att_knorm=m_att_knorm, att_w_out=m_att_w_out, sgu_norm=m_sgu_norm, sgu_w_in=m_sgu_w_in,
             sgu_ln_g=m_sgu_ln_g, sgu_ln_b=m_sgu_ln_b, sgu_w_s=m_sgu_w_s, sgu_b_s=m_sgu_b_s,
             sgu_w_out=m_sgu_w_out, mlp_norm=m_mlp_norm, mlp_w1=m_mlp_w1, mlp_w2=m_mlp_w2,
             final_norm=m_final_norm)
    v = dict(att_norm=v_att_norm, att_w_in=v_att_w_in, att_sink=v_att_sink, att_qnorm=v_att_qnorm,
             att_knorm=v_att_knorm, att_w_out=v_att_w_out, sgu_norm=v_sgu_norm, sgu_w_in=v_sgu_w_in,
             sgu_ln_g=v_sgu_ln_g, sgu_ln_b=v_sgu_ln_b, sgu_w_s=v_sgu_w_s, sgu_b_s=v_sgu_b_s,
             sgu_w_out=v_sgu_w_out, mlp_norm=v_mlp_norm, mlp_w1=v_mlp_w1, mlp_w2=v_mlp_w2,
             final_norm=v_final_norm)
    chip = 2 * lax.axis_index("x") + lax.axis_index("y")

    vecs = jnp.stack([w[n] for n in SHARDED_VEC])
    gathered = gather_chips([w[n].astype(BF16) for n in BIG] + [vecs], "gather_weights")
    gw = dict(zip(BIG, gathered[:len(BIG)]))
    vec_full = gathered[-1].transpose(1, 2, 0, 3).reshape(vecs.shape[0], vecs.shape[1], -1)
    rep = {n: w[n] for n in REPLICATED}
    rep.update({n: vec_full[k] for k, n in enumerate(SHARDED_VEC)})

    loss_tile, grad_x, big, tags, small = local_step(x[0], loss_target[0], gw, rep)

    grads = reduce_to_owner_chips(big, tags, "grads")
    packed = _pack_small(small, loss_tile)
    everyone = gather_all(packed, "gather_small")
    add8 = lambda *a: (((a[0] + a[1]) + (a[2] + a[3])) + ((a[4] + a[5]) + (a[6] + a[7])),)
    total = ew(add8, [everyone[k] for k in range(8)], [F32], "sum_small")[0]
    small_g, loss = _unpack_small(total, {n: small[n].shape for n in SMALL})
    width = w["sgu_norm"].shape[1]
    for n in SHARDED_VEC:
        small_g[n] = lax.dynamic_slice_in_dim(small_g[n], chip * width, width, axis=1)
    grads.update(small_g)
    for n in BIG:
        grads[n] = grads[n].reshape(w[n].shape)

    delta, new_m, new_v = {}, {}, {}
    for n in WEIGHTS:
        shape = w[n].shape
        two_d = (lambda a: a.reshape(1, -1)) if len(shape) == 1 else (lambda a: a)
        dn, mn, vn = adamw(two_d(w[n]), two_d(grads[n]), two_d(m[n]), two_d(v[n]), "adamw_" + n)
        delta[n], new_m[n], new_v[n] = dn.reshape(shape), mn.reshape(shape), vn.reshape(shape)
    return (loss, grad_x[None], *[grads[n] for n in WEIGHTS], *[delta[n] for n in WEIGHTS],
            *[new_m[n] for n in WEIGHTS], *[new_v[n] for n in WEIGHTS])
```

```python
import numpy as np
import jax
import jax.numpy as jnp
from jax import lax
from jax.experimental import pallas as pl
from jax.experimental.pallas import tpu as pltpu

F32 = jnp.float32
BF16 = jnp.bfloat16
MESH = pl.DeviceIdType.MESH

EPS = 1e-6
HEAD_DIM = 64
BLOCK = 128
GRID_W = 64
ROPE_THETA = 10000.0
N_CHIPS = 4
LANES = 128
V7X_VMEM_BYTES = 64 * 1024 * 1024
VMEM_LIMIT = V7X_VMEM_BYTES - 8 * 1024 * 1024

ADAM_LR = 0.001
ADAM_B1 = 0.9
ADAM_B2 = 0.999
ADAM_EPS = 1e-08
ADAM_WD = 0.01
ADAM_STEP = 10

NT_DIMS = (((1,), (1,)), ((), ()))
TN_DIMS = (((0,), (0,)), ((), ()))


def _params(*sem):
    return pltpu.CompilerParams(dimension_semantics=sem, vmem_limit_bytes=VMEM_LIMIT)


def _sds(shape, dtype):
    return jax.ShapeDtypeStruct(tuple(shape), dtype)


def _row_tile(rows, want):
    t = min(rows, want)
    assert rows % t == 0, (rows, want)
    return t


def norm_mm(x, g, w4, layer, out_dtype, name):
    s_len, d = x.shape
    ns = w4.shape[-1]
    tm = _row_tile(s_len, 512)

    def body(x_ref, g_ref, w_ref, h_ref, y_ref):
        xf = x_ref[...]
        r = lax.rsqrt(jnp.mean(xf * xf, axis=-1, keepdims=True) + EPS)
        h = ((xf * r) * g_ref[...]).astype(BF16)
        h_ref[...] = h
        for s in range(N_CHIPS):
            y_ref[:, s * ns:(s + 1) * ns] = jnp.dot(h, w_ref[s], preferred_element_type=F32).astype(y_ref.dtype)

    return pl.pallas_call(
        body, name=name, grid=(s_len // tm,),
        in_specs=[pl.BlockSpec((tm, d), lambda i: (i, 0)),
                  pl.BlockSpec((1, d), lambda i: (0, 0)),
                  pl.BlockSpec((N_CHIPS, None, d, ns), lambda i: (0, layer, 0, 0))],
        out_specs=[pl.BlockSpec((tm, d), lambda i: (i, 0)),
                   pl.BlockSpec((tm, N_CHIPS * ns), lambda i: (i, 0))],
        out_shape=[_sds((s_len, d), BF16), _sds((s_len, N_CHIPS * ns), out_dtype)],
        compiler_params=_params("arbitrary"),
    )(x, g, w4)


def mm_res(a, w4, layer, res, name, relu2=False):
    s_len, k = a.shape
    kq, n = w4.shape[-2:]
    assert kq * N_CHIPS == k
    tm = _row_tile(s_len, 256 if k > 1024 else 512)

    def body(a_ref, w0, w1, w2, w3, r_ref, o_ref):
        acc = r_ref[...]
        for s, w_ref in enumerate((w0, w1, w2, w3)):
            av = a_ref[:, s * kq:(s + 1) * kq]
            if relu2:
                t = jnp.maximum(av.astype(F32), 0.0)
                av = (t * t).astype(BF16)
            acc = acc + jnp.dot(av, w_ref[...], preferred_element_type=F32)
        o_ref[...] = acc

    def wspec(s):
        return pl.BlockSpec((None, None, kq, n), lambda i: (s, layer, 0, 0))

    return pl.pallas_call(
        body, name=name, grid=(s_len // tm,),
        in_specs=[pl.BlockSpec((tm, k), lambda i: (i, 0)), wspec(0), wspec(1), wspec(2), wspec(3),
                  pl.BlockSpec((tm, n), lambda i: (i, 0))],
        out_specs=pl.BlockSpec((tm, n), lambda i: (i, 0)),
        out_shape=_sds((s_len, n), F32),
        compiler_params=_params("arbitrary"),
    )(a, w4, w4, w4, w4, res)


def mm_nt(dy, w4, layer, name):
    s_len, n = dy.shape
    mq = w4.shape[-2]
    tm = _row_tile(s_len, 512)

    def body(d_ref, w0, w1, w2, w3, o_ref):
        dv = d_ref[...]
        for s, w_ref in enumerate((w0, w1, w2, w3)):
            o_ref[:, s * mq:(s + 1) * mq] = lax.dot_general(
                dv, w_ref[...], NT_DIMS, preferred_element_type=F32).astype(BF16)

    def wspec(s):
        return pl.BlockSpec((None, None, mq, n), lambda i: (s, layer, 0, 0))

    return pl.pallas_call(
        body, name=name, grid=(s_len // tm,),
        in_specs=[pl.BlockSpec((tm, n), lambda i: (i, 0)), wspec(0), wspec(1), wspec(2), wspec(3)],
        out_specs=pl.BlockSpec((tm, N_CHIPS * mq), lambda i: (i, 0)),
        out_shape=_sds((s_len, N_CHIPS * mq), BF16),
        compiler_params=_params("arbitrary"),
    )(dy, w4, w4, w4, w4)


def mm_nt_relu2_bwd(dy, w4, layer, a, name):
    s_len, n = dy.shape
    mq = w4.shape[-2]
    tm = _row_tile(s_len, 512)

    def body(d_ref, w_ref, a_ref, o_ref):
        dv = d_ref[...]
        for s in range(N_CHIPS):
            cols = slice(s * mq, (s + 1) * mq)
            dz = lax.dot_general(dv, w_ref[s], NT_DIMS, preferred_element_type=F32)
            o_ref[:, cols] = (dz * (2.0 * jnp.maximum(a_ref[:, cols].astype(F32), 0.0))).astype(BF16)

    return pl.pallas_call(
        body, name=name, grid=(s_len // tm,),
        in_specs=[pl.BlockSpec((tm, n), lambda i: (i, 0)),
                  pl.BlockSpec((N_CHIPS, None, mq, n), lambda i: (0, layer, 0, 0)),
                  pl.BlockSpec((tm, N_CHIPS * mq), lambda i: (i, 0))],
        out_specs=pl.BlockSpec((tm, N_CHIPS * mq), lambda i: (i, 0)),
        out_shape=_sds((s_len, N_CHIPS * mq), BF16),
        compiler_params=_params("arbitrary"),
    )(dy, w4, a)


def dx_norm(dy, w4, layer, x, g, dres, name):
    s_len, d = x.shape
    ns = w4.shape[-1]
    tm = _row_tile(s_len, 512)

    def body(dy_ref, w_ref, x_ref, g_ref, dr_ref, dx_ref, dxb_ref, dg_ref):
        i = pl.program_id(0)
        dh = lax.dot_general(dy_ref[:, 0:ns], w_ref[0], NT_DIMS, preferred_element_type=F32)
        for s in range(1, N_CHIPS):
            dh = dh + lax.dot_general(dy_ref[:, s * ns:(s + 1) * ns], w_ref[s], NT_DIMS,
                                      preferred_element_type=F32)
        xf = x_ref[...]
        r = lax.rsqrt(jnp.mean(xf * xf, axis=-1, keepdims=True) + EPS)
        xhat = xf * r
        dg_part = jnp.sum(dh * xhat, axis=0, keepdims=True)

        @pl.when(i == 0)
        def _():
            dg_ref[...] = dg_part

        @pl.when(i > 0)
        def _():
            dg_ref[...] += dg_part

        dxh = dh * g_ref[...]
        dx = dr_ref[...] + r * (dxh - xhat * jnp.mean(dxh * xhat, axis=-1, keepdims=True))
        dx_ref[...] = dx
        dxb_ref[...] = dx.astype(BF16)

    row = pl.BlockSpec((tm, d), lambda i: (i, 0))
    vec = pl.BlockSpec((1, d), lambda i: (0, 0))
    return pl.pallas_call(
        body, name=name, grid=(s_len // tm,),
        in_specs=[pl.BlockSpec((tm, N_CHIPS * ns), lambda i: (i, 0)),
                  pl.BlockSpec((N_CHIPS, None, d, ns), lambda i: (0, layer, 0, 0)), row, vec, row],
        out_specs=[row, row, vec],
        out_shape=[_sds((s_len, d), F32), _sds((s_len, d), BF16), _sds((1, d), F32)],
        compiler_params=_params("arbitrary"),
    )(dy, w4, x, g, dres)


def dw_mm(a, b, name, col_sharded, relu2=False):
    s_len, k = a.shape
    n = b.shape[1]
    ts = _row_tile(s_len, 2048)
    tk = min(k, 1024)
    tn = n // N_CHIPS if col_sharded else min(n, 1024)
    n_s = s_len // ts

    def body(a_ref, b_ref, o_ref, acc_ref):
        s = pl.program_id(2)
        av = a_ref[...]
        if relu2:
            t = jnp.maximum(av.astype(F32), 0.0)
            av = (t * t).astype(BF16)
        part = lax.dot_general(av, b_ref[...], TN_DIMS, preferred_element_type=F32)
        if n_s == 1:
            o_ref[...] = part.astype(BF16)
            return

        @pl.when(s == 0)
        def _():
            acc_ref[...] = part

        @pl.when((s > 0) & (s < n_s - 1))
        def _():
            acc_ref[...] += part

        @pl.when(s == n_s - 1)
        def _():
            o_ref[...] = (acc_ref[...] + part).astype(BF16)

    if col_sharded:
        out_shape = _sds((N_CHIPS, k, tn), BF16)
        out_spec = pl.BlockSpec((None, tk, tn), lambda i, j, s: (j, i, 0))
    else:
        out_shape = _sds((N_CHIPS, k // N_CHIPS, n), BF16)
        rows_per = k // N_CHIPS
        assert tk % rows_per == 0 or rows_per % tk == 0
        if tk >= rows_per:
            out_shape = _sds((k, n), BF16)
            out_spec = pl.BlockSpec((tk, tn), lambda i, j, s: (i, j))
        else:
            per = rows_per // tk
            out_spec = pl.BlockSpec((None, tk, tn), lambda i, j, s: (i // per, i % per, j))

    out = pl.pallas_call(
        body, name=name, grid=(k // tk, n // tn, n_s),
        in_specs=[pl.BlockSpec((ts, tk), lambda i, j, s: (s, i)),
                  pl.BlockSpec((ts, tn), lambda i, j, s: (s, j))],
        out_specs=out_spec, out_shape=out_shape,
        scratch_shapes=[pltpu.VMEM((tk, tn), F32)],
        compiler_params=_params("arbitrary", "arbitrary", "arbitrary"),
    )(a, b)
    if not col_sharded:
        out = out.reshape(N_CHIPS, k // N_CHIPS, n)
    return out


def ew(fn, ins, out_dtypes, name, tile_rows=256):
    rows, cols = ins[0].shape
    for a in ins:
        assert a.shape == (rows, cols), (name, a.shape, rows, cols)
    tr = rows if (rows <= tile_rows or rows % tile_rows) else tile_rows
    n_in = len(ins)

    def body(*refs):
        outs = fn(*[r[...] for r in refs[:n_in]])
        for o_ref, val in zip(refs[n_in:], outs):
            o_ref[...] = val.astype(o_ref.dtype)

    spec = pl.BlockSpec((tr, cols), lambda i: (i, 0))
    return pl.pallas_call(
        body, name=name, grid=(rows // tr,),
        in_specs=[spec] * n_in, out_specs=[spec] * len(out_dtypes),
        out_shape=[_sds((rows, cols), dt) for dt in out_dtypes],
        compiler_params=_params("arbitrary"),
    )(*ins)


def adamw(w, g, m, v, name):
    shape = w.shape
    cols = shape[-1]
    two_d = lambda a: a.reshape(-1, cols)

    def fn(wv, gv, mv, vv):
        m_new = ADAM_B1 * mv + (1.0 - ADAM_B1) * gv
        v_new = ADAM_B2 * vv + (1.0 - ADAM_B2) * (gv * gv)
        m_hat = m_new / (1.0 - ADAM_B1 ** ADAM_STEP)
        v_hat = v_new / (1.0 - ADAM_B2 ** ADAM_STEP)
        delta = -ADAM_LR * (m_hat / (jnp.sqrt(v_hat) + ADAM_EPS) + ADAM_WD * wv)
        return delta, m_new, v_new

    d, mn, vn = ew(fn, [two_d(w), two_d(g), two_d(m), two_d(v)], [F32, F32, F32], name)
    return d.reshape(shape), mn.reshape(shape), vn.reshape(shape)


def rope_tables(s_len):
    def angles(pos, dim):
        freqs = ROPE_THETA ** (-jnp.arange(0, dim, 2, dtype=F32) / dim)
        ang = pos.astype(F32)[:, None] * freqs[None, :]
        return jnp.cos(ang), jnp.sin(ang)

    pos = jnp.arange(s_len)
    rows = s_len // GRID_W
    row_idx = jnp.repeat(jnp.arange(rows), GRID_W)
    col_idx = jnp.tile(jnp.arange(GRID_W), rows)
    c1, s1 = angles(pos, HEAD_DIM)
    cr, sr = angles(row_idx, HEAD_DIM // 2)
    cc, sc = angles(col_idx, HEAD_DIM // 2)
    cos1 = jnp.tile(jnp.concatenate([c1, c1], -1), (1, 2))
    sin1 = jnp.tile(jnp.concatenate([-s1, s1], -1), (1, 2))
    cos2 = jnp.tile(jnp.concatenate([cr, cr, cc, cc], -1), (1, 2))
    sin2 = jnp.tile(jnp.concatenate([-sr, sr, -sc, sc], -1), (1, 2))
    return cos1, sin1, cos2, sin2


def _lane_iota(rows):
    return lax.broadcasted_iota(jnp.int32, (rows, LANES), 1)


def _swap(x, dist, lane):
    return jnp.where((lane & dist) != 0, pltpu.roll(x, dist, 1), pltpu.roll(x, LANES - dist, 1))


def _head_sum(t, lane):
    for dist in (32, 16, 8, 4, 2, 1):
        t = t + _swap(t, dist, lane)
    return t


Q_SCALE = HEAD_DIM ** -0.5
CHUNK_KIND = ["qa"] * 4 + ["ka", "va"] + ["qb"] * 4 + ["kb", "vb"]
QA_COL, KA_COL, QB_COL, KB_COL = 0, 512, 768, 1280


def prep_fwd(proj, tabs, qn_g, kn_g, name):
    s_len, width = proj.shape
    ts = _row_tile(s_len, 512)
    cos1, sin1, cos2, sin2 = tabs

    def body(p_ref, c1_ref, s1_ref, c2_ref, s2_ref, qg_ref, kg_ref, o_ref):
        lane = _lane_iota(ts)
        c1, s1, c2, s2 = c1_ref[...], s1_ref[...], c2_ref[...], s2_ref[...]
        for cb, kind in enumerate(CHUNK_KIND):
            x = p_ref[:, cb * LANES:(cb + 1) * LANES]
            if kind in ("qa", "ka"):
                y = x * c1 + _swap(x, 32, lane) * s1
            elif kind in ("qb", "kb"):
                gain = qg_ref[...] if kind == "qb" else kg_ref[...]
                ms = _head_sum(x * x, lane) * (1.0 / HEAD_DIM)
                xn = (x * lax.rsqrt(ms + EPS)) * gain
                y = xn * c2 + _swap(xn, 16, lane) * s2
            else:
                y = x
            if kind in ("qa", "qb"):
                y = y * Q_SCALE
            o_ref[:, cb * LANES:(cb + 1) * LANES] = y.astype(BF16)

    tab = pl.BlockSpec((ts, LANES), lambda i: (i, 0))
    vec = pl.BlockSpec((1, LANES), lambda i: (0, 0))
    return pl.pallas_call(
        body, name=name, grid=(s_len // ts,),
        in_specs=[pl.BlockSpec((ts, width), lambda i: (i, 0)), tab, tab, tab, tab, vec, vec],
        out_specs=pl.BlockSpec((ts, width), lambda i: (i, 0)),
        out_shape=_sds((s_len, width), BF16),
        compiler_params=_params("arbitrary"),
    )(proj, cos1, sin1, cos2, sin2, qn_g, kn_g)


def prep_bwd(proj, dqa, dka, dva, dqb, dkb, dvb, tabs, qn_g, kn_g, name):
    s_len, width = proj.shape
    ts = _row_tile(s_len, 256)
    cos1, sin1, cos2, sin2 = tabs

    def body(p_ref, dqa_ref, dka_ref, dva_ref, dqb_ref, dkb_ref, dvb_ref,
             c1_ref, s1_ref, c2_ref, s2_ref, qg_ref, kg_ref, o_ref, dqg_ref, dkg_ref):
        i = pl.program_id(0)
        lane = _lane_iota(ts)
        c1, s1, c2, s2 = c1_ref[...], s1_ref[...], c2_ref[...], s2_ref[...]

        def rope_t(dy, cos, sin, dist):
            return dy * cos + _swap(dy * sin, dist, lane)

        def norm_bwd(dy, x, gain):
            r = lax.rsqrt(_head_sum(x * x, lane) * (1.0 / HEAD_DIM) + EPS)
            xhat = x * r
            dgain = jnp.sum(dy * xhat, axis=0, keepdims=True)
            dxh = dy * gain
            dx = r * (dxh - xhat * (_head_sum(dxh * xhat, lane) * (1.0 / HEAD_DIM)))
            return dx, dgain

        dqg = jnp.zeros((1, LANES), F32)
        dkg = jnp.zeros((1, LANES), F32)
        for cb, kind in enumerate(CHUNK_KIND):
            cols = slice(cb * LANES, (cb + 1) * LANES)
            if kind == "qa":
                dx = rope_t(dqa_ref[:, cols] * Q_SCALE, c1, s1, 32)
            elif kind == "ka":
                dx = rope_t(dka_ref[...], c1, s1, 32)
            elif kind == "va":
                dx = dva_ref[...]
            elif kind == "qb":
                qcols = slice((cb - 6) * LANES, (cb - 5) * LANES)
                dy = rope_t(dqb_ref[:, qcols] * Q_SCALE, c2, s2, 16)
                dx, dgain = norm_bwd(dy, p_ref[:, cols], qg_ref[...])
                dqg = dqg + dgain
            elif kind == "kb":
                dy = rope_t(dkb_ref[...], c2, s2, 16)
                dx, dgain = norm_bwd(dy, p_ref[:, cols], kg_ref[...])
                dkg = dkg + dgain
            else:
                dx = dvb_ref[...]
            o_ref[:, cols] = dx.astype(BF16)

        @pl.when(i == 0)
        def _():
            dqg_ref[...] = dqg
            dkg_ref[...] = dkg

        @pl.when(i > 0)
        def _():
            dqg_ref[...] += dqg
            dkg_ref[...] += dkg

    tab = pl.BlockSpec((ts, LANES), lambda i: (i, 0))
    vec = pl.BlockSpec((1, LANES), lambda i: (0, 0))
    dq_spec = pl.BlockSpec((ts, 4 * LANES), lambda i: (i, 0))
    return pl.pallas_call(
        body, name=name, grid=(s_len // ts,),
        in_specs=([pl.BlockSpec((ts, width), lambda i: (i, 0)), dq_spec, tab, tab, dq_spec, tab, tab]
                  + [tab] * 4 + [vec, vec]),
        out_specs=[pl.BlockSpec((ts, width), lambda i: (i, 0)), vec, vec],
        out_shape=[_sds((s_len, width), BF16), _sds((1, LANES), F32), _sds((1, LANES), F32)],
        compiler_params=_params("arbitrary"),
    )(proj, dqa, dka, dva, dqb, dkb, dvb, cos1, sin1, cos2, sin2, qn_g, kn_g)


NEG = -1e30
GROUP = 4
KV_HEADS = 2
GROUP_W = GROUP * HEAD_DIM
LSE_ROWS = 8


def _pos_mask_t(k_start, q_start, s_len, tk, tq):
    kpos = k_start + lax.broadcasted_iota(jnp.int32, (tk, tq), 0)
    qpos = q_start + lax.broadcasted_iota(jnp.int32, (tk, tq), 1)
    return (jnp.abs(kpos - qpos) <= BLOCK) & (kpos >= 0) & (kpos < s_len)


def flash_fwd_t(qkv_t, kv_tok, q_rb, k_i, v_rb, sink, window, name):
    s_len = qkv_t.shape[1]
    tq = _row_tile(s_len, 512)
    if window:
        tk = 256
        per = tq // tk
        n_kv = per + 2
    else:
        tk = _row_tile(s_len, 512)
        n_kv = s_len // tk
    n_kb = s_len // tk

    def body(*refs):
        if window:
            sink_ref, q_ref, k_ref, v_ref, o_ref, lse_ref, m_sc, l_sc, acc_sc = refs
        else:
            q_ref, k_ref, v_ref, o_ref, lse_ref, m_sc, l_sc, acc_sc = refs
        h, i, t = pl.program_id(0), pl.program_id(1), pl.program_id(2)

        @pl.when(t == 0)
        def _():
            for g in range(GROUP):
                if window:
                    m_sc[g] = jnp.full((1, tq), sink_ref[h * GROUP + g], F32)
                    l_sc[g] = jnp.ones((1, tq), F32)
                else:
                    m_sc[g] = jnp.full((1, tq), NEG, F32)
                    l_sc[g] = jnp.zeros((1, tq), F32)
                acc_sc[g] = jnp.zeros((HEAD_DIM, tq), F32)

        k = k_ref[...]
        v_t = v_ref[...]
        if window:
            mask = _pos_mask_t((i * per - 1 + t) * tk, i * tq, s_len, tk, tq)
        for g in range(GROUP):
            q_t = q_ref[g * HEAD_DIM:(g + 1) * HEAD_DIM, :]
            s_t = jnp.dot(k, q_t, preferred_element_type=F32)
            if window:
                s_t = jnp.where(mask, s_t, NEG)
            m_prev = m_sc[g]
            m_new = jnp.maximum(m_prev, jnp.max(s_t, axis=0, keepdims=True))
            alpha = jnp.exp(m_prev - m_new)
            p_t = jnp.exp(s_t - m_new)
            l_sc[g] = alpha * l_sc[g] + jnp.sum(p_t, axis=0, keepdims=True)
            acc_sc[g] = alpha * acc_sc[g] + jnp.dot(v_t, p_t.astype(BF16), preferred_element_type=F32)
            m_sc[g] = m_new

        @pl.when(t == n_kv - 1)
        def _():
            for g in range(GROUP):
                l = l_sc[g]
                o_ref[g * HEAD_DIM:(g + 1) * HEAD_DIM, :] = (acc_sc[g] / l).astype(BF16)
                lse_ref[g * LSE_ROWS:(g + 1) * LSE_ROWS, :] = jnp.broadcast_to(m_sc[g] + jnp.log(l), (LSE_ROWS, tq))

    if window:
        kv_blk = lambda i, t: jnp.clip(i * per - 1 + t, 0, n_kb - 1)
    else:
        kv_blk = lambda i, t: t
    in_specs = [pl.BlockSpec((GROUP_W, tq), lambda h, i, t: (q_rb + h, i)),
                pl.BlockSpec((None, tk, HEAD_DIM), lambda h, i, t: (k_i + h, kv_blk(i, t), 0)),
                pl.BlockSpec((HEAD_DIM, tk), lambda h, i, t: (v_rb + h, kv_blk(i, t)))]
    args = [qkv_t, kv_tok, qkv_t]
    if window:
        in_specs = [pl.BlockSpec(memory_space=pltpu.SMEM)] + in_specs
        args = [sink] + args
    return pl.pallas_call(
        body, name=name, grid=(KV_HEADS, s_len // tq, n_kv),
        in_specs=in_specs,
        out_specs=[pl.BlockSpec((GROUP_W, tq), lambda h, i, t: (h, i)),
                   pl.BlockSpec((GROUP * LSE_ROWS, tq), lambda h, i, t: (h, i))],
        out_shape=[_sds((KV_HEADS * GROUP_W, s_len), BF16), _sds((KV_HEADS * GROUP * LSE_ROWS, s_len), F32)],
        scratch_shapes=[pltpu.VMEM((GROUP, 1, tq), F32), pltpu.VMEM((GROUP, 1, tq), F32),
                        pltpu.VMEM((GROUP, HEAD_DIM, tq), F32)],
        compiler_params=_params("arbitrary", "arbitrary", "arbitrary"),
    )(*args)


def flash_bwd_t(qkv_t, kv_tok, o_t, do_t, lse, q_rb, k_i, v_i, k_rb, o_rb, sink, window, name):
    s_len = qkv_t.shape[1]
    tq = _row_tile(s_len, 512)
    if window:
        tk = 256
        n_q = 2
    else:
        tk = _row_tile(s_len, 512)
        n_q = s_len // tq
    n_qb = s_len // tq

    def body(*refs):
        if window:
            (sink_ref, q_ref, k_ref, v_ref, kt_ref, o_ref, do_ref, lse_ref,
             dq_ref, dk_ref, dv_ref, dsink_ref, dk_sc, dv_sc) = refs
        else:
            (q_ref, k_ref, v_ref, kt_ref, o_ref, do_ref, lse_ref, dq_ref, dk_ref, dv_ref, dk_sc, dv_sc) = refs
        h, j, t = pl.program_id(0), pl.program_id(1), pl.program_id(2)
        q_blk = (j + 1) // 2 - 1 + t if window else t

        @pl.when((j == 0) & (t == 0))
        def _():
            dq_ref[...] = jnp.zeros(dq_ref.shape, F32)
            if window:
                dsink_ref[...] = jnp.zeros((8, LANES), F32)

        @pl.when(t == 0)
        def _():
            dk_sc[...] = jnp.zeros((tk, HEAD_DIM), F32)
            dv_sc[...] = jnp.zeros((tk, HEAD_DIM), F32)

        def step():
            k, v, k_t = k_ref[...], v_ref[...], kt_ref[...]
            if window:
                mask = _pos_mask_t(j * tk, q_blk * tq, s_len, tk, tq)
                lane = lax.broadcasted_iota(jnp.int32, (8, LANES), 1)
                sink_tile = jnp.zeros((8, LANES), F32)
            dk_acc = dk_sc[...]
            dv_acc = dv_sc[...]
            for g in range(GROUP):
                rows = slice(g * HEAD_DIM, (g + 1) * HEAD_DIM)
                q_t, o_g, do_g = q_ref[rows, :], o_ref[rows, :], do_ref[rows, :]
                s_t = jnp.dot(k, q_t, preferred_element_type=F32)
                if window:
                    s_t = jnp.where(mask, s_t, NEG)
                lse_row = lse_ref[g * LSE_ROWS:g * LSE_ROWS + 1, :]
                p_t = jnp.exp(s_t - lse_row)
                delta = jnp.sum(do_g.astype(F32) * o_g.astype(F32), axis=0, keepdims=True)
                dp_t = jnp.dot(v, do_g, preferred_element_type=F32)
                ds_t = (p_t * (dp_t - delta)).astype(BF16)
                dv_acc = dv_acc + lax.dot_general(p_t.astype(BF16), do_g, NT_DIMS, preferred_element_type=F32)
                dk_acc = dk_acc + lax.dot_general(ds_t, q_t, NT_DIMS, preferred_element_type=F32)
                dq_ref[q_blk, rows, :] += jnp.dot(k_t, ds_t, preferred_element_type=F32)
                if window:
                    p_sink = jnp.exp(sink_ref[h * GROUP + g] - lse_row)
                    term = -jnp.sum(p_sink * delta, axis=1, keepdims=True)
                    sink_tile = jnp.where(lane == g, term, sink_tile)
            dk_sc[...] = dk_acc
            dv_sc[...] = dv_acc
            if window:
                @pl.when((j % 2 == 0) & (t == 1))
                def _():
                    dsink_ref[...] += sink_tile

        if window:
            pl.when((q_blk >= 0) & (q_blk < n_qb))(step)
        else:
            step()

        @pl.when(t == n_q - 1)
        def _():
            dk_ref[...] = dk_sc[...]
            dv_ref[...] = dv_sc[...]

    if window:
        qb = lambda j, t: jnp.clip((j + 1) // 2 - 1 + t, 0, n_qb - 1)
    else:
        qb = lambda j, t: t
    in_specs = [pl.BlockSpec((GROUP_W, tq), lambda h, j, t: (q_rb + h, qb(j, t))),
                pl.BlockSpec((None, tk, HEAD_DIM), lambda h, j, t: (k_i + h, j, 0)),
                pl.BlockSpec((None, tk, HEAD_DIM), lambda h, j, t: (v_i + h, j, 0)),
                pl.BlockSpec((HEAD_DIM, tk), lambda h, j, t: (k_rb + h, j)),
                pl.BlockSpec((GROUP_W, tq), lambda h, j, t: (o_rb + h, qb(j, t))),
                pl.BlockSpec((GROUP_W, tq), lambda h, j, t: (o_rb + h, qb(j, t))),
                pl.BlockSpec((GROUP * LSE_ROWS, tq), lambda h, j, t: (h, qb(j, t)))]
    args = [qkv_t, kv_tok, kv_tok, qkv_t, o_t, do_t, lse]
    kv_out = _sds((KV_HEADS, s_len, HEAD_DIM), F32)
    out_specs = [pl.BlockSpec((n_qb, GROUP_W, tq), lambda h, j, t: (0, h, 0)),
                 pl.BlockSpec((None, tk, HEAD_DIM), lambda h, j, t: (h, j, 0)),
                 pl.BlockSpec((None, tk, HEAD_DIM), lambda h, j, t: (h, j, 0))]
    out_shape = [_sds((n_qb, KV_HEADS * GROUP_W, tq), F32), kv_out, kv_out]
    if window:
        in_specs = [pl.BlockSpec(memory_space=pltpu.SMEM)] + in_specs
        args = [sink] + args
        out_specs.append(pl.BlockSpec((None, 8, LANES), lambda h, j, t: (h, 0, 0)))
        out_shape.append(_sds((KV_HEADS, 8, LANES), F32))
    return pl.pallas_call(
        body, name=name, grid=(KV_HEADS, s_len // tk, n_q),
        in_specs=in_specs, out_specs=out_specs, out_shape=out_shape,
        scratch_shapes=[pltpu.VMEM((tk, HEAD_DIM), F32), pltpu.VMEM((tk, HEAD_DIM), F32)],
        compiler_params=_params("arbitrary", "arbitrary", "arbitrary"),
    )(*args)


SGU_GROUPS = 8
SGU_CHUNK = 128
GELU_C = float(np.sqrt(2.0 / np.pi))
GELU_A = 0.044715


def _gelu(x):
    return x * (0.5 * (1.0 + jnp.tanh(GELU_C * (x + GELU_A * (x * x * x)))))


def _gelu_grad(x):
    t = jnp.tanh(GELU_C * (x + GELU_A * (x * x * x)))
    return 0.5 * (1.0 + t) + 0.5 * x * (1.0 - t * t) * (GELU_C * (1.0 + 3.0 * GELU_A * (x * x)))


def _layernorm_stats(v):
    mu = jnp.mean(v, axis=-1, keepdims=True)
    var = jnp.mean(jnp.square(v - mu), axis=-1, keepdims=True)
    rstd = lax.rsqrt(var + EPS)
    return (v - mu) * rstd, rstd


def sgu_mid_fwd(zpre, ln_g, ln_b, ws, bsb, name):
    s_len, width = zpre.shape
    d = width // 2
    ts = _row_tile(s_len, 256)

    def body(z_ref, g_ref, b_ref, ws_ref, bs_ref, y_ref):
        z = _gelu(z_ref[...])
        u, v = z[:, :d], z[:, d:]
        vhat, _ = _layernorm_stats(v)
        vn = (vhat * g_ref[...] + b_ref[...]).astype(BF16)
        for n in range(ts // SGU_CHUNK):
            rows = slice(n * SGU_CHUNK, (n + 1) * SGU_CHUNK)
            for g in range(SGU_GROUPS):
                cols = slice(g * LANES, (g + 1) * LANES)
                mixed = jnp.dot(ws_ref[g], vn[rows, cols], preferred_element_type=F32) + bs_ref[g]
                y_ref[rows, cols] = (u[rows, cols] * mixed).astype(BF16)

    vec = pl.BlockSpec((1, d), lambda i: (0, 0))
    cube = pl.BlockSpec((SGU_GROUPS, SGU_CHUNK, SGU_CHUNK), lambda i: (0, 0, 0))
    return pl.pallas_call(
        body, name=name, grid=(s_len // ts,),
        in_specs=[pl.BlockSpec((ts, width), lambda i: (i, 0)), vec, vec, cube, cube],
        out_specs=pl.BlockSpec((ts, d), lambda i: (i, 0)),
        out_shape=_sds((s_len, d), BF16),
        compiler_params=_params("arbitrary"),
    )(zpre, ln_g, ln_b, ws, bsb)


def sgu_mid_bwd(zpre, dy, ln_g, ln_b, ws, wst, bsb, name):
    s_len, width = zpre.shape
    d = width // 2
    ts = _row_tile(s_len, 256)
    n_steps = s_len // ts

    def body(z_ref, dy_ref, g_ref, b_ref, ws_ref, wst_ref, bs_ref,
             dz_ref, dws_ref, dbs_ref, dg_ref, db_ref, du_sc, dvn_sc):
        i = pl.program_id(0)

        @pl.when(i == 0)
        def _():
            dws_ref[...] = jnp.zeros(dws_ref.shape, F32)
            dbs_ref[...] = jnp.zeros(dbs_ref.shape, F32)
            dg_ref[...] = jnp.zeros(dg_ref.shape, F32)
            db_ref[...] = jnp.zeros(db_ref.shape, F32)

        zp = z_ref[...]
        z = _gelu(zp)
        u, v = z[:, :d], z[:, d:]
        vhat, rstd = _layernorm_stats(v)
        gain = g_ref[...]
        vn = (vhat * gain + b_ref[...]).astype(BF16)
        dyf = dy_ref[...].astype(F32)
        for n in range(ts // SGU_CHUNK):
            rows = slice(n * SGU_CHUNK, (n + 1) * SGU_CHUNK)
            for g in range(SGU_GROUPS):
                cols = slice(g * LANES, (g + 1) * LANES)
                vt = vn[rows, cols]
                mixed = jnp.dot(ws_ref[g], vt, preferred_element_type=F32) + bs_ref[g]
                dyt = dyf[rows, cols]
                du_sc[rows, cols] = dyt * mixed
                dmixed = dyt * u[rows, cols]
                dmb = dmixed.astype(BF16)
                dvn_sc[rows, cols] = jnp.dot(wst_ref[g], dmb, preferred_element_type=F32)
                dws_ref[g] += lax.dot_general(dmb, vt, NT_DIMS, preferred_element_type=F32)
                dbs_ref[g] += dmixed
        dvn = dvn_sc[...]
        dg_ref[...] += jnp.sum(dvn * vhat, axis=0, keepdims=True)
        db_ref[...] += jnp.sum(dvn, axis=0, keepdims=True)
        dvh = dvn * gain
        dv = rstd * (dvh - jnp.mean(dvh, axis=-1, keepdims=True)
                     - vhat * jnp.mean(dvh * vhat, axis=-1, keepdims=True))
        gp = _gelu_grad(zp)
        dz_ref[:, :d] = (du_sc[...] * gp[:, :d]).astype(BF16)
        dz_ref[:, d:] = (dv * gp[:, d:]).astype(BF16)

        @pl.when(i == n_steps - 1)
        def _():
            for g in range(SGU_GROUPS):
                tot = jnp.sum(dbs_ref[g], axis=1, keepdims=True)
                dbs_ref[g] = jnp.broadcast_to(tot, (SGU_CHUNK, LANES))

    vec = pl.BlockSpec((1, d), lambda i: (0, 0))
    cube = pl.BlockSpec((SGU_GROUPS, SGU_CHUNK, SGU_CHUNK), lambda i: (0, 0, 0))
    cube_shape = _sds((SGU_GROUPS, SGU_CHUNK, SGU_CHUNK), F32)
    return pl.pallas_call(
        body, name=name, grid=(n_steps,),
        in_specs=[pl.BlockSpec((ts, width), lambda i: (i, 0)), pl.BlockSpec((ts, d), lambda i: (i, 0)),
                  vec, vec, cube, cube, cube],
        out_specs=[pl.BlockSpec((ts, width), lambda i: (i, 0)), cube, cube, vec, vec],
        out_shape=[_sds((s_len, width), BF16), cube_shape, cube_shape, _sds((1, d), F32), _sds((1, d), F32)],
        scratch_shapes=[pltpu.VMEM((ts, d), F32), pltpu.VMEM((ts, d), F32)],
        compiler_params=_params("arbitrary"),
    )(zpre, dy, ln_g, ln_b, ws, wst, bsb)


def loss_head(x, g, target, name):
    s_len, d = x.shape
    tm = _row_tile(s_len, 512)

    def body(x_ref, g_ref, t_ref, dx_ref, dxb_ref, dg_ref, loss_ref):
        i = pl.program_id(0)
        xf = x_ref[...]
        gain = g_ref[...]
        r = lax.rsqrt(jnp.mean(xf * xf, axis=-1, keepdims=True) + EPS)
        xhat = xf * r
        err = xhat * gain - t_ref[...]
        row = jnp.mean(err * err, axis=-1, keepdims=True)
        part = 0.5 * jnp.sum(row, axis=0, keepdims=True)
        dy = err * (1.0 / d)
        dg_part = jnp.sum(dy * xhat, axis=0, keepdims=True)

        @pl.when(i == 0)
        def _():
            dg_ref[...] = dg_part
            loss_ref[...] = jnp.broadcast_to(part, (8, LANES))

        @pl.when(i > 0)
        def _():
            dg_ref[...] += dg_part
            loss_ref[...] += jnp.broadcast_to(part, (8, LANES))

        dxh = dy * gain
        dx = r * (dxh - xhat * jnp.mean(dxh * xhat, axis=-1, keepdims=True))
        dx_ref[...] = dx
        dxb_ref[...] = dx.astype(BF16)

    row_spec = pl.BlockSpec((tm, d), lambda i: (i, 0))
    vec = pl.BlockSpec((1, d), lambda i: (0, 0))
    return pl.pallas_call(
        body, name=name, grid=(s_len // tm,),
        in_specs=[row_spec, vec, row_spec],
        out_specs=[row_spec, row_spec, vec, pl.BlockSpec((8, LANES), lambda i: (0, 0))],
        out_shape=[_sds((s_len, d), F32), _sds((s_len, d), BF16), _sds((1, d), F32), _sds((8, LANES), F32)],
        compiler_params=_params("arbitrary"),
    )(x, g, target)


FLIP_BITS = {"c": (0, 0, 1), "x": (1, 0, 0), "y": (0, 1, 0), "xy": (1, 1, 0),
             "xc": (1, 0, 1), "yc": (0, 1, 1), "xyc": (1, 1, 1)}
CHIP_FLIPS = ("x", "y", "xy")


def _flip(pos, name):
    return tuple(1 - p if bit else p for p, bit in zip(pos, FLIP_BITS[name]))


def _chip(pos):
    return 2 * pos[0] + pos[1]


def exchange(name, ins, out_shapes, remote, local):
    n_in, n_out = len(ins), len(out_shapes)

    def body(*refs):
        in_refs, out_refs = refs[:n_in], refs[n_in:n_in + n_out]
        send_sems, recv_sems, local_sems = refs[n_in + n_out:]
        me = (lax.axis_index("x"), lax.axis_index("y"), lax.axis_index("c"))

        def copy(k, sender, receiver):
            ii, src_fn, oi, dst_fn, flip = remote[k]
            return pltpu.make_async_remote_copy(
                src_ref=src_fn(in_refs[ii], sender, receiver), dst_ref=dst_fn(out_refs[oi], sender),
                send_sem=send_sems.at[k], recv_sem=recv_sems.at[k],
                device_id=receiver, device_id_type=MESH)

        stays = []
        for k, (ii, src_fn, oi, dst_fn) in enumerate(local):
            cp = pltpu.make_async_copy(src_fn(in_refs[ii], me), dst_fn(out_refs[oi], me), local_sems.at[k])
            cp.start()
            stays.append(cp)
        sends = []
        for k in range(len(remote)):
            cp = copy(k, me, _flip(me, remote[k][4]))
            cp.start()
            sends.append(cp)
        for k in range(len(remote)):
            copy(k, _flip(me, remote[k][4]), me).wait_recv()
        for cp in sends:
            cp.wait_send()
        for cp in stays:
            cp.wait()

    hbm = pl.BlockSpec(memory_space=pl.ANY)
    return pl.pallas_call(
        body, name=name,
        in_specs=[hbm] * n_in, out_specs=[hbm] * n_out, out_shape=list(out_shapes),
        scratch_shapes=[pltpu.SemaphoreType.DMA((max(len(remote), 1),)),
                        pltpu.SemaphoreType.DMA((max(len(remote), 1),)),
                        pltpu.SemaphoreType.DMA((max(len(local), 1),))],
        compiler_params=pltpu.CompilerParams(has_side_effects=True),
    )(*ins)


def staged_push(name, ins, out_shapes, jobs, n_alias=0):
    n_in, n_out = len(ins), len(out_shapes)
    n_copies = sum(len(dsts) for _, _, dsts in jobs)
    n_remote = sum(1 for _, _, dsts in jobs for d in dsts if d[2] is not None)

    def chunk_of(ii, src_fn):
        probe = _ShapeRef(ins[ii].shape, ins[ii].dtype)
        got = src_fn(probe, (0, 0, 0))
        return tuple(got.shape), got.dtype

    classes = []
    for ii, src_fn, _ in jobs:
        c = chunk_of(ii, src_fn)
        if c not in classes:
            classes.append(c)

    def body(*refs):
        in_refs, out_refs = refs[:n_in], refs[n_in:n_in + n_out]
        bufs = refs[n_in + n_out:n_in + n_out + len(classes)]
        load_sem, out_sems, recv_sems = refs[n_in + n_out + len(classes):]
        me = (lax.axis_index("x"), lax.axis_index("y"), lax.axis_index("c"))
        pending = [[[], []] for _ in classes]
        used = [0] * len(classes)
        arrivals = []
        k = r = 0
        for ii, src_fn, dsts in jobs:
            cls = classes.index(chunk_of(ii, src_fn))
            slot = used[cls] % 2
            used[cls] += 1
            for kind, cp in pending[cls][slot]:
                cp.wait_send() if kind == "remote" else cp.wait()
            buf = bufs[cls].at[slot]
            load = pltpu.make_async_copy(src_fn(in_refs[ii], me), buf, load_sem.at[0])
            load.start()
            load.wait()
            sent = []
            for oi, dst_fn, flip in dsts:
                if flip is None:
                    cp = pltpu.make_async_copy(buf, dst_fn(out_refs[oi], me), out_sems.at[k])
                    cp.start()
                    sent.append(("local", cp))
                else:
                    peer = _flip(me, flip)
                    cp = pltpu.make_async_remote_copy(
                        src_ref=buf, dst_ref=dst_fn(out_refs[oi], me), send_sem=out_sems.at[k],
                        recv_sem=recv_sems.at[r], device_id=peer, device_id_type=MESH)
                    cp.start()
                    sent.append(("remote", cp))
                    arrivals.append((r, cls, oi, dst_fn, peer))
                    r += 1
                k += 1
            pending[cls][slot] = sent
        for per_class in pending:
            for slot_list in per_class:
                for kind, cp in slot_list:
                    cp.wait_send() if kind == "remote" else cp.wait()
        for r, cls, oi, dst_fn, peer in arrivals:
            pltpu.make_async_remote_copy(
                src_ref=bufs[cls].at[0], dst_ref=dst_fn(out_refs[oi], peer), send_sem=out_sems.at[0],
                recv_sem=recv_sems.at[r], device_id=peer, device_id_type=MESH).wait_recv()

    hbm = pl.BlockSpec(memory_space=pl.ANY)
    return pl.pallas_call(
        body, name=name,
        in_specs=[hbm] * n_in, out_specs=[hbm] * n_out, out_shape=list(out_shapes),
        scratch_shapes=[pltpu.VMEM((2,) + shape, dtype) for shape, dtype in classes]
        + [pltpu.SemaphoreType.DMA((1,)), pltpu.SemaphoreType.DMA((max(n_copies, 1),)),
           pltpu.SemaphoreType.DMA((max(n_remote, 1),))],
        input_output_aliases={i: i for i in range(n_alias)},
        compiler_params=pltpu.CompilerParams(has_side_effects=True, vmem_limit_bytes=VMEM_LIMIT),
    )(*ins)


class _ShapeRef:
    def __init__(self, shape, dtype):
        self.shape, self.dtype = tuple(shape), dtype

    @property
    def at(self):
        return self

    def __getitem__(self, idx):
        idx = idx if isinstance(idx, tuple) else (idx,)
        shape = []
        for dim, i in zip(self.shape, idx):
            if isinstance(i, slice):
                shape.append(len(range(*i.indices(dim))))
            elif hasattr(i, "size") and hasattr(i, "start"):
                shape.append(i.size)
        shape += self.shape[len(idx):]
        return _ShapeRef(shape, self.dtype)


def gather_weights(shards, vecs, name):
    n_t = len(shards)
    remote, local = [], []
    for t in range(n_t):
        half = lambda ref, sender, receiver=None: _half(ref, sender[2], 0)
        slot = lambda ref, sender: _half(ref.at[_chip(sender)], sender[2], 0)
        for flip in CHIP_FLIPS:
            remote.append((t, half, t, slot, flip))
    whole = lambda ref, sender, receiver=None: ref
    vslot = lambda ref, sender: ref.at[_chip(sender)]
    local.append((n_t, whole, n_t, vslot))
    for flip in CHIP_FLIPS:
        remote.append((n_t, whole, n_t, vslot, flip))
    outs = [_sds((N_CHIPS,) + a.shape, a.dtype) for a in list(shards) + [vecs]]
    got = exchange(name + "_ici", list(shards) + [vecs], outs, remote, local)
    jobs = []
    for t, a in enumerate(shards):
        layers = a.shape[0]
        for l in range(layers):
            jobs.append((n_t + t, lambda ref, me, l=l: ref.at[l],
                         [(t, lambda ref, sender, l=l: ref.at[_chip(sender), l], None)]))
        for flip in CHIP_FLIPS:
            for j in range(layers // 2):
                src = lambda ref, me, flip=flip, j=j, n=layers // 2: ref.at[_chip(_flip(me, flip)), me[2] * n + j]
                dst = lambda ref, sender, flip=flip, j=j, n=layers // 2: ref.at[_chip(_flip(sender, flip)), sender[2] * n + j]
                jobs.append((t, src, [(t, dst, "c")]))
    full = staged_push(name + "_fill", list(got[:n_t]) + list(shards), outs[:n_t], jobs, n_alias=n_t)
    return list(full), got[n_t]


def gather_all(buf, name):
    whole = lambda ref, sender, receiver=None: ref
    slot = lambda ref, sender: ref.at[4 * sender[0] + 2 * sender[1] + sender[2]]
    remote = [(0, whole, 0, slot, flip) for flip in FLIP_BITS]
    local = [(0, whole, 0, slot)]
    return exchange(name, [buf], [_sds((8,) + buf.shape, buf.dtype)], remote, local)[0]


def _half(ref, core, axis):
    rows = ref.shape[axis] // 2
    idx = (slice(None),) * axis + (pl.ds(core * rows, rows),)
    return ref.at[idx]


def reduce_to_owner_chips(grads, stacks, name):
    n_t = len(grads)
    core = lax.axis_index("c").astype(jnp.int32).reshape(1)
    chip = (2 * lax.axis_index("x") + lax.axis_index("y")).astype(jnp.int32).reshape(1)
    jobs, outs = [], []
    for t, g in enumerate(grads):
        outs.append(_sds((N_CHIPS, g.shape[1] // 2, g.shape[2]), BF16))
        for s in range(N_CHIPS):
            jobs.append((t, lambda ref, me, s=s: _half(ref.at[s], 1 - me[2], 0),
                         [(t, lambda ref, sender, s=s: ref.at[s], "c")]))
    theirs = staged_push(name + "_swap", grads, outs, jobs)
    chip_sums = [add_half(g, r, core, f"{name}_add{t}") for t, (g, r) in enumerate(zip(grads, theirs))]
    remote, outs = [], []
    for t, p in enumerate(chip_sums):
        outs.append(_sds((len(CHIP_FLIPS),) + p.shape[1:], BF16))
        for f, flip in enumerate(CHIP_FLIPS):
            remote.append((t, lambda ref, sender, receiver: ref.at[_chip(receiver)],
                           t, lambda ref, sender, f=f: ref.at[f], flip))
    got = exchange(name + "_scatter", chip_sums, outs, remote, [])
    totals = [sum_chips(p, r, chip, f"{name}_sum{t}") for t, (p, r) in enumerate(zip(chip_sums, got))]
    names = []
    for out_name, _ in stacks:
        if out_name not in names:
            names.append(out_name)
    outs = []
    for out_name in names:
        layers = 1 + max(l for n, l in stacks if n == out_name)
        t0 = [t for t, (n, _) in enumerate(stacks) if n == out_name][0]
        outs.append(_sds((layers, 2 * totals[t0].shape[0], totals[t0].shape[1]), F32))
    jobs = []
    for t, (out_name, layer) in enumerate(stacks):
        oi = names.index(out_name)
        rows, cols = totals[t].shape
        pieces = max(1, rows * cols * 4 // STAGE_BYTES)
        step = rows // pieces
        for q in range(pieces):
            src = lambda ref, me, q=q, step=step: ref.at[pl.ds(q * step, step)]
            place = lambda ref, sender, layer=layer, q=q, step=step, rows=rows: ref.at[
                layer, pl.ds(sender[2] * rows + q * step, step)]
            jobs.append((t, src, [(oi, place, None), (oi, place, "c")]))
    full = staged_push(name + "_share", totals, outs, jobs)
    return dict(zip(names, full))


STAGE_BYTES = 1024 * 1024


def add_half(g, theirs, core, name):
    n_s, rows, cols = g.shape
    half = rows // 2
    tr = _row_tile(half, 256)
    nb = half // tr

    def body(core_ref, g_ref, t_ref, o_ref):
        o_ref[...] = (g_ref[...].astype(F32) + t_ref[...].astype(F32)).astype(BF16)

    return pl.pallas_call(
        body, name=name,
        grid_spec=pltpu.PrefetchScalarGridSpec(
            num_scalar_prefetch=1, grid=(n_s, nb),
            in_specs=[pl.BlockSpec((None, tr, cols), lambda s, i, c: (s, c[0] * nb + i, 0)),
                      pl.BlockSpec((None, tr, cols), lambda s, i, c: (s, i, 0))],
            out_specs=pl.BlockSpec((None, tr, cols), lambda s, i, c: (s, i, 0))),
        out_shape=_sds((n_s, half, cols), BF16),
        compiler_params=_params("arbitrary", "arbitrary"),
    )(core, g, theirs)


def sum_chips(mine, theirs, chip, name):
    _, half, cols = mine.shape
    tr = _row_tile(half, 256)

    def body(chip_ref, m_ref, a_ref, b_ref, c_ref, o_ref):
        o_ref[...] = ((m_ref[...].astype(F32) + a_ref[...].astype(F32))
                      + b_ref[...].astype(F32)) + c_ref[...].astype(F32)

    got = lambda f: pl.BlockSpec((None, tr, cols), lambda i, ch: (f, i, 0))
    return pl.pallas_call(
        body, name=name,
        grid_spec=pltpu.PrefetchScalarGridSpec(
            num_scalar_prefetch=1, grid=(half // tr,),
            in_specs=[pl.BlockSpec((None, tr, cols), lambda i, ch: (ch[0], i, 0)), got(0), got(1), got(2)],
            out_specs=pl.BlockSpec((tr, cols), lambda i, ch: (i, 0))),
        out_shape=_sds((half, cols), F32),
        compiler_params=_params("arbitrary"),
    )(chip, mine, theirs, theirs, theirs)


def _tok(t):
    return t.transpose(1, 0, 2).reshape(t.shape[1], t.shape[0] * t.shape[2])


def _heads(t):
    return t.reshape(t.shape[0], t.shape[1] // HEAD_DIM, HEAD_DIM).transpose(1, 0, 2)


def _tile2(vec):
    return jnp.tile(vec.reshape(1, HEAD_DIM), (1, 2))


def local_step(x, target, gw, rep):
    s_len, d = x.shape
    tabs = rope_tables(s_len)
    depth = rep["mlp_norm"].shape[0]
    row = lambda a: a.reshape(1, -1)
    saved = []
    h = x
    for layer in range(depth):
        i = layer // 2
        tag = f"l{layer}"
        if layer % 2 == 0:
            hn, proj = norm_mm(h, row(rep["att_norm"][i]), gw["att_w_in"], i, F32, tag + "_att_proj")
            qkv = prep_fwd(proj, tabs, _tile2(rep["att_qnorm"][i]), _tile2(rep["att_knorm"][i]), tag + "_att_prep")
            qkv_t = qkv.T
            kv_tok = _heads(jnp.concatenate([qkv[:, KA_COL:KA_COL + 2 * LANES], qkv[:, KB_COL:KB_COL + 2 * LANES]], 1))
            oa, lse_a = flash_fwd_t(qkv_t, kv_tok, QA_COL // GROUP_W, 0, (KA_COL + LANES) // HEAD_DIM,
                                    rep["att_sink"][i], True, tag + "_win_fwd")
            ob, lse_b = flash_fwd_t(qkv_t, kv_tok, QB_COL // GROUP_W, 4, (KB_COL + LANES) // HEAD_DIM,
                                    None, False, tag + "_grid_fwd")
            o_t = jnp.concatenate([oa, ob], axis=0)
            ocat = o_t.T
            out = mm_res(ocat, gw["att_w_out"], i, h, tag + "_att_out")
            mix_saved = (h, hn, proj, qkv_t, kv_tok, o_t, ocat, lse_a, lse_b)
        else:
            hn, zpre = norm_mm(h, row(rep["sgu_norm"][i]), gw["sgu_w_in"], i, F32, tag + "_sgu_in")
            ws = rep["sgu_w_s"][i].astype(BF16)
            bsb = jnp.broadcast_to(rep["sgu_b_s"][i][:, :, None], (SGU_GROUPS, SGU_CHUNK, LANES))
            y = sgu_mid_fwd(zpre, row(rep["sgu_ln_g"][i]), row(rep["sgu_ln_b"][i]), ws, bsb, tag + "_sgu_mid")
            out = mm_res(y, gw["sgu_w_out"], i, h, tag + "_sgu_out")
            mix_saved = (h, hn, zpre, y, ws, bsb)
        hm, a = norm_mm(out, row(rep["mlp_norm"][layer]), gw["mlp_w1"], layer, BF16, tag + "_mlp_up")
        nxt = mm_res(a, gw["mlp_w2"], layer, out, tag + "_mlp_down", relu2=True)
        saved.append((mix_saved, (out, hm, a)))
        h = nxt
    dh, dhb, d_final, loss_tile = loss_head(h, row(rep["final_norm"]), target, "loss_head")
    big, tags = [], []
    small = {k: [None] * v.shape[0] for k, v in rep.items() if k != "final_norm"}
    small["final_norm"] = d_final.reshape(-1)
    for layer in reversed(range(depth)):
        i = layer // 2
        tag = f"l{layer}"
        mix_saved, (xin, hm, a) = saved[layer]
        da = mm_nt_relu2_bwd(dhb, gw["mlp_w2"], layer, a, tag + "_mlp_down_bwd")
        big.append(dw_mm(a, dhb, tag + "_mlp_dw2", col_sharded=False, relu2=True))
        tags.append(("mlp_w2", layer))
        big.append(dw_mm(hm, da, tag + "_mlp_dw1", col_sharded=True))
        tags.append(("mlp_w1", layer))
        dh, dhb, dg = dx_norm(da, gw["mlp_w1"], layer, xin, row(rep["mlp_norm"][layer]), dh, tag + "_mlp_up_bwd")
        small["mlp_norm"][layer] = dg.reshape(-1)
        if layer % 2 == 0:
            xin, hn, proj, qkv_t, kv_tok, o_t, ocat, lse_a, lse_b = mix_saved
            docat = mm_nt(dhb, gw["att_w_out"], i, tag + "_att_out_bwd")
            big.append(dw_mm(ocat, dhb, tag + "_att_dwout", col_sharded=False))
            tags.append(("att_w_out", i))
            do_t = docat.T
            dqa, dka, dva, dsink = flash_bwd_t(qkv_t, kv_tok, o_t, do_t, lse_a, QA_COL // GROUP_W, 0, 2,
                                               KA_COL // HEAD_DIM, 0, rep["att_sink"][i], True, tag + "_win_bwd")
            dqb, dkb, dvb = flash_bwd_t(qkv_t, kv_tok, o_t, do_t, lse_b, QB_COL // GROUP_W, 4, 6,
                                        KB_COL // HEAD_DIM, 2, None, False, tag + "_grid_bwd")
            qg, kg = _tile2(rep["att_qnorm"][i]), _tile2(rep["att_knorm"][i])
            q_tok = lambda t: t.transpose(0, 2, 1).reshape(s_len, t.shape[1])
            dproj, dqg, dkg = prep_bwd(proj, q_tok(dqa), _tok(dka), _tok(dva), q_tok(dqb), _tok(dkb), _tok(dvb),
                                       tabs, qg, kg, tag + "_att_prep_bwd")
            big.append(dw_mm(hn, dproj, tag + "_att_dwin", col_sharded=True))
            tags.append(("att_w_in", i))
            dh, dhb, dg = dx_norm(dproj, gw["att_w_in"], i, xin, row(rep["att_norm"][i]), dh, tag + "_att_proj_bwd")
            small["att_norm"][i] = dg.reshape(-1)
            small["att_sink"][i] = dsink[:, 0, :GROUP].reshape(-1)
            small["att_qnorm"][i] = dqg[0, :HEAD_DIM] + dqg[0, HEAD_DIM:]
            small["att_knorm"][i] = dkg[0, :HEAD_DIM] + dkg[0, HEAD_DIM:]
        else:
            xin, hn, zpre, y, ws, bsb = mix_saved
            dy = mm_nt(dhb, gw["sgu_w_out"], i, tag + "_sgu_out_bwd")
            big.append(dw_mm(y, dhb, tag + "_sgu_dwout", col_sharded=False))
            tags.append(("sgu_w_out", i))
            wst = ws.transpose(0, 2, 1)
            dz, dws, dbs, dlg, dlb = sgu_mid_bwd(zpre, dy, row(rep["sgu_ln_g"][i]), row(rep["sgu_ln_b"][i]),
                                                 ws, wst, bsb, tag + "_sgu_mid_bwd")
            big.append(dw_mm(hn, dz, tag + "_sgu_dwin", col_sharded=True))
            tags.append(("sgu_w_in", i))
            dh, dhb, dg = dx_norm(dz, gw["sgu_w_in"], i, xin, row(rep["sgu_norm"][i]), dh, tag + "_sgu_in_bwd")
            small["sgu_norm"][i] = dg.reshape(-1)
            small["sgu_ln_g"][i] = dlg.reshape(-1)
            small["sgu_ln_b"][i] = dlb.reshape(-1)
            small["sgu_w_s"][i] = dws
            small["sgu_b_s"][i] = dbs[:, :, 0]
    small = {k: (v if k == "final_norm" else jnp.stack(v)) for k, v in small.items()}
    return loss_tile, dh, big, tags, small


BIG = ("att_w_in", "att_w_out", "sgu_w_in", "sgu_w_out", "mlp_w1", "mlp_w2")
SHARDED_VEC = ("sgu_norm", "sgu_ln_g", "sgu_ln_b")
REPLICATED = ("att_norm", "att_sink", "att_qnorm", "att_knorm", "sgu_w_s", "sgu_b_s", "mlp_norm", "final_norm")
WEIGHTS = ("att_norm", "att_w_in", "att_sink", "att_qnorm", "att_knorm", "att_w_out", "sgu_norm", "sgu_w_in",
           "sgu_ln_g", "sgu_ln_b", "sgu_w_s", "sgu_b_s", "sgu_w_out", "mlp_norm", "mlp_w1", "mlp_w2", "final_norm")
SMALL = tuple(n for n in WEIGHTS if n not in BIG)
PACK_ALIGN = 8 * LANES


def _pack_small(small, loss_tile):
    parts = [small[n].reshape(-1) for n in SMALL] + [loss_tile[0, :1]]
    flat = jnp.concatenate(parts)
    pad = -flat.shape[0] % PACK_ALIGN
    return jnp.pad(flat, (0, pad)).reshape(-1, LANES)


def _unpack_small(flat2d, shapes):
    flat = flat2d.reshape(-1)
    out, off = {}, 0
    for n in SMALL:
        size = int(np.prod(shapes[n]))
        out[n] = flat[off:off + size].reshape(shapes[n])
        off += size
    return out, flat[off]


def kernel(x, att_norm, att_w_in, att_sink, att_qnorm, att_knorm, att_w_out, sgu_norm, sgu_w_in, sgu_ln_g, sgu_ln_b, sgu_w_s, sgu_b_s, sgu_w_out, mlp_norm, mlp_w1, mlp_w2, final_norm, loss_target, m_att_norm, m_att_w_in, m_att_sink, m_att_qnorm, m_att_knorm, m_att_w_out, m_sgu_norm, m_sgu_w_in, m_sgu_ln_g, m_sgu_ln_b, m_sgu_w_s, m_sgu_b_s, m_sgu_w_out, m_mlp_norm, m_mlp_w1, m_mlp_w2, m_final_norm, v_att_norm, v_att_w_in, v_att_sink, v_att_qnorm, v_att_knorm, v_att_w_out, v_sgu_norm, v_sgu_w_in, v_sgu_ln_g, v_sgu_ln_b, v_sgu_w_s, v_sgu_b_s, v_sgu_w_out, v_mlp_norm, v_mlp_w1, v_mlp_w2, v_final_norm):
    w = dict(att_norm=att_norm, att_w_in=att_w_in, att_sink=att_sink, att_qnorm=att_qnorm, att_knorm=att_knorm,
             att_w_out=att_w_out, sgu_norm=sgu_norm, sgu_w_in=sgu_w_in, sgu_ln_g=sgu_ln_g, sgu_ln_b=sgu_ln_b,
             sgu_w_s=sgu_w_s, sgu_b_s=sgu_b_s, sgu_w_out=sgu_w_out, mlp_norm=mlp_norm, mlp_w1=mlp_w1,
             mlp_w2=mlp_w2, final_norm=final_norm)
    m = dict(att_norm=m_att_norm, att_w_in=m_att_w_in, att_sink=m_att_sink, att_qnorm=m_att_qnorm,
             att_knorm=m_att_knorm, att_w_out=m_att_w_out, sgu_norm=m_sgu_norm, sgu_w_in=m_sgu_w_in,
             sgu_ln_g=m_sgu_ln_g, sgu_ln_b=m_sgu_ln_b, sgu_w_s=m_sgu_w_s, sgu_b_s=m_sgu_b_s,
             sgu_w_out=m_sgu_w_out, mlp_norm=m_mlp_norm, mlp_w1=m_mlp_w1, mlp_w2=m_mlp_w2,
             final_norm=m_final_norm)
    v = dict(att_norm=v_att_norm, att_w_in=v_att_w_in, att_sink=v_att_sink, att_qnorm=v_att_qnorm,
             att_knorm=v_att_knorm, att_w_out=v_att_w_out, sgu_norm=v_sgu_norm, sgu_w_in=v_sgu_w_in,
             sgu_ln_g=v_sgu_ln_g, sgu_ln_b=v_sgu_ln_b, sgu_w_s=v_sgu_w_s, sgu_b_s=v_sgu_b_s,
             sgu_w_out=v_sgu_w_out, mlp_norm=v_mlp_norm, mlp_w1=v_mlp_w1, mlp_w2=v_mlp_w2,
             final_norm=v_final_norm)
    chip = 2 * lax.axis_index("x") + lax.axis_index("y")

    vecs = jnp.stack([w[n] for n in SHARDED_VEC])
    gathered, vec_all = gather_weights([w[n].astype(BF16) for n in BIG], vecs, "gather_weights")
    gw = dict(zip(BIG, gathered))
    vec_full = vec_all.transpose(1, 2, 0, 3).reshape(vecs.shape[0], vecs.shape[1], -1)
    rep = {n: w[n] for n in REPLICATED}
    rep.update({n: vec_full[k] for k, n in enumerate(SHARDED_VEC)})

    loss_tile, grad_x, big, tags, small = local_step(x[0], loss_target[0], gw, rep)

    grads = reduce_to_owner_chips(big, tags, "grads")
    packed = _pack_small(small, loss_tile)
    everyone = gather_all(packed, "gather_small")
    add8 = lambda *a: (((a[0] + a[1]) + (a[2] + a[3])) + ((a[4] + a[5]) + (a[6] + a[7])),)
    total = ew(add8, [everyone[k] for k in range(8)], [F32], "sum_small")[0]
    small_g, loss = _unpack_small(total, {n: small[n].shape for n in SMALL})
    width = w["sgu_norm"].shape[1]
    for n in SHARDED_VEC:
        small_g[n] = lax.dynamic_slice_in_dim(small_g[n], chip * width, width, axis=1)
    grads.update(small_g)
    for n in BIG:
        grads[n] = grads[n].reshape(w[n].shape)

    delta, new_m, new_v = {}, {}, {}
    for n in WEIGHTS:
        shape = w[n].shape
        two_d = (lambda a: a.reshape(1, -1)) if len(shape) == 1 else (lambda a: a)
        dn, mn, vn = adamw(two_d(w[n]), two_d(grads[n]), two_d(m[n]), two_d(v[n]), "adamw_" + n)
        delta[n], new_m[n], new_v[n] = dn.reshape(shape), mn.reshape(shape), vn.reshape(shape)
    return (loss, grad_x[None], *[grads[n] for n in WEIGHTS], *[delta[n] for n in WEIGHTS],
            *[new_m[n] for n in WEIGHTS], *[new_v[n] for n in WEIGHTS])
```

```python
import numpy as np
import jax
import jax.numpy as jnp
from jax import lax
from jax.experimental import pallas as pl
from jax.experimental.pallas import tpu as pltpu

F32 = jnp.float32
BF16 = jnp.bfloat16
MESH = pl.DeviceIdType.MESH

EPS = 1e-6
HEAD_DIM = 64
BLOCK = 128
GRID_W = 64
ROPE_THETA = 10000.0
N_CHIPS = 4
LANES = 128
V7X_VMEM_BYTES = 64 * 1024 * 1024
VMEM_LIMIT = V7X_VMEM_BYTES - 8 * 1024 * 1024

ADAM_LR = 0.001
ADAM_B1 = 0.9
ADAM_B2 = 0.999
ADAM_EPS = 1e-08
ADAM_WD = 0.01
ADAM_STEP = 10

NT_DIMS = (((1,), (1,)), ((), ()))
TN_DIMS = (((0,), (0,)), ((), ()))


def _params(*sem):
    return pltpu.CompilerParams(dimension_semantics=sem, vmem_limit_bytes=VMEM_LIMIT)


def _sds(shape, dtype):
    return jax.ShapeDtypeStruct(tuple(shape), dtype)


def _row_tile(rows, want):
    t = min(rows, want)
    assert rows % t == 0, (rows, want)
    return t


def norm_mm(x, g, w4, layer, out_dtype, name):
    s_len, d = x.shape
    ns = w4.shape[-1]
    tm = _row_tile(s_len, 512)

    def body(x_ref, g_ref, w_ref, h_ref, y_ref):
        xf = x_ref[...]
        r = lax.rsqrt(jnp.mean(xf * xf, axis=-1, keepdims=True) + EPS)
        h = ((xf * r) * g_ref[...]).astype(BF16)
        h_ref[...] = h
        for s in range(N_CHIPS):
            y_ref[:, s * ns:(s + 1) * ns] = jnp.dot(h, w_ref[s], preferred_element_type=F32).astype(y_ref.dtype)

    return pl.pallas_call(
        body, name=name, grid=(s_len // tm,),
        in_specs=[pl.BlockSpec((tm, d), lambda i: (i, 0)),
                  pl.BlockSpec((1, d), lambda i: (0, 0)),
                  pl.BlockSpec((N_CHIPS, None, d, ns), lambda i: (0, layer, 0, 0))],
        out_specs=[pl.BlockSpec((tm, d), lambda i: (i, 0)),
                   pl.BlockSpec((tm, N_CHIPS * ns), lambda i: (i, 0))],
        out_shape=[_sds((s_len, d), BF16), _sds((s_len, N_CHIPS * ns), out_dtype)],
        compiler_params=_params("arbitrary"),
    )(x, g, w4)


def mm_res(a, w4, layer, res, name, relu2=False):
    s_len, k = a.shape
    kq, n = w4.shape[-2:]
    assert kq * N_CHIPS == k
    tm = _row_tile(s_len, 256 if k > 1024 else 512)

    def body(a_ref, w0, w1, w2, w3, r_ref, o_ref):
        acc = r_ref[...]
        for s, w_ref in enumerate((w0, w1, w2, w3)):
            av = a_ref[:, s * kq:(s + 1) * kq]
            if relu2:
                t = jnp.maximum(av.astype(F32), 0.0)
                av = (t * t).astype(BF16)
            acc = acc + jnp.dot(av, w_ref[...], preferred_element_type=F32)
        o_ref[...] = acc

    def wspec(s):
        return pl.BlockSpec((None, None, kq, n), lambda i: (s, layer, 0, 0))

    return pl.pallas_call(
        body, name=name, grid=(s_len // tm,),
        in_specs=[pl.BlockSpec((tm, k), lambda i: (i, 0)), wspec(0), wspec(1), wspec(2), wspec(3),
                  pl.BlockSpec((tm, n), lambda i: (i, 0))],
        out_specs=pl.BlockSpec((tm, n), lambda i: (i, 0)),
        out_shape=_sds((s_len, n), F32),
        compiler_params=_params("arbitrary"),
    )(a, w4, w4, w4, w4, res)


def mm_nt(dy, w4, layer, name):
    s_len, n = dy.shape
    mq = w4.shape[-2]
    tm = _row_tile(s_len, 512)

    def body(d_ref, w0, w1, w2, w3, o_ref):
        dv = d_ref[...]
        for s, w_ref in enumerate((w0, w1, w2, w3)):
            o_ref[:, s * mq:(s + 1) * mq] = lax.dot_general(
                dv, w_ref[...], NT_DIMS, preferred_element_type=F32).astype(BF16)

    def wspec(s):
        return pl.BlockSpec((None, None, mq, n), lambda i: (s, layer, 0, 0))

    return pl.pallas_call(
        body, name=name, grid=(s_len // tm,),
        in_specs=[pl.BlockSpec((tm, n), lambda i: (i, 0)), wspec(0), wspec(1), wspec(2), wspec(3)],
        out_specs=pl.BlockSpec((tm, N_CHIPS * mq), lambda i: (i, 0)),
        out_shape=_sds((s_len, N_CHIPS * mq), BF16),
        compiler_params=_params("arbitrary"),
    )(dy, w4, w4, w4, w4)


def mm_nt_relu2_bwd(dy, w4, layer, a, name):
    s_len, n = dy.shape
    mq = w4.shape[-2]
    tm = _row_tile(s_len, 512)

    def body(d_ref, w_ref, a_ref, o_ref):
        dv = d_ref[...]
        for s in range(N_CHIPS):
            cols = slice(s * mq, (s + 1) * mq)
            dz = lax.dot_general(dv, w_ref[s], NT_DIMS, preferred_element_type=F32)
            o_ref[:, cols] = (dz * (2.0 * jnp.maximum(a_ref[:, cols].astype(F32), 0.0))).astype(BF16)

    return pl.pallas_call(
        body, name=name, grid=(s_len // tm,),
        in_specs=[pl.BlockSpec((tm, n), lambda i: (i, 0)),
                  pl.BlockSpec((N_CHIPS, None, mq, n), lambda i: (0, layer, 0, 0)),
                  pl.BlockSpec((tm, N_CHIPS * mq), lambda i: (i, 0))],
        out_specs=pl.BlockSpec((tm, N_CHIPS * mq), lambda i: (i, 0)),
        out_shape=_sds((s_len, N_CHIPS * mq), BF16),
        compiler_params=_params("arbitrary"),
    )(dy, w4, a)


def dx_norm(dy, w4, layer, x, g, dres, name):
    s_len, d = x.shape
    ns = w4.shape[-1]
    tm = _row_tile(s_len, 512)

    def body(dy_ref, w_ref, x_ref, g_ref, dr_ref, dx_ref, dxb_ref, dg_ref):
        i = pl.program_id(0)
        dh = lax.dot_general(dy_ref[:, 0:ns], w_ref[0], NT_DIMS, preferred_element_type=F32)
        for s in range(1, N_CHIPS):
            dh = dh + lax.dot_general(dy_ref[:, s * ns:(s + 1) * ns], w_ref[s], NT_DIMS,
                                      preferred_element_type=F32)
        xf = x_ref[...]
        r = lax.rsqrt(jnp.mean(xf * xf, axis=-1, keepdims=True) + EPS)
        xhat = xf * r
        dg_part = jnp.sum(dh * xhat, axis=0, keepdims=True)

        @pl.when(i == 0)
        def _():
            dg_ref[...] = dg_part

        @pl.when(i > 0)
        def _():
            dg_ref[...] += dg_part

        dxh = dh * g_ref[...]
        dx = dr_ref[...] + r * (dxh - xhat * jnp.mean(dxh * xhat, axis=-1, keepdims=True))
        dx_ref[...] = dx
        dxb_ref[...] = dx.astype(BF16)

    row = pl.BlockSpec((tm, d), lambda i: (i, 0))
    vec = pl.BlockSpec((1, d), lambda i: (0, 0))
    return pl.pallas_call(
        body, name=name, grid=(s_len // tm,),
        in_specs=[pl.BlockSpec((tm, N_CHIPS * ns), lambda i: (i, 0)),
                  pl.BlockSpec((N_CHIPS, None, d, ns), lambda i: (0, layer, 0, 0)), row, vec, row],
        out_specs=[row, row, vec],
        out_shape=[_sds((s_len, d), F32), _sds((s_len, d), BF16), _sds((1, d), F32)],
        compiler_params=_params("arbitrary"),
    )(dy, w4, x, g, dres)


def dw_mm(a, b, name, col_sharded, relu2=False):
    s_len, k = a.shape
    n = b.shape[1]
    ts = _row_tile(s_len, 2048)
    tk = min(k, 1024)
    tn = n // N_CHIPS if col_sharded else min(n, 1024)
    n_s = s_len // ts

    def body(a_ref, b_ref, o_ref, acc_ref):
        s = pl.program_id(2)
        av = a_ref[...]
        if relu2:
            t = jnp.maximum(av.astype(F32), 0.0)
            av = (t * t).astype(BF16)
        part = lax.dot_general(av, b_ref[...], TN_DIMS, preferred_element_type=F32)
        if n_s == 1:
            o_ref[...] = part.astype(BF16)
            return

        @pl.when(s == 0)
        def _():
            acc_ref[...] = part

        @pl.when((s > 0) & (s < n_s - 1))
        def _():
            acc_ref[...] += part

        @pl.when(s == n_s - 1)
        def _():
            o_ref[...] = (acc_ref[...] + part).astype(BF16)

    if col_sharded:
        out_shape = _sds((N_CHIPS, k, tn), BF16)
        out_spec = pl.BlockSpec((None, tk, tn), lambda i, j, s: (j, i, 0))
    else:
        out_shape = _sds((N_CHIPS, k // N_CHIPS, n), BF16)
        rows_per = k // N_CHIPS
        assert tk % rows_per == 0 or rows_per % tk == 0
        if tk >= rows_per:
            out_shape = _sds((k, n), BF16)
            out_spec = pl.BlockSpec((tk, tn), lambda i, j, s: (i, j))
        else:
            per = rows_per // tk
            out_spec = pl.BlockSpec((None, tk, tn), lambda i, j, s: (i // per, i % per, j))

    out = pl.pallas_call(
        body, name=name, grid=(k // tk, n // tn, n_s),
        in_specs=[pl.BlockSpec((ts, tk), lambda i, j, s: (s, i)),
                  pl.BlockSpec((ts, tn), lambda i, j, s: (s, j))],
        out_specs=out_spec, out_shape=out_shape,
        scratch_shapes=[pltpu.VMEM((tk, tn), F32)],
        compiler_params=_params("arbitrary", "arbitrary", "arbitrary"),
    )(a, b)
    if not col_sharded:
        out = out.reshape(N_CHIPS, k // N_CHIPS, n)
    return out


def ew(fn, ins, out_dtypes, name, tile_rows=256):
    rows, cols = ins[0].shape
    for a in ins:
        assert a.shape == (rows, cols), (name, a.shape, rows, cols)
    tr = rows if (rows <= tile_rows or rows % tile_rows) else tile_rows
    n_in = len(ins)

    def body(*refs):
        outs = fn(*[r[...] for r in refs[:n_in]])
        for o_ref, val in zip(refs[n_in:], outs):
            o_ref[...] = val.astype(o_ref.dtype)

    spec = pl.BlockSpec((tr, cols), lambda i: (i, 0))
    return pl.pallas_call(
        body, name=name, grid=(rows // tr,),
        in_specs=[spec] * n_in, out_specs=[spec] * len(out_dtypes),
        out_shape=[_sds((rows, cols), dt) for dt in out_dtypes],
        compiler_params=_params("arbitrary"),
    )(*ins)


def adamw(w, g, m, v, name):
    shape = w.shape
    cols = shape[-1]
    two_d = lambda a: a.reshape(-1, cols)

    def fn(wv, gv, mv, vv):
        m_new = ADAM_B1 * mv + (1.0 - ADAM_B1) * gv
        v_new = ADAM_B2 * vv + (1.0 - ADAM_B2) * (gv * gv)
        m_hat = m_new / (1.0 - ADAM_B1 ** ADAM_STEP)
        v_hat = v_new / (1.0 - ADAM_B2 ** ADAM_STEP)
        delta = -ADAM_LR * (m_hat / (jnp.sqrt(v_hat) + ADAM_EPS) + ADAM_WD * wv)
        return delta, m_new, v_new

    d, mn, vn = ew(fn, [two_d(w), two_d(g), two_d(m), two_d(v)], [F32, F32, F32], name)
    return d.reshape(shape), mn.reshape(shape), vn.reshape(shape)


def rope_tables(s_len):
    def angles(pos, dim):
        freqs = ROPE_THETA ** (-jnp.arange(0, dim, 2, dtype=F32) / dim)
        ang = pos.astype(F32)[:, None] * freqs[None, :]
        return jnp.cos(ang), jnp.sin(ang)

    pos = jnp.arange(s_len)
    rows = s_len // GRID_W
    row_idx = jnp.repeat(jnp.arange(rows), GRID_W)
    col_idx = jnp.tile(jnp.arange(GRID_W), rows)
    c1, s1 = angles(pos, HEAD_DIM)
    cr, sr = angles(row_idx, HEAD_DIM // 2)
    cc, sc = angles(col_idx, HEAD_DIM // 2)
    cos1 = jnp.tile(jnp.concatenate([c1, c1], -1), (1, 2))
    sin1 = jnp.tile(jnp.concatenate([-s1, s1], -1), (1, 2))
    cos2 = jnp.tile(jnp.concatenate([cr, cr, cc, cc], -1), (1, 2))
    sin2 = jnp.tile(jnp.concatenate([-sr, sr, -sc, sc], -1), (1, 2))
    return cos1, sin1, cos2, sin2


def _lane_iota(rows):
    return lax.broadcasted_iota(jnp.int32, (rows, LANES), 1)


def _swap(x, dist, lane):
    return jnp.where((lane & dist) != 0, pltpu.roll(x, dist, 1), pltpu.roll(x, LANES - dist, 1))


def _head_sum(t, lane):
    for dist in (32, 16, 8, 4, 2, 1):
        t = t + _swap(t, dist, lane)
    return t


Q_SCALE = HEAD_DIM ** -0.5
LOG2E = 1.4426950408889634
LN2 = 0.6931471805599453
CHUNK_KIND = ["qa"] * 4 + ["ka", "va"] + ["qb"] * 4 + ["kb", "vb"]
QA_COL, KA_COL, QB_COL, KB_COL = 0, 512, 768, 1280


def prep_fwd(proj, tabs, qn_g, kn_g, name):
    s_len, width = proj.shape
    ts = _row_tile(s_len, 512)
    cos1, sin1, cos2, sin2 = tabs

    def body(p_ref, c1_ref, s1_ref, c2_ref, s2_ref, qg_ref, kg_ref, o_ref):
        lane = _lane_iota(ts)
        c1, s1, c2, s2 = c1_ref[...], s1_ref[...], c2_ref[...], s2_ref[...]
        for cb, kind in enumerate(CHUNK_KIND):
            x = p_ref[:, cb * LANES:(cb + 1) * LANES]
            if kind in ("qa", "ka"):
                y = x * c1 + _swap(x, 32, lane) * s1
            elif kind in ("qb", "kb"):
                gain = qg_ref[...] if kind == "qb" else kg_ref[...]
                ms = _head_sum(x * x, lane) * (1.0 / HEAD_DIM)
                xn = (x * lax.rsqrt(ms + EPS)) * gain
                y = xn * c2 + _swap(xn, 16, lane) * s2
            else:
                y = x
            if kind in ("qa", "qb"):
                y = y * (Q_SCALE * LOG2E)
            o_ref[:, cb * LANES:(cb + 1) * LANES] = y.astype(BF16)

    tab = pl.BlockSpec((ts, LANES), lambda i: (i, 0))
    vec = pl.BlockSpec((1, LANES), lambda i: (0, 0))
    return pl.pallas_call(
        body, name=name, grid=(s_len // ts,),
        in_specs=[pl.BlockSpec((ts, width), lambda i: (i, 0)), tab, tab, tab, tab, vec, vec],
        out_specs=pl.BlockSpec((ts, width), lambda i: (i, 0)),
        out_shape=_sds((s_len, width), BF16),
        compiler_params=_params("arbitrary"),
    )(proj, cos1, sin1, cos2, sin2, qn_g, kn_g)


def prep_bwd(proj, dqa, dka, dva, dqb, dkb, dvb, tabs, qn_g, kn_g, name):
    s_len, width = proj.shape
    ts = _row_tile(s_len, 256)
    cos1, sin1, cos2, sin2 = tabs

    def body(p_ref, dqa_ref, dka_ref, dva_ref, dqb_ref, dkb_ref, dvb_ref,
             c1_ref, s1_ref, c2_ref, s2_ref, qg_ref, kg_ref, o_ref, dqg_ref, dkg_ref):
        i = pl.program_id(0)
        lane = _lane_iota(ts)
        c1, s1, c2, s2 = c1_ref[...], s1_ref[...], c2_ref[...], s2_ref[...]

        def rope_t(dy, cos, sin, dist):
            return dy * cos + _swap(dy * sin, dist, lane)

        def norm_bwd(dy, x, gain):
            r = lax.rsqrt(_head_sum(x * x, lane) * (1.0 / HEAD_DIM) + EPS)
            xhat = x * r
            dgain = jnp.sum(dy * xhat, axis=0, keepdims=True)
            dxh = dy * gain
            dx = r * (dxh - xhat * (_head_sum(dxh * xhat, lane) * (1.0 / HEAD_DIM)))
            return dx, dgain

        dqg = jnp.zeros((1, LANES), F32)
        dkg = jnp.zeros((1, LANES), F32)
        for cb, kind in enumerate(CHUNK_KIND):
            cols = slice(cb * LANES, (cb + 1) * LANES)
            if kind == "qa":
                dx = rope_t(dqa_ref[:, cols] * Q_SCALE, c1, s1, 32)
            elif kind == "ka":
                dx = rope_t(dka_ref[...], c1, s1, 32)
            elif kind == "va":
                dx = dva_ref[...]
            elif kind == "qb":
                qcols = slice((cb - 6) * LANES, (cb - 5) * LANES)
                dy = rope_t(dqb_ref[:, qcols] * Q_SCALE, c2, s2, 16)
                dx, dgain = norm_bwd(dy, p_ref[:, cols], qg_ref[...])
                dqg = dqg + dgain
            elif kind == "kb":
                dy = rope_t(dkb_ref[...], c2, s2, 16)
                dx, dgain = norm_bwd(dy, p_ref[:, cols], kg_ref[...])
                dkg = dkg + dgain
            else:
                dx = dvb_ref[...]
            o_ref[:, cols] = dx.astype(BF16)

        @pl.when(i == 0)
        def _():
            dqg_ref[...] = dqg
            dkg_ref[...] = dkg

        @pl.when(i > 0)
        def _():
            dqg_ref[...] += dqg
            dkg_ref[...] += dkg

    tab = pl.BlockSpec((ts, LANES), lambda i: (i, 0))
    vec = pl.BlockSpec((1, LANES), lambda i: (0, 0))
    dq_spec = pl.BlockSpec((ts, 4 * LANES), lambda i: (i, 0))
    return pl.pallas_call(
        body, name=name, grid=(s_len // ts,),
        in_specs=([pl.BlockSpec((ts, width), lambda i: (i, 0)), dq_spec, tab, tab, dq_spec, tab, tab]
                  + [tab] * 4 + [vec, vec]),
        out_specs=[pl.BlockSpec((ts, width), lambda i: (i, 0)), vec, vec],
        out_shape=[_sds((s_len, width), BF16), _sds((1, LANES), F32), _sds((1, LANES), F32)],
        compiler_params=_params("arbitrary"),
    )(proj, dqa, dka, dva, dqb, dkb, dvb, cos1, sin1, cos2, sin2, qn_g, kn_g)


NEG = -1e30
GROUP = 4
KV_HEADS = 2
GROUP_W = GROUP * HEAD_DIM
LSE_ROWS = 8


def _pos_mask_t(k_start, q_start, s_len, tk, tq):
    kpos = k_start + lax.broadcasted_iota(jnp.int32, (tk, tq), 0)
    qpos = q_start + lax.broadcasted_iota(jnp.int32, (tk, tq), 1)
    return (jnp.abs(kpos - qpos) <= BLOCK) & (kpos >= 0) & (kpos < s_len)


def flash_fwd_t(qkv_t, kv_tok, q_rb, k_i, v_rb, sink, window, name, comm=None):
    s_len = qkv_t.shape[1]
    if window:
        tq = _row_tile(s_len, 512)
        tk = 256
        per = tq // tk
        n_kv = per + 2
    else:
        tq = tk = _row_tile(s_len, 1024)
        n_kv = s_len // tk
    n_kb = s_len // tk
    n_i = s_len // tq
    c_ins, c_outs, c_remote = comm if comm else ([], [], [])
    n_main = 4 if window else 3

    def body(*refs):
        main, c_in_refs = refs[:n_main], refs[n_main:n_main + len(c_ins)]
        rest = refs[n_main + len(c_ins):]
        (o_ref, lse_ref), c_out_refs = rest[:2], rest[2:2 + len(c_outs)]
        m_sc, l_sc, acc_sc = rest[2 + len(c_outs):5 + len(c_outs)]
        c_sems = rest[5 + len(c_outs):]
        if window:
            sink_ref, q_ref, k_ref, v_ref = main
        else:
            q_ref, k_ref, v_ref = main
        h, i, t = pl.program_id(0), pl.program_id(1), pl.program_id(2)
        if comm:
            @pl.when((h == 0) & (i == 0) & (t == 0))
            def _():
                _exchange_start(c_remote, c_in_refs, c_out_refs, *c_sems)

        @pl.when(t == 0)
        def _():
            for g in range(GROUP):
                if window:
                    m_sc[g] = jnp.full((1, tq), sink_ref[h * GROUP + g] * LOG2E, F32)
                    l_sc[g] = jnp.ones((1, tq), F32)
                else:
                    m_sc[g] = jnp.full((1, tq), NEG, F32)
                    l_sc[g] = jnp.zeros((1, tq), F32)
                acc_sc[g] = jnp.zeros((HEAD_DIM, tq), F32)

        k = k_ref[...]
        v_t = v_ref[...]
        if window:
            mask = _pos_mask_t((i * per - 1 + t) * tk, i * tq, s_len, tk, tq)
        for g in range(GROUP):
            q_t = q_ref[g * HEAD_DIM:(g + 1) * HEAD_DIM, :]
            s_t = jnp.dot(k, q_t, preferred_element_type=F32)
            if window:
                s_t = jnp.where(mask, s_t, NEG)
            m_prev = m_sc[g]
            m_new = jnp.maximum(m_prev, jnp.max(s_t, axis=0, keepdims=True))
            alpha = jnp.exp2(m_prev - m_new)
            p_t = jnp.exp2(s_t - m_new)
            l_sc[g] = alpha * l_sc[g] + jnp.sum(p_t, axis=0, keepdims=True)
            acc_sc[g] = alpha * acc_sc[g] + jnp.dot(v_t, p_t.astype(BF16), preferred_element_type=F32)
            m_sc[g] = m_new

        @pl.when(t == n_kv - 1)
        def _():
            for g in range(GROUP):
                l = l_sc[g]
                o_ref[g * HEAD_DIM:(g + 1) * HEAD_DIM, :] = (acc_sc[g] / l).astype(BF16)
                lse_ref[g * LSE_ROWS:(g + 1) * LSE_ROWS, :] = jnp.broadcast_to(
                    m_sc[g] + jnp.log(l) * LOG2E, (LSE_ROWS, tq))

        if comm:
            @pl.when((h == KV_HEADS - 1) & (i == n_i - 1) & (t == n_kv - 1))
            def _():
                _exchange_finish(c_remote, c_in_refs, c_out_refs, *c_sems)

    if window:
        kv_blk = lambda i, t: jnp.clip(i * per - 1 + t, 0, n_kb - 1)
    else:
        kv_blk = lambda i, t: t
    hbm = pl.BlockSpec(memory_space=pl.ANY)
    in_specs = [pl.BlockSpec((GROUP_W, tq), lambda h, i, t: (q_rb + h, i)),
                pl.BlockSpec((None, tk, HEAD_DIM), lambda h, i, t: (k_i + h, kv_blk(i, t), 0)),
                pl.BlockSpec((HEAD_DIM, tk), lambda h, i, t: (v_rb + h, kv_blk(i, t)))]
    args = [qkv_t, kv_tok, qkv_t]
    if window:
        in_specs = [pl.BlockSpec(memory_space=pltpu.SMEM)] + in_specs
        args = [sink] + args
    return pl.pallas_call(
        body, name=name, grid=(KV_HEADS, n_i, n_kv),
        in_specs=in_specs + [hbm] * len(c_ins),
        out_specs=[pl.BlockSpec((GROUP_W, tq), lambda h, i, t: (h, i)),
                   pl.BlockSpec((GROUP * LSE_ROWS, tq), lambda h, i, t: (h, i))] + [hbm] * len(c_outs),
        out_shape=[_sds((KV_HEADS * GROUP_W, s_len), BF16),
                   _sds((KV_HEADS * GROUP * LSE_ROWS, s_len), F32)] + list(c_outs),
        scratch_shapes=[pltpu.VMEM((GROUP, 1, tq), F32), pltpu.VMEM((GROUP, 1, tq), F32),
                        pltpu.VMEM((GROUP, HEAD_DIM, tq), F32)] + _exchange_sems(c_remote),
        compiler_params=_params("arbitrary", "arbitrary", "arbitrary"),
    )(*args, *c_ins)


def flash_bwd_t(qkv_t, kv_tok, o_t, do_t, lse, q_rb, k_i, v_i, k_rb, o_rb, sink, window, name, comm=None):
    s_len = qkv_t.shape[1]
    if window:
        tq = _row_tile(s_len, 512)
        tk = 256
        n_q = 2
    else:
        tq = tk = _row_tile(s_len, 1024)
        n_q = s_len // tq
    n_qb = s_len // tq
    n_j = s_len // tk
    c_ins, c_outs, c_remote = comm if comm else ([], [], [])
    n_main = 8 if window else 7
    n_out = 4 if window else 3

    def body(*refs):
        main, c_in_refs = refs[:n_main], refs[n_main:n_main + len(c_ins)]
        rest = refs[n_main + len(c_ins):]
        outs, c_out_refs = rest[:n_out], rest[n_out:n_out + len(c_outs)]
        dk_sc, dv_sc = rest[n_out + len(c_outs):n_out + len(c_outs) + 2]
        c_sems = rest[n_out + len(c_outs) + 2:]
        if window:
            sink_ref, q_ref, k_ref, v_ref, kt_ref, o_ref, do_ref, lse_ref = main
            dq_ref, dk_ref, dv_ref, dsink_ref = outs
        else:
            q_ref, k_ref, v_ref, kt_ref, o_ref, do_ref, lse_ref = main
            dq_ref, dk_ref, dv_ref = outs
        h, j, t = pl.program_id(0), pl.program_id(1), pl.program_id(2)
        q_blk = (j + 1) // 2 - 1 + t if window else t
        if comm:
            @pl.when((h == 0) & (j == 0) & (t == 0))
            def _():
                _exchange_start(c_remote, c_in_refs, c_out_refs, *c_sems)

        @pl.when((j == 0) & (t == 0))
        def _():
            dq_ref[...] = jnp.zeros(dq_ref.shape, F32)
            if window:
                dsink_ref[...] = jnp.zeros((8, LANES), F32)

        @pl.when(t == 0)
        def _():
            dk_sc[...] = jnp.zeros((tk, HEAD_DIM), F32)
            dv_sc[...] = jnp.zeros((tk, HEAD_DIM), F32)

        def step():
            k, v, k_t = k_ref[...], v_ref[...], kt_ref[...]
            if window:
                mask = _pos_mask_t(j * tk, q_blk * tq, s_len, tk, tq)
                lane = lax.broadcasted_iota(jnp.int32, (8, LANES), 1)
                sink_tile = jnp.zeros((8, LANES), F32)
            dk_acc = dk_sc[...]
            dv_acc = dv_sc[...]
            for g in range(GROUP):
                rows = slice(g * HEAD_DIM, (g + 1) * HEAD_DIM)
                q_t, o_g, do_g = q_ref[rows, :], o_ref[rows, :], do_ref[rows, :]
                s_t = jnp.dot(k, q_t, preferred_element_type=F32)
                if window:
                    s_t = jnp.where(mask, s_t, NEG)
                lse_row = lse_ref[g * LSE_ROWS:g * LSE_ROWS + 1, :]
                p_t = jnp.exp2(s_t - lse_row)
                delta = jnp.sum(do_g.astype(F32) * o_g.astype(F32), axis=0, keepdims=True)
                dp_t = jnp.dot(v, do_g, preferred_element_type=F32)
                ds_t = (p_t * (dp_t - delta)).astype(BF16)
                dv_acc = dv_acc + lax.dot_general(p_t.astype(BF16), do_g, NT_DIMS, preferred_element_type=F32)
                dk_acc = dk_acc + lax.dot_general(ds_t, q_t, NT_DIMS, preferred_element_type=F32)
                dq_ref[q_blk, rows, :] += jnp.dot(k_t, ds_t, preferred_element_type=F32)
                if window:
                    p_sink = jnp.exp2(sink_ref[h * GROUP + g] * LOG2E - lse_row)
                    term = -jnp.sum(p_sink * delta, axis=1, keepdims=True)
                    sink_tile = jnp.where(lane == g, term, sink_tile)
            dk_sc[...] = dk_acc
            dv_sc[...] = dv_acc
            if window:
                @pl.when((j % 2 == 0) & (t == 1))
                def _():
                    dsink_ref[...] += sink_tile

        if window:
            pl.when((q_blk >= 0) & (q_blk < n_qb))(step)
        else:
            step()

        @pl.when(t == n_q - 1)
        def _():
            dk_ref[...] = dk_sc[...] * LN2
            dv_ref[...] = dv_sc[...]

        if comm:
            @pl.when((h == KV_HEADS - 1) & (j == n_j - 1) & (t == n_q - 1))
            def _():
                _exchange_finish(c_remote, c_in_refs, c_out_refs, *c_sems)

    if window:
        qb = lambda j, t: jnp.clip((j + 1) // 2 - 1 + t, 0, n_qb - 1)
    else:
        qb = lambda j, t: t
    hbm = pl.BlockSpec(memory_space=pl.ANY)
    in_specs = [pl.BlockSpec((GROUP_W, tq), lambda h, j, t: (q_rb + h, qb(j, t))),
                pl.BlockSpec((None, tk, HEAD_DIM), lambda h, j, t: (k_i + h, j, 0)),
                pl.BlockSpec((None, tk, HEAD_DIM), lambda h, j, t: (v_i + h, j, 0)),
                pl.BlockSpec((HEAD_DIM, tk), lambda h, j, t: (k_rb + h, j)),
                pl.BlockSpec((GROUP_W, tq), lambda h, j, t: (o_rb + h, qb(j, t))),
                pl.BlockSpec((GROUP_W, tq), lambda h, j, t: (o_rb + h, qb(j, t))),
                pl.BlockSpec((GROUP * LSE_ROWS, tq), lambda h, j, t: (h, qb(j, t)))]
    args = [qkv_t, kv_tok, kv_tok, qkv_t, o_t, do_t, lse]
    kv_out = _sds((KV_HEADS, s_len, HEAD_DIM), F32)
    out_specs = [pl.BlockSpec((n_qb, GROUP_W, tq), lambda h, j, t: (0, h, 0)),
                 pl.BlockSpec((None, tk, HEAD_DIM), lambda h, j, t: (h, j, 0)),
                 pl.BlockSpec((None, tk, HEAD_DIM), lambda h, j, t: (h, j, 0))]
    out_shape = [_sds((n_qb, KV_HEADS * GROUP_W, tq), F32), kv_out, kv_out]
    if window:
        in_specs = [pl.BlockSpec(memory_space=pltpu.SMEM)] + in_specs
        args = [sink] + args
        out_specs.append(pl.BlockSpec((None, 8, LANES), lambda h, j, t: (h, 0, 0)))
        out_shape.append(_sds((KV_HEADS, 8, LANES), F32))
    return pl.pallas_call(
        body, name=name, grid=(KV_HEADS, n_j, n_q),
        in_specs=in_specs + [hbm] * len(c_ins), out_specs=out_specs + [hbm] * len(c_outs),
        out_shape=out_shape + list(c_outs),
        scratch_shapes=[pltpu.VMEM((tk, HEAD_DIM), F32), pltpu.VMEM((tk, HEAD_DIM), F32)]
        + _exchange_sems(c_remote),
        compiler_params=_params("arbitrary", "arbitrary", "arbitrary"),
    )(*args, *c_ins)


SGU_GROUPS = 8
SGU_CHUNK = 128
GELU_C = float(np.sqrt(2.0 / np.pi))
GELU_A = 0.044715


def _gelu(x):
    return x * (0.5 * (1.0 + jnp.tanh(GELU_C * (x + GELU_A * (x * x * x)))))


def _gelu_grad(x):
    t = jnp.tanh(GELU_C * (x + GELU_A * (x * x * x)))
    return 0.5 * (1.0 + t) + 0.5 * x * (1.0 - t * t) * (GELU_C * (1.0 + 3.0 * GELU_A * (x * x)))


def _layernorm_stats(v):
    mu = jnp.mean(v, axis=-1, keepdims=True)
    var = jnp.mean(jnp.square(v - mu), axis=-1, keepdims=True)
    rstd = lax.rsqrt(var + EPS)
    return (v - mu) * rstd, rstd


def sgu_mid_fwd(zpre, ln_g, ln_b, ws, bsb, name):
    s_len, width = zpre.shape
    d = width // 2
    ts = _row_tile(s_len, 256)

    def body(z_ref, g_ref, b_ref, ws_ref, bs_ref, y_ref):
        z = _gelu(z_ref[...])
        u, v = z[:, :d], z[:, d:]
        vhat, _ = _layernorm_stats(v)
        vn = (vhat * g_ref[...] + b_ref[...]).astype(BF16)
        for n in range(ts // SGU_CHUNK):
            rows = slice(n * SGU_CHUNK, (n + 1) * SGU_CHUNK)
            for g in range(SGU_GROUPS):
                cols = slice(g * LANES, (g + 1) * LANES)
                mixed = jnp.dot(ws_ref[g], vn[rows, cols], preferred_element_type=F32) + bs_ref[g]
                y_ref[rows, cols] = (u[rows, cols] * mixed).astype(BF16)

    vec = pl.BlockSpec((1, d), lambda i: (0, 0))
    cube = pl.BlockSpec((SGU_GROUPS, SGU_CHUNK, SGU_CHUNK), lambda i: (0, 0, 0))
    return pl.pallas_call(
        body, name=name, grid=(s_len // ts,),
        in_specs=[pl.BlockSpec((ts, width), lambda i: (i, 0)), vec, vec, cube, cube],
        out_specs=pl.BlockSpec((ts, d), lambda i: (i, 0)),
        out_shape=_sds((s_len, d), BF16),
        compiler_params=_params("arbitrary"),
    )(zpre, ln_g, ln_b, ws, bsb)


def sgu_mid_bwd(zpre, dy, ln_g, ln_b, ws, wst, bsb, name):
    s_len, width = zpre.shape
    d = width // 2
    ts = _row_tile(s_len, 256)
    n_steps = s_len // ts

    def body(z_ref, dy_ref, g_ref, b_ref, ws_ref, wst_ref, bs_ref,
             dz_ref, dws_ref, dbs_ref, dg_ref, db_ref, du_sc, dvn_sc):
        i = pl.program_id(0)

        @pl.when(i == 0)
        def _():
            dws_ref[...] = jnp.zeros(dws_ref.shape, F32)
            dbs_ref[...] = jnp.zeros(dbs_ref.shape, F32)
            dg_ref[...] = jnp.zeros(dg_ref.shape, F32)
            db_ref[...] = jnp.zeros(db_ref.shape, F32)

        zp = z_ref[...]
        z = _gelu(zp)
        u, v = z[:, :d], z[:, d:]
        vhat, rstd = _layernorm_stats(v)
        gain = g_ref[...]
        vn = (vhat * gain + b_ref[...]).astype(BF16)
        dyf = dy_ref[...].astype(F32)
        for n in range(ts // SGU_CHUNK):
            rows = slice(n * SGU_CHUNK, (n + 1) * SGU_CHUNK)
            for g in range(SGU_GROUPS):
                cols = slice(g * LANES, (g + 1) * LANES)
                vt = vn[rows, cols]
                mixed = jnp.dot(ws_ref[g], vt, preferred_element_type=F32) + bs_ref[g]
                dyt = dyf[rows, cols]
                du_sc[rows, cols] = dyt * mixed
                dmixed = dyt * u[rows, cols]
                dmb = dmixed.astype(BF16)
                dvn_sc[rows, cols] = jnp.dot(wst_ref[g], dmb, preferred_element_type=F32)
                dws_ref[g] += lax.dot_general(dmb, vt, NT_DIMS, preferred_element_type=F32)
                dbs_ref[g] += dmixed
        dvn = dvn_sc[...]
        dg_ref[...] += jnp.sum(dvn * vhat, axis=0, keepdims=True)
        db_ref[...] += jnp.sum(dvn, axis=0, keepdims=True)
        dvh = dvn * gain
        dv = rstd * (dvh - jnp.mean(dvh, axis=-1, keepdims=True)
                     - vhat * jnp.mean(dvh * vhat, axis=-1, keepdims=True))
        gp = _gelu_grad(zp)
        dz_ref[:, :d] = (du_sc[...] * gp[:, :d]).astype(BF16)
        dz_ref[:, d:] = (dv * gp[:, d:]).astype(BF16)

        @pl.when(i == n_steps - 1)
        def _():
            for g in range(SGU_GROUPS):
                tot = jnp.sum(dbs_ref[g], axis=1, keepdims=True)
                dbs_ref[g] = jnp.broadcast_to(tot, (SGU_CHUNK, LANES))

    vec = pl.BlockSpec((1, d), lambda i: (0, 0))
    cube = pl.BlockSpec((SGU_GROUPS, SGU_CHUNK, SGU_CHUNK), lambda i: (0, 0, 0))
    cube_shape = _sds((SGU_GROUPS, SGU_CHUNK, SGU_CHUNK), F32)
    return pl.pallas_call(
        body, name=name, grid=(n_steps,),
        in_specs=[pl.BlockSpec((ts, width), lambda i: (i, 0)), pl.BlockSpec((ts, d), lambda i: (i, 0)),
                  vec, vec, cube, cube, cube],
        out_specs=[pl.BlockSpec((ts, width), lambda i: (i, 0)), cube, cube, vec, vec],
        out_shape=[_sds((s_len, width), BF16), cube_shape, cube_shape, _sds((1, d), F32), _sds((1, d), F32)],
        scratch_shapes=[pltpu.VMEM((ts, d), F32), pltpu.VMEM((ts, d), F32)],
        compiler_params=_params("arbitrary"),
    )(zpre, dy, ln_g, ln_b, ws, wst, bsb)


def loss_head(x, g, target, name):
    s_len, d = x.shape
    tm = _row_tile(s_len, 512)

    def body(x_ref, g_ref, t_ref, dx_ref, dxb_ref, dg_ref, loss_ref):
        i = pl.program_id(0)
        xf = x_ref[...]
        gain = g_ref[...]
        r = lax.rsqrt(jnp.mean(xf * xf, axis=-1, keepdims=True) + EPS)
        xhat = xf * r
        err = xhat * gain - t_ref[...]
        row = jnp.mean(err * err, axis=-1, keepdims=True)
        part = 0.5 * jnp.sum(row, axis=0, keepdims=True)
        dy = err * (1.0 / d)
        dg_part = jnp.sum(dy * xhat, axis=0, keepdims=True)

        @pl.when(i == 0)
        def _():
            dg_ref[...] = dg_part
            loss_ref[...] = jnp.broadcast_to(part, (8, LANES))

        @pl.when(i > 0)
        def _():
            dg_ref[...] += dg_part
            loss_ref[...] += jnp.broadcast_to(part, (8, LANES))

        dxh = dy * gain
        dx = r * (dxh - xhat * jnp.mean(dxh * xhat, axis=-1, keepdims=True))
        dx_ref[...] = dx
        dxb_ref[...] = dx.astype(BF16)

    row_spec = pl.BlockSpec((tm, d), lambda i: (i, 0))
    vec = pl.BlockSpec((1, d), lambda i: (0, 0))
    return pl.pallas_call(
        body, name=name, grid=(s_len // tm,),
        in_specs=[row_spec, vec, row_spec],
        out_specs=[row_spec, row_spec, vec, pl.BlockSpec((8, LANES), lambda i: (0, 0))],
        out_shape=[_sds((s_len, d), F32), _sds((s_len, d), BF16), _sds((1, d), F32), _sds((8, LANES), F32)],
        compiler_params=_params("arbitrary"),
    )(x, g, target)


FLIP_BITS = {"c": (0, 0, 1), "x": (1, 0, 0), "y": (0, 1, 0), "xy": (1, 1, 0),
             "xc": (1, 0, 1), "yc": (0, 1, 1), "xyc": (1, 1, 1)}
CHIP_FLIPS = ("x", "y", "xy")


def _flip(pos, name):
    return tuple(1 - p if bit else p for p, bit in zip(pos, FLIP_BITS[name]))


def _chip(pos):
    return 2 * pos[0] + pos[1]


def _me():
    return (lax.axis_index("x"), lax.axis_index("y"), lax.axis_index("c"))


def _exchange_copy(remote, k, in_refs, out_refs, send_sems, recv_sems, sender, receiver):
    ii, src_fn, oi, dst_fn, _ = remote[k]
    return pltpu.make_async_remote_copy(
        src_ref=src_fn(in_refs[ii], sender, receiver), dst_ref=dst_fn(out_refs[oi], sender),
        send_sem=send_sems.at[k], recv_sem=recv_sems.at[k], device_id=receiver, device_id_type=MESH)


def _exchange_start(remote, in_refs, out_refs, send_sems, recv_sems):
    me = _me()
    for k in range(len(remote)):
        _exchange_copy(remote, k, in_refs, out_refs, send_sems, recv_sems, me, _flip(me, remote[k][4])).start()


def _exchange_finish(remote, in_refs, out_refs, send_sems, recv_sems):
    me = _me()
    for k in range(len(remote)):
        _exchange_copy(remote, k, in_refs, out_refs, send_sems, recv_sems, _flip(me, remote[k][4]), me).wait_recv()
    for k in range(len(remote)):
        _exchange_copy(remote, k, in_refs, out_refs, send_sems, recv_sems, me, _flip(me, remote[k][4])).wait_send()


def _exchange_sems(remote):
    n = len(remote)
    return [pltpu.SemaphoreType.DMA((n,)), pltpu.SemaphoreType.DMA((n,))] if n else []


def exchange(name, ins, out_shapes, remote, local):
    n_in, n_out = len(ins), len(out_shapes)

    def body(*refs):
        in_refs, out_refs = refs[:n_in], refs[n_in:n_in + n_out]
        send_sems, recv_sems, local_sems = refs[n_in + n_out:]
        me = _me()
        stays = []
        for k, (ii, src_fn, oi, dst_fn) in enumerate(local):
            cp = pltpu.make_async_copy(src_fn(in_refs[ii], me), dst_fn(out_refs[oi], me), local_sems.at[k])
            cp.start()
            stays.append(cp)
        _exchange_start(remote, in_refs, out_refs, send_sems, recv_sems)
        _exchange_finish(remote, in_refs, out_refs, send_sems, recv_sems)
        for cp in stays:
            cp.wait()

    hbm = pl.BlockSpec(memory_space=pl.ANY)
    return pl.pallas_call(
        body, name=name,
        in_specs=[hbm] * n_in, out_specs=[hbm] * n_out, out_shape=list(out_shapes),
        scratch_shapes=[pltpu.SemaphoreType.DMA((max(len(remote), 1),)),
                        pltpu.SemaphoreType.DMA((max(len(remote), 1),)),
                        pltpu.SemaphoreType.DMA((max(len(local), 1),))],
        compiler_params=pltpu.CompilerParams(has_side_effects=True),
    )(*ins)


def staged_push(name, ins, out_shapes, jobs, n_alias=0):
    n_in, n_out = len(ins), len(out_shapes)
    n_copies = sum(len(dsts) for _, _, dsts in jobs)
    n_remote = sum(1 for _, _, dsts in jobs for d in dsts if d[2] is not None)

    def chunk_of(ii, src_fn):
        probe = _ShapeRef(ins[ii].shape, ins[ii].dtype)
        got = src_fn(probe, (0, 0, 0))
        return tuple(got.shape), got.dtype

    classes = []
    for ii, src_fn, _ in jobs:
        c = chunk_of(ii, src_fn)
        if c not in classes:
            classes.append(c)

    def body(*refs):
        in_refs, out_refs = refs[:n_in], refs[n_in:n_in + n_out]
        bufs = refs[n_in + n_out:n_in + n_out + len(classes)]
        load_sem, out_sems, recv_sems = refs[n_in + n_out + len(classes):]
        me = (lax.axis_index("x"), lax.axis_index("y"), lax.axis_index("c"))
        pending = [[[], []] for _ in classes]
        used = [0] * len(classes)
        arrivals = []
        k = r = 0
        for ii, src_fn, dsts in jobs:
            cls = classes.index(chunk_of(ii, src_fn))
            slot = used[cls] % 2
            used[cls] += 1
            for kind, cp in pending[cls][slot]:
                cp.wait_send() if kind == "remote" else cp.wait()
            buf = bufs[cls].at[slot]
            load = pltpu.make_async_copy(src_fn(in_refs[ii], me), buf, load_sem.at[0])
            load.start()
            load.wait()
            sent = []
            for oi, dst_fn, flip in dsts:
                if flip is None:
                    cp = pltpu.make_async_copy(buf, dst_fn(out_refs[oi], me), out_sems.at[k])
                    cp.start()
                    sent.append(("local", cp))
                else:
                    peer = _flip(me, flip)
                    cp = pltpu.make_async_remote_copy(
                        src_ref=buf, dst_ref=dst_fn(out_refs[oi], me), send_sem=out_sems.at[k],
                        recv_sem=recv_sems.at[r], device_id=peer, device_id_type=MESH)
                    cp.start()
                    sent.append(("remote", cp))
                    arrivals.append((r, cls, oi, dst_fn, peer))
                    r += 1
                k += 1
            pending[cls][slot] = sent
        for per_class in pending:
            for slot_list in per_class:
                for kind, cp in slot_list:
                    cp.wait_send() if kind == "remote" else cp.wait()
        for r, cls, oi, dst_fn, peer in arrivals:
            pltpu.make_async_remote_copy(
                src_ref=bufs[cls].at[0], dst_ref=dst_fn(out_refs[oi], peer), send_sem=out_sems.at[0],
                recv_sem=recv_sems.at[r], device_id=peer, device_id_type=MESH).wait_recv()

    hbm = pl.BlockSpec(memory_space=pl.ANY)
    return pl.pallas_call(
        body, name=name,
        in_specs=[hbm] * n_in, out_specs=[hbm] * n_out, out_shape=list(out_shapes),
        scratch_shapes=[pltpu.VMEM((2,) + shape, dtype) for shape, dtype in classes]
        + [pltpu.SemaphoreType.DMA((1,)), pltpu.SemaphoreType.DMA((max(n_copies, 1),)),
           pltpu.SemaphoreType.DMA((max(n_remote, 1),))],
        input_output_aliases={i: i for i in range(n_alias)},
        compiler_params=pltpu.CompilerParams(has_side_effects=True, vmem_limit_bytes=VMEM_LIMIT),
    )(*ins)


class _ShapeRef:
    def __init__(self, shape, dtype):
        self.shape, self.dtype = tuple(shape), dtype

    @property
    def at(self):
        return self

    def __getitem__(self, idx):
        idx = idx if isinstance(idx, tuple) else (idx,)
        shape = []
        for dim, i in zip(self.shape, idx):
            if isinstance(i, slice):
                shape.append(len(range(*i.indices(dim))))
            elif hasattr(i, "size") and hasattr(i, "start"):
                shape.append(i.size)
        shape += self.shape[len(idx):]
        return _ShapeRef(shape, self.dtype)


def gather_whole(shards, name):
    whole = lambda ref, sender, receiver=None: ref
    slot = lambda ref, sender: ref.at[_chip(sender)]
    remote = [(t, whole, t, slot, flip) for t in range(len(shards)) for flip in CHIP_FLIPS]
    local = [(t, whole, t, slot) for t in range(len(shards))]
    outs = [_sds((N_CHIPS,) + a.shape, a.dtype) for a in shards]
    return exchange(name, list(shards), outs, remote, local)


def _half_axis(shape):
    return 0 if shape[0] >= 2 else 1


def gather_halves_plan(shards):
    remote = []
    for t, a in enumerate(shards):
        ax = _half_axis(a.shape)
        half = lambda ref, sender, receiver=None, ax=ax: _half(ref, sender[2], ax)
        slot = lambda ref, sender, ax=ax: _half(ref.at[_chip(sender)], sender[2], ax)
        remote += [(t, half, t, slot, flip) for flip in CHIP_FLIPS]
    outs = [_sds((N_CHIPS,) + a.shape, a.dtype) for a in shards]
    return list(shards), outs, remote


def gather_halves_fill(got, shards, name):
    n_t = len(shards)
    jobs = []
    for t, a in enumerate(shards):
        layers = a.shape[0]
        for l in range(layers):
            jobs.append((n_t + t, lambda ref, me, l=l: ref.at[l],
                         [(t, lambda ref, sender, l=l: ref.at[_chip(sender), l], None)]))
        for flip in CHIP_FLIPS:
            if _half_axis(a.shape) == 0:
                n = layers // 2
                for j in range(n):
                    at = lambda ref, pos, flip=flip, j=j, n=n: ref.at[_chip(_flip(pos, flip)), pos[2] * n + j]
                    jobs.append((t, at, [(t, at, "c")]))
            else:
                rows = a.shape[1] // 2
                at = lambda ref, pos, flip=flip, rows=rows: ref.at[
                    _chip(_flip(pos, flip)), 0, pl.ds(pos[2] * rows, rows)]
                jobs.append((t, at, [(t, at, "c")]))
    outs = [_sds(g.shape, g.dtype) for g in got]
    return staged_push(name, list(got) + list(shards), outs, jobs, n_alias=n_t)


def gather_all(buf, name):
    whole = lambda ref, sender, receiver=None: ref
    slot = lambda ref, sender: ref.at[4 * sender[0] + 2 * sender[1] + sender[2]]
    remote = [(0, whole, 0, slot, flip) for flip in FLIP_BITS]
    local = [(0, whole, 0, slot)]
    return exchange(name, [buf], [_sds((8,) + buf.shape, buf.dtype)], remote, local)[0]


def _half(ref, core, axis):
    rows = ref.shape[axis] // 2
    idx = (slice(None),) * axis + (pl.ds(core * rows, rows),)
    return ref.at[idx]


def reduce_begin(grads, name):
    core = lax.axis_index("c").astype(jnp.int32).reshape(1)
    jobs, outs = [], []
    for t, g in enumerate(grads):
        outs.append(_sds((N_CHIPS, g.shape[1] // 2, g.shape[2]), BF16))
        for s in range(N_CHIPS):
            jobs.append((t, lambda ref, me, s=s: _half(ref.at[s], 1 - me[2], 0),
                         [(t, lambda ref, sender, s=s: ref.at[s], "c")]))
    theirs = staged_push(name + "_swap", grads, outs, jobs)
    chip_sums = [add_half(g, r, core, f"{name}_add{t}") for t, (g, r) in enumerate(zip(grads, theirs))]
    remote, outs = [], []
    for t, p in enumerate(chip_sums):
        outs.append(_sds((len(CHIP_FLIPS),) + p.shape[1:], BF16))
        for f, flip in enumerate(CHIP_FLIPS):
            remote.append((t, lambda ref, sender, receiver: ref.at[_chip(receiver)],
                           t, lambda ref, sender, f=f: ref.at[f], flip))
    return chip_sums, (chip_sums, outs, remote)


def reduce_finish(chip_sums, got, stacks, full_shapes, into, name):
    chip = (2 * lax.axis_index("x") + lax.axis_index("y")).astype(jnp.int32).reshape(1)
    totals = [sum_chips(p, r, chip, f"{name}_sum{t}") for t, (p, r) in enumerate(zip(chip_sums, got))]
    names = []
    for out_name, _ in stacks:
        if out_name not in names:
            names.append(out_name)
    names = [n for n in names if n in into] + [n for n in names if n not in into]
    kept = [into[n] for n in names if n in into]
    outs = [_sds(full_shapes[n], F32) for n in names]
    jobs = []
    for t, (out_name, layer) in enumerate(stacks):
        oi = names.index(out_name)
        rows, cols = totals[t].shape
        pieces = max(1, rows * cols * 4 // STAGE_BYTES)
        step = rows // pieces
        for q in range(pieces):
            src = lambda ref, me, q=q, step=step: ref.at[pl.ds(q * step, step)]
            place = lambda ref, sender, layer=layer, q=q, step=step, rows=rows: ref.at[
                layer, pl.ds(sender[2] * rows + q * step, step)]
            jobs.append((len(kept) + t, src, [(oi, place, None), (oi, place, "c")]))
    full = staged_push(name + "_share", kept + totals, outs, jobs, n_alias=len(kept))
    return {**into, **dict(zip(names, full))}


STAGE_BYTES = 1024 * 1024


def add_half(g, theirs, core, name):
    n_s, rows, cols = g.shape
    half = rows // 2
    tr = _row_tile(half, 256)
    nb = half // tr

    def body(core_ref, g_ref, t_ref, o_ref):
        o_ref[...] = (g_ref[...].astype(F32) + t_ref[...].astype(F32)).astype(BF16)

    return pl.pallas_call(
        body, name=name,
        grid_spec=pltpu.PrefetchScalarGridSpec(
            num_scalar_prefetch=1, grid=(n_s, nb),
            in_specs=[pl.BlockSpec((None, tr, cols), lambda s, i, c: (s, c[0] * nb + i, 0)),
                      pl.BlockSpec((None, tr, cols), lambda s, i, c: (s, i, 0))],
            out_specs=pl.BlockSpec((None, tr, cols), lambda s, i, c: (s, i, 0))),
        out_shape=_sds((n_s, half, cols), BF16),
        compiler_params=_params("arbitrary", "arbitrary"),
    )(core, g, theirs)


def sum_chips(mine, theirs, chip, name):
    _, half, cols = mine.shape
    tr = _row_tile(half, 256)

    def body(chip_ref, m_ref, a_ref, b_ref, c_ref, o_ref):
        o_ref[...] = ((m_ref[...].astype(F32) + a_ref[...].astype(F32))
                      + b_ref[...].astype(F32)) + c_ref[...].astype(F32)

    got = lambda f: pl.BlockSpec((None, tr, cols), lambda i, ch: (f, i, 0))
    return pl.pallas_call(
        body, name=name,
        grid_spec=pltpu.PrefetchScalarGridSpec(
            num_scalar_prefetch=1, grid=(half // tr,),
            in_specs=[pl.BlockSpec((None, tr, cols), lambda i, ch: (ch[0], i, 0)), got(0), got(1), got(2)],
            out_specs=pl.BlockSpec((tr, cols), lambda i, ch: (i, 0))),
        out_shape=_sds((half, cols), F32),
        compiler_params=_params("arbitrary"),
    )(chip, mine, theirs, theirs, theirs)


def _tok(t):
    return t.transpose(1, 0, 2).reshape(t.shape[1], t.shape[0] * t.shape[2])


def _heads(t):
    return t.reshape(t.shape[0], t.shape[1] // HEAD_DIM, HEAD_DIM).transpose(1, 0, 2)


def _tile2(vec):
    return jnp.tile(vec.reshape(1, HEAD_DIM), (1, 2))


REST = ("att_w_in", "att_w_out", "sgu_w_in", "sgu_w_out", "mlp_w1", "mlp_w2")
LAST_GROUP = (("mlp_w2", 0), ("mlp_w1", 0), ("att_w_out", 0), ("att_w_in", 0))


def local_step(x, target, first, rest_shards, rep, full_shapes):
    s_len, d = x.shape
    tabs = rope_tables(s_len)
    depth = rep["mlp_norm"].shape[0]
    row = lambda a: a.reshape(1, -1)
    saved = []
    h = x
    gw = {"att_w_in": [first[0]], "att_w_out": [first[1]]}

    def wl(name, idx):
        return (gw[name][idx], 0) if name.startswith("att") else (gw[name], idx)

    for layer in range(depth):
        i = layer // 2
        tag = f"l{layer}"
        if layer % 2 == 0:
            hn, proj = norm_mm(h, row(rep["att_norm"][i]), *wl("att_w_in", i), F32, tag + "_att_proj")
            qkv = prep_fwd(proj, tabs, _tile2(rep["att_qnorm"][i]), _tile2(rep["att_knorm"][i]), tag + "_att_prep")
            qkv_t = qkv.T
            kv_tok = _heads(jnp.concatenate([qkv[:, KA_COL:KA_COL + 2 * LANES], qkv[:, KB_COL:KB_COL + 2 * LANES]], 1))
            oa, lse_a = flash_fwd_t(qkv_t, kv_tok, QA_COL // GROUP_W, 0, (KA_COL + LANES) // HEAD_DIM,
                                    rep["att_sink"][i], True, tag + "_win_fwd")
            plan = gather_halves_plan(rest_shards) if layer == 0 else None
            ob, lse_b, *got = flash_fwd_t(qkv_t, kv_tok, QB_COL // GROUP_W, 4, (KB_COL + LANES) // HEAD_DIM,
                                          None, False, tag + "_grid_fwd", comm=plan)
            if layer == 0:
                rest = dict(zip(REST, gather_halves_fill(got, rest_shards, "gather_rest_fill")))
                gw["att_w_in"].append(rest.pop("att_w_in"))
                gw["att_w_out"].append(rest.pop("att_w_out"))
                gw.update(rest)
            o_t = jnp.concatenate([oa, ob], axis=0)
            ocat = o_t.T
            out = mm_res(ocat, *wl("att_w_out", i), h, tag + "_att_out")
            mix_saved = (h, hn, proj, qkv_t, kv_tok, o_t, ocat, lse_a, lse_b)
        else:
            hn, zpre = norm_mm(h, row(rep["sgu_norm"][i]), *wl("sgu_w_in", i), F32, tag + "_sgu_in")
            ws = rep["sgu_w_s"][i].astype(BF16)
            bsb = jnp.broadcast_to(rep["sgu_b_s"][i][:, :, None], (SGU_GROUPS, SGU_CHUNK, LANES))
            y = sgu_mid_fwd(zpre, row(rep["sgu_ln_g"][i]), row(rep["sgu_ln_b"][i]), ws, bsb, tag + "_sgu_mid")
            out = mm_res(y, *wl("sgu_w_out", i), h, tag + "_sgu_out")
            mix_saved = (h, hn, zpre, y, ws, bsb)
        hm, a = norm_mm(out, row(rep["mlp_norm"][layer]), *wl("mlp_w1", layer), BF16, tag + "_mlp_up")
        nxt = mm_res(a, *wl("mlp_w2", layer), out, tag + "_mlp_down", relu2=True)
        saved.append((mix_saved, (out, hm, a)))
        h = nxt
    dh, dhb, d_final, loss_tile = loss_head(h, row(rep["final_norm"]), target, "loss_head")
    big, tags = [], []
    small = {k: [None] * v.shape[0] for k, v in rep.items() if k != "final_norm"}
    small["final_norm"] = d_final.reshape(-1)
    for layer in reversed(range(depth)):
        i = layer // 2
        tag = f"l{layer}"
        mix_saved, (xin, hm, a) = saved[layer]
        da = mm_nt_relu2_bwd(dhb, *wl("mlp_w2", layer), a, tag + "_mlp_down_bwd")
        big.append(dw_mm(a, dhb, tag + "_mlp_dw2", col_sharded=False, relu2=True))
        tags.append(("mlp_w2", layer))
        big.append(dw_mm(hm, da, tag + "_mlp_dw1", col_sharded=True))
        tags.append(("mlp_w1", layer))
        dh, dhb, dg = dx_norm(da, *wl("mlp_w1", layer), xin, row(rep["mlp_norm"][layer]), dh, tag + "_mlp_up_bwd")
        small["mlp_norm"][layer] = dg.reshape(-1)
        if layer % 2 == 0:
            xin, hn, proj, qkv_t, kv_tok, o_t, ocat, lse_a, lse_b = mix_saved
            docat = mm_nt(dhb, *wl("att_w_out", i), tag + "_att_out_bwd")
            big.append(dw_mm(ocat, dhb, tag + "_att_dwout", col_sharded=False))
            tags.append(("att_w_out", i))
            do_t = docat.T
            dqa, dka, dva, dsink = flash_bwd_t(qkv_t, kv_tok, o_t, do_t, lse_a, QA_COL // GROUP_W, 0, 2,
                                               KA_COL // HEAD_DIM, 0, rep["att_sink"][i], True, tag + "_win_bwd")
            plan = None
            if layer == 0:
                early = [k for k, t in enumerate(tags) if t not in LAST_GROUP]
                chip_sums, plan = reduce_begin([big[k] for k in early], "grads1")
            dqb, dkb, dvb, *got = flash_bwd_t(qkv_t, kv_tok, o_t, do_t, lse_b, QB_COL // GROUP_W, 4, 6,
                                              KB_COL // HEAD_DIM, 2, None, False, tag + "_grid_bwd", comm=plan)
            if layer == 0:
                grads = reduce_finish(chip_sums, got, [tags[k] for k in early], full_shapes, {}, "grads1")
            qg, kg = _tile2(rep["att_qnorm"][i]), _tile2(rep["att_knorm"][i])
            q_tok = lambda t: t.transpose(0, 2, 1).reshape(s_len, t.shape[1])
            dproj, dqg, dkg = prep_bwd(proj, q_tok(dqa), _tok(dka), _tok(dva), q_tok(dqb), _tok(dkb), _tok(dvb),
                                       tabs, qg, kg, tag + "_att_prep_bwd")
            big.append(dw_mm(hn, dproj, tag + "_att_dwin", col_sharded=True))
            tags.append(("att_w_in", i))
            dh, dhb, dg = dx_norm(dproj, *wl("att_w_in", i), xin, row(rep["att_norm"][i]), dh, tag + "_att_proj_bwd")
            small["att_norm"][i] = dg.reshape(-1)
            small["att_sink"][i] = dsink[:, 0, :GROUP].reshape(-1)
            small["att_qnorm"][i] = dqg[0, :HEAD_DIM] + dqg[0, HEAD_DIM:]
            small["att_knorm"][i] = dkg[0, :HEAD_DIM] + dkg[0, HEAD_DIM:]
        else:
            xin, hn, zpre, y, ws, bsb = mix_saved
            dy = mm_nt(dhb, *wl("sgu_w_out", i), tag + "_sgu_out_bwd")
            big.append(dw_mm(y, dhb, tag + "_sgu_dwout", col_sharded=False))
            tags.append(("sgu_w_out", i))
            wst = ws.transpose(0, 2, 1)
            dz, dws, dbs, dlg, dlb = sgu_mid_bwd(zpre, dy, row(rep["sgu_ln_g"][i]), row(rep["sgu_ln_b"][i]),
                                                 ws, wst, bsb, tag + "_sgu_mid_bwd")
            big.append(dw_mm(hn, dz, tag + "_sgu_dwin", col_sharded=True))
            tags.append(("sgu_w_in", i))
            dh, dhb, dg = dx_norm(dz, *wl("sgu_w_in", i), xin, row(rep["sgu_norm"][i]), dh, tag + "_sgu_in_bwd")
            small["sgu_norm"][i] = dg.reshape(-1)
            small["sgu_ln_g"][i] = dlg.reshape(-1)
            small["sgu_ln_b"][i] = dlb.reshape(-1)
            small["sgu_w_s"][i] = dws
            small["sgu_b_s"][i] = dbs[:, :, 0]
    small = {k: (v if k == "final_norm" else jnp.stack(v)) for k, v in small.items()}
    late = [k for k, t in enumerate(tags) if t in LAST_GROUP]
    chip_sums, plan = reduce_begin([big[k] for k in late], "grads2")
    got = exchange("grads2_scatter", *plan, [])
    grads = reduce_finish(chip_sums, got, [tags[k] for k in late], full_shapes, grads, "grads2")
    return loss_tile, dh, grads, small


BIG = ("att_w_in", "att_w_out", "sgu_w_in", "sgu_w_out", "mlp_w1", "mlp_w2")
SHARDED_VEC = ("sgu_norm", "sgu_ln_g", "sgu_ln_b")
REPLICATED = ("att_norm", "att_sink", "att_qnorm", "att_knorm", "sgu_w_s", "sgu_b_s", "mlp_norm", "final_norm")
WEIGHTS = ("att_norm", "att_w_in", "att_sink", "att_qnorm", "att_knorm", "att_w_out", "sgu_norm", "sgu_w_in",
           "sgu_ln_g", "sgu_ln_b", "sgu_w_s", "sgu_b_s", "sgu_w_out", "mlp_norm", "mlp_w1", "mlp_w2", "final_norm")
SMALL = tuple(n for n in WEIGHTS if n not in BIG)
PACK_ALIGN = 8 * LANES


def _pack_small(small, loss_tile):
    parts = [small[n].reshape(-1) for n in SMALL] + [loss_tile[0, :1]]
    flat = jnp.concatenate(parts)
    pad = -flat.shape[0] % PACK_ALIGN
    return jnp.pad(flat, (0, pad)).reshape(-1, LANES)


def _unpack_small(flat2d, shapes):
    flat = flat2d.reshape(-1)
    out, off = {}, 0
    for n in SMALL:
        size = int(np.prod(shapes[n]))
        out[n] = flat[off:off + size].reshape(shapes[n])
        off += size
    return out, flat[off]


def kernel(x, att_norm, att_w_in, att_sink, att_qnorm, att_knorm, att_w_out, sgu_norm, sgu_w_in, sgu_ln_g, sgu_ln_b, sgu_w_s, sgu_b_s, sgu_w_out, mlp_norm, mlp_w1, mlp_w2, final_norm, loss_target, m_att_norm, m_att_w_in, m_att_sink, m_att_qnorm, m_att_knorm, m_att_w_out, m_sgu_norm, m_sgu_w_in, m_sgu_ln_g, m_sgu_ln_b, m_sgu_w_s, m_sgu_b_s, m_sgu_w_out, m_mlp_norm, m_mlp_w1, m_mlp_w2, m_final_norm, v_att_norm, v_att_w_in, v_att_sink, v_att_qnorm, v_att_knorm, v_att_w_out, v_sgu_norm, v_sgu_w_in, v_sgu_ln_g, v_sgu_ln_b, v_sgu_w_s, v_sgu_b_s, v_sgu_w_out, v_mlp_norm, v_mlp_w1, v_mlp_w2, v_final_norm):
    w = dict(att_norm=att_norm, att_w_in=att_w_in, att_sink=att_sink, att_qnorm=att_qnorm, att_knorm=att_knorm,
             att_w_out=att_w_out, sgu_norm=sgu_norm, sgu_w_in=sgu_w_in, sgu_ln_g=sgu_ln_g, sgu_ln_b=sgu_ln_b,
             sgu_w_s=sgu_w_s, sgu_b_s=sgu_b_s, sgu_w_out=sgu_w_out, mlp_norm=mlp_norm, mlp_w1=mlp_w1,
             mlp_w2=mlp_w2, final_norm=final_norm)
    m = dict(att_norm=m_att_norm, att_w_in=m_att_w_in, att_sink=m_att_sink, att_qnorm=m_att_qnorm,
             att_knorm=m_att_knorm, att_w_out=m_att_w_out, sgu_norm=m_sgu_norm, sgu_w_in=m_sgu_w_in,
             sgu_ln_g=m_sgu_ln_g, sgu_ln_b=m_sgu_ln_b, sgu_w_s=m_sgu_w_s, sgu_b_s=m_sgu_b_s,
             sgu_w_out=m_sgu_w_out, mlp_norm=m_mlp_norm, mlp_w1=m_mlp_w1, mlp_w2=m_mlp_w2,
             final_norm=m_final_norm)
    v = dict(att_norm=v_att_norm, att_w_in=v_att_w_in, att_sink=v_att_sink, att_qnorm=v_att_qnorm,
             att_knorm=v_att_knorm, att_w_out=v_att_w_out, sgu_norm=v_sgu_norm, sgu_w_in=v_sgu_w_in,
             sgu_ln_g=v_sgu_ln_g, sgu_ln_b=v_sgu_ln_b, sgu_w_s=v_sgu_w_s, sgu_b_s=v_sgu_b_s,
             sgu_w_out=v_sgu_w_out, mlp_norm=v_mlp_norm, mlp_w1=v_mlp_w1, mlp_w2=v_mlp_w2,
             final_norm=v_final_norm)
    chip = 2 * lax.axis_index("x") + lax.axis_index("y")

    vecs = jnp.stack([w[n] for n in SHARDED_VEC])
    wb = {n: w[n].astype(BF16) for n in BIG}
    *first, vec_all = gather_whole([wb["att_w_in"][0:1], wb["att_w_out"][0:1], vecs], "gather_first")
    rest_shards = [wb[n][1:2] if n.startswith("att") else wb[n] for n in REST]
    vec_full = vec_all.transpose(1, 2, 0, 3).reshape(vecs.shape[0], vecs.shape[1], -1)
    rep = {n: w[n] for n in REPLICATED}
    rep.update({n: vec_full[k] for k, n in enumerate(SHARDED_VEC)})

    loss_tile, grad_x, grads, small = local_step(x[0], loss_target[0], first, rest_shards, rep,
                                                 {n: w[n].shape for n in BIG})
    packed = _pack_small(small, loss_tile)
    everyone = gather_all(packed, "gather_small")
    add8 = lambda *a: (((a[0] + a[1]) + (a[2] + a[3])) + ((a[4] + a[5]) + (a[6] + a[7])),)
    total = ew(add8, [everyone[k] for k in range(8)], [F32], "sum_small")[0]
    small_g, loss = _unpack_small(total, {n: small[n].shape for n in SMALL})
    width = w["sgu_norm"].shape[1]
    for n in SHARDED_VEC:
        small_g[n] = lax.dynamic_slice_in_dim(small_g[n], chip * width, width, axis=1)
    grads.update(small_g)
    for n in BIG:
        grads[n] = grads[n].reshape(w[n].shape)

    delta, new_m, new_v = {}, {}, {}
    for n in WEIGHTS:
        shape = w[n].shape
        two_d = (lambda a: a.reshape(1, -1)) if len(shape) == 1 else (lambda a: a)
        dn, mn, vn = adamw(two_d(w[n]), two_d(grads[n]), two_d(m[n]), two_d(v[n]), "adamw_" + n)
        delta[n], new_m[n], new_v[n] = dn.reshape(shape), mn.reshape(shape), vn.reshape(shape)
    return (loss, grad_x[None], *[grads[n] for n in WEIGHTS], *[delta[n] for n in WEIGHTS],
            *[new_m[n] for n in WEIGHTS], *[new_v[n] for n in WEIGHTS])
```

```python
import numpy as np
import jax
import jax.numpy as jnp
from jax import lax
from jax.experimental import pallas as pl
from jax.experimental.pallas import tpu as pltpu

F32 = jnp.float32
BF16 = jnp.bfloat16
MESH = pl.DeviceIdType.MESH

EPS = 1e-6
HEAD_DIM = 64
BLOCK = 128
GRID_W = 64
ROPE_THETA = 10000.0
N_CHIPS = 4
LANES = 128
V7X_VMEM_BYTES = 64 * 1024 * 1024
VMEM_LIMIT = V7X_VMEM_BYTES - 8 * 1024 * 1024

ADAM_LR = 0.001
ADAM_B1 = 0.9
ADAM_B2 = 0.999
ADAM_EPS = 1e-08
ADAM_WD = 0.01
ADAM_STEP = 10

NT_DIMS = (((1,), (1,)), ((), ()))
TN_DIMS = (((0,), (0,)), ((), ()))


def _params(*sem):
    return pltpu.CompilerParams(dimension_semantics=sem, vmem_limit_bytes=VMEM_LIMIT)


def _sds(shape, dtype):
    return jax.ShapeDtypeStruct(tuple(shape), dtype)


def _row_tile(rows, want):
    t = min(rows, want)
    assert rows % t == 0, (rows, want)
    return t


def norm_mm(x, g, w4, layer, out_dtype, name):
    s_len, d = x.shape
    ns = w4.shape[-1]
    tm = _row_tile(s_len, 512)

    def body(x_ref, g_ref, w_ref, h_ref, y_ref):
        xf = x_ref[...]
        r = lax.rsqrt(jnp.mean(xf * xf, axis=-1, keepdims=True) + EPS)
        h = ((xf * r) * g_ref[...]).astype(BF16)
        h_ref[...] = h
        for s in range(N_CHIPS):
            y_ref[:, s * ns:(s + 1) * ns] = jnp.dot(h, w_ref[s], preferred_element_type=F32).astype(y_ref.dtype)

    return pl.pallas_call(
        body, name=name, grid=(s_len // tm,),
        in_specs=[pl.BlockSpec((tm, d), lambda i: (i, 0)),
                  pl.BlockSpec((1, d), lambda i: (0, 0)),
                  pl.BlockSpec((N_CHIPS, None, d, ns), lambda i: (0, layer, 0, 0))],
        out_specs=[pl.BlockSpec((tm, d), lambda i: (i, 0)),
                   pl.BlockSpec((tm, N_CHIPS * ns), lambda i: (i, 0))],
        out_shape=[_sds((s_len, d), BF16), _sds((s_len, N_CHIPS * ns), out_dtype)],
        compiler_params=_params("arbitrary"),
    )(x, g, w4)


def mm_res(a, w4, layer, res, name, relu2=False):
    s_len, k = a.shape
    kq, n = w4.shape[-2:]
    assert kq * N_CHIPS == k
    tm = _row_tile(s_len, 256 if k > 1024 else 512)

    def body(a_ref, w0, w1, w2, w3, r_ref, o_ref):
        acc = r_ref[...]
        for s, w_ref in enumerate((w0, w1, w2, w3)):
            av = a_ref[:, s * kq:(s + 1) * kq]
            if relu2:
                t = jnp.maximum(av.astype(F32), 0.0)
                av = (t * t).astype(BF16)
            acc = acc + jnp.dot(av, w_ref[...], preferred_element_type=F32)
        o_ref[...] = acc

    def wspec(s):
        return pl.BlockSpec((None, None, kq, n), lambda i: (s, layer, 0, 0))

    return pl.pallas_call(
        body, name=name, grid=(s_len // tm,),
        in_specs=[pl.BlockSpec((tm, k), lambda i: (i, 0)), wspec(0), wspec(1), wspec(2), wspec(3),
                  pl.BlockSpec((tm, n), lambda i: (i, 0))],
        out_specs=pl.BlockSpec((tm, n), lambda i: (i, 0)),
        out_shape=_sds((s_len, n), F32),
        compiler_params=_params("arbitrary"),
    )(a, w4, w4, w4, w4, res)


def mm_res_t(pieces, w4, layer, res, name):
    s_len = res.shape[0]
    kq, n = w4.shape[-2:]
    rows = pieces[0].shape[0]
    assert rows % kq == 0 and rows * len(pieces) == kq * N_CHIPS
    tm = _row_tile(s_len, 512)
    n_p = len(pieces)

    def body(*refs):
        p_refs, w_refs, (r_ref, o_ref) = refs[:n_p], refs[n_p:n_p + N_CHIPS], refs[n_p + N_CHIPS:]
        acc = r_ref[...]
        for s in range(N_CHIPS):
            p, off = divmod(s * kq, rows)
            acc = acc + lax.dot_general(p_refs[p][off:off + kq, :], w_refs[s][...], TN_DIMS,
                                        preferred_element_type=F32)
        o_ref[...] = acc

    def wspec(s):
        return pl.BlockSpec((None, None, kq, n), lambda i: (s, layer, 0, 0))

    return pl.pallas_call(
        body, name=name, grid=(s_len // tm,),
        in_specs=[pl.BlockSpec((rows, tm), lambda i: (0, i))] * n_p + [wspec(s) for s in range(N_CHIPS)]
        + [pl.BlockSpec((tm, n), lambda i: (i, 0))],
        out_specs=pl.BlockSpec((tm, n), lambda i: (i, 0)),
        out_shape=_sds((s_len, n), F32),
        compiler_params=_params("arbitrary"),
    )(*pieces, w4, w4, w4, w4, res)


def dw_nn(pieces, b, name):
    s_len, n = b.shape
    rows = pieces[0].shape[0]
    n_p = len(pieces)
    k = rows * n_p
    ts = _row_tile(s_len, 2048)
    n_s = s_len // ts

    def body(*refs):
        p_refs, (b_ref, o_ref, acc_ref) = refs[:n_p], refs[n_p:]
        s = pl.program_id(0)
        bv = b_ref[...]
        for p in range(n_p):
            part = jnp.dot(p_refs[p][...], bv, preferred_element_type=F32)
            at = slice(p * rows, (p + 1) * rows)
            if n_s == 1:
                o_ref[at, :] = part.astype(BF16)
                continue

            @pl.when(s == 0)
            def _():
                acc_ref[at, :] = part

            @pl.when((s > 0) & (s < n_s - 1))
            def _():
                acc_ref[at, :] += part

            @pl.when(s == n_s - 1)
            def _():
                o_ref[at, :] = (acc_ref[at, :] + part).astype(BF16)

    out = pl.pallas_call(
        body, name=name, grid=(n_s,),
        in_specs=[pl.BlockSpec((rows, ts), lambda s: (0, s))] * n_p + [pl.BlockSpec((ts, n), lambda s: (s, 0))],
        out_specs=pl.BlockSpec((k, n), lambda s: (0, 0)), out_shape=_sds((k, n), BF16),
        scratch_shapes=[pltpu.VMEM((k, n), F32)],
        compiler_params=_params("arbitrary"),
    )(*pieces, b)
    return out.reshape(N_CHIPS, k // N_CHIPS, n)


def mm_nt(dy, w4, layer, name, transposed=False):
    s_len, n = dy.shape
    mq = w4.shape[-2]
    tm = _row_tile(s_len, 512)

    def body(d_ref, w0, w1, w2, w3, o_ref):
        dv = d_ref[...]
        for s, w_ref in enumerate((w0, w1, w2, w3)):
            if transposed:
                o_ref[s * mq:(s + 1) * mq, :] = lax.dot_general(
                    w_ref[...], dv, NT_DIMS, preferred_element_type=F32).astype(BF16)
            else:
                o_ref[:, s * mq:(s + 1) * mq] = lax.dot_general(
                    dv, w_ref[...], NT_DIMS, preferred_element_type=F32).astype(BF16)

    def wspec(s):
        return pl.BlockSpec((None, None, mq, n), lambda i: (s, layer, 0, 0))

    m = N_CHIPS * mq
    return pl.pallas_call(
        body, name=name, grid=(s_len // tm,),
        in_specs=[pl.BlockSpec((tm, n), lambda i: (i, 0)), wspec(0), wspec(1), wspec(2), wspec(3)],
        out_specs=pl.BlockSpec((m, tm), lambda i: (0, i)) if transposed else pl.BlockSpec((tm, m), lambda i: (i, 0)),
        out_shape=_sds((m, s_len) if transposed else (s_len, m), BF16),
        compiler_params=_params("arbitrary"),
    )(dy, w4, w4, w4, w4)


def mm_nt_relu2_bwd(dy, w4, layer, a, name):
    s_len, n = dy.shape
    mq = w4.shape[-2]
    tm = _row_tile(s_len, 512)

    def body(d_ref, w_ref, a_ref, o_ref):
        dv = d_ref[...]
        for s in range(N_CHIPS):
            cols = slice(s * mq, (s + 1) * mq)
            dz = lax.dot_general(dv, w_ref[s], NT_DIMS, preferred_element_type=F32)
            o_ref[:, cols] = (dz * (2.0 * jnp.maximum(a_ref[:, cols].astype(F32), 0.0))).astype(BF16)

    return pl.pallas_call(
        body, name=name, grid=(s_len // tm,),
        in_specs=[pl.BlockSpec((tm, n), lambda i: (i, 0)),
                  pl.BlockSpec((N_CHIPS, None, mq, n), lambda i: (0, layer, 0, 0)),
                  pl.BlockSpec((tm, N_CHIPS * mq), lambda i: (i, 0))],
        out_specs=pl.BlockSpec((tm, N_CHIPS * mq), lambda i: (i, 0)),
        out_shape=_sds((s_len, N_CHIPS * mq), BF16),
        compiler_params=_params("arbitrary"),
    )(dy, w4, a)


def dx_norm(dy, w4, layer, x, g, dres, name):
    s_len, d = x.shape
    ns = w4.shape[-1]
    tm = _row_tile(s_len, 512)

    def body(dy_ref, w_ref, x_ref, g_ref, dr_ref, dx_ref, dxb_ref, dg_ref):
        i = pl.program_id(0)
        dh = lax.dot_general(dy_ref[:, 0:ns], w_ref[0], NT_DIMS, preferred_element_type=F32)
        for s in range(1, N_CHIPS):
            dh = dh + lax.dot_general(dy_ref[:, s * ns:(s + 1) * ns], w_ref[s], NT_DIMS,
                                      preferred_element_type=F32)
        xf = x_ref[...]
        r = lax.rsqrt(jnp.mean(xf * xf, axis=-1, keepdims=True) + EPS)
        xhat = xf * r
        dg_part = jnp.sum(dh * xhat, axis=0, keepdims=True)

        @pl.when(i == 0)
        def _():
            dg_ref[...] = dg_part

        @pl.when(i > 0)
        def _():
            dg_ref[...] += dg_part

        dxh = dh * g_ref[...]
        dx = dr_ref[...] + r * (dxh - xhat * jnp.mean(dxh * xhat, axis=-1, keepdims=True))
        dx_ref[...] = dx
        dxb_ref[...] = dx.astype(BF16)

    row = pl.BlockSpec((tm, d), lambda i: (i, 0))
    vec = pl.BlockSpec((1, d), lambda i: (0, 0))
    return pl.pallas_call(
        body, name=name, grid=(s_len // tm,),
        in_specs=[pl.BlockSpec((tm, N_CHIPS * ns), lambda i: (i, 0)),
                  pl.BlockSpec((N_CHIPS, None, d, ns), lambda i: (0, layer, 0, 0)), row, vec, row],
        out_specs=[row, row, vec],
        out_shape=[_sds((s_len, d), F32), _sds((s_len, d), BF16), _sds((1, d), F32)],
        compiler_params=_params("arbitrary"),
    )(dy, w4, x, g, dres)


def dw_mm(a, b, name, col_sharded, relu2=False):
    s_len, k = a.shape
    n = b.shape[1]
    ts = _row_tile(s_len, 2048)
    tk = min(k, 1024)
    tn = n // N_CHIPS if col_sharded else min(n, 1024)
    n_s = s_len // ts

    def body(a_ref, b_ref, o_ref, acc_ref):
        s = pl.program_id(2)
        av = a_ref[...]
        if relu2:
            t = jnp.maximum(av.astype(F32), 0.0)
            av = (t * t).astype(BF16)
        part = lax.dot_general(av, b_ref[...], TN_DIMS, preferred_element_type=F32)
        if n_s == 1:
            o_ref[...] = part.astype(BF16)
            return

        @pl.when(s == 0)
        def _():
            acc_ref[...] = part

        @pl.when((s > 0) & (s < n_s - 1))
        def _():
            acc_ref[...] += part

        @pl.when(s == n_s - 1)
        def _():
            o_ref[...] = (acc_ref[...] + part).astype(BF16)

    if col_sharded:
        out_shape = _sds((N_CHIPS, k, tn), BF16)
        out_spec = pl.BlockSpec((None, tk, tn), lambda i, j, s: (j, i, 0))
    else:
        out_shape = _sds((N_CHIPS, k // N_CHIPS, n), BF16)
        rows_per = k // N_CHIPS
        assert tk % rows_per == 0 or rows_per % tk == 0
        if tk >= rows_per:
            out_shape = _sds((k, n), BF16)
            out_spec = pl.BlockSpec((tk, tn), lambda i, j, s: (i, j))
        else:
            per = rows_per // tk
            out_spec = pl.BlockSpec((None, tk, tn), lambda i, j, s: (i // per, i % per, j))

    out = pl.pallas_call(
        body, name=name, grid=(k // tk, n // tn, n_s),
        in_specs=[pl.BlockSpec((ts, tk), lambda i, j, s: (s, i)),
                  pl.BlockSpec((ts, tn), lambda i, j, s: (s, j))],
        out_specs=out_spec, out_shape=out_shape,
        scratch_shapes=[pltpu.VMEM((tk, tn), F32)],
        compiler_params=_params("arbitrary", "arbitrary", "arbitrary"),
    )(a, b)
    if not col_sharded:
        out = out.reshape(N_CHIPS, k // N_CHIPS, n)
    return out


def ew(fn, ins, out_dtypes, name, tile_rows=256):
    rows, cols = ins[0].shape
    for a in ins:
        assert a.shape == (rows, cols), (name, a.shape, rows, cols)
    tr = rows if (rows <= tile_rows or rows % tile_rows) else tile_rows
    n_in = len(ins)

    def body(*refs):
        outs = fn(*[r[...] for r in refs[:n_in]])
        for o_ref, val in zip(refs[n_in:], outs):
            o_ref[...] = val.astype(o_ref.dtype)

    spec = pl.BlockSpec((tr, cols), lambda i: (i, 0))
    return pl.pallas_call(
        body, name=name, grid=(rows // tr,),
        in_specs=[spec] * n_in, out_specs=[spec] * len(out_dtypes),
        out_shape=[_sds((rows, cols), dt) for dt in out_dtypes],
        compiler_params=_params("arbitrary"),
    )(*ins)


def adamw(w, g, m, v, name):
    shape = w.shape
    cols = shape[-1]
    two_d = lambda a: a.reshape(-1, cols)

    def fn(wv, gv, mv, vv):
        m_new = ADAM_B1 * mv + (1.0 - ADAM_B1) * gv
        v_new = ADAM_B2 * vv + (1.0 - ADAM_B2) * (gv * gv)
        m_hat = m_new / (1.0 - ADAM_B1 ** ADAM_STEP)
        v_hat = v_new / (1.0 - ADAM_B2 ** ADAM_STEP)
        delta = -ADAM_LR * (m_hat / (jnp.sqrt(v_hat) + ADAM_EPS) + ADAM_WD * wv)
        return delta, m_new, v_new

    d, mn, vn = ew(fn, [two_d(w), two_d(g), two_d(m), two_d(v)], [F32, F32, F32], name)
    return d.reshape(shape), mn.reshape(shape), vn.reshape(shape)


def rope_tables(s_len):
    def angles(pos, dim):
        freqs = ROPE_THETA ** (-jnp.arange(0, dim, 2, dtype=F32) / dim)
        ang = pos.astype(F32)[:, None] * freqs[None, :]
        return jnp.cos(ang), jnp.sin(ang)

    pos = jnp.arange(s_len)
    rows = s_len // GRID_W
    row_idx = jnp.repeat(jnp.arange(rows), GRID_W)
    col_idx = jnp.tile(jnp.arange(GRID_W), rows)
    c1, s1 = angles(pos, HEAD_DIM)
    cr, sr = angles(row_idx, HEAD_DIM // 2)
    cc, sc = angles(col_idx, HEAD_DIM // 2)
    cos1 = jnp.tile(jnp.concatenate([c1, c1], -1), (1, 2))
    sin1 = jnp.tile(jnp.concatenate([-s1, s1], -1), (1, 2))
    cos2 = jnp.tile(jnp.concatenate([cr, cr, cc, cc], -1), (1, 2))
    sin2 = jnp.tile(jnp.concatenate([-sr, sr, -sc, sc], -1), (1, 2))
    return cos1, sin1, cos2, sin2


def _lane_iota(rows):
    return lax.broadcasted_iota(jnp.int32, (rows, LANES), 1)


def _swap(x, dist, lane):
    return jnp.where((lane & dist) != 0, pltpu.roll(x, dist, 1), pltpu.roll(x, LANES - dist, 1))


def _head_ones():
    r = lax.broadcasted_iota(jnp.int32, (LANES, LANES), 0) // HEAD_DIM
    c = lax.broadcasted_iota(jnp.int32, (LANES, LANES), 1) // HEAD_DIM
    return (r == c).astype(BF16)


def _head_sum(t, ones):
    hi = t.astype(BF16)
    lo = (t - hi.astype(F32)).astype(BF16)
    return (jnp.dot(hi, ones, preferred_element_type=F32) + jnp.dot(lo, ones, preferred_element_type=F32))


Q_SCALE = HEAD_DIM ** -0.5
LOG2E = 1.4426950408889634
LN2 = 0.6931471805599453
CHUNK_KIND = ["qa"] * 4 + ["ka", "va"] + ["qb"] * 4 + ["kb", "vb"]
QA_COL, KA_COL, QB_COL, KB_COL = 0, 512, 768, 1280


def prep_fwd(proj, tabs, qn_g, kn_g, name):
    s_len, width = proj.shape
    ts = _row_tile(s_len, 512)
    cos1, sin1, cos2, sin2 = tabs

    def body(p_ref, c1_ref, s1_ref, c2_ref, s2_ref, qg_ref, kg_ref, o_ref, kv_ref):
        lane = _lane_iota(ts)
        ones = _head_ones()
        c1, s1, c2, s2 = c1_ref[...], s1_ref[...], c2_ref[...], s2_ref[...]
        n_kv = 0
        for cb, kind in enumerate(CHUNK_KIND):
            x = p_ref[:, cb * LANES:(cb + 1) * LANES]
            if kind in ("qa", "ka"):
                y = x * c1 + _swap(x, 32, lane) * s1
            elif kind in ("qb", "kb"):
                gain = qg_ref[...] if kind == "qb" else kg_ref[...]
                ms = _head_sum(x * x, ones) * (1.0 / HEAD_DIM)
                xn = (x * lax.rsqrt(ms + EPS)) * gain
                y = xn * c2 + _swap(xn, 16, lane) * s2
            else:
                y = x
            if kind in ("qa", "qb"):
                y = y * (Q_SCALE * LOG2E)
            else:
                kv_ref[:, n_kv * LANES:(n_kv + 1) * LANES] = y.astype(BF16)
                n_kv += 1
            o_ref[cb * LANES:(cb + 1) * LANES, :] = y.T.astype(BF16)

    tab = pl.BlockSpec((ts, LANES), lambda i: (i, 0))
    vec = pl.BlockSpec((1, LANES), lambda i: (0, 0))
    return pl.pallas_call(
        body, name=name, grid=(s_len // ts,),
        in_specs=[pl.BlockSpec((ts, width), lambda i: (i, 0)), tab, tab, tab, tab, vec, vec],
        out_specs=[pl.BlockSpec((width, ts), lambda i: (0, i)), pl.BlockSpec((ts, 4 * LANES), lambda i: (i, 0))],
        out_shape=[_sds((width, s_len), BF16), _sds((s_len, 4 * LANES), BF16)],
        compiler_params=_params("arbitrary"),
    )(proj, cos1, sin1, cos2, sin2, qn_g, kn_g)


def prep_bwd(proj, dqa, dka, dva, dqb, dkb, dvb, tabs, qn_g, kn_g, name):
    s_len, width = proj.shape
    ts = _row_tile(s_len, 256)
    cos1, sin1, cos2, sin2 = tabs

    def body(p_ref, dqa_ref, dka_ref, dva_ref, dqb_ref, dkb_ref, dvb_ref,
             c1_ref, s1_ref, c2_ref, s2_ref, qg_ref, kg_ref, o_ref, dqg_ref, dkg_ref):
        i = pl.program_id(0)
        lane = _lane_iota(ts)
        c1, s1, c2, s2 = c1_ref[...], s1_ref[...], c2_ref[...], s2_ref[...]

        def rope_t(dy, cos, sin, dist):
            return dy * cos + _swap(dy * sin, dist, lane)

        ones = _head_ones()

        def norm_bwd(dy, x, gain):
            r = lax.rsqrt(_head_sum(x * x, ones) * (1.0 / HEAD_DIM) + EPS)
            xhat = x * r
            dgain = jnp.sum(dy * xhat, axis=0, keepdims=True)
            dxh = dy * gain
            dx = r * (dxh - xhat * (_head_sum(dxh * xhat, ones) * (1.0 / HEAD_DIM)))
            return dx, dgain

        dqg = jnp.zeros((1, LANES), F32)
        dkg = jnp.zeros((1, LANES), F32)
        for cb, kind in enumerate(CHUNK_KIND):
            cols = slice(cb * LANES, (cb + 1) * LANES)
            if kind == "qa":
                dx = rope_t(dqa_ref[cols, :].T * Q_SCALE, c1, s1, 32)
            elif kind == "ka":
                dx = rope_t(dka_ref[...], c1, s1, 32)
            elif kind == "va":
                dx = dva_ref[...]
            elif kind == "qb":
                qcols = slice((cb - 6) * LANES, (cb - 5) * LANES)
                dy = rope_t(dqb_ref[qcols, :].T * Q_SCALE, c2, s2, 16)
                dx, dgain = norm_bwd(dy, p_ref[:, cols], qg_ref[...])
                dqg = dqg + dgain
            elif kind == "kb":
                dy = rope_t(dkb_ref[...], c2, s2, 16)
                dx, dgain = norm_bwd(dy, p_ref[:, cols], kg_ref[...])
                dkg = dkg + dgain
            else:
                dx = dvb_ref[...]
            o_ref[:, cols] = dx.astype(BF16)

        @pl.when(i == 0)
        def _():
            dqg_ref[...] = dqg
            dkg_ref[...] = dkg

        @pl.when(i > 0)
        def _():
            dqg_ref[...] += dqg
            dkg_ref[...] += dkg

    tab = pl.BlockSpec((ts, LANES), lambda i: (i, 0))
    vec = pl.BlockSpec((1, LANES), lambda i: (0, 0))

    def dq_spec(dq):
        per = dq.shape[2] // ts
        return pl.BlockSpec((None, 4 * LANES, ts), lambda i: (i // per, 0, i % per))

    return pl.pallas_call(
        body, name=name, grid=(s_len // ts,),
        in_specs=([pl.BlockSpec((ts, width), lambda i: (i, 0)), dq_spec(dqa), tab, tab, dq_spec(dqb), tab, tab]
                  + [tab] * 4 + [vec, vec]),
        out_specs=[pl.BlockSpec((ts, width), lambda i: (i, 0)), vec, vec],
        out_shape=[_sds((s_len, width), BF16), _sds((1, LANES), F32), _sds((1, LANES), F32)],
        compiler_params=_params("arbitrary"),
    )(proj, dqa, dka, dva, dqb, dkb, dvb, cos1, sin1, cos2, sin2, qn_g, kn_g)


NEG = -1e30
GROUP = 4
KV_HEADS = 2
GROUP_W = GROUP * HEAD_DIM
LSE_ROWS = 8


def _pos_mask_t(k_start, q_start, s_len, tk, tq):
    kpos = k_start + lax.broadcasted_iota(jnp.int32, (tk, tq), 0)
    qpos = q_start + lax.broadcasted_iota(jnp.int32, (tk, tq), 1)
    return (jnp.abs(kpos - qpos) <= BLOCK) & (kpos >= 0) & (kpos < s_len)


def flash_fwd_t(qkv_t, kv_tok, q_rb, k_i, v_rb, sink, window, name, comm=None):
    s_len = qkv_t.shape[1]
    if window:
        tq = _row_tile(s_len, 512)
        tk = 256
        per = tq // tk
        n_kv = per + 2
    else:
        tq = tk = _row_tile(s_len, 1024)
        n_kv = s_len // tk
    n_kb = s_len // tk
    n_i = s_len // tq
    c_ins, c_outs, c_remote = comm if comm else ([], [], [])
    n_main = 4 if window else 3

    def body(*refs):
        main, c_in_refs = refs[:n_main], refs[n_main:n_main + len(c_ins)]
        rest = refs[n_main + len(c_ins):]
        (o_ref, lse_ref), c_out_refs = rest[:2], rest[2:2 + len(c_outs)]
        m_sc, l_sc, acc_sc = rest[2 + len(c_outs):5 + len(c_outs)]
        c_sems = rest[5 + len(c_outs):]
        if window:
            sink_ref, q_ref, k_ref, v_ref = main
        else:
            q_ref, k_ref, v_ref = main
        h, i, t = pl.program_id(0), pl.program_id(1), pl.program_id(2)
        if comm:
            @pl.when((h == 0) & (i == 0) & (t == 0))
            def _():
                _exchange_start(c_remote, c_in_refs, c_out_refs, *c_sems)

        @pl.when(t == 0)
        def _():
            for g in range(GROUP):
                if window:
                    m_sc[g] = jnp.full((1, tq), sink_ref[h * GROUP + g] * LOG2E, F32)
                    l_sc[g] = jnp.ones((1, tq), F32)
                else:
                    m_sc[g] = jnp.full((1, tq), NEG, F32)
                    l_sc[g] = jnp.zeros((1, tq), F32)
                acc_sc[g] = jnp.zeros((HEAD_DIM, tq), F32)

        k = k_ref[...]
        v_t = v_ref[...]
        if window:
            mask = _pos_mask_t((i * per - 1 + t) * tk, i * tq, s_len, tk, tq)
        s_next = jnp.dot(k, q_ref[0:HEAD_DIM, :], preferred_element_type=F32)
        for g in range(GROUP):
            s_t = s_next
            if g + 1 < GROUP:
                s_next = jnp.dot(k, q_ref[(g + 1) * HEAD_DIM:(g + 2) * HEAD_DIM, :], preferred_element_type=F32)
            if window:
                s_t = jnp.where(mask, s_t, NEG)
            m_prev = m_sc[g]
            m_new = jnp.maximum(m_prev, jnp.max(s_t, axis=0, keepdims=True))
            alpha = jnp.exp2(m_prev - m_new)
            p_t = jnp.exp2(s_t - m_new)
            l_sc[g] = alpha * l_sc[g] + jnp.sum(p_t, axis=0, keepdims=True)
            acc_sc[g] = alpha * acc_sc[g] + jnp.dot(v_t, p_t.astype(BF16), preferred_element_type=F32)
            m_sc[g] = m_new

        @pl.when(t == n_kv - 1)
        def _():
            for g in range(GROUP):
                l = l_sc[g]
                o_ref[g * HEAD_DIM:(g + 1) * HEAD_DIM, :] = (acc_sc[g] / l).astype(BF16)
                lse_ref[g * LSE_ROWS:(g + 1) * LSE_ROWS, :] = jnp.broadcast_to(
                    m_sc[g] + jnp.log(l) * LOG2E, (LSE_ROWS, tq))

        if comm:
            @pl.when((h == KV_HEADS - 1) & (i == n_i - 1) & (t == n_kv - 1))
            def _():
                _exchange_finish(c_remote, c_in_refs, c_out_refs, *c_sems)

    if window:
        kv_blk = lambda i, t: jnp.clip(i * per - 1 + t, 0, n_kb - 1)
    else:
        kv_blk = lambda i, t: t
    hbm = pl.BlockSpec(memory_space=pl.ANY)
    in_specs = [pl.BlockSpec((GROUP_W, tq), lambda h, i, t: (q_rb + h, i)),
                pl.BlockSpec((None, tk, HEAD_DIM), lambda h, i, t: (k_i + h, kv_blk(i, t), 0)),
                pl.BlockSpec((HEAD_DIM, tk), lambda h, i, t: (v_rb + h, kv_blk(i, t)))]
    args = [qkv_t, kv_tok, qkv_t]
    if window:
        in_specs = [pl.BlockSpec(memory_space=pltpu.SMEM)] + in_specs
        args = [sink] + args
    return pl.pallas_call(
        body, name=name, grid=(KV_HEADS, n_i, n_kv),
        in_specs=in_specs + [hbm] * len(c_ins),
        out_specs=[pl.BlockSpec((GROUP_W, tq), lambda h, i, t: (h, i)),
                   pl.BlockSpec((GROUP * LSE_ROWS, tq), lambda h, i, t: (h, i))] + [hbm] * len(c_outs),
        out_shape=[_sds((KV_HEADS * GROUP_W, s_len), BF16),
                   _sds((KV_HEADS * GROUP * LSE_ROWS, s_len), F32)] + list(c_outs),
        scratch_shapes=[pltpu.VMEM((GROUP, 1, tq), F32), pltpu.VMEM((GROUP, 1, tq), F32),
                        pltpu.VMEM((GROUP, HEAD_DIM, tq), F32)] + _exchange_sems(c_remote),
        compiler_params=_params("arbitrary", "arbitrary", "arbitrary"),
    )(*args, *c_ins)


def flash_bwd_t(qkv_t, kv_tok, o_t, do_t, lse, q_rb, k_i, v_i, k_rb, do_rb, sink, window, name, comm=None):
    s_len = qkv_t.shape[1]
    if window:
        tq = _row_tile(s_len, 512)
        tk = 256
        n_q = 2
    else:
        tq = tk = _row_tile(s_len, 1024)
        n_q = s_len // tq
    n_qb = s_len // tq
    n_j = s_len // tk
    c_ins, c_outs, c_remote = comm if comm else ([], [], [])
    n_main = 8 if window else 7
    n_out = 4 if window else 3

    def body(*refs):
        main, c_in_refs = refs[:n_main], refs[n_main:n_main + len(c_ins)]
        rest = refs[n_main + len(c_ins):]
        outs, c_out_refs = rest[:n_out], rest[n_out:n_out + len(c_outs)]
        dk_sc, dv_sc = rest[n_out + len(c_outs):n_out + len(c_outs) + 2]
        c_sems = rest[n_out + len(c_outs) + 2:]
        if window:
            sink_ref, q_ref, k_ref, v_ref, kt_ref, o_ref, do_ref, lse_ref = main
            dq_ref, dk_ref, dv_ref, dsink_ref = outs
        else:
            q_ref, k_ref, v_ref, kt_ref, o_ref, do_ref, lse_ref = main
            dq_ref, dk_ref, dv_ref = outs
        h, j, t = pl.program_id(0), pl.program_id(1), pl.program_id(2)
        q_blk = (j + 1) // 2 - 1 + t if window else t
        if comm:
            @pl.when((h == 0) & (j == 0) & (t == 0))
            def _():
                _exchange_start(c_remote, c_in_refs, c_out_refs, *c_sems)

        @pl.when((j == 0) & (t == 0))
        def _():
            dq_ref[...] = jnp.zeros(dq_ref.shape, F32)
            if window:
                dsink_ref[...] = jnp.zeros((8, LANES), F32)

        @pl.when(t == 0)
        def _():
            dk_sc[...] = jnp.zeros((tk, HEAD_DIM), F32)
            dv_sc[...] = jnp.zeros((tk, HEAD_DIM), F32)

        def step():
            k, v, k_t = k_ref[...], v_ref[...], kt_ref[...]
            if window:
                mask = _pos_mask_t(j * tk, q_blk * tq, s_len, tk, tq)
                lane = lax.broadcasted_iota(jnp.int32, (8, LANES), 1)
                sink_tile = jnp.zeros((8, LANES), F32)
            dk_acc = dk_sc[...]
            dv_acc = dv_sc[...]
            for g in range(GROUP):
                rows = slice(g * HEAD_DIM, (g + 1) * HEAD_DIM)
                q_t, o_g, do_g = q_ref[rows, :], o_ref[rows, :], do_ref[rows, :]
                s_t = jnp.dot(k, q_t, preferred_element_type=F32)
                if window:
                    s_t = jnp.where(mask, s_t, NEG)
                lse_row = lse_ref[g * LSE_ROWS:g * LSE_ROWS + 1, :]
                p_t = jnp.exp2(s_t - lse_row)
                delta = jnp.sum(do_g.astype(F32) * o_g.astype(F32), axis=0, keepdims=True)
                dp_t = jnp.dot(v, do_g, preferred_element_type=F32)
                ds_t = (p_t * (dp_t - delta)).astype(BF16)
                dv_acc = dv_acc + lax.dot_general(p_t.astype(BF16), do_g, NT_DIMS, preferred_element_type=F32)
                dk_acc = dk_acc + lax.dot_general(ds_t, q_t, NT_DIMS, preferred_element_type=F32)
                dq_ref[q_blk, rows, :] += jnp.dot(k_t, ds_t, preferred_element_type=F32)
                if window:
                    p_sink = jnp.exp2(sink_ref[h * GROUP + g] * LOG2E - lse_row)
                    term = -jnp.sum(p_sink * delta, axis=1, keepdims=True)
                    sink_tile = jnp.where(lane == g, term, sink_tile)
            dk_sc[...] = dk_acc
            dv_sc[...] = dv_acc
            if window:
                @pl.when((j % 2 == 0) & (t == 1))
                def _():
                    dsink_ref[...] += sink_tile

        if window:
            pl.when((q_blk >= 0) & (q_blk < n_qb))(step)
        else:
            step()

        @pl.when(t == n_q - 1)
        def _():
            dk_ref[...] = dk_sc[...] * LN2
            dv_ref[...] = dv_sc[...]

        if comm:
            @pl.when((h == KV_HEADS - 1) & (j == n_j - 1) & (t == n_q - 1))
            def _():
                _exchange_finish(c_remote, c_in_refs, c_out_refs, *c_sems)

    if window:
        qb = lambda j, t: jnp.clip((j + 1) // 2 - 1 + t, 0, n_qb - 1)
    else:
        qb = lambda j, t: t
    hbm = pl.BlockSpec(memory_space=pl.ANY)
    in_specs = [pl.BlockSpec((GROUP_W, tq), lambda h, j, t: (q_rb + h, qb(j, t))),
                pl.BlockSpec((None, tk, HEAD_DIM), lambda h, j, t: (k_i + h, j, 0)),
                pl.BlockSpec((None, tk, HEAD_DIM), lambda h, j, t: (v_i + h, j, 0)),
                pl.BlockSpec((HEAD_DIM, tk), lambda h, j, t: (k_rb + h, j)),
                pl.BlockSpec((GROUP_W, tq), lambda h, j, t: (h, qb(j, t))),
                pl.BlockSpec((GROUP_W, tq), lambda h, j, t: (do_rb + h, qb(j, t))),
                pl.BlockSpec((GROUP * LSE_ROWS, tq), lambda h, j, t: (h, qb(j, t)))]
    args = [qkv_t, kv_tok, kv_tok, qkv_t, o_t, do_t, lse]
    kv_out = _sds((KV_HEADS, s_len, HEAD_DIM), F32)
    out_specs = [pl.BlockSpec((n_qb, GROUP_W, tq), lambda h, j, t: (0, h, 0)),
                 pl.BlockSpec((None, tk, HEAD_DIM), lambda h, j, t: (h, j, 0)),
                 pl.BlockSpec((None, tk, HEAD_DIM), lambda h, j, t: (h, j, 0))]
    out_shape = [_sds((n_qb, KV_HEADS * GROUP_W, tq), F32), kv_out, kv_out]
    if window:
        in_specs = [pl.BlockSpec(memory_space=pltpu.SMEM)] + in_specs
        args = [sink] + args
        out_specs.append(pl.BlockSpec((None, 8, LANES), lambda h, j, t: (h, 0, 0)))
        out_shape.append(_sds((KV_HEADS, 8, LANES), F32))
    return pl.pallas_call(
        body, name=name, grid=(KV_HEADS, n_j, n_q),
        in_specs=in_specs + [hbm] * len(c_ins), out_specs=out_specs + [hbm] * len(c_outs),
        out_shape=out_shape + list(c_outs),
        scratch_shapes=[pltpu.VMEM((tk, HEAD_DIM), F32), pltpu.VMEM((tk, HEAD_DIM), F32)]
        + _exchange_sems(c_remote),
        compiler_params=_params("arbitrary", "arbitrary", "arbitrary"),
    )(*args, *c_ins)


SGU_GROUPS = 8
SGU_CHUNK = 128
GELU_C = float(np.sqrt(2.0 / np.pi))
GELU_A = 0.044715


def _gelu(x):
    return x * (0.5 * (1.0 + jnp.tanh(GELU_C * (x + GELU_A * (x * x * x)))))


def _gelu_grad(x):
    t = jnp.tanh(GELU_C * (x + GELU_A * (x * x * x)))
    return 0.5 * (1.0 + t) + 0.5 * x * (1.0 - t * t) * (GELU_C * (1.0 + 3.0 * GELU_A * (x * x)))


def _layernorm_stats(v):
    mu = jnp.mean(v, axis=-1, keepdims=True)
    var = jnp.mean(jnp.square(v - mu), axis=-1, keepdims=True)
    rstd = lax.rsqrt(var + EPS)
    return (v - mu) * rstd, rstd


def sgu_mid_fwd(zpre, ln_g, ln_b, ws, bsb, name):
    s_len, width = zpre.shape
    d = width // 2
    ts = _row_tile(s_len, 256)

    def body(z_ref, g_ref, b_ref, ws_ref, bs_ref, y_ref):
        z = _gelu(z_ref[...])
        u, v = z[:, :d], z[:, d:]
        vhat, _ = _layernorm_stats(v)
        vn = (vhat * g_ref[...] + b_ref[...]).astype(BF16)
        for n in range(ts // SGU_CHUNK):
            rows = slice(n * SGU_CHUNK, (n + 1) * SGU_CHUNK)
            for g in range(SGU_GROUPS):
                cols = slice(g * LANES, (g + 1) * LANES)
                mixed = jnp.dot(ws_ref[g], vn[rows, cols], preferred_element_type=F32) + bs_ref[g]
                y_ref[rows, cols] = (u[rows, cols] * mixed).astype(BF16)

    vec = pl.BlockSpec((1, d), lambda i: (0, 0))
    cube = pl.BlockSpec((SGU_GROUPS, SGU_CHUNK, SGU_CHUNK), lambda i: (0, 0, 0))
    return pl.pallas_call(
        body, name=name, grid=(s_len // ts,),
        in_specs=[pl.BlockSpec((ts, width), lambda i: (i, 0)), vec, vec, cube, cube],
        out_specs=pl.BlockSpec((ts, d), lambda i: (i, 0)),
        out_shape=_sds((s_len, d), BF16),
        compiler_params=_params("arbitrary"),
    )(zpre, ln_g, ln_b, ws, bsb)


def sgu_mid_bwd(zpre, dy, ln_g, ln_b, ws, wst, bsb, name):
    s_len, width = zpre.shape
    d = width // 2
    ts = _row_tile(s_len, 256)
    n_steps = s_len // ts

    def body(z_ref, dy_ref, g_ref, b_ref, ws_ref, wst_ref, bs_ref,
             dz_ref, dws_ref, dbs_ref, dg_ref, db_ref, du_sc, dvn_sc):
        i = pl.program_id(0)

        @pl.when(i == 0)
        def _():
            dws_ref[...] = jnp.zeros(dws_ref.shape, F32)
            dbs_ref[...] = jnp.zeros(dbs_ref.shape, F32)
            dg_ref[...] = jnp.zeros(dg_ref.shape, F32)
            db_ref[...] = jnp.zeros(db_ref.shape, F32)

        zp = z_ref[...]
        z = _gelu(zp)
        u, v = z[:, :d], z[:, d:]
        vhat, rstd = _layernorm_stats(v)
        gain = g_ref[...]
        vn = (vhat * gain + b_ref[...]).astype(BF16)
        dyf = dy_ref[...].astype(F32)
        for n in range(ts // SGU_CHUNK):
            rows = slice(n * SGU_CHUNK, (n + 1) * SGU_CHUNK)
            for g in range(SGU_GROUPS):
                cols = slice(g * LANES, (g + 1) * LANES)
                vt = vn[rows, cols]
                mixed = jnp.dot(ws_ref[g], vt, preferred_element_type=F32) + bs_ref[g]
                dyt = dyf[rows, cols]
                du_sc[rows, cols] = dyt * mixed
                dmixed = dyt * u[rows, cols]
                dmb = dmixed.astype(BF16)
                dvn_sc[rows, cols] = jnp.dot(wst_ref[g], dmb, preferred_element_type=F32)
                dws_ref[g] += lax.dot_general(dmb, vt, NT_DIMS, preferred_element_type=F32)
                dbs_ref[g] += dmixed
        dvn = dvn_sc[...]
        dg_ref[...] += jnp.sum(dvn * vhat, axis=0, keepdims=True)
        db_ref[...] += jnp.sum(dvn, axis=0, keepdims=True)
        dvh = dvn * gain
        dv = rstd * (dvh - jnp.mean(dvh, axis=-1, keepdims=True)
                     - vhat * jnp.mean(dvh * vhat, axis=-1, keepdims=True))
        gp = _gelu_grad(zp)
        dz_ref[:, :d] = (du_sc[...] * gp[:, :d]).astype(BF16)
        dz_ref[:, d:] = (dv * gp[:, d:]).astype(BF16)

        @pl.when(i == n_steps - 1)
        def _():
            for g in range(SGU_GROUPS):
                tot = jnp.sum(dbs_ref[g], axis=1, keepdims=True)
                dbs_ref[g] = jnp.broadcast_to(tot, (SGU_CHUNK, LANES))

    vec = pl.BlockSpec((1, d), lambda i: (0, 0))
    cube = pl.BlockSpec((SGU_GROUPS, SGU_CHUNK, SGU_CHUNK), lambda i: (0, 0, 0))
    cube_shape = _sds((SGU_GROUPS, SGU_CHUNK, SGU_CHUNK), F32)
    return pl.pallas_call(
        body, name=name, grid=(n_steps,),
        in_specs=[pl.BlockSpec((ts, width), lambda i: (i, 0)), pl.BlockSpec((ts, d), lambda i: (i, 0)),
                  vec, vec, cube, cube, cube],
        out_specs=[pl.BlockSpec((ts, width), lambda i: (i, 0)), cube, cube, vec, vec],
        out_shape=[_sds((s_len, width), BF16), cube_shape, cube_shape, _sds((1, d), F32), _sds((1, d), F32)],
        scratch_shapes=[pltpu.VMEM((ts, d), F32), pltpu.VMEM((ts, d), F32)],
        compiler_params=_params("arbitrary"),
    )(zpre, dy, ln_g, ln_b, ws, wst, bsb)


def loss_head(x, g, target, name):
    s_len, d = x.shape
    tm = _row_tile(s_len, 512)

    def body(x_ref, g_ref, t_ref, dx_ref, dxb_ref, dg_ref, loss_ref):
        i = pl.program_id(0)
        xf = x_ref[...]
        gain = g_ref[...]
        r = lax.rsqrt(jnp.mean(xf * xf, axis=-1, keepdims=True) + EPS)
        xhat = xf * r
        err = xhat * gain - t_ref[...]
        row = jnp.mean(err * err, axis=-1, keepdims=True)
        part = 0.5 * jnp.sum(row, axis=0, keepdims=True)
        dy = err * (1.0 / d)
        dg_part = jnp.sum(dy * xhat, axis=0, keepdims=True)

        @pl.when(i == 0)
        def _():
            dg_ref[...] = dg_part
            loss_ref[...] = jnp.broadcast_to(part, (8, LANES))

        @pl.when(i > 0)
        def _():
            dg_ref[...] += dg_part
            loss_ref[...] += jnp.broadcast_to(part, (8, LANES))

        dxh = dy * gain
        dx = r * (dxh - xhat * jnp.mean(dxh * xhat, axis=-1, keepdims=True))
        dx_ref[...] = dx
        dxb_ref[...] = dx.astype(BF16)

    row_spec = pl.BlockSpec((tm, d), lambda i: (i, 0))
    vec = pl.BlockSpec((1, d), lambda i: (0, 0))
    return pl.pallas_call(
        body, name=name, grid=(s_len // tm,),
        in_specs=[row_spec, vec, row_spec],
        out_specs=[row_spec, row_spec, vec, pl.BlockSpec((8, LANES), lambda i: (0, 0))],
        out_shape=[_sds((s_len, d), F32), _sds((s_len, d), BF16), _sds((1, d), F32), _sds((8, LANES), F32)],
        compiler_params=_params("arbitrary"),
    )(x, g, target)


FLIP_BITS = {"c": (0, 0, 1), "x": (1, 0, 0), "y": (0, 1, 0), "xy": (1, 1, 0),
             "xc": (1, 0, 1), "yc": (0, 1, 1), "xyc": (1, 1, 1)}
CHIP_FLIPS = ("x", "y", "xy")


def _flip(pos, name):
    return tuple(1 - p if bit else p for p, bit in zip(pos, FLIP_BITS[name]))


def _chip(pos):
    return 2 * pos[0] + pos[1]


def _me():
    return (lax.axis_index("x"), lax.axis_index("y"), lax.axis_index("c"))


def _exchange_copy(remote, k, in_refs, out_refs, send_sems, recv_sems, sender, receiver):
    ii, src_fn, oi, dst_fn, _ = remote[k]
    return pltpu.make_async_remote_copy(
        src_ref=src_fn(in_refs[ii], sender, receiver), dst_ref=dst_fn(out_refs[oi], sender),
        send_sem=send_sems.at[k], recv_sem=recv_sems.at[k], device_id=receiver, device_id_type=MESH)


def _exchange_start(remote, in_refs, out_refs, send_sems, recv_sems):
    me = _me()
    for k in range(len(remote)):
        _exchange_copy(remote, k, in_refs, out_refs, send_sems, recv_sems, me, _flip(me, remote[k][4])).start()


def _exchange_finish(remote, in_refs, out_refs, send_sems, recv_sems):
    me = _me()
    for k in range(len(remote)):
        _exchange_copy(remote, k, in_refs, out_refs, send_sems, recv_sems, _flip(me, remote[k][4]), me).wait_recv()
    for k in range(len(remote)):
        _exchange_copy(remote, k, in_refs, out_refs, send_sems, recv_sems, me, _flip(me, remote[k][4])).wait_send()


def _exchange_sems(remote):
    n = len(remote)
    return [pltpu.SemaphoreType.DMA((n,)), pltpu.SemaphoreType.DMA((n,))] if n else []


def exchange(name, ins, out_shapes, remote, local):
    n_in, n_out = len(ins), len(out_shapes)

    def body(*refs):
        in_refs, out_refs = refs[:n_in], refs[n_in:n_in + n_out]
        send_sems, recv_sems, local_sems = refs[n_in + n_out:]
        me = _me()
        stays = []
        for k, (ii, src_fn, oi, dst_fn) in enumerate(local):
            cp = pltpu.make_async_copy(src_fn(in_refs[ii], me), dst_fn(out_refs[oi], me), local_sems.at[k])
            cp.start()
            stays.append(cp)
        _exchange_start(remote, in_refs, out_refs, send_sems, recv_sems)
        _exchange_finish(remote, in_refs, out_refs, send_sems, recv_sems)
        for cp in stays:
            cp.wait()

    hbm = pl.BlockSpec(memory_space=pl.ANY)
    return pl.pallas_call(
        body, name=name,
        in_specs=[hbm] * n_in, out_specs=[hbm] * n_out, out_shape=list(out_shapes),
        scratch_shapes=[pltpu.SemaphoreType.DMA((max(len(remote), 1),)),
                        pltpu.SemaphoreType.DMA((max(len(remote), 1),)),
                        pltpu.SemaphoreType.DMA((max(len(local), 1),))],
        compiler_params=pltpu.CompilerParams(has_side_effects=True),
    )(*ins)


def staged_push(name, ins, out_shapes, jobs, n_alias=0):
    n_in, n_out = len(ins), len(out_shapes)
    n_copies = sum(len(dsts) for _, _, dsts in jobs)
    n_remote = sum(1 for _, _, dsts in jobs for d in dsts if d[2] is not None)

    def chunk_of(ii, src_fn):
        probe = _ShapeRef(ins[ii].shape, ins[ii].dtype)
        got = src_fn(probe, (0, 0, 0))
        return tuple(got.shape), got.dtype

    classes = []
    for ii, src_fn, _ in jobs:
        c = chunk_of(ii, src_fn)
        if c not in classes:
            classes.append(c)

    def body(*refs):
        in_refs, out_refs = refs[:n_in], refs[n_in:n_in + n_out]
        bufs = refs[n_in + n_out:n_in + n_out + len(classes)]
        load_sem, out_sems, recv_sems = refs[n_in + n_out + len(classes):]
        me = (lax.axis_index("x"), lax.axis_index("y"), lax.axis_index("c"))
        pending = [[[], []] for _ in classes]
        used = [0] * len(classes)
        arrivals = []
        k = r = 0
        for ii, src_fn, dsts in jobs:
            cls = classes.index(chunk_of(ii, src_fn))
            slot = used[cls] % 2
            used[cls] += 1
            for kind, cp in pending[cls][slot]:
                cp.wait_send() if kind == "remote" else cp.wait()
            buf = bufs[cls].at[slot]
            load = pltpu.make_async_copy(src_fn(in_refs[ii], me), buf, load_sem.at[0])
            load.start()
            load.wait()
            sent = []
            for oi, dst_fn, flip in dsts:
                if flip is None:
                    cp = pltpu.make_async_copy(buf, dst_fn(out_refs[oi], me), out_sems.at[k])
                    cp.start()
                    sent.append(("local", cp))
                else:
                    peer = _flip(me, flip)
                    cp = pltpu.make_async_remote_copy(
                        src_ref=buf, dst_ref=dst_fn(out_refs[oi], me), send_sem=out_sems.at[k],
                        recv_sem=recv_sems.at[r], device_id=peer, device_id_type=MESH)
                    cp.start()
                    sent.append(("remote", cp))
                    arrivals.append((r, cls, oi, dst_fn, peer))
                    r += 1
                k += 1
            pending[cls][slot] = sent
        for per_class in pending:
            for slot_list in per_class:
                for kind, cp in slot_list:
                    cp.wait_send() if kind == "remote" else cp.wait()
        for r, cls, oi, dst_fn, peer in arrivals:
            pltpu.make_async_remote_copy(
                src_ref=bufs[cls].at[0], dst_ref=dst_fn(out_refs[oi], peer), send_sem=out_sems.at[0],
                recv_sem=recv_sems.at[r], device_id=peer, device_id_type=MESH).wait_recv()

    hbm = pl.BlockSpec(memory_space=pl.ANY)
    return pl.pallas_call(
        body, name=name,
        in_specs=[hbm] * n_in, out_specs=[hbm] * n_out, out_shape=list(out_shapes),
        scratch_shapes=[pltpu.VMEM((2,) + shape, dtype) for shape, dtype in classes]
        + [pltpu.SemaphoreType.DMA((1,)), pltpu.SemaphoreType.DMA((max(n_copies, 1),)),
           pltpu.SemaphoreType.DMA((max(n_remote, 1),))],
        input_output_aliases={i: i for i in range(n_alias)},
        compiler_params=pltpu.CompilerParams(has_side_effects=True, vmem_limit_bytes=VMEM_LIMIT),
    )(*ins)


class _ShapeRef:
    def __init__(self, shape, dtype):
        self.shape, self.dtype = tuple(shape), dtype

    @property
    def at(self):
        return self

    def __getitem__(self, idx):
        idx = idx if isinstance(idx, tuple) else (idx,)
        shape = []
        for dim, i in zip(self.shape, idx):
            if isinstance(i, slice):
                shape.append(len(range(*i.indices(dim))))
            elif hasattr(i, "size") and hasattr(i, "start"):
                shape.append(i.size)
        shape += self.shape[len(idx):]
        return _ShapeRef(shape, self.dtype)


def gather_whole(shards, name):
    whole = lambda ref, sender, receiver=None: ref
    slot = lambda ref, sender: ref.at[_chip(sender)]
    remote = [(t, whole, t, slot, flip) for t in range(len(shards)) for flip in CHIP_FLIPS]
    local = [(t, whole, t, slot) for t in range(len(shards))]
    outs = [_sds((N_CHIPS,) + a.shape, a.dtype) for a in shards]
    return exchange(name, list(shards), outs, remote, local)


def _half_axis(shape):
    return 0 if shape[0] >= 2 else 1


def gather_halves_plan(shards):
    remote = []
    for t, a in enumerate(shards):
        ax = _half_axis(a.shape)
        half = lambda ref, sender, receiver=None, ax=ax: _half(ref, sender[2], ax)
        slot = lambda ref, sender, ax=ax: _half(ref.at[_chip(sender)], sender[2], ax)
        remote += [(t, half, t, slot, flip) for flip in CHIP_FLIPS]
    outs = [_sds((N_CHIPS,) + a.shape, a.dtype) for a in shards]
    return list(shards), outs, remote


def gather_halves_fill(got, shards, name):
    n_t = len(shards)
    jobs = []
    for t, a in enumerate(shards):
        layers = a.shape[0]
        for l in range(layers):
            jobs.append((n_t + t, lambda ref, me, l=l: ref.at[l],
                         [(t, lambda ref, sender, l=l: ref.at[_chip(sender), l], None)]))
        for flip in CHIP_FLIPS:
            if _half_axis(a.shape) == 0:
                n = layers // 2
                for j in range(n):
                    at = lambda ref, pos, flip=flip, j=j, n=n: ref.at[_chip(_flip(pos, flip)), pos[2] * n + j]
                    jobs.append((t, at, [(t, at, "c")]))
            else:
                rows = a.shape[1] // 2
                at = lambda ref, pos, flip=flip, rows=rows: ref.at[
                    _chip(_flip(pos, flip)), 0, pl.ds(pos[2] * rows, rows)]
                jobs.append((t, at, [(t, at, "c")]))
    outs = [_sds(g.shape, g.dtype) for g in got]
    return staged_push(name, list(got) + list(shards), outs, jobs, n_alias=n_t)


def gather_all(buf, name):
    whole = lambda ref, sender, receiver=None: ref
    slot = lambda ref, sender: ref.at[4 * sender[0] + 2 * sender[1] + sender[2]]
    remote = [(0, whole, 0, slot, flip) for flip in FLIP_BITS]
    local = [(0, whole, 0, slot)]
    return exchange(name, [buf], [_sds((8,) + buf.shape, buf.dtype)], remote, local)[0]


def _half(ref, core, axis):
    rows = ref.shape[axis] // 2
    idx = (slice(None),) * axis + (pl.ds(core * rows, rows),)
    return ref.at[idx]


def reduce_begin(grads, name):
    core = lax.axis_index("c").astype(jnp.int32).reshape(1)
    jobs, outs = [], []
    for t, g in enumerate(grads):
        outs.append(_sds((N_CHIPS, g.shape[1] // 2, g.shape[2]), BF16))
        for s in range(N_CHIPS):
            jobs.append((t, lambda ref, me, s=s: _half(ref.at[s], 1 - me[2], 0),
                         [(t, lambda ref, sender, s=s: ref.at[s], "c")]))
    theirs = staged_push(name + "_swap", grads, outs, jobs)
    chip_sums = [add_half(g, r, core, f"{name}_add{t}") for t, (g, r) in enumerate(zip(grads, theirs))]
    remote, outs = [], []
    for t, p in enumerate(chip_sums):
        outs.append(_sds((len(CHIP_FLIPS),) + p.shape[1:], BF16))
        for f, flip in enumerate(CHIP_FLIPS):
            remote.append((t, lambda ref, sender, receiver: ref.at[_chip(receiver)],
                           t, lambda ref, sender, f=f: ref.at[f], flip))
    return chip_sums, (chip_sums, outs, remote)


def reduce_finish(chip_sums, got, stacks, full_shapes, into, name):
    chip = (2 * lax.axis_index("x") + lax.axis_index("y")).astype(jnp.int32).reshape(1)
    totals = [sum_chips(p, r, chip, f"{name}_sum{t}") for t, (p, r) in enumerate(zip(chip_sums, got))]
    names = []
    for out_name, _ in stacks:
        if out_name not in names:
            names.append(out_name)
    names = [n for n in names if n in into] + [n for n in names if n not in into]
    kept = [into[n] for n in names if n in into]
    outs = [_sds(full_shapes[n], F32) for n in names]
    jobs = []
    for t, (out_name, layer) in enumerate(stacks):
        oi = names.index(out_name)
        rows, cols = totals[t].shape
        pieces = max(1, rows * cols * 4 // STAGE_BYTES)
        step = rows // pieces
        for q in range(pieces):
            src = lambda ref, me, q=q, step=step: ref.at[pl.ds(q * step, step)]
            place = lambda ref, sender, layer=layer, q=q, step=step, rows=rows: ref.at[
                layer, pl.ds(sender[2] * rows + q * step, step)]
            jobs.append((len(kept) + t, src, [(oi, place, None), (oi, place, "c")]))
    full = staged_push(name + "_share", kept + totals, outs, jobs, n_alias=len(kept))
    return {**into, **dict(zip(names, full))}


STAGE_BYTES = 1024 * 1024


def add_half(g, theirs, core, name):
    n_s, rows, cols = g.shape
    half = rows // 2
    tr = _row_tile(half, 256)
    nb = half // tr

    def body(core_ref, g_ref, t_ref, o_ref):
        o_ref[...] = (g_ref[...].astype(F32) + t_ref[...].astype(F32)).astype(BF16)

    return pl.pallas_call(
        body, name=name,
        grid_spec=pltpu.PrefetchScalarGridSpec(
            num_scalar_prefetch=1, grid=(n_s, nb),
            in_specs=[pl.BlockSpec((None, tr, cols), lambda s, i, c: (s, c[0] * nb + i, 0)),
                      pl.BlockSpec((None, tr, cols), lambda s, i, c: (s, i, 0))],
            out_specs=pl.BlockSpec((None, tr, cols), lambda s, i, c: (s, i, 0))),
        out_shape=_sds((n_s, half, cols), BF16),
        compiler_params=_params("arbitrary", "arbitrary"),
    )(core, g, theirs)


def sum_chips(mine, theirs, chip, name):
    _, half, cols = mine.shape
    tr = _row_tile(half, 256)

    def body(chip_ref, m_ref, a_ref, b_ref, c_ref, o_ref):
        o_ref[...] = ((m_ref[...].astype(F32) + a_ref[...].astype(F32))
                      + b_ref[...].astype(F32)) + c_ref[...].astype(F32)

    got = lambda f: pl.BlockSpec((None, tr, cols), lambda i, ch: (f, i, 0))
    return pl.pallas_call(
        body, name=name,
        grid_spec=pltpu.PrefetchScalarGridSpec(
            num_scalar_prefetch=1, grid=(half // tr,),
            in_specs=[pl.BlockSpec((None, tr, cols), lambda i, ch: (ch[0], i, 0)), got(0), got(1), got(2)],
            out_specs=pl.BlockSpec((tr, cols), lambda i, ch: (i, 0))),
        out_shape=_sds((half, cols), F32),
        compiler_params=_params("arbitrary"),
    )(chip, mine, theirs, theirs, theirs)


def _tok(t):
    return t.transpose(1, 0, 2).reshape(t.shape[1], t.shape[0] * t.shape[2])


def _heads(t):
    return t.reshape(t.shape[0], t.shape[1] // HEAD_DIM, HEAD_DIM).transpose(1, 0, 2)


def _tile2(vec):
    return jnp.tile(vec.reshape(1, HEAD_DIM), (1, 2))


REST = ("att_w_in", "att_w_out", "sgu_w_in", "sgu_w_out", "mlp_w1", "mlp_w2")
LAST_GROUP = (("att_w_in", 0),)


def local_step(x, target, first, rest_shards, rep, full_shapes):
    s_len, d = x.shape
    tabs = rope_tables(s_len)
    depth = rep["mlp_norm"].shape[0]
    row = lambda a: a.reshape(1, -1)
    saved = []
    h = x
    gw = {"att_w_in": [first[0]], "att_w_out": [first[1]]}

    def wl(name, idx):
        return (gw[name][idx], 0) if name.startswith("att") else (gw[name], idx)

    for layer in range(depth):
        i = layer // 2
        tag = f"l{layer}"
        if layer % 2 == 0:
            hn, proj = norm_mm(h, row(rep["att_norm"][i]), *wl("att_w_in", i), F32, tag + "_att_proj")
            qkv_t, kv = prep_fwd(proj, tabs, _tile2(rep["att_qnorm"][i]), _tile2(rep["att_knorm"][i]),
                                 tag + "_att_prep")
            kv_tok = _heads(kv)
            oa, lse_a = flash_fwd_t(qkv_t, kv_tok, QA_COL // GROUP_W, 0, (KA_COL + LANES) // HEAD_DIM,
                                    rep["att_sink"][i], True, tag + "_win_fwd")
            plan = gather_halves_plan(rest_shards) if layer == 0 else None
            ob, lse_b, *got = flash_fwd_t(qkv_t, kv_tok, QB_COL // GROUP_W, 4, (KB_COL + LANES) // HEAD_DIM,
                                          None, False, tag + "_grid_fwd", comm=plan)
            if layer == 0:
                rest = dict(zip(REST, gather_halves_fill(got, rest_shards, "gather_rest_fill")))
                gw["att_w_in"].append(rest.pop("att_w_in"))
                gw["att_w_out"].append(rest.pop("att_w_out"))
                gw.update(rest)
            out = mm_res_t([oa, ob], *wl("att_w_out", i), h, tag + "_att_out")
            mix_saved = (h, hn, proj, qkv_t, kv_tok, oa, ob, lse_a, lse_b)
        else:
            hn, zpre = norm_mm(h, row(rep["sgu_norm"][i]), *wl("sgu_w_in", i), F32, tag + "_sgu_in")
            ws = rep["sgu_w_s"][i].astype(BF16)
            bsb = jnp.broadcast_to(rep["sgu_b_s"][i][:, :, None], (SGU_GROUPS, SGU_CHUNK, LANES))
            y = sgu_mid_fwd(zpre, row(rep["sgu_ln_g"][i]), row(rep["sgu_ln_b"][i]), ws, bsb, tag + "_sgu_mid")
            out = mm_res(y, *wl("sgu_w_out", i), h, tag + "_sgu_out")
            mix_saved = (h, hn, zpre, y, ws, bsb)
        hm, a = norm_mm(out, row(rep["mlp_norm"][layer]), *wl("mlp_w1", layer), BF16, tag + "_mlp_up")
        nxt = mm_res(a, *wl("mlp_w2", layer), out, tag + "_mlp_down", relu2=True)
        saved.append((mix_saved, (out, hm, a)))
        h = nxt
    dh, dhb, d_final, loss_tile = loss_head(h, row(rep["final_norm"]), target, "loss_head")
    big, tags = [], []
    small = {k: [None] * v.shape[0] for k, v in rep.items() if k != "final_norm"}
    small["final_norm"] = d_final.reshape(-1)
    for layer in reversed(range(depth)):
        i = layer // 2
        tag = f"l{layer}"
        mix_saved, (xin, hm, a) = saved[layer]
        da = mm_nt_relu2_bwd(dhb, *wl("mlp_w2", layer), a, tag + "_mlp_down_bwd")
        big.append(dw_mm(a, dhb, tag + "_mlp_dw2", col_sharded=False, relu2=True))
        tags.append(("mlp_w2", layer))
        big.append(dw_mm(hm, da, tag + "_mlp_dw1", col_sharded=True))
        tags.append(("mlp_w1", layer))
        dh, dhb, dg = dx_norm(da, *wl("mlp_w1", layer), xin, row(rep["mlp_norm"][layer]), dh, tag + "_mlp_up_bwd")
        small["mlp_norm"][layer] = dg.reshape(-1)
        if layer % 2 == 0:
            xin, hn, proj, qkv_t, kv_tok, oa, ob, lse_a, lse_b = mix_saved
            do_t = mm_nt(dhb, *wl("att_w_out", i), tag + "_att_out_bwd", transposed=True)
            big.append(dw_nn([oa, ob], dhb, tag + "_att_dwout"))
            tags.append(("att_w_out", i))
            dqa, dka, dva, dsink = flash_bwd_t(qkv_t, kv_tok, oa, do_t, lse_a, QA_COL // GROUP_W, 0, 2,
                                               KA_COL // HEAD_DIM, 0, rep["att_sink"][i], True, tag + "_win_bwd")
            plan = None
            if layer == 0:
                early = [k for k, t in enumerate(tags) if t not in LAST_GROUP]
                chip_sums, plan = reduce_begin([big[k] for k in early], "grads1")
            dqb, dkb, dvb, *got = flash_bwd_t(qkv_t, kv_tok, ob, do_t, lse_b, QB_COL // GROUP_W, 4, 6,
                                              KB_COL // HEAD_DIM, 2, None, False, tag + "_grid_bwd", comm=plan)
            if layer == 0:
                grads = reduce_finish(chip_sums, got, [tags[k] for k in early], full_shapes, {}, "grads1")
            qg, kg = _tile2(rep["att_qnorm"][i]), _tile2(rep["att_knorm"][i])
            dproj, dqg, dkg = prep_bwd(proj, dqa, _tok(dka), _tok(dva), dqb, _tok(dkb), _tok(dvb),
                                       tabs, qg, kg, tag + "_att_prep_bwd")
            big.append(dw_mm(hn, dproj, tag + "_att_dwin", col_sharded=True))
            tags.append(("att_w_in", i))
            dh, dhb, dg = dx_norm(dproj, *wl("att_w_in", i), xin, row(rep["att_norm"][i]), dh, tag + "_att_proj_bwd")
            small["att_norm"][i] = dg.reshape(-1)
            small["att_sink"][i] = dsink[:, 0, :GROUP].reshape(-1)
            small["att_qnorm"][i] = dqg[0, :HEAD_DIM] + dqg[0, HEAD_DIM:]
            small["att_knorm"][i] = dkg[0, :HEAD_DIM] + dkg[0, HEAD_DIM:]
        else:
            xin, hn, zpre, y, ws, bsb = mix_saved
            dy = mm_nt(dhb, *wl("sgu_w_out", i), tag + "_sgu_out_bwd")
            big.append(dw_mm(y, dhb, tag + "_sgu_dwout", col_sharded=False))
            tags.append(("sgu_w_out", i))
            wst = ws.transpose(0, 2, 1)
            dz, dws, dbs, dlg, dlb = sgu_mid_bwd(zpre, dy, row(rep["sgu_ln_g"][i]), row(rep["sgu_ln_b"][i]),
                                                 ws, wst, bsb, tag + "_sgu_mid_bwd")
            big.append(dw_mm(hn, dz, tag + "_sgu_dwin", col_sharded=True))
            tags.append(("sgu_w_in", i))
            dh, dhb, dg = dx_norm(dz, *wl("sgu_w_in", i), xin, row(rep["sgu_norm"][i]), dh, tag + "_sgu_in_bwd")
            small["sgu_norm"][i] = dg.reshape(-1)
            small["sgu_ln_g"][i] = dlg.reshape(-1)
            small["sgu_ln_b"][i] = dlb.reshape(-1)
            small["sgu_w_s"][i] = dws
            small["sgu_b_s"][i] = dbs[:, :, 0]
    small = {k: (v if k == "final_norm" else jnp.stack(v)) for k, v in small.items()}
    late = [k for k, t in enumerate(tags) if t in LAST_GROUP]
    chip_sums, plan = reduce_begin([big[k] for k in late], "grads2")
    got = exchange("grads2_scatter", *plan, [])
    grads = reduce_finish(chip_sums, got, [tags[k] for k in late], full_shapes, grads, "grads2")
    return loss_tile, dh, grads, small


BIG = ("att_w_in", "att_w_out", "sgu_w_in", "sgu_w_out", "mlp_w1", "mlp_w2")
SHARDED_VEC = ("sgu_norm", "sgu_ln_g", "sgu_ln_b")
REPLICATED = ("att_norm", "att_sink", "att_qnorm", "att_knorm", "sgu_w_s", "sgu_b_s", "mlp_norm", "final_norm")
WEIGHTS = ("att_norm", "att_w_in", "att_sink", "att_qnorm", "att_knorm", "att_w_out", "sgu_norm", "sgu_w_in",
           "sgu_ln_g", "sgu_ln_b", "sgu_w_s", "sgu_b_s", "sgu_w_out", "mlp_norm", "mlp_w1", "mlp_w2", "final_norm")
SMALL = tuple(n for n in WEIGHTS if n not in BIG)
PACK_ALIGN = 8 * LANES


def _pack_small(small, loss_tile):
    parts = [small[n].reshape(-1) for n in SMALL] + [loss_tile[0, :1]]
    flat = jnp.concatenate(parts)
    pad = -flat.shape[0] % PACK_ALIGN
    return jnp.pad(flat, (0, pad)).reshape(-1, LANES)


def _unpack_small(flat2d, shapes):
    flat = flat2d.reshape(-1)
    out, off = {}, 0
    for n in SMALL:
        size = int(np.prod(shapes[n]))
        out[n] = flat[off:off + size].reshape(shapes[n])
        off += size
    return out, flat[off]


def kernel(x, att_norm, att_w_in, att_sink, att_qnorm, att_knorm, att_w_out, sgu_norm, sgu_w_in, sgu_ln_g, sgu_ln_b, sgu_w_s, sgu_b_s, sgu_w_out, mlp_norm, mlp_w1, mlp_w2, final_norm, loss_target, m_att_norm, m_att_w_in, m_att_sink, m_att_qnorm, m_att_knorm, m_att_w_out, m_sgu_norm, m_sgu_w_in, m_sgu_ln_g, m_sgu_ln_b, m_sgu_w_s, m_sgu_b_s, m_sgu_w_out, m_mlp_norm, m_mlp_w1, m_mlp_w2, m_final_norm, v_att_norm, v_att_w_in, v_att_sink, v_att_qnorm, v_att_knorm, v_att_w_out, v_sgu_norm, v_sgu_w_in, v_sgu_ln_g, v_sgu_ln_b, v_sgu_w_s, v_sgu_b_s, v_sgu_w_out, v_mlp_norm, v_mlp_w1, v_mlp_w2, v_final_norm):
    w = dict(att_norm=att_norm, att_w_in=att_w_in, att_sink=att_sink, att_qnorm=att_qnorm, att_knorm=att_knorm,
             att_w_out=att_w_out, sgu_norm=sgu_norm, sgu_w_in=sgu_w_in, sgu_ln_g=sgu_ln_g, sgu_ln_b=sgu_ln_b,
             sgu_w_s=sgu_w_s, sgu_b_s=sgu_b_s, sgu_w_out=sgu_w_out, mlp_norm=mlp_norm, mlp_w1=mlp_w1,
             mlp_w2=mlp_w2, final_norm=final_norm)
    m = dict(att_norm=m_att_norm, att_w_in=m_att_w_in, att_sink=m_att_sink, att_qnorm=m_att_qnorm,
             att_knorm=m_att_knorm, att_w_out=m_att_w_out, sgu_norm=m_sgu_norm, sgu_w_in=m_sgu_w_in,
             sgu_ln_g=m_sgu_ln_g, sgu_ln_b=m_sgu_ln_b, sgu_w_s=m_sgu_w_s, sgu_b_s=m_sgu_b_s,
             sgu_w_out=m_sgu_w_out, mlp_norm=m_mlp_norm, mlp_w1=m_mlp_w1, mlp_w2=m_mlp_w2,
             final_norm=m_final_norm)
    v = dict(att_norm=v_att_norm, att_w_in=v_att_w_in, att_sink=v_att_sink, att_qnorm=v_att_qnorm,
             att_knorm=v_att_knorm, att_w_out=v_att_w_out, sgu_norm=v_sgu_norm, sgu_w_in=v_sgu_w_in,
             sgu_ln_g=v_sgu_ln_g, sgu_ln_b=v_sgu_ln_b, sgu_w_s=v_sgu_w_s, sgu_b_s=v_sgu_b_s,
             sgu_w_out=v_sgu_w_out, mlp_norm=v_mlp_norm, mlp_w1=v_mlp_w1, mlp_w2=v_mlp_w2,
             final_norm=v_final_norm)
    chip = 2 * lax.axis_index("x") + lax.axis_index("y")

    vecs = jnp.stack([w[n] for n in SHARDED_VEC])
    wb = {n: w[n].astype(BF16) for n in BIG}
    *first, vec_all = gather_whole([wb["att_w_in"][0:1], wb["att_w_out"][0:1], vecs], "gather_first")
    rest_shards = [wb[n][1:2] if n.startswith("att") else wb[n] for n in REST]
    vec_full = vec_all.transpose(1, 2, 0, 3).reshape(vecs.shape[0], vecs.shape[1], -1)
    rep = {n: w[n] for n in REPLICATED}
    rep.update({n: vec_full[k] for k, n in enumerate(SHARDED_VEC)})

    loss_tile, grad_x, grads, small = local_step(x[0], loss_target[0], first, rest_shards, rep,
                                                 {n: w[n].shape for n in BIG})
    packed = _pack_small(small, loss_tile)
    everyone = gather_all(packed, "gather_small")
    add8 = lambda *a: (((a[0] + a[1]) + (a[2] + a[3])) + ((a[4] + a[5]) + (a[6] + a[7])),)
    total = ew(add8, [everyone[k] for k in range(8)], [F32], "sum_small")[0]
    small_g, loss = _unpack_small(total, {n: small[n].shape for n in SMALL})
    width = w["sgu_norm"].shape[1]
    for n in SHARDED_VEC:
        small_g[n] = lax.dynamic_slice_in_dim(small_g[n], chip * width, width, axis=1)
    grads.update(small_g)
    for n in BIG:
        grads[n] = grads[n].reshape(w[n].shape)

    delta, new_m, new_v = {}, {}, {}
    for n in WEIGHTS:
        shape = w[n].shape
        two_d = (lambda a: a.reshape(1, -1)) if len(shape) == 1 else (lambda a: a)
        dn, mn, vn = adamw(two_d(w[n]), two_d(grads[n]), two_d(m[n]), two_d(v[n]), "adamw_" + n)
        delta[n], new_m[n], new_v[n] = dn.reshape(shape), mn.reshape(shape), vn.reshape(shape)
    return (loss, grad_x[None], *[grads[n] for n in WEIGHTS], *[delta[n] for n in WEIGHTS],
            *[new_m[n] for n in WEIGHTS], *[new_v[n] for n in WEIGHTS])
```

```python
import numpy as np
import jax
import jax.numpy as jnp
from jax import lax
from jax.experimental import pallas as pl
from jax.experimental.pallas import tpu as pltpu

F32 = jnp.float32
BF16 = jnp.bfloat16
MESH = pl.DeviceIdType.MESH

EPS = 1e-6
HEAD_DIM = 64
BLOCK = 128
GRID_W = 64
ROPE_THETA = 10000.0
N_CHIPS = 4
LANES = 128
V7X_VMEM_BYTES = 64 * 1024 * 1024
VMEM_LIMIT = V7X_VMEM_BYTES - 8 * 1024 * 1024

ADAM_LR = 0.001
ADAM_B1 = 0.9
ADAM_B2 = 0.999
ADAM_EPS = 1e-08
ADAM_WD = 0.01
ADAM_STEP = 10

NT_DIMS = (((1,), (1,)), ((), ()))
TN_DIMS = (((0,), (0,)), ((), ()))


def _params(*sem):
    return pltpu.CompilerParams(dimension_semantics=sem, vmem_limit_bytes=VMEM_LIMIT)


def _sds(shape, dtype):
    return jax.ShapeDtypeStruct(tuple(shape), dtype)


def _row_tile(rows, want):
    t = min(rows, want)
    assert rows % t == 0, (rows, want)
    return t


def norm_mm(x, g, w4, layer, out_dtype, name):
    s_len, d = x.shape
    ns = w4.shape[-1]
    tm = _row_tile(s_len, 512)

    def body(x_ref, g_ref, w_ref, h_ref, y_ref):
        xf = x_ref[...]
        r = lax.rsqrt(jnp.mean(xf * xf, axis=-1, keepdims=True) + EPS)
        h = ((xf * r) * g_ref[...]).astype(BF16)
        h_ref[...] = h
        for s in range(N_CHIPS):
            y_ref[:, s * ns:(s + 1) * ns] = jnp.dot(h, w_ref[s], preferred_element_type=F32).astype(y_ref.dtype)

    return pl.pallas_call(
        body, name=name, grid=(s_len // tm,),
        in_specs=[pl.BlockSpec((tm, d), lambda i: (i, 0)),
                  pl.BlockSpec((1, d), lambda i: (0, 0)),
                  pl.BlockSpec((N_CHIPS, None, d, ns), lambda i: (0, layer, 0, 0))],
        out_specs=[pl.BlockSpec((tm, d), lambda i: (i, 0)),
                   pl.BlockSpec((tm, N_CHIPS * ns), lambda i: (i, 0))],
        out_shape=[_sds((s_len, d), BF16), _sds((s_len, N_CHIPS * ns), out_dtype)],
        compiler_params=_params("arbitrary"),
    )(x, g, w4)


def mm_res(a, w4, layer, res, name, relu2=False):
    s_len, k = a.shape
    kq, n = w4.shape[-2:]
    assert kq * N_CHIPS == k
    tm = _row_tile(s_len, 256 if k > 1024 else 512)

    def body(a_ref, w0, w1, w2, w3, r_ref, o_ref):
        acc = r_ref[...]
        for s, w_ref in enumerate((w0, w1, w2, w3)):
            av = a_ref[:, s * kq:(s + 1) * kq]
            if relu2:
                t = jnp.maximum(av.astype(F32), 0.0)
                av = (t * t).astype(BF16)
            acc = acc + jnp.dot(av, w_ref[...], preferred_element_type=F32)
        o_ref[...] = acc

    def wspec(s):
        return pl.BlockSpec((None, None, kq, n), lambda i: (s, layer, 0, 0))

    return pl.pallas_call(
        body, name=name, grid=(s_len // tm,),
        in_specs=[pl.BlockSpec((tm, k), lambda i: (i, 0)), wspec(0), wspec(1), wspec(2), wspec(3),
                  pl.BlockSpec((tm, n), lambda i: (i, 0))],
        out_specs=pl.BlockSpec((tm, n), lambda i: (i, 0)),
        out_shape=_sds((s_len, n), F32),
        compiler_params=_params("arbitrary"),
    )(a, w4, w4, w4, w4, res)


def mm_res_t(pieces, w4, layer, res, name):
    s_len = res.shape[0]
    kq, n = w4.shape[-2:]
    rows = pieces[0].shape[0]
    assert rows % kq == 0 and rows * len(pieces) == kq * N_CHIPS
    tm = _row_tile(s_len, 512)
    n_p = len(pieces)

    def body(*refs):
        p_refs, w_refs, (r_ref, o_ref) = refs[:n_p], refs[n_p:n_p + N_CHIPS], refs[n_p + N_CHIPS:]
        acc = r_ref[...]
        for s in range(N_CHIPS):
            p, off = divmod(s * kq, rows)
            acc = acc + lax.dot_general(p_refs[p][off:off + kq, :], w_refs[s][...], TN_DIMS,
                                        preferred_element_type=F32)
        o_ref[...] = acc

    def wspec(s):
        return pl.BlockSpec((None, None, kq, n), lambda i: (s, layer, 0, 0))

    return pl.pallas_call(
        body, name=name, grid=(s_len // tm,),
        in_specs=[pl.BlockSpec((rows, tm), lambda i: (0, i))] * n_p + [wspec(s) for s in range(N_CHIPS)]
        + [pl.BlockSpec((tm, n), lambda i: (i, 0))],
        out_specs=pl.BlockSpec((tm, n), lambda i: (i, 0)),
        out_shape=_sds((s_len, n), F32),
        compiler_params=_params("arbitrary"),
    )(*pieces, w4, w4, w4, w4, res)


def dw_nn(pieces, b, name):
    s_len, n = b.shape
    rows = pieces[0].shape[0]
    n_p = len(pieces)
    k = rows * n_p
    ts = _row_tile(s_len, 2048)
    n_s = s_len // ts

    def body(*refs):
        p_refs, (b_ref, o_ref, acc_ref) = refs[:n_p], refs[n_p:]
        s = pl.program_id(0)
        bv = b_ref[...]
        for p in range(n_p):
            part = jnp.dot(p_refs[p][...], bv, preferred_element_type=F32)
            at = slice(p * rows, (p + 1) * rows)
            if n_s == 1:
                o_ref[at, :] = part.astype(BF16)
                continue

            @pl.when(s == 0)
            def _():
                acc_ref[at, :] = part

            @pl.when((s > 0) & (s < n_s - 1))
            def _():
                acc_ref[at, :] += part

            @pl.when(s == n_s - 1)
            def _():
                o_ref[at, :] = (acc_ref[at, :] + part).astype(BF16)

    out = pl.pallas_call(
        body, name=name, grid=(n_s,),
        in_specs=[pl.BlockSpec((rows, ts), lambda s: (0, s))] * n_p + [pl.BlockSpec((ts, n), lambda s: (s, 0))],
        out_specs=pl.BlockSpec((k, n), lambda s: (0, 0)), out_shape=_sds((k, n), BF16),
        scratch_shapes=[pltpu.VMEM((k, n), F32)],
        compiler_params=_params("arbitrary"),
    )(*pieces, b)
    return out.reshape(N_CHIPS, k // N_CHIPS, n)


def mm_nt(dy, w4, layer, name, transposed=False):
    s_len, n = dy.shape
    mq = w4.shape[-2]
    tm = _row_tile(s_len, 512)

    def body(d_ref, w0, w1, w2, w3, o_ref):
        dv = d_ref[...]
        for s, w_ref in enumerate((w0, w1, w2, w3)):
            if transposed:
                o_ref[s * mq:(s + 1) * mq, :] = lax.dot_general(
                    w_ref[...], dv, NT_DIMS, preferred_element_type=F32).astype(BF16)
            else:
                o_ref[:, s * mq:(s + 1) * mq] = lax.dot_general(
                    dv, w_ref[...], NT_DIMS, preferred_element_type=F32).astype(BF16)

    def wspec(s):
        return pl.BlockSpec((None, None, mq, n), lambda i: (s, layer, 0, 0))

    m = N_CHIPS * mq
    return pl.pallas_call(
        body, name=name, grid=(s_len // tm,),
        in_specs=[pl.BlockSpec((tm, n), lambda i: (i, 0)), wspec(0), wspec(1), wspec(2), wspec(3)],
        out_specs=pl.BlockSpec((m, tm), lambda i: (0, i)) if transposed else pl.BlockSpec((tm, m), lambda i: (i, 0)),
        out_shape=_sds((m, s_len) if transposed else (s_len, m), BF16),
        compiler_params=_params("arbitrary"),
    )(dy, w4, w4, w4, w4)


def mm_nt_relu2_bwd(dy, w4, layer, a, name):
    s_len, n = dy.shape
    mq = w4.shape[-2]
    tm = _row_tile(s_len, 512)

    def body(d_ref, w_ref, a_ref, o_ref):
        dv = d_ref[...]
        for s in range(N_CHIPS):
            cols = slice(s * mq, (s + 1) * mq)
            dz = lax.dot_general(dv, w_ref[s], NT_DIMS, preferred_element_type=F32)
            o_ref[:, cols] = (dz * (2.0 * jnp.maximum(a_ref[:, cols].astype(F32), 0.0))).astype(BF16)

    return pl.pallas_call(
        body, name=name, grid=(s_len // tm,),
        in_specs=[pl.BlockSpec((tm, n), lambda i: (i, 0)),
                  pl.BlockSpec((N_CHIPS, None, mq, n), lambda i: (0, layer, 0, 0)),
                  pl.BlockSpec((tm, N_CHIPS * mq), lambda i: (i, 0))],
        out_specs=pl.BlockSpec((tm, N_CHIPS * mq), lambda i: (i, 0)),
        out_shape=_sds((s_len, N_CHIPS * mq), BF16),
        compiler_params=_params("arbitrary"),
    )(dy, w4, a)


def dx_norm(dy, w4, layer, x, g, dres, name):
    s_len, d = x.shape
    ns = w4.shape[-1]
    tm = _row_tile(s_len, 512)

    def body(dy_ref, w_ref, x_ref, g_ref, dr_ref, dx_ref, dxb_ref, dg_ref):
        i = pl.program_id(0)
        dh = lax.dot_general(dy_ref[:, 0:ns], w_ref[0], NT_DIMS, preferred_element_type=F32)
        for s in range(1, N_CHIPS):
            dh = dh + lax.dot_general(dy_ref[:, s * ns:(s + 1) * ns], w_ref[s], NT_DIMS,
                                      preferred_element_type=F32)
        xf = x_ref[...]
        r = lax.rsqrt(jnp.mean(xf * xf, axis=-1, keepdims=True) + EPS)
        xhat = xf * r
        dg_part = jnp.sum(dh * xhat, axis=0, keepdims=True)

        @pl.when(i == 0)
        def _():
            dg_ref[...] = dg_part

        @pl.when(i > 0)
        def _():
            dg_ref[...] += dg_part

        dxh = dh * g_ref[...]
        dx = dr_ref[...] + r * (dxh - xhat * jnp.mean(dxh * xhat, axis=-1, keepdims=True))
        dx_ref[...] = dx
        dxb_ref[...] = dx.astype(BF16)

    row = pl.BlockSpec((tm, d), lambda i: (i, 0))
    vec = pl.BlockSpec((1, d), lambda i: (0, 0))
    return pl.pallas_call(
        body, name=name, grid=(s_len // tm,),
        in_specs=[pl.BlockSpec((tm, N_CHIPS * ns), lambda i: (i, 0)),
                  pl.BlockSpec((N_CHIPS, None, d, ns), lambda i: (0, layer, 0, 0)), row, vec, row],
        out_specs=[row, row, vec],
        out_shape=[_sds((s_len, d), F32), _sds((s_len, d), BF16), _sds((1, d), F32)],
        compiler_params=_params("arbitrary"),
    )(dy, w4, x, g, dres)


def dw_mm(a, b, name, col_sharded, relu2=False):
    s_len, k = a.shape
    n = b.shape[1]
    ts = _row_tile(s_len, 2048)
    tk = min(k, 1024)
    tn = n // N_CHIPS if col_sharded else min(n, 1024)
    n_s = s_len // ts

    def body(a_ref, b_ref, o_ref, acc_ref):
        s = pl.program_id(2)
        av = a_ref[...]
        if relu2:
            t = jnp.maximum(av.astype(F32), 0.0)
            av = (t * t).astype(BF16)
        part = lax.dot_general(av, b_ref[...], TN_DIMS, preferred_element_type=F32)
        if n_s == 1:
            o_ref[...] = part.astype(BF16)
            return

        @pl.when(s == 0)
        def _():
            acc_ref[...] = part

        @pl.when((s > 0) & (s < n_s - 1))
        def _():
            acc_ref[...] += part

        @pl.when(s == n_s - 1)
        def _():
            o_ref[...] = (acc_ref[...] + part).astype(BF16)

    if col_sharded:
        out_shape = _sds((N_CHIPS, k, tn), BF16)
        out_spec = pl.BlockSpec((None, tk, tn), lambda i, j, s: (j, i, 0))
    else:
        out_shape = _sds((N_CHIPS, k // N_CHIPS, n), BF16)
        rows_per = k // N_CHIPS
        assert tk % rows_per == 0 or rows_per % tk == 0
        if tk >= rows_per:
            out_shape = _sds((k, n), BF16)
            out_spec = pl.BlockSpec((tk, tn), lambda i, j, s: (i, j))
        else:
            per = rows_per // tk
            out_spec = pl.BlockSpec((None, tk, tn), lambda i, j, s: (i // per, i % per, j))

    out = pl.pallas_call(
        body, name=name, grid=(k // tk, n // tn, n_s),
        in_specs=[pl.BlockSpec((ts, tk), lambda i, j, s: (s, i)),
                  pl.BlockSpec((ts, tn), lambda i, j, s: (s, j))],
        out_specs=out_spec, out_shape=out_shape,
        scratch_shapes=[pltpu.VMEM((tk, tn), F32)],
        compiler_params=_params("arbitrary", "arbitrary", "arbitrary"),
    )(a, b)
    if not col_sharded:
        out = out.reshape(N_CHIPS, k // N_CHIPS, n)
    return out


def ew(fn, ins, out_dtypes, name, tile_rows=256):
    rows, cols = ins[0].shape
    for a in ins:
        assert a.shape == (rows, cols), (name, a.shape, rows, cols)
    tr = rows if (rows <= tile_rows or rows % tile_rows) else tile_rows
    n_in = len(ins)

    def body(*refs):
        outs = fn(*[r[...] for r in refs[:n_in]])
        for o_ref, val in zip(refs[n_in:], outs):
            o_ref[...] = val.astype(o_ref.dtype)

    spec = pl.BlockSpec((tr, cols), lambda i: (i, 0))
    return pl.pallas_call(
        body, name=name, grid=(rows // tr,),
        in_specs=[spec] * n_in, out_specs=[spec] * len(out_dtypes),
        out_shape=[_sds((rows, cols), dt) for dt in out_dtypes],
        compiler_params=_params("arbitrary"),
    )(*ins)


def adamw(w, g, m, v, name):
    shape = w.shape
    cols = shape[-1]
    two_d = lambda a: a.reshape(-1, cols)

    def fn(wv, gv, mv, vv):
        m_new = ADAM_B1 * mv + (1.0 - ADAM_B1) * gv
        v_new = ADAM_B2 * vv + (1.0 - ADAM_B2) * (gv * gv)
        m_hat = m_new / (1.0 - ADAM_B1 ** ADAM_STEP)
        v_hat = v_new / (1.0 - ADAM_B2 ** ADAM_STEP)
        delta = -ADAM_LR * (m_hat / (jnp.sqrt(v_hat) + ADAM_EPS) + ADAM_WD * wv)
        return delta, m_new, v_new

    d, mn, vn = ew(fn, [two_d(w), two_d(g), two_d(m), two_d(v)], [F32, F32, F32], name)
    return d.reshape(shape), mn.reshape(shape), vn.reshape(shape)


def rope_tables(s_len):
    def angles(pos, dim):
        freqs = ROPE_THETA ** (-jnp.arange(0, dim, 2, dtype=F32) / dim)
        ang = pos.astype(F32)[:, None] * freqs[None, :]
        return jnp.cos(ang), jnp.sin(ang)

    pos = jnp.arange(s_len)
    rows = s_len // GRID_W
    row_idx = jnp.repeat(jnp.arange(rows), GRID_W)
    col_idx = jnp.tile(jnp.arange(GRID_W), rows)
    c1, s1 = angles(pos, HEAD_DIM)
    cr, sr = angles(row_idx, HEAD_DIM // 2)
    cc, sc = angles(col_idx, HEAD_DIM // 2)
    cos1 = jnp.tile(jnp.concatenate([c1, c1], -1), (1, 2))
    sin1 = jnp.tile(jnp.concatenate([-s1, s1], -1), (1, 2))
    cos2 = jnp.tile(jnp.concatenate([cr, cr, cc, cc], -1), (1, 2))
    sin2 = jnp.tile(jnp.concatenate([-sr, sr, -sc, sc], -1), (1, 2))
    return cos1, sin1, cos2, sin2


def _lane_iota(rows):
    return lax.broadcasted_iota(jnp.int32, (rows, LANES), 1)


def _swap(x, dist, lane):
    return jnp.where((lane & dist) != 0, pltpu.roll(x, dist, 1), pltpu.roll(x, LANES - dist, 1))


def _head_ones():
    r = lax.broadcasted_iota(jnp.int32, (LANES, LANES), 0) // HEAD_DIM
    c = lax.broadcasted_iota(jnp.int32, (LANES, LANES), 1) // HEAD_DIM
    return (r == c).astype(BF16)


def _head_sum(t, ones):
    hi = t.astype(BF16)
    lo = (t - hi.astype(F32)).astype(BF16)
    return (jnp.dot(hi, ones, preferred_element_type=F32) + jnp.dot(lo, ones, preferred_element_type=F32))


Q_SCALE = HEAD_DIM ** -0.5
LOG2E = 1.4426950408889634
LN2 = 0.6931471805599453
CHUNK_KIND = ["qa"] * 4 + ["ka", "va"] + ["qb"] * 4 + ["kb", "vb"]
QA_COL, KA_COL, QB_COL, KB_COL = 0, 512, 768, 1280


def prep_fwd(proj, tabs, qn_g, kn_g, name):
    s_len, width = proj.shape
    ts = _row_tile(s_len, 512)
    cos1, sin1, cos2, sin2 = tabs

    def body(p_ref, c1_ref, s1_ref, c2_ref, s2_ref, qg_ref, kg_ref, o_ref, kv_ref):
        lane = _lane_iota(ts)
        ones = _head_ones()
        c1, s1, c2, s2 = c1_ref[...], s1_ref[...], c2_ref[...], s2_ref[...]
        n_kv = 0
        for cb, kind in enumerate(CHUNK_KIND):
            x = p_ref[:, cb * LANES:(cb + 1) * LANES]
            if kind in ("qa", "ka"):
                y = x * c1 + _swap(x, 32, lane) * s1
            elif kind in ("qb", "kb"):
                gain = qg_ref[...] if kind == "qb" else kg_ref[...]
                ms = _head_sum(x * x, ones) * (1.0 / HEAD_DIM)
                xn = (x * lax.rsqrt(ms + EPS)) * gain
                y = xn * c2 + _swap(xn, 16, lane) * s2
            else:
                y = x
            if kind in ("qa", "qb"):
                y = y * (Q_SCALE * LOG2E)
            else:
                kv_ref[:, n_kv * LANES:(n_kv + 1) * LANES] = y.astype(BF16)
                n_kv += 1
            o_ref[cb * LANES:(cb + 1) * LANES, :] = y.T.astype(BF16)

    tab = pl.BlockSpec((ts, LANES), lambda i: (i, 0))
    vec = pl.BlockSpec((1, LANES), lambda i: (0, 0))
    return pl.pallas_call(
        body, name=name, grid=(s_len // ts,),
        in_specs=[pl.BlockSpec((ts, width), lambda i: (i, 0)), tab, tab, tab, tab, vec, vec],
        out_specs=[pl.BlockSpec((width, ts), lambda i: (0, i)), pl.BlockSpec((ts, 4 * LANES), lambda i: (i, 0))],
        out_shape=[_sds((width, s_len), BF16), _sds((s_len, 4 * LANES), BF16)],
        compiler_params=_params("arbitrary"),
    )(proj, cos1, sin1, cos2, sin2, qn_g, kn_g)


def prep_bwd(proj, dqa, dka, dva, dqb, dkb, dvb, tabs, qn_g, kn_g, name):
    s_len, width = proj.shape
    ts = _row_tile(s_len, 256)
    cos1, sin1, cos2, sin2 = tabs

    def body(p_ref, dqa_ref, dka_ref, dva_ref, dqb_ref, dkb_ref, dvb_ref,
             c1_ref, s1_ref, c2_ref, s2_ref, qg_ref, kg_ref, o_ref, dqg_ref, dkg_ref):
        i = pl.program_id(0)
        lane = _lane_iota(ts)
        c1, s1, c2, s2 = c1_ref[...], s1_ref[...], c2_ref[...], s2_ref[...]

        def rope_t(dy, cos, sin, dist):
            return dy * cos + _swap(dy * sin, dist, lane)

        ones = _head_ones()

        def norm_bwd(dy, x, gain):
            r = lax.rsqrt(_head_sum(x * x, ones) * (1.0 / HEAD_DIM) + EPS)
            xhat = x * r
            dgain = jnp.sum(dy * xhat, axis=0, keepdims=True)
            dxh = dy * gain
            dx = r * (dxh - xhat * (_head_sum(dxh * xhat, ones) * (1.0 / HEAD_DIM)))
            return dx, dgain

        dqg = jnp.zeros((1, LANES), F32)
        dkg = jnp.zeros((1, LANES), F32)
        for cb, kind in enumerate(CHUNK_KIND):
            cols = slice(cb * LANES, (cb + 1) * LANES)
            if kind == "qa":
                dx = rope_t(dqa_ref[cols, :].T * Q_SCALE, c1, s1, 32)
            elif kind == "ka":
                dx = rope_t(dka_ref[...], c1, s1, 32)
            elif kind == "va":
                dx = dva_ref[...]
            elif kind == "qb":
                qcols = slice((cb - 6) * LANES, (cb - 5) * LANES)
                dy = rope_t(dqb_ref[qcols, :].T * Q_SCALE, c2, s2, 16)
                dx, dgain = norm_bwd(dy, p_ref[:, cols], qg_ref[...])
                dqg = dqg + dgain
            elif kind == "kb":
                dy = rope_t(dkb_ref[...], c2, s2, 16)
                dx, dgain = norm_bwd(dy, p_ref[:, cols], kg_ref[...])
                dkg = dkg + dgain
            else:
                dx = dvb_ref[...]
            o_ref[:, cols] = dx.astype(BF16)

        @pl.when(i == 0)
        def _():
            dqg_ref[...] = dqg
            dkg_ref[...] = dkg

        @pl.when(i > 0)
        def _():
            dqg_ref[...] += dqg
            dkg_ref[...] += dkg

    tab = pl.BlockSpec((ts, LANES), lambda i: (i, 0))
    vec = pl.BlockSpec((1, LANES), lambda i: (0, 0))

    def dq_spec(dq):
        per = dq.shape[2] // ts
        return pl.BlockSpec((None, 4 * LANES, ts), lambda i: (i // per, 0, i % per))

    return pl.pallas_call(
        body, name=name, grid=(s_len // ts,),
        in_specs=([pl.BlockSpec((ts, width), lambda i: (i, 0)), dq_spec(dqa), tab, tab, dq_spec(dqb), tab, tab]
                  + [tab] * 4 + [vec, vec]),
        out_specs=[pl.BlockSpec((ts, width), lambda i: (i, 0)), vec, vec],
        out_shape=[_sds((s_len, width), BF16), _sds((1, LANES), F32), _sds((1, LANES), F32)],
        compiler_params=_params("arbitrary"),
    )(proj, dqa, dka, dva, dqb, dkb, dvb, cos1, sin1, cos2, sin2, qn_g, kn_g)


NEG = -1e30
GROUP = 4
KV_HEADS = 2
GROUP_W = GROUP * HEAD_DIM
LSE_ROWS = 8


def _pos_mask_t(k_start, q_start, s_len, tk, tq):
    kpos = k_start + lax.broadcasted_iota(jnp.int32, (tk, tq), 0)
    qpos = q_start + lax.broadcasted_iota(jnp.int32, (tk, tq), 1)
    return (jnp.abs(kpos - qpos) <= BLOCK) & (kpos >= 0) & (kpos < s_len)


def flash_fwd_t(qkv_t, kv_tok, q_rb, k_i, v_rb, sink, window, name, comm=None):
    s_len = qkv_t.shape[1]
    if window:
        tq = _row_tile(s_len, 512)
        tk = 256
        per = tq // tk
        n_kv = per + 2
    else:
        tq = tk = _row_tile(s_len, 1024)
        n_kv = s_len // tk
    n_kb = s_len // tk
    n_i = s_len // tq
    c_ins, c_outs, c_remote = comm if comm else ([], [], [])
    n_main = 4 if window else 3

    def body(*refs):
        main, c_in_refs = refs[:n_main], refs[n_main:n_main + len(c_ins)]
        rest = refs[n_main + len(c_ins):]
        (o_ref, lse_ref), c_out_refs = rest[:2], rest[2:2 + len(c_outs)]
        m_sc, l_sc, acc_sc = rest[2 + len(c_outs):5 + len(c_outs)]
        c_sems = rest[5 + len(c_outs):]
        if window:
            sink_ref, q_ref, k_ref, v_ref = main
        else:
            q_ref, k_ref, v_ref = main
        h, i, t = pl.program_id(0), pl.program_id(1), pl.program_id(2)
        if comm:
            @pl.when((h == 0) & (i == 0) & (t == 0))
            def _():
                _exchange_start(c_remote, c_in_refs, c_out_refs, *c_sems)

        @pl.when(t == 0)
        def _():
            for g in range(GROUP):
                if window:
                    m_sc[g] = jnp.full((1, tq), sink_ref[h * GROUP + g] * LOG2E, F32)
                    l_sc[g] = jnp.ones((1, tq), F32)
                else:
                    m_sc[g] = jnp.full((1, tq), NEG, F32)
                    l_sc[g] = jnp.zeros((1, tq), F32)
                acc_sc[g] = jnp.zeros((HEAD_DIM, tq), F32)

        k = k_ref[...]
        v_t = v_ref[...]
        if window:
            mask = _pos_mask_t((i * per - 1 + t) * tk, i * tq, s_len, tk, tq)
        s_next = jnp.dot(k, q_ref[0:HEAD_DIM, :], preferred_element_type=F32)
        for g in range(GROUP):
            s_t = s_next
            if g + 1 < GROUP:
                s_next = jnp.dot(k, q_ref[(g + 1) * HEAD_DIM:(g + 2) * HEAD_DIM, :], preferred_element_type=F32)
            if window:
                s_t = jnp.where(mask, s_t, NEG)
            m_prev = m_sc[g]
            m_new = jnp.maximum(m_prev, jnp.max(s_t, axis=0, keepdims=True))
            alpha = jnp.exp2(m_prev - m_new)
            p_t = jnp.exp2(s_t - m_new)
            l_sc[g] = alpha * l_sc[g] + jnp.sum(p_t, axis=0, keepdims=True)
            acc_sc[g] = alpha * acc_sc[g] + jnp.dot(v_t, p_t.astype(BF16), preferred_element_type=F32)
            m_sc[g] = m_new

        @pl.when(t == n_kv - 1)
        def _():
            for g in range(GROUP):
                l = l_sc[g]
                o_ref[g * HEAD_DIM:(g + 1) * HEAD_DIM, :] = (acc_sc[g] / l).astype(BF16)
                lse_ref[g * LSE_ROWS:(g + 1) * LSE_ROWS, :] = jnp.broadcast_to(
                    m_sc[g] + jnp.log(l) * LOG2E, (LSE_ROWS, tq))

        if comm:
            @pl.when((h == KV_HEADS - 1) & (i == n_i - 1) & (t == n_kv - 1))
            def _():
                _exchange_finish(c_remote, c_in_refs, c_out_refs, *c_sems)

    if window:
        kv_blk = lambda i, t: jnp.clip(i * per - 1 + t, 0, n_kb - 1)
    else:
        kv_blk = lambda i, t: t
    hbm = pl.BlockSpec(memory_space=pl.ANY)
    in_specs = [pl.BlockSpec((GROUP_W, tq), lambda h, i, t: (q_rb + h, i)),
                pl.BlockSpec((None, tk, HEAD_DIM), lambda h, i, t: (k_i + h, kv_blk(i, t), 0)),
                pl.BlockSpec((HEAD_DIM, tk), lambda h, i, t: (v_rb + h, kv_blk(i, t)))]
    args = [qkv_t, kv_tok, qkv_t]
    if window:
        in_specs = [pl.BlockSpec(memory_space=pltpu.SMEM)] + in_specs
        args = [sink] + args
    return pl.pallas_call(
        body, name=name, grid=(KV_HEADS, n_i, n_kv),
        in_specs=in_specs + [hbm] * len(c_ins),
        out_specs=[pl.BlockSpec((GROUP_W, tq), lambda h, i, t: (h, i)),
                   pl.BlockSpec((GROUP * LSE_ROWS, tq), lambda h, i, t: (h, i))] + [hbm] * len(c_outs),
        out_shape=[_sds((KV_HEADS * GROUP_W, s_len), BF16),
                   _sds((KV_HEADS * GROUP * LSE_ROWS, s_len), F32)] + list(c_outs),
        scratch_shapes=[pltpu.VMEM((GROUP, 1, tq), F32), pltpu.VMEM((GROUP, 1, tq), F32),
                        pltpu.VMEM((GROUP, HEAD_DIM, tq), F32)] + _exchange_sems(c_remote),
        compiler_params=_params("arbitrary", "arbitrary", "arbitrary"),
    )(*args, *c_ins)


def flash_bwd_t(qkv_t, kv_tok, o_t, do_t, lse, q_rb, k_i, v_i, k_rb, do_rb, sink, window, name, comm=None):
    s_len = qkv_t.shape[1]
    if window:
        tq = _row_tile(s_len, 512)
        tk = 256
        n_q = 2
    else:
        tq = tk = _row_tile(s_len, 1024)
        n_q = s_len // tq
    n_qb = s_len // tq
    n_j = s_len // tk
    c_ins, c_outs, c_remote = comm if comm else ([], [], [])
    n_main = 8 if window else 7
    n_out = 4 if window else 3

    def body(*refs):
        main, c_in_refs = refs[:n_main], refs[n_main:n_main + len(c_ins)]
        rest = refs[n_main + len(c_ins):]
        outs, c_out_refs = rest[:n_out], rest[n_out:n_out + len(c_outs)]
        dk_sc, dv_sc = rest[n_out + len(c_outs):n_out + len(c_outs) + 2]
        c_sems = rest[n_out + len(c_outs) + 2:]
        if window:
            sink_ref, q_ref, k_ref, v_ref, kt_ref, o_ref, do_ref, lse_ref = main
            dq_ref, dk_ref, dv_ref, dsink_ref = outs
        else:
            q_ref, k_ref, v_ref, kt_ref, o_ref, do_ref, lse_ref = main
            dq_ref, dk_ref, dv_ref = outs
        h, j, t = pl.program_id(0), pl.program_id(1), pl.program_id(2)
        q_blk = (j + 1) // 2 - 1 + t if window else t
        if comm:
            @pl.when((h == 0) & (j == 0) & (t == 0))
            def _():
                _exchange_start(c_remote, c_in_refs, c_out_refs, *c_sems)

        @pl.when((j == 0) & (t == 0))
        def _():
            dq_ref[...] = jnp.zeros(dq_ref.shape, F32)
            if window:
                dsink_ref[...] = jnp.zeros((8, LANES), F32)

        @pl.when(t == 0)
        def _():
            dk_sc[...] = jnp.zeros((tk, HEAD_DIM), F32)
            dv_sc[...] = jnp.zeros((tk, HEAD_DIM), F32)

        def step():
            k, v, k_t = k_ref[...], v_ref[...], kt_ref[...]
            if window:
                mask = _pos_mask_t(j * tk, q_blk * tq, s_len, tk, tq)
                lane = lax.broadcasted_iota(jnp.int32, (8, LANES), 1)
                sink_tile = jnp.zeros((8, LANES), F32)
            dk_acc = dk_sc[...]
            dv_acc = dv_sc[...]
            for g in range(GROUP):
                rows = slice(g * HEAD_DIM, (g + 1) * HEAD_DIM)
                q_t, o_g, do_g = q_ref[rows, :], o_ref[rows, :], do_ref[rows, :]
                s_t = jnp.dot(k, q_t, preferred_element_type=F32)
                if window:
                    s_t = jnp.where(mask, s_t, NEG)
                lse_row = lse_ref[g * LSE_ROWS:g * LSE_ROWS + 1, :]
                p_t = jnp.exp2(s_t - lse_row)
                delta = jnp.sum(do_g.astype(F32) * o_g.astype(F32), axis=0, keepdims=True)
                dp_t = jnp.dot(v, do_g, preferred_element_type=F32)
                ds_t = (p_t * (dp_t - delta)).astype(BF16)
                dv_acc = dv_acc + lax.dot_general(p_t.astype(BF16), do_g, NT_DIMS, preferred_element_type=F32)
                dk_acc = dk_acc + lax.dot_general(ds_t, q_t, NT_DIMS, preferred_element_type=F32)
                dq_ref[q_blk, rows, :] += jnp.dot(k_t, ds_t, preferred_element_type=F32)
                if window:
                    p_sink = jnp.exp2(sink_ref[h * GROUP + g] * LOG2E - lse_row)
                    term = -jnp.sum(p_sink * delta, axis=1, keepdims=True)
                    sink_tile = jnp.where(lane == g, term, sink_tile)
            dk_sc[...] = dk_acc
            dv_sc[...] = dv_acc
            if window:
                @pl.when((j % 2 == 0) & (t == 1))
                def _():
                    dsink_ref[...] += sink_tile

        if window:
            pl.when((q_blk >= 0) & (q_blk < n_qb))(step)
        else:
            step()

        @pl.when(t == n_q - 1)
        def _():
            dk_ref[...] = dk_sc[...] * LN2
            dv_ref[...] = dv_sc[...]

        if comm:
            @pl.when((h == KV_HEADS - 1) & (j == n_j - 1) & (t == n_q - 1))
            def _():
                _exchange_finish(c_remote, c_in_refs, c_out_refs, *c_sems)

    if window:
        qb = lambda j, t: jnp.clip((j + 1) // 2 - 1 + t, 0, n_qb - 1)
    else:
        qb = lambda j, t: t
    hbm = pl.BlockSpec(memory_space=pl.ANY)
    in_specs = [pl.BlockSpec((GROUP_W, tq), lambda h, j, t: (q_rb + h, qb(j, t))),
                pl.BlockSpec((None, tk, HEAD_DIM), lambda h, j, t: (k_i + h, j, 0)),
                pl.BlockSpec((None, tk, HEAD_DIM), lambda h, j, t: (v_i + h, j, 0)),
                pl.BlockSpec((HEAD_DIM, tk), lambda h, j, t: (k_rb + h, j)),
                pl.BlockSpec((GROUP_W, tq), lambda h, j, t: (h, qb(j, t))),
                pl.BlockSpec((GROUP_W, tq), lambda h, j, t: (do_rb + h, qb(j, t))),
                pl.BlockSpec((GROUP * LSE_ROWS, tq), lambda h, j, t: (h, qb(j, t)))]
    args = [qkv_t, kv_tok, kv_tok, qkv_t, o_t, do_t, lse]
    kv_out = _sds((KV_HEADS, s_len, HEAD_DIM), F32)
    out_specs = [pl.BlockSpec((n_qb, GROUP_W, tq), lambda h, j, t: (0, h, 0)),
                 pl.BlockSpec((None, tk, HEAD_DIM), lambda h, j, t: (h, j, 0)),
                 pl.BlockSpec((None, tk, HEAD_DIM), lambda h, j, t: (h, j, 0))]
    out_shape = [_sds((n_qb, KV_HEADS * GROUP_W, tq), F32), kv_out, kv_out]
    if window:
        in_specs = [pl.BlockSpec(memory_space=pltpu.SMEM)] + in_specs
        args = [sink] + args
        out_specs.append(pl.BlockSpec((None, 8, LANES), lambda h, j, t: (h, 0, 0)))
        out_shape.append(_sds((KV_HEADS, 8, LANES), F32))
    return pl.pallas_call(
        body, name=name, grid=(KV_HEADS, n_j, n_q),
        in_specs=in_specs + [hbm] * len(c_ins), out_specs=out_specs + [hbm] * len(c_outs),
        out_shape=out_shape + list(c_outs),
        scratch_shapes=[pltpu.VMEM((tk, HEAD_DIM), F32), pltpu.VMEM((tk, HEAD_DIM), F32)]
        + _exchange_sems(c_remote),
        compiler_params=_params("arbitrary", "arbitrary", "arbitrary"),
    )(*args, *c_ins)


SGU_GROUPS = 8
SGU_CHUNK = 128
GELU_C = float(np.sqrt(2.0 / np.pi))
GELU_A = 0.044715


def _gelu_and_grad(x):
    x2 = x * x
    t = jnp.tanh(x * (GELU_C + (GELU_C * GELU_A) * x2))
    hx = 0.5 * x
    return hx + hx * t, (0.5 + 0.5 * t) + (hx * (1.0 - t * t)) * (GELU_C + (3.0 * GELU_C * GELU_A) * x2)


def _gelu(x):
    t = jnp.tanh(x * (GELU_C + (GELU_C * GELU_A) * (x * x)))
    hx = 0.5 * x
    return hx + hx * t


def _layernorm_stats(v):
    mu = jnp.mean(v, axis=-1, keepdims=True)
    var = jnp.mean(jnp.square(v - mu), axis=-1, keepdims=True)
    rstd = lax.rsqrt(var + EPS)
    return (v - mu) * rstd, rstd


def sgu_mid_fwd(zpre, ln_g, ln_b, ws, bsb, name):
    s_len, width = zpre.shape
    d = width // 2
    ts = _row_tile(s_len, 256)

    def body(z_ref, g_ref, b_ref, ws_ref, bs_ref, y_ref):
        z = _gelu(z_ref[...])
        u, v = z[:, :d], z[:, d:]
        vhat, _ = _layernorm_stats(v)
        vn = (vhat * g_ref[...] + b_ref[...]).astype(BF16)
        for n in range(ts // SGU_CHUNK):
            rows = slice(n * SGU_CHUNK, (n + 1) * SGU_CHUNK)
            for g in range(SGU_GROUPS):
                cols = slice(g * LANES, (g + 1) * LANES)
                mixed = jnp.dot(ws_ref[g], vn[rows, cols], preferred_element_type=F32) + bs_ref[g]
                y_ref[rows, cols] = (u[rows, cols] * mixed).astype(BF16)

    vec = pl.BlockSpec((1, d), lambda i: (0, 0))
    cube = pl.BlockSpec((SGU_GROUPS, SGU_CHUNK, SGU_CHUNK), lambda i: (0, 0, 0))
    return pl.pallas_call(
        body, name=name, grid=(s_len // ts,),
        in_specs=[pl.BlockSpec((ts, width), lambda i: (i, 0)), vec, vec, cube, cube],
        out_specs=pl.BlockSpec((ts, d), lambda i: (i, 0)),
        out_shape=_sds((s_len, d), BF16),
        compiler_params=_params("arbitrary"),
    )(zpre, ln_g, ln_b, ws, bsb)


def sgu_mid_bwd(zpre, dy, ln_g, ln_b, ws, wst, bsb, name):
    s_len, width = zpre.shape
    d = width // 2
    ts = _row_tile(s_len, 256)
    n_steps = s_len // ts

    def body(z_ref, dy_ref, g_ref, b_ref, ws_ref, wst_ref, bs_ref,
             dz_ref, dws_ref, dbs_ref, dg_ref, db_ref, du_sc, dvn_sc):
        i = pl.program_id(0)

        @pl.when(i == 0)
        def _():
            dws_ref[...] = jnp.zeros(dws_ref.shape, F32)
            dbs_ref[...] = jnp.zeros(dbs_ref.shape, F32)
            dg_ref[...] = jnp.zeros(dg_ref.shape, F32)
            db_ref[...] = jnp.zeros(db_ref.shape, F32)

        zp = z_ref[...]
        z, gp = _gelu_and_grad(zp)
        u, v = z[:, :d], z[:, d:]
        vhat, rstd = _layernorm_stats(v)
        gain = g_ref[...]
        vn = (vhat * gain + b_ref[...]).astype(BF16)
        dyf = dy_ref[...].astype(F32)
        for n in range(ts // SGU_CHUNK):
            rows = slice(n * SGU_CHUNK, (n + 1) * SGU_CHUNK)
            for g in range(SGU_GROUPS):
                cols = slice(g * LANES, (g + 1) * LANES)
                vt = vn[rows, cols]
                mixed = jnp.dot(ws_ref[g], vt, preferred_element_type=F32) + bs_ref[g]
                dyt = dyf[rows, cols]
                du_sc[rows, cols] = dyt * mixed
                dmixed = dyt * u[rows, cols]
                dmb = dmixed.astype(BF16)
                dvn_sc[rows, cols] = jnp.dot(wst_ref[g], dmb, preferred_element_type=F32)
                dws_ref[g] += lax.dot_general(dmb, vt, NT_DIMS, preferred_element_type=F32)
                dbs_ref[g] += dmixed
        dvn = dvn_sc[...]
        dg_ref[...] += jnp.sum(dvn * vhat, axis=0, keepdims=True)
        db_ref[...] += jnp.sum(dvn, axis=0, keepdims=True)
        dvh = dvn * gain
        dv = rstd * (dvh - jnp.mean(dvh, axis=-1, keepdims=True)
                     - vhat * jnp.mean(dvh * vhat, axis=-1, keepdims=True))
        dz_ref[:, :d] = (du_sc[...] * gp[:, :d]).astype(BF16)
        dz_ref[:, d:] = (dv * gp[:, d:]).astype(BF16)

        @pl.when(i == n_steps - 1)
        def _():
            for g in range(SGU_GROUPS):
                tot = jnp.sum(dbs_ref[g], axis=1, keepdims=True)
                dbs_ref[g] = jnp.broadcast_to(tot, (SGU_CHUNK, LANES))

    vec = pl.BlockSpec((1, d), lambda i: (0, 0))
    cube = pl.BlockSpec((SGU_GROUPS, SGU_CHUNK, SGU_CHUNK), lambda i: (0, 0, 0))
    cube_shape = _sds((SGU_GROUPS, SGU_CHUNK, SGU_CHUNK), F32)
    return pl.pallas_call(
        body, name=name, grid=(n_steps,),
        in_specs=[pl.BlockSpec((ts, width), lambda i: (i, 0)), pl.BlockSpec((ts, d), lambda i: (i, 0)),
                  vec, vec, cube, cube, cube],
        out_specs=[pl.BlockSpec((ts, width), lambda i: (i, 0)), cube, cube, vec, vec],
        out_shape=[_sds((s_len, width), BF16), cube_shape, cube_shape, _sds((1, d), F32), _sds((1, d), F32)],
        scratch_shapes=[pltpu.VMEM((ts, d), F32), pltpu.VMEM((ts, d), F32)],
        compiler_params=_params("arbitrary"),
    )(zpre, dy, ln_g, ln_b, ws, wst, bsb)


def loss_head(x, g, target, name):
    s_len, d = x.shape
    tm = _row_tile(s_len, 512)

    def body(x_ref, g_ref, t_ref, dx_ref, dxb_ref, dg_ref, loss_ref):
        i = pl.program_id(0)
        xf = x_ref[...]
        gain = g_ref[...]
        r = lax.rsqrt(jnp.mean(xf * xf, axis=-1, keepdims=True) + EPS)
        xhat = xf * r
        err = xhat * gain - t_ref[...]
        row = jnp.mean(err * err, axis=-1, keepdims=True)
        part = 0.5 * jnp.sum(row, axis=0, keepdims=True)
        dy = err * (1.0 / d)
        dg_part = jnp.sum(dy * xhat, axis=0, keepdims=True)

        @pl.when(i == 0)
        def _():
            dg_ref[...] = dg_part
            loss_ref[...] = jnp.broadcast_to(part, (8, LANES))

        @pl.when(i > 0)
        def _():
            dg_ref[...] += dg_part
            loss_ref[...] += jnp.broadcast_to(part, (8, LANES))

        dxh = dy * gain
        dx = r * (dxh - xhat * jnp.mean(dxh * xhat, axis=-1, keepdims=True))
        dx_ref[...] = dx
        dxb_ref[...] = dx.astype(BF16)

    row_spec = pl.BlockSpec((tm, d), lambda i: (i, 0))
    vec = pl.BlockSpec((1, d), lambda i: (0, 0))
    return pl.pallas_call(
        body, name=name, grid=(s_len // tm,),
        in_specs=[row_spec, vec, row_spec],
        out_specs=[row_spec, row_spec, vec, pl.BlockSpec((8, LANES), lambda i: (0, 0))],
        out_shape=[_sds((s_len, d), F32), _sds((s_len, d), BF16), _sds((1, d), F32), _sds((8, LANES), F32)],
        compiler_params=_params("arbitrary"),
    )(x, g, target)


FLIP_BITS = {"c": (0, 0, 1), "x": (1, 0, 0), "y": (0, 1, 0), "xy": (1, 1, 0),
             "xc": (1, 0, 1), "yc": (0, 1, 1), "xyc": (1, 1, 1)}
CHIP_FLIPS = ("x", "y", "xy")


def _flip(pos, name):
    return tuple(1 - p if bit else p for p, bit in zip(pos, FLIP_BITS[name]))


def _chip(pos):
    return 2 * pos[0] + pos[1]


def _me():
    return (lax.axis_index("x"), lax.axis_index("y"), lax.axis_index("c"))


def _exchange_copy(remote, k, in_refs, out_refs, send_sems, recv_sems, sender, receiver):
    ii, src_fn, oi, dst_fn, _ = remote[k]
    return pltpu.make_async_remote_copy(
        src_ref=src_fn(in_refs[ii], sender, receiver), dst_ref=dst_fn(out_refs[oi], sender),
        send_sem=send_sems.at[k], recv_sem=recv_sems.at[k], device_id=receiver, device_id_type=MESH)


def _exchange_start(remote, in_refs, out_refs, send_sems, recv_sems):
    me = _me()
    for k in range(len(remote)):
        _exchange_copy(remote, k, in_refs, out_refs, send_sems, recv_sems, me, _flip(me, remote[k][4])).start()


def _exchange_finish(remote, in_refs, out_refs, send_sems, recv_sems):
    me = _me()
    for k in range(len(remote)):
        _exchange_copy(remote, k, in_refs, out_refs, send_sems, recv_sems, _flip(me, remote[k][4]), me).wait_recv()
    for k in range(len(remote)):
        _exchange_copy(remote, k, in_refs, out_refs, send_sems, recv_sems, me, _flip(me, remote[k][4])).wait_send()


def _exchange_sems(remote):
    n = len(remote)
    return [pltpu.SemaphoreType.DMA((n,)), pltpu.SemaphoreType.DMA((n,))] if n else []


def exchange(name, ins, out_shapes, remote, local):
    n_in, n_out = len(ins), len(out_shapes)

    def body(*refs):
        in_refs, out_refs = refs[:n_in], refs[n_in:n_in + n_out]
        send_sems, recv_sems, local_sems = refs[n_in + n_out:]
        me = _me()
        stays = []
        for k, (ii, src_fn, oi, dst_fn) in enumerate(local):
            cp = pltpu.make_async_copy(src_fn(in_refs[ii], me), dst_fn(out_refs[oi], me), local_sems.at[k])
            cp.start()
            stays.append(cp)
        _exchange_start(remote, in_refs, out_refs, send_sems, recv_sems)
        _exchange_finish(remote, in_refs, out_refs, send_sems, recv_sems)
        for cp in stays:
            cp.wait()

    hbm = pl.BlockSpec(memory_space=pl.ANY)
    return pl.pallas_call(
        body, name=name,
        in_specs=[hbm] * n_in, out_specs=[hbm] * n_out, out_shape=list(out_shapes),
        scratch_shapes=[pltpu.SemaphoreType.DMA((max(len(remote), 1),)),
                        pltpu.SemaphoreType.DMA((max(len(remote), 1),)),
                        pltpu.SemaphoreType.DMA((max(len(local), 1),))],
        compiler_params=pltpu.CompilerParams(has_side_effects=True),
    )(*ins)


def staged_push(name, ins, out_shapes, jobs, n_alias=0):
    n_in, n_out = len(ins), len(out_shapes)
    n_copies = sum(len(dsts) for _, _, dsts in jobs)
    n_remote = sum(1 for _, _, dsts in jobs for d in dsts if d[2] is not None)

    def chunk_of(ii, src_fn):
        probe = _ShapeRef(ins[ii].shape, ins[ii].dtype)
        got = src_fn(probe, (0, 0, 0))
        return tuple(got.shape), got.dtype

    classes = []
    for ii, src_fn, _ in jobs:
        c = chunk_of(ii, src_fn)
        if c not in classes:
            classes.append(c)

    def body(*refs):
        in_refs, out_refs = refs[:n_in], refs[n_in:n_in + n_out]
        bufs = refs[n_in + n_out:n_in + n_out + len(classes)]
        load_sems, out_sems, recv_sems = refs[n_in + n_out + len(classes):]
        me = _me()
        pending = [[[], []] for _ in classes]
        used = [0] * len(classes)
        arrivals = []
        k = r = 0

        def begin_load(job):
            ii, src_fn, _ = job
            cls = classes.index(chunk_of(ii, src_fn))
            slot = used[cls] % 2
            used[cls] += 1
            for kind, cp in pending[cls][slot]:
                cp.wait_send() if kind == "remote" else cp.wait()
            pending[cls][slot] = []
            load = pltpu.make_async_copy(src_fn(in_refs[ii], me), bufs[cls].at[slot], load_sems.at[2 * cls + slot])
            load.start()
            return load, cls, slot

        nxt = begin_load(jobs[0])
        for n, (ii, src_fn, dsts) in enumerate(jobs):
            load, cls, slot = nxt
            load.wait()
            buf = bufs[cls].at[slot]
            sent = []
            for oi, dst_fn, flip in dsts:
                if flip is None:
                    cp = pltpu.make_async_copy(buf, dst_fn(out_refs[oi], me), out_sems.at[k])
                    cp.start()
                    sent.append(("local", cp))
                else:
                    peer = _flip(me, flip)
                    cp = pltpu.make_async_remote_copy(
                        src_ref=buf, dst_ref=dst_fn(out_refs[oi], me), send_sem=out_sems.at[k],
                        recv_sem=recv_sems.at[r], device_id=peer, device_id_type=MESH)
                    cp.start()
                    sent.append(("remote", cp))
                    arrivals.append((r, cls, oi, dst_fn, peer))
                    r += 1
                k += 1
            pending[cls][slot] = sent
            if n + 1 < len(jobs):
                nxt = begin_load(jobs[n + 1])
        for per_class in pending:
            for slot_list in per_class:
                for kind, cp in slot_list:
                    cp.wait_send() if kind == "remote" else cp.wait()
        for r, cls, oi, dst_fn, peer in arrivals:
            pltpu.make_async_remote_copy(
                src_ref=bufs[cls].at[0], dst_ref=dst_fn(out_refs[oi], peer), send_sem=out_sems.at[0],
                recv_sem=recv_sems.at[r], device_id=peer, device_id_type=MESH).wait_recv()

    hbm = pl.BlockSpec(memory_space=pl.ANY)
    return pl.pallas_call(
        body, name=name,
        in_specs=[hbm] * n_in, out_specs=[hbm] * n_out, out_shape=list(out_shapes),
        scratch_shapes=[pltpu.VMEM((2,) + shape, dtype) for shape, dtype in classes]
        + [pltpu.SemaphoreType.DMA((2 * len(classes),)), pltpu.SemaphoreType.DMA((max(n_copies, 1),)),
           pltpu.SemaphoreType.DMA((max(n_remote, 1),))],
        input_output_aliases={i: i for i in range(n_alias)},
        compiler_params=pltpu.CompilerParams(has_side_effects=True, vmem_limit_bytes=VMEM_LIMIT),
    )(*ins)


class _ShapeRef:
    def __init__(self, shape, dtype):
        self.shape, self.dtype = tuple(shape), dtype

    @property
    def at(self):
        return self

    def __getitem__(self, idx):
        idx = idx if isinstance(idx, tuple) else (idx,)
        shape = []
        for dim, i in zip(self.shape, idx):
            if isinstance(i, slice):
                shape.append(len(range(*i.indices(dim))))
            elif hasattr(i, "size") and hasattr(i, "start"):
                shape.append(i.size)
        shape += self.shape[len(idx):]
        return _ShapeRef(shape, self.dtype)


def gather_whole(shards, name):
    whole = lambda ref, sender, receiver=None: ref
    slot = lambda ref, sender: ref.at[_chip(sender)]
    remote = [(t, whole, t, slot, flip) for t in range(len(shards)) for flip in CHIP_FLIPS]
    local = [(t, whole, t, slot) for t in range(len(shards))]
    outs = [_sds((N_CHIPS,) + a.shape, a.dtype) for a in shards]
    return exchange(name, list(shards), outs, remote, local)


def _half_axis(shape):
    return 0 if shape[0] >= 2 else 1


def gather_halves_plan(shards):
    remote = []
    for t, a in enumerate(shards):
        ax = _half_axis(a.shape)
        half = lambda ref, sender, receiver=None, ax=ax: _half(ref, sender[2], ax)
        slot = lambda ref, sender, ax=ax: _half(ref.at[_chip(sender)], sender[2], ax)
        remote += [(t, half, t, slot, flip) for flip in CHIP_FLIPS]
    outs = [_sds((N_CHIPS,) + a.shape, a.dtype) for a in shards]
    return list(shards), outs, remote


def gather_halves_fill(got, shards, name):
    n_t = len(shards)
    jobs = []
    for t, a in enumerate(shards):
        layers = a.shape[0]
        for l in range(layers):
            jobs.append((n_t + t, lambda ref, me, l=l: ref.at[l],
                         [(t, lambda ref, sender, l=l: ref.at[_chip(sender), l], None)]))
        for flip in CHIP_FLIPS:
            if _half_axis(a.shape) == 0:
                n = layers // 2
                for j in range(n):
                    at = lambda ref, pos, flip=flip, j=j, n=n: ref.at[_chip(_flip(pos, flip)), pos[2] * n + j]
                    jobs.append((t, at, [(t, at, "c")]))
            else:
                rows = a.shape[1] // 2
                at = lambda ref, pos, flip=flip, rows=rows: ref.at[
                    _chip(_flip(pos, flip)), 0, pl.ds(pos[2] * rows, rows)]
                jobs.append((t, at, [(t, at, "c")]))
    outs = [_sds(g.shape, g.dtype) for g in got]
    return staged_push(name, list(got) + list(shards), outs, jobs, n_alias=n_t)


def add_to_all(plan, buf):
    ins, outs, remote = plan
    whole = lambda ref, sender, receiver=None: ref
    more = [(len(ins), whole, len(outs), (lambda ref, sender, f=f: ref.at[f]), flip)
            for f, flip in enumerate(FLIPS_BY_INDEX)]
    return list(ins) + [buf], list(outs) + [_sds((len(more),) + buf.shape, buf.dtype)], list(remote) + more


FLIPS_BY_INDEX = ("c", "y", "yc", "x", "xc", "xy", "xyc")


def sum_devices(own, got, name):
    rows = own.shape[0]
    tr = LANES if rows % LANES == 0 else rows
    me = (4 * lax.axis_index("x") + 2 * lax.axis_index("y") + lax.axis_index("c")).astype(jnp.int32).reshape(1)
    everyone = jnp.concatenate([own[None], got], axis=0)

    def body(me_ref, a0, a1, a2, a3, a4, a5, a6, a7, o_ref):
        o_ref[...] = ((a0[...] + a1[...]) + (a2[...] + a3[...])) + ((a4[...] + a5[...]) + (a6[...] + a7[...]))

    return pl.pallas_call(
        body, name=name,
        grid_spec=pltpu.PrefetchScalarGridSpec(
            num_scalar_prefetch=1, grid=(rows // tr,),
            in_specs=[pl.BlockSpec((None, tr, LANES), lambda i, m, k=k: (m[0] ^ k, i, 0)) for k in range(8)],
            out_specs=pl.BlockSpec((tr, LANES), lambda i, m: (i, 0))),
        out_shape=_sds((rows, LANES), F32),
        compiler_params=_params("arbitrary"),
    )(me, *([everyone] * 8))


def _half(ref, core, axis):
    rows = ref.shape[axis] // 2
    idx = (slice(None),) * axis + (pl.ds(core * rows, rows),)
    return ref.at[idx]


def reduce_begin(grads, name):
    core = lax.axis_index("c").astype(jnp.int32).reshape(1)
    jobs, outs = [], []
    for t, g in enumerate(grads):
        outs.append(_sds((N_CHIPS, g.shape[1] // 2, g.shape[2]), BF16))
        for s in range(N_CHIPS):
            jobs.append((t, lambda ref, me, s=s: _half(ref.at[s], 1 - me[2], 0),
                         [(t, lambda ref, sender, s=s: ref.at[s], "c")]))
    theirs = staged_push(name + "_swap", grads, outs, jobs)
    chip_sums = [add_half(g, r, core, f"{name}_add{t}") for t, (g, r) in enumerate(zip(grads, theirs))]
    remote, outs = [], []
    for t, p in enumerate(chip_sums):
        outs.append(_sds((len(CHIP_FLIPS),) + p.shape[1:], BF16))
        for f, flip in enumerate(CHIP_FLIPS):
            remote.append((t, lambda ref, sender, receiver: ref.at[_chip(receiver)],
                           t, lambda ref, sender, f=f: ref.at[f], flip))
    return chip_sums, (chip_sums, outs, remote)


def reduce_finish(chip_sums, got, stacks, full_shapes, into, name):
    chip = (2 * lax.axis_index("x") + lax.axis_index("y")).astype(jnp.int32).reshape(1)
    totals = [sum_chips(p, r, chip, f"{name}_sum{t}") for t, (p, r) in enumerate(zip(chip_sums, got))]
    names = []
    for out_name, _ in stacks:
        if out_name not in names:
            names.append(out_name)
    names = [n for n in names if n in into] + [n for n in names if n not in into]
    kept = [into[n] for n in names if n in into]
    outs = [_sds(full_shapes[n], F32) for n in names]
    jobs = []
    for t, (out_name, layer) in enumerate(stacks):
        oi = names.index(out_name)
        rows, cols = totals[t].shape
        pieces = max(1, rows * cols * 4 // STAGE_BYTES)
        step = rows // pieces
        for q in range(pieces):
            src = lambda ref, me, q=q, step=step: ref.at[pl.ds(q * step, step)]
            place = lambda ref, sender, layer=layer, q=q, step=step, rows=rows: ref.at[
                layer, pl.ds(sender[2] * rows + q * step, step)]
            jobs.append((len(kept) + t, src, [(oi, place, None), (oi, place, "c")]))
    full = staged_push(name + "_share", kept + totals, outs, jobs, n_alias=len(kept))
    return {**into, **dict(zip(names, full))}


STAGE_BYTES = 1024 * 1024


def add_half(g, theirs, core, name):
    n_s, rows, cols = g.shape
    half = rows // 2
    tr = _row_tile(half, 256)
    nb = half // tr

    def body(core_ref, g_ref, t_ref, o_ref):
        o_ref[...] = (g_ref[...].astype(F32) + t_ref[...].astype(F32)).astype(BF16)

    return pl.pallas_call(
        body, name=name,
        grid_spec=pltpu.PrefetchScalarGridSpec(
            num_scalar_prefetch=1, grid=(n_s, nb),
            in_specs=[pl.BlockSpec((None, tr, cols), lambda s, i, c: (s, c[0] * nb + i, 0)),
                      pl.BlockSpec((None, tr, cols), lambda s, i, c: (s, i, 0))],
            out_specs=pl.BlockSpec((None, tr, cols), lambda s, i, c: (s, i, 0))),
        out_shape=_sds((n_s, half, cols), BF16),
        compiler_params=_params("arbitrary", "arbitrary"),
    )(core, g, theirs)


def sum_chips(mine, theirs, chip, name):
    _, half, cols = mine.shape
    tr = _row_tile(half, 256)

    def body(chip_ref, m_ref, a_ref, b_ref, c_ref, o_ref):
        o_ref[...] = ((m_ref[...].astype(F32) + a_ref[...].astype(F32))
                      + b_ref[...].astype(F32)) + c_ref[...].astype(F32)

    got = lambda f: pl.BlockSpec((None, tr, cols), lambda i, ch: (f, i, 0))
    return pl.pallas_call(
        body, name=name,
        grid_spec=pltpu.PrefetchScalarGridSpec(
            num_scalar_prefetch=1, grid=(half // tr,),
            in_specs=[pl.BlockSpec((None, tr, cols), lambda i, ch: (ch[0], i, 0)), got(0), got(1), got(2)],
            out_specs=pl.BlockSpec((tr, cols), lambda i, ch: (i, 0))),
        out_shape=_sds((half, cols), F32),
        compiler_params=_params("arbitrary"),
    )(chip, mine, theirs, theirs, theirs)


def _tok(t):
    return t.transpose(1, 0, 2).reshape(t.shape[1], t.shape[0] * t.shape[2])


def _heads(t):
    return t.reshape(t.shape[0], t.shape[1] // HEAD_DIM, HEAD_DIM).transpose(1, 0, 2)


def _tile2(vec):
    return jnp.tile(vec.reshape(1, HEAD_DIM), (1, 2))


REST = ("att_w_in", "att_w_out", "sgu_w_in", "sgu_w_out", "mlp_w1", "mlp_w2")
LAST_GROUP = (("att_w_in", 0),)
LATE_SMALL = ("att_norm", "att_sink", "att_qnorm", "att_knorm")


def local_step(x, target, first, rest_shards, rep, full_shapes):
    s_len, d = x.shape
    tabs = rope_tables(s_len)
    depth = rep["mlp_norm"].shape[0]
    row = lambda a: a.reshape(1, -1)
    saved = []
    h = x
    gw = {"att_w_in": [first]}

    def wl(name, idx):
        return (gw[name][idx], 0) if name == "att_w_in" else (gw[name], idx)

    for layer in range(depth):
        i = layer // 2
        tag = f"l{layer}"
        if layer % 2 == 0:
            hn, proj = norm_mm(h, row(rep["att_norm"][i]), *wl("att_w_in", i), F32, tag + "_att_proj")
            qkv_t, kv = prep_fwd(proj, tabs, _tile2(rep["att_qnorm"][i]), _tile2(rep["att_knorm"][i]),
                                 tag + "_att_prep")
            kv_tok = _heads(kv)
            oa, lse_a = flash_fwd_t(qkv_t, kv_tok, QA_COL // GROUP_W, 0, (KA_COL + LANES) // HEAD_DIM,
                                    rep["att_sink"][i], True, tag + "_win_fwd")
            plan = gather_halves_plan(rest_shards) if layer == 0 else None
            ob, lse_b, *got = flash_fwd_t(qkv_t, kv_tok, QB_COL // GROUP_W, 4, (KB_COL + LANES) // HEAD_DIM,
                                          None, False, tag + "_grid_fwd", comm=plan)
            if layer == 0:
                rest = dict(zip(REST, gather_halves_fill(got, rest_shards, "gather_rest_fill")))
                gw["att_w_in"].append(rest.pop("att_w_in"))
                gw.update(rest)
            out = mm_res_t([oa, ob], *wl("att_w_out", i), h, tag + "_att_out")
            mix_saved = (h, hn, proj, qkv_t, kv_tok, oa, ob, lse_a, lse_b)
        else:
            hn, zpre = norm_mm(h, row(rep["sgu_norm"][i]), *wl("sgu_w_in", i), F32, tag + "_sgu_in")
            ws = rep["sgu_w_s"][i].astype(BF16)
            bsb = jnp.broadcast_to(rep["sgu_b_s"][i][:, :, None], (SGU_GROUPS, SGU_CHUNK, LANES))
            y = sgu_mid_fwd(zpre, row(rep["sgu_ln_g"][i]), row(rep["sgu_ln_b"][i]), ws, bsb, tag + "_sgu_mid")
            out = mm_res(y, *wl("sgu_w_out", i), h, tag + "_sgu_out")
            mix_saved = (h, hn, zpre, y, ws, bsb)
        hm, a = norm_mm(out, row(rep["mlp_norm"][layer]), *wl("mlp_w1", layer), BF16, tag + "_mlp_up")
        nxt = mm_res(a, *wl("mlp_w2", layer), out, tag + "_mlp_down", relu2=True)
        saved.append((mix_saved, (out, hm, a)))
        h = nxt
    dh, dhb, d_final, loss_tile = loss_head(h, row(rep["final_norm"]), target, "loss_head")
    big, tags = [], []
    small = {k: [jnp.zeros(v.shape[1:], F32)] * v.shape[0] for k, v in rep.items() if k != "final_norm"}
    small["final_norm"] = d_final.reshape(-1)
    stacked = lambda: [small[n] if n == "final_norm" else jnp.stack(small[n]) for n in SMALL]
    for layer in reversed(range(depth)):
        i = layer // 2
        tag = f"l{layer}"
        mix_saved, (xin, hm, a) = saved[layer]
        da = mm_nt_relu2_bwd(dhb, *wl("mlp_w2", layer), a, tag + "_mlp_down_bwd")
        big.append(dw_mm(a, dhb, tag + "_mlp_dw2", col_sharded=False, relu2=True))
        tags.append(("mlp_w2", layer))
        big.append(dw_mm(hm, da, tag + "_mlp_dw1", col_sharded=True))
        tags.append(("mlp_w1", layer))
        dh, dhb, dg = dx_norm(da, *wl("mlp_w1", layer), xin, row(rep["mlp_norm"][layer]), dh, tag + "_mlp_up_bwd")
        small["mlp_norm"][layer] = dg.reshape(-1)
        if layer % 2 == 0:
            xin, hn, proj, qkv_t, kv_tok, oa, ob, lse_a, lse_b = mix_saved
            do_t = mm_nt(dhb, *wl("att_w_out", i), tag + "_att_out_bwd", transposed=True)
            big.append(dw_nn([oa, ob], dhb, tag + "_att_dwout"))
            tags.append(("att_w_out", i))
            dqa, dka, dva, dsink = flash_bwd_t(qkv_t, kv_tok, oa, do_t, lse_a, QA_COL // GROUP_W, 0, 2,
                                               KA_COL // HEAD_DIM, 0, rep["att_sink"][i], True, tag + "_win_bwd")
            plan = None
            if layer == 0:
                early = [k for k, t in enumerate(tags) if t not in LAST_GROUP]
                chip_sums, plan = reduce_begin([big[k] for k in early], "grads1")
                early_pack = _pack(stacked() + [loss_tile[0, :1]])
                plan = add_to_all(plan, early_pack)
            dqb, dkb, dvb, *got = flash_bwd_t(qkv_t, kv_tok, ob, do_t, lse_b, QB_COL // GROUP_W, 4, 6,
                                              KB_COL // HEAD_DIM, 2, None, False, tag + "_grid_bwd", comm=plan)
            if layer == 0:
                grads = reduce_finish(chip_sums, got[:-1], [tags[k] for k in early], full_shapes, {}, "grads1")
                early_sum = sum_devices(early_pack, got[-1], "sum_small")
            qg, kg = _tile2(rep["att_qnorm"][i]), _tile2(rep["att_knorm"][i])
            dproj, dqg, dkg = prep_bwd(proj, dqa, _tok(dka), _tok(dva), dqb, _tok(dkb), _tok(dvb),
                                       tabs, qg, kg, tag + "_att_prep_bwd")
            big.append(dw_mm(hn, dproj, tag + "_att_dwin", col_sharded=True))
            tags.append(("att_w_in", i))
            dh, dhb, dg = dx_norm(dproj, *wl("att_w_in", i), xin, row(rep["att_norm"][i]), dh, tag + "_att_proj_bwd")
            small["att_norm"][i] = dg.reshape(-1)
            small["att_sink"][i] = dsink[:, 0, :GROUP].reshape(-1)
            small["att_qnorm"][i] = dqg[0, :HEAD_DIM] + dqg[0, HEAD_DIM:]
            small["att_knorm"][i] = dkg[0, :HEAD_DIM] + dkg[0, HEAD_DIM:]
        else:
            xin, hn, zpre, y, ws, bsb = mix_saved
            dy = mm_nt(dhb, *wl("sgu_w_out", i), tag + "_sgu_out_bwd")
            big.append(dw_mm(y, dhb, tag + "_sgu_dwout", col_sharded=False))
            tags.append(("sgu_w_out", i))
            wst = ws.transpose(0, 2, 1)
            dz, dws, dbs, dlg, dlb = sgu_mid_bwd(zpre, dy, row(rep["sgu_ln_g"][i]), row(rep["sgu_ln_b"][i]),
                                                 ws, wst, bsb, tag + "_sgu_mid_bwd")
            big.append(dw_mm(hn, dz, tag + "_sgu_dwin", col_sharded=True))
            tags.append(("sgu_w_in", i))
            dh, dhb, dg = dx_norm(dz, *wl("sgu_w_in", i), xin, row(rep["sgu_norm"][i]), dh, tag + "_sgu_in_bwd")
            small["sgu_norm"][i] = dg.reshape(-1)
            small["sgu_ln_g"][i] = dlg.reshape(-1)
            small["sgu_ln_b"][i] = dlb.reshape(-1)
            small["sgu_w_s"][i] = dws
            small["sgu_b_s"][i] = dbs[:, :, 0]
    late = [k for k, t in enumerate(tags) if t in LAST_GROUP]
    chip_sums, plan = reduce_begin([big[k] for k in late], "grads2")
    late_pack = _pack([small[n][0] for n in LATE_SMALL])
    got = exchange("grads2_scatter", *add_to_all(plan, late_pack), [])
    grads = reduce_finish(chip_sums, got[:-1], [tags[k] for k in late], full_shapes, grads, "grads2")
    late_sum = sum_devices(late_pack, got[-1], "sum_small_late")
    shapes = [a.shape for a in stacked()]
    *small_g, loss = _unpack(early_sum, shapes + [()])
    small_g = dict(zip(SMALL, small_g))
    for n, g in zip(LATE_SMALL, _unpack(late_sum, [small[n][0].shape for n in LATE_SMALL])):
        small_g[n] = small_g[n].at[0].add(g)
    return loss, dh, grads, small_g


BIG = ("att_w_in", "att_w_out", "sgu_w_in", "sgu_w_out", "mlp_w1", "mlp_w2")
SHARDED_VEC = ("sgu_norm", "sgu_ln_g", "sgu_ln_b")
REPLICATED = ("att_norm", "att_sink", "att_qnorm", "att_knorm", "sgu_w_s", "sgu_b_s", "mlp_norm", "final_norm")
WEIGHTS = ("att_norm", "att_w_in", "att_sink", "att_qnorm", "att_knorm", "att_w_out", "sgu_norm", "sgu_w_in",
           "sgu_ln_g", "sgu_ln_b", "sgu_w_s", "sgu_b_s", "sgu_w_out", "mlp_norm", "mlp_w1", "mlp_w2", "final_norm")
SMALL = tuple(n for n in WEIGHTS if n not in BIG)
PACK_ALIGN = 8 * LANES


def _pack(arrays):
    flat = jnp.concatenate([a.reshape(-1) for a in arrays])
    pad = -flat.shape[0] % PACK_ALIGN
    return jnp.pad(flat, (0, pad)).reshape(-1, LANES)


def _unpack(flat2d, shapes):
    flat = flat2d.reshape(-1)
    out, off = [], 0
    for shape in shapes:
        size = int(np.prod(shape))
        out.append(flat[off:off + size].reshape(shape))
        off += size
    return out


def kernel(x, att_norm, att_w_in, att_sink, att_qnorm, att_knorm, att_w_out, sgu_norm, sgu_w_in, sgu_ln_g, sgu_ln_b, sgu_w_s, sgu_b_s, sgu_w_out, mlp_norm, mlp_w1, mlp_w2, final_norm, loss_target, m_att_norm, m_att_w_in, m_att_sink, m_att_qnorm, m_att_knorm, m_att_w_out, m_sgu_norm, m_sgu_w_in, m_sgu_ln_g, m_sgu_ln_b, m_sgu_w_s, m_sgu_b_s, m_sgu_w_out, m_mlp_norm, m_mlp_w1, m_mlp_w2, m_final_norm, v_att_norm, v_att_w_in, v_att_sink, v_att_qnorm, v_att_knorm, v_att_w_out, v_sgu_norm, v_sgu_w_in, v_sgu_ln_g, v_sgu_ln_b, v_sgu_w_s, v_sgu_b_s, v_sgu_w_out, v_mlp_norm, v_mlp_w1, v_mlp_w2, v_final_norm):
    w = dict(att_norm=att_norm, att_w_in=att_w_in, att_sink=att_sink, att_qnorm=att_qnorm, att_knorm=att_knorm,
             att_w_out=att_w_out, sgu_norm=sgu_norm, sgu_w_in=sgu_w_in, sgu_ln_g=sgu_ln_g, sgu_ln_b=sgu_ln_b,
             sgu_w_s=sgu_w_s, sgu_b_s=sgu_b_s, sgu_w_out=sgu_w_out, mlp_norm=mlp_norm, mlp_w1=mlp_w1,
             mlp_w2=mlp_w2, final_norm=final_norm)
    m = dict(att_norm=m_att_norm, att_w_in=m_att_w_in, att_sink=m_att_sink, att_qnorm=m_att_qnorm,
             att_knorm=m_att_knorm, att_w_out=m_att_w_out, sgu_norm=m_sgu_norm, sgu_w_in=m_sgu_w_in,
             sgu_ln_g=m_sgu_ln_g, sgu_ln_b=m_sgu_ln_b, sgu_w_s=m_sgu_w_s, sgu_b_s=m_sgu_b_s,
             sgu_w_out=m_sgu_w_out, mlp_norm=m_mlp_norm, mlp_w1=m_mlp_w1, mlp_w2=m_mlp_w2,
             final_norm=m_final_norm)
    v = dict(att_norm=v_att_norm, att_w_in=v_att_w_in, att_sink=v_att_sink, att_qnorm=v_att_qnorm,
             att_knorm=v_att_knorm, att_w_out=v_att_w_out, sgu_norm=v_sgu_norm, sgu_w_in=v_sgu_w_in,
             sgu_ln_g=v_sgu_ln_g, sgu_ln_b=v_sgu_ln_b, sgu_w_s=v_sgu_w_s, sgu_b_s=v_sgu_b_s,
             sgu_w_out=v_sgu_w_out, mlp_norm=v_mlp_norm, mlp_w1=v_mlp_w1, mlp_w2=v_mlp_w2,
             final_norm=v_final_norm)
    chip = 2 * lax.axis_index("x") + lax.axis_index("y")

    vecs = jnp.stack([w[n] for n in SHARDED_VEC])
    wb = {n: w[n].astype(BF16) for n in BIG}
    first, vec_all = gather_whole([wb["att_w_in"][0:1], vecs], "gather_first")
    rest_shards = [wb[n][1:2] if n == "att_w_in" else wb[n] for n in REST]
    vec_full = vec_all.transpose(1, 2, 0, 3).reshape(vecs.shape[0], vecs.shape[1], -1)
    rep = {n: w[n] for n in REPLICATED}
    rep.update({n: vec_full[k] for k, n in enumerate(SHARDED_VEC)})

    loss, grad_x, grads, small_g = local_step(x[0], loss_target[0], first, rest_shards, rep,
                                              {n: w[n].shape for n in BIG})
    width = w["sgu_norm"].shape[1]
    for n in SHARDED_VEC:
        small_g[n] = lax.dynamic_slice_in_dim(small_g[n], chip * width, width, axis=1)
    grads.update(small_g)
    for n in BIG:
        grads[n] = grads[n].reshape(w[n].shape)

    delta, new_m, new_v = {}, {}, {}
    for n in WEIGHTS:
        shape = w[n].shape
        two_d = (lambda a: a.reshape(1, -1)) if len(shape) == 1 else (lambda a: a)
        dn, mn, vn = adamw(two_d(w[n]), two_d(grads[n]), two_d(m[n]), two_d(v[n]), "adamw_" + n)
        delta[n], new_m[n], new_v[n] = dn.reshape(shape), mn.reshape(shape), vn.reshape(shape)
    return (loss, grad_x[None], *[grads[n] for n in WEIGHTS], *[delta[n] for n in WEIGHTS],
            *[new_m[n] for n in WEIGHTS], *[new_v[n] for n in WEIGHTS])
```

```python
import functools

import numpy as np
import jax
import jax.numpy as jnp
from jax import lax
from jax.experimental import pallas as pl
from jax.experimental.pallas import tpu as pltpu

F32 = jnp.float32
BF16 = jnp.bfloat16
MESH = pl.DeviceIdType.MESH

EPS = 1e-6
HEAD_DIM = 64
BLOCK = 128
GRID_W = 64
ROPE_THETA = 10000.0
N_CHIPS = 4
LANES = 128
V7X_VMEM_BYTES = 64 * 1024 * 1024
VMEM_LIMIT = V7X_VMEM_BYTES - 8 * 1024 * 1024

ADAM_LR = 0.001
ADAM_B1 = 0.9
ADAM_B2 = 0.999
ADAM_EPS = 1e-08
ADAM_WD = 0.01
ADAM_STEP = 10

NT_DIMS = (((1,), (1,)), ((), ()))
TN_DIMS = (((0,), (0,)), ((), ()))


def _params(*sem):
    return pltpu.CompilerParams(dimension_semantics=sem, vmem_limit_bytes=VMEM_LIMIT)


def _sds(shape, dtype):
    return jax.ShapeDtypeStruct(tuple(shape), dtype)


def _row_tile(rows, want):
    t = min(rows, want)
    assert rows % t == 0, (rows, want)
    return t


def norm_mm(x, g, w4, layer, out_dtype, name):
    s_len, d = x.shape
    ns = w4.shape[-1]
    tm = _row_tile(s_len, 512)

    def body(x_ref, g_ref, w_ref, h_ref, y_ref):
        xf = x_ref[...]
        r = lax.rsqrt(jnp.mean(xf * xf, axis=-1, keepdims=True) + EPS)
        h = ((xf * r) * g_ref[...]).astype(BF16)
        h_ref[...] = h
        for s in range(N_CHIPS):
            y_ref[:, s * ns:(s + 1) * ns] = jnp.dot(h, w_ref[s], preferred_element_type=F32).astype(y_ref.dtype)

    return pl.pallas_call(
        body, name=name, grid=(s_len // tm,),
        in_specs=[pl.BlockSpec((tm, d), lambda i: (i, 0)),
                  pl.BlockSpec((1, d), lambda i: (0, 0)),
                  pl.BlockSpec((N_CHIPS, None, d, ns), lambda i: (0, layer, 0, 0))],
        out_specs=[pl.BlockSpec((tm, d), lambda i: (i, 0)),
                   pl.BlockSpec((tm, N_CHIPS * ns), lambda i: (i, 0))],
        out_shape=[_sds((s_len, d), BF16), _sds((s_len, N_CHIPS * ns), out_dtype)],
        compiler_params=_params("arbitrary"),
    )(x, g, w4)


def mm_res(a, w4, layer, res, name, relu2=False):
    s_len, k = a.shape
    kq, n = w4.shape[-2:]
    assert kq * N_CHIPS == k
    tm = _row_tile(s_len, 256 if k > 1024 else 512)

    def body(a_ref, w0, w1, w2, w3, r_ref, o_ref):
        acc = r_ref[...]
        for s, w_ref in enumerate((w0, w1, w2, w3)):
            av = a_ref[:, s * kq:(s + 1) * kq]
            if relu2:
                t = jnp.maximum(av.astype(F32), 0.0)
                av = (t * t).astype(BF16)
            acc = acc + jnp.dot(av, w_ref[...], preferred_element_type=F32)
        o_ref[...] = acc

    def wspec(s):
        return pl.BlockSpec((None, None, kq, n), lambda i: (s, layer, 0, 0))

    return pl.pallas_call(
        body, name=name, grid=(s_len // tm,),
        in_specs=[pl.BlockSpec((tm, k), lambda i: (i, 0)), wspec(0), wspec(1), wspec(2), wspec(3),
                  pl.BlockSpec((tm, n), lambda i: (i, 0))],
        out_specs=pl.BlockSpec((tm, n), lambda i: (i, 0)),
        out_shape=_sds((s_len, n), F32),
        compiler_params=_params("arbitrary"),
    )(a, w4, w4, w4, w4, res)


def mm_res_t(pieces, w4, layer, res, name):
    s_len = res.shape[0]
    kq, n = w4.shape[-2:]
    rows = pieces[0].shape[0]
    assert rows % kq == 0 and rows * len(pieces) == kq * N_CHIPS
    tm = _row_tile(s_len, 512)
    n_p = len(pieces)

    def body(*refs):
        p_refs, w_refs, (r_ref, o_ref) = refs[:n_p], refs[n_p:n_p + N_CHIPS], refs[n_p + N_CHIPS:]
        acc = r_ref[...]
        for s in range(N_CHIPS):
            p, off = divmod(s * kq, rows)
            acc = acc + lax.dot_general(p_refs[p][off:off + kq, :], w_refs[s][...], TN_DIMS,
                                        preferred_element_type=F32)
        o_ref[...] = acc

    def wspec(s):
        return pl.BlockSpec((None, None, kq, n), lambda i: (s, layer, 0, 0))

    return pl.pallas_call(
        body, name=name, grid=(s_len // tm,),
        in_specs=[pl.BlockSpec((rows, tm), lambda i: (0, i))] * n_p + [wspec(s) for s in range(N_CHIPS)]
        + [pl.BlockSpec((tm, n), lambda i: (i, 0))],
        out_specs=pl.BlockSpec((tm, n), lambda i: (i, 0)),
        out_shape=_sds((s_len, n), F32),
        compiler_params=_params("arbitrary"),
    )(*pieces, w4, w4, w4, w4, res)


def dw_nn(pieces, b, name):
    s_len, n = b.shape
    rows = pieces[0].shape[0]
    n_p = len(pieces)
    k = rows * n_p
    ts = _row_tile(s_len, 2048)
    n_s = s_len // ts

    def body(*refs):
        p_refs, (b_ref, o_ref, acc_ref) = refs[:n_p], refs[n_p:]
        s = pl.program_id(0)
        bv = b_ref[...]
        for p in range(n_p):
            part = jnp.dot(p_refs[p][...], bv, preferred_element_type=F32)
            at = slice(p * rows, (p + 1) * rows)
            if n_s == 1:
                o_ref[at, :] = part.astype(BF16)
                continue

            @pl.when(s == 0)
            def _():
                acc_ref[at, :] = part

            @pl.when((s > 0) & (s < n_s - 1))
            def _():
                acc_ref[at, :] += part

            @pl.when(s == n_s - 1)
            def _():
                o_ref[at, :] = (acc_ref[at, :] + part).astype(BF16)

    out = pl.pallas_call(
        body, name=name, grid=(n_s,),
        in_specs=[pl.BlockSpec((rows, ts), lambda s: (0, s))] * n_p + [pl.BlockSpec((ts, n), lambda s: (s, 0))],
        out_specs=pl.BlockSpec((k, n), lambda s: (0, 0)), out_shape=_sds((k, n), BF16),
        scratch_shapes=[pltpu.VMEM((k, n), F32)],
        compiler_params=_params("arbitrary"),
    )(*pieces, b)
    return out.reshape(N_CHIPS, k // N_CHIPS, n)


def mm_nt(dy, w4, layer, name, transposed=False):
    s_len, n = dy.shape
    mq = w4.shape[-2]
    tm = _row_tile(s_len, 512)

    def body(d_ref, w0, w1, w2, w3, o_ref):
        dv = d_ref[...]
        for s, w_ref in enumerate((w0, w1, w2, w3)):
            if transposed:
                o_ref[s * mq:(s + 1) * mq, :] = lax.dot_general(
                    w_ref[...], dv, NT_DIMS, preferred_element_type=F32).astype(BF16)
            else:
                o_ref[:, s * mq:(s + 1) * mq] = lax.dot_general(
                    dv, w_ref[...], NT_DIMS, preferred_element_type=F32).astype(BF16)

    def wspec(s):
        return pl.BlockSpec((None, None, mq, n), lambda i: (s, layer, 0, 0))

    m = N_CHIPS * mq
    return pl.pallas_call(
        body, name=name, grid=(s_len // tm,),
        in_specs=[pl.BlockSpec((tm, n), lambda i: (i, 0)), wspec(0), wspec(1), wspec(2), wspec(3)],
        out_specs=pl.BlockSpec((m, tm), lambda i: (0, i)) if transposed else pl.BlockSpec((tm, m), lambda i: (i, 0)),
        out_shape=_sds((m, s_len) if transposed else (s_len, m), BF16),
        compiler_params=_params("arbitrary"),
    )(dy, w4, w4, w4, w4)


def mm_nt_relu2_bwd(dy, w4, layer, a, name):
    s_len, n = dy.shape
    mq = w4.shape[-2]
    tm = _row_tile(s_len, 512)

    def body(d_ref, w_ref, a_ref, o_ref):
        dv = d_ref[...]
        for s in range(N_CHIPS):
            cols = slice(s * mq, (s + 1) * mq)
            dz = lax.dot_general(dv, w_ref[s], NT_DIMS, preferred_element_type=F32)
            o_ref[:, cols] = (dz * (2.0 * jnp.maximum(a_ref[:, cols].astype(F32), 0.0))).astype(BF16)

    return pl.pallas_call(
        body, name=name, grid=(s_len // tm,),
        in_specs=[pl.BlockSpec((tm, n), lambda i: (i, 0)),
                  pl.BlockSpec((N_CHIPS, None, mq, n), lambda i: (0, layer, 0, 0)),
                  pl.BlockSpec((tm, N_CHIPS * mq), lambda i: (i, 0))],
        out_specs=pl.BlockSpec((tm, N_CHIPS * mq), lambda i: (i, 0)),
        out_shape=_sds((s_len, N_CHIPS * mq), BF16),
        compiler_params=_params("arbitrary"),
    )(dy, w4, a)


def dx_norm(dy, w4, layer, x, g, dres, name):
    s_len, d = x.shape
    ns = w4.shape[-1]
    tm = _row_tile(s_len, 512)

    def body(dy_ref, w_ref, x_ref, g_ref, dr_ref, dx_ref, dxb_ref, dg_ref):
        i = pl.program_id(0)
        dh = lax.dot_general(dy_ref[:, 0:ns], w_ref[0], NT_DIMS, preferred_element_type=F32)
        for s in range(1, N_CHIPS):
            dh = dh + lax.dot_general(dy_ref[:, s * ns:(s + 1) * ns], w_ref[s], NT_DIMS,
                                      preferred_element_type=F32)
        xf = x_ref[...]
        r = lax.rsqrt(jnp.mean(xf * xf, axis=-1, keepdims=True) + EPS)
        xhat = xf * r
        dg_part = jnp.sum(dh * xhat, axis=0, keepdims=True)

        @pl.when(i == 0)
        def _():
            dg_ref[...] = dg_part

        @pl.when(i > 0)
        def _():
            dg_ref[...] += dg_part

        dxh = dh * g_ref[...]
        dx = dr_ref[...] + r * (dxh - xhat * jnp.mean(dxh * xhat, axis=-1, keepdims=True))
        dx_ref[...] = dx
        dxb_ref[...] = dx.astype(BF16)

    row = pl.BlockSpec((tm, d), lambda i: (i, 0))
    vec = pl.BlockSpec((1, d), lambda i: (0, 0))
    return pl.pallas_call(
        body, name=name, grid=(s_len // tm,),
        in_specs=[pl.BlockSpec((tm, N_CHIPS * ns), lambda i: (i, 0)),
                  pl.BlockSpec((N_CHIPS, None, d, ns), lambda i: (0, layer, 0, 0)), row, vec, row],
        out_specs=[row, row, vec],
        out_shape=[_sds((s_len, d), F32), _sds((s_len, d), BF16), _sds((1, d), F32)],
        compiler_params=_params("arbitrary"),
    )(dy, w4, x, g, dres)


def dw_mm(a, b, name, col_sharded, relu2=False):
    s_len, k = a.shape
    n = b.shape[1]
    ts = _row_tile(s_len, 2048)
    tk = min(k, 1024)
    tn = n // N_CHIPS if col_sharded else min(n, 1024)
    n_s = s_len // ts

    def body(a_ref, b_ref, o_ref, acc_ref):
        s = pl.program_id(2)
        av = a_ref[...]
        if relu2:
            t = jnp.maximum(av.astype(F32), 0.0)
            av = (t * t).astype(BF16)
        part = lax.dot_general(av, b_ref[...], TN_DIMS, preferred_element_type=F32)
        if n_s == 1:
            o_ref[...] = part.astype(BF16)
            return

        @pl.when(s == 0)
        def _():
            acc_ref[...] = part

        @pl.when((s > 0) & (s < n_s - 1))
        def _():
            acc_ref[...] += part

        @pl.when(s == n_s - 1)
        def _():
            o_ref[...] = (acc_ref[...] + part).astype(BF16)

    if col_sharded:
        out_shape = _sds((N_CHIPS, k, tn), BF16)
        out_spec = pl.BlockSpec((None, tk, tn), lambda i, j, s: (j, i, 0))
    else:
        out_shape = _sds((N_CHIPS, k // N_CHIPS, n), BF16)
        rows_per = k // N_CHIPS
        assert tk % rows_per == 0 or rows_per % tk == 0
        if tk >= rows_per:
            out_shape = _sds((k, n), BF16)
            out_spec = pl.BlockSpec((tk, tn), lambda i, j, s: (i, j))
        else:
            per = rows_per // tk
            out_spec = pl.BlockSpec((None, tk, tn), lambda i, j, s: (i // per, i % per, j))

    out = pl.pallas_call(
        body, name=name, grid=(k // tk, n // tn, n_s),
        in_specs=[pl.BlockSpec((ts, tk), lambda i, j, s: (s, i)),
                  pl.BlockSpec((ts, tn), lambda i, j, s: (s, j))],
        out_specs=out_spec, out_shape=out_shape,
        scratch_shapes=[pltpu.VMEM((tk, tn), F32)],
        compiler_params=_params("arbitrary", "arbitrary", "arbitrary"),
    )(a, b)
    if not col_sharded:
        out = out.reshape(N_CHIPS, k // N_CHIPS, n)
    return out


def ew(fn, ins, out_dtypes, name, tile_rows=256):
    rows, cols = ins[0].shape
    for a in ins:
        assert a.shape == (rows, cols), (name, a.shape, rows, cols)
    tr = rows if (rows <= tile_rows or rows % tile_rows) else tile_rows
    n_in = len(ins)

    def body(*refs):
        outs = fn(*[r[...] for r in refs[:n_in]])
        for o_ref, val in zip(refs[n_in:], outs):
            o_ref[...] = val.astype(o_ref.dtype)

    spec = pl.BlockSpec((tr, cols), lambda i: (i, 0))
    return pl.pallas_call(
        body, name=name, grid=(rows // tr,),
        in_specs=[spec] * n_in, out_specs=[spec] * len(out_dtypes),
        out_shape=[_sds((rows, cols), dt) for dt in out_dtypes],
        compiler_params=_params("arbitrary"),
    )(*ins)


def adamw(w, g, m, v, name):
    shape = w.shape
    cols = shape[-1]
    two_d = lambda a: a.reshape(-1, cols)

    def fn(wv, gv, mv, vv):
        m_new = ADAM_B1 * mv + (1.0 - ADAM_B1) * gv
        v_new = ADAM_B2 * vv + (1.0 - ADAM_B2) * (gv * gv)
        m_hat = m_new / (1.0 - ADAM_B1 ** ADAM_STEP)
        v_hat = v_new / (1.0 - ADAM_B2 ** ADAM_STEP)
        delta = -ADAM_LR * (m_hat / (jnp.sqrt(v_hat) + ADAM_EPS) + ADAM_WD * wv)
        return delta, m_new, v_new

    d, mn, vn = ew(fn, [two_d(w), two_d(g), two_d(m), two_d(v)], [F32, F32, F32], name)
    return d.reshape(shape), mn.reshape(shape), vn.reshape(shape)


def rope_tables(s_len):
    def angles(pos, dim):
        freqs = ROPE_THETA ** (-jnp.arange(0, dim, 2, dtype=F32) / dim)
        ang = pos.astype(F32)[:, None] * freqs[None, :]
        return jnp.cos(ang), jnp.sin(ang)

    pos = jnp.arange(s_len)
    rows = s_len // GRID_W
    row_idx = jnp.repeat(jnp.arange(rows), GRID_W)
    col_idx = jnp.tile(jnp.arange(GRID_W), rows)
    c1, s1 = angles(pos, HEAD_DIM)
    cr, sr = angles(row_idx, HEAD_DIM // 2)
    cc, sc = angles(col_idx, HEAD_DIM // 2)
    cos1 = jnp.tile(jnp.concatenate([c1, c1], -1), (1, 2))
    sin1 = jnp.tile(jnp.concatenate([-s1, s1], -1), (1, 2))
    cos2 = jnp.tile(jnp.concatenate([cr, cr, cc, cc], -1), (1, 2))
    sin2 = jnp.tile(jnp.concatenate([-sr, sr, -sc, sc], -1), (1, 2))
    return cos1, sin1, cos2, sin2


def _lane_iota(rows):
    return lax.broadcasted_iota(jnp.int32, (rows, LANES), 1)


def _swap(x, dist, lane):
    return jnp.where((lane & dist) != 0, pltpu.roll(x, dist, 1), pltpu.roll(x, LANES - dist, 1))


def _head_ones():
    r = lax.broadcasted_iota(jnp.int32, (LANES, LANES), 0) // HEAD_DIM
    c = lax.broadcasted_iota(jnp.int32, (LANES, LANES), 1) // HEAD_DIM
    return (r == c).astype(BF16)


def _head_sum(t, ones):
    hi = t.astype(BF16)
    lo = (t - hi.astype(F32)).astype(BF16)
    return (jnp.dot(hi, ones, preferred_element_type=F32) + jnp.dot(lo, ones, preferred_element_type=F32))


Q_SCALE = HEAD_DIM ** -0.5
LOG2E = 1.4426950408889634
LN2 = 0.6931471805599453
CHUNK_KIND = ["qa"] * 4 + ["ka", "va"] + ["qb"] * 4 + ["kb", "vb"]
QA_COL, KA_COL, QB_COL, KB_COL = 0, 512, 768, 1280


def prep_fwd(proj, tabs, qn_g, kn_g, name):
    s_len, width = proj.shape
    ts = _row_tile(s_len, 512)
    cos1, sin1, cos2, sin2 = tabs

    def body(p_ref, c1_ref, s1_ref, c2_ref, s2_ref, qg_ref, kg_ref, o_ref, kv_ref):
        lane = _lane_iota(ts)
        ones = _head_ones()
        c1, s1, c2, s2 = c1_ref[...], s1_ref[...], c2_ref[...], s2_ref[...]
        n_kv = 0
        for cb, kind in enumerate(CHUNK_KIND):
            x = p_ref[:, cb * LANES:(cb + 1) * LANES]
            if kind in ("qa", "ka"):
                y = x * c1 + _swap(x, 32, lane) * s1
            elif kind in ("qb", "kb"):
                gain = qg_ref[...] if kind == "qb" else kg_ref[...]
                ms = _head_sum(x * x, ones) * (1.0 / HEAD_DIM)
                xn = (x * lax.rsqrt(ms + EPS)) * gain
                y = xn * c2 + _swap(xn, 16, lane) * s2
            else:
                y = x
            if kind in ("qa", "qb"):
                y = y * (Q_SCALE * LOG2E)
            else:
                kv_ref[:, n_kv * LANES:(n_kv + 1) * LANES] = y.astype(BF16)
                n_kv += 1
            o_ref[cb * LANES:(cb + 1) * LANES, :] = y.T.astype(BF16)

    tab = pl.BlockSpec((ts, LANES), lambda i: (i, 0))
    vec = pl.BlockSpec((1, LANES), lambda i: (0, 0))
    return pl.pallas_call(
        body, name=name, grid=(s_len // ts,),
        in_specs=[pl.BlockSpec((ts, width), lambda i: (i, 0)), tab, tab, tab, tab, vec, vec],
        out_specs=[pl.BlockSpec((width, ts), lambda i: (0, i)), pl.BlockSpec((ts, 4 * LANES), lambda i: (i, 0))],
        out_shape=[_sds((width, s_len), BF16), _sds((s_len, 4 * LANES), BF16)],
        compiler_params=_params("arbitrary"),
    )(proj, cos1, sin1, cos2, sin2, qn_g, kn_g)


def prep_bwd(proj, dqa, dka, dva, dqb, dkb, dvb, tabs, qn_g, kn_g, name):
    s_len, width = proj.shape
    ts = _row_tile(s_len, 256)
    cos1, sin1, cos2, sin2 = tabs

    def body(p_ref, dqa_ref, dka_ref, dva_ref, dqb_ref, dkb_ref, dvb_ref,
             c1_ref, s1_ref, c2_ref, s2_ref, qg_ref, kg_ref, o_ref, dqg_ref, dkg_ref):
        i = pl.program_id(0)
        lane = _lane_iota(ts)
        c1, s1, c2, s2 = c1_ref[...], s1_ref[...], c2_ref[...], s2_ref[...]

        def rope_t(dy, cos, sin, dist):
            return dy * cos + _swap(dy * sin, dist, lane)

        ones = _head_ones()

        def norm_bwd(dy, x, gain):
            r = lax.rsqrt(_head_sum(x * x, ones) * (1.0 / HEAD_DIM) + EPS)
            xhat = x * r
            dgain = jnp.sum(dy * xhat, axis=0, keepdims=True)
            dxh = dy * gain
            dx = r * (dxh - xhat * (_head_sum(dxh * xhat, ones) * (1.0 / HEAD_DIM)))
            return dx, dgain

        dqg = jnp.zeros((1, LANES), F32)
        dkg = jnp.zeros((1, LANES), F32)
        for cb, kind in enumerate(CHUNK_KIND):
            cols = slice(cb * LANES, (cb + 1) * LANES)
            if kind == "qa":
                dx = rope_t(dqa_ref[cols, :].T * Q_SCALE, c1, s1, 32)
            elif kind == "ka":
                dx = rope_t(dka_ref[...], c1, s1, 32)
            elif kind == "va":
                dx = dva_ref[...]
            elif kind == "qb":
                qcols = slice((cb - 6) * LANES, (cb - 5) * LANES)
                dy = rope_t(dqb_ref[qcols, :].T * Q_SCALE, c2, s2, 16)
                dx, dgain = norm_bwd(dy, p_ref[:, cols], qg_ref[...])
                dqg = dqg + dgain
            elif kind == "kb":
                dy = rope_t(dkb_ref[...], c2, s2, 16)
                dx, dgain = norm_bwd(dy, p_ref[:, cols], kg_ref[...])
                dkg = dkg + dgain
            else:
                dx = dvb_ref[...]
            o_ref[:, cols] = dx.astype(BF16)

        @pl.when(i == 0)
        def _():
            dqg_ref[...] = dqg
            dkg_ref[...] = dkg

        @pl.when(i > 0)
        def _():
            dqg_ref[...] += dqg
            dkg_ref[...] += dkg

    tab = pl.BlockSpec((ts, LANES), lambda i: (i, 0))
    vec = pl.BlockSpec((1, LANES), lambda i: (0, 0))

    def dq_spec(dq):
        per = dq.shape[2] // ts
        return pl.BlockSpec((None, 4 * LANES, ts), lambda i: (i // per, 0, i % per))

    return pl.pallas_call(
        body, name=name, grid=(s_len // ts,),
        in_specs=([pl.BlockSpec((ts, width), lambda i: (i, 0)), dq_spec(dqa), tab, tab, dq_spec(dqb), tab, tab]
                  + [tab] * 4 + [vec, vec]),
        out_specs=[pl.BlockSpec((ts, width), lambda i: (i, 0)), vec, vec],
        out_shape=[_sds((s_len, width), BF16), _sds((1, LANES), F32), _sds((1, LANES), F32)],
        compiler_params=_params("arbitrary"),
    )(proj, dqa, dka, dva, dqb, dkb, dvb, cos1, sin1, cos2, sin2, qn_g, kn_g)


NEG = -1e30
GROUP = 4
KV_HEADS = 2
GROUP_W = GROUP * HEAD_DIM
LSE_ROWS = 8


def _pos_mask_t(k_start, q_start, s_len, tk, tq):
    kpos = k_start + lax.broadcasted_iota(jnp.int32, (tk, tq), 0)
    qpos = q_start + lax.broadcasted_iota(jnp.int32, (tk, tq), 1)
    return (jnp.abs(kpos - qpos) <= BLOCK) & (kpos >= 0) & (kpos < s_len)


def flash_fwd_t(qkv_t, kv_tok, q_rb, k_i, v_rb, sink, window, name, comm=None):
    s_len = qkv_t.shape[1]
    if window:
        tq = _row_tile(s_len, 512)
        tk = 2 * BLOCK
        per = tq // tk
        assert per == 2, "the band parts below are written for query blocks of two key blocks"
        n_kv = per + 2
    else:
        tq = tk = _row_tile(s_len, 1024)
        n_kv = s_len // tk
    n_kb = s_len // tk
    n_i = s_len // tq
    c_ins, c_outs, c_remote = comm if comm else ([], [], [])
    n_main = 4 if window else 3

    def body(*refs):
        main, c_in_refs = refs[:n_main], refs[n_main:n_main + len(c_ins)]
        rest = refs[n_main + len(c_ins):]
        (o_ref, lse_ref), c_out_refs = rest[:2], rest[2:2 + len(c_outs)]
        m_sc, l_sc, acc_sc = rest[2 + len(c_outs):5 + len(c_outs)]
        c_sems = rest[5 + len(c_outs):]
        if window:
            sink_ref, q_ref, k_ref, v_ref = main
        else:
            q_ref, k_ref, v_ref = main
        h, i, t = pl.program_id(0), pl.program_id(1), pl.program_id(2)
        if comm:
            @pl.when((h == 0) & (i == 0) & (t == 0))
            def _():
                _exchange_start(c_remote, c_in_refs, c_out_refs, *c_sems)

        @pl.when(t == 0)
        def _():
            for g in range(GROUP):
                if window:
                    m_sc[g] = jnp.full((1, tq), sink_ref[h * GROUP + g] * LOG2E, F32)
                    l_sc[g] = jnp.ones((1, tq), F32)
                else:
                    m_sc[g] = jnp.full((1, tq), NEG, F32)
                    l_sc[g] = jnp.zeros((1, tq), F32)
                acc_sc[g] = jnp.zeros((HEAD_DIM, tq), F32)

        def tile(k_lo, k_hi, q_lo, q_hi):
            ks, qs = slice(k_lo, k_hi), slice(q_lo, q_hi)
            k = k_ref[ks, :]
            v_t = v_ref[:, ks]
            if window:
                mask = _pos_mask_t((i * per - 1 + t) * tk + k_lo, i * tq + q_lo, s_len, k_hi - k_lo, q_hi - q_lo)
            s_next = jnp.dot(k, q_ref[0:HEAD_DIM, qs], preferred_element_type=F32)
            for g in range(GROUP):
                s_t = s_next
                if g + 1 < GROUP:
                    s_next = jnp.dot(k, q_ref[(g + 1) * HEAD_DIM:(g + 2) * HEAD_DIM, qs],
                                     preferred_element_type=F32)
                if window:
                    s_t = jnp.where(mask, s_t, NEG)
                m_prev = m_sc[g, :, qs]
                m_new = jnp.maximum(m_prev, jnp.max(s_t, axis=0, keepdims=True))
                alpha = jnp.exp2(m_prev - m_new)
                p_t = jnp.exp2(s_t - m_new)
                l_sc[g, :, qs] = alpha * l_sc[g, :, qs] + jnp.sum(p_t, axis=0, keepdims=True)
                acc_sc[g, :, qs] = alpha * acc_sc[g, :, qs] + jnp.dot(v_t, p_t.astype(BF16),
                                                                     preferred_element_type=F32)
                m_sc[g, :, qs] = m_new

        if window:
            for tt, part in enumerate([(tk - BLOCK, tk, 0, BLOCK), (0, tk, 0, tq - BLOCK),
                                       (0, tk, BLOCK, tq), (0, BLOCK, tq - BLOCK, tq)]):
                pl.when(t == tt)(functools.partial(tile, *part))
        else:
            tile(0, tk, 0, tq)

        @pl.when(t == n_kv - 1)
        def _():
            for g in range(GROUP):
                l = l_sc[g]
                o_ref[g * HEAD_DIM:(g + 1) * HEAD_DIM, :] = (acc_sc[g] / l).astype(BF16)
                lse_ref[g * LSE_ROWS:(g + 1) * LSE_ROWS, :] = jnp.broadcast_to(
                    m_sc[g] + jnp.log(l) * LOG2E, (LSE_ROWS, tq))

        if comm:
            @pl.when((h == KV_HEADS - 1) & (i == n_i - 1) & (t == n_kv - 1))
            def _():
                _exchange_finish(c_remote, c_in_refs, c_out_refs, *c_sems)

    if window:
        kv_blk = lambda i, t: jnp.clip(i * per - 1 + t, 0, n_kb - 1)
    else:
        kv_blk = lambda i, t: t
    hbm = pl.BlockSpec(memory_space=pl.ANY)
    in_specs = [pl.BlockSpec((GROUP_W, tq), lambda h, i, t: (q_rb + h, i)),
                pl.BlockSpec((None, tk, HEAD_DIM), lambda h, i, t: (k_i + h, kv_blk(i, t), 0)),
                pl.BlockSpec((HEAD_DIM, tk), lambda h, i, t: (v_rb + h, kv_blk(i, t)))]
    args = [qkv_t, kv_tok, qkv_t]
    if window:
        in_specs = [pl.BlockSpec(memory_space=pltpu.SMEM)] + in_specs
        args = [sink] + args
    return pl.pallas_call(
        body, name=name, grid=(KV_HEADS, n_i, n_kv),
        in_specs=in_specs + [hbm] * len(c_ins),
        out_specs=[pl.BlockSpec((GROUP_W, tq), lambda h, i, t: (h, i)),
                   pl.BlockSpec((GROUP * LSE_ROWS, tq), lambda h, i, t: (h, i))] + [hbm] * len(c_outs),
        out_shape=[_sds((KV_HEADS * GROUP_W, s_len), BF16),
                   _sds((KV_HEADS * GROUP * LSE_ROWS, s_len), F32)] + list(c_outs),
        scratch_shapes=[pltpu.VMEM((GROUP, 1, tq), F32), pltpu.VMEM((GROUP, 1, tq), F32),
                        pltpu.VMEM((GROUP, HEAD_DIM, tq), F32)] + _exchange_sems(c_remote),
        compiler_params=_params("arbitrary", "arbitrary", "arbitrary"),
    )(*args, *c_ins)


def flash_bwd_t(qkv_t, kv_tok, o_t, do_t, lse, q_rb, k_i, v_i, k_rb, do_rb, sink, window, name, comm=None):
    s_len = qkv_t.shape[1]
    if window:
        tq = _row_tile(s_len, 512)
        tk = 2 * BLOCK
        assert tq == 2 * tk, "the band parts below are written for query blocks of two key blocks"
        n_q = 2
    else:
        tq = tk = _row_tile(s_len, 1024)
        n_q = s_len // tq
    n_qb = s_len // tq
    n_j = s_len // tk
    c_ins, c_outs, c_remote = comm if comm else ([], [], [])
    n_main = 8 if window else 7
    n_out = 4 if window else 3

    def body(*refs):
        main, c_in_refs = refs[:n_main], refs[n_main:n_main + len(c_ins)]
        rest = refs[n_main + len(c_ins):]
        outs, c_out_refs = rest[:n_out], rest[n_out:n_out + len(c_outs)]
        dk_sc, dv_sc = rest[n_out + len(c_outs):n_out + len(c_outs) + 2]
        c_sems = rest[n_out + len(c_outs) + 2:]
        if window:
            sink_ref, q_ref, k_ref, v_ref, kt_ref, o_ref, do_ref, lse_ref = main
            dq_ref, dk_ref, dv_ref, dsink_ref = outs
        else:
            q_ref, k_ref, v_ref, kt_ref, o_ref, do_ref, lse_ref = main
            dq_ref, dk_ref, dv_ref = outs
        h, j, t = pl.program_id(0), pl.program_id(1), pl.program_id(2)
        q_blk = (j + 1) // 2 - 1 + t if window else t
        if comm:
            @pl.when((h == 0) & (j == 0) & (t == 0))
            def _():
                _exchange_start(c_remote, c_in_refs, c_out_refs, *c_sems)

        @pl.when((j == 0) & (t == 0))
        def _():
            dq_ref[...] = jnp.zeros(dq_ref.shape, F32)
            if window:
                dsink_ref[...] = jnp.zeros((8, LANES), F32)

        @pl.when(t == 0)
        def _():
            dk_sc[...] = jnp.zeros((tk, HEAD_DIM), F32)
            dv_sc[...] = jnp.zeros((tk, HEAD_DIM), F32)

        def tile(k_lo, k_hi, q_lo, q_hi, sink_lo=0, sink_hi=0):
            ks, qs = slice(k_lo, k_hi), slice(q_lo, q_hi)
            k, v, k_t = k_ref[ks, :], v_ref[ks, :], kt_ref[:, ks]
            if window:
                mask = _pos_mask_t(j * tk + k_lo, q_blk * tq + q_lo, s_len, k_hi - k_lo, q_hi - q_lo)
                lane = lax.broadcasted_iota(jnp.int32, (8, LANES), 1)
                sink_tile = jnp.zeros((8, LANES), F32)
            dk_acc = dk_sc[ks, :]
            dv_acc = dv_sc[ks, :]
            for g in range(GROUP):
                rows = slice(g * HEAD_DIM, (g + 1) * HEAD_DIM)
                q_t, o_g, do_g = q_ref[rows, qs], o_ref[rows, qs], do_ref[rows, qs]
                s_t = jnp.dot(k, q_t, preferred_element_type=F32)
                if window:
                    s_t = jnp.where(mask, s_t, NEG)
                lse_row = lse_ref[g * LSE_ROWS:g * LSE_ROWS + 1, qs]
                p_t = jnp.exp2(s_t - lse_row)
                delta = jnp.sum(do_g.astype(F32) * o_g.astype(F32), axis=0, keepdims=True)
                dp_t = jnp.dot(v, do_g, preferred_element_type=F32)
                ds_t = (p_t * (dp_t - delta)).astype(BF16)
                dv_acc = dv_acc + lax.dot_general(p_t.astype(BF16), do_g, NT_DIMS, preferred_element_type=F32)
                dk_acc = dk_acc + lax.dot_general(ds_t, q_t, NT_DIMS, preferred_element_type=F32)
                dq_ref[q_blk, rows, qs] += jnp.dot(k_t, ds_t, preferred_element_type=F32)
                if sink_hi > sink_lo:
                    at = slice(sink_lo - q_lo, sink_hi - q_lo)
                    p_sink = jnp.exp2(sink_ref[h * GROUP + g] * LOG2E - lse_row[:, at])
                    term = -jnp.sum(p_sink * delta[:, at], axis=1, keepdims=True)
                    sink_tile = jnp.where(lane == g, term, sink_tile)
            dk_sc[ks, :] = dk_acc
            dv_sc[ks, :] = dv_acc
            if sink_hi > sink_lo:
                dsink_ref[...] += sink_tile

        if window:
            parts = {(0, 0): (0, BLOCK, tq - BLOCK, tq), (0, 1): (0, tk, 0, tq - BLOCK, 0, tq - BLOCK),
                     (1, 0): (0, tk, BLOCK, tq, tq - BLOCK, tq), (1, 1): (tk - BLOCK, tk, 0, BLOCK)}
            for (parity, tt), part in parts.items():
                pl.when((q_blk >= 0) & (q_blk < n_qb) & (j % 2 == parity) & (t == tt))(
                    functools.partial(tile, *part))
        else:
            tile(0, tk, 0, tq)

        @pl.when(t == n_q - 1)
        def _():
            dk_ref[...] = dk_sc[...] * LN2
            dv_ref[...] = dv_sc[...]

        if comm:
            @pl.when((h == KV_HEADS - 1) & (j == n_j - 1) & (t == n_q - 1))
            def _():
                _exchange_finish(c_remote, c_in_refs, c_out_refs, *c_sems)

    if window:
        qb = lambda j, t: jnp.clip((j + 1) // 2 - 1 + t, 0, n_qb - 1)
    else:
        qb = lambda j, t: t
    hbm = pl.BlockSpec(memory_space=pl.ANY)
    in_specs = [pl.BlockSpec((GROUP_W, tq), lambda h, j, t: (q_rb + h, qb(j, t))),
                pl.BlockSpec((None, tk, HEAD_DIM), lambda h, j, t: (k_i + h, j, 0)),
                pl.BlockSpec((None, tk, HEAD_DIM), lambda h, j, t: (v_i + h, j, 0)),
                pl.BlockSpec((HEAD_DIM, tk), lambda h, j, t: (k_rb + h, j)),
                pl.BlockSpec((GROUP_W, tq), lambda h, j, t: (h, qb(j, t))),
                pl.BlockSpec((GROUP_W, tq), lambda h, j, t: (do_rb + h, qb(j, t))),
                pl.BlockSpec((GROUP * LSE_ROWS, tq), lambda h, j, t: (h, qb(j, t)))]
    args = [qkv_t, kv_tok, kv_tok, qkv_t, o_t, do_t, lse]
    kv_out = _sds((KV_HEADS, s_len, HEAD_DIM), F32)
    out_specs = [pl.BlockSpec((n_qb, GROUP_W, tq), lambda h, j, t: (0, h, 0)),
                 pl.BlockSpec((None, tk, HEAD_DIM), lambda h, j, t: (h, j, 0)),
                 pl.BlockSpec((None, tk, HEAD_DIM), lambda h, j, t: (h, j, 0))]
    out_shape = [_sds((n_qb, KV_HEADS * GROUP_W, tq), F32), kv_out, kv_out]
    if window:
        in_specs = [pl.BlockSpec(memory_space=pltpu.SMEM)] + in_specs
        args = [sink] + args
        out_specs.append(pl.BlockSpec((None, 8, LANES), lambda h, j, t: (h, 0, 0)))
        out_shape.append(_sds((KV_HEADS, 8, LANES), F32))
    return pl.pallas_call(
        body, name=name, grid=(KV_HEADS, n_j, n_q),
        in_specs=in_specs + [hbm] * len(c_ins), out_specs=out_specs + [hbm] * len(c_outs),
        out_shape=out_shape + list(c_outs),
        scratch_shapes=[pltpu.VMEM((tk, HEAD_DIM), F32), pltpu.VMEM((tk, HEAD_DIM), F32)]
        + _exchange_sems(c_remote),
        compiler_params=_params("arbitrary", "arbitrary", "arbitrary"),
    )(*args, *c_ins)


SGU_GROUPS = 8
SGU_CHUNK = 128
GELU_C = float(np.sqrt(2.0 / np.pi))
GELU_A = 0.044715


def _gelu_and_grad(x):
    x2 = x * x
    t = jnp.tanh(x * (GELU_C + (GELU_C * GELU_A) * x2))
    hx = 0.5 * x
    return hx + hx * t, (0.5 + 0.5 * t) + (hx * (1.0 - t * t)) * (GELU_C + (3.0 * GELU_C * GELU_A) * x2)


def _gelu(x):
    t = jnp.tanh(x * (GELU_C + (GELU_C * GELU_A) * (x * x)))
    hx = 0.5 * x
    return hx + hx * t


def _layernorm_stats(v):
    mu = jnp.mean(v, axis=-1, keepdims=True)
    var = jnp.mean(jnp.square(v - mu), axis=-1, keepdims=True)
    rstd = lax.rsqrt(var + EPS)
    return (v - mu) * rstd, rstd


def sgu_mid_fwd(zpre, ln_g, ln_b, ws, bsb, name):
    s_len, width = zpre.shape
    d = width // 2
    ts = _row_tile(s_len, 256)

    def body(z_ref, g_ref, b_ref, ws_ref, bs_ref, y_ref):
        z = _gelu(z_ref[...])
        u, v = z[:, :d], z[:, d:]
        vhat, _ = _layernorm_stats(v)
        vn = (vhat * g_ref[...] + b_ref[...]).astype(BF16)
        for n in range(ts // SGU_CHUNK):
            rows = slice(n * SGU_CHUNK, (n + 1) * SGU_CHUNK)
            for g in range(SGU_GROUPS):
                cols = slice(g * LANES, (g + 1) * LANES)
                mixed = jnp.dot(ws_ref[g], vn[rows, cols], preferred_element_type=F32) + bs_ref[g]
                y_ref[rows, cols] = (u[rows, cols] * mixed).astype(BF16)

    vec = pl.BlockSpec((1, d), lambda i: (0, 0))
    cube = pl.BlockSpec((SGU_GROUPS, SGU_CHUNK, SGU_CHUNK), lambda i: (0, 0, 0))
    return pl.pallas_call(
        body, name=name, grid=(s_len // ts,),
        in_specs=[pl.BlockSpec((ts, width), lambda i: (i, 0)), vec, vec, cube, cube],
        out_specs=pl.BlockSpec((ts, d), lambda i: (i, 0)),
        out_shape=_sds((s_len, d), BF16),
        compiler_params=_params("arbitrary"),
    )(zpre, ln_g, ln_b, ws, bsb)


def sgu_mid_bwd(zpre, dy, ln_g, ln_b, ws, wst, bsb, name):
    s_len, width = zpre.shape
    d = width // 2
    ts = _row_tile(s_len, 256)
    n_steps = s_len // ts

    def body(z_ref, dy_ref, g_ref, b_ref, ws_ref, wst_ref, bs_ref,
             dz_ref, dws_ref, dbs_ref, dg_ref, db_ref, du_sc, dvn_sc):
        i = pl.program_id(0)

        @pl.when(i == 0)
        def _():
            dws_ref[...] = jnp.zeros(dws_ref.shape, F32)
            dbs_ref[...] = jnp.zeros(dbs_ref.shape, F32)
            dg_ref[...] = jnp.zeros(dg_ref.shape, F32)
            db_ref[...] = jnp.zeros(db_ref.shape, F32)

        zp = z_ref[...]
        z, gp = _gelu_and_grad(zp)
        u, v = z[:, :d], z[:, d:]
        vhat, rstd = _layernorm_stats(v)
        gain = g_ref[...]
        vn = (vhat * gain + b_ref[...]).astype(BF16)
        dyf = dy_ref[...].astype(F32)
        for n in range(ts // SGU_CHUNK):
            rows = slice(n * SGU_CHUNK, (n + 1) * SGU_CHUNK)
            for g in range(SGU_GROUPS):
                cols = slice(g * LANES, (g + 1) * LANES)
                vt = vn[rows, cols]
                mixed = jnp.dot(ws_ref[g], vt, preferred_element_type=F32) + bs_ref[g]
                dyt = dyf[rows, cols]
                du_sc[rows, cols] = dyt * mixed
                dmixed = dyt * u[rows, cols]
                dmb = dmixed.astype(BF16)
                dvn_sc[rows, cols] = jnp.dot(wst_ref[g], dmb, preferred_element_type=F32)
                dws_ref[g] += lax.dot_general(dmb, vt, NT_DIMS, preferred_element_type=F32)
                dbs_ref[g] += dmixed
        dvn = dvn_sc[...]
        dg_ref[...] += jnp.sum(dvn * vhat, axis=0, keepdims=True)
        db_ref[...] += jnp.sum(dvn, axis=0, keepdims=True)
        dvh = dvn * gain
        dv = rstd * (dvh - jnp.mean(dvh, axis=-1, keepdims=True)
                     - vhat * jnp.mean(dvh * vhat, axis=-1, keepdims=True))
        dz_ref[:, :d] = (du_sc[...] * gp[:, :d]).astype(BF16)
        dz_ref[:, d:] = (dv * gp[:, d:]).astype(BF16)

        @pl.when(i == n_steps - 1)
        def _():
            for g in range(SGU_GROUPS):
                tot = jnp.sum(dbs_ref[g], axis=1, keepdims=True)
                dbs_ref[g] = jnp.broadcast_to(tot, (SGU_CHUNK, LANES))

    vec = pl.BlockSpec((1, d), lambda i: (0, 0))
    cube = pl.BlockSpec((SGU_GROUPS, SGU_CHUNK, SGU_CHUNK), lambda i: (0, 0, 0))
    cube_shape = _sds((SGU_GROUPS, SGU_CHUNK, SGU_CHUNK), F32)
    return pl.pallas_call(
        body, name=name, grid=(n_steps,),
        in_specs=[pl.BlockSpec((ts, width), lambda i: (i, 0)), pl.BlockSpec((ts, d), lambda i: (i, 0)),
                  vec, vec, cube, cube, cube],
        out_specs=[pl.BlockSpec((ts, width), lambda i: (i, 0)), cube, cube, vec, vec],
        out_shape=[_sds((s_len, width), BF16), cube_shape, cube_shape, _sds((1, d), F32), _sds((1, d), F32)],
        scratch_shapes=[pltpu.VMEM((ts, d), F32), pltpu.VMEM((ts, d), F32)],
        compiler_params=_params("arbitrary"),
    )(zpre, dy, ln_g, ln_b, ws, wst, bsb)


def loss_head(x, g, target, name):
    s_len, d = x.shape
    tm = _row_tile(s_len, 512)

    def body(x_ref, g_ref, t_ref, dx_ref, dxb_ref, dg_ref, loss_ref):
        i = pl.program_id(0)
        xf = x_ref[...]
        gain = g_ref[...]
        r = lax.rsqrt(jnp.mean(xf * xf, axis=-1, keepdims=True) + EPS)
        xhat = xf * r
        err = xhat * gain - t_ref[...]
        row = jnp.mean(err * err, axis=-1, keepdims=True)
        part = 0.5 * jnp.sum(row, axis=0, keepdims=True)
        dy = err * (1.0 / d)
        dg_part = jnp.sum(dy * xhat, axis=0, keepdims=True)

        @pl.when(i == 0)
        def _():
            dg_ref[...] = dg_part
            loss_ref[...] = jnp.broadcast_to(part, (8, LANES))

        @pl.when(i > 0)
        def _():
            dg_ref[...] += dg_part
            loss_ref[...] += jnp.broadcast_to(part, (8, LANES))

        dxh = dy * gain
        dx = r * (dxh - xhat * jnp.mean(dxh * xhat, axis=-1, keepdims=True))
        dx_ref[...] = dx
        dxb_ref[...] = dx.astype(BF16)

    row_spec = pl.BlockSpec((tm, d), lambda i: (i, 0))
    vec = pl.BlockSpec((1, d), lambda i: (0, 0))
    return pl.pallas_call(
        body, name=name, grid=(s_len // tm,),
        in_specs=[row_spec, vec, row_spec],
        out_specs=[row_spec, row_spec, vec, pl.BlockSpec((8, LANES), lambda i: (0, 0))],
        out_shape=[_sds((s_len, d), F32), _sds((s_len, d), BF16), _sds((1, d), F32), _sds((8, LANES), F32)],
        compiler_params=_params("arbitrary"),
    )(x, g, target)


FLIP_BITS = {"c": (0, 0, 1), "x": (1, 0, 0), "y": (0, 1, 0), "xy": (1, 1, 0),
             "xc": (1, 0, 1), "yc": (0, 1, 1), "xyc": (1, 1, 1)}
CHIP_FLIPS = ("x", "y", "xy")


def _flip(pos, name):
    return tuple(1 - p if bit else p for p, bit in zip(pos, FLIP_BITS[name]))


def _chip(pos):
    return 2 * pos[0] + pos[1]


def _me():
    return (lax.axis_index("x"), lax.axis_index("y"), lax.axis_index("c"))


def _exchange_copy(remote, k, in_refs, out_refs, send_sems, recv_sems, sender, receiver):
    ii, src_fn, oi, dst_fn, _ = remote[k]
    return pltpu.make_async_remote_copy(
        src_ref=src_fn(in_refs[ii], sender, receiver), dst_ref=dst_fn(out_refs[oi], sender),
        send_sem=send_sems.at[k], recv_sem=recv_sems.at[k], device_id=receiver, device_id_type=MESH)


def _exchange_start(remote, in_refs, out_refs, send_sems, recv_sems):
    me = _me()
    for k in range(len(remote)):
        _exchange_copy(remote, k, in_refs, out_refs, send_sems, recv_sems, me, _flip(me, remote[k][4])).start()


def _exchange_finish(remote, in_refs, out_refs, send_sems, recv_sems):
    me = _me()
    for k in range(len(remote)):
        _exchange_copy(remote, k, in_refs, out_refs, send_sems, recv_sems, _flip(me, remote[k][4]), me).wait_recv()
    for k in range(len(remote)):
        _exchange_copy(remote, k, in_refs, out_refs, send_sems, recv_sems, me, _flip(me, remote[k][4])).wait_send()


def _exchange_sems(remote):
    n = len(remote)
    return [pltpu.SemaphoreType.DMA((n,)), pltpu.SemaphoreType.DMA((n,))] if n else []


def exchange(name, ins, out_shapes, remote, local):
    n_in, n_out = len(ins), len(out_shapes)

    def body(*refs):
        in_refs, out_refs = refs[:n_in], refs[n_in:n_in + n_out]
        send_sems, recv_sems, local_sems = refs[n_in + n_out:]
        me = _me()
        stays = []
        for k, (ii, src_fn, oi, dst_fn) in enumerate(local):
            cp = pltpu.make_async_copy(src_fn(in_refs[ii], me), dst_fn(out_refs[oi], me), local_sems.at[k])
            cp.start()
            stays.append(cp)
        _exchange_start(remote, in_refs, out_refs, send_sems, recv_sems)
        _exchange_finish(remote, in_refs, out_refs, send_sems, recv_sems)
        for cp in stays:
            cp.wait()

    hbm = pl.BlockSpec(memory_space=pl.ANY)
    return pl.pallas_call(
        body, name=name,
        in_specs=[hbm] * n_in, out_specs=[hbm] * n_out, out_shape=list(out_shapes),
        scratch_shapes=[pltpu.SemaphoreType.DMA((max(len(remote), 1),)),
                        pltpu.SemaphoreType.DMA((max(len(remote), 1),)),
                        pltpu.SemaphoreType.DMA((max(len(local), 1),))],
        compiler_params=pltpu.CompilerParams(has_side_effects=True),
    )(*ins)


def staged_push(name, ins, out_shapes, jobs, n_alias=0):
    n_in, n_out = len(ins), len(out_shapes)
    n_copies = sum(len(dsts) for _, _, dsts in jobs)
    n_remote = sum(1 for _, _, dsts in jobs for d in dsts if d[2] is not None)

    def chunk_of(ii, src_fn):
        probe = _ShapeRef(ins[ii].shape, ins[ii].dtype)
        got = src_fn(probe, (0, 0, 0))
        return tuple(got.shape), got.dtype

    classes = []
    for ii, src_fn, _ in jobs:
        c = chunk_of(ii, src_fn)
        if c not in classes:
            classes.append(c)

    def body(*refs):
        in_refs, out_refs = refs[:n_in], refs[n_in:n_in + n_out]
        bufs = refs[n_in + n_out:n_in + n_out + len(classes)]
        load_sems, out_sems, recv_sems = refs[n_in + n_out + len(classes):]
        me = _me()
        pending = [[[], []] for _ in classes]
        used = [0] * len(classes)
        arrivals = []
        k = r = 0

        def begin_load(job):
            ii, src_fn, _ = job
            cls = classes.index(chunk_of(ii, src_fn))
            slot = used[cls] % 2
            used[cls] += 1
            for kind, cp in pending[cls][slot]:
                cp.wait_send() if kind == "remote" else cp.wait()
            pending[cls][slot] = []
            load = pltpu.make_async_copy(src_fn(in_refs[ii], me), bufs[cls].at[slot], load_sems.at[2 * cls + slot])
            load.start()
            return load, cls, slot

        nxt = begin_load(jobs[0])
        for n, (ii, src_fn, dsts) in enumerate(jobs):
            load, cls, slot = nxt
            load.wait()
            buf = bufs[cls].at[slot]
            sent = []
            for oi, dst_fn, flip in dsts:
                if flip is None:
                    cp = pltpu.make_async_copy(buf, dst_fn(out_refs[oi], me), out_sems.at[k])
                    cp.start()
                    sent.append(("local", cp))
                else:
                    peer = _flip(me, flip)
                    cp = pltpu.make_async_remote_copy(
                        src_ref=buf, dst_ref=dst_fn(out_refs[oi], me), send_sem=out_sems.at[k],
                        recv_sem=recv_sems.at[r], device_id=peer, device_id_type=MESH)
                    cp.start()
                    sent.append(("remote", cp))
                    arrivals.append((r, cls, oi, dst_fn, peer))
                    r += 1
                k += 1
            pending[cls][slot] = sent
            if n + 1 < len(jobs):
                nxt = begin_load(jobs[n + 1])
        for per_class in pending:
            for slot_list in per_class:
                for kind, cp in slot_list:
                    cp.wait_send() if kind == "remote" else cp.wait()
        for r, cls, oi, dst_fn, peer in arrivals:
            pltpu.make_async_remote_copy(
                src_ref=bufs[cls].at[0], dst_ref=dst_fn(out_refs[oi], peer), send_sem=out_sems.at[0],
                recv_sem=recv_sems.at[r], device_id=peer, device_id_type=MESH).wait_recv()

    hbm = pl.BlockSpec(memory_space=pl.ANY)
    return pl.pallas_call(
        body, name=name,
        in_specs=[hbm] * n_in, out_specs=[hbm] * n_out, out_shape=list(out_shapes),
        scratch_shapes=[pltpu.VMEM((2,) + shape, dtype) for shape, dtype in classes]
        + [pltpu.SemaphoreType.DMA((2 * len(classes),)), pltpu.SemaphoreType.DMA((max(n_copies, 1),)),
           pltpu.SemaphoreType.DMA((max(n_remote, 1),))],
        input_output_aliases={i: i for i in range(n_alias)},
        compiler_params=pltpu.CompilerParams(has_side_effects=True, vmem_limit_bytes=VMEM_LIMIT),
    )(*ins)


class _ShapeRef:
    def __init__(self, shape, dtype):
        self.shape, self.dtype = tuple(shape), dtype

    @property
    def at(self):
        return self

    def __getitem__(self, idx):
        idx = idx if isinstance(idx, tuple) else (idx,)
        shape = []
        for dim, i in zip(self.shape, idx):
            if isinstance(i, slice):
                shape.append(len(range(*i.indices(dim))))
            elif hasattr(i, "size") and hasattr(i, "start"):
                shape.append(i.size)
        shape += self.shape[len(idx):]
        return _ShapeRef(shape, self.dtype)


def gather_whole(shards, name):
    whole = lambda ref, sender, receiver=None: ref
    slot = lambda ref, sender: ref.at[_chip(sender)]
    remote = [(t, whole, t, slot, flip) for t in range(len(shards)) for flip in CHIP_FLIPS]
    local = [(t, whole, t, slot) for t in range(len(shards))]
    outs = [_sds((N_CHIPS,) + a.shape, a.dtype) for a in shards]
    return exchange(name, list(shards), outs, remote, local)


def _half_axis(shape):
    return 0 if shape[0] >= 2 else 1


def gather_halves_plan(shards):
    remote = []
    for t, a in enumerate(shards):
        ax = _half_axis(a.shape)
        half = lambda ref, sender, receiver=None, ax=ax: _half(ref, sender[2], ax)
        slot = lambda ref, sender, ax=ax: _half(ref.at[_chip(sender)], sender[2], ax)
        remote += [(t, half, t, slot, flip) for flip in CHIP_FLIPS]
    outs = [_sds((N_CHIPS,) + a.shape, a.dtype) for a in shards]
    return list(shards), outs, remote


def gather_halves_fill(got, shards, name):
    n_t = len(shards)
    jobs = []
    for t, a in enumerate(shards):
        layers = a.shape[0]
        for l in range(layers):
            jobs.append((n_t + t, lambda ref, me, l=l: ref.at[l],
                         [(t, lambda ref, sender, l=l: ref.at[_chip(sender), l], None)]))
        for flip in CHIP_FLIPS:
            if _half_axis(a.shape) == 0:
                n = layers // 2
                for j in range(n):
                    at = lambda ref, pos, flip=flip, j=j, n=n: ref.at[_chip(_flip(pos, flip)), pos[2] * n + j]
                    jobs.append((t, at, [(t, at, "c")]))
            else:
                rows = a.shape[1] // 2
                at = lambda ref, pos, flip=flip, rows=rows: ref.at[
                    _chip(_flip(pos, flip)), 0, pl.ds(pos[2] * rows, rows)]
                jobs.append((t, at, [(t, at, "c")]))
    outs = [_sds(g.shape, g.dtype) for g in got]
    return staged_push(name, list(got) + list(shards), outs, jobs, n_alias=n_t)


def add_to_all(plan, buf):
    ins, outs, remote = plan
    whole = lambda ref, sender, receiver=None: ref
    more = [(len(ins), whole, len(outs), (lambda ref, sender, f=f: ref.at[f]), flip)
            for f, flip in enumerate(FLIPS_BY_INDEX)]
    return list(ins) + [buf], list(outs) + [_sds((len(more),) + buf.shape, buf.dtype)], list(remote) + more


FLIPS_BY_INDEX = ("c", "y", "yc", "x", "xc", "xy", "xyc")


def sum_devices(own, got, name):
    rows = own.shape[0]
    tr = LANES if rows % LANES == 0 else rows
    me = (4 * lax.axis_index("x") + 2 * lax.axis_index("y") + lax.axis_index("c")).astype(jnp.int32).reshape(1)
    everyone = jnp.concatenate([own[None], got], axis=0)

    def body(me_ref, a0, a1, a2, a3, a4, a5, a6, a7, o_ref):
        o_ref[...] = ((a0[...] + a1[...]) + (a2[...] + a3[...])) + ((a4[...] + a5[...]) + (a6[...] + a7[...]))

    return pl.pallas_call(
        body, name=name,
        grid_spec=pltpu.PrefetchScalarGridSpec(
            num_scalar_prefetch=1, grid=(rows // tr,),
            in_specs=[pl.BlockSpec((None, tr, LANES), lambda i, m, k=k: (m[0] ^ k, i, 0)) for k in range(8)],
            out_specs=pl.BlockSpec((tr, LANES), lambda i, m: (i, 0))),
        out_shape=_sds((rows, LANES), F32),
        compiler_params=_params("arbitrary"),
    )(me, *([everyone] * 8))


def _half(ref, core, axis):
    rows = ref.shape[axis] // 2
    idx = (slice(None),) * axis + (pl.ds(core * rows, rows),)
    return ref.at[idx]


def reduce_begin(grads, name):
    core = lax.axis_index("c").astype(jnp.int32).reshape(1)
    jobs, outs = [], []
    for t, g in enumerate(grads):
        outs.append(_sds((N_CHIPS, g.shape[1] // 2, g.shape[2]), BF16))
        for s in range(N_CHIPS):
            jobs.append((t, lambda ref, me, s=s: _half(ref.at[s], 1 - me[2], 0),
                         [(t, lambda ref, sender, s=s: ref.at[s], "c")]))
    theirs = staged_push(name + "_swap", grads, outs, jobs)
    chip_sums = [add_half(g, r, core, f"{name}_add{t}") for t, (g, r) in enumerate(zip(grads, theirs))]
    remote, outs = [], []
    for t, p in enumerate(chip_sums):
        outs.append(_sds((len(CHIP_FLIPS),) + p.shape[1:], BF16))
        for f, flip in enumerate(CHIP_FLIPS):
            remote.append((t, lambda ref, sender, receiver: ref.at[_chip(receiver)],
                           t, lambda ref, sender, f=f: ref.at[f], flip))
    return chip_sums, (chip_sums, outs, remote)


def reduce_finish(chip_sums, got, stacks, full_shapes, into, name):
    chip = (2 * lax.axis_index("x") + lax.axis_index("y")).astype(jnp.int32).reshape(1)
    totals = [sum_chips(p, r, chip, f"{name}_sum{t}") for t, (p, r) in enumerate(zip(chip_sums, got))]
    names = []
    for out_name, _ in stacks:
        if out_name not in names:
            names.append(out_name)
    names = [n for n in names if n in into] + [n for n in names if n not in into]
    kept = [into[n] for n in names if n in into]
    outs = [_sds(full_shapes[n], F32) for n in names]
    jobs = []
    for t, (out_name, layer) in enumerate(stacks):
        oi = names.index(out_name)
        rows, cols = totals[t].shape
        pieces = max(1, rows * cols * 4 // STAGE_BYTES)
        step = rows // pieces
        for q in range(pieces):
            src = lambda ref, me, q=q, step=step: ref.at[pl.ds(q * step, step)]
            place = lambda ref, sender, layer=layer, q=q, step=step, rows=rows: ref.at[
                layer, pl.ds(sender[2] * rows + q * step, step)]
            jobs.append((len(kept) + t, src, [(oi, place, None), (oi, place, "c")]))
    full = staged_push(name + "_share", kept + totals, outs, jobs, n_alias=len(kept))
    return {**into, **dict(zip(names, full))}


STAGE_BYTES = 1024 * 1024


def add_half(g, theirs, core, name):
    n_s, rows, cols = g.shape
    half = rows // 2
    tr = _row_tile(half, 256)
    nb = half // tr

    def body(core_ref, g_ref, t_ref, o_ref):
        o_ref[...] = (g_ref[...].astype(F32) + t_ref[...].astype(F32)).astype(BF16)

    return pl.pallas_call(
        body, name=name,
        grid_spec=pltpu.PrefetchScalarGridSpec(
            num_scalar_prefetch=1, grid=(n_s, nb),
            in_specs=[pl.BlockSpec((None, tr, cols), lambda s, i, c: (s, c[0] * nb + i, 0)),
                      pl.BlockSpec((None, tr, cols), lambda s, i, c: (s, i, 0))],
            out_specs=pl.BlockSpec((None, tr, cols), lambda s, i, c: (s, i, 0))),
        out_shape=_sds((n_s, half, cols), BF16),
        compiler_params=_params("arbitrary", "arbitrary"),
    )(core, g, theirs)


def sum_chips(mine, theirs, chip, name):
    _, half, cols = mine.shape
    tr = _row_tile(half, 256)

    def body(chip_ref, m_ref, a_ref, b_ref, c_ref, o_ref):
        o_ref[...] = ((m_ref[...].astype(F32) + a_ref[...].astype(F32))
                      + b_ref[...].astype(F32)) + c_ref[...].astype(F32)

    got = lambda f: pl.BlockSpec((None, tr, cols), lambda i, ch: (f, i, 0))
    return pl.pallas_call(
        body, name=name,
        grid_spec=pltpu.PrefetchScalarGridSpec(
            num_scalar_prefetch=1, grid=(half // tr,),
            in_specs=[pl.BlockSpec((None, tr, cols), lambda i, ch: (ch[0], i, 0)), got(0), got(1), got(2)],
            out_specs=pl.BlockSpec((tr, cols), lambda i, ch: (i, 0))),
        out_shape=_sds((half, cols), F32),
        compiler_params=_params("arbitrary"),
    )(chip, mine, theirs, theirs, theirs)


def _tok(t):
    return t.transpose(1, 0, 2).reshape(t.shape[1], t.shape[0] * t.shape[2])


def _heads(t):
    return t.reshape(t.shape[0], t.shape[1] // HEAD_DIM, HEAD_DIM).transpose(1, 0, 2)


def _tile2(vec):
    return jnp.tile(vec.reshape(1, HEAD_DIM), (1, 2))


REST = ("att_w_in", "att_w_out", "sgu_w_in", "sgu_w_out", "mlp_w1", "mlp_w2")
LAST_GROUP = (("att_w_in", 0),)
LATE_SMALL = ("att_norm", "att_sink", "att_qnorm", "att_knorm")


def local_step(x, target, first, rest_shards, rep, full_shapes):
    s_len, d = x.shape
    tabs = rope_tables(s_len)
    depth = rep["mlp_norm"].shape[0]
    row = lambda a: a.reshape(1, -1)
    saved = []
    h = x
    gw = {"att_w_in": [first]}

    def wl(name, idx):
        return (gw[name][idx], 0) if name == "att_w_in" else (gw[name], idx)

    for layer in range(depth):
        i = layer // 2
        tag = f"l{layer}"
        if layer % 2 == 0:
            hn, proj = norm_mm(h, row(rep["att_norm"][i]), *wl("att_w_in", i), F32, tag + "_att_proj")
            qkv_t, kv = prep_fwd(proj, tabs, _tile2(rep["att_qnorm"][i]), _tile2(rep["att_knorm"][i]),
                                 tag + "_att_prep")
            kv_tok = _heads(kv)
            oa, lse_a = flash_fwd_t(qkv_t, kv_tok, QA_COL // GROUP_W, 0, (KA_COL + LANES) // HEAD_DIM,
                                    rep["att_sink"][i], True, tag + "_win_fwd")
            plan = gather_halves_plan(rest_shards) if layer == 0 else None
            ob, lse_b, *got = flash_fwd_t(qkv_t, kv_tok, QB_COL // GROUP_W, 4, (KB_COL + LANES) // HEAD_DIM,
                                          None, False, tag + "_grid_fwd", comm=plan)
            if layer == 0:
                rest = dict(zip(REST, gather_halves_fill(got, rest_shards, "gather_rest_fill")))
                gw["att_w_in"].append(rest.pop("att_w_in"))
                gw.update(rest)
            out = mm_res_t([oa, ob], *wl("att_w_out", i), h, tag + "_att_out")
            mix_saved = (h, hn, proj, qkv_t, kv_tok, oa, ob, lse_a, lse_b)
        else:
            hn, zpre = norm_mm(h, row(rep["sgu_norm"][i]), *wl("sgu_w_in", i), F32, tag + "_sgu_in")
            ws = rep["sgu_w_s"][i].astype(BF16)
            bsb = jnp.broadcast_to(rep["sgu_b_s"][i][:, :, None], (SGU_GROUPS, SGU_CHUNK, LANES))
            y = sgu_mid_fwd(zpre, row(rep["sgu_ln_g"][i]), row(rep["sgu_ln_b"][i]), ws, bsb, tag + "_sgu_mid")
            out = mm_res(y, *wl("sgu_w_out", i), h, tag + "_sgu_out")
            mix_saved = (h, hn, zpre, y, ws, bsb)
        hm, a = norm_mm(out, row(rep["mlp_norm"][layer]), *wl("mlp_w1", layer), BF16, tag + "_mlp_up")
        nxt = mm_res(a, *wl("mlp_w2", layer), out, tag + "_mlp_down", relu2=True)
        saved.append((mix_saved, (out, hm, a)))
        h = nxt
    dh, dhb, d_final, loss_tile = loss_head(h, row(rep["final_norm"]), target, "loss_head")
    big, tags = [], []
    small = {k: [jnp.zeros(v.shape[1:], F32)] * v.shape[0] for k, v in rep.items() if k != "final_norm"}
    small["final_norm"] = d_final.reshape(-1)
    stacked = lambda: [small[n] if n == "final_norm" else jnp.stack(small[n]) for n in SMALL]
    for layer in reversed(range(depth)):
        i = layer // 2
        tag = f"l{layer}"
        mix_saved, (xin, hm, a) = saved[layer]
        da = mm_nt_relu2_bwd(dhb, *wl("mlp_w2", layer), a, tag + "_mlp_down_bwd")
        big.append(dw_mm(a, dhb, tag + "_mlp_dw2", col_sharded=False, relu2=True))
        tags.append(("mlp_w2", layer))
        big.append(dw_mm(hm, da, tag + "_mlp_dw1", col_sharded=True))
        tags.append(("mlp_w1", layer))
        dh, dhb, dg = dx_norm(da, *wl("mlp_w1", layer), xin, row(rep["mlp_norm"][layer]), dh, tag + "_mlp_up_bwd")
        small["mlp_norm"][layer] = dg.reshape(-1)
        if layer % 2 == 0:
            xin, hn, proj, qkv_t, kv_tok, oa, ob, lse_a, lse_b = mix_saved
            do_t = mm_nt(dhb, *wl("att_w_out", i), tag + "_att_out_bwd", transposed=True)
            big.append(dw_nn([oa, ob], dhb, tag + "_att_dwout"))
            tags.append(("att_w_out", i))
            dqa, dka, dva, dsink = flash_bwd_t(qkv_t, kv_tok, oa, do_t, lse_a, QA_COL // GROUP_W, 0, 2,
                                               KA_COL // HEAD_DIM, 0, rep["att_sink"][i], True, tag + "_win_bwd")
            plan = None
            if layer == 0:
                early = [k for k, t in enumerate(tags) if t not in LAST_GROUP]
                chip_sums, plan = reduce_begin([big[k] for k in early], "grads1")
                early_pack = _pack(stacked() + [loss_tile[0, :1]])
                plan = add_to_all(plan, early_pack)
            dqb, dkb, dvb, *got = flash_bwd_t(qkv_t, kv_tok, ob, do_t, lse_b, QB_COL // GROUP_W, 4, 6,
                                              KB_COL // HEAD_DIM, 2, None, False, tag + "_grid_bwd", comm=plan)
            if layer == 0:
                grads = reduce_finish(chip_sums, got[:-1], [tags[k] for k in early], full_shapes, {}, "grads1")
                early_sum = sum_devices(early_pack, got[-1], "sum_small")
            qg, kg = _tile2(rep["att_qnorm"][i]), _tile2(rep["att_knorm"][i])
            dproj, dqg, dkg = prep_bwd(proj, dqa, _tok(dka), _tok(dva), dqb, _tok(dkb), _tok(dvb),
                                       tabs, qg, kg, tag + "_att_prep_bwd")
            big.append(dw_mm(hn, dproj, tag + "_att_dwin", col_sharded=True))
            tags.append(("att_w_in", i))
            dh, dhb, dg = dx_norm(dproj, *wl("att_w_in", i), xin, row(rep["att_norm"][i]), dh, tag + "_att_proj_bwd")
            small["att_norm"][i] = dg.reshape(-1)
            small["att_sink"][i] = dsink[:, 0, :GROUP].reshape(-1)
            small["att_qnorm"][i] = dqg[0, :HEAD_DIM] + dqg[0, HEAD_DIM:]
            small["att_knorm"][i] = dkg[0, :HEAD_DIM] + dkg[0, HEAD_DIM:]
        else:
            xin, hn, zpre, y, ws, bsb = mix_saved
            dy = mm_nt(dhb, *wl("sgu_w_out", i), tag + "_sgu_out_bwd")
            big.append(dw_mm(y, dhb, tag + "_sgu_dwout", col_sharded=False))
            tags.append(("sgu_w_out", i))
            wst = ws.transpose(0, 2, 1)
            dz, dws, dbs, dlg, dlb = sgu_mid_bwd(zpre, dy, row(rep["sgu_ln_g"][i]), row(rep["sgu_ln_b"][i]),
                                                 ws, wst, bsb, tag + "_sgu_mid_bwd")
            big.append(dw_mm(hn, dz, tag + "_sgu_dwin", col_sharded=True))
            tags.append(("sgu_w_in", i))
            dh, dhb, dg = dx_norm(dz, *wl("sgu_w_in", i), xin, row(rep["sgu_norm"][i]), dh, tag + "_sgu_in_bwd")
            small["sgu_norm"][i] = dg.reshape(-1)
            small["sgu_ln_g"][i] = dlg.reshape(-1)
            small["sgu_ln_b"][i] = dlb.reshape(-1)
            small["sgu_w_s"][i] = dws
            small["sgu_b_s"][i] = dbs[:, :, 0]
    late = [k for k, t in enumerate(tags) if t in LAST_GROUP]
    chip_sums, plan = reduce_begin([big[k] for k in late], "grads2")
    late_pack = _pack([small[n][0] for n in LATE_SMALL])
    got = exchange("grads2_scatter", *add_to_all(plan, late_pack), [])
    grads = reduce_finish(chip_sums, got[:-1], [tags[k] for k in late], full_shapes, grads, "grads2")
    late_sum = sum_devices(late_pack, got[-1], "sum_small_late")
    shapes = [a.shape for a in stacked()]
    *small_g, loss = _unpack(early_sum, shapes + [()])
    small_g = dict(zip(SMALL, small_g))
    for n, g in zip(LATE_SMALL, _unpack(late_sum, [small[n][0].shape for n in LATE_SMALL])):
        small_g[n] = small_g[n].at[0].add(g)
    return loss, dh, grads, small_g


BIG = ("att_w_in", "att_w_out", "sgu_w_in", "sgu_w_out", "mlp_w1", "mlp_w2")
SHARDED_VEC = ("sgu_norm", "sgu_ln_g", "sgu_ln_b")
REPLICATED = ("att_norm", "att_sink", "att_qnorm", "att_knorm", "sgu_w_s", "sgu_b_s", "mlp_norm", "final_norm")
WEIGHTS = ("att_norm", "att_w_in", "att_sink", "att_qnorm", "att_knorm", "att_w_out", "sgu_norm", "sgu_w_in",
           "sgu_ln_g", "sgu_ln_b", "sgu_w_s", "sgu_b_s", "sgu_w_out", "mlp_norm", "mlp_w1", "mlp_w2", "final_norm")
SMALL = tuple(n for n in WEIGHTS if n not in BIG)
PACK_ALIGN = 8 * LANES


def _pack(arrays):
    flat = jnp.concatenate([a.reshape(-1) for a in arrays])
    pad = -flat.shape[0] % PACK_ALIGN
    return jnp.pad(flat, (0, pad)).reshape(-1, LANES)


def _unpack(flat2d, shapes):
    flat = flat2d.reshape(-1)
    out, off = [], 0
    for shape in shapes:
        size = int(np.prod(shape))
        out.append(flat[off:off + size].reshape(shape))
        off += size
    return out


def kernel(x, att_norm, att_w_in, att_sink, att_qnorm, att_knorm, att_w_out, sgu_norm, sgu_w_in, sgu_ln_g, sgu_ln_b, sgu_w_s, sgu_b_s, sgu_w_out, mlp_norm, mlp_w1, mlp_w2, final_norm, loss_target, m_att_norm, m_att_w_in, m_att_sink, m_att_qnorm, m_att_knorm, m_att_w_out, m_sgu_norm, m_sgu_w_in, m_sgu_ln_g, m_sgu_ln_b, m_sgu_w_s, m_sgu_b_s, m_sgu_w_out, m_mlp_norm, m_mlp_w1, m_mlp_w2, m_final_norm, v_att_norm, v_att_w_in, v_att_sink, v_att_qnorm, v_att_knorm, v_att_w_out, v_sgu_norm, v_sgu_w_in, v_sgu_ln_g, v_sgu_ln_b, v_sgu_w_s, v_sgu_b_s, v_sgu_w_out, v_mlp_norm, v_mlp_w1, v_mlp_w2, v_final_norm):
    w = dict(att_norm=att_norm, att_w_in=att_w_in, att_sink=att_sink, att_qnorm=att_qnorm, att_knorm=att_knorm,
             att_w_out=att_w_out, sgu_norm=sgu_norm, sgu_w_in=sgu_w_in, sgu_ln_g=sgu_ln_g, sgu_ln_b=sgu_ln_b,
             sgu_w_s=sgu_w_s, sgu_b_s=sgu_b_s, sgu_w_out=sgu_w_out, mlp_norm=mlp_norm, mlp_w1=mlp_w1,
             mlp_w2=mlp_w2, final_norm=final_norm)
    m = dict(att_norm=m_att_norm, att_w_in=m_att_w_in, att_sink=m_att_sink, att_qnorm=m_att_qnorm,
             att_knorm=m_att_knorm, att_w_out=m_att_w_out, sgu_norm=m_sgu_norm, sgu_w_in=m_sgu_w_in,
             sgu_ln_g=m_sgu_ln_g, sgu_ln_b=m_sgu_ln_b, sgu_w_s=m_sgu_w_s, sgu_b_s=m_sgu_b_s,
             sgu_w_out=m_sgu_w_out, mlp_norm=m_mlp_norm, mlp_w1=m_mlp_w1, mlp_w2=m_mlp_w2,
             final_norm=m_final_norm)
    v = dict(att_norm=v_att_norm, att_w_in=v_att_w_in, att_sink=v_att_sink, att_qnorm=v_att_qnorm,
             att_knorm=v_att_knorm, att_w_out=v_att_w_out, sgu_norm=v_sgu_norm, sgu_w_in=v_sgu_w_in,
             sgu_ln_g=v_sgu_ln_g, sgu_ln_b=v_sgu_ln_b, sgu_w_s=v_sgu_w_s, sgu_b_s=v_sgu_b_s,
             sgu_w_out=v_sgu_w_out, mlp_norm=v_mlp_norm, mlp_w1=v_mlp_w1, mlp_w2=v_mlp_w2,
             final_norm=v_final_norm)
    chip = 2 * lax.axis_index("x") + lax.axis_index("y")

    vecs = jnp.stack([w[n] for n in SHARDED_VEC])
    wb = {n: w[n].astype(BF16) for n in BIG}
    first, vec_all = gather_whole([wb["att_w_in"][0:1], vecs], "gather_first")
    rest_shards = [wb[n][1:2] if n == "att_w_in" else wb[n] for n in REST]
    vec_full = vec_all.transpose(1, 2, 0, 3).reshape(vecs.shape[0], vecs.shape[1], -1)
    rep = {n: w[n] for n in REPLICATED}
    rep.update({n: vec_full[k] for k, n in enumerate(SHARDED_VEC)})

    loss, grad_x, grads, small_g = local_step(x[0], loss_target[0], first, rest_shards, rep,
                                              {n: w[n].shape for n in BIG})
    width = w["sgu_norm"].shape[1]
    for n in SHARDED_VEC:
        small_g[n] = lax.dynamic_slice_in_dim(small_g[n], chip * width, width, axis=1)
    grads.update(small_g)
    for n in BIG:
        grads[n] = grads[n].reshape(w[n].shape)

    delta, new_m, new_v = {}, {}, {}
    for n in WEIGHTS:
        shape = w[n].shape
        two_d = (lambda a: a.reshape(1, -1)) if len(shape) == 1 else (lambda a: a)
        dn, mn, vn = adamw(two_d(w[n]), two_d(grads[n]), two_d(m[n]), two_d(v[n]), "adamw_" + n)
        delta[n], new_m[n], new_v[n] = dn.reshape(shape), mn.reshape(shape), vn.reshape(shape)
    return (loss, grad_x[None], *[grads[n] for n in WEIGHTS], *[delta[n] for n in WEIGHTS],
            *[new_m[n] for n in WEIGHTS], *[new_v[n] for n in WEIGHTS])
```

```python
import functools

import numpy as np
import jax
import jax.numpy as jnp
from jax import lax
from jax.experimental import pallas as pl
from jax.experimental.pallas import tpu as pltpu

F32 = jnp.float32
BF16 = jnp.bfloat16
MESH = pl.DeviceIdType.MESH

EPS = 1e-6
HEAD_DIM = 64
BLOCK = 128
GRID_W = 64
ROPE_THETA = 10000.0
N_CHIPS = 4
LANES = 128
V7X_VMEM_BYTES = 64 * 1024 * 1024
VMEM_LIMIT = V7X_VMEM_BYTES - 8 * 1024 * 1024

ADAM_LR = 0.001
ADAM_B1 = 0.9
ADAM_B2 = 0.999
ADAM_EPS = 1e-08
ADAM_WD = 0.01
ADAM_STEP = 10

NT_DIMS = (((1,), (1,)), ((), ()))
TN_DIMS = (((0,), (0,)), ((), ()))


def _params(*sem):
    return pltpu.CompilerParams(dimension_semantics=sem, vmem_limit_bytes=VMEM_LIMIT)


def _sds(shape, dtype):
    return jax.ShapeDtypeStruct(tuple(shape), dtype)


def _row_tile(rows, want):
    t = min(rows, want)
    assert rows % t == 0, (rows, want)
    return t


def norm_mm(x, g, w4, layer, out_dtype, name):
    s_len, d = x.shape
    ns = w4.shape[-1]
    tm = _row_tile(s_len, 512)

    def body(x_ref, g_ref, w_ref, h_ref, y_ref):
        xf = x_ref[...]
        r = lax.rsqrt(jnp.mean(xf * xf, axis=-1, keepdims=True) + EPS)
        h = ((xf * r) * g_ref[...]).astype(BF16)
        h_ref[...] = h
        for s in range(N_CHIPS):
            y_ref[:, s * ns:(s + 1) * ns] = jnp.dot(h, w_ref[s], preferred_element_type=F32).astype(y_ref.dtype)

    return pl.pallas_call(
        body, name=name, grid=(s_len // tm,),
        in_specs=[pl.BlockSpec((tm, d), lambda i: (i, 0)),
                  pl.BlockSpec((1, d), lambda i: (0, 0)),
                  pl.BlockSpec((N_CHIPS, None, d, ns), lambda i: (0, layer, 0, 0))],
        out_specs=[pl.BlockSpec((tm, d), lambda i: (i, 0)),
                   pl.BlockSpec((tm, N_CHIPS * ns), lambda i: (i, 0))],
        out_shape=[_sds((s_len, d), BF16), _sds((s_len, N_CHIPS * ns), out_dtype)],
        compiler_params=_params("arbitrary"),
    )(x, g, w4)


def mm_res(a, w4, layer, res, name, relu2=False):
    s_len, k = a.shape
    kq, n = w4.shape[-2:]
    assert kq * N_CHIPS == k
    tm = _row_tile(s_len, 256 if k > 1024 else 512)

    def body(a_ref, w0, w1, w2, w3, r_ref, o_ref):
        acc = r_ref[...]
        for s, w_ref in enumerate((w0, w1, w2, w3)):
            av = a_ref[:, s * kq:(s + 1) * kq]
            if relu2:
                t = jnp.maximum(av.astype(F32), 0.0)
                av = (t * t).astype(BF16)
            acc = acc + jnp.dot(av, w_ref[...], preferred_element_type=F32)
        o_ref[...] = acc

    def wspec(s):
        return pl.BlockSpec((None, None, kq, n), lambda i: (s, layer, 0, 0))

    return pl.pallas_call(
        body, name=name, grid=(s_len // tm,),
        in_specs=[pl.BlockSpec((tm, k), lambda i: (i, 0)), wspec(0), wspec(1), wspec(2), wspec(3),
                  pl.BlockSpec((tm, n), lambda i: (i, 0))],
        out_specs=pl.BlockSpec((tm, n), lambda i: (i, 0)),
        out_shape=_sds((s_len, n), F32),
        compiler_params=_params("arbitrary"),
    )(a, w4, w4, w4, w4, res)


def mm_res_t(pieces, w4, layer, res, name):
    s_len = res.shape[0]
    kq, n = w4.shape[-2:]
    rows = pieces[0].shape[0]
    assert rows % kq == 0 and rows * len(pieces) == kq * N_CHIPS
    tm = _row_tile(s_len, 512)
    n_p = len(pieces)

    def body(*refs):
        p_refs, w_refs, (r_ref, o_ref) = refs[:n_p], refs[n_p:n_p + N_CHIPS], refs[n_p + N_CHIPS:]
        acc = r_ref[...]
        for s in range(N_CHIPS):
            p, off = divmod(s * kq, rows)
            acc = acc + lax.dot_general(p_refs[p][off:off + kq, :], w_refs[s][...], TN_DIMS,
                                        preferred_element_type=F32)
        o_ref[...] = acc

    def wspec(s):
        return pl.BlockSpec((None, None, kq, n), lambda i: (s, layer, 0, 0))

    return pl.pallas_call(
        body, name=name, grid=(s_len // tm,),
        in_specs=[pl.BlockSpec((rows, tm), lambda i: (0, i))] * n_p + [wspec(s) for s in range(N_CHIPS)]
        + [pl.BlockSpec((tm, n), lambda i: (i, 0))],
        out_specs=pl.BlockSpec((tm, n), lambda i: (i, 0)),
        out_shape=_sds((s_len, n), F32),
        compiler_params=_params("arbitrary"),
    )(*pieces, w4, w4, w4, w4, res)


def dw_nn(pieces, b, name):
    s_len, n = b.shape
    rows = pieces[0].shape[0]
    n_p = len(pieces)
    k = rows * n_p
    ts = _row_tile(s_len, 2048)
    n_s = s_len // ts

    def body(*refs):
        p_refs, (b_ref, o_ref, acc_ref) = refs[:n_p], refs[n_p:]
        s = pl.program_id(0)
        bv = b_ref[...]
        for p in range(n_p):
            part = jnp.dot(p_refs[p][...], bv, preferred_element_type=F32)
            at = slice(p * rows, (p + 1) * rows)
            if n_s == 1:
                o_ref[at, :] = part.astype(BF16)
                continue

            @pl.when(s == 0)
            def _():
                acc_ref[at, :] = part

            @pl.when((s > 0) & (s < n_s - 1))
            def _():
                acc_ref[at, :] += part

            @pl.when(s == n_s - 1)
            def _():
                o_ref[at, :] = (acc_ref[at, :] + part).astype(BF16)

    out = pl.pallas_call(
        body, name=name, grid=(n_s,),
        in_specs=[pl.BlockSpec((rows, ts), lambda s: (0, s))] * n_p + [pl.BlockSpec((ts, n), lambda s: (s, 0))],
        out_specs=pl.BlockSpec((k, n), lambda s: (0, 0)), out_shape=_sds((k, n), BF16),
        scratch_shapes=[pltpu.VMEM((k, n), F32)],
        compiler_params=_params("arbitrary"),
    )(*pieces, b)
    return out.reshape(N_CHIPS, k // N_CHIPS, n)


def mm_nt(dy, w4, layer, name, transposed=False):
    s_len, n = dy.shape
    mq = w4.shape[-2]
    tm = _row_tile(s_len, 512)

    def body(d_ref, w0, w1, w2, w3, o_ref):
        dv = d_ref[...]
        for s, w_ref in enumerate((w0, w1, w2, w3)):
            if transposed:
                o_ref[s * mq:(s + 1) * mq, :] = lax.dot_general(
                    w_ref[...], dv, NT_DIMS, preferred_element_type=F32).astype(BF16)
            else:
                o_ref[:, s * mq:(s + 1) * mq] = lax.dot_general(
                    dv, w_ref[...], NT_DIMS, preferred_element_type=F32).astype(BF16)

    def wspec(s):
        return pl.BlockSpec((None, None, mq, n), lambda i: (s, layer, 0, 0))

    m = N_CHIPS * mq
    return pl.pallas_call(
        body, name=name, grid=(s_len // tm,),
        in_specs=[pl.BlockSpec((tm, n), lambda i: (i, 0)), wspec(0), wspec(1), wspec(2), wspec(3)],
        out_specs=pl.BlockSpec((m, tm), lambda i: (0, i)) if transposed else pl.BlockSpec((tm, m), lambda i: (i, 0)),
        out_shape=_sds((m, s_len) if transposed else (s_len, m), BF16),
        compiler_params=_params("arbitrary"),
    )(dy, w4, w4, w4, w4)


def mm_nt_relu2_bwd(dy, w4, layer, a, name):
    s_len, n = dy.shape
    mq = w4.shape[-2]
    tm = _row_tile(s_len, 512)

    def body(d_ref, w_ref, a_ref, o_ref):
        dv = d_ref[...]
        for s in range(N_CHIPS):
            cols = slice(s * mq, (s + 1) * mq)
            dz = lax.dot_general(dv, w_ref[s], NT_DIMS, preferred_element_type=F32)
            o_ref[:, cols] = (dz * (2.0 * jnp.maximum(a_ref[:, cols].astype(F32), 0.0))).astype(BF16)

    return pl.pallas_call(
        body, name=name, grid=(s_len // tm,),
        in_specs=[pl.BlockSpec((tm, n), lambda i: (i, 0)),
                  pl.BlockSpec((N_CHIPS, None, mq, n), lambda i: (0, layer, 0, 0)),
                  pl.BlockSpec((tm, N_CHIPS * mq), lambda i: (i, 0))],
        out_specs=pl.BlockSpec((tm, N_CHIPS * mq), lambda i: (i, 0)),
        out_shape=_sds((s_len, N_CHIPS * mq), BF16),
        compiler_params=_params("arbitrary"),
    )(dy, w4, a)


def dx_norm(dy, w4, layer, x, g, dres, name):
    s_len, d = x.shape
    ns = w4.shape[-1]
    tm = _row_tile(s_len, 512)

    def body(dy_ref, w_ref, x_ref, g_ref, dr_ref, dx_ref, dxb_ref, dg_ref):
        i = pl.program_id(0)
        dh = lax.dot_general(dy_ref[:, 0:ns], w_ref[0], NT_DIMS, preferred_element_type=F32)
        for s in range(1, N_CHIPS):
            dh = dh + lax.dot_general(dy_ref[:, s * ns:(s + 1) * ns], w_ref[s], NT_DIMS,
                                      preferred_element_type=F32)
        xf = x_ref[...]
        r = lax.rsqrt(jnp.mean(xf * xf, axis=-1, keepdims=True) + EPS)
        xhat = xf * r
        dg_part = jnp.sum(dh * xhat, axis=0, keepdims=True)

        @pl.when(i == 0)
        def _():
            dg_ref[...] = dg_part

        @pl.when(i > 0)
        def _():
            dg_ref[...] += dg_part

        dxh = dh * g_ref[...]
        dx = dr_ref[...] + r * (dxh - xhat * jnp.mean(dxh * xhat, axis=-1, keepdims=True))
        dx_ref[...] = dx
        dxb_ref[...] = dx.astype(BF16)

    row = pl.BlockSpec((tm, d), lambda i: (i, 0))
    vec = pl.BlockSpec((1, d), lambda i: (0, 0))
    return pl.pallas_call(
        body, name=name, grid=(s_len // tm,),
        in_specs=[pl.BlockSpec((tm, N_CHIPS * ns), lambda i: (i, 0)),
                  pl.BlockSpec((N_CHIPS, None, d, ns), lambda i: (0, layer, 0, 0)), row, vec, row],
        out_specs=[row, row, vec],
        out_shape=[_sds((s_len, d), F32), _sds((s_len, d), BF16), _sds((1, d), F32)],
        compiler_params=_params("arbitrary"),
    )(dy, w4, x, g, dres)


def dw_mm(a, b, name, col_sharded, relu2=False):
    s_len, k = a.shape
    n = b.shape[1]
    ts = _row_tile(s_len, 2048)
    tk = min(k, 1024)
    tn = n // N_CHIPS if col_sharded else min(n, 1024)
    n_s = s_len // ts

    def body(a_ref, b_ref, o_ref, acc_ref):
        s = pl.program_id(2)
        av = a_ref[...]
        if relu2:
            t = jnp.maximum(av.astype(F32), 0.0)
            av = (t * t).astype(BF16)
        part = lax.dot_general(av, b_ref[...], TN_DIMS, preferred_element_type=F32)
        if n_s == 1:
            o_ref[...] = part.astype(BF16)
            return

        @pl.when(s == 0)
        def _():
            acc_ref[...] = part

        @pl.when((s > 0) & (s < n_s - 1))
        def _():
            acc_ref[...] += part

        @pl.when(s == n_s - 1)
        def _():
            o_ref[...] = (acc_ref[...] + part).astype(BF16)

    if col_sharded:
        out_shape = _sds((N_CHIPS, k, tn), BF16)
        out_spec = pl.BlockSpec((None, tk, tn), lambda i, j, s: (j, i, 0))
    else:
        out_shape = _sds((N_CHIPS, k // N_CHIPS, n), BF16)
        rows_per = k // N_CHIPS
        assert tk % rows_per == 0 or rows_per % tk == 0
        if tk >= rows_per:
            out_shape = _sds((k, n), BF16)
            out_spec = pl.BlockSpec((tk, tn), lambda i, j, s: (i, j))
        else:
            per = rows_per // tk
            out_spec = pl.BlockSpec((None, tk, tn), lambda i, j, s: (i // per, i % per, j))

    out = pl.pallas_call(
        body, name=name, grid=(k // tk, n // tn, n_s),
        in_specs=[pl.BlockSpec((ts, tk), lambda i, j, s: (s, i)),
                  pl.BlockSpec((ts, tn), lambda i, j, s: (s, j))],
        out_specs=out_spec, out_shape=out_shape,
        scratch_shapes=[pltpu.VMEM((tk, tn), F32)],
        compiler_params=_params("arbitrary", "arbitrary", "arbitrary"),
    )(a, b)
    if not col_sharded:
        out = out.reshape(N_CHIPS, k // N_CHIPS, n)
    return out


def ew(fn, ins, out_dtypes, name, tile_rows=256):
    rows, cols = ins[0].shape
    for a in ins:
        assert a.shape == (rows, cols), (name, a.shape, rows, cols)
    tr = rows if (rows <= tile_rows or rows % tile_rows) else tile_rows
    n_in = len(ins)

    def body(*refs):
        outs = fn(*[r[...] for r in refs[:n_in]])
        for o_ref, val in zip(refs[n_in:], outs):
            o_ref[...] = val.astype(o_ref.dtype)

    spec = pl.BlockSpec((tr, cols), lambda i: (i, 0))
    return pl.pallas_call(
        body, name=name, grid=(rows // tr,),
        in_specs=[spec] * n_in, out_specs=[spec] * len(out_dtypes),
        out_shape=[_sds((rows, cols), dt) for dt in out_dtypes],
        compiler_params=_params("arbitrary"),
    )(*ins)


def adamw(w, g, m, v, name):
    shape = w.shape
    cols = shape[-1]
    two_d = lambda a: a.reshape(-1, cols)

    def fn(wv, gv, mv, vv):
        m_new = ADAM_B1 * mv + (1.0 - ADAM_B1) * gv
        v_new = ADAM_B2 * vv + (1.0 - ADAM_B2) * (gv * gv)
        m_hat = m_new / (1.0 - ADAM_B1 ** ADAM_STEP)
        v_hat = v_new / (1.0 - ADAM_B2 ** ADAM_STEP)
        delta = -ADAM_LR * (m_hat / (jnp.sqrt(v_hat) + ADAM_EPS) + ADAM_WD * wv)
        return delta, m_new, v_new

    d, mn, vn = ew(fn, [two_d(w), two_d(g), two_d(m), two_d(v)], [F32, F32, F32], name)
    return d.reshape(shape), mn.reshape(shape), vn.reshape(shape)


def rope_tables(s_len):
    def angles(pos, dim):
        freqs = ROPE_THETA ** (-jnp.arange(0, dim, 2, dtype=F32) / dim)
        ang = pos.astype(F32)[:, None] * freqs[None, :]
        return jnp.cos(ang), jnp.sin(ang)

    pos = jnp.arange(s_len)
    rows = s_len // GRID_W
    row_idx = jnp.repeat(jnp.arange(rows), GRID_W)
    col_idx = jnp.tile(jnp.arange(GRID_W), rows)
    c1, s1 = angles(pos, HEAD_DIM)
    cr, sr = angles(row_idx, HEAD_DIM // 2)
    cc, sc = angles(col_idx, HEAD_DIM // 2)
    cos1 = jnp.tile(jnp.concatenate([c1, c1], -1), (1, 2))
    sin1 = jnp.tile(jnp.concatenate([-s1, s1], -1), (1, 2))
    cos2 = jnp.tile(jnp.concatenate([cr, cr, cc, cc], -1), (1, 2))
    sin2 = jnp.tile(jnp.concatenate([-sr, sr, -sc, sc], -1), (1, 2))
    return cos1, sin1, cos2, sin2


def _lane_iota(rows):
    return lax.broadcasted_iota(jnp.int32, (rows, LANES), 1)


def _swap(x, dist, lane):
    return jnp.where((lane & dist) != 0, pltpu.roll(x, dist, 1), pltpu.roll(x, LANES - dist, 1))


def _head_ones():
    r = lax.broadcasted_iota(jnp.int32, (LANES, LANES), 0) // HEAD_DIM
    c = lax.broadcasted_iota(jnp.int32, (LANES, LANES), 1) // HEAD_DIM
    return (r == c).astype(BF16)


def _head_sum(t, ones):
    hi = t.astype(BF16)
    lo = (t - hi.astype(F32)).astype(BF16)
    return (jnp.dot(hi, ones, preferred_element_type=F32) + jnp.dot(lo, ones, preferred_element_type=F32))


Q_SCALE = HEAD_DIM ** -0.5
LOG2E = 1.4426950408889634
LN2 = 0.6931471805599453
CHUNK_KIND = ["qa"] * 4 + ["ka", "va"] + ["qb"] * 4 + ["kb", "vb"]
QA_COL, KA_COL, QB_COL, KB_COL = 0, 512, 768, 1280


def prep_fwd(proj, tabs, qn_g, kn_g, name):
    s_len, width = proj.shape
    ts = _row_tile(s_len, 512)
    cos1, sin1, cos2, sin2 = tabs

    def body(p_ref, c1_ref, s1_ref, c2_ref, s2_ref, qg_ref, kg_ref, o_ref, kv_ref):
        lane = _lane_iota(ts)
        ones = _head_ones()
        c1, s1, c2, s2 = c1_ref[...], s1_ref[...], c2_ref[...], s2_ref[...]
        n_kv = 0
        for cb, kind in enumerate(CHUNK_KIND):
            x = p_ref[:, cb * LANES:(cb + 1) * LANES]
            if kind in ("qa", "ka"):
                y = x * c1 + _swap(x, 32, lane) * s1
            elif kind in ("qb", "kb"):
                gain = qg_ref[...] if kind == "qb" else kg_ref[...]
                ms = _head_sum(x * x, ones) * (1.0 / HEAD_DIM)
                xn = (x * lax.rsqrt(ms + EPS)) * gain
                y = xn * c2 + _swap(xn, 16, lane) * s2
            else:
                y = x
            if kind in ("qa", "qb"):
                y = y * (Q_SCALE * LOG2E)
            else:
                kv_ref[:, n_kv * LANES:(n_kv + 1) * LANES] = y.astype(BF16)
                n_kv += 1
            o_ref[cb * LANES:(cb + 1) * LANES, :] = y.T.astype(BF16)

    tab = pl.BlockSpec((ts, LANES), lambda i: (i, 0))
    vec = pl.BlockSpec((1, LANES), lambda i: (0, 0))
    return pl.pallas_call(
        body, name=name, grid=(s_len // ts,),
        in_specs=[pl.BlockSpec((ts, width), lambda i: (i, 0)), tab, tab, tab, tab, vec, vec],
        out_specs=[pl.BlockSpec((width, ts), lambda i: (0, i)), pl.BlockSpec((ts, 4 * LANES), lambda i: (i, 0))],
        out_shape=[_sds((width, s_len), BF16), _sds((s_len, 4 * LANES), BF16)],
        compiler_params=_params("arbitrary"),
    )(proj, cos1, sin1, cos2, sin2, qn_g, kn_g)


def prep_bwd(proj, dqa, dka, dva, dqb, dkb, dvb, tabs, qn_g, kn_g, name):
    s_len, width = proj.shape
    ts = _row_tile(s_len, 256)
    cos1, sin1, cos2, sin2 = tabs

    def body(p_ref, dqa_ref, dka_ref, dva_ref, dqb_ref, dkb_ref, dvb_ref,
             c1_ref, s1_ref, c2_ref, s2_ref, qg_ref, kg_ref, o_ref, dqg_ref, dkg_ref):
        i = pl.program_id(0)
        lane = _lane_iota(ts)
        c1, s1, c2, s2 = c1_ref[...], s1_ref[...], c2_ref[...], s2_ref[...]

        def rope_t(dy, cos, sin, dist):
            return dy * cos + _swap(dy * sin, dist, lane)

        ones = _head_ones()

        def norm_bwd(dy, x, gain):
            r = lax.rsqrt(_head_sum(x * x, ones) * (1.0 / HEAD_DIM) + EPS)
            xhat = x * r
            dgain = jnp.sum(dy * xhat, axis=0, keepdims=True)
            dxh = dy * gain
            dx = r * (dxh - xhat * (_head_sum(dxh * xhat, ones) * (1.0 / HEAD_DIM)))
            return dx, dgain

        dqg = jnp.zeros((1, LANES), F32)
        dkg = jnp.zeros((1, LANES), F32)
        for cb, kind in enumerate(CHUNK_KIND):
            cols = slice(cb * LANES, (cb + 1) * LANES)
            if kind == "qa":
                dx = rope_t(dqa_ref[cols, :].T * Q_SCALE, c1, s1, 32)
            elif kind == "ka":
                dx = rope_t(dka_ref[...], c1, s1, 32)
            elif kind == "va":
                dx = dva_ref[...]
            elif kind == "qb":
                qcols = slice((cb - 6) * LANES, (cb - 5) * LANES)
                dy = rope_t(dqb_ref[qcols, :].T * Q_SCALE, c2, s2, 16)
                dx, dgain = norm_bwd(dy, p_ref[:, cols], qg_ref[...])
                dqg = dqg + dgain
            elif kind == "kb":
                dy = rope_t(dkb_ref[...], c2, s2, 16)
                dx, dgain = norm_bwd(dy, p_ref[:, cols], kg_ref[...])
                dkg = dkg + dgain
            else:
                dx = dvb_ref[...]
            o_ref[:, cols] = dx.astype(BF16)

        @pl.when(i == 0)
        def _():
            dqg_ref[...] = dqg
            dkg_ref[...] = dkg

        @pl.when(i > 0)
        def _():
            dqg_ref[...] += dqg
            dkg_ref[...] += dkg

    tab = pl.BlockSpec((ts, LANES), lambda i: (i, 0))
    vec = pl.BlockSpec((1, LANES), lambda i: (0, 0))

    def dq_spec(dq):
        per = dq.shape[2] // ts
        return pl.BlockSpec((None, 4 * LANES, ts), lambda i: (i // per, 0, i % per))

    return pl.pallas_call(
        body, name=name, grid=(s_len // ts,),
        in_specs=([pl.BlockSpec((ts, width), lambda i: (i, 0)), dq_spec(dqa), tab, tab, dq_spec(dqb), tab, tab]
                  + [tab] * 4 + [vec, vec]),
        out_specs=[pl.BlockSpec((ts, width), lambda i: (i, 0)), vec, vec],
        out_shape=[_sds((s_len, width), BF16), _sds((1, LANES), F32), _sds((1, LANES), F32)],
        compiler_params=_params("arbitrary"),
    )(proj, dqa, dka, dva, dqb, dkb, dvb, cos1, sin1, cos2, sin2, qn_g, kn_g)


NEG = -1e30
GROUP = 4
KV_HEADS = 2
GROUP_W = GROUP * HEAD_DIM
LSE_ROWS = 8
ONES_ROWS = 16


def _pos_mask_t(k_start, q_start, s_len, tk, tq):
    kpos = k_start + lax.broadcasted_iota(jnp.int32, (tk, tq), 0)
    qpos = q_start + lax.broadcasted_iota(jnp.int32, (tk, tq), 1)
    return (jnp.abs(kpos - qpos) <= BLOCK) & (kpos >= 0) & (kpos < s_len)


def flash_fwd_t(qkv_t, kv_tok, q_rb, k_i, v_rb, sink, window, name, comm=None):
    s_len = qkv_t.shape[1]
    if window:
        tq = _row_tile(s_len, 512)
        tk = 2 * BLOCK
        per = tq // tk
        assert per == 2, "the band parts below are written for query blocks of two key blocks"
        n_kv = per + 2
    else:
        tq = tk = _row_tile(s_len, 1024)
        n_kv = s_len // tk
    n_kb = s_len // tk
    n_i = s_len // tq
    c_ins, c_outs, c_remote = comm if comm else ([], [], [])
    n_main = 4 if window else 3

    def body(*refs):
        main, c_in_refs = refs[:n_main], refs[n_main:n_main + len(c_ins)]
        rest = refs[n_main + len(c_ins):]
        (o_ref, lse_ref), c_out_refs = rest[:2], rest[2:2 + len(c_outs)]
        m_sc, acc_sc = rest[2 + len(c_outs):4 + len(c_outs)]
        c_sems = rest[4 + len(c_outs):]
        if window:
            sink_ref, q_ref, k_ref, v_ref = main
        else:
            q_ref, k_ref, v_ref = main
        h, i, t = pl.program_id(0), pl.program_id(1), pl.program_id(2)
        if comm:
            @pl.when((h == 0) & (i == 0) & (t == 0))
            def _():
                _exchange_start(c_remote, c_in_refs, c_out_refs, *c_sems)

        @pl.when(t == 0)
        def _():
            for g in range(GROUP):
                acc_sc[g, 0:HEAD_DIM, :] = jnp.zeros((HEAD_DIM, tq), F32)
                if window:
                    m_sc[g] = jnp.full((1, tq), sink_ref[h * GROUP + g] * LOG2E, F32)
                    acc_sc[g, HEAD_DIM:, :] = jnp.ones((ONES_ROWS, tq), F32)
                else:
                    m_sc[g] = jnp.full((1, tq), NEG, F32)
                    acc_sc[g, HEAD_DIM:, :] = jnp.zeros((ONES_ROWS, tq), F32)

        def tile(k_lo, k_hi, q_lo, q_hi):
            ks, qs = slice(k_lo, k_hi), slice(q_lo, q_hi)
            k = k_ref[ks, :]
            v_t = jnp.concatenate([v_ref[:, ks], jnp.ones((ONES_ROWS, k_hi - k_lo), BF16)], axis=0)
            if window:
                mask = _pos_mask_t((i * per - 1 + t) * tk + k_lo, i * tq + q_lo, s_len, k_hi - k_lo, q_hi - q_lo)
            s_next = jnp.dot(k, q_ref[0:HEAD_DIM, qs], preferred_element_type=F32)
            for g in range(GROUP):
                s_t = s_next
                if g + 1 < GROUP:
                    s_next = jnp.dot(k, q_ref[(g + 1) * HEAD_DIM:(g + 2) * HEAD_DIM, qs],
                                     preferred_element_type=F32)
                if window:
                    s_t = jnp.where(mask, s_t, NEG)
                m_prev = m_sc[g, :, qs]
                m_new = jnp.maximum(m_prev, jnp.max(s_t, axis=0, keepdims=True))
                alpha = jnp.exp2(m_prev - m_new)
                p_t = jnp.exp2(s_t - m_new)
                acc_sc[g, :, qs] = alpha * acc_sc[g, :, qs] + jnp.dot(v_t, p_t.astype(BF16),
                                                                     preferred_element_type=F32)
                m_sc[g, :, qs] = m_new

        if window:
            for tt, part in enumerate([(tk - BLOCK, tk, 0, BLOCK), (0, tk, 0, tq - BLOCK),
                                       (0, tk, BLOCK, tq), (0, BLOCK, tq - BLOCK, tq)]):
                pl.when(t == tt)(functools.partial(tile, *part))
        else:
            tile(0, tk, 0, tq)

        @pl.when(t == n_kv - 1)
        def _():
            for g in range(GROUP):
                l = acc_sc[g, HEAD_DIM:HEAD_DIM + 1, :]
                o_ref[g * HEAD_DIM:(g + 1) * HEAD_DIM, :] = (acc_sc[g, 0:HEAD_DIM, :] / l).astype(BF16)
                lse_ref[g * LSE_ROWS:(g + 1) * LSE_ROWS, :] = jnp.broadcast_to(
                    m_sc[g] + jnp.log(l) * LOG2E, (LSE_ROWS, tq))

        if comm:
            @pl.when((h == KV_HEADS - 1) & (i == n_i - 1) & (t == n_kv - 1))
            def _():
                _exchange_finish(c_remote, c_in_refs, c_out_refs, *c_sems)

    if window:
        kv_blk = lambda i, t: jnp.clip(i * per - 1 + t, 0, n_kb - 1)
    else:
        kv_blk = lambda i, t: t
    hbm = pl.BlockSpec(memory_space=pl.ANY)
    in_specs = [pl.BlockSpec((GROUP_W, tq), lambda h, i, t: (q_rb + h, i)),
                pl.BlockSpec((None, tk, HEAD_DIM), lambda h, i, t: (k_i + h, kv_blk(i, t), 0)),
                pl.BlockSpec((HEAD_DIM, tk), lambda h, i, t: (v_rb + h, kv_blk(i, t)))]
    args = [qkv_t, kv_tok, qkv_t]
    if window:
        in_specs = [pl.BlockSpec(memory_space=pltpu.SMEM)] + in_specs
        args = [sink] + args
    return pl.pallas_call(
        body, name=name, grid=(KV_HEADS, n_i, n_kv),
        in_specs=in_specs + [hbm] * len(c_ins),
        out_specs=[pl.BlockSpec((GROUP_W, tq), lambda h, i, t: (h, i)),
                   pl.BlockSpec((GROUP * LSE_ROWS, tq), lambda h, i, t: (h, i))] + [hbm] * len(c_outs),
        out_shape=[_sds((KV_HEADS * GROUP_W, s_len), BF16),
                   _sds((KV_HEADS * GROUP * LSE_ROWS, s_len), F32)] + list(c_outs),
        scratch_shapes=[pltpu.VMEM((GROUP, 1, tq), F32),
                        pltpu.VMEM((GROUP, HEAD_DIM + ONES_ROWS, tq), F32)] + _exchange_sems(c_remote),
        compiler_params=_params("arbitrary", "arbitrary", "arbitrary"),
    )(*args, *c_ins)


def flash_bwd_t(qkv_t, kv_tok, o_t, do_t, lse, q_rb, k_i, v_i, k_rb, do_rb, sink, window, name, comm=None):
    s_len = qkv_t.shape[1]
    if window:
        tq = _row_tile(s_len, 512)
        tk = 2 * BLOCK
        assert tq == 2 * tk, "the band parts below are written for query blocks of two key blocks"
        n_q = 2
    else:
        tq = tk = _row_tile(s_len, 1024)
        n_q = s_len // tq
    n_qb = s_len // tq
    n_j = s_len // tk
    c_ins, c_outs, c_remote = comm if comm else ([], [], [])
    n_main = 8 if window else 7
    n_out = 4 if window else 3

    def body(*refs):
        main, c_in_refs = refs[:n_main], refs[n_main:n_main + len(c_ins)]
        rest = refs[n_main + len(c_ins):]
        outs, c_out_refs = rest[:n_out], rest[n_out:n_out + len(c_outs)]
        dk_sc, dv_sc = rest[n_out + len(c_outs):n_out + len(c_outs) + 2]
        c_sems = rest[n_out + len(c_outs) + 2:]
        if window:
            sink_ref, q_ref, k_ref, v_ref, kt_ref, o_ref, do_ref, lse_ref = main
            dq_ref, dk_ref, dv_ref, dsink_ref = outs
        else:
            q_ref, k_ref, v_ref, kt_ref, o_ref, do_ref, lse_ref = main
            dq_ref, dk_ref, dv_ref = outs
        h, j, t = pl.program_id(0), pl.program_id(1), pl.program_id(2)
        q_blk = (j + 1) // 2 - 1 + t if window else t
        if comm:
            @pl.when((h == 0) & (j == 0) & (t == 0))
            def _():
                _exchange_start(c_remote, c_in_refs, c_out_refs, *c_sems)

        @pl.when((j == 0) & (t == 0))
        def _():
            dq_ref[...] = jnp.zeros(dq_ref.shape, F32)
            if window:
                dsink_ref[...] = jnp.zeros((8, LANES), F32)

        @pl.when(t == 0)
        def _():
            dk_sc[...] = jnp.zeros((tk, HEAD_DIM), F32)
            dv_sc[...] = jnp.zeros((tk, HEAD_DIM), F32)

        def tile(k_lo, k_hi, q_lo, q_hi, sink_lo=0, sink_hi=0):
            ks, qs = slice(k_lo, k_hi), slice(q_lo, q_hi)
            k, v, k_t = k_ref[ks, :], v_ref[ks, :], kt_ref[:, ks]
            if window:
                mask = _pos_mask_t(j * tk + k_lo, q_blk * tq + q_lo, s_len, k_hi - k_lo, q_hi - q_lo)
                lane = lax.broadcasted_iota(jnp.int32, (8, LANES), 1)
                sink_tile = jnp.zeros((8, LANES), F32)
            dk_acc = dk_sc[ks, :]
            dv_acc = dv_sc[ks, :]
            for g in range(GROUP):
                rows = slice(g * HEAD_DIM, (g + 1) * HEAD_DIM)
                q_t, o_g, do_g = q_ref[rows, qs], o_ref[rows, qs], do_ref[rows, qs]
                s_t = jnp.dot(k, q_t, preferred_element_type=F32)
                if window:
                    s_t = jnp.where(mask, s_t, NEG)
                lse_row = lse_ref[g * LSE_ROWS:g * LSE_ROWS + 1, qs]
                p_t = jnp.exp2(s_t - lse_row)
                delta = jnp.sum(do_g.astype(F32) * o_g.astype(F32), axis=0, keepdims=True)
                dp_t = jnp.dot(v, do_g, preferred_element_type=F32)
                ds_t = (p_t * (dp_t - delta)).astype(BF16)
                dv_acc = dv_acc + lax.dot_general(p_t.astype(BF16), do_g, NT_DIMS, preferred_element_type=F32)
                dk_acc = dk_acc + lax.dot_general(ds_t, q_t, NT_DIMS, preferred_element_type=F32)
                dq_ref[q_blk, rows, qs] += jnp.dot(k_t, ds_t, preferred_element_type=F32)
                if sink_hi > sink_lo:
                    at = slice(sink_lo - q_lo, sink_hi - q_lo)
                    p_sink = jnp.exp2(sink_ref[h * GROUP + g] * LOG2E - lse_row[:, at])
                    term = -jnp.sum(p_sink * delta[:, at], axis=1, keepdims=True)
                    sink_tile = jnp.where(lane == g, term, sink_tile)
            dk_sc[ks, :] = dk_acc
            dv_sc[ks, :] = dv_acc
            if sink_hi > sink_lo:
                dsink_ref[...] += sink_tile

        if window:
            parts = {(0, 0): (0, BLOCK, tq - BLOCK, tq), (0, 1): (0, tk, 0, tq - BLOCK, 0, tq - BLOCK),
                     (1, 0): (0, tk, BLOCK, tq, tq - BLOCK, tq), (1, 1): (tk - BLOCK, tk, 0, BLOCK)}
            for (parity, tt), part in parts.items():
                pl.when((q_blk >= 0) & (q_blk < n_qb) & (j % 2 == parity) & (t == tt))(
                    functools.partial(tile, *part))
        else:
            tile(0, tk, 0, tq)

        @pl.when(t == n_q - 1)
        def _():
            dk_ref[...] = dk_sc[...] * LN2
            dv_ref[...] = dv_sc[...]

        if comm:
            @pl.when((h == KV_HEADS - 1) & (j == n_j - 1) & (t == n_q - 1))
            def _():
                _exchange_finish(c_remote, c_in_refs, c_out_refs, *c_sems)

    if window:
        qb = lambda j, t: jnp.clip((j + 1) // 2 - 1 + t, 0, n_qb - 1)
    else:
        qb = lambda j, t: t
    hbm = pl.BlockSpec(memory_space=pl.ANY)
    in_specs = [pl.BlockSpec((GROUP_W, tq), lambda h, j, t: (q_rb + h, qb(j, t))),
                pl.BlockSpec((None, tk, HEAD_DIM), lambda h, j, t: (k_i + h, j, 0)),
                pl.BlockSpec((None, tk, HEAD_DIM), lambda h, j, t: (v_i + h, j, 0)),
                pl.BlockSpec((HEAD_DIM, tk), lambda h, j, t: (k_rb + h, j)),
                pl.BlockSpec((GROUP_W, tq), lambda h, j, t: (h, qb(j, t))),
                pl.BlockSpec((GROUP_W, tq), lambda h, j, t: (do_rb + h, qb(j, t))),
                pl.BlockSpec((GROUP * LSE_ROWS, tq), lambda h, j, t: (h, qb(j, t)))]
    args = [qkv_t, kv_tok, kv_tok, qkv_t, o_t, do_t, lse]
    kv_out = _sds((KV_HEADS, s_len, HEAD_DIM), F32)
    out_specs = [pl.BlockSpec((n_qb, GROUP_W, tq), lambda h, j, t: (0, h, 0)),
                 pl.BlockSpec((None, tk, HEAD_DIM), lambda h, j, t: (h, j, 0)),
                 pl.BlockSpec((None, tk, HEAD_DIM), lambda h, j, t: (h, j, 0))]
    out_shape = [_sds((n_qb, KV_HEADS * GROUP_W, tq), F32), kv_out, kv_out]
    if window:
        in_specs = [pl.BlockSpec(memory_space=pltpu.SMEM)] + in_specs
        args = [sink] + args
        out_specs.append(pl.BlockSpec((None, 8, LANES), lambda h, j, t: (h, 0, 0)))
        out_shape.append(_sds((KV_HEADS, 8, LANES), F32))
    return pl.pallas_call(
        body, name=name, grid=(KV_HEADS, n_j, n_q),
        in_specs=in_specs + [hbm] * len(c_ins), out_specs=out_specs + [hbm] * len(c_outs),
        out_shape=out_shape + list(c_outs),
        scratch_shapes=[pltpu.VMEM((tk, HEAD_DIM), F32), pltpu.VMEM((tk, HEAD_DIM), F32)]
        + _exchange_sems(c_remote),
        compiler_params=_params("arbitrary", "arbitrary", "arbitrary"),
    )(*args, *c_ins)


SGU_GROUPS = 8
SGU_CHUNK = 128
GELU_C = float(np.sqrt(2.0 / np.pi))
GELU_A = 0.044715


def _gelu_and_grad(x):
    x2 = x * x
    t = jnp.tanh(x * (GELU_C + (GELU_C * GELU_A) * x2))
    hx = 0.5 * x
    return hx + hx * t, (0.5 + 0.5 * t) + (hx * (1.0 - t * t)) * (GELU_C + (3.0 * GELU_C * GELU_A) * x2)


def _gelu(x):
    t = jnp.tanh(x * (GELU_C + (GELU_C * GELU_A) * (x * x)))
    hx = 0.5 * x
    return hx + hx * t


def _layernorm_stats(v):
    mu = jnp.mean(v, axis=-1, keepdims=True)
    var = jnp.mean(jnp.square(v - mu), axis=-1, keepdims=True)
    rstd = lax.rsqrt(var + EPS)
    return (v - mu) * rstd, rstd


def sgu_mid_fwd(zpre, ln_g, ln_b, ws, bsb, name):
    s_len, width = zpre.shape
    d = width // 2
    ts = _row_tile(s_len, 256)

    def body(z_ref, g_ref, b_ref, ws_ref, bs_ref, y_ref):
        z = _gelu(z_ref[...])
        u, v = z[:, :d], z[:, d:]
        vhat, _ = _layernorm_stats(v)
        vn = (vhat * g_ref[...] + b_ref[...]).astype(BF16)
        for n in range(ts // SGU_CHUNK):
            rows = slice(n * SGU_CHUNK, (n + 1) * SGU_CHUNK)
            for g in range(SGU_GROUPS):
                cols = slice(g * LANES, (g + 1) * LANES)
                mixed = jnp.dot(ws_ref[g], vn[rows, cols], preferred_element_type=F32) + bs_ref[g]
                y_ref[rows, cols] = (u[rows, cols] * mixed).astype(BF16)

    vec = pl.BlockSpec((1, d), lambda i: (0, 0))
    cube = pl.BlockSpec((SGU_GROUPS, SGU_CHUNK, SGU_CHUNK), lambda i: (0, 0, 0))
    return pl.pallas_call(
        body, name=name, grid=(s_len // ts,),
        in_specs=[pl.BlockSpec((ts, width), lambda i: (i, 0)), vec, vec, cube, cube],
        out_specs=pl.BlockSpec((ts, d), lambda i: (i, 0)),
        out_shape=_sds((s_len, d), BF16),
        compiler_params=_params("arbitrary"),
    )(zpre, ln_g, ln_b, ws, bsb)


def sgu_mid_bwd(zpre, dy, ln_g, ln_b, ws, wst, bsb, name):
    s_len, width = zpre.shape
    d = width // 2
    ts = _row_tile(s_len, 256)
    n_steps = s_len // ts

    def body(z_ref, dy_ref, g_ref, b_ref, ws_ref, wst_ref, bs_ref,
             dz_ref, dws_ref, dbs_ref, dg_ref, db_ref, du_sc, dvn_sc):
        i = pl.program_id(0)

        @pl.when(i == 0)
        def _():
            dws_ref[...] = jnp.zeros(dws_ref.shape, F32)
            dbs_ref[...] = jnp.zeros(dbs_ref.shape, F32)
            dg_ref[...] = jnp.zeros(dg_ref.shape, F32)
            db_ref[...] = jnp.zeros(db_ref.shape, F32)

        zp = z_ref[...]
        z, gp = _gelu_and_grad(zp)
        u, v = z[:, :d], z[:, d:]
        vhat, rstd = _layernorm_stats(v)
        gain = g_ref[...]
        vn = (vhat * gain + b_ref[...]).astype(BF16)
        dyf = dy_ref[...].astype(F32)
        for n in range(ts // SGU_CHUNK):
            rows = slice(n * SGU_CHUNK, (n + 1) * SGU_CHUNK)
            for g in range(SGU_GROUPS):
                cols = slice(g * LANES, (g + 1) * LANES)
                vt = vn[rows, cols]
                mixed = jnp.dot(ws_ref[g], vt, preferred_element_type=F32) + bs_ref[g]
                dyt = dyf[rows, cols]
                du_sc[rows, cols] = dyt * mixed
                dmixed = dyt * u[rows, cols]
                dmb = dmixed.astype(BF16)
                dvn_sc[rows, cols] = jnp.dot(wst_ref[g], dmb, preferred_element_type=F32)
                dws_ref[g] += lax.dot_general(dmb, vt, NT_DIMS, preferred_element_type=F32)
                dbs_ref[g] += dmixed
        dvn = dvn_sc[...]
        dg_ref[...] += jnp.sum(dvn * vhat, axis=0, keepdims=True)
        db_ref[...] += jnp.sum(dvn, axis=0, keepdims=True)
        dvh = dvn * gain
        dv = rstd * (dvh - jnp.mean(dvh, axis=-1, keepdims=True)
                     - vhat * jnp.mean(dvh * vhat, axis=-1, keepdims=True))
        dz_ref[:, :d] = (du_sc[...] * gp[:, :d]).astype(BF16)
        dz_ref[:, d:] = (dv * gp[:, d:]).astype(BF16)

        @pl.when(i == n_steps - 1)
        def _():
            for g in range(SGU_GROUPS):
                tot = jnp.sum(dbs_ref[g], axis=1, keepdims=True)
                dbs_ref[g] = jnp.broadcast_to(tot, (SGU_CHUNK, LANES))

    vec = pl.BlockSpec((1, d), lambda i: (0, 0))
    cube = pl.BlockSpec((SGU_GROUPS, SGU_CHUNK, SGU_CHUNK), lambda i: (0, 0, 0))
    cube_shape = _sds((SGU_GROUPS, SGU_CHUNK, SGU_CHUNK), F32)
    return pl.pallas_call(
        body, name=name, grid=(n_steps,),
        in_specs=[pl.BlockSpec((ts, width), lambda i: (i, 0)), pl.BlockSpec((ts, d), lambda i: (i, 0)),
                  vec, vec, cube, cube, cube],
        out_specs=[pl.BlockSpec((ts, width), lambda i: (i, 0)), cube, cube, vec, vec],
        out_shape=[_sds((s_len, width), BF16), cube_shape, cube_shape, _sds((1, d), F32), _sds((1, d), F32)],
        scratch_shapes=[pltpu.VMEM((ts, d), F32), pltpu.VMEM((ts, d), F32)],
        compiler_params=_params("arbitrary"),
    )(zpre, dy, ln_g, ln_b, ws, wst, bsb)


def loss_head(x, g, target, name):
    s_len, d = x.shape
    tm = _row_tile(s_len, 512)

    def body(x_ref, g_ref, t_ref, dx_ref, dxb_ref, dg_ref, loss_ref):
        i = pl.program_id(0)
        xf = x_ref[...]
        gain = g_ref[...]
        r = lax.rsqrt(jnp.mean(xf * xf, axis=-1, keepdims=True) + EPS)
        xhat = xf * r
        err = xhat * gain - t_ref[...]
        row = jnp.mean(err * err, axis=-1, keepdims=True)
        part = 0.5 * jnp.sum(row, axis=0, keepdims=True)
        dy = err * (1.0 / d)
        dg_part = jnp.sum(dy * xhat, axis=0, keepdims=True)

        @pl.when(i == 0)
        def _():
            dg_ref[...] = dg_part
            loss_ref[...] = jnp.broadcast_to(part, (8, LANES))

        @pl.when(i > 0)
        def _():
            dg_ref[...] += dg_part
            loss_ref[...] += jnp.broadcast_to(part, (8, LANES))

        dxh = dy * gain
        dx = r * (dxh - xhat * jnp.mean(dxh * xhat, axis=-1, keepdims=True))
        dx_ref[...] = dx
        dxb_ref[...] = dx.astype(BF16)

    row_spec = pl.BlockSpec((tm, d), lambda i: (i, 0))
    vec = pl.BlockSpec((1, d), lambda i: (0, 0))
    return pl.pallas_call(
        body, name=name, grid=(s_len // tm,),
        in_specs=[row_spec, vec, row_spec],
        out_specs=[row_spec, row_spec, vec, pl.BlockSpec((8, LANES), lambda i: (0, 0))],
        out_shape=[_sds((s_len, d), F32), _sds((s_len, d), BF16), _sds((1, d), F32), _sds((8, LANES), F32)],
        compiler_params=_params("arbitrary"),
    )(x, g, target)


FLIP_BITS = {"c": (0, 0, 1), "x": (1, 0, 0), "y": (0, 1, 0), "xy": (1, 1, 0),
             "xc": (1, 0, 1), "yc": (0, 1, 1), "xyc": (1, 1, 1)}
CHIP_FLIPS = ("x", "y", "xy")


def _flip(pos, name):
    return tuple(1 - p if bit else p for p, bit in zip(pos, FLIP_BITS[name]))


def _chip(pos):
    return 2 * pos[0] + pos[1]


def _me():
    return (lax.axis_index("x"), lax.axis_index("y"), lax.axis_index("c"))


def _exchange_copy(remote, k, in_refs, out_refs, send_sems, recv_sems, sender, receiver):
    ii, src_fn, oi, dst_fn, _ = remote[k]
    return pltpu.make_async_remote_copy(
        src_ref=src_fn(in_refs[ii], sender, receiver), dst_ref=dst_fn(out_refs[oi], sender),
        send_sem=send_sems.at[k], recv_sem=recv_sems.at[k], device_id=receiver, device_id_type=MESH)


def _exchange_start(remote, in_refs, out_refs, send_sems, recv_sems):
    me = _me()
    for k in range(len(remote)):
        _exchange_copy(remote, k, in_refs, out_refs, send_sems, recv_sems, me, _flip(me, remote[k][4])).start()


def _exchange_finish(remote, in_refs, out_refs, send_sems, recv_sems):
    me = _me()
    for k in range(len(remote)):
        _exchange_copy(remote, k, in_refs, out_refs, send_sems, recv_sems, _flip(me, remote[k][4]), me).wait_recv()
    for k in range(len(remote)):
        _exchange_copy(remote, k, in_refs, out_refs, send_sems, recv_sems, me, _flip(me, remote[k][4])).wait_send()


def _exchange_sems(remote):
    n = len(remote)
    return [pltpu.SemaphoreType.DMA((n,)), pltpu.SemaphoreType.DMA((n,))] if n else []


def exchange(name, ins, out_shapes, remote, local):
    n_in, n_out = len(ins), len(out_shapes)

    def body(*refs):
        in_refs, out_refs = refs[:n_in], refs[n_in:n_in + n_out]
        send_sems, recv_sems, local_sems = refs[n_in + n_out:]
        me = _me()
        stays = []
        for k, (ii, src_fn, oi, dst_fn) in enumerate(local):
            cp = pltpu.make_async_copy(src_fn(in_refs[ii], me), dst_fn(out_refs[oi], me), local_sems.at[k])
            cp.start()
            stays.append(cp)
        _exchange_start(remote, in_refs, out_refs, send_sems, recv_sems)
        _exchange_finish(remote, in_refs, out_refs, send_sems, recv_sems)
        for cp in stays:
            cp.wait()

    hbm = pl.BlockSpec(memory_space=pl.ANY)
    return pl.pallas_call(
        body, name=name,
        in_specs=[hbm] * n_in, out_specs=[hbm] * n_out, out_shape=list(out_shapes),
        scratch_shapes=[pltpu.SemaphoreType.DMA((max(len(remote), 1),)),
                        pltpu.SemaphoreType.DMA((max(len(remote), 1),)),
                        pltpu.SemaphoreType.DMA((max(len(local), 1),))],
        compiler_params=pltpu.CompilerParams(has_side_effects=True),
    )(*ins)


def staged_push(name, ins, out_shapes, jobs, n_alias=0):
    n_in, n_out = len(ins), len(out_shapes)
    n_copies = sum(len(dsts) for _, _, dsts in jobs)
    n_remote = sum(1 for _, _, dsts in jobs for d in dsts if d[2] is not None)

    def chunk_of(ii, src_fn):
        probe = _ShapeRef(ins[ii].shape, ins[ii].dtype)
        got = src_fn(probe, (0, 0, 0))
        return tuple(got.shape), got.dtype

    classes = []
    for ii, src_fn, _ in jobs:
        c = chunk_of(ii, src_fn)
        if c not in classes:
            classes.append(c)

    def body(*refs):
        in_refs, out_refs = refs[:n_in], refs[n_in:n_in + n_out]
        bufs = refs[n_in + n_out:n_in + n_out + len(classes)]
        load_sems, out_sems, recv_sems = refs[n_in + n_out + len(classes):]
        me = _me()
        pending = [[[], []] for _ in classes]
        used = [0] * len(classes)
        arrivals = []
        k = r = 0

        def begin_load(job):
            ii, src_fn, _ = job
            cls = classes.index(chunk_of(ii, src_fn))
            slot = used[cls] % 2
            used[cls] += 1
            for kind, cp in pending[cls][slot]:
                cp.wait_send() if kind == "remote" else cp.wait()
            pending[cls][slot] = []
            load = pltpu.make_async_copy(src_fn(in_refs[ii], me), bufs[cls].at[slot], load_sems.at[2 * cls + slot])
            load.start()
            return load, cls, slot

        nxt = begin_load(jobs[0])
        for n, (ii, src_fn, dsts) in enumerate(jobs):
            load, cls, slot = nxt
            load.wait()
            buf = bufs[cls].at[slot]
            sent = []
            for oi, dst_fn, flip in dsts:
                if flip is None:
                    cp = pltpu.make_async_copy(buf, dst_fn(out_refs[oi], me), out_sems.at[k])
                    cp.start()
                    sent.append(("local", cp))
                else:
                    peer = _flip(me, flip)
                    cp = pltpu.make_async_remote_copy(
                        src_ref=buf, dst_ref=dst_fn(out_refs[oi], me), send_sem=out_sems.at[k],
                        recv_sem=recv_sems.at[r], device_id=peer, device_id_type=MESH)
                    cp.start()
                    sent.append(("remote", cp))
                    arrivals.append((r, cls, oi, dst_fn, peer))
                    r += 1
                k += 1
            pending[cls][slot] = sent
            if n + 1 < len(jobs):
                nxt = begin_load(jobs[n + 1])
        for per_class in pending:
            for slot_list in per_class:
                for kind, cp in slot_list:
                    cp.wait_send() if kind == "remote" else cp.wait()
        for r, cls, oi, dst_fn, peer in arrivals:
            pltpu.make_async_remote_copy(
                src_ref=bufs[cls].at[0], dst_ref=dst_fn(out_refs[oi], peer), send_sem=out_sems.at[0],
                recv_sem=recv_sems.at[r], device_id=peer, device_id_type=MESH).wait_recv()

    hbm = pl.BlockSpec(memory_space=pl.ANY)
    return pl.pallas_call(
        body, name=name,
        in_specs=[hbm] * n_in, out_specs=[hbm] * n_out, out_shape=list(out_shapes),
        scratch_shapes=[pltpu.VMEM((2,) + shape, dtype) for shape, dtype in classes]
        + [pltpu.SemaphoreType.DMA((2 * len(classes),)), pltpu.SemaphoreType.DMA((max(n_copies, 1),)),
           pltpu.SemaphoreType.DMA((max(n_remote, 1),))],
        input_output_aliases={i: i for i in range(n_alias)},
        compiler_params=pltpu.CompilerParams(has_side_effects=True, vmem_limit_bytes=VMEM_LIMIT),
    )(*ins)


class _ShapeRef:
    def __init__(self, shape, dtype):
        self.shape, self.dtype = tuple(shape), dtype

    @property
    def at(self):
        return self

    def __getitem__(self, idx):
        idx = idx if isinstance(idx, tuple) else (idx,)
        shape = []
        for dim, i in zip(self.shape, idx):
            if isinstance(i, slice):
                shape.append(len(range(*i.indices(dim))))
            elif hasattr(i, "size") and hasattr(i, "start"):
                shape.append(i.size)
        shape += self.shape[len(idx):]
        return _ShapeRef(shape, self.dtype)


def gather_whole(shards, name):
    whole = lambda ref, sender, receiver=None: ref
    slot = lambda ref, sender: ref.at[_chip(sender)]
    remote = [(t, whole, t, slot, flip) for t in range(len(shards)) for flip in CHIP_FLIPS]
    local = [(t, whole, t, slot) for t in range(len(shards))]
    outs = [_sds((N_CHIPS,) + a.shape, a.dtype) for a in shards]
    return exchange(name, list(shards), outs, remote, local)


def _half_axis(shape):
    return 0 if shape[0] >= 2 else 1


def gather_halves_plan(shards):
    remote = []
    for t, a in enumerate(shards):
        ax = _half_axis(a.shape)
        half = lambda ref, sender, receiver=None, ax=ax: _half(ref, sender[2], ax)
        slot = lambda ref, sender, ax=ax: _half(ref.at[_chip(sender)], sender[2], ax)
        remote += [(t, half, t, slot, flip) for flip in CHIP_FLIPS]
    outs = [_sds((N_CHIPS,) + a.shape, a.dtype) for a in shards]
    return list(shards), outs, remote


def gather_halves_fill(got, shards, name):
    n_t = len(shards)
    jobs = []
    for t, a in enumerate(shards):
        layers = a.shape[0]
        for l in range(layers):
            jobs.append((n_t + t, lambda ref, me, l=l: ref.at[l],
                         [(t, lambda ref, sender, l=l: ref.at[_chip(sender), l], None)]))
        for flip in CHIP_FLIPS:
            if _half_axis(a.shape) == 0:
                n = layers // 2
                for j in range(n):
                    at = lambda ref, pos, flip=flip, j=j, n=n: ref.at[_chip(_flip(pos, flip)), pos[2] * n + j]
                    jobs.append((t, at, [(t, at, "c")]))
            else:
                rows = a.shape[1] // 2
                at = lambda ref, pos, flip=flip, rows=rows: ref.at[
                    _chip(_flip(pos, flip)), 0, pl.ds(pos[2] * rows, rows)]
                jobs.append((t, at, [(t, at, "c")]))
    outs = [_sds(g.shape, g.dtype) for g in got]
    return staged_push(name, list(got) + list(shards), outs, jobs, n_alias=n_t)


def add_to_all(plan, buf):
    ins, outs, remote = plan
    whole = lambda ref, sender, receiver=None: ref
    more = [(len(ins), whole, len(outs), (lambda ref, sender, f=f: ref.at[f]), flip)
            for f, flip in enumerate(FLIPS_BY_INDEX)]
    return list(ins) + [buf], list(outs) + [_sds((len(more),) + buf.shape, buf.dtype)], list(remote) + more


FLIPS_BY_INDEX = ("c", "y", "yc", "x", "xc", "xy", "xyc")


def sum_devices(own, got, name):
    rows = own.shape[0]
    tr = LANES if rows % LANES == 0 else rows
    me = (4 * lax.axis_index("x") + 2 * lax.axis_index("y") + lax.axis_index("c")).astype(jnp.int32).reshape(1)
    everyone = jnp.concatenate([own[None], got], axis=0)

    def body(me_ref, a0, a1, a2, a3, a4, a5, a6, a7, o_ref):
        o_ref[...] = ((a0[...] + a1[...]) + (a2[...] + a3[...])) + ((a4[...] + a5[...]) + (a6[...] + a7[...]))

    return pl.pallas_call(
        body, name=name,
        grid_spec=pltpu.PrefetchScalarGridSpec(
            num_scalar_prefetch=1, grid=(rows // tr,),
            in_specs=[pl.BlockSpec((None, tr, LANES), lambda i, m, k=k: (m[0] ^ k, i, 0)) for k in range(8)],
            out_specs=pl.BlockSpec((tr, LANES), lambda i, m: (i, 0))),
        out_shape=_sds((rows, LANES), F32),
        compiler_params=_params("arbitrary"),
    )(me, *([everyone] * 8))


def _half(ref, core, axis):
    rows = ref.shape[axis] // 2
    idx = (slice(None),) * axis + (pl.ds(core * rows, rows),)
    return ref.at[idx]


def reduce_plan(grads):
    remote, outs = [], []
    for t, g in enumerate(grads):
        outs.append(_sds((len(CHIP_FLIPS),) + g.shape[1:], BF16))
        for f, flip in enumerate(CHIP_FLIPS):
            remote.append((t, lambda ref, sender, receiver: ref.at[_chip(receiver)],
                           t, lambda ref, sender, f=f: ref.at[f], flip))
    return list(grads), outs, remote


def reduce_finish(grads, got, stacks, full_shapes, into, name):
    chip = (2 * lax.axis_index("x") + lax.axis_index("y")).astype(jnp.int32).reshape(1)
    core = lax.axis_index("c").astype(jnp.int32).reshape(1)
    sums = [sum_chips(g, r, chip, f"{name}_sum{t}") for t, (g, r) in enumerate(zip(grads, got))]
    jobs, outs = [], []
    for t, a in enumerate(sums):
        half, cols = a.shape[0] // 2, a.shape[1]
        outs.append(_sds((half, cols), F32))
        pieces = max(1, half * cols * 4 // STAGE_BYTES)
        step = half // pieces
        for q in range(pieces):
            jobs.append((t, lambda ref, me, q=q, step=step, half=half: ref.at[pl.ds((1 - me[2]) * half + q * step, step)],
                         [(t, lambda ref, sender, q=q, step=step: ref.at[pl.ds(q * step, step)], "c")]))
    theirs = staged_push(name + "_swap", sums, outs, jobs)
    totals = [add_rows(a, r, core, f"{name}_add{t}") for t, (a, r) in enumerate(zip(sums, theirs))]
    names = []
    for out_name, _ in stacks:
        if out_name not in names:
            names.append(out_name)
    names = [n for n in names if n in into] + [n for n in names if n not in into]
    kept = [into[n] for n in names if n in into]
    outs = [_sds(full_shapes[n], F32) for n in names]
    jobs = []
    for t, (out_name, layer) in enumerate(stacks):
        oi = names.index(out_name)
        rows, cols = totals[t].shape
        pieces = max(1, rows * cols * 4 // STAGE_BYTES)
        step = rows // pieces
        for q in range(pieces):
            src = lambda ref, me, q=q, step=step: ref.at[pl.ds(q * step, step)]
            place = lambda ref, sender, layer=layer, q=q, step=step, rows=rows: ref.at[
                layer, pl.ds(sender[2] * rows + q * step, step)]
            jobs.append((len(kept) + t, src, [(oi, place, None), (oi, place, "c")]))
    full = staged_push(name + "_share", kept + totals, outs, jobs, n_alias=len(kept))
    return {**into, **dict(zip(names, full))}


STAGE_BYTES = 1024 * 1024


def add_rows(a, theirs, core, name):
    rows, cols = a.shape
    half = rows // 2
    tr = _row_tile(half, 256)
    nb = half // tr

    def body(core_ref, a_ref, t_ref, o_ref):
        o_ref[...] = a_ref[...] + t_ref[...]

    return pl.pallas_call(
        body, name=name,
        grid_spec=pltpu.PrefetchScalarGridSpec(
            num_scalar_prefetch=1, grid=(nb,),
            in_specs=[pl.BlockSpec((tr, cols), lambda i, c: (c[0] * nb + i, 0)),
                      pl.BlockSpec((tr, cols), lambda i, c: (i, 0))],
            out_specs=pl.BlockSpec((tr, cols), lambda i, c: (i, 0))),
        out_shape=_sds((half, cols), F32),
        compiler_params=_params("arbitrary"),
    )(core, a, theirs)


def sum_chips(mine, theirs, chip, name):
    _, half, cols = mine.shape
    tr = _row_tile(half, 256)

    def body(chip_ref, m_ref, a_ref, b_ref, c_ref, o_ref):
        o_ref[...] = ((m_ref[...].astype(F32) + a_ref[...].astype(F32))
                      + b_ref[...].astype(F32)) + c_ref[...].astype(F32)

    got = lambda f: pl.BlockSpec((None, tr, cols), lambda i, ch: (f, i, 0))
    return pl.pallas_call(
        body, name=name,
        grid_spec=pltpu.PrefetchScalarGridSpec(
            num_scalar_prefetch=1, grid=(half // tr,),
            in_specs=[pl.BlockSpec((None, tr, cols), lambda i, ch: (ch[0], i, 0)), got(0), got(1), got(2)],
            out_specs=pl.BlockSpec((tr, cols), lambda i, ch: (i, 0))),
        out_shape=_sds((half, cols), F32),
        compiler_params=_params("arbitrary"),
    )(chip, mine, theirs, theirs, theirs)


def _tok(t):
    return t.transpose(1, 0, 2).reshape(t.shape[1], t.shape[0] * t.shape[2])


def _heads(t):
    return t.reshape(t.shape[0], t.shape[1] // HEAD_DIM, HEAD_DIM).transpose(1, 0, 2)


def _tile2(vec):
    return jnp.tile(vec.reshape(1, HEAD_DIM), (1, 2))


REST = ("att_w_in", "att_w_out", "sgu_w_in", "sgu_w_out", "mlp_w1", "mlp_w2")
LAST_GROUP = (("att_w_in", 0),)
LATE_SMALL = ("att_norm", "att_sink", "att_qnorm", "att_knorm")


def local_step(x, target, first, rest_shards, rep, full_shapes):
    s_len, d = x.shape
    tabs = rope_tables(s_len)
    depth = rep["mlp_norm"].shape[0]
    row = lambda a: a.reshape(1, -1)
    saved = []
    h = x
    gw = {"att_w_in": [first]}

    def wl(name, idx):
        return (gw[name][idx], 0) if name == "att_w_in" else (gw[name], idx)

    for layer in range(depth):
        i = layer // 2
        tag = f"l{layer}"
        if layer % 2 == 0:
            hn, proj = norm_mm(h, row(rep["att_norm"][i]), *wl("att_w_in", i), F32, tag + "_att_proj")
            qkv_t, kv = prep_fwd(proj, tabs, _tile2(rep["att_qnorm"][i]), _tile2(rep["att_knorm"][i]),
                                 tag + "_att_prep")
            kv_tok = _heads(kv)
            oa, lse_a = flash_fwd_t(qkv_t, kv_tok, QA_COL // GROUP_W, 0, (KA_COL + LANES) // HEAD_DIM,
                                    rep["att_sink"][i], True, tag + "_win_fwd")
            plan = gather_halves_plan(rest_shards) if layer == 0 else None
            ob, lse_b, *got = flash_fwd_t(qkv_t, kv_tok, QB_COL // GROUP_W, 4, (KB_COL + LANES) // HEAD_DIM,
                                          None, False, tag + "_grid_fwd", comm=plan)
            if layer == 0:
                rest = dict(zip(REST, gather_halves_fill(got, rest_shards, "gather_rest_fill")))
                gw["att_w_in"].append(rest.pop("att_w_in"))
                gw.update(rest)
            out = mm_res_t([oa, ob], *wl("att_w_out", i), h, tag + "_att_out")
            mix_saved = (h, hn, proj, qkv_t, kv_tok, oa, ob, lse_a, lse_b)
        else:
            hn, zpre = norm_mm(h, row(rep["sgu_norm"][i]), *wl("sgu_w_in", i), F32, tag + "_sgu_in")
            ws = rep["sgu_w_s"][i].astype(BF16)
            bsb = jnp.broadcast_to(rep["sgu_b_s"][i][:, :, None], (SGU_GROUPS, SGU_CHUNK, LANES))
            y = sgu_mid_fwd(zpre, row(rep["sgu_ln_g"][i]), row(rep["sgu_ln_b"][i]), ws, bsb, tag + "_sgu_mid")
            out = mm_res(y, *wl("sgu_w_out", i), h, tag + "_sgu_out")
            mix_saved = (h, hn, zpre, y, ws, bsb)
        hm, a = norm_mm(out, row(rep["mlp_norm"][layer]), *wl("mlp_w1", layer), BF16, tag + "_mlp_up")
        nxt = mm_res(a, *wl("mlp_w2", layer), out, tag + "_mlp_down", relu2=True)
        saved.append((mix_saved, (out, hm, a)))
        h = nxt
    dh, dhb, d_final, loss_tile = loss_head(h, row(rep["final_norm"]), target, "loss_head")
    big, tags = [], []
    small = {k: [jnp.zeros(v.shape[1:], F32)] * v.shape[0] for k, v in rep.items() if k != "final_norm"}
    small["final_norm"] = d_final.reshape(-1)
    stacked = lambda: [small[n] if n == "final_norm" else jnp.stack(small[n]) for n in SMALL]
    for layer in reversed(range(depth)):
        i = layer // 2
        tag = f"l{layer}"
        mix_saved, (xin, hm, a) = saved[layer]
        da = mm_nt_relu2_bwd(dhb, *wl("mlp_w2", layer), a, tag + "_mlp_down_bwd")
        big.append(dw_mm(a, dhb, tag + "_mlp_dw2", col_sharded=False, relu2=True))
        tags.append(("mlp_w2", layer))
        big.append(dw_mm(hm, da, tag + "_mlp_dw1", col_sharded=True))
        tags.append(("mlp_w1", layer))
        dh, dhb, dg = dx_norm(da, *wl("mlp_w1", layer), xin, row(rep["mlp_norm"][layer]), dh, tag + "_mlp_up_bwd")
        small["mlp_norm"][layer] = dg.reshape(-1)
        if layer % 2 == 0:
            xin, hn, proj, qkv_t, kv_tok, oa, ob, lse_a, lse_b = mix_saved
            do_t = mm_nt(dhb, *wl("att_w_out", i), tag + "_att_out_bwd", transposed=True)
            big.append(dw_nn([oa, ob], dhb, tag + "_att_dwout"))
            tags.append(("att_w_out", i))
            dqa, dka, dva, dsink = flash_bwd_t(qkv_t, kv_tok, oa, do_t, lse_a, QA_COL // GROUP_W, 0, 2,
                                               KA_COL // HEAD_DIM, 0, rep["att_sink"][i], True, tag + "_win_bwd")
            plan = None
            if layer == 0:
                early = [k for k, t in enumerate(tags) if t not in LAST_GROUP]
                early_pack = _pack(stacked() + [loss_tile[0, :1]])
                plan = add_to_all(reduce_plan([big[k] for k in early]), early_pack)
            dqb, dkb, dvb, *got = flash_bwd_t(qkv_t, kv_tok, ob, do_t, lse_b, QB_COL // GROUP_W, 4, 6,
                                              KB_COL // HEAD_DIM, 2, None, False, tag + "_grid_bwd", comm=plan)
            if layer == 0:
                grads = reduce_finish([big[k] for k in early], got[:-1], [tags[k] for k in early], full_shapes,
                                      {}, "grads1")
                early_sum = sum_devices(early_pack, got[-1], "sum_small")
            qg, kg = _tile2(rep["att_qnorm"][i]), _tile2(rep["att_knorm"][i])
            dproj, dqg, dkg = prep_bwd(proj, dqa, _tok(dka), _tok(dva), dqb, _tok(dkb), _tok(dvb),
                                       tabs, qg, kg, tag + "_att_prep_bwd")
            big.append(dw_mm(hn, dproj, tag + "_att_dwin", col_sharded=True))
            tags.append(("att_w_in", i))
            dh, dhb, dg = dx_norm(dproj, *wl("att_w_in", i), xin, row(rep["att_norm"][i]), dh, tag + "_att_proj_bwd")
            small["att_norm"][i] = dg.reshape(-1)
            small["att_sink"][i] = dsink[:, 0, :GROUP].reshape(-1)
            small["att_qnorm"][i] = dqg[0, :HEAD_DIM] + dqg[0, HEAD_DIM:]
            small["att_knorm"][i] = dkg[0, :HEAD_DIM] + dkg[0, HEAD_DIM:]
        else:
            xin, hn, zpre, y, ws, bsb = mix_saved
            dy = mm_nt(dhb, *wl("sgu_w_out", i), tag + "_sgu_out_bwd")
            big.append(dw_mm(y, dhb, tag + "_sgu_dwout", col_sharded=False))
            tags.append(("sgu_w_out", i))
            wst = ws.transpose(0, 2, 1)
            dz, dws, dbs, dlg, dlb = sgu_mid_bwd(zpre, dy, row(rep["sgu_ln_g"][i]), row(rep["sgu_ln_b"][i]),
                                                 ws, wst, bsb, tag + "_sgu_mid_bwd")
            big.append(dw_mm(hn, dz, tag + "_sgu_dwin", col_sharded=True))
            tags.append(("sgu_w_in", i))
            dh, dhb, dg = dx_norm(dz, *wl("sgu_w_in", i), xin, row(rep["sgu_norm"][i]), dh, tag + "_sgu_in_bwd")
            small["sgu_norm"][i] = dg.reshape(-1)
            small["sgu_ln_g"][i] = dlg.reshape(-1)
            small["sgu_ln_b"][i] = dlb.reshape(-1)
            small["sgu_w_s"][i] = dws
            small["sgu_b_s"][i] = dbs[:, :, 0]
    late = [k for k, t in enumerate(tags) if t in LAST_GROUP]
    late_pack = _pack([small[n][0] for n in LATE_SMALL])
    got = exchange("grads2_scatter", *add_to_all(reduce_plan([big[k] for k in late]), late_pack), [])
    grads = reduce_finish([big[k] for k in late], got[:-1], [tags[k] for k in late], full_shapes, grads, "grads2")
    late_sum = sum_devices(late_pack, got[-1], "sum_small_late")
    shapes = [a.shape for a in stacked()]
    *small_g, loss = _unpack(early_sum, shapes + [()])
    small_g = dict(zip(SMALL, small_g))
    for n, g in zip(LATE_SMALL, _unpack(late_sum, [small[n][0].shape for n in LATE_SMALL])):
        small_g[n] = small_g[n].at[0].add(g)
    return loss, dh, grads, small_g


BIG = ("att_w_in", "att_w_out", "sgu_w_in", "sgu_w_out", "mlp_w1", "mlp_w2")
SHARDED_VEC = ("sgu_norm", "sgu_ln_g", "sgu_ln_b")
REPLICATED = ("att_norm", "att_sink", "att_qnorm", "att_knorm", "sgu_w_s", "sgu_b_s", "mlp_norm", "final_norm")
WEIGHTS = ("att_norm", "att_w_in", "att_sink", "att_qnorm", "att_knorm", "att_w_out", "sgu_norm", "sgu_w_in",
           "sgu_ln_g", "sgu_ln_b", "sgu_w_s", "sgu_b_s", "sgu_w_out", "mlp_norm", "mlp_w1", "mlp_w2", "final_norm")
SMALL = tuple(n for n in WEIGHTS if n not in BIG)
PACK_ALIGN = 8 * LANES


def _pack(arrays):
    flat = jnp.concatenate([a.reshape(-1) for a in arrays])
    pad = -flat.shape[0] % PACK_ALIGN
    return jnp.pad(flat, (0, pad)).reshape(-1, LANES)


def _unpack(flat2d, shapes):
    flat = flat2d.reshape(-1)
    out, off = [], 0
    for shape in shapes:
        size = int(np.prod(shape))
        out.append(flat[off:off + size].reshape(shape))
        off += size
    return out


def kernel(x, att_norm, att_w_in, att_sink, att_qnorm, att_knorm, att_w_out, sgu_norm, sgu_w_in, sgu_ln_g, sgu_ln_b, sgu_w_s, sgu_b_s, sgu_w_out, mlp_norm, mlp_w1, mlp_w2, final_norm, loss_target, m_att_norm, m_att_w_in, m_att_sink, m_att_qnorm, m_att_knorm, m_att_w_out, m_sgu_norm, m_sgu_w_in, m_sgu_ln_g, m_sgu_ln_b, m_sgu_w_s, m_sgu_b_s, m_sgu_w_out, m_mlp_norm, m_mlp_w1, m_mlp_w2, m_final_norm, v_att_norm, v_att_w_in, v_att_sink, v_att_qnorm, v_att_knorm, v_att_w_out, v_sgu_norm, v_sgu_w_in, v_sgu_ln_g, v_sgu_ln_b, v_sgu_w_s, v_sgu_b_s, v_sgu_w_out, v_mlp_norm, v_mlp_w1, v_mlp_w2, v_final_norm):
    w = dict(att_norm=att_norm, att_w_in=att_w_in, att_sink=att_sink, att_qnorm=att_qnorm, att_knorm=att_knorm,
             att_w_out=att_w_out, sgu_norm=sgu_norm, sgu_w_in=sgu_w_in, sgu_ln_g=sgu_ln_g, sgu_ln_b=sgu_ln_b,
             sgu_w_s=sgu_w_s, sgu_b_s=sgu_b_s, sgu_w_out=sgu_w_out, mlp_norm=mlp_norm, mlp_w1=mlp_w1,
             mlp_w2=mlp_w2, final_norm=final_norm)
    m = dict(att_norm=m_att_norm, att_w_in=m_att_w_in, att_sink=m_att_sink, att_qnorm=m_att_qnorm,
             att_knorm=m_att_knorm, att_w_out=m_att_w_out, sgu_norm=m_sgu_norm, sgu_w_in=m_sgu_w_in,
             sgu_ln_g=m_sgu_ln_g, sgu_ln_b=m_sgu_ln_b, sgu_w_s=m_sgu_w_s, sgu_b_s=m_sgu_b_s,
             sgu_w_out=m_sgu_w_out, mlp_norm=m_mlp_norm, mlp_w1=m_mlp_w1, mlp_w2=m_mlp_w2,
             final_norm=m_final_norm)
    v = dict(att_norm=v_att_norm, att_w_in=v_att_w_in, att_sink=v_att_sink, att_qnorm=v_att_qnorm,
             att_knorm=v_att_knorm, att_w_out=v_att_w_out, sgu_norm=v_sgu_norm, sgu_w_in=v_sgu_w_in,
             sgu_ln_g=v_sgu_ln_g, sgu_ln_b=v_sgu_ln_b, sgu_w_s=v_sgu_w_s, sgu_b_s=v_sgu_b_s,
             sgu_w_out=v_sgu_w_out, mlp_norm=v_mlp_norm, mlp_w1=v_mlp_w1, mlp_w2=v_mlp_w2,
             final_norm=v_final_norm)
    chip = 2 * lax.axis_index("x") + lax.axis_index("y")

    vecs = jnp.stack([w[n] for n in SHARDED_VEC])
    wb = {n: w[n].astype(BF16) for n in BIG}
    first, vec_all = gather_whole([wb["att_w_in"][0:1], vecs], "gather_first")
    rest_shards = [wb[n][1:2] if n == "att_w_in" else wb[n] for n in REST]
    vec_full = vec_all.transpose(1, 2, 0, 3).reshape(vecs.shape[0], vecs.shape[1], -1)
    rep = {n: w[n] for n in REPLICATED}
    rep.update({n: vec_full[k] for k, n in enumerate(SHARDED_VEC)})

    loss, grad_x, grads, small_g = local_step(x[0], loss_target[0], first, rest_shards, rep,
                                              {n: w[n].shape for n in BIG})
    width = w["sgu_norm"].shape[1]
    for n in SHARDED_VEC:
        small_g[n] = lax.dynamic_slice_in_dim(small_g[n], chip * width, width, axis=1)
    grads.update(small_g)
    for n in BIG:
        grads[n] = grads[n].reshape(w[n].shape)

    delta, new_m, new_v = {}, {}, {}
    for n in WEIGHTS:
        shape = w[n].shape
        two_d = (lambda a: a.reshape(1, -1)) if len(shape) == 1 else (lambda a: a)
        dn, mn, vn = adamw(two_d(w[n]), two_d(grads[n]), two_d(m[n]), two_d(v[n]), "adamw_" + n)
        delta[n], new_m[n], new_v[n] = dn.reshape(shape), mn.reshape(shape), vn.reshape(shape)
    return (loss, grad_x[None], *[grads[n] for n in WEIGHTS], *[delta[n] for n in WEIGHTS],
            *[new_m[n] for n in WEIGHTS], *[new_v[n] for n in WEIGHTS])
```

```python
import functools

import numpy as np
import jax
import jax.numpy as jnp
from jax import lax
from jax.experimental import pallas as pl
from jax.experimental.pallas import tpu as pltpu

F32 = jnp.float32
BF16 = jnp.bfloat16
MESH = pl.DeviceIdType.MESH

EPS = 1e-6
HEAD_DIM = 64
BLOCK = 128
GRID_W = 64
ROPE_THETA = 10000.0
N_CHIPS = 4
LANES = 128
V7X_VMEM_BYTES = 64 * 1024 * 1024
VMEM_LIMIT = V7X_VMEM_BYTES - 8 * 1024 * 1024

ADAM_LR = 0.001
ADAM_B1 = 0.9
ADAM_B2 = 0.999
ADAM_EPS = 1e-08
ADAM_WD = 0.01
ADAM_STEP = 10

NT_DIMS = (((1,), (1,)), ((), ()))
TN_DIMS = (((0,), (0,)), ((), ()))


def _params(*sem):
    return pltpu.CompilerParams(dimension_semantics=sem, vmem_limit_bytes=VMEM_LIMIT)


def _sds(shape, dtype):
    return jax.ShapeDtypeStruct(tuple(shape), dtype)


def _row_tile(rows, want):
    t = min(rows, want)
    assert rows % t == 0, (rows, want)
    return t


def norm_mm(x, g, w4, layer, out_dtype, name):
    s_len, d = x.shape
    ns = w4.shape[-1]
    tm = _row_tile(s_len, 512)

    def body(x_ref, g_ref, w_ref, h_ref, y_ref):
        xf = x_ref[...]
        r = lax.rsqrt(jnp.mean(xf * xf, axis=-1, keepdims=True) + EPS)
        h = ((xf * r) * g_ref[...]).astype(BF16)
        h_ref[...] = h
        for s in range(N_CHIPS):
            y_ref[:, s * ns:(s + 1) * ns] = jnp.dot(h, w_ref[s], preferred_element_type=F32).astype(y_ref.dtype)

    return pl.pallas_call(
        body, name=name, grid=(s_len // tm,),
        in_specs=[pl.BlockSpec((tm, d), lambda i: (i, 0)),
                  pl.BlockSpec((1, d), lambda i: (0, 0)),
                  pl.BlockSpec((N_CHIPS, None, d, ns), lambda i: (0, layer, 0, 0))],
        out_specs=[pl.BlockSpec((tm, d), lambda i: (i, 0)),
                   pl.BlockSpec((tm, N_CHIPS * ns), lambda i: (i, 0))],
        out_shape=[_sds((s_len, d), BF16), _sds((s_len, N_CHIPS * ns), out_dtype)],
        compiler_params=_params("arbitrary"),
    )(x, g, w4)


def mm_res(a, w4, layer, res, name, relu2=False):
    s_len, k = a.shape
    kq, n = w4.shape[-2:]
    assert kq * N_CHIPS == k
    tm = _row_tile(s_len, 256 if k > 1024 else 512)

    def body(a_ref, w0, w1, w2, w3, r_ref, o_ref):
        acc = r_ref[...]
        for s, w_ref in enumerate((w0, w1, w2, w3)):
            av = a_ref[:, s * kq:(s + 1) * kq]
            if relu2:
                t = jnp.maximum(av.astype(F32), 0.0)
                av = (t * t).astype(BF16)
            acc = acc + jnp.dot(av, w_ref[...], preferred_element_type=F32)
        o_ref[...] = acc

    def wspec(s):
        return pl.BlockSpec((None, None, kq, n), lambda i: (s, layer, 0, 0))

    return pl.pallas_call(
        body, name=name, grid=(s_len // tm,),
        in_specs=[pl.BlockSpec((tm, k), lambda i: (i, 0)), wspec(0), wspec(1), wspec(2), wspec(3),
                  pl.BlockSpec((tm, n), lambda i: (i, 0))],
        out_specs=pl.BlockSpec((tm, n), lambda i: (i, 0)),
        out_shape=_sds((s_len, n), F32),
        compiler_params=_params("arbitrary"),
    )(a, w4, w4, w4, w4, res)


def mm_res_t(pieces, w4, layer, res, name):
    s_len = res.shape[0]
    kq, n = w4.shape[-2:]
    rows = pieces[0].shape[0]
    assert rows % kq == 0 and rows * len(pieces) == kq * N_CHIPS
    tm = _row_tile(s_len, 512)
    n_p = len(pieces)

    def body(*refs):
        p_refs, w_refs, (r_ref, o_ref) = refs[:n_p], refs[n_p:n_p + N_CHIPS], refs[n_p + N_CHIPS:]
        acc = r_ref[...]
        for s in range(N_CHIPS):
            p, off = divmod(s * kq, rows)
            acc = acc + lax.dot_general(p_refs[p][off:off + kq, :], w_refs[s][...], TN_DIMS,
                                        preferred_element_type=F32)
        o_ref[...] = acc

    def wspec(s):
        return pl.BlockSpec((None, None, kq, n), lambda i: (s, layer, 0, 0))

    return pl.pallas_call(
        body, name=name, grid=(s_len // tm,),
        in_specs=[pl.BlockSpec((rows, tm), lambda i: (0, i))] * n_p + [wspec(s) for s in range(N_CHIPS)]
        + [pl.BlockSpec((tm, n), lambda i: (i, 0))],
        out_specs=pl.BlockSpec((tm, n), lambda i: (i, 0)),
        out_shape=_sds((s_len, n), F32),
        compiler_params=_params("arbitrary"),
    )(*pieces, w4, w4, w4, w4, res)


def dw_nn(pieces, b, name):
    s_len, n = b.shape
    rows = pieces[0].shape[0]
    n_p = len(pieces)
    k = rows * n_p
    ts = _row_tile(s_len, 2048)
    n_s = s_len // ts

    def body(*refs):
        p_refs, (b_ref, o_ref, acc_ref) = refs[:n_p], refs[n_p:]
        s = pl.program_id(0)
        bv = b_ref[...]
        for p in range(n_p):
            part = jnp.dot(p_refs[p][...], bv, preferred_element_type=F32)
            at = slice(p * rows, (p + 1) * rows)
            if n_s == 1:
                o_ref[at, :] = part.astype(BF16)
                continue

            @pl.when(s == 0)
            def _():
                acc_ref[at, :] = part

            @pl.when((s > 0) & (s < n_s - 1))
            def _():
                acc_ref[at, :] += part

            @pl.when(s == n_s - 1)
            def _():
                o_ref[at, :] = (acc_ref[at, :] + part).astype(BF16)

    out = pl.pallas_call(
        body, name=name, grid=(n_s,),
        in_specs=[pl.BlockSpec((rows, ts), lambda s: (0, s))] * n_p + [pl.BlockSpec((ts, n), lambda s: (s, 0))],
        out_specs=pl.BlockSpec((k, n), lambda s: (0, 0)), out_shape=_sds((k, n), BF16),
        scratch_shapes=[pltpu.VMEM((k, n), F32)],
        compiler_params=_params("arbitrary"),
    )(*pieces, b)
    return out.reshape(N_CHIPS, k // N_CHIPS, n)


def mm_nt(dy, w4, layer, name, transposed=False):
    s_len, n = dy.shape
    mq = w4.shape[-2]
    tm = _row_tile(s_len, 512)

    def body(d_ref, w0, w1, w2, w3, o_ref):
        dv = d_ref[...]
        for s, w_ref in enumerate((w0, w1, w2, w3)):
            if transposed:
                o_ref[s * mq:(s + 1) * mq, :] = lax.dot_general(
                    w_ref[...], dv, NT_DIMS, preferred_element_type=F32).astype(BF16)
            else:
                o_ref[:, s * mq:(s + 1) * mq] = lax.dot_general(
                    dv, w_ref[...], NT_DIMS, preferred_element_type=F32).astype(BF16)

    def wspec(s):
        return pl.BlockSpec((None, None, mq, n), lambda i: (s, layer, 0, 0))

    m = N_CHIPS * mq
    return pl.pallas_call(
        body, name=name, grid=(s_len // tm,),
        in_specs=[pl.BlockSpec((tm, n), lambda i: (i, 0)), wspec(0), wspec(1), wspec(2), wspec(3)],
        out_specs=pl.BlockSpec((m, tm), lambda i: (0, i)) if transposed else pl.BlockSpec((tm, m), lambda i: (i, 0)),
        out_shape=_sds((m, s_len) if transposed else (s_len, m), BF16),
        compiler_params=_params("arbitrary"),
    )(dy, w4, w4, w4, w4)


def mm_nt_relu2_bwd(dy, w4, layer, a, name):
    s_len, n = dy.shape
    mq = w4.shape[-2]
    tm = _row_tile(s_len, 512)

    def body(d_ref, w_ref, a_ref, o_ref):
        dv = d_ref[...]
        for s in range(N_CHIPS):
            cols = slice(s * mq, (s + 1) * mq)
            dz = lax.dot_general(dv, w_ref[s], NT_DIMS, preferred_element_type=F32)
            o_ref[:, cols] = (dz * (2.0 * jnp.maximum(a_ref[:, cols].astype(F32), 0.0))).astype(BF16)

    return pl.pallas_call(
        body, name=name, grid=(s_len // tm,),
        in_specs=[pl.BlockSpec((tm, n), lambda i: (i, 0)),
                  pl.BlockSpec((N_CHIPS, None, mq, n), lambda i: (0, layer, 0, 0)),
                  pl.BlockSpec((tm, N_CHIPS * mq), lambda i: (i, 0))],
        out_specs=pl.BlockSpec((tm, N_CHIPS * mq), lambda i: (i, 0)),
        out_shape=_sds((s_len, N_CHIPS * mq), BF16),
        compiler_params=_params("arbitrary"),
    )(dy, w4, a)


def dx_norm(dy, w4, layer, x, g, dres, name):
    s_len, d = x.shape
    ns = w4.shape[-1]
    tm = _row_tile(s_len, 512)

    def body(dy_ref, w_ref, x_ref, g_ref, dr_ref, dx_ref, dxb_ref, dg_ref):
        i = pl.program_id(0)
        dh = lax.dot_general(dy_ref[:, 0:ns], w_ref[0], NT_DIMS, preferred_element_type=F32)
        for s in range(1, N_CHIPS):
            dh = dh + lax.dot_general(dy_ref[:, s * ns:(s + 1) * ns], w_ref[s], NT_DIMS,
                                      preferred_element_type=F32)
        xf = x_ref[...]
        r = lax.rsqrt(jnp.mean(xf * xf, axis=-1, keepdims=True) + EPS)
        xhat = xf * r
        dg_part = jnp.sum(dh * xhat, axis=0, keepdims=True)

        @pl.when(i == 0)
        def _():
            dg_ref[...] = dg_part

        @pl.when(i > 0)
        def _():
            dg_ref[...] += dg_part

        dxh = dh * g_ref[...]
        dx = dr_ref[...] + r * (dxh - xhat * jnp.mean(dxh * xhat, axis=-1, keepdims=True))
        dx_ref[...] = dx
        dxb_ref[...] = dx.astype(BF16)

    row = pl.BlockSpec((tm, d), lambda i: (i, 0))
    vec = pl.BlockSpec((1, d), lambda i: (0, 0))
    return pl.pallas_call(
        body, name=name, grid=(s_len // tm,),
        in_specs=[pl.BlockSpec((tm, N_CHIPS * ns), lambda i: (i, 0)),
                  pl.BlockSpec((N_CHIPS, None, d, ns), lambda i: (0, layer, 0, 0)), row, vec, row],
        out_specs=[row, row, vec],
        out_shape=[_sds((s_len, d), F32), _sds((s_len, d), BF16), _sds((1, d), F32)],
        compiler_params=_params("arbitrary"),
    )(dy, w4, x, g, dres)


def dw_mm(a, b, name, col_sharded, relu2=False):
    s_len, k = a.shape
    n = b.shape[1]
    ts = _row_tile(s_len, 2048)
    tk = min(k, 1024)
    tn = n // N_CHIPS if col_sharded else min(n, 1024)
    n_s = s_len // ts

    def body(a_ref, b_ref, o_ref, acc_ref):
        s = pl.program_id(2)
        av = a_ref[...]
        if relu2:
            t = jnp.maximum(av.astype(F32), 0.0)
            av = (t * t).astype(BF16)
        part = lax.dot_general(av, b_ref[...], TN_DIMS, preferred_element_type=F32)
        if n_s == 1:
            o_ref[...] = part.astype(BF16)
            return

        @pl.when(s == 0)
        def _():
            acc_ref[...] = part

        @pl.when((s > 0) & (s < n_s - 1))
        def _():
            acc_ref[...] += part

        @pl.when(s == n_s - 1)
        def _():
            o_ref[...] = (acc_ref[...] + part).astype(BF16)

    if col_sharded:
        out_shape = _sds((N_CHIPS, k, tn), BF16)
        out_spec = pl.BlockSpec((None, tk, tn), lambda i, j, s: (j, i, 0))
    else:
        out_shape = _sds((N_CHIPS, k // N_CHIPS, n), BF16)
        rows_per = k // N_CHIPS
        assert tk % rows_per == 0 or rows_per % tk == 0
        if tk >= rows_per:
            out_shape = _sds((k, n), BF16)
            out_spec = pl.BlockSpec((tk, tn), lambda i, j, s: (i, j))
        else:
            per = rows_per // tk
            out_spec = pl.BlockSpec((None, tk, tn), lambda i, j, s: (i // per, i % per, j))

    out = pl.pallas_call(
        body, name=name, grid=(k // tk, n // tn, n_s),
        in_specs=[pl.BlockSpec((ts, tk), lambda i, j, s: (s, i)),
                  pl.BlockSpec((ts, tn), lambda i, j, s: (s, j))],
        out_specs=out_spec, out_shape=out_shape,
        scratch_shapes=[pltpu.VMEM((tk, tn), F32)],
        compiler_params=_params("arbitrary", "arbitrary", "arbitrary"),
    )(a, b)
    if not col_sharded:
        out = out.reshape(N_CHIPS, k // N_CHIPS, n)
    return out


def ew(fn, ins, out_dtypes, name, tile_rows=256):
    rows, cols = ins[0].shape
    for a in ins:
        assert a.shape == (rows, cols), (name, a.shape, rows, cols)
    tr = rows if (rows <= tile_rows or rows % tile_rows) else tile_rows
    n_in = len(ins)

    def body(*refs):
        outs = fn(*[r[...] for r in refs[:n_in]])
        for o_ref, val in zip(refs[n_in:], outs):
            o_ref[...] = val.astype(o_ref.dtype)

    spec = pl.BlockSpec((tr, cols), lambda i: (i, 0))
    return pl.pallas_call(
        body, name=name, grid=(rows // tr,),
        in_specs=[spec] * n_in, out_specs=[spec] * len(out_dtypes),
        out_shape=[_sds((rows, cols), dt) for dt in out_dtypes],
        compiler_params=_params("arbitrary"),
    )(*ins)


def adamw(w, g, m, v, name):
    shape = w.shape
    cols = shape[-1]
    two_d = lambda a: a.reshape(-1, cols)

    def fn(wv, gv, mv, vv):
        m_new = ADAM_B1 * mv + (1.0 - ADAM_B1) * gv
        v_new = ADAM_B2 * vv + (1.0 - ADAM_B2) * (gv * gv)
        m_hat = m_new / (1.0 - ADAM_B1 ** ADAM_STEP)
        v_hat = v_new / (1.0 - ADAM_B2 ** ADAM_STEP)
        delta = -ADAM_LR * (m_hat / (jnp.sqrt(v_hat) + ADAM_EPS) + ADAM_WD * wv)
        return delta, m_new, v_new

    d, mn, vn = ew(fn, [two_d(w), two_d(g), two_d(m), two_d(v)], [F32, F32, F32], name)
    return d.reshape(shape), mn.reshape(shape), vn.reshape(shape)


def rope_tables(s_len):
    def angles(pos, dim):
        freqs = ROPE_THETA ** (-jnp.arange(0, dim, 2, dtype=F32) / dim)
        ang = pos.astype(F32)[:, None] * freqs[None, :]
        return jnp.cos(ang), jnp.sin(ang)

    pos = jnp.arange(s_len)
    rows = s_len // GRID_W
    row_idx = jnp.repeat(jnp.arange(rows), GRID_W)
    col_idx = jnp.tile(jnp.arange(GRID_W), rows)
    c1, s1 = angles(pos, HEAD_DIM)
    cr, sr = angles(row_idx, HEAD_DIM // 2)
    cc, sc = angles(col_idx, HEAD_DIM // 2)
    cos1 = jnp.tile(jnp.concatenate([c1, c1], -1), (1, 2))
    sin1 = jnp.tile(jnp.concatenate([-s1, s1], -1), (1, 2))
    cos2 = jnp.tile(jnp.concatenate([cr, cr, cc, cc], -1), (1, 2))
    sin2 = jnp.tile(jnp.concatenate([-sr, sr, -sc, sc], -1), (1, 2))
    return cos1, sin1, cos2, sin2


def _lane_iota(rows):
    return lax.broadcasted_iota(jnp.int32, (rows, LANES), 1)


def _swap(x, dist, lane):
    return jnp.where((lane & dist) != 0, pltpu.roll(x, dist, 1), pltpu.roll(x, LANES - dist, 1))


def _head_ones():
    r = lax.broadcasted_iota(jnp.int32, (LANES, LANES), 0) // HEAD_DIM
    c = lax.broadcasted_iota(jnp.int32, (LANES, LANES), 1) // HEAD_DIM
    return (r == c).astype(BF16)


def _head_sum(t, ones):
    hi = t.astype(BF16)
    lo = (t - hi.astype(F32)).astype(BF16)
    return (jnp.dot(hi, ones, preferred_element_type=F32) + jnp.dot(lo, ones, preferred_element_type=F32))


Q_SCALE = HEAD_DIM ** -0.5
LOG2E = 1.4426950408889634
LN2 = 0.6931471805599453
CHUNK_KIND = ["qa"] * 4 + ["ka", "va"] + ["qb"] * 4 + ["kb", "vb"]
QA_COL, KA_COL, QB_COL, KB_COL = 0, 512, 768, 1280


def prep_fwd(proj, tabs, qn_g, kn_g, name):
    s_len, width = proj.shape
    ts = _row_tile(s_len, 512)
    cos1, sin1, cos2, sin2 = tabs

    def body(p_ref, c1_ref, s1_ref, c2_ref, s2_ref, qg_ref, kg_ref, o_ref, kv_ref):
        lane = _lane_iota(ts)
        ones = _head_ones()
        c1, s1, c2, s2 = c1_ref[...], s1_ref[...], c2_ref[...], s2_ref[...]
        n_kv = 0
        for cb, kind in enumerate(CHUNK_KIND):
            x = p_ref[:, cb * LANES:(cb + 1) * LANES]
            if kind in ("qa", "ka"):
                y = x * c1 + _swap(x, 32, lane) * s1
            elif kind in ("qb", "kb"):
                gain = qg_ref[...] if kind == "qb" else kg_ref[...]
                ms = _head_sum(x * x, ones) * (1.0 / HEAD_DIM)
                xn = (x * lax.rsqrt(ms + EPS)) * gain
                y = xn * c2 + _swap(xn, 16, lane) * s2
            else:
                y = x
            if kind in ("qa", "qb"):
                y = y * (Q_SCALE * LOG2E)
            else:
                kv_ref[:, n_kv * LANES:(n_kv + 1) * LANES] = y.astype(BF16)
                n_kv += 1
            o_ref[cb * LANES:(cb + 1) * LANES, :] = y.T.astype(BF16)

    tab = pl.BlockSpec((ts, LANES), lambda i: (i, 0))
    vec = pl.BlockSpec((1, LANES), lambda i: (0, 0))
    return pl.pallas_call(
        body, name=name, grid=(s_len // ts,),
        in_specs=[pl.BlockSpec((ts, width), lambda i: (i, 0)), tab, tab, tab, tab, vec, vec],
        out_specs=[pl.BlockSpec((width, ts), lambda i: (0, i)), pl.BlockSpec((ts, 4 * LANES), lambda i: (i, 0))],
        out_shape=[_sds((width, s_len), BF16), _sds((s_len, 4 * LANES), BF16)],
        compiler_params=_params("arbitrary"),
    )(proj, cos1, sin1, cos2, sin2, qn_g, kn_g)


def prep_bwd(proj, dqa, dka, dva, dqb, dkb, dvb, tabs, qn_g, kn_g, name):
    s_len, width = proj.shape
    ts = _row_tile(s_len, 256)
    cos1, sin1, cos2, sin2 = tabs

    def body(p_ref, dqa_ref, dka_ref, dva_ref, dqb_ref, dkb_ref, dvb_ref,
             c1_ref, s1_ref, c2_ref, s2_ref, qg_ref, kg_ref, o_ref, dqg_ref, dkg_ref):
        i = pl.program_id(0)
        lane = _lane_iota(ts)
        c1, s1, c2, s2 = c1_ref[...], s1_ref[...], c2_ref[...], s2_ref[...]

        def rope_t(dy, cos, sin, dist):
            return dy * cos + _swap(dy * sin, dist, lane)

        ones = _head_ones()

        def norm_bwd(dy, x, gain):
            r = lax.rsqrt(_head_sum(x * x, ones) * (1.0 / HEAD_DIM) + EPS)
            xhat = x * r
            dgain = jnp.sum(dy * xhat, axis=0, keepdims=True)
            dxh = dy * gain
            dx = r * (dxh - xhat * (_head_sum(dxh * xhat, ones) * (1.0 / HEAD_DIM)))
            return dx, dgain

        dqg = jnp.zeros((1, LANES), F32)
        dkg = jnp.zeros((1, LANES), F32)
        for cb, kind in enumerate(CHUNK_KIND):
            cols = slice(cb * LANES, (cb + 1) * LANES)
            if kind == "qa":
                dx = rope_t(dqa_ref[cols, :].T * Q_SCALE, c1, s1, 32)
            elif kind == "ka":
                dx = rope_t(dka_ref[...], c1, s1, 32)
            elif kind == "va":
                dx = dva_ref[...]
            elif kind == "qb":
                qcols = slice((cb - 6) * LANES, (cb - 5) * LANES)
                dy = rope_t(dqb_ref[qcols, :].T * Q_SCALE, c2, s2, 16)
                dx, dgain = norm_bwd(dy, p_ref[:, cols], qg_ref[...])
                dqg = dqg + dgain
            elif kind == "kb":
                dy = rope_t(dkb_ref[...], c2, s2, 16)
                dx, dgain = norm_bwd(dy, p_ref[:, cols], kg_ref[...])
                dkg = dkg + dgain
            else:
                dx = dvb_ref[...]
            o_ref[:, cols] = dx.astype(BF16)

        @pl.when(i == 0)
        def _():
            dqg_ref[...] = dqg
            dkg_ref[...] = dkg

        @pl.when(i > 0)
        def _():
            dqg_ref[...] += dqg
            dkg_ref[...] += dkg

    tab = pl.BlockSpec((ts, LANES), lambda i: (i, 0))
    vec = pl.BlockSpec((1, LANES), lambda i: (0, 0))

    def dq_spec(dq):
        per = dq.shape[2] // ts
        return pl.BlockSpec((None, 4 * LANES, ts), lambda i: (i // per, 0, i % per))

    return pl.pallas_call(
        body, name=name, grid=(s_len // ts,),
        in_specs=([pl.BlockSpec((ts, width), lambda i: (i, 0)), dq_spec(dqa), tab, tab, dq_spec(dqb), tab, tab]
                  + [tab] * 4 + [vec, vec]),
        out_specs=[pl.BlockSpec((ts, width), lambda i: (i, 0)), vec, vec],
        out_shape=[_sds((s_len, width), BF16), _sds((1, LANES), F32), _sds((1, LANES), F32)],
        compiler_params=_params("arbitrary"),
    )(proj, dqa, dka, dva, dqb, dkb, dvb, cos1, sin1, cos2, sin2, qn_g, kn_g)


NEG = -1e30
GROUP = 4
KV_HEADS = 2
GROUP_W = GROUP * HEAD_DIM
LSE_ROWS = 8
ONES_ROWS = 16


def _pos_mask_t(k_start, q_start, s_len, tk, tq):
    kpos = k_start + lax.broadcasted_iota(jnp.int32, (tk, tq), 0)
    qpos = q_start + lax.broadcasted_iota(jnp.int32, (tk, tq), 1)
    return (jnp.abs(kpos - qpos) <= BLOCK) & (kpos >= 0) & (kpos < s_len)


def flash_fwd_t(qkv_t, kv_tok, q_rb, k_i, v_rb, sink, window, name, comm=None):
    s_len = qkv_t.shape[1]
    if window:
        tq = _row_tile(s_len, 512)
        tk = 2 * BLOCK
        per = tq // tk
        assert per == 2, "the band parts below are written for query blocks of two key blocks"
        n_kv = per + 2
    else:
        tq, tk = _row_tile(s_len, 1024), _row_tile(s_len, 2048)
        n_kv = s_len // tk
    n_kb = s_len // tk
    n_i = s_len // tq
    c_ins, c_outs, c_remote = comm if comm else ([], [], [])
    n_main = 4 if window else 3

    def body(*refs):
        main, c_in_refs = refs[:n_main], refs[n_main:n_main + len(c_ins)]
        rest = refs[n_main + len(c_ins):]
        (o_ref, lse_ref), c_out_refs = rest[:2], rest[2:2 + len(c_outs)]
        m_sc, acc_sc = rest[2 + len(c_outs):4 + len(c_outs)]
        c_sems = rest[4 + len(c_outs):]
        if window:
            sink_ref, q_ref, k_ref, v_ref = main
        else:
            q_ref, k_ref, v_ref = main
        h, i, t = pl.program_id(0), pl.program_id(1), pl.program_id(2)
        if comm:
            @pl.when((h == 0) & (i == 0) & (t == 0))
            def _():
                _exchange_start(c_remote, c_in_refs, c_out_refs, *c_sems)

        @pl.when(t == 0)
        def _():
            for g in range(GROUP):
                acc_sc[g, 0:HEAD_DIM, :] = jnp.zeros((HEAD_DIM, tq), F32)
                if window:
                    m_sc[g] = jnp.full((1, tq), sink_ref[h * GROUP + g] * LOG2E, F32)
                    acc_sc[g, HEAD_DIM:, :] = jnp.ones((ONES_ROWS, tq), F32)
                else:
                    m_sc[g] = jnp.full((1, tq), NEG, F32)
                    acc_sc[g, HEAD_DIM:, :] = jnp.zeros((ONES_ROWS, tq), F32)

        def tile(k_lo, k_hi, q_lo, q_hi):
            ks, qs = slice(k_lo, k_hi), slice(q_lo, q_hi)
            k = k_ref[ks, :]
            v_t = jnp.concatenate([v_ref[:, ks], jnp.ones((ONES_ROWS, k_hi - k_lo), BF16)], axis=0)
            if window:
                mask = _pos_mask_t((i * per - 1 + t) * tk + k_lo, i * tq + q_lo, s_len, k_hi - k_lo, q_hi - q_lo)
            s_next = jnp.dot(k, q_ref[0:HEAD_DIM, qs], preferred_element_type=F32)
            for g in range(GROUP):
                s_t = s_next
                if g + 1 < GROUP:
                    s_next = jnp.dot(k, q_ref[(g + 1) * HEAD_DIM:(g + 2) * HEAD_DIM, qs],
                                     preferred_element_type=F32)
                if window:
                    s_t = jnp.where(mask, s_t, NEG)
                m_prev = m_sc[g, :, qs]
                m_new = jnp.maximum(m_prev, jnp.max(s_t, axis=0, keepdims=True))
                alpha = jnp.exp2(m_prev - m_new)
                p_t = jnp.exp2(s_t - m_new)
                acc_sc[g, :, qs] = alpha * acc_sc[g, :, qs] + jnp.dot(v_t, p_t.astype(BF16),
                                                                     preferred_element_type=F32)
                m_sc[g, :, qs] = m_new

        if window:
            for tt, part in enumerate([(tk - BLOCK, tk, 0, BLOCK), (0, tk, 0, tq - BLOCK),
                                       (0, tk, BLOCK, tq), (0, BLOCK, tq - BLOCK, tq)]):
                pl.when(t == tt)(functools.partial(tile, *part))
        else:
            tile(0, tk, 0, tq)

        @pl.when(t == n_kv - 1)
        def _():
            for g in range(GROUP):
                l = acc_sc[g, HEAD_DIM:HEAD_DIM + 1, :]
                o_ref[g * HEAD_DIM:(g + 1) * HEAD_DIM, :] = (acc_sc[g, 0:HEAD_DIM, :] / l).astype(BF16)
                lse_ref[g * LSE_ROWS:(g + 1) * LSE_ROWS, :] = jnp.broadcast_to(
                    m_sc[g] + jnp.log(l) * LOG2E, (LSE_ROWS, tq))

        if comm:
            @pl.when((h == KV_HEADS - 1) & (i == n_i - 1) & (t == n_kv - 1))
            def _():
                _exchange_finish(c_remote, c_in_refs, c_out_refs, *c_sems)

    if window:
        kv_blk = lambda i, t: jnp.clip(i * per - 1 + t, 0, n_kb - 1)
    else:
        kv_blk = lambda i, t: t
    hbm = pl.BlockSpec(memory_space=pl.ANY)
    in_specs = [pl.BlockSpec((GROUP_W, tq), lambda h, i, t: (q_rb + h, i)),
                pl.BlockSpec((None, tk, HEAD_DIM), lambda h, i, t: (k_i + h, kv_blk(i, t), 0)),
                pl.BlockSpec((HEAD_DIM, tk), lambda h, i, t: (v_rb + h, kv_blk(i, t)))]
    args = [qkv_t, kv_tok, qkv_t]
    if window:
        in_specs = [pl.BlockSpec(memory_space=pltpu.SMEM)] + in_specs
        args = [sink] + args
    return pl.pallas_call(
        body, name=name, grid=(KV_HEADS, n_i, n_kv),
        in_specs=in_specs + [hbm] * len(c_ins),
        out_specs=[pl.BlockSpec((GROUP_W, tq), lambda h, i, t: (h, i)),
                   pl.BlockSpec((GROUP * LSE_ROWS, tq), lambda h, i, t: (h, i))] + [hbm] * len(c_outs),
        out_shape=[_sds((KV_HEADS * GROUP_W, s_len), BF16),
                   _sds((KV_HEADS * GROUP * LSE_ROWS, s_len), F32)] + list(c_outs),
        scratch_shapes=[pltpu.VMEM((GROUP, 1, tq), F32),
                        pltpu.VMEM((GROUP, HEAD_DIM + ONES_ROWS, tq), F32)] + _exchange_sems(c_remote),
        compiler_params=_params("arbitrary", "arbitrary", "arbitrary"),
    )(*args, *c_ins)


def flash_bwd_t(qkv_t, kv_tok, o_t, do_t, lse, q_rb, k_i, v_i, k_rb, do_rb, sink, window, name, comm=None):
    s_len = qkv_t.shape[1]
    if window:
        tq = _row_tile(s_len, 512)
        tk = 2 * BLOCK
        assert tq == 2 * tk, "the band parts below are written for query blocks of two key blocks"
        n_q = 2
    else:
        tq, tk = _row_tile(s_len, 2048), _row_tile(s_len, 1024)
        n_q = s_len // tq
    n_qb = s_len // tq
    n_j = s_len // tk
    c_ins, c_outs, c_remote = comm if comm else ([], [], [])
    n_main = 8 if window else 7
    n_out = 4 if window else 3

    def body(*refs):
        main, c_in_refs = refs[:n_main], refs[n_main:n_main + len(c_ins)]
        rest = refs[n_main + len(c_ins):]
        outs, c_out_refs = rest[:n_out], rest[n_out:n_out + len(c_outs)]
        dk_sc, dv_sc = rest[n_out + len(c_outs):n_out + len(c_outs) + 2]
        c_sems = rest[n_out + len(c_outs) + 2:]
        if window:
            sink_ref, q_ref, k_ref, v_ref, kt_ref, o_ref, do_ref, lse_ref = main
            dq_ref, dk_ref, dv_ref, dsink_ref = outs
        else:
            q_ref, k_ref, v_ref, kt_ref, o_ref, do_ref, lse_ref = main
            dq_ref, dk_ref, dv_ref = outs
        h, j, t = pl.program_id(0), pl.program_id(1), pl.program_id(2)
        q_blk = (j + 1) // 2 - 1 + t if window else t
        if comm:
            @pl.when((h == 0) & (j == 0) & (t == 0))
            def _():
                _exchange_start(c_remote, c_in_refs, c_out_refs, *c_sems)

        @pl.when((j == 0) & (t == 0))
        def _():
            dq_ref[...] = jnp.zeros(dq_ref.shape, F32)
            if window:
                dsink_ref[...] = jnp.zeros((8, LANES), F32)

        @pl.when(t == 0)
        def _():
            dk_sc[...] = jnp.zeros((tk, HEAD_DIM), F32)
            dv_sc[...] = jnp.zeros((tk, HEAD_DIM), F32)

        def tile(k_lo, k_hi, q_lo, q_hi, sink_lo=0, sink_hi=0):
            ks, qs = slice(k_lo, k_hi), slice(q_lo, q_hi)
            k, v, k_t = k_ref[ks, :], v_ref[ks, :], kt_ref[:, ks]
            if window:
                mask = _pos_mask_t(j * tk + k_lo, q_blk * tq + q_lo, s_len, k_hi - k_lo, q_hi - q_lo)
                lane = lax.broadcasted_iota(jnp.int32, (8, LANES), 1)
                sink_tile = jnp.zeros((8, LANES), F32)
            dk_acc = dk_sc[ks, :]
            dv_acc = dv_sc[ks, :]
            for g in range(GROUP):
                rows = slice(g * HEAD_DIM, (g + 1) * HEAD_DIM)
                q_t, o_g, do_g = q_ref[rows, qs], o_ref[rows, qs], do_ref[rows, qs]
                s_t = jnp.dot(k, q_t, preferred_element_type=F32)
                if window:
                    s_t = jnp.where(mask, s_t, NEG)
                lse_row = lse_ref[g * LSE_ROWS:g * LSE_ROWS + 1, qs]
                p_t = jnp.exp2(s_t - lse_row)
                delta = jnp.sum(do_g.astype(F32) * o_g.astype(F32), axis=0, keepdims=True)
                dp_t = jnp.dot(v, do_g, preferred_element_type=F32)
                ds_t = (p_t * (dp_t - delta)).astype(BF16)
                dv_acc = dv_acc + lax.dot_general(p_t.astype(BF16), do_g, NT_DIMS, preferred_element_type=F32)
                dk_acc = dk_acc + lax.dot_general(ds_t, q_t, NT_DIMS, preferred_element_type=F32)
                dq_ref[q_blk, rows, qs] += jnp.dot(k_t, ds_t, preferred_element_type=F32)
                if sink_hi > sink_lo:
                    at = slice(sink_lo - q_lo, sink_hi - q_lo)
                    p_sink = jnp.exp2(sink_ref[h * GROUP + g] * LOG2E - lse_row[:, at])
                    term = -jnp.sum(p_sink * delta[:, at], axis=1, keepdims=True)
                    sink_tile = jnp.where(lane == g, term, sink_tile)
            dk_sc[ks, :] = dk_acc
            dv_sc[ks, :] = dv_acc
            if sink_hi > sink_lo:
                dsink_ref[...] += sink_tile

        if window:
            parts = {(0, 0): (0, BLOCK, tq - BLOCK, tq), (0, 1): (0, tk, 0, tq - BLOCK, 0, tq - BLOCK),
                     (1, 0): (0, tk, BLOCK, tq, tq - BLOCK, tq), (1, 1): (tk - BLOCK, tk, 0, BLOCK)}
            for (parity, tt), part in parts.items():
                pl.when((q_blk >= 0) & (q_blk < n_qb) & (j % 2 == parity) & (t == tt))(
                    functools.partial(tile, *part))
        else:
            tile(0, tk, 0, tq)

        @pl.when(t == n_q - 1)
        def _():
            dk_ref[...] = dk_sc[...] * LN2
            dv_ref[...] = dv_sc[...]

        if comm:
            @pl.when((h == KV_HEADS - 1) & (j == n_j - 1) & (t == n_q - 1))
            def _():
                _exchange_finish(c_remote, c_in_refs, c_out_refs, *c_sems)

    if window:
        qb = lambda j, t: jnp.clip((j + 1) // 2 - 1 + t, 0, n_qb - 1)
    else:
        qb = lambda j, t: t
    hbm = pl.BlockSpec(memory_space=pl.ANY)
    in_specs = [pl.BlockSpec((GROUP_W, tq), lambda h, j, t: (q_rb + h, qb(j, t))),
                pl.BlockSpec((None, tk, HEAD_DIM), lambda h, j, t: (k_i + h, j, 0)),
                pl.BlockSpec((None, tk, HEAD_DIM), lambda h, j, t: (v_i + h, j, 0)),
                pl.BlockSpec((HEAD_DIM, tk), lambda h, j, t: (k_rb + h, j)),
                pl.BlockSpec((GROUP_W, tq), lambda h, j, t: (h, qb(j, t))),
                pl.BlockSpec((GROUP_W, tq), lambda h, j, t: (do_rb + h, qb(j, t))),
                pl.BlockSpec((GROUP * LSE_ROWS, tq), lambda h, j, t: (h, qb(j, t)))]
    args = [qkv_t, kv_tok, kv_tok, qkv_t, o_t, do_t, lse]
    kv_out = _sds((KV_HEADS, s_len, HEAD_DIM), F32)
    out_specs = [pl.BlockSpec((n_qb, GROUP_W, tq), lambda h, j, t: (0, h, 0)),
                 pl.BlockSpec((None, tk, HEAD_DIM), lambda h, j, t: (h, j, 0)),
                 pl.BlockSpec((None, tk, HEAD_DIM), lambda h, j, t: (h, j, 0))]
    out_shape = [_sds((n_qb, KV_HEADS * GROUP_W, tq), F32), kv_out, kv_out]
    if window:
        in_specs = [pl.BlockSpec(memory_space=pltpu.SMEM)] + in_specs
        args = [sink] + args
        out_specs.append(pl.BlockSpec((None, 8, LANES), lambda h, j, t: (h, 0, 0)))
        out_shape.append(_sds((KV_HEADS, 8, LANES), F32))
    return pl.pallas_call(
        body, name=name, grid=(KV_HEADS, n_j, n_q),
        in_specs=in_specs + [hbm] * len(c_ins), out_specs=out_specs + [hbm] * len(c_outs),
        out_shape=out_shape + list(c_outs),
        scratch_shapes=[pltpu.VMEM((tk, HEAD_DIM), F32), pltpu.VMEM((tk, HEAD_DIM), F32)]
        + _exchange_sems(c_remote),
        compiler_params=_params("arbitrary", "arbitrary", "arbitrary"),
    )(*args, *c_ins)


SGU_GROUPS = 8
SGU_CHUNK = 128
GELU_C = float(np.sqrt(2.0 / np.pi))
GELU_A = 0.044715


def _gelu_and_grad(x):
    x2 = x * x
    t = jnp.tanh(x * (GELU_C + (GELU_C * GELU_A) * x2))
    hx = 0.5 * x
    return hx + hx * t, (0.5 + 0.5 * t) + (hx * (1.0 - t * t)) * (GELU_C + (3.0 * GELU_C * GELU_A) * x2)


def _gelu(x):
    t = jnp.tanh(x * (GELU_C + (GELU_C * GELU_A) * (x * x)))
    hx = 0.5 * x
    return hx + hx * t


def _layernorm_stats(v):
    mu = jnp.mean(v, axis=-1, keepdims=True)
    var = jnp.mean(jnp.square(v - mu), axis=-1, keepdims=True)
    rstd = lax.rsqrt(var + EPS)
    return (v - mu) * rstd, rstd


def sgu_mid_fwd(zpre, ln_g, ln_b, ws, bsb, name):
    s_len, width = zpre.shape
    d = width // 2
    ts = _row_tile(s_len, 256)

    def body(z_ref, g_ref, b_ref, ws_ref, bs_ref, y_ref):
        z = _gelu(z_ref[...])
        u, v = z[:, :d], z[:, d:]
        vhat, _ = _layernorm_stats(v)
        vn = (vhat * g_ref[...] + b_ref[...]).astype(BF16)
        for n in range(ts // SGU_CHUNK):
            rows = slice(n * SGU_CHUNK, (n + 1) * SGU_CHUNK)
            for g in range(SGU_GROUPS):
                cols = slice(g * LANES, (g + 1) * LANES)
                mixed = jnp.dot(ws_ref[g], vn[rows, cols], preferred_element_type=F32) + bs_ref[g]
                y_ref[rows, cols] = (u[rows, cols] * mixed).astype(BF16)

    vec = pl.BlockSpec((1, d), lambda i: (0, 0))
    cube = pl.BlockSpec((SGU_GROUPS, SGU_CHUNK, SGU_CHUNK), lambda i: (0, 0, 0))
    return pl.pallas_call(
        body, name=name, grid=(s_len // ts,),
        in_specs=[pl.BlockSpec((ts, width), lambda i: (i, 0)), vec, vec, cube, cube],
        out_specs=pl.BlockSpec((ts, d), lambda i: (i, 0)),
        out_shape=_sds((s_len, d), BF16),
        compiler_params=_params("arbitrary"),
    )(zpre, ln_g, ln_b, ws, bsb)


def sgu_mid_bwd(zpre, dy, ln_g, ln_b, ws, wst, bsb, name):
    s_len, width = zpre.shape
    d = width // 2
    ts = _row_tile(s_len, 256)
    n_steps = s_len // ts

    def body(z_ref, dy_ref, g_ref, b_ref, ws_ref, wst_ref, bs_ref,
             dz_ref, dws_ref, dbs_ref, dg_ref, db_ref, du_sc, dvn_sc):
        i = pl.program_id(0)

        @pl.when(i == 0)
        def _():
            dws_ref[...] = jnp.zeros(dws_ref.shape, F32)
            dbs_ref[...] = jnp.zeros(dbs_ref.shape, F32)
            dg_ref[...] = jnp.zeros(dg_ref.shape, F32)
            db_ref[...] = jnp.zeros(db_ref.shape, F32)

        zp = z_ref[...]
        z, gp = _gelu_and_grad(zp)
        u, v = z[:, :d], z[:, d:]
        vhat, rstd = _layernorm_stats(v)
        gain = g_ref[...]
        vn = (vhat * gain + b_ref[...]).astype(BF16)
        dyf = dy_ref[...].astype(F32)
        for n in range(ts // SGU_CHUNK):
            rows = slice(n * SGU_CHUNK, (n + 1) * SGU_CHUNK)
            for g in range(SGU_GROUPS):
                cols = slice(g * LANES, (g + 1) * LANES)
                vt = vn[rows, cols]
                mixed = jnp.dot(ws_ref[g], vt, preferred_element_type=F32) + bs_ref[g]
                dyt = dyf[rows, cols]
                du_sc[rows, cols] = dyt * mixed
                dmixed = dyt * u[rows, cols]
                dmb = dmixed.astype(BF16)
                dvn_sc[rows, cols] = jnp.dot(wst_ref[g], dmb, preferred_element_type=F32)
                dws_ref[g] += lax.dot_general(dmb, vt, NT_DIMS, preferred_element_type=F32)
                dbs_ref[g] += dmixed
        dvn = dvn_sc[...]
        dg_ref[...] += jnp.sum(dvn * vhat, axis=0, keepdims=True)
        db_ref[...] += jnp.sum(dvn, axis=0, keepdims=True)
        dvh = dvn * gain
        dv = rstd * (dvh - jnp.mean(dvh, axis=-1, keepdims=True)
                     - vhat * jnp.mean(dvh * vhat, axis=-1, keepdims=True))
        dz_ref[:, :d] = (du_sc[...] * gp[:, :d]).astype(BF16)
        dz_ref[:, d:] = (dv * gp[:, d:]).astype(BF16)

        @pl.when(i == n_steps - 1)
        def _():
            for g in range(SGU_GROUPS):
                tot = jnp.sum(dbs_ref[g], axis=1, keepdims=True)
                dbs_ref[g] = jnp.broadcast_to(tot, (SGU_CHUNK, LANES))

    vec = pl.BlockSpec((1, d), lambda i: (0, 0))
    cube = pl.BlockSpec((SGU_GROUPS, SGU_CHUNK, SGU_CHUNK), lambda i: (0, 0, 0))
    cube_shape = _sds((SGU_GROUPS, SGU_CHUNK, SGU_CHUNK), F32)
    return pl.pallas_call(
        body, name=name, grid=(n_steps,),
        in_specs=[pl.BlockSpec((ts, width), lambda i: (i, 0)), pl.BlockSpec((ts, d), lambda i: (i, 0)),
                  vec, vec, cube, cube, cube],
        out_specs=[pl.BlockSpec((ts, width), lambda i: (i, 0)), cube, cube, vec, vec],
        out_shape=[_sds((s_len, width), BF16), cube_shape, cube_shape, _sds((1, d), F32), _sds((1, d), F32)],
        scratch_shapes=[pltpu.VMEM((ts, d), F32), pltpu.VMEM((ts, d), F32)],
        compiler_params=_params("arbitrary"),
    )(zpre, dy, ln_g, ln_b, ws, wst, bsb)


def loss_head(x, g, target, name):
    s_len, d = x.shape
    tm = _row_tile(s_len, 512)

    def body(x_ref, g_ref, t_ref, dx_ref, dxb_ref, dg_ref, loss_ref):
        i = pl.program_id(0)
        xf = x_ref[...]
        gain = g_ref[...]
        r = lax.rsqrt(jnp.mean(xf * xf, axis=-1, keepdims=True) + EPS)
        xhat = xf * r
        err = xhat * gain - t_ref[...]
        row = jnp.mean(err * err, axis=-1, keepdims=True)
        part = 0.5 * jnp.sum(row, axis=0, keepdims=True)
        dy = err * (1.0 / d)
        dg_part = jnp.sum(dy * xhat, axis=0, keepdims=True)

        @pl.when(i == 0)
        def _():
            dg_ref[...] = dg_part
            loss_ref[...] = jnp.broadcast_to(part, (8, LANES))

        @pl.when(i > 0)
        def _():
            dg_ref[...] += dg_part
            loss_ref[...] += jnp.broadcast_to(part, (8, LANES))

        dxh = dy * gain
        dx = r * (dxh - xhat * jnp.mean(dxh * xhat, axis=-1, keepdims=True))
        dx_ref[...] = dx
        dxb_ref[...] = dx.astype(BF16)

    row_spec = pl.BlockSpec((tm, d), lambda i: (i, 0))
    vec = pl.BlockSpec((1, d), lambda i: (0, 0))
    return pl.pallas_call(
        body, name=name, grid=(s_len // tm,),
        in_specs=[row_spec, vec, row_spec],
        out_specs=[row_spec, row_spec, vec, pl.BlockSpec((8, LANES), lambda i: (0, 0))],
        out_shape=[_sds((s_len, d), F32), _sds((s_len, d), BF16), _sds((1, d), F32), _sds((8, LANES), F32)],
        compiler_params=_params("arbitrary"),
    )(x, g, target)


FLIP_BITS = {"c": (0, 0, 1), "x": (1, 0, 0), "y": (0, 1, 0), "xy": (1, 1, 0),
             "xc": (1, 0, 1), "yc": (0, 1, 1), "xyc": (1, 1, 1)}
CHIP_FLIPS = ("x", "y", "xy")


def _flip(pos, name):
    return tuple(1 - p if bit else p for p, bit in zip(pos, FLIP_BITS[name]))


def _chip(pos):
    return 2 * pos[0] + pos[1]


def _me():
    return (lax.axis_index("x"), lax.axis_index("y"), lax.axis_index("c"))


def _exchange_copy(remote, k, in_refs, out_refs, send_sems, recv_sems, sender, receiver):
    ii, src_fn, oi, dst_fn, _ = remote[k]
    return pltpu.make_async_remote_copy(
        src_ref=src_fn(in_refs[ii], sender, receiver), dst_ref=dst_fn(out_refs[oi], sender),
        send_sem=send_sems.at[k], recv_sem=recv_sems.at[k], device_id=receiver, device_id_type=MESH)


def _exchange_start(remote, in_refs, out_refs, send_sems, recv_sems):
    me = _me()
    for k in range(len(remote)):
        _exchange_copy(remote, k, in_refs, out_refs, send_sems, recv_sems, me, _flip(me, remote[k][4])).start()


def _exchange_finish(remote, in_refs, out_refs, send_sems, recv_sems):
    me = _me()
    for k in range(len(remote)):
        _exchange_copy(remote, k, in_refs, out_refs, send_sems, recv_sems, _flip(me, remote[k][4]), me).wait_recv()
    for k in range(len(remote)):
        _exchange_copy(remote, k, in_refs, out_refs, send_sems, recv_sems, me, _flip(me, remote[k][4])).wait_send()


def _exchange_sems(remote):
    n = len(remote)
    return [pltpu.SemaphoreType.DMA((n,)), pltpu.SemaphoreType.DMA((n,))] if n else []


def exchange(name, ins, out_shapes, remote, local):
    n_in, n_out = len(ins), len(out_shapes)

    def body(*refs):
        in_refs, out_refs = refs[:n_in], refs[n_in:n_in + n_out]
        send_sems, recv_sems, local_sems = refs[n_in + n_out:]
        me = _me()
        stays = []
        for k, (ii, src_fn, oi, dst_fn) in enumerate(local):
            cp = pltpu.make_async_copy(src_fn(in_refs[ii], me), dst_fn(out_refs[oi], me), local_sems.at[k])
            cp.start()
            stays.append(cp)
        _exchange_start(remote, in_refs, out_refs, send_sems, recv_sems)
        _exchange_finish(remote, in_refs, out_refs, send_sems, recv_sems)
        for cp in stays:
            cp.wait()

    hbm = pl.BlockSpec(memory_space=pl.ANY)
    return pl.pallas_call(
        body, name=name,
        in_specs=[hbm] * n_in, out_specs=[hbm] * n_out, out_shape=list(out_shapes),
        scratch_shapes=[pltpu.SemaphoreType.DMA((max(len(remote), 1),)),
                        pltpu.SemaphoreType.DMA((max(len(remote), 1),)),
                        pltpu.SemaphoreType.DMA((max(len(local), 1),))],
        compiler_params=pltpu.CompilerParams(has_side_effects=True),
    )(*ins)


def staged_push(name, ins, out_shapes, jobs, n_alias=0):
    n_in, n_out = len(ins), len(out_shapes)
    n_copies = sum(len(dsts) for _, _, dsts in jobs)
    n_remote = sum(1 for _, _, dsts in jobs for d in dsts if d[2] is not None)

    def chunk_of(ii, src_fn):
        probe = _ShapeRef(ins[ii].shape, ins[ii].dtype)
        got = src_fn(probe, (0, 0, 0))
        return tuple(got.shape), got.dtype

    classes = []
    for ii, src_fn, _ in jobs:
        c = chunk_of(ii, src_fn)
        if c not in classes:
            classes.append(c)

    def body(*refs):
        in_refs, out_refs = refs[:n_in], refs[n_in:n_in + n_out]
        bufs = refs[n_in + n_out:n_in + n_out + len(classes)]
        load_sems, out_sems, recv_sems = refs[n_in + n_out + len(classes):]
        me = _me()
        pending = [[[], []] for _ in classes]
        used = [0] * len(classes)
        arrivals = []
        k = r = 0

        def begin_load(job):
            ii, src_fn, _ = job
            cls = classes.index(chunk_of(ii, src_fn))
            slot = used[cls] % 2
            used[cls] += 1
            for kind, cp in pending[cls][slot]:
                cp.wait_send() if kind == "remote" else cp.wait()
            pending[cls][slot] = []
            load = pltpu.make_async_copy(src_fn(in_refs[ii], me), bufs[cls].at[slot], load_sems.at[2 * cls + slot])
            load.start()
            return load, cls, slot

        nxt = begin_load(jobs[0])
        for n, (ii, src_fn, dsts) in enumerate(jobs):
            load, cls, slot = nxt
            load.wait()
            buf = bufs[cls].at[slot]
            sent = []
            for oi, dst_fn, flip in dsts:
                if flip is None:
                    cp = pltpu.make_async_copy(buf, dst_fn(out_refs[oi], me), out_sems.at[k])
                    cp.start()
                    sent.append(("local", cp))
                else:
                    peer = _flip(me, flip)
                    cp = pltpu.make_async_remote_copy(
                        src_ref=buf, dst_ref=dst_fn(out_refs[oi], me), send_sem=out_sems.at[k],
                        recv_sem=recv_sems.at[r], device_id=peer, device_id_type=MESH)
                    cp.start()
                    sent.append(("remote", cp))
                    arrivals.append((r, cls, oi, dst_fn, peer))
                    r += 1
                k += 1
            pending[cls][slot] = sent
            if n + 1 < len(jobs):
                nxt = begin_load(jobs[n + 1])
        for per_class in pending:
            for slot_list in per_class:
                for kind, cp in slot_list:
                    cp.wait_send() if kind == "remote" else cp.wait()
        for r, cls, oi, dst_fn, peer in arrivals:
            pltpu.make_async_remote_copy(
                src_ref=bufs[cls].at[0], dst_ref=dst_fn(out_refs[oi], peer), send_sem=out_sems.at[0],
                recv_sem=recv_sems.at[r], device_id=peer, device_id_type=MESH).wait_recv()

    hbm = pl.BlockSpec(memory_space=pl.ANY)
    return pl.pallas_call(
        body, name=name,
        in_specs=[hbm] * n_in, out_specs=[hbm] * n_out, out_shape=list(out_shapes),
        scratch_shapes=[pltpu.VMEM((2,) + shape, dtype) for shape, dtype in classes]
        + [pltpu.SemaphoreType.DMA((2 * len(classes),)), pltpu.SemaphoreType.DMA((max(n_copies, 1),)),
           pltpu.SemaphoreType.DMA((max(n_remote, 1),))],
        input_output_aliases={i: i for i in range(n_alias)},
        compiler_params=pltpu.CompilerParams(has_side_effects=True, vmem_limit_bytes=VMEM_LIMIT),
    )(*ins)


class _ShapeRef:
    def __init__(self, shape, dtype):
        self.shape, self.dtype = tuple(shape), dtype

    @property
    def at(self):
        return self

    def __getitem__(self, idx):
        idx = idx if isinstance(idx, tuple) else (idx,)
        shape = []
        for dim, i in zip(self.shape, idx):
            if isinstance(i, slice):
                shape.append(len(range(*i.indices(dim))))
            elif hasattr(i, "size") and hasattr(i, "start"):
                shape.append(i.size)
        shape += self.shape[len(idx):]
        return _ShapeRef(shape, self.dtype)


def gather_whole(shards, name):
    whole = lambda ref, sender, receiver=None: ref
    slot = lambda ref, sender: ref.at[_chip(sender)]
    remote = [(t, whole, t, slot, flip) for t in range(len(shards)) for flip in CHIP_FLIPS]
    local = [(t, whole, t, slot) for t in range(len(shards))]
    outs = [_sds((N_CHIPS,) + a.shape, a.dtype) for a in shards]
    return exchange(name, list(shards), outs, remote, local)


def _half_axis(shape):
    return 0 if shape[0] >= 2 else 1


def gather_halves_plan(shards):
    remote = []
    for t, a in enumerate(shards):
        ax = _half_axis(a.shape)
        half = lambda ref, sender, receiver=None, ax=ax: _half(ref, sender[2], ax)
        slot = lambda ref, sender, ax=ax: _half(ref.at[_chip(sender)], sender[2], ax)
        remote += [(t, half, t, slot, flip) for flip in CHIP_FLIPS]
    outs = [_sds((N_CHIPS,) + a.shape, a.dtype) for a in shards]
    return list(shards), outs, remote


def gather_halves_fill(got, shards, name):
    n_t = len(shards)
    jobs = []
    for t, a in enumerate(shards):
        layers = a.shape[0]
        for l in range(layers):
            jobs.append((n_t + t, lambda ref, me, l=l: ref.at[l],
                         [(t, lambda ref, sender, l=l: ref.at[_chip(sender), l], None)]))
        for flip in CHIP_FLIPS:
            if _half_axis(a.shape) == 0:
                n = layers // 2
                for j in range(n):
                    at = lambda ref, pos, flip=flip, j=j, n=n: ref.at[_chip(_flip(pos, flip)), pos[2] * n + j]
                    jobs.append((t, at, [(t, at, "c")]))
            else:
                rows = a.shape[1] // 2
                at = lambda ref, pos, flip=flip, rows=rows: ref.at[
                    _chip(_flip(pos, flip)), 0, pl.ds(pos[2] * rows, rows)]
                jobs.append((t, at, [(t, at, "c")]))
    outs = [_sds(g.shape, g.dtype) for g in got]
    return staged_push(name, list(got) + list(shards), outs, jobs, n_alias=n_t)


def add_to_all(plan, buf):
    ins, outs, remote = plan
    whole = lambda ref, sender, receiver=None: ref
    more = [(len(ins), whole, len(outs), (lambda ref, sender, f=f: ref.at[f]), flip)
            for f, flip in enumerate(FLIPS_BY_INDEX)]
    return list(ins) + [buf], list(outs) + [_sds((len(more),) + buf.shape, buf.dtype)], list(remote) + more


FLIPS_BY_INDEX = ("c", "y", "yc", "x", "xc", "xy", "xyc")


def sum_devices(own, got, name):
    rows = own.shape[0]
    tr = LANES if rows % LANES == 0 else rows
    me = (4 * lax.axis_index("x") + 2 * lax.axis_index("y") + lax.axis_index("c")).astype(jnp.int32).reshape(1)
    everyone = jnp.concatenate([own[None], got], axis=0)

    def body(me_ref, a0, a1, a2, a3, a4, a5, a6, a7, o_ref):
        o_ref[...] = ((a0[...] + a1[...]) + (a2[...] + a3[...])) + ((a4[...] + a5[...]) + (a6[...] + a7[...]))

    return pl.pallas_call(
        body, name=name,
        grid_spec=pltpu.PrefetchScalarGridSpec(
            num_scalar_prefetch=1, grid=(rows // tr,),
            in_specs=[pl.BlockSpec((None, tr, LANES), lambda i, m, k=k: (m[0] ^ k, i, 0)) for k in range(8)],
            out_specs=pl.BlockSpec((tr, LANES), lambda i, m: (i, 0))),
        out_shape=_sds((rows, LANES), F32),
        compiler_params=_params("arbitrary"),
    )(me, *([everyone] * 8))


def _half(ref, core, axis):
    rows = ref.shape[axis] // 2
    idx = (slice(None),) * axis + (pl.ds(core * rows, rows),)
    return ref.at[idx]


def reduce_plan(grads):
    remote, outs = [], []
    for t, g in enumerate(grads):
        outs.append(_sds((len(CHIP_FLIPS),) + g.shape[1:], BF16))
        for f, flip in enumerate(CHIP_FLIPS):
            remote.append((t, lambda ref, sender, receiver: ref.at[_chip(receiver)],
                           t, lambda ref, sender, f=f: ref.at[f], flip))
    return list(grads), outs, remote


def reduce_finish(grads, got, stacks, full_shapes, into, name):
    chip = (2 * lax.axis_index("x") + lax.axis_index("y")).astype(jnp.int32).reshape(1)
    core = lax.axis_index("c").astype(jnp.int32).reshape(1)
    sums = [sum_chips(g, r, chip, f"{name}_sum{t}") for t, (g, r) in enumerate(zip(grads, got))]
    jobs, outs = [], []
    for t, a in enumerate(sums):
        half, cols = a.shape[0] // 2, a.shape[1]
        outs.append(_sds((half, cols), F32))
        pieces = max(1, half * cols * 4 // STAGE_BYTES)
        step = half // pieces
        for q in range(pieces):
            jobs.append((t, lambda ref, me, q=q, step=step, half=half: ref.at[pl.ds((1 - me[2]) * half + q * step, step)],
                         [(t, lambda ref, sender, q=q, step=step: ref.at[pl.ds(q * step, step)], "c")]))
    theirs = staged_push(name + "_swap", sums, outs, jobs)
    totals = [add_rows(a, r, core, f"{name}_add{t}") for t, (a, r) in enumerate(zip(sums, theirs))]
    names = []
    for out_name, _ in stacks:
        if out_name not in names:
            names.append(out_name)
    names = [n for n in names if n in into] + [n for n in names if n not in into]
    kept = [into[n] for n in names if n in into]
    outs = [_sds(full_shapes[n], F32) for n in names]
    jobs = []
    for t, (out_name, layer) in enumerate(stacks):
        oi = names.index(out_name)
        rows, cols = totals[t].shape
        pieces = max(1, rows * cols * 4 // STAGE_BYTES)
        step = rows // pieces
        for q in range(pieces):
            src = lambda ref, me, q=q, step=step: ref.at[pl.ds(q * step, step)]
            place = lambda ref, sender, layer=layer, q=q, step=step, rows=rows: ref.at[
                layer, pl.ds(sender[2] * rows + q * step, step)]
            jobs.append((len(kept) + t, src, [(oi, place, None), (oi, place, "c")]))
    full = staged_push(name + "_share", kept + totals, outs, jobs, n_alias=len(kept))
    return {**into, **dict(zip(names, full))}


STAGE_BYTES = 1024 * 1024


def add_rows(a, theirs, core, name):
    rows, cols = a.shape
    half = rows // 2
    tr = _row_tile(half, 256)
    nb = half // tr

    def body(core_ref, a_ref, t_ref, o_ref):
        o_ref[...] = a_ref[...] + t_ref[...]

    return pl.pallas_call(
        body, name=name,
        grid_spec=pltpu.PrefetchScalarGridSpec(
            num_scalar_prefetch=1, grid=(nb,),
            in_specs=[pl.BlockSpec((tr, cols), lambda i, c: (c[0] * nb + i, 0)),
                      pl.BlockSpec((tr, cols), lambda i, c: (i, 0))],
            out_specs=pl.BlockSpec((tr, cols), lambda i, c: (i, 0))),
        out_shape=_sds((half, cols), F32),
        compiler_params=_params("arbitrary"),
    )(core, a, theirs)


def sum_chips(mine, theirs, chip, name):
    _, half, cols = mine.shape
    tr = _row_tile(half, 256)

    def body(chip_ref, m_ref, a_ref, b_ref, c_ref, o_ref):
        o_ref[...] = ((m_ref[...].astype(F32) + a_ref[...].astype(F32))
                      + b_ref[...].astype(F32)) + c_ref[...].astype(F32)

    got = lambda f: pl.BlockSpec((None, tr, cols), lambda i, ch: (f, i, 0))
    return pl.pallas_call(
        body, name=name,
        grid_spec=pltpu.PrefetchScalarGridSpec(
            num_scalar_prefetch=1, grid=(half // tr,),
            in_specs=[pl.BlockSpec((None, tr, cols), lambda i, ch: (ch[0], i, 0)), got(0), got(1), got(2)],
            out_specs=pl.BlockSpec((tr, cols), lambda i, ch: (i, 0))),
        out_shape=_sds((half, cols), F32),
        compiler_params=_params("arbitrary"),
    )(chip, mine, theirs, theirs, theirs)


def _tok(t):
    return t.transpose(1, 0, 2).reshape(t.shape[1], t.shape[0] * t.shape[2])


def _heads(t):
    return t.reshape(t.shape[0], t.shape[1] // HEAD_DIM, HEAD_DIM).transpose(1, 0, 2)


def _tile2(vec):
    return jnp.tile(vec.reshape(1, HEAD_DIM), (1, 2))


REST = ("att_w_in", "att_w_out", "sgu_w_in", "sgu_w_out", "mlp_w1", "mlp_w2")
LAST_GROUP = (("att_w_in", 0),)
LATE_SMALL = ("att_norm", "att_sink", "att_qnorm", "att_knorm")


def local_step(x, target, first, rest_shards, rep, full_shapes):
    s_len, d = x.shape
    tabs = rope_tables(s_len)
    depth = rep["mlp_norm"].shape[0]
    row = lambda a: a.reshape(1, -1)
    saved = []
    h = x
    gw = {"att_w_in": [first]}

    def wl(name, idx):
        return (gw[name][idx], 0) if name == "att_w_in" else (gw[name], idx)

    for layer in range(depth):
        i = layer // 2
        tag = f"l{layer}"
        if layer % 2 == 0:
            hn, proj = norm_mm(h, row(rep["att_norm"][i]), *wl("att_w_in", i), F32, tag + "_att_proj")
            qkv_t, kv = prep_fwd(proj, tabs, _tile2(rep["att_qnorm"][i]), _tile2(rep["att_knorm"][i]),
                                 tag + "_att_prep")
            kv_tok = _heads(kv)
            oa, lse_a = flash_fwd_t(qkv_t, kv_tok, QA_COL // GROUP_W, 0, (KA_COL + LANES) // HEAD_DIM,
                                    rep["att_sink"][i], True, tag + "_win_fwd")
            plan = gather_halves_plan(rest_shards) if layer == 0 else None
            ob, lse_b, *got = flash_fwd_t(qkv_t, kv_tok, QB_COL // GROUP_W, 4, (KB_COL + LANES) // HEAD_DIM,
                                          None, False, tag + "_grid_fwd", comm=plan)
            if layer == 0:
                rest = dict(zip(REST, gather_halves_fill(got, rest_shards, "gather_rest_fill")))
                gw["att_w_in"].append(rest.pop("att_w_in"))
                gw.update(rest)
            out = mm_res_t([oa, ob], *wl("att_w_out", i), h, tag + "_att_out")
            mix_saved = (h, hn, proj, qkv_t, kv_tok, oa, ob, lse_a, lse_b)
        else:
            hn, zpre = norm_mm(h, row(rep["sgu_norm"][i]), *wl("sgu_w_in", i), F32, tag + "_sgu_in")
            ws = rep["sgu_w_s"][i].astype(BF16)
            bsb = jnp.broadcast_to(rep["sgu_b_s"][i][:, :, None], (SGU_GROUPS, SGU_CHUNK, LANES))
            y = sgu_mid_fwd(zpre, row(rep["sgu_ln_g"][i]), row(rep["sgu_ln_b"][i]), ws, bsb, tag + "_sgu_mid")
            out = mm_res(y, *wl("sgu_w_out", i), h, tag + "_sgu_out")
            mix_saved = (h, hn, zpre, y, ws, bsb)
        hm, a = norm_mm(out, row(rep["mlp_norm"][layer]), *wl("mlp_w1", layer), BF16, tag + "_mlp_up")
        nxt = mm_res(a, *wl("mlp_w2", layer), out, tag + "_mlp_down", relu2=True)
        saved.append((mix_saved, (out, hm, a)))
        h = nxt
    dh, dhb, d_final, loss_tile = loss_head(h, row(rep["final_norm"]), target, "loss_head")
    big, tags = [], []
    small = {k: [jnp.zeros(v.shape[1:], F32)] * v.shape[0] for k, v in rep.items() if k != "final_norm"}
    small["final_norm"] = d_final.reshape(-1)
    stacked = lambda: [small[n] if n == "final_norm" else jnp.stack(small[n]) for n in SMALL]
    for layer in reversed(range(depth)):
        i = layer // 2
        tag = f"l{layer}"
        mix_saved, (xin, hm, a) = saved[layer]
        da = mm_nt_relu2_bwd(dhb, *wl("mlp_w2", layer), a, tag + "_mlp_down_bwd")
        big.append(dw_mm(a, dhb, tag + "_mlp_dw2", col_sharded=False, relu2=True))
        tags.append(("mlp_w2", layer))
        big.append(dw_mm(hm, da, tag + "_mlp_dw1", col_sharded=True))
        tags.append(("mlp_w1", layer))
        dh, dhb, dg = dx_norm(da, *wl("mlp_w1", layer), xin, row(rep["mlp_norm"][layer]), dh, tag + "_mlp_up_bwd")
        small["mlp_norm"][layer] = dg.reshape(-1)
        if layer % 2 == 0:
            xin, hn, proj, qkv_t, kv_tok, oa, ob, lse_a, lse_b = mix_saved
            do_t = mm_nt(dhb, *wl("att_w_out", i), tag + "_att_out_bwd", transposed=True)
            big.append(dw_nn([oa, ob], dhb, tag + "_att_dwout"))
            tags.append(("att_w_out", i))
            dqa, dka, dva, dsink = flash_bwd_t(qkv_t, kv_tok, oa, do_t, lse_a, QA_COL // GROUP_W, 0, 2,
                                               KA_COL // HEAD_DIM, 0, rep["att_sink"][i], True, tag + "_win_bwd")
            plan = None
            if layer == 0:
                early = [k for k, t in enumerate(tags) if t not in LAST_GROUP]
                early_pack = _pack(stacked() + [loss_tile[0, :1]])
                plan = add_to_all(reduce_plan([big[k] for k in early]), early_pack)
            dqb, dkb, dvb, *got = flash_bwd_t(qkv_t, kv_tok, ob, do_t, lse_b, QB_COL // GROUP_W, 4, 6,
                                              KB_COL // HEAD_DIM, 2, None, False, tag + "_grid_bwd", comm=plan)
            if layer == 0:
                grads = reduce_finish([big[k] for k in early], got[:-1], [tags[k] for k in early], full_shapes,
                                      {}, "grads1")
                early_sum = sum_devices(early_pack, got[-1], "sum_small")
            qg, kg = _tile2(rep["att_qnorm"][i]), _tile2(rep["att_knorm"][i])
            dproj, dqg, dkg = prep_bwd(proj, dqa, _tok(dka), _tok(dva), dqb, _tok(dkb), _tok(dvb),
                                       tabs, qg, kg, tag + "_att_prep_bwd")
            big.append(dw_mm(hn, dproj, tag + "_att_dwin", col_sharded=True))
            tags.append(("att_w_in", i))
            dh, dhb, dg = dx_norm(dproj, *wl("att_w_in", i), xin, row(rep["att_norm"][i]), dh, tag + "_att_proj_bwd")
            small["att_norm"][i] = dg.reshape(-1)
            small["att_sink"][i] = dsink[:, 0, :GROUP].reshape(-1)
            small["att_qnorm"][i] = dqg[0, :HEAD_DIM] + dqg[0, HEAD_DIM:]
            small["att_knorm"][i] = dkg[0, :HEAD_DIM] + dkg[0, HEAD_DIM:]
        else:
            xin, hn, zpre, y, ws, bsb = mix_saved
            dy = mm_nt(dhb, *wl("sgu_w_out", i), tag + "_sgu_out_bwd")
            big.append(dw_mm(y, dhb, tag + "_sgu_dwout", col_sharded=False))
            tags.append(("sgu_w_out", i))
            wst = ws.transpose(0, 2, 1)
            dz, dws, dbs, dlg, dlb = sgu_mid_bwd(zpre, dy, row(rep["sgu_ln_g"][i]), row(rep["sgu_ln_b"][i]),
                                                 ws, wst, bsb, tag + "_sgu_mid_bwd")
            big.append(dw_mm(hn, dz, tag + "_sgu_dwin", col_sharded=True))
            tags.append(("sgu_w_in", i))
            dh, dhb, dg = dx_norm(dz, *wl("sgu_w_in", i), xin, row(rep["sgu_norm"][i]), dh, tag + "_sgu_in_bwd")
            small["sgu_norm"][i] = dg.reshape(-1)
            small["sgu_ln_g"][i] = dlg.reshape(-1)
            small["sgu_ln_b"][i] = dlb.reshape(-1)
            small["sgu_w_s"][i] = dws
            small["sgu_b_s"][i] = dbs[:, :, 0]
    late = [k for k, t in enumerate(tags) if t in LAST_GROUP]
    late_pack = _pack([small[n][0] for n in LATE_SMALL])
    got = exchange("grads2_scatter", *add_to_all(reduce_plan([big[k] for k in late]), late_pack), [])
    grads = reduce_finish([big[k] for k in late], got[:-1], [tags[k] for k in late], full_shapes, grads, "grads2")
    late_sum = sum_devices(late_pack, got[-1], "sum_small_late")
    shapes = [a.shape for a in stacked()]
    *small_g, loss = _unpack(early_sum, shapes + [()])
    small_g = dict(zip(SMALL, small_g))
    for n, g in zip(LATE_SMALL, _unpack(late_sum, [small[n][0].shape for n in LATE_SMALL])):
        small_g[n] = small_g[n].at[0].add(g)
    return loss, dh, grads, small_g


BIG = ("att_w_in", "att_w_out", "sgu_w_in", "sgu_w_out", "mlp_w1", "mlp_w2")
SHARDED_VEC = ("sgu_norm", "sgu_ln_g", "sgu_ln_b")
REPLICATED = ("att_norm", "att_sink", "att_qnorm", "att_knorm", "sgu_w_s", "sgu_b_s", "mlp_norm", "final_norm")
WEIGHTS = ("att_norm", "att_w_in", "att_sink", "att_qnorm", "att_knorm", "att_w_out", "sgu_norm", "sgu_w_in",
           "sgu_ln_g", "sgu_ln_b", "sgu_w_s", "sgu_b_s", "sgu_w_out", "mlp_norm", "mlp_w1", "mlp_w2", "final_norm")
SMALL = tuple(n for n in WEIGHTS if n not in BIG)
PACK_ALIGN = 8 * LANES


def _pack(arrays):
    flat = jnp.concatenate([a.reshape(-1) for a in arrays])
    pad = -flat.shape[0] % PACK_ALIGN
    return jnp.pad(flat, (0, pad)).reshape(-1, LANES)


def _unpack(flat2d, shapes):
    flat = flat2d.reshape(-1)
    out, off = [], 0
    for shape in shapes:
        size = int(np.prod(shape))
        out.append(flat[off:off + size].reshape(shape))
        off += size
    return out


def kernel(x, att_norm, att_w_in, att_sink, att_qnorm, att_knorm, att_w_out, sgu_norm, sgu_w_in, sgu_ln_g, sgu_ln_b, sgu_w_s, sgu_b_s, sgu_w_out, mlp_norm, mlp_w1, mlp_w2, final_norm, loss_target, m_att_norm, m_att_w_in, m_att_sink, m_att_qnorm, m_att_knorm, m_att_w_out, m_sgu_norm, m_sgu_w_in, m_sgu_ln_g, m_sgu_ln_b, m_sgu_w_s, m_sgu_b_s, m_sgu_w_out, m_mlp_norm, m_mlp_w1, m_mlp_w2, m_final_norm, v_att_norm, v_att_w_in, v_att_sink, v_att_qnorm, v_att_knorm, v_att_w_out, v_sgu_norm, v_sgu_w_in, v_sgu_ln_g, v_sgu_ln_b, v_sgu_w_s, v_sgu_b_s, v_sgu_w_out, v_mlp_norm, v_mlp_w1, v_mlp_w2, v_final_norm):
    w = dict(att_norm=att_norm, att_w_in=att_w_in, att_sink=att_sink, att_qnorm=att_qnorm, att_knorm=att_knorm,
             att_w_out=att_w_out, sgu_norm=sgu_norm, sgu_w_in=sgu_w_in, sgu_ln_g=sgu_ln_g, sgu_ln_b=sgu_ln_b,
             sgu_w_s=sgu_w_s, sgu_b_s=sgu_b_s, sgu_w_out=sgu_w_out, mlp_norm=mlp_norm, mlp_w1=mlp_w1,
             mlp_w2=mlp_w2, final_norm=final_norm)
    m = dict(att_norm=m_att_norm, att_w_in=m_att_w_in, att_sink=m_att_sink, att_qnorm=m_att_qnorm,
             att_knorm=m_att_knorm, att_w_out=m_att_w_out, sgu_norm=m_sgu_norm, sgu_w_in=m_sgu_w_in,
             sgu_ln_g=m_sgu_ln_g, sgu_ln_b=m_sgu_ln_b, sgu_w_s=m_sgu_w_s, sgu_b_s=m_sgu_b_s,
             sgu_w_out=m_sgu_w_out, mlp_norm=m_mlp_norm, mlp_w1=m_mlp_w1, mlp_w2=m_mlp_w2,
             final_norm=m_final_norm)
    v = dict(att_norm=v_att_norm, att_w_in=v_att_w_in, att_sink=v_att_sink, att_qnorm=v_att_qnorm,
             att_knorm=v_att_knorm, att_w_out=v_att_w_out, sgu_norm=v_sgu_norm, sgu_w_in=v_sgu_w_in,
             sgu_ln_g=v_sgu_ln_g, sgu_ln_b=v_sgu_ln_b, sgu_w_s=v_sgu_w_s, sgu_b_s=v_sgu_b_s,
             sgu_w_out=v_sgu_w_out, mlp_norm=v_mlp_norm, mlp_w1=v_mlp_w1, mlp_w2=v_mlp_w2,
             final_norm=v_final_norm)
    chip = 2 * lax.axis_index("x") + lax.axis_index("y")

    vecs = jnp.stack([w[n] for n in SHARDED_VEC])
    wb = {n: w[n].astype(BF16) for n in BIG}
    first, vec_all = gather_whole([wb["att_w_in"][0:1], vecs], "gather_first")
    rest_shards = [wb[n][1:2] if n == "att_w_in" else wb[n] for n in REST]
    vec_full = vec_all.transpose(1, 2, 0, 3).reshape(vecs.shape[0], vecs.shape[1], -1)
    rep = {n: w[n] for n in REPLICATED}
    rep.update({n: vec_full[k] for k, n in enumerate(SHARDED_VEC)})

    loss, grad_x, grads, small_g = local_step(x[0], loss_target[0], first, rest_shards, rep,
                                              {n: w[n].shape for n in BIG})
    width = w["sgu_norm"].shape[1]
    for n in SHARDED_VEC:
        small_g[n] = lax.dynamic_slice_in_dim(small_g[n], chip * width, width, axis=1)
    grads.update(small_g)
    for n in BIG:
        grads[n] = grads[n].reshape(w[n].shape)

    delta, new_m, new_v = {}, {}, {}
    for n in WEIGHTS:
        shape = w[n].shape
        two_d = (lambda a: a.reshape(1, -1)) if len(shape) == 1 else (lambda a: a)
        dn, mn, vn = adamw(two_d(w[n]), two_d(grads[n]), two_d(m[n]), two_d(v[n]), "adamw_" + n)
        delta[n], new_m[n], new_v[n] = dn.reshape(shape), mn.reshape(shape), vn.reshape(shape)
    return (loss, grad_x[None], *[grads[n] for n in WEIGHTS], *[delta[n] for n in WEIGHTS],
            *[new_m[n] for n in WEIGHTS], *[new_v[n] for n in WEIGHTS])
```

```python
import functools

import numpy as np
import jax
import jax.numpy as jnp
from jax import lax
from jax.experimental import pallas as pl
from jax.experimental.pallas import tpu as pltpu

F32 = jnp.float32
BF16 = jnp.bfloat16
MESH = pl.DeviceIdType.MESH

EPS = 1e-6
HEAD_DIM = 64
BLOCK = 128
GRID_W = 64
ROPE_THETA = 10000.0
N_CHIPS = 4
LANES = 128
V7X_VMEM_BYTES = 64 * 1024 * 1024
VMEM_LIMIT = V7X_VMEM_BYTES - 8 * 1024 * 1024

ADAM_LR = 0.001
ADAM_B1 = 0.9
ADAM_B2 = 0.999
ADAM_EPS = 1e-08
ADAM_WD = 0.01
ADAM_STEP = 10

NT_DIMS = (((1,), (1,)), ((), ()))
TN_DIMS = (((0,), (0,)), ((), ()))


def _params(*sem):
    return pltpu.CompilerParams(dimension_semantics=sem, vmem_limit_bytes=VMEM_LIMIT)


def _sds(shape, dtype):
    return jax.ShapeDtypeStruct(tuple(shape), dtype)


def _row_tile(rows, want):
    t = min(rows, want)
    assert rows % t == 0, (rows, want)
    return t


def norm_mm(x, g, w4, layer, out_dtype, name):
    s_len, d = x.shape
    ns = w4.shape[-1]
    tm = _row_tile(s_len, 512)

    def body(x_ref, g_ref, w_ref, h_ref, y_ref):
        xf = x_ref[...]
        r = lax.rsqrt(jnp.mean(xf * xf, axis=-1, keepdims=True) + EPS)
        h = ((xf * r) * g_ref[...]).astype(BF16)
        h_ref[...] = h
        for s in range(N_CHIPS):
            y_ref[:, s * ns:(s + 1) * ns] = jnp.dot(h, w_ref[s], preferred_element_type=F32).astype(y_ref.dtype)

    return pl.pallas_call(
        body, name=name, grid=(s_len // tm,),
        in_specs=[pl.BlockSpec((tm, d), lambda i: (i, 0)),
                  pl.BlockSpec((1, d), lambda i: (0, 0)),
                  pl.BlockSpec((N_CHIPS, None, d, ns), lambda i: (0, layer, 0, 0))],
        out_specs=[pl.BlockSpec((tm, d), lambda i: (i, 0)),
                   pl.BlockSpec((tm, N_CHIPS * ns), lambda i: (i, 0))],
        out_shape=[_sds((s_len, d), BF16), _sds((s_len, N_CHIPS * ns), out_dtype)],
        compiler_params=_params("arbitrary"),
    )(x, g, w4)


def mm_res(a, w4, layer, res, name, relu2=False):
    s_len, k = a.shape
    kq, n = w4.shape[-2:]
    assert kq * N_CHIPS == k
    tm = _row_tile(s_len, 256 if k > 1024 else 512)

    def body(a_ref, w0, w1, w2, w3, r_ref, o_ref):
        acc = r_ref[...]
        for s, w_ref in enumerate((w0, w1, w2, w3)):
            av = a_ref[:, s * kq:(s + 1) * kq]
            if relu2:
                t = jnp.maximum(av.astype(F32), 0.0)
                av = (t * t).astype(BF16)
            acc = acc + jnp.dot(av, w_ref[...], preferred_element_type=F32)
        o_ref[...] = acc

    def wspec(s):
        return pl.BlockSpec((None, None, kq, n), lambda i: (s, layer, 0, 0))

    return pl.pallas_call(
        body, name=name, grid=(s_len // tm,),
        in_specs=[pl.BlockSpec((tm, k), lambda i: (i, 0)), wspec(0), wspec(1), wspec(2), wspec(3),
                  pl.BlockSpec((tm, n), lambda i: (i, 0))],
        out_specs=pl.BlockSpec((tm, n), lambda i: (i, 0)),
        out_shape=_sds((s_len, n), F32),
        compiler_params=_params("arbitrary"),
    )(a, w4, w4, w4, w4, res)


def mm_res_t(pieces, w4, layer, res, name):
    s_len = res.shape[0]
    kq, n = w4.shape[-2:]
    rows = pieces[0].shape[0]
    assert rows % kq == 0 and rows * len(pieces) == kq * N_CHIPS
    tm = _row_tile(s_len, 512)
    n_p = len(pieces)

    def body(*refs):
        p_refs, w_refs, (r_ref, o_ref) = refs[:n_p], refs[n_p:n_p + N_CHIPS], refs[n_p + N_CHIPS:]
        acc = r_ref[...]
        for s in range(N_CHIPS):
            p, off = divmod(s * kq, rows)
            acc = acc + lax.dot_general(p_refs[p][off:off + kq, :], w_refs[s][...], TN_DIMS,
                                        preferred_element_type=F32)
        o_ref[...] = acc

    def wspec(s):
        return pl.BlockSpec((None, None, kq, n), lambda i: (s, layer, 0, 0))

    return pl.pallas_call(
        body, name=name, grid=(s_len // tm,),
        in_specs=[pl.BlockSpec((rows, tm), lambda i: (0, i))] * n_p + [wspec(s) for s in range(N_CHIPS)]
        + [pl.BlockSpec((tm, n), lambda i: (i, 0))],
        out_specs=pl.BlockSpec((tm, n), lambda i: (i, 0)),
        out_shape=_sds((s_len, n), F32),
        compiler_params=_params("arbitrary"),
    )(*pieces, w4, w4, w4, w4, res)


def dw_nn(pieces, b, name):
    s_len, n = b.shape
    rows = pieces[0].shape[0]
    n_p = len(pieces)
    k = rows * n_p
    ts = _row_tile(s_len, 2048)
    n_s = s_len // ts

    def body(*refs):
        p_refs, (b_ref, o_ref, acc_ref) = refs[:n_p], refs[n_p:]
        s = pl.program_id(0)
        bv = b_ref[...]
        for p in range(n_p):
            part = jnp.dot(p_refs[p][...], bv, preferred_element_type=F32)
            at = slice(p * rows, (p + 1) * rows)
            if n_s == 1:
                o_ref[at, :] = part.astype(BF16)
                continue

            @pl.when(s == 0)
            def _():
                acc_ref[at, :] = part

            @pl.when((s > 0) & (s < n_s - 1))
            def _():
                acc_ref[at, :] += part

            @pl.when(s == n_s - 1)
            def _():
                o_ref[at, :] = (acc_ref[at, :] + part).astype(BF16)

    out = pl.pallas_call(
        body, name=name, grid=(n_s,),
        in_specs=[pl.BlockSpec((rows, ts), lambda s: (0, s))] * n_p + [pl.BlockSpec((ts, n), lambda s: (s, 0))],
        out_specs=pl.BlockSpec((k, n), lambda s: (0, 0)), out_shape=_sds((k, n), BF16),
        scratch_shapes=[pltpu.VMEM((k, n), F32)],
        compiler_params=_params("arbitrary"),
    )(*pieces, b)
    return out.reshape(N_CHIPS, k // N_CHIPS, n)


def mm_nt(dy, w4, layer, name, transposed=False):
    s_len, n = dy.shape
    mq = w4.shape[-2]
    tm = _row_tile(s_len, 512)

    def body(d_ref, w0, w1, w2, w3, o_ref):
        dv = d_ref[...]
        for s, w_ref in enumerate((w0, w1, w2, w3)):
            if transposed:
                o_ref[s * mq:(s + 1) * mq, :] = lax.dot_general(
                    w_ref[...], dv, NT_DIMS, preferred_element_type=F32).astype(BF16)
            else:
                o_ref[:, s * mq:(s + 1) * mq] = lax.dot_general(
                    dv, w_ref[...], NT_DIMS, preferred_element_type=F32).astype(BF16)

    def wspec(s):
        return pl.BlockSpec((None, None, mq, n), lambda i: (s, layer, 0, 0))

    m = N_CHIPS * mq
    return pl.pallas_call(
        body, name=name, grid=(s_len // tm,),
        in_specs=[pl.BlockSpec((tm, n), lambda i: (i, 0)), wspec(0), wspec(1), wspec(2), wspec(3)],
        out_specs=pl.BlockSpec((m, tm), lambda i: (0, i)) if transposed else pl.BlockSpec((tm, m), lambda i: (i, 0)),
        out_shape=_sds((m, s_len) if transposed else (s_len, m), BF16),
        compiler_params=_params("arbitrary"),
    )(dy, w4, w4, w4, w4)


def mm_nt_relu2_bwd(dy, w4, layer, a, name):
    s_len, n = dy.shape
    mq = w4.shape[-2]
    tm = _row_tile(s_len, 512)

    def body(d_ref, w_ref, a_ref, o_ref):
        dv = d_ref[...]
        for s in range(N_CHIPS):
            cols = slice(s * mq, (s + 1) * mq)
            dz = lax.dot_general(dv, w_ref[s], NT_DIMS, preferred_element_type=F32)
            o_ref[:, cols] = (dz * (2.0 * jnp.maximum(a_ref[:, cols].astype(F32), 0.0))).astype(BF16)

    return pl.pallas_call(
        body, name=name, grid=(s_len // tm,),
        in_specs=[pl.BlockSpec((tm, n), lambda i: (i, 0)),
                  pl.BlockSpec((N_CHIPS, None, mq, n), lambda i: (0, layer, 0, 0)),
                  pl.BlockSpec((tm, N_CHIPS * mq), lambda i: (i, 0))],
        out_specs=pl.BlockSpec((tm, N_CHIPS * mq), lambda i: (i, 0)),
        out_shape=_sds((s_len, N_CHIPS * mq), BF16),
        compiler_params=_params("arbitrary"),
    )(dy, w4, a)


def dx_norm(dy, w4, layer, x, g, dres, name):
    s_len, d = x.shape
    ns = w4.shape[-1]
    tm = _row_tile(s_len, 512)

    def body(dy_ref, w_ref, x_ref, g_ref, dr_ref, dx_ref, dxb_ref, dg_ref):
        i = pl.program_id(0)
        dh = lax.dot_general(dy_ref[:, 0:ns], w_ref[0], NT_DIMS, preferred_element_type=F32)
        for s in range(1, N_CHIPS):
            dh = dh + lax.dot_general(dy_ref[:, s * ns:(s + 1) * ns], w_ref[s], NT_DIMS,
                                      preferred_element_type=F32)
        xf = x_ref[...]
        r = lax.rsqrt(jnp.mean(xf * xf, axis=-1, keepdims=True) + EPS)
        xhat = xf * r
        dg_part = jnp.sum(dh * xhat, axis=0, keepdims=True)

        @pl.when(i == 0)
        def _():
            dg_ref[...] = dg_part

        @pl.when(i > 0)
        def _():
            dg_ref[...] += dg_part

        dxh = dh * g_ref[...]
        dx = dr_ref[...] + r * (dxh - xhat * jnp.mean(dxh * xhat, axis=-1, keepdims=True))
        dx_ref[...] = dx
        dxb_ref[...] = dx.astype(BF16)

    row = pl.BlockSpec((tm, d), lambda i: (i, 0))
    vec = pl.BlockSpec((1, d), lambda i: (0, 0))
    return pl.pallas_call(
        body, name=name, grid=(s_len // tm,),
        in_specs=[pl.BlockSpec((tm, N_CHIPS * ns), lambda i: (i, 0)),
                  pl.BlockSpec((N_CHIPS, None, d, ns), lambda i: (0, layer, 0, 0)), row, vec, row],
        out_specs=[row, row, vec],
        out_shape=[_sds((s_len, d), F32), _sds((s_len, d), BF16), _sds((1, d), F32)],
        compiler_params=_params("arbitrary"),
    )(dy, w4, x, g, dres)


def dw_mm(a, b, name, col_sharded, relu2=False):
    s_len, k = a.shape
    n = b.shape[1]
    ts = _row_tile(s_len, 2048)
    tk = min(k, 1024)
    tn = n // N_CHIPS if col_sharded else min(n, 1024)
    n_s = s_len // ts

    def body(a_ref, b_ref, o_ref, acc_ref):
        s = pl.program_id(2)
        av = a_ref[...]
        if relu2:
            t = jnp.maximum(av.astype(F32), 0.0)
            av = (t * t).astype(BF16)
        part = lax.dot_general(av, b_ref[...], TN_DIMS, preferred_element_type=F32)
        if n_s == 1:
            o_ref[...] = part.astype(BF16)
            return

        @pl.when(s == 0)
        def _():
            acc_ref[...] = part

        @pl.when((s > 0) & (s < n_s - 1))
        def _():
            acc_ref[...] += part

        @pl.when(s == n_s - 1)
        def _():
            o_ref[...] = (acc_ref[...] + part).astype(BF16)

    if col_sharded:
        out_shape = _sds((N_CHIPS, k, tn), BF16)
        out_spec = pl.BlockSpec((None, tk, tn), lambda i, j, s: (j, i, 0))
    else:
        out_shape = _sds((N_CHIPS, k // N_CHIPS, n), BF16)
        rows_per = k // N_CHIPS
        assert tk % rows_per == 0 or rows_per % tk == 0
        if tk >= rows_per:
            out_shape = _sds((k, n), BF16)
            out_spec = pl.BlockSpec((tk, tn), lambda i, j, s: (i, j))
        else:
            per = rows_per // tk
            out_spec = pl.BlockSpec((None, tk, tn), lambda i, j, s: (i // per, i % per, j))

    out = pl.pallas_call(
        body, name=name, grid=(k // tk, n // tn, n_s),
        in_specs=[pl.BlockSpec((ts, tk), lambda i, j, s: (s, i)),
                  pl.BlockSpec((ts, tn), lambda i, j, s: (s, j))],
        out_specs=out_spec, out_shape=out_shape,
        scratch_shapes=[pltpu.VMEM((tk, tn), F32)],
        compiler_params=_params("arbitrary", "arbitrary", "arbitrary"),
    )(a, b)
    if not col_sharded:
        out = out.reshape(N_CHIPS, k // N_CHIPS, n)
    return out


def ew(fn, ins, out_dtypes, name, tile_rows=256):
    rows, cols = ins[0].shape
    for a in ins:
        assert a.shape == (rows, cols), (name, a.shape, rows, cols)
    tr = rows if (rows <= tile_rows or rows % tile_rows) else tile_rows
    n_in = len(ins)

    def body(*refs):
        outs = fn(*[r[...] for r in refs[:n_in]])
        for o_ref, val in zip(refs[n_in:], outs):
            o_ref[...] = val.astype(o_ref.dtype)

    spec = pl.BlockSpec((tr, cols), lambda i: (i, 0))
    return pl.pallas_call(
        body, name=name, grid=(rows // tr,),
        in_specs=[spec] * n_in, out_specs=[spec] * len(out_dtypes),
        out_shape=[_sds((rows, cols), dt) for dt in out_dtypes],
        compiler_params=_params("arbitrary"),
    )(*ins)


def adamw(w, g, m, v, name):
    shape = w.shape
    cols = shape[-1]
    two_d = lambda a: a.reshape(-1, cols)

    def fn(wv, gv, mv, vv):
        m_new = ADAM_B1 * mv + (1.0 - ADAM_B1) * gv
        v_new = ADAM_B2 * vv + (1.0 - ADAM_B2) * (gv * gv)
        m_hat = m_new / (1.0 - ADAM_B1 ** ADAM_STEP)
        v_hat = v_new / (1.0 - ADAM_B2 ** ADAM_STEP)
        delta = -ADAM_LR * (m_hat / (jnp.sqrt(v_hat) + ADAM_EPS) + ADAM_WD * wv)
        return delta, m_new, v_new

    d, mn, vn = ew(fn, [two_d(w), two_d(g), two_d(m), two_d(v)], [F32, F32, F32], name)
    return d.reshape(shape), mn.reshape(shape), vn.reshape(shape)


def rope_tables(s_len):
    def angles(pos, dim):
        freqs = ROPE_THETA ** (-jnp.arange(0, dim, 2, dtype=F32) / dim)
        ang = pos.astype(F32)[:, None] * freqs[None, :]
        return jnp.cos(ang), jnp.sin(ang)

    pos = jnp.arange(s_len)
    rows = s_len // GRID_W
    row_idx = jnp.repeat(jnp.arange(rows), GRID_W)
    col_idx = jnp.tile(jnp.arange(GRID_W), rows)
    c1, s1 = angles(pos, HEAD_DIM)
    cr, sr = angles(row_idx, HEAD_DIM // 2)
    cc, sc = angles(col_idx, HEAD_DIM // 2)
    cos1 = jnp.tile(jnp.concatenate([c1, c1], -1), (1, 2))
    sin1 = jnp.tile(jnp.concatenate([-s1, s1], -1), (1, 2))
    cos2 = jnp.tile(jnp.concatenate([cr, cr, cc, cc], -1), (1, 2))
    sin2 = jnp.tile(jnp.concatenate([-sr, sr, -sc, sc], -1), (1, 2))
    return cos1, sin1, cos2, sin2


def _lane_iota(rows):
    return lax.broadcasted_iota(jnp.int32, (rows, LANES), 1)


def _swap(x, dist, lane):
    return jnp.where((lane & dist) != 0, pltpu.roll(x, dist, 1), pltpu.roll(x, LANES - dist, 1))


def _head_ones():
    r = lax.broadcasted_iota(jnp.int32, (LANES, LANES), 0) // HEAD_DIM
    c = lax.broadcasted_iota(jnp.int32, (LANES, LANES), 1) // HEAD_DIM
    return (r == c).astype(BF16)


def _head_sum(t, ones):
    hi = t.astype(BF16)
    lo = (t - hi.astype(F32)).astype(BF16)
    return (jnp.dot(hi, ones, preferred_element_type=F32) + jnp.dot(lo, ones, preferred_element_type=F32))


Q_SCALE = HEAD_DIM ** -0.5
LOG2E = 1.4426950408889634
LN2 = 0.6931471805599453
CHUNK_KIND = ["qa"] * 4 + ["ka", "va"] + ["qb"] * 4 + ["kb", "vb"]
QA_COL, KA_COL, QB_COL, KB_COL = 0, 512, 768, 1280


def prep_fwd(proj, tabs, qn_g, kn_g, name):
    s_len, width = proj.shape
    ts = _row_tile(s_len, 512)
    cos1, sin1, cos2, sin2 = tabs

    def body(p_ref, c1_ref, s1_ref, c2_ref, s2_ref, qg_ref, kg_ref, o_ref, kv_ref):
        lane = _lane_iota(ts)
        ones = _head_ones()
        c1, s1, c2, s2 = c1_ref[...], s1_ref[...], c2_ref[...], s2_ref[...]
        n_kv = 0
        for cb, kind in enumerate(CHUNK_KIND):
            x = p_ref[:, cb * LANES:(cb + 1) * LANES]
            if kind in ("qa", "ka"):
                y = x * c1 + _swap(x, 32, lane) * s1
            elif kind in ("qb", "kb"):
                gain = qg_ref[...] if kind == "qb" else kg_ref[...]
                ms = _head_sum(x * x, ones) * (1.0 / HEAD_DIM)
                xn = (x * lax.rsqrt(ms + EPS)) * gain
                y = xn * c2 + _swap(xn, 16, lane) * s2
            else:
                y = x
            if kind in ("qa", "qb"):
                y = y * (Q_SCALE * LOG2E)
            else:
                kv_ref[:, n_kv * LANES:(n_kv + 1) * LANES] = y.astype(BF16)
                n_kv += 1
            o_ref[cb * LANES:(cb + 1) * LANES, :] = y.T.astype(BF16)

    tab = pl.BlockSpec((ts, LANES), lambda i: (i, 0))
    vec = pl.BlockSpec((1, LANES), lambda i: (0, 0))
    return pl.pallas_call(
        body, name=name, grid=(s_len // ts,),
        in_specs=[pl.BlockSpec((ts, width), lambda i: (i, 0)), tab, tab, tab, tab, vec, vec],
        out_specs=[pl.BlockSpec((width, ts), lambda i: (0, i)), pl.BlockSpec((ts, 4 * LANES), lambda i: (i, 0))],
        out_shape=[_sds((width, s_len), BF16), _sds((s_len, 4 * LANES), BF16)],
        compiler_params=_params("arbitrary"),
    )(proj, cos1, sin1, cos2, sin2, qn_g, kn_g)


def prep_bwd(proj, dqa, dka, dva, dqb, dkb, dvb, tabs, qn_g, kn_g, name):
    s_len, width = proj.shape
    ts = _row_tile(s_len, 256)
    cos1, sin1, cos2, sin2 = tabs

    def body(p_ref, dqa_ref, dka_ref, dva_ref, dqb_ref, dkb_ref, dvb_ref,
             c1_ref, s1_ref, c2_ref, s2_ref, qg_ref, kg_ref, o_ref, dqg_ref, dkg_ref):
        i = pl.program_id(0)
        lane = _lane_iota(ts)
        c1, s1, c2, s2 = c1_ref[...], s1_ref[...], c2_ref[...], s2_ref[...]

        def rope_t(dy, cos, sin, dist):
            return dy * cos + _swap(dy * sin, dist, lane)

        ones = _head_ones()

        def norm_bwd(dy, x, gain):
            r = lax.rsqrt(_head_sum(x * x, ones) * (1.0 / HEAD_DIM) + EPS)
            xhat = x * r
            dgain = jnp.sum(dy * xhat, axis=0, keepdims=True)
            dxh = dy * gain
            dx = r * (dxh - xhat * (_head_sum(dxh * xhat, ones) * (1.0 / HEAD_DIM)))
            return dx, dgain

        dqg = jnp.zeros((1, LANES), F32)
        dkg = jnp.zeros((1, LANES), F32)
        for cb, kind in enumerate(CHUNK_KIND):
            cols = slice(cb * LANES, (cb + 1) * LANES)
            if kind == "qa":
                dx = rope_t(dqa_ref[cols, :].T * Q_SCALE, c1, s1, 32)
            elif kind == "ka":
                dx = rope_t(dka_ref[...], c1, s1, 32)
            elif kind == "va":
                dx = dva_ref[...]
            elif kind == "qb":
                qcols = slice((cb - 6) * LANES, (cb - 5) * LANES)
                dy = rope_t(dqb_ref[qcols, :].T * Q_SCALE, c2, s2, 16)
                dx, dgain = norm_bwd(dy, p_ref[:, cols], qg_ref[...])
                dqg = dqg + dgain
            elif kind == "kb":
                dy = rope_t(dkb_ref[...], c2, s2, 16)
                dx, dgain = norm_bwd(dy, p_ref[:, cols], kg_ref[...])
                dkg = dkg + dgain
            else:
                dx = dvb_ref[...]
            o_ref[:, cols] = dx.astype(BF16)

        @pl.when(i == 0)
        def _():
            dqg_ref[...] = dqg
            dkg_ref[...] = dkg

        @pl.when(i > 0)
        def _():
            dqg_ref[...] += dqg
            dkg_ref[...] += dkg

    tab = pl.BlockSpec((ts, LANES), lambda i: (i, 0))
    vec = pl.BlockSpec((1, LANES), lambda i: (0, 0))

    def dq_spec(dq):
        per = dq.shape[2] // ts
        return pl.BlockSpec((None, 4 * LANES, ts), lambda i: (i // per, 0, i % per))

    return pl.pallas_call(
        body, name=name, grid=(s_len // ts,),
        in_specs=([pl.BlockSpec((ts, width), lambda i: (i, 0)), dq_spec(dqa), tab, tab, dq_spec(dqb), tab, tab]
                  + [tab] * 4 + [vec, vec]),
        out_specs=[pl.BlockSpec((ts, width), lambda i: (i, 0)), vec, vec],
        out_shape=[_sds((s_len, width), BF16), _sds((1, LANES), F32), _sds((1, LANES), F32)],
        compiler_params=_params("arbitrary"),
    )(proj, dqa, dka, dva, dqb, dkb, dvb, cos1, sin1, cos2, sin2, qn_g, kn_g)


NEG = -1e30
GROUP = 4
KV_HEADS = 2
GROUP_W = GROUP * HEAD_DIM
LSE_ROWS = 8
ONES_ROWS = 16


def _pos_mask_t(k_start, q_start, s_len, tk, tq):
    kpos = k_start + lax.broadcasted_iota(jnp.int32, (tk, tq), 0)
    qpos = q_start + lax.broadcasted_iota(jnp.int32, (tk, tq), 1)
    return (jnp.abs(kpos - qpos) <= BLOCK) & (kpos >= 0) & (kpos < s_len)


def flash_fwd_t(qkv_t, kv_tok, q_rb, k_i, v_rb, sink, window, name, comm=None):
    s_len = qkv_t.shape[1]
    if window:
        tq = _row_tile(s_len, 512)
        tk = 2 * BLOCK
        assert tq == 2 * tk, "the band parts below are written for query blocks of two key blocks"
        n_kv = 2
    else:
        tq, tk = _row_tile(s_len, 1024), _row_tile(s_len, 4096)
        n_kv = s_len // tk
    n_i = s_len // tq
    c_ins, c_outs, c_remote = comm if comm else ([], [], [])
    n_main = 6 if window else 3

    def body(*refs):
        main, c_in_refs = refs[:n_main], refs[n_main:n_main + len(c_ins)]
        rest = refs[n_main + len(c_ins):]
        (o_ref, lse_ref), c_out_refs = rest[:2], rest[2:2 + len(c_outs)]
        m_sc, acc_sc = rest[2 + len(c_outs):4 + len(c_outs)]
        c_sems = rest[4 + len(c_outs):]
        if window:
            sink_ref, q_ref, k_ref, v_ref, kc_ref, vc_ref = main
        else:
            q_ref, k_ref, v_ref = main
        h, i, t = pl.program_id(0), pl.program_id(1), pl.program_id(2)
        if comm:
            @pl.when((h == 0) & (i == 0) & (t == 0))
            def _():
                _exchange_start(c_remote, c_in_refs, c_out_refs, *c_sems)

        @pl.when(t == 0)
        def _():
            for g in range(GROUP):
                acc_sc[g, 0:HEAD_DIM, :] = jnp.zeros((HEAD_DIM, tq), F32)
                if window:
                    m_sc[g] = jnp.full((1, tq), sink_ref[h * GROUP + g] * LOG2E, F32)
                    acc_sc[g, HEAD_DIM:, :] = jnp.ones((ONES_ROWS, tq), F32)
                else:
                    m_sc[g] = jnp.full((1, tq), NEG, F32)
                    acc_sc[g, HEAD_DIM:, :] = jnp.zeros((ONES_ROWS, tq), F32)

        def tile(k_src, v_src, n_keys, key_pos, q_lo, q_hi):
            qs = slice(q_lo, q_hi)
            k = k_src[...]
            v_t = jnp.concatenate([v_src[...], jnp.ones((ONES_ROWS, n_keys), BF16)], axis=0)
            if window:
                mask = _pos_mask_t(key_pos, i * tq + q_lo, s_len, n_keys, q_hi - q_lo)
            s_next = jnp.dot(k, q_ref[0:HEAD_DIM, qs], preferred_element_type=F32)
            for g in range(GROUP):
                s_t = s_next
                if g + 1 < GROUP:
                    s_next = jnp.dot(k, q_ref[(g + 1) * HEAD_DIM:(g + 2) * HEAD_DIM, qs],
                                     preferred_element_type=F32)
                if window:
                    s_t = jnp.where(mask, s_t, NEG)
                m_prev = m_sc[g, :, qs]
                m_new = jnp.maximum(m_prev, jnp.max(s_t, axis=0, keepdims=True))
                alpha = jnp.exp2(m_prev - m_new)
                p_t = jnp.exp2(s_t - m_new)
                acc_sc[g, :, qs] = alpha * acc_sc[g, :, qs] + jnp.dot(v_t, p_t.astype(BF16),
                                                                     preferred_element_type=F32)
                m_sc[g, :, qs] = m_new

        if window:
            @pl.when(t == 0)
            def _():
                tile(k_ref, v_ref, tk, i * tq, 0, tq - BLOCK)
                tile(kc_ref, vc_ref, BLOCK, i * tq - BLOCK, 0, BLOCK)

            @pl.when(t == 1)
            def _():
                tile(k_ref, v_ref, tk, i * tq + tk, BLOCK, tq)
                tile(kc_ref, vc_ref, BLOCK, i * tq + tq, tq - BLOCK, tq)
        else:
            tile(k_ref, v_ref, tk, t * tk, 0, tq)

        @pl.when(t == n_kv - 1)
        def _():
            for g in range(GROUP):
                l = acc_sc[g, HEAD_DIM:HEAD_DIM + 1, :]
                o_ref[g * HEAD_DIM:(g + 1) * HEAD_DIM, :] = (acc_sc[g, 0:HEAD_DIM, :] / l).astype(BF16)
                lse_ref[g * LSE_ROWS:(g + 1) * LSE_ROWS, :] = jnp.broadcast_to(
                    m_sc[g] + jnp.log(l) * LOG2E, (LSE_ROWS, tq))

        if comm:
            @pl.when((h == KV_HEADS - 1) & (i == n_i - 1) & (t == n_kv - 1))
            def _():
                _exchange_finish(c_remote, c_in_refs, c_out_refs, *c_sems)

    kv_blk = (lambda i, t: 2 * i + t) if window else (lambda i, t: t)
    hbm = pl.BlockSpec(memory_space=pl.ANY)
    in_specs = [pl.BlockSpec((GROUP_W, tq), lambda h, i, t: (q_rb + h, i)),
                pl.BlockSpec((None, tk, HEAD_DIM), lambda h, i, t: (k_i + h, kv_blk(i, t), 0)),
                pl.BlockSpec((HEAD_DIM, tk), lambda h, i, t: (v_rb + h, kv_blk(i, t)))]
    args = [qkv_t, kv_tok, qkv_t]
    if window:
        corner = lambda i, t: jnp.clip((tq // BLOCK) * i - 1 + (tq // BLOCK + 1) * t, 0, s_len // BLOCK - 1)
        in_specs = ([pl.BlockSpec(memory_space=pltpu.SMEM)] + in_specs
                    + [pl.BlockSpec((None, BLOCK, HEAD_DIM), lambda h, i, t: (k_i + h, corner(i, t), 0)),
                       pl.BlockSpec((HEAD_DIM, BLOCK), lambda h, i, t: (v_rb + h, corner(i, t)))])
        args = [sink] + args + [kv_tok, qkv_t]
    return pl.pallas_call(
        body, name=name, grid=(KV_HEADS, n_i, n_kv),
        in_specs=in_specs + [hbm] * len(c_ins),
        out_specs=[pl.BlockSpec((GROUP_W, tq), lambda h, i, t: (h, i)),
                   pl.BlockSpec((GROUP * LSE_ROWS, tq), lambda h, i, t: (h, i))] + [hbm] * len(c_outs),
        out_shape=[_sds((KV_HEADS * GROUP_W, s_len), BF16),
                   _sds((KV_HEADS * GROUP * LSE_ROWS, s_len), F32)] + list(c_outs),
        scratch_shapes=[pltpu.VMEM((GROUP, 1, tq), F32),
                        pltpu.VMEM((GROUP, HEAD_DIM + ONES_ROWS, tq), F32)] + _exchange_sems(c_remote),
        compiler_params=_params("arbitrary", "arbitrary", "arbitrary"),
    )(*args, *c_ins)


def flash_bwd_t(qkv_t, kv_tok, o_t, do_t, lse, q_rb, k_i, v_i, k_rb, do_rb, sink, window, name, comm=None):
    s_len = qkv_t.shape[1]
    if window:
        tq = _row_tile(s_len, 512)
        tk = 2 * BLOCK
        assert tq == 2 * tk, "the band parts below are written for query blocks of two key blocks"
        n_q = 2
    else:
        tq, tk = _row_tile(s_len, 2048), _row_tile(s_len, 1024)
        n_q = s_len // tq
    n_qb = s_len // tq
    n_j = s_len // tk
    c_ins, c_outs, c_remote = comm if comm else ([], [], [])
    n_main = 8 if window else 7
    n_out = 4 if window else 3

    def body(*refs):
        main, c_in_refs = refs[:n_main], refs[n_main:n_main + len(c_ins)]
        rest = refs[n_main + len(c_ins):]
        outs, c_out_refs = rest[:n_out], rest[n_out:n_out + len(c_outs)]
        dk_sc, dv_sc = rest[n_out + len(c_outs):n_out + len(c_outs) + 2]
        c_sems = rest[n_out + len(c_outs) + 2:]
        if window:
            sink_ref, q_ref, k_ref, v_ref, kt_ref, o_ref, do_ref, lse_ref = main
            dq_ref, dk_ref, dv_ref, dsink_ref = outs
        else:
            q_ref, k_ref, v_ref, kt_ref, o_ref, do_ref, lse_ref = main
            dq_ref, dk_ref, dv_ref = outs
        h, j, t = pl.program_id(0), pl.program_id(1), pl.program_id(2)
        q_blk = (j + 1) // 2 - 1 + t if window else t
        if comm:
            @pl.when((h == 0) & (j == 0) & (t == 0))
            def _():
                _exchange_start(c_remote, c_in_refs, c_out_refs, *c_sems)

        @pl.when((j == 0) & (t == 0))
        def _():
            dq_ref[...] = jnp.zeros(dq_ref.shape, F32)
            if window:
                dsink_ref[...] = jnp.zeros((8, LANES), F32)

        @pl.when(t == 0)
        def _():
            dk_sc[...] = jnp.zeros((tk, HEAD_DIM), F32)
            dv_sc[...] = jnp.zeros((tk, HEAD_DIM), F32)

        def tile(k_lo, k_hi, q_lo, q_hi, sink_lo=0, sink_hi=0):
            ks, qs = slice(k_lo, k_hi), slice(q_lo, q_hi)
            k, v, k_t = k_ref[ks, :], v_ref[ks, :], kt_ref[:, ks]
            if window:
                mask = _pos_mask_t(j * tk + k_lo, q_blk * tq + q_lo, s_len, k_hi - k_lo, q_hi - q_lo)
                lane = lax.broadcasted_iota(jnp.int32, (8, LANES), 1)
                sink_tile = jnp.zeros((8, LANES), F32)
            dk_acc = dk_sc[ks, :]
            dv_acc = dv_sc[ks, :]
            for g in range(GROUP):
                rows = slice(g * HEAD_DIM, (g + 1) * HEAD_DIM)
                q_t, o_g, do_g = q_ref[rows, qs], o_ref[rows, qs], do_ref[rows, qs]
                s_t = jnp.dot(k, q_t, preferred_element_type=F32)
                if window:
                    s_t = jnp.where(mask, s_t, NEG)
                lse_row = lse_ref[g * LSE_ROWS:g * LSE_ROWS + 1, qs]
                p_t = jnp.exp2(s_t - lse_row)
                delta = jnp.sum(do_g.astype(F32) * o_g.astype(F32), axis=0, keepdims=True)
                dp_t = jnp.dot(v, do_g, preferred_element_type=F32)
                ds_t = (p_t * (dp_t - delta)).astype(BF16)
                dv_acc = dv_acc + lax.dot_general(p_t.astype(BF16), do_g, NT_DIMS, preferred_element_type=F32)
                dk_acc = dk_acc + lax.dot_general(ds_t, q_t, NT_DIMS, preferred_element_type=F32)
                dq_ref[q_blk, rows, qs] += jnp.dot(k_t, ds_t, preferred_element_type=F32)
                if sink_hi > sink_lo:
                    at = slice(sink_lo - q_lo, sink_hi - q_lo)
                    p_sink = jnp.exp2(sink_ref[h * GROUP + g] * LOG2E - lse_row[:, at])
                    term = -jnp.sum(p_sink * delta[:, at], axis=1, keepdims=True)
                    sink_tile = jnp.where(lane == g, term, sink_tile)
            dk_sc[ks, :] = dk_acc
            dv_sc[ks, :] = dv_acc
            if sink_hi > sink_lo:
                dsink_ref[...] += sink_tile

        if window:
            parts = {(0, 0): (0, BLOCK, tq - BLOCK, tq), (0, 1): (0, tk, 0, tq - BLOCK, 0, tq - BLOCK),
                     (1, 0): (0, tk, BLOCK, tq, tq - BLOCK, tq), (1, 1): (tk - BLOCK, tk, 0, BLOCK)}
            for (parity, tt), part in parts.items():
                pl.when((q_blk >= 0) & (q_blk < n_qb) & (j % 2 == parity) & (t == tt))(
                    functools.partial(tile, *part))
        else:
            tile(0, tk, 0, tq)

        @pl.when(t == n_q - 1)
        def _():
            dk_ref[...] = dk_sc[...] * LN2
            dv_ref[...] = dv_sc[...]

        if comm:
            @pl.when((h == KV_HEADS - 1) & (j == n_j - 1) & (t == n_q - 1))
            def _():
                _exchange_finish(c_remote, c_in_refs, c_out_refs, *c_sems)

    if window:
        qb = lambda j, t: jnp.clip((j + 1) // 2 - 1 + t, 0, n_qb - 1)
    else:
        qb = lambda j, t: t
    hbm = pl.BlockSpec(memory_space=pl.ANY)
    in_specs = [pl.BlockSpec((GROUP_W, tq), lambda h, j, t: (q_rb + h, qb(j, t))),
                pl.BlockSpec((None, tk, HEAD_DIM), lambda h, j, t: (k_i + h, j, 0)),
                pl.BlockSpec((None, tk, HEAD_DIM), lambda h, j, t: (v_i + h, j, 0)),
                pl.BlockSpec((HEAD_DIM, tk), lambda h, j, t: (k_rb + h, j)),
                pl.BlockSpec((GROUP_W, tq), lambda h, j, t: (h, qb(j, t))),
                pl.BlockSpec((GROUP_W, tq), lambda h, j, t: (do_rb + h, qb(j, t))),
                pl.BlockSpec((GROUP * LSE_ROWS, tq), lambda h, j, t: (h, qb(j, t)))]
    args = [qkv_t, kv_tok, kv_tok, qkv_t, o_t, do_t, lse]
    kv_out = _sds((KV_HEADS, s_len, HEAD_DIM), F32)
    out_specs = [pl.BlockSpec((n_qb, GROUP_W, tq), lambda h, j, t: (0, h, 0)),
                 pl.BlockSpec((None, tk, HEAD_DIM), lambda h, j, t: (h, j, 0)),
                 pl.BlockSpec((None, tk, HEAD_DIM), lambda h, j, t: (h, j, 0))]
    out_shape = [_sds((n_qb, KV_HEADS * GROUP_W, tq), F32), kv_out, kv_out]
    if window:
        in_specs = [pl.BlockSpec(memory_space=pltpu.SMEM)] + in_specs
        args = [sink] + args
        out_specs.append(pl.BlockSpec((None, 8, LANES), lambda h, j, t: (h, 0, 0)))
        out_shape.append(_sds((KV_HEADS, 8, LANES), F32))
    return pl.pallas_call(
        body, name=name, grid=(KV_HEADS, n_j, n_q),
        in_specs=in_specs + [hbm] * len(c_ins), out_specs=out_specs + [hbm] * len(c_outs),
        out_shape=out_shape + list(c_outs),
        scratch_shapes=[pltpu.VMEM((tk, HEAD_DIM), F32), pltpu.VMEM((tk, HEAD_DIM), F32)]
        + _exchange_sems(c_remote),
        compiler_params=_params("arbitrary", "arbitrary", "arbitrary"),
    )(*args, *c_ins)


SGU_GROUPS = 8
SGU_CHUNK = 128
GELU_C = float(np.sqrt(2.0 / np.pi))
GELU_A = 0.044715


def _gelu_and_grad(x):
    x2 = x * x
    t = jnp.tanh(x * (GELU_C + (GELU_C * GELU_A) * x2))
    hx = 0.5 * x
    return hx + hx * t, (0.5 + 0.5 * t) + (hx * (1.0 - t * t)) * (GELU_C + (3.0 * GELU_C * GELU_A) * x2)


def _gelu(x):
    t = jnp.tanh(x * (GELU_C + (GELU_C * GELU_A) * (x * x)))
    hx = 0.5 * x
    return hx + hx * t


def _layernorm_stats(v):
    mu = jnp.mean(v, axis=-1, keepdims=True)
    var = jnp.mean(jnp.square(v - mu), axis=-1, keepdims=True)
    rstd = lax.rsqrt(var + EPS)
    return (v - mu) * rstd, rstd


def sgu_mid_fwd(zpre, ln_g, ln_b, ws, bsb, name):
    s_len, width = zpre.shape
    d = width // 2
    ts = _row_tile(s_len, 256)

    def body(z_ref, g_ref, b_ref, ws_ref, bs_ref, y_ref):
        z = _gelu(z_ref[...])
        u, v = z[:, :d], z[:, d:]
        vhat, _ = _layernorm_stats(v)
        vn = (vhat * g_ref[...] + b_ref[...]).astype(BF16)
        for n in range(ts // SGU_CHUNK):
            rows = slice(n * SGU_CHUNK, (n + 1) * SGU_CHUNK)
            for g in range(SGU_GROUPS):
                cols = slice(g * LANES, (g + 1) * LANES)
                mixed = jnp.dot(ws_ref[g], vn[rows, cols], preferred_element_type=F32) + bs_ref[g]
                y_ref[rows, cols] = (u[rows, cols] * mixed).astype(BF16)

    vec = pl.BlockSpec((1, d), lambda i: (0, 0))
    cube = pl.BlockSpec((SGU_GROUPS, SGU_CHUNK, SGU_CHUNK), lambda i: (0, 0, 0))
    return pl.pallas_call(
        body, name=name, grid=(s_len // ts,),
        in_specs=[pl.BlockSpec((ts, width), lambda i: (i, 0)), vec, vec, cube, cube],
        out_specs=pl.BlockSpec((ts, d), lambda i: (i, 0)),
        out_shape=_sds((s_len, d), BF16),
        compiler_params=_params("arbitrary"),
    )(zpre, ln_g, ln_b, ws, bsb)


def sgu_mid_bwd(zpre, dy, ln_g, ln_b, ws, wst, bsb, name):
    s_len, width = zpre.shape
    d = width // 2
    ts = _row_tile(s_len, 256)
    n_steps = s_len // ts

    def body(z_ref, dy_ref, g_ref, b_ref, ws_ref, wst_ref, bs_ref,
             dz_ref, dws_ref, dbs_ref, dg_ref, db_ref, du_sc, dvn_sc):
        i = pl.program_id(0)

        @pl.when(i == 0)
        def _():
            dws_ref[...] = jnp.zeros(dws_ref.shape, F32)
            dbs_ref[...] = jnp.zeros(dbs_ref.shape, F32)
            dg_ref[...] = jnp.zeros(dg_ref.shape, F32)
            db_ref[...] = jnp.zeros(db_ref.shape, F32)

        zp = z_ref[...]
        z, gp = _gelu_and_grad(zp)
        u, v = z[:, :d], z[:, d:]
        vhat, rstd = _layernorm_stats(v)
        gain = g_ref[...]
        vn = (vhat * gain + b_ref[...]).astype(BF16)
        dyf = dy_ref[...].astype(F32)
        for n in range(ts // SGU_CHUNK):
            rows = slice(n * SGU_CHUNK, (n + 1) * SGU_CHUNK)
            for g in range(SGU_GROUPS):
                cols = slice(g * LANES, (g + 1) * LANES)
                vt = vn[rows, cols]
                mixed = jnp.dot(ws_ref[g], vt, preferred_element_type=F32) + bs_ref[g]
                dyt = dyf[rows, cols]
                du_sc[rows, cols] = dyt * mixed
                dmixed = dyt * u[rows, cols]
                dmb = dmixed.astype(BF16)
                dvn_sc[rows, cols] = jnp.dot(wst_ref[g], dmb, preferred_element_type=F32)
                dws_ref[g] += lax.dot_general(dmb, vt, NT_DIMS, preferred_element_type=F32)
                dbs_ref[g] += dmixed
        dvn = dvn_sc[...]
        dg_ref[...] += jnp.sum(dvn * vhat, axis=0, keepdims=True)
        db_ref[...] += jnp.sum(dvn, axis=0, keepdims=True)
        dvh = dvn * gain
        dv = rstd * (dvh - jnp.mean(dvh, axis=-1, keepdims=True)
                     - vhat * jnp.mean(dvh * vhat, axis=-1, keepdims=True))
        dz_ref[:, :d] = (du_sc[...] * gp[:, :d]).astype(BF16)
        dz_ref[:, d:] = (dv * gp[:, d:]).astype(BF16)

        @pl.when(i == n_steps - 1)
        def _():
            for g in range(SGU_GROUPS):
                tot = jnp.sum(dbs_ref[g], axis=1, keepdims=True)
                dbs_ref[g] = jnp.broadcast_to(tot, (SGU_CHUNK, LANES))

    vec = pl.BlockSpec((1, d), lambda i: (0, 0))
    cube = pl.BlockSpec((SGU_GROUPS, SGU_CHUNK, SGU_CHUNK), lambda i: (0, 0, 0))
    cube_shape = _sds((SGU_GROUPS, SGU_CHUNK, SGU_CHUNK), F32)
    return pl.pallas_call(
        body, name=name, grid=(n_steps,),
        in_specs=[pl.BlockSpec((ts, width), lambda i: (i, 0)), pl.BlockSpec((ts, d), lambda i: (i, 0)),
                  vec, vec, cube, cube, cube],
        out_specs=[pl.BlockSpec((ts, width), lambda i: (i, 0)), cube, cube, vec, vec],
        out_shape=[_sds((s_len, width), BF16), cube_shape, cube_shape, _sds((1, d), F32), _sds((1, d), F32)],
        scratch_shapes=[pltpu.VMEM((ts, d), F32), pltpu.VMEM((ts, d), F32)],
        compiler_params=_params("arbitrary"),
    )(zpre, dy, ln_g, ln_b, ws, wst, bsb)


def loss_head(x, g, target, name):
    s_len, d = x.shape
    tm = _row_tile(s_len, 512)

    def body(x_ref, g_ref, t_ref, dx_ref, dxb_ref, dg_ref, loss_ref):
        i = pl.program_id(0)
        xf = x_ref[...]
        gain = g_ref[...]
        r = lax.rsqrt(jnp.mean(xf * xf, axis=-1, keepdims=True) + EPS)
        xhat = xf * r
        err = xhat * gain - t_ref[...]
        row = jnp.mean(err * err, axis=-1, keepdims=True)
        part = 0.5 * jnp.sum(row, axis=0, keepdims=True)
        dy = err * (1.0 / d)
        dg_part = jnp.sum(dy * xhat, axis=0, keepdims=True)

        @pl.when(i == 0)
        def _():
            dg_ref[...] = dg_part
            loss_ref[...] = jnp.broadcast_to(part, (8, LANES))

        @pl.when(i > 0)
        def _():
            dg_ref[...] += dg_part
            loss_ref[...] += jnp.broadcast_to(part, (8, LANES))

        dxh = dy * gain
        dx = r * (dxh - xhat * jnp.mean(dxh * xhat, axis=-1, keepdims=True))
        dx_ref[...] = dx
        dxb_ref[...] = dx.astype(BF16)

    row_spec = pl.BlockSpec((tm, d), lambda i: (i, 0))
    vec = pl.BlockSpec((1, d), lambda i: (0, 0))
    return pl.pallas_call(
        body, name=name, grid=(s_len // tm,),
        in_specs=[row_spec, vec, row_spec],
        out_specs=[row_spec, row_spec, vec, pl.BlockSpec((8, LANES), lambda i: (0, 0))],
        out_shape=[_sds((s_len, d), F32), _sds((s_len, d), BF16), _sds((1, d), F32), _sds((8, LANES), F32)],
        compiler_params=_params("arbitrary"),
    )(x, g, target)


FLIP_BITS = {"c": (0, 0, 1), "x": (1, 0, 0), "y": (0, 1, 0), "xy": (1, 1, 0),
             "xc": (1, 0, 1), "yc": (0, 1, 1), "xyc": (1, 1, 1)}
CHIP_FLIPS = ("x", "y", "xy")


def _flip(pos, name):
    return tuple(1 - p if bit else p for p, bit in zip(pos, FLIP_BITS[name]))


def _chip(pos):
    return 2 * pos[0] + pos[1]


def _me():
    return (lax.axis_index("x"), lax.axis_index("y"), lax.axis_index("c"))


def _exchange_copy(remote, k, in_refs, out_refs, send_sems, recv_sems, sender, receiver):
    ii, src_fn, oi, dst_fn, _ = remote[k]
    return pltpu.make_async_remote_copy(
        src_ref=src_fn(in_refs[ii], sender, receiver), dst_ref=dst_fn(out_refs[oi], sender),
        send_sem=send_sems.at[k], recv_sem=recv_sems.at[k], device_id=receiver, device_id_type=MESH)


def _exchange_start(remote, in_refs, out_refs, send_sems, recv_sems):
    me = _me()
    for k in range(len(remote)):
        _exchange_copy(remote, k, in_refs, out_refs, send_sems, recv_sems, me, _flip(me, remote[k][4])).start()


def _exchange_finish(remote, in_refs, out_refs, send_sems, recv_sems):
    me = _me()
    for k in range(len(remote)):
        _exchange_copy(remote, k, in_refs, out_refs, send_sems, recv_sems, _flip(me, remote[k][4]), me).wait_recv()
    for k in range(len(remote)):
        _exchange_copy(remote, k, in_refs, out_refs, send_sems, recv_sems, me, _flip(me, remote[k][4])).wait_send()


def _exchange_sems(remote):
    n = len(remote)
    return [pltpu.SemaphoreType.DMA((n,)), pltpu.SemaphoreType.DMA((n,))] if n else []


def exchange(name, ins, out_shapes, remote, local):
    n_in, n_out = len(ins), len(out_shapes)

    def body(*refs):
        in_refs, out_refs = refs[:n_in], refs[n_in:n_in + n_out]
        send_sems, recv_sems, local_sems = refs[n_in + n_out:]
        me = _me()
        stays = []
        for k, (ii, src_fn, oi, dst_fn) in enumerate(local):
            cp = pltpu.make_async_copy(src_fn(in_refs[ii], me), dst_fn(out_refs[oi], me), local_sems.at[k])
            cp.start()
            stays.append(cp)
        _exchange_start(remote, in_refs, out_refs, send_sems, recv_sems)
        _exchange_finish(remote, in_refs, out_refs, send_sems, recv_sems)
        for cp in stays:
            cp.wait()

    hbm = pl.BlockSpec(memory_space=pl.ANY)
    return pl.pallas_call(
        body, name=name,
        in_specs=[hbm] * n_in, out_specs=[hbm] * n_out, out_shape=list(out_shapes),
        scratch_shapes=[pltpu.SemaphoreType.DMA((max(len(remote), 1),)),
                        pltpu.SemaphoreType.DMA((max(len(remote), 1),)),
                        pltpu.SemaphoreType.DMA((max(len(local), 1),))],
        compiler_params=pltpu.CompilerParams(has_side_effects=True),
    )(*ins)


def staged_push(name, ins, out_shapes, jobs, n_alias=0):
    n_in, n_out = len(ins), len(out_shapes)
    n_copies = sum(len(dsts) for _, _, dsts in jobs)
    n_remote = sum(1 for _, _, dsts in jobs for d in dsts if d[2] is not None)

    def chunk_of(ii, src_fn):
        probe = _ShapeRef(ins[ii].shape, ins[ii].dtype)
        got = src_fn(probe, (0, 0, 0))
        return tuple(got.shape), got.dtype

    classes = []
    for ii, src_fn, _ in jobs:
        c = chunk_of(ii, src_fn)
        if c not in classes:
            classes.append(c)

    def body(*refs):
        in_refs, out_refs = refs[:n_in], refs[n_in:n_in + n_out]
        bufs = refs[n_in + n_out:n_in + n_out + len(classes)]
        load_sems, out_sems, recv_sems = refs[n_in + n_out + len(classes):]
        me = _me()
        pending = [[[], []] for _ in classes]
        used = [0] * len(classes)
        arrivals = []
        k = r = 0

        def begin_load(job):
            ii, src_fn, _ = job
            cls = classes.index(chunk_of(ii, src_fn))
            slot = used[cls] % 2
            used[cls] += 1
            for kind, cp in pending[cls][slot]:
                cp.wait_send() if kind == "remote" else cp.wait()
            pending[cls][slot] = []
            load = pltpu.make_async_copy(src_fn(in_refs[ii], me), bufs[cls].at[slot], load_sems.at[2 * cls + slot])
            load.start()
            return load, cls, slot

        nxt = begin_load(jobs[0])
        for n, (ii, src_fn, dsts) in enumerate(jobs):
            load, cls, slot = nxt
            load.wait()
            buf = bufs[cls].at[slot]
            sent = []
            for oi, dst_fn, flip in dsts:
                if flip is None:
                    cp = pltpu.make_async_copy(buf, dst_fn(out_refs[oi], me), out_sems.at[k])
                    cp.start()
                    sent.append(("local", cp))
                else:
                    peer = _flip(me, flip)
                    cp = pltpu.make_async_remote_copy(
                        src_ref=buf, dst_ref=dst_fn(out_refs[oi], me), send_sem=out_sems.at[k],
                        recv_sem=recv_sems.at[r], device_id=peer, device_id_type=MESH)
                    cp.start()
                    sent.append(("remote", cp))
                    arrivals.append((r, cls, oi, dst_fn, peer))
                    r += 1
                k += 1
            pending[cls][slot] = sent
            if n + 1 < len(jobs):
                nxt = begin_load(jobs[n + 1])
        for per_class in pending:
            for slot_list in per_class:
                for kind, cp in slot_list:
                    cp.wait_send() if kind == "remote" else cp.wait()
        for r, cls, oi, dst_fn, peer in arrivals:
            pltpu.make_async_remote_copy(
                src_ref=bufs[cls].at[0], dst_ref=dst_fn(out_refs[oi], peer), send_sem=out_sems.at[0],
                recv_sem=recv_sems.at[r], device_id=peer, device_id_type=MESH).wait_recv()

    hbm = pl.BlockSpec(memory_space=pl.ANY)
    return pl.pallas_call(
        body, name=name,
        in_specs=[hbm] * n_in, out_specs=[hbm] * n_out, out_shape=list(out_shapes),
        scratch_shapes=[pltpu.VMEM((2,) + shape, dtype) for shape, dtype in classes]
        + [pltpu.SemaphoreType.DMA((2 * len(classes),)), pltpu.SemaphoreType.DMA((max(n_copies, 1),)),
           pltpu.SemaphoreType.DMA((max(n_remote, 1),))],
        input_output_aliases={i: i for i in range(n_alias)},
        compiler_params=pltpu.CompilerParams(has_side_effects=True, vmem_limit_bytes=VMEM_LIMIT),
    )(*ins)


class _ShapeRef:
    def __init__(self, shape, dtype):
        self.shape, self.dtype = tuple(shape), dtype

    @property
    def at(self):
        return self

    def __getitem__(self, idx):
        idx = idx if isinstance(idx, tuple) else (idx,)
        shape = []
        for dim, i in zip(self.shape, idx):
            if isinstance(i, slice):
                shape.append(len(range(*i.indices(dim))))
            elif hasattr(i, "size") and hasattr(i, "start"):
                shape.append(i.size)
        shape += self.shape[len(idx):]
        return _ShapeRef(shape, self.dtype)


def gather_whole(shards, name):
    whole = lambda ref, sender, receiver=None: ref
    slot = lambda ref, sender: ref.at[_chip(sender)]
    remote = [(t, whole, t, slot, flip) for t in range(len(shards)) for flip in CHIP_FLIPS]
    local = [(t, whole, t, slot) for t in range(len(shards))]
    outs = [_sds((N_CHIPS,) + a.shape, a.dtype) for a in shards]
    return exchange(name, list(shards), outs, remote, local)


def _half_axis(shape):
    return 0 if shape[0] >= 2 else 1


def gather_halves_plan(shards):
    remote = []
    for t, a in enumerate(shards):
        ax = _half_axis(a.shape)
        half = lambda ref, sender, receiver=None, ax=ax: _half(ref, sender[2], ax)
        slot = lambda ref, sender, ax=ax: _half(ref.at[_chip(sender)], sender[2], ax)
        remote += [(t, half, t, slot, flip) for flip in CHIP_FLIPS]
    outs = [_sds((N_CHIPS,) + a.shape, a.dtype) for a in shards]
    return list(shards), outs, remote


def gather_halves_fill(got, shards, name):
    n_t = len(shards)
    jobs = []
    for t, a in enumerate(shards):
        layers = a.shape[0]
        for l in range(layers):
            jobs.append((n_t + t, lambda ref, me, l=l: ref.at[l],
                         [(t, lambda ref, sender, l=l: ref.at[_chip(sender), l], None)]))
        for flip in CHIP_FLIPS:
            if _half_axis(a.shape) == 0:
                n = layers // 2
                for j in range(n):
                    at = lambda ref, pos, flip=flip, j=j, n=n: ref.at[_chip(_flip(pos, flip)), pos[2] * n + j]
                    jobs.append((t, at, [(t, at, "c")]))
            else:
                rows = a.shape[1] // 2
                at = lambda ref, pos, flip=flip, rows=rows: ref.at[
                    _chip(_flip(pos, flip)), 0, pl.ds(pos[2] * rows, rows)]
                jobs.append((t, at, [(t, at, "c")]))
    outs = [_sds(g.shape, g.dtype) for g in got]
    return staged_push(name, list(got) + list(shards), outs, jobs, n_alias=n_t)


def add_to_all(plan, buf):
    ins, outs, remote = plan
    whole = lambda ref, sender, receiver=None: ref
    more = [(len(ins), whole, len(outs), (lambda ref, sender, f=f: ref.at[f]), flip)
            for f, flip in enumerate(FLIPS_BY_INDEX)]
    return list(ins) + [buf], list(outs) + [_sds((len(more),) + buf.shape, buf.dtype)], list(remote) + more


FLIPS_BY_INDEX = ("c", "y", "yc", "x", "xc", "xy", "xyc")


def sum_devices(own, got, name):
    rows = own.shape[0]
    tr = LANES if rows % LANES == 0 else rows
    me = (4 * lax.axis_index("x") + 2 * lax.axis_index("y") + lax.axis_index("c")).astype(jnp.int32).reshape(1)
    everyone = jnp.concatenate([own[None], got], axis=0)

    def body(me_ref, a0, a1, a2, a3, a4, a5, a6, a7, o_ref):
        o_ref[...] = ((a0[...] + a1[...]) + (a2[...] + a3[...])) + ((a4[...] + a5[...]) + (a6[...] + a7[...]))

    return pl.pallas_call(
        body, name=name,
        grid_spec=pltpu.PrefetchScalarGridSpec(
            num_scalar_prefetch=1, grid=(rows // tr,),
            in_specs=[pl.BlockSpec((None, tr, LANES), lambda i, m, k=k: (m[0] ^ k, i, 0)) for k in range(8)],
            out_specs=pl.BlockSpec((tr, LANES), lambda i, m: (i, 0))),
        out_shape=_sds((rows, LANES), F32),
        compiler_params=_params("arbitrary"),
    )(me, *([everyone] * 8))


def _half(ref, core, axis):
    rows = ref.shape[axis] // 2
    idx = (slice(None),) * axis + (pl.ds(core * rows, rows),)
    return ref.at[idx]


def reduce_plan(grads):
    remote, outs = [], []
    for t, g in enumerate(grads):
        outs.append(_sds((len(CHIP_FLIPS),) + g.shape[1:], BF16))
        for f, flip in enumerate(CHIP_FLIPS):
            remote.append((t, lambda ref, sender, receiver: ref.at[_chip(receiver)],
                           t, lambda ref, sender, f=f: ref.at[f], flip))
    return list(grads), outs, remote


def reduce_finish(grads, got, stacks, full_shapes, into, name):
    chip = (2 * lax.axis_index("x") + lax.axis_index("y")).astype(jnp.int32).reshape(1)
    core = lax.axis_index("c").astype(jnp.int32).reshape(1)
    sums = [sum_chips(g, r, chip, f"{name}_sum{t}") for t, (g, r) in enumerate(zip(grads, got))]
    jobs, outs = [], []
    for t, a in enumerate(sums):
        half, cols = a.shape[0] // 2, a.shape[1]
        outs.append(_sds((half, cols), F32))
        pieces = max(1, half * cols * 4 // STAGE_BYTES)
        step = half // pieces
        for q in range(pieces):
            jobs.append((t, lambda ref, me, q=q, step=step, half=half: ref.at[pl.ds((1 - me[2]) * half + q * step, step)],
                         [(t, lambda ref, sender, q=q, step=step: ref.at[pl.ds(q * step, step)], "c")]))
    theirs = staged_push(name + "_swap", sums, outs, jobs)
    totals = [add_rows(a, r, core, f"{name}_add{t}") for t, (a, r) in enumerate(zip(sums, theirs))]
    names = []
    for out_name, _ in stacks:
        if out_name not in names:
            names.append(out_name)
    names = [n for n in names if n in into] + [n for n in names if n not in into]
    kept = [into[n] for n in names if n in into]
    outs = [_sds(full_shapes[n], F32) for n in names]
    jobs = []
    for t, (out_name, layer) in enumerate(stacks):
        oi = names.index(out_name)
        rows, cols = totals[t].shape
        pieces = max(1, rows * cols * 4 // STAGE_BYTES)
        step = rows // pieces
        for q in range(pieces):
            src = lambda ref, me, q=q, step=step: ref.at[pl.ds(q * step, step)]
            place = lambda ref, sender, layer=layer, q=q, step=step, rows=rows: ref.at[
                layer, pl.ds(sender[2] * rows + q * step, step)]
            jobs.append((len(kept) + t, src, [(oi, place, None), (oi, place, "c")]))
    full = staged_push(name + "_share", kept + totals, outs, jobs, n_alias=len(kept))
    return {**into, **dict(zip(names, full))}


STAGE_BYTES = 1024 * 1024


def add_rows(a, theirs, core, name):
    rows, cols = a.shape
    half = rows // 2
    tr = _row_tile(half, 256)
    nb = half // tr

    def body(core_ref, a_ref, t_ref, o_ref):
        o_ref[...] = a_ref[...] + t_ref[...]

    return pl.pallas_call(
        body, name=name,
        grid_spec=pltpu.PrefetchScalarGridSpec(
            num_scalar_prefetch=1, grid=(nb,),
            in_specs=[pl.BlockSpec((tr, cols), lambda i, c: (c[0] * nb + i, 0)),
                      pl.BlockSpec((tr, cols), lambda i, c: (i, 0))],
            out_specs=pl.BlockSpec((tr, cols), lambda i, c: (i, 0))),
        out_shape=_sds((half, cols), F32),
        compiler_params=_params("arbitrary"),
    )(core, a, theirs)


def sum_chips(mine, theirs, chip, name):
    _, half, cols = mine.shape
    tr = _row_tile(half, 256)

    def body(chip_ref, m_ref, a_ref, b_ref, c_ref, o_ref):
        o_ref[...] = ((m_ref[...].astype(F32) + a_ref[...].astype(F32))
                      + b_ref[...].astype(F32)) + c_ref[...].astype(F32)

    got = lambda f: pl.BlockSpec((None, tr, cols), lambda i, ch: (f, i, 0))
    return pl.pallas_call(
        body, name=name,
        grid_spec=pltpu.PrefetchScalarGridSpec(
            num_scalar_prefetch=1, grid=(half // tr,),
            in_specs=[pl.BlockSpec((None, tr, cols), lambda i, ch: (ch[0], i, 0)), got(0), got(1), got(2)],
            out_specs=pl.BlockSpec((tr, cols), lambda i, ch: (i, 0))),
        out_shape=_sds((half, cols), F32),
        compiler_params=_params("arbitrary"),
    )(chip, mine, theirs, theirs, theirs)


def _tok(t):
    return t.transpose(1, 0, 2).reshape(t.shape[1], t.shape[0] * t.shape[2])


def _heads(t):
    return t.reshape(t.shape[0], t.shape[1] // HEAD_DIM, HEAD_DIM).transpose(1, 0, 2)


def _tile2(vec):
    return jnp.tile(vec.reshape(1, HEAD_DIM), (1, 2))


REST = ("att_w_in", "att_w_out", "sgu_w_in", "sgu_w_out", "mlp_w1", "mlp_w2")
LAST_GROUP = (("att_w_in", 0),)
LATE_SMALL = ("att_norm", "att_sink", "att_qnorm", "att_knorm")


def local_step(x, target, first, rest_shards, rep, full_shapes):
    s_len, d = x.shape
    tabs = rope_tables(s_len)
    depth = rep["mlp_norm"].shape[0]
    row = lambda a: a.reshape(1, -1)
    saved = []
    h = x
    gw = {"att_w_in": [first]}

    def wl(name, idx):
        return (gw[name][idx], 0) if name == "att_w_in" else (gw[name], idx)

    for layer in range(depth):
        i = layer // 2
        tag = f"l{layer}"
        if layer % 2 == 0:
            hn, proj = norm_mm(h, row(rep["att_norm"][i]), *wl("att_w_in", i), F32, tag + "_att_proj")
            qkv_t, kv = prep_fwd(proj, tabs, _tile2(rep["att_qnorm"][i]), _tile2(rep["att_knorm"][i]),
                                 tag + "_att_prep")
            kv_tok = _heads(kv)
            oa, lse_a = flash_fwd_t(qkv_t, kv_tok, QA_COL // GROUP_W, 0, (KA_COL + LANES) // HEAD_DIM,
                                    rep["att_sink"][i], True, tag + "_win_fwd")
            plan = gather_halves_plan(rest_shards) if layer == 0 else None
            ob, lse_b, *got = flash_fwd_t(qkv_t, kv_tok, QB_COL // GROUP_W, 4, (KB_COL + LANES) // HEAD_DIM,
                                          None, False, tag + "_grid_fwd", comm=plan)
            if layer == 0:
                rest = dict(zip(REST, gather_halves_fill(got, rest_shards, "gather_rest_fill")))
                gw["att_w_in"].append(rest.pop("att_w_in"))
                gw.update(rest)
            out = mm_res_t([oa, ob], *wl("att_w_out", i), h, tag + "_att_out")
            mix_saved = (h, hn, proj, qkv_t, kv_tok, oa, ob, lse_a, lse_b)
        else:
            hn, zpre = norm_mm(h, row(rep["sgu_norm"][i]), *wl("sgu_w_in", i), F32, tag + "_sgu_in")
            ws = rep["sgu_w_s"][i].astype(BF16)
            bsb = jnp.broadcast_to(rep["sgu_b_s"][i][:, :, None], (SGU_GROUPS, SGU_CHUNK, LANES))
            y = sgu_mid_fwd(zpre, row(rep["sgu_ln_g"][i]), row(rep["sgu_ln_b"][i]), ws, bsb, tag + "_sgu_mid")
            out = mm_res(y, *wl("sgu_w_out", i), h, tag + "_sgu_out")
            mix_saved = (h, hn, zpre, y, ws, bsb)
        hm, a = norm_mm(out, row(rep["mlp_norm"][layer]), *wl("mlp_w1", layer), BF16, tag + "_mlp_up")
        nxt = mm_res(a, *wl("mlp_w2", layer), out, tag + "_mlp_down", relu2=True)
        saved.append((mix_saved, (out, hm, a)))
        h = nxt
    dh, dhb, d_final, loss_tile = loss_head(h, row(rep["final_norm"]), target, "loss_head")
    big, tags = [], []
    small = {k: [jnp.zeros(v.shape[1:], F32)] * v.shape[0] for k, v in rep.items() if k != "final_norm"}
    small["final_norm"] = d_final.reshape(-1)
    stacked = lambda: [small[n] if n == "final_norm" else jnp.stack(small[n]) for n in SMALL]
    for layer in reversed(range(depth)):
        i = layer // 2
        tag = f"l{layer}"
        mix_saved, (xin, hm, a) = saved[layer]
        da = mm_nt_relu2_bwd(dhb, *wl("mlp_w2", layer), a, tag + "_mlp_down_bwd")
        big.append(dw_mm(a, dhb, tag + "_mlp_dw2", col_sharded=False, relu2=True))
        tags.append(("mlp_w2", layer))
        big.append(dw_mm(hm, da, tag + "_mlp_dw1", col_sharded=True))
        tags.append(("mlp_w1", layer))
        dh, dhb, dg = dx_norm(da, *wl("mlp_w1", layer), xin, row(rep["mlp_norm"][layer]), dh, tag + "_mlp_up_bwd")
        small["mlp_norm"][layer] = dg.reshape(-1)
        if layer % 2 == 0:
            xin, hn, proj, qkv_t, kv_tok, oa, ob, lse_a, lse_b = mix_saved
            do_t = mm_nt(dhb, *wl("att_w_out", i), tag + "_att_out_bwd", transposed=True)
            big.append(dw_nn([oa, ob], dhb, tag + "_att_dwout"))
            tags.append(("att_w_out", i))
            dqa, dka, dva, dsink = flash_bwd_t(qkv_t, kv_tok, oa, do_t, lse_a, QA_COL // GROUP_W, 0, 2,
                                               KA_COL // HEAD_DIM, 0, rep["att_sink"][i], True, tag + "_win_bwd")
            plan = None
            if layer == 0:
                early = [k for k, t in enumerate(tags) if t not in LAST_GROUP]
                early_pack = _pack(stacked() + [loss_tile[0, :1]])
                plan = add_to_all(reduce_plan([big[k] for k in early]), early_pack)
            dqb, dkb, dvb, *got = flash_bwd_t(qkv_t, kv_tok, ob, do_t, lse_b, QB_COL // GROUP_W, 4, 6,
                                              KB_COL // HEAD_DIM, 2, None, False, tag + "_grid_bwd", comm=plan)
            if layer == 0:
                grads = reduce_finish([big[k] for k in early], got[:-1], [tags[k] for k in early], full_shapes,
                                      {}, "grads1")
                early_sum = sum_devices(early_pack, got[-1], "sum_small")
            qg, kg = _tile2(rep["att_qnorm"][i]), _tile2(rep["att_knorm"][i])
            dproj, dqg, dkg = prep_bwd(proj, dqa, _tok(dka), _tok(dva), dqb, _tok(dkb), _tok(dvb),
                                       tabs, qg, kg, tag + "_att_prep_bwd")
            big.append(dw_mm(hn, dproj, tag + "_att_dwin", col_sharded=True))
            tags.append(("att_w_in", i))
            dh, dhb, dg = dx_norm(dproj, *wl("att_w_in", i), xin, row(rep["att_norm"][i]), dh, tag + "_att_proj_bwd")
            small["att_norm"][i] = dg.reshape(-1)
            small["att_sink"][i] = dsink[:, 0, :GROUP].reshape(-1)
            small["att_qnorm"][i] = dqg[0, :HEAD_DIM] + dqg[0, HEAD_DIM:]
            small["att_knorm"][i] = dkg[0, :HEAD_DIM] + dkg[0, HEAD_DIM:]
        else:
            xin, hn, zpre, y, ws, bsb = mix_saved
            dy = mm_nt(dhb, *wl("sgu_w_out", i), tag + "_sgu_out_bwd")
            big.append(dw_mm(y, dhb, tag + "_sgu_dwout", col_sharded=False))
            tags.append(("sgu_w_out", i))
            wst = ws.transpose(0, 2, 1)
            dz, dws, dbs, dlg, dlb = sgu_mid_bwd(zpre, dy, row(rep["sgu_ln_g"][i]), row(rep["sgu_ln_b"][i]),
                                                 ws, wst, bsb, tag + "_sgu_mid_bwd")
            big.append(dw_mm(hn, dz, tag + "_sgu_dwin", col_sharded=True))
            tags.append(("sgu_w_in", i))
            dh, dhb, dg = dx_norm(dz, *wl("sgu_w_in", i), xin, row(rep["sgu_norm"][i]), dh, tag + "_sgu_in_bwd")
            small["sgu_norm"][i] = dg.reshape(-1)
            small["sgu_ln_g"][i] = dlg.reshape(-1)
            small["sgu_ln_b"][i] = dlb.reshape(-1)
            small["sgu_w_s"][i] = dws
            small["sgu_b_s"][i] = dbs[:, :, 0]
    late = [k for k, t in enumerate(tags) if t in LAST_GROUP]
    late_pack = _pack([small[n][0] for n in LATE_SMALL])
    got = exchange("grads2_scatter", *add_to_all(reduce_plan([big[k] for k in late]), late_pack), [])
    grads = reduce_finish([big[k] for k in late], got[:-1], [tags[k] for k in late], full_shapes, grads, "grads2")
    late_sum = sum_devices(late_pack, got[-1], "sum_small_late")
    shapes = [a.shape for a in stacked()]
    *small_g, loss = _unpack(early_sum, shapes + [()])
    small_g = dict(zip(SMALL, small_g))
    for n, g in zip(LATE_SMALL, _unpack(late_sum, [small[n][0].shape for n in LATE_SMALL])):
        small_g[n] = small_g[n].at[0].add(g)
    return loss, dh, grads, small_g


BIG = ("att_w_in", "att_w_out", "sgu_w_in", "sgu_w_out", "mlp_w1", "mlp_w2")
SHARDED_VEC = ("sgu_norm", "sgu_ln_g", "sgu_ln_b")
REPLICATED = ("att_norm", "att_sink", "att_qnorm", "att_knorm", "sgu_w_s", "sgu_b_s", "mlp_norm", "final_norm")
WEIGHTS = ("att_norm", "att_w_in", "att_sink", "att_qnorm", "att_knorm", "att_w_out", "sgu_norm", "sgu_w_in",
           "sgu_ln_g", "sgu_ln_b", "sgu_w_s", "sgu_b_s", "sgu_w_out", "mlp_norm", "mlp_w1", "mlp_w2", "final_norm")
SMALL = tuple(n for n in WEIGHTS if n not in BIG)
PACK_ALIGN = 8 * LANES


def _pack(arrays):
    flat = jnp.concatenate([a.reshape(-1) for a in arrays])
    pad = -flat.shape[0] % PACK_ALIGN
    return jnp.pad(flat, (0, pad)).reshape(-1, LANES)


def _unpack(flat2d, shapes):
    flat = flat2d.reshape(-1)
    out, off = [], 0
    for shape in shapes:
        size = int(np.prod(shape))
        out.append(flat[off:off + size].reshape(shape))
        off += size
    return out


def kernel(x, att_norm, att_w_in, att_sink, att_qnorm, att_knorm, att_w_out, sgu_norm, sgu_w_in, sgu_ln_g, sgu_ln_b, sgu_w_s, sgu_b_s, sgu_w_out, mlp_norm, mlp_w1, mlp_w2, final_norm, loss_target, m_att_norm, m_att_w_in, m_att_sink, m_att_qnorm, m_att_knorm, m_att_w_out, m_sgu_norm, m_sgu_w_in, m_sgu_ln_g, m_sgu_ln_b, m_sgu_w_s, m_sgu_b_s, m_sgu_w_out, m_mlp_norm, m_mlp_w1, m_mlp_w2, m_final_norm, v_att_norm, v_att_w_in, v_att_sink, v_att_qnorm, v_att_knorm, v_att_w_out, v_sgu_norm, v_sgu_w_in, v_sgu_ln_g, v_sgu_ln_b, v_sgu_w_s, v_sgu_b_s, v_sgu_w_out, v_mlp_norm, v_mlp_w1, v_mlp_w2, v_final_norm):
    w = dict(att_norm=att_norm, att_w_in=att_w_in, att_sink=att_sink, att_qnorm=att_qnorm, att_knorm=att_knorm,
             att_w_out=att_w_out, sgu_norm=sgu_norm, sgu_w_in=sgu_w_in, sgu_ln_g=sgu_ln_g, sgu_ln_b=sgu_ln_b,
             sgu_w_s=sgu_w_s, sgu_b_s=sgu_b_s, sgu_w_out=sgu_w_out, mlp_norm=mlp_norm, mlp_w1=mlp_w1,
             mlp_w2=mlp_w2, final_norm=final_norm)
    m = dict(att_norm=m_att_norm, att_w_in=m_att_w_in, att_sink=m_att_sink, att_qnorm=m_att_qnorm,
             att_knorm=m_att_knorm, att_w_out=m_att_w_out, sgu_norm=m_sgu_norm, sgu_w_in=m_sgu_w_in,
             sgu_ln_g=m_sgu_ln_g, sgu_ln_b=m_sgu_ln_b, sgu_w_s=m_sgu_w_s, sgu_b_s=m_sgu_b_s,
             sgu_w_out=m_sgu_w_out, mlp_norm=m_mlp_norm, mlp_w1=m_mlp_w1, mlp_w2=m_mlp_w2,
             final_norm=m_final_norm)
    v = dict(att_norm=v_att_norm, att_w_in=v_att_w_in, att_sink=v_att_sink, att_qnorm=v_att_qnorm,
             att_knorm=v_att_knorm, att_w_out=v_att_w_out, sgu_norm=v_sgu_norm, sgu_w_in=v_sgu_w_in,
             sgu_ln_g=v_sgu_ln_g, sgu_ln_b=v_sgu_ln_b, sgu_w_s=v_sgu_w_s, sgu_b_s=v_sgu_b_s,
             sgu_w_out=v_sgu_w_out, mlp_norm=v_mlp_norm, mlp_w1=v_mlp_w1, mlp_w2=v_mlp_w2,
             final_norm=v_final_norm)
    chip = 2 * lax.axis_index("x") + lax.axis_index("y")

    vecs = jnp.stack([w[n] for n in SHARDED_VEC])
    wb = {n: w[n].astype(BF16) for n in BIG}
    first, vec_all = gather_whole([wb["att_w_in"][0:1], vecs], "gather_first")
    rest_shards = [wb[n][1:2] if n == "att_w_in" else wb[n] for n in REST]
    vec_full = vec_all.transpose(1, 2, 0, 3).reshape(vecs.shape[0], vecs.shape[1], -1)
    rep = {n: w[n] for n in REPLICATED}
    rep.update({n: vec_full[k] for k, n in enumerate(SHARDED_VEC)})

    loss, grad_x, grads, small_g = local_step(x[0], loss_target[0], first, rest_shards, rep,
                                              {n: w[n].shape for n in BIG})
    width = w["sgu_norm"].shape[1]
    for n in SHARDED_VEC:
        small_g[n] = lax.dynamic_slice_in_dim(small_g[n], chip * width, width, axis=1)
    grads.update(small_g)
    for n in BIG:
        grads[n] = grads[n].reshape(w[n].shape)

    delta, new_m, new_v = {}, {}, {}
    for n in WEIGHTS:
        shape = w[n].shape
        two_d = (lambda a: a.reshape(1, -1)) if len(shape) == 1 else (lambda a: a)
        dn, mn, vn = adamw(two_d(w[n]), two_d(grads[n]), two_d(m[n]), two_d(v[n]), "adamw_" + n)
        delta[n], new_m[n], new_v[n] = dn.reshape(shape), mn.reshape(shape), vn.reshape(shape)
    return (loss, grad_x[None], *[grads[n] for n in WEIGHTS], *[delta[n] for n in WEIGHTS],
            *[new_m[n] for n in WEIGHTS], *[new_v[n] for n in WEIGHTS])
```

```python
import functools

import numpy as np
import jax
import jax.numpy as jnp
from jax import lax
from jax.experimental import pallas as pl
from jax.experimental.pallas import tpu as pltpu

F32 = jnp.float32
BF16 = jnp.bfloat16
MESH = pl.DeviceIdType.MESH

EPS = 1e-6
HEAD_DIM = 64
BLOCK = 128
GRID_W = 64
ROPE_THETA = 10000.0
N_CHIPS = 4
LANES = 128
V7X_VMEM_BYTES = 64 * 1024 * 1024
VMEM_LIMIT = V7X_VMEM_BYTES - 8 * 1024 * 1024

ADAM_LR = 0.001
ADAM_B1 = 0.9
ADAM_B2 = 0.999
ADAM_EPS = 1e-08
ADAM_WD = 0.01
ADAM_STEP = 10

NT_DIMS = (((1,), (1,)), ((), ()))
TN_DIMS = (((0,), (0,)), ((), ()))


def _params(*sem):
    return pltpu.CompilerParams(dimension_semantics=sem, vmem_limit_bytes=VMEM_LIMIT)


def _sds(shape, dtype):
    return jax.ShapeDtypeStruct(tuple(shape), dtype)


def _row_tile(rows, want):
    t = min(rows, want)
    assert rows % t == 0, (rows, want)
    return t


def norm_mm(x, g, w4, layer, out_dtype, name):
    s_len, d = x.shape
    ns = w4.shape[-1]
    tm = _row_tile(s_len, 512)

    def body(x_ref, g_ref, w_ref, h_ref, y_ref):
        xf = x_ref[...]
        r = lax.rsqrt(jnp.mean(xf * xf, axis=-1, keepdims=True) + EPS)
        h = ((xf * r) * g_ref[...]).astype(BF16)
        h_ref[...] = h
        for s in range(N_CHIPS):
            y_ref[:, s * ns:(s + 1) * ns] = jnp.dot(h, w_ref[s], preferred_element_type=F32).astype(y_ref.dtype)

    return pl.pallas_call(
        body, name=name, grid=(s_len // tm,),
        in_specs=[pl.BlockSpec((tm, d), lambda i: (i, 0)),
                  pl.BlockSpec((1, d), lambda i: (0, 0)),
                  pl.BlockSpec((N_CHIPS, None, d, ns), lambda i: (0, layer, 0, 0))],
        out_specs=[pl.BlockSpec((tm, d), lambda i: (i, 0)),
                   pl.BlockSpec((tm, N_CHIPS * ns), lambda i: (i, 0))],
        out_shape=[_sds((s_len, d), BF16), _sds((s_len, N_CHIPS * ns), out_dtype)],
        compiler_params=_params("arbitrary"),
    )(x, g, w4)


def mm_res(a, w4, layer, res, name, relu2=False):
    s_len, k = a.shape
    kq, n = w4.shape[-2:]
    assert kq * N_CHIPS == k
    tm = _row_tile(s_len, 256 if k > 1024 else 512)

    def body(a_ref, w0, w1, w2, w3, r_ref, o_ref):
        acc = r_ref[...]
        for s, w_ref in enumerate((w0, w1, w2, w3)):
            av = a_ref[:, s * kq:(s + 1) * kq]
            if relu2:
                t = jnp.maximum(av.astype(F32), 0.0)
                av = (t * t).astype(BF16)
            acc = acc + jnp.dot(av, w_ref[...], preferred_element_type=F32)
        o_ref[...] = acc

    def wspec(s):
        return pl.BlockSpec((None, None, kq, n), lambda i: (s, layer, 0, 0))

    return pl.pallas_call(
        body, name=name, grid=(s_len // tm,),
        in_specs=[pl.BlockSpec((tm, k), lambda i: (i, 0)), wspec(0), wspec(1), wspec(2), wspec(3),
                  pl.BlockSpec((tm, n), lambda i: (i, 0))],
        out_specs=pl.BlockSpec((tm, n), lambda i: (i, 0)),
        out_shape=_sds((s_len, n), F32),
        compiler_params=_params("arbitrary"),
    )(a, w4, w4, w4, w4, res)


def mm_res_t(pieces, w4, layer, res, name):
    s_len = res.shape[0]
    kq, n = w4.shape[-2:]
    rows = pieces[0].shape[0]
    assert rows % kq == 0 and rows * len(pieces) == kq * N_CHIPS
    tm = _row_tile(s_len, 512)
    n_p = len(pieces)

    def body(*refs):
        p_refs, w_refs, (r_ref, o_ref) = refs[:n_p], refs[n_p:n_p + N_CHIPS], refs[n_p + N_CHIPS:]
        acc = r_ref[...]
        for s in range(N_CHIPS):
            p, off = divmod(s * kq, rows)
            acc = acc + lax.dot_general(p_refs[p][off:off + kq, :], w_refs[s][...], TN_DIMS,
                                        preferred_element_type=F32)
        o_ref[...] = acc

    def wspec(s):
        return pl.BlockSpec((None, None, kq, n), lambda i: (s, layer, 0, 0))

    return pl.pallas_call(
        body, name=name, grid=(s_len // tm,),
        in_specs=[pl.BlockSpec((rows, tm), lambda i: (0, i))] * n_p + [wspec(s) for s in range(N_CHIPS)]
        + [pl.BlockSpec((tm, n), lambda i: (i, 0))],
        out_specs=pl.BlockSpec((tm, n), lambda i: (i, 0)),
        out_shape=_sds((s_len, n), F32),
        compiler_params=_params("arbitrary"),
    )(*pieces, w4, w4, w4, w4, res)


def dw_nn(pieces, b, name):
    s_len, n = b.shape
    rows = pieces[0].shape[0]
    n_p = len(pieces)
    k = rows * n_p
    ts = _row_tile(s_len, 2048)
    n_s = s_len // ts

    def body(*refs):
        p_refs, (b_ref, o_ref, acc_ref) = refs[:n_p], refs[n_p:]
        s = pl.program_id(0)
        bv = b_ref[...]
        for p in range(n_p):
            part = jnp.dot(p_refs[p][...], bv, preferred_element_type=F32)
            at = slice(p * rows, (p + 1) * rows)
            if n_s == 1:
                o_ref[at, :] = part.astype(BF16)
                continue

            @pl.when(s == 0)
            def _():
                acc_ref[at, :] = part

            @pl.when((s > 0) & (s < n_s - 1))
            def _():
                acc_ref[at, :] += part

            @pl.when(s == n_s - 1)
            def _():
                o_ref[at, :] = (acc_ref[at, :] + part).astype(BF16)

    out = pl.pallas_call(
        body, name=name, grid=(n_s,),
        in_specs=[pl.BlockSpec((rows, ts), lambda s: (0, s))] * n_p + [pl.BlockSpec((ts, n), lambda s: (s, 0))],
        out_specs=pl.BlockSpec((k, n), lambda s: (0, 0)), out_shape=_sds((k, n), BF16),
        scratch_shapes=[pltpu.VMEM((k, n), F32)],
        compiler_params=_params("arbitrary"),
    )(*pieces, b)
    return out.reshape(N_CHIPS, k // N_CHIPS, n)


def mm_nt(dy, w4, layer, name, transposed=False):
    s_len, n = dy.shape
    mq = w4.shape[-2]
    tm = _row_tile(s_len, 512)

    def body(d_ref, w0, w1, w2, w3, o_ref):
        dv = d_ref[...]
        for s, w_ref in enumerate((w0, w1, w2, w3)):
            if transposed:
                o_ref[s * mq:(s + 1) * mq, :] = lax.dot_general(
                    w_ref[...], dv, NT_DIMS, preferred_element_type=F32).astype(BF16)
            else:
                o_ref[:, s * mq:(s + 1) * mq] = lax.dot_general(
                    dv, w_ref[...], NT_DIMS, preferred_element_type=F32).astype(BF16)

    def wspec(s):
        return pl.BlockSpec((None, None, mq, n), lambda i: (s, layer, 0, 0))

    m = N_CHIPS * mq
    return pl.pallas_call(
        body, name=name, grid=(s_len // tm,),
        in_specs=[pl.BlockSpec((tm, n), lambda i: (i, 0)), wspec(0), wspec(1), wspec(2), wspec(3)],
        out_specs=pl.BlockSpec((m, tm), lambda i: (0, i)) if transposed else pl.BlockSpec((tm, m), lambda i: (i, 0)),
        out_shape=_sds((m, s_len) if transposed else (s_len, m), BF16),
        compiler_params=_params("arbitrary"),
    )(dy, w4, w4, w4, w4)


def mm_nt_relu2_bwd(dy, w4, layer, a, name):
    s_len, n = dy.shape
    mq = w4.shape[-2]
    tm = _row_tile(s_len, 512)

    def body(d_ref, w_ref, a_ref, o_ref):
        dv = d_ref[...]
        for s in range(N_CHIPS):
            cols = slice(s * mq, (s + 1) * mq)
            dz = lax.dot_general(dv, w_ref[s], NT_DIMS, preferred_element_type=F32)
            o_ref[:, cols] = (dz * (2.0 * jnp.maximum(a_ref[:, cols].astype(F32), 0.0))).astype(BF16)

    return pl.pallas_call(
        body, name=name, grid=(s_len // tm,),
        in_specs=[pl.BlockSpec((tm, n), lambda i: (i, 0)),
                  pl.BlockSpec((N_CHIPS, None, mq, n), lambda i: (0, layer, 0, 0)),
                  pl.BlockSpec((tm, N_CHIPS * mq), lambda i: (i, 0))],
        out_specs=pl.BlockSpec((tm, N_CHIPS * mq), lambda i: (i, 0)),
        out_shape=_sds((s_len, N_CHIPS * mq), BF16),
        compiler_params=_params("arbitrary"),
    )(dy, w4, a)


def dx_norm(dy, w4, layer, x, g, dres, name):
    s_len, d = x.shape
    ns = w4.shape[-1]
    tm = _row_tile(s_len, 512)

    def body(dy_ref, w_ref, x_ref, g_ref, dr_ref, dx_ref, dxb_ref, dg_ref):
        i = pl.program_id(0)
        dh = lax.dot_general(dy_ref[:, 0:ns], w_ref[0], NT_DIMS, preferred_element_type=F32)
        for s in range(1, N_CHIPS):
            dh = dh + lax.dot_general(dy_ref[:, s * ns:(s + 1) * ns], w_ref[s], NT_DIMS,
                                      preferred_element_type=F32)
        xf = x_ref[...]
        r = lax.rsqrt(jnp.mean(xf * xf, axis=-1, keepdims=True) + EPS)
        xhat = xf * r
        dg_part = jnp.sum(dh * xhat, axis=0, keepdims=True)

        @pl.when(i == 0)
        def _():
            dg_ref[...] = dg_part

        @pl.when(i > 0)
        def _():
            dg_ref[...] += dg_part

        dxh = dh * g_ref[...]
        dx = dr_ref[...] + r * (dxh - xhat * jnp.mean(dxh * xhat, axis=-1, keepdims=True))
        dx_ref[...] = dx
        dxb_ref[...] = dx.astype(BF16)

    row = pl.BlockSpec((tm, d), lambda i: (i, 0))
    vec = pl.BlockSpec((1, d), lambda i: (0, 0))
    return pl.pallas_call(
        body, name=name, grid=(s_len // tm,),
        in_specs=[pl.BlockSpec((tm, N_CHIPS * ns), lambda i: (i, 0)),
                  pl.BlockSpec((N_CHIPS, None, d, ns), lambda i: (0, layer, 0, 0)), row, vec, row],
        out_specs=[row, row, vec],
        out_shape=[_sds((s_len, d), F32), _sds((s_len, d), BF16), _sds((1, d), F32)],
        compiler_params=_params("arbitrary"),
    )(dy, w4, x, g, dres)


def dw_mm(a, b, name, col_sharded, relu2=False):
    s_len, k = a.shape
    n = b.shape[1]
    ts = _row_tile(s_len, 2048)
    tk = min(k, 1024)
    tn = n // N_CHIPS if col_sharded else min(n, 1024)
    n_s = s_len // ts

    def body(a_ref, b_ref, o_ref, acc_ref):
        s = pl.program_id(2)
        av = a_ref[...]
        if relu2:
            t = jnp.maximum(av.astype(F32), 0.0)
            av = (t * t).astype(BF16)
        part = lax.dot_general(av, b_ref[...], TN_DIMS, preferred_element_type=F32)
        if n_s == 1:
            o_ref[...] = part.astype(BF16)
            return

        @pl.when(s == 0)
        def _():
            acc_ref[...] = part

        @pl.when((s > 0) & (s < n_s - 1))
        def _():
            acc_ref[...] += part

        @pl.when(s == n_s - 1)
        def _():
            o_ref[...] = (acc_ref[...] + part).astype(BF16)

    if col_sharded:
        out_shape = _sds((N_CHIPS, k, tn), BF16)
        out_spec = pl.BlockSpec((None, tk, tn), lambda i, j, s: (j, i, 0))
    else:
        out_shape = _sds((N_CHIPS, k // N_CHIPS, n), BF16)
        rows_per = k // N_CHIPS
        assert tk % rows_per == 0 or rows_per % tk == 0
        if tk >= rows_per:
            out_shape = _sds((k, n), BF16)
            out_spec = pl.BlockSpec((tk, tn), lambda i, j, s: (i, j))
        else:
            per = rows_per // tk
            out_spec = pl.BlockSpec((None, tk, tn), lambda i, j, s: (i // per, i % per, j))

    out = pl.pallas_call(
        body, name=name, grid=(k // tk, n // tn, n_s),
        in_specs=[pl.BlockSpec((ts, tk), lambda i, j, s: (s, i)),
                  pl.BlockSpec((ts, tn), lambda i, j, s: (s, j))],
        out_specs=out_spec, out_shape=out_shape,
        scratch_shapes=[pltpu.VMEM((tk, tn), F32)],
        compiler_params=_params("arbitrary", "arbitrary", "arbitrary"),
    )(a, b)
    if not col_sharded:
        out = out.reshape(N_CHIPS, k // N_CHIPS, n)
    return out


def ew(fn, ins, out_dtypes, name, tile_rows=256):
    rows, cols = ins[0].shape
    for a in ins:
        assert a.shape == (rows, cols), (name, a.shape, rows, cols)
    tr = rows if (rows <= tile_rows or rows % tile_rows) else tile_rows
    n_in = len(ins)

    def body(*refs):
        outs = fn(*[r[...] for r in refs[:n_in]])
        for o_ref, val in zip(refs[n_in:], outs):
            o_ref[...] = val.astype(o_ref.dtype)

    spec = pl.BlockSpec((tr, cols), lambda i: (i, 0))
    return pl.pallas_call(
        body, name=name, grid=(rows // tr,),
        in_specs=[spec] * n_in, out_specs=[spec] * len(out_dtypes),
        out_shape=[_sds((rows, cols), dt) for dt in out_dtypes],
        compiler_params=_params("arbitrary"),
    )(*ins)


def adamw(w, g, m, v, name):
    shape = w.shape
    cols = shape[-1]
    two_d = lambda a: a.reshape(-1, cols)

    def fn(wv, gv, mv, vv):
        m_new = ADAM_B1 * mv + (1.0 - ADAM_B1) * gv
        v_new = ADAM_B2 * vv + (1.0 - ADAM_B2) * (gv * gv)
        m_hat = m_new / (1.0 - ADAM_B1 ** ADAM_STEP)
        v_hat = v_new / (1.0 - ADAM_B2 ** ADAM_STEP)
        delta = -ADAM_LR * (m_hat / (jnp.sqrt(v_hat) + ADAM_EPS) + ADAM_WD * wv)
        return delta, m_new, v_new

    d, mn, vn = ew(fn, [two_d(w), two_d(g), two_d(m), two_d(v)], [F32, F32, F32], name)
    return d.reshape(shape), mn.reshape(shape), vn.reshape(shape)


def rope_tables(s_len):
    def angles(pos, dim):
        freqs = ROPE_THETA ** (-jnp.arange(0, dim, 2, dtype=F32) / dim)
        ang = pos.astype(F32)[:, None] * freqs[None, :]
        return jnp.cos(ang), jnp.sin(ang)

    pos = jnp.arange(s_len)
    rows = s_len // GRID_W
    row_idx = jnp.repeat(jnp.arange(rows), GRID_W)
    col_idx = jnp.tile(jnp.arange(GRID_W), rows)
    c1, s1 = angles(pos, HEAD_DIM)
    cr, sr = angles(row_idx, HEAD_DIM // 2)
    cc, sc = angles(col_idx, HEAD_DIM // 2)
    cos1 = jnp.tile(jnp.concatenate([c1, c1], -1), (1, 2))
    sin1 = jnp.tile(jnp.concatenate([-s1, s1], -1), (1, 2))
    cos2 = jnp.tile(jnp.concatenate([cr, cr, cc, cc], -1), (1, 2))
    sin2 = jnp.tile(jnp.concatenate([-sr, sr, -sc, sc], -1), (1, 2))
    return cos1, sin1, cos2, sin2


def _lane_iota(rows):
    return lax.broadcasted_iota(jnp.int32, (rows, LANES), 1)


def _swap(x, dist, lane):
    return jnp.where((lane & dist) != 0, pltpu.roll(x, dist, 1), pltpu.roll(x, LANES - dist, 1))


def _head_ones():
    r = lax.broadcasted_iota(jnp.int32, (LANES, LANES), 0) // HEAD_DIM
    c = lax.broadcasted_iota(jnp.int32, (LANES, LANES), 1) // HEAD_DIM
    return (r == c).astype(BF16)


def _head_sum(t, ones):
    hi = t.astype(BF16)
    lo = (t - hi.astype(F32)).astype(BF16)
    return (jnp.dot(hi, ones, preferred_element_type=F32) + jnp.dot(lo, ones, preferred_element_type=F32))


Q_SCALE = HEAD_DIM ** -0.5
LOG2E = 1.4426950408889634
LN2 = 0.6931471805599453
CHUNK_KIND = ["qa"] * 4 + ["ka", "va"] + ["qb"] * 4 + ["kb", "vb"]
QA_COL, KA_COL, QB_COL, KB_COL = 0, 512, 768, 1280


def prep_fwd(proj, tabs, qn_g, kn_g, name):
    s_len, width = proj.shape
    ts = _row_tile(s_len, 512)
    cos1, sin1, cos2, sin2 = tabs

    def body(p_ref, c1_ref, s1_ref, c2_ref, s2_ref, qg_ref, kg_ref, o_ref, kv_ref):
        lane = _lane_iota(ts)
        ones = _head_ones()
        c1, s1, c2, s2 = c1_ref[...], s1_ref[...], c2_ref[...], s2_ref[...]
        n_kv = 0
        for cb, kind in enumerate(CHUNK_KIND):
            x = p_ref[:, cb * LANES:(cb + 1) * LANES]
            if kind in ("qa", "ka"):
                y = x * c1 + _swap(x, 32, lane) * s1
            elif kind in ("qb", "kb"):
                gain = qg_ref[...] if kind == "qb" else kg_ref[...]
                ms = _head_sum(x * x, ones) * (1.0 / HEAD_DIM)
                xn = (x * lax.rsqrt(ms + EPS)) * gain
                y = xn * c2 + _swap(xn, 16, lane) * s2
            else:
                y = x
            if kind in ("qa", "qb"):
                y = y * (Q_SCALE * LOG2E)
            else:
                kv_ref[:, n_kv * LANES:(n_kv + 1) * LANES] = y.astype(BF16)
                n_kv += 1
            o_ref[cb * LANES:(cb + 1) * LANES, :] = y.T.astype(BF16)

    tab = pl.BlockSpec((ts, LANES), lambda i: (i, 0))
    vec = pl.BlockSpec((1, LANES), lambda i: (0, 0))
    return pl.pallas_call(
        body, name=name, grid=(s_len // ts,),
        in_specs=[pl.BlockSpec((ts, width), lambda i: (i, 0)), tab, tab, tab, tab, vec, vec],
        out_specs=[pl.BlockSpec((width, ts), lambda i: (0, i)), pl.BlockSpec((ts, 4 * LANES), lambda i: (i, 0))],
        out_shape=[_sds((width, s_len), BF16), _sds((s_len, 4 * LANES), BF16)],
        compiler_params=_params("arbitrary"),
    )(proj, cos1, sin1, cos2, sin2, qn_g, kn_g)


def prep_bwd(proj, dqa, dka, dva, dqb, dkb, dvb, tabs, qn_g, kn_g, name):
    s_len, width = proj.shape
    ts = _row_tile(s_len, 256)
    cos1, sin1, cos2, sin2 = tabs

    def body(p_ref, dqa_ref, dka_ref, dva_ref, dqb_ref, dkb_ref, dvb_ref,
             c1_ref, s1_ref, c2_ref, s2_ref, qg_ref, kg_ref, o_ref, dqg_ref, dkg_ref):
        i = pl.program_id(0)
        lane = _lane_iota(ts)
        c1, s1, c2, s2 = c1_ref[...], s1_ref[...], c2_ref[...], s2_ref[...]

        def rope_t(dy, cos, sin, dist):
            return dy * cos + _swap(dy * sin, dist, lane)

        ones = _head_ones()

        def norm_bwd(dy, x, gain):
            r = lax.rsqrt(_head_sum(x * x, ones) * (1.0 / HEAD_DIM) + EPS)
            xhat = x * r
            dgain = jnp.sum(dy * xhat, axis=0, keepdims=True)
            dxh = dy * gain
            dx = r * (dxh - xhat * (_head_sum(dxh * xhat, ones) * (1.0 / HEAD_DIM)))
            return dx, dgain

        dqg = jnp.zeros((1, LANES), F32)
        dkg = jnp.zeros((1, LANES), F32)
        for cb, kind in enumerate(CHUNK_KIND):
            cols = slice(cb * LANES, (cb + 1) * LANES)
            if kind == "qa":
                dx = rope_t(dqa_ref[cols, :].T * Q_SCALE, c1, s1, 32)
            elif kind == "ka":
                dx = rope_t(dka_ref[...], c1, s1, 32)
            elif kind == "va":
                dx = dva_ref[...]
            elif kind == "qb":
                qcols = slice((cb - 6) * LANES, (cb - 5) * LANES)
                dy = rope_t(dqb_ref[qcols, :].T * Q_SCALE, c2, s2, 16)
                dx, dgain = norm_bwd(dy, p_ref[:, cols], qg_ref[...])
                dqg = dqg + dgain
            elif kind == "kb":
                dy = rope_t(dkb_ref[...], c2, s2, 16)
                dx, dgain = norm_bwd(dy, p_ref[:, cols], kg_ref[...])
                dkg = dkg + dgain
            else:
                dx = dvb_ref[...]
            o_ref[:, cols] = dx.astype(BF16)

        @pl.when(i == 0)
        def _():
            dqg_ref[...] = dqg
            dkg_ref[...] = dkg

        @pl.when(i > 0)
        def _():
            dqg_ref[...] += dqg
            dkg_ref[...] += dkg

    tab = pl.BlockSpec((ts, LANES), lambda i: (i, 0))
    vec = pl.BlockSpec((1, LANES), lambda i: (0, 0))

    def dq_spec(dq):
        per = dq.shape[2] // ts
        return pl.BlockSpec((None, 4 * LANES, ts), lambda i: (i // per, 0, i % per))

    return pl.pallas_call(
        body, name=name, grid=(s_len // ts,),
        in_specs=([pl.BlockSpec((ts, width), lambda i: (i, 0)), dq_spec(dqa), tab, tab, dq_spec(dqb), tab, tab]
                  + [tab] * 4 + [vec, vec]),
        out_specs=[pl.BlockSpec((ts, width), lambda i: (i, 0)), vec, vec],
        out_shape=[_sds((s_len, width), BF16), _sds((1, LANES), F32), _sds((1, LANES), F32)],
        compiler_params=_params("arbitrary"),
    )(proj, dqa, dka, dva, dqb, dkb, dvb, cos1, sin1, cos2, sin2, qn_g, kn_g)


NEG = -1e30
GROUP = 4
KV_HEADS = 2
GROUP_W = GROUP * HEAD_DIM
LSE_ROWS = 8
ONES_ROWS = 16


def _pos_mask_t(k_start, q_start, s_len, tk, tq):
    kpos = k_start + lax.broadcasted_iota(jnp.int32, (tk, tq), 0)
    qpos = q_start + lax.broadcasted_iota(jnp.int32, (tk, tq), 1)
    return (jnp.abs(kpos - qpos) <= BLOCK) & (kpos >= 0) & (kpos < s_len)


def flash_fwd_t(qkv_t, kv_tok, q_rb, k_i, v_rb, sink, window, name, comm=None):
    s_len = qkv_t.shape[1]
    if window:
        tq = _row_tile(s_len, 512)
        tk = 2 * BLOCK
        assert tq == 2 * tk, "the band parts below are written for query blocks of two key blocks"
        n_kv = 2
    else:
        tq, tk = _row_tile(s_len, 1024), _row_tile(s_len, 4096)
        n_kv = s_len // tk
    n_i = s_len // tq
    c_ins, c_outs, c_remote = comm if comm else ([], [], [])
    n_main = 6 if window else 3

    def body(*refs):
        main, c_in_refs = refs[:n_main], refs[n_main:n_main + len(c_ins)]
        rest = refs[n_main + len(c_ins):]
        (o_ref, lse_ref), c_out_refs = rest[:2], rest[2:2 + len(c_outs)]
        m_sc, acc_sc = rest[2 + len(c_outs):4 + len(c_outs)]
        c_sems = rest[4 + len(c_outs):]
        if window:
            sink_ref, q_ref, k_ref, v_ref, kc_ref, vc_ref = main
        else:
            q_ref, k_ref, v_ref = main
        h, i, t = pl.program_id(0), pl.program_id(1), pl.program_id(2)
        if comm:
            @pl.when((h == 0) & (i == 0) & (t == 0))
            def _():
                _exchange_start(c_remote, c_in_refs, c_out_refs, *c_sems)

        @pl.when(t == 0)
        def _():
            for g in range(GROUP):
                acc_sc[g, 0:HEAD_DIM, :] = jnp.zeros((HEAD_DIM, tq), F32)
                if window:
                    m_sc[g] = jnp.full((1, tq), sink_ref[h * GROUP + g] * LOG2E, F32)
                    acc_sc[g, HEAD_DIM:, :] = jnp.ones((ONES_ROWS, tq), F32)
                else:
                    m_sc[g] = jnp.full((1, tq), NEG, F32)
                    acc_sc[g, HEAD_DIM:, :] = jnp.zeros((ONES_ROWS, tq), F32)

        def tile(k_src, v_src, n_keys, key_pos, q_lo, q_hi):
            qs = slice(q_lo, q_hi)
            k = k_src[...]
            v_t = jnp.concatenate([v_src[...], jnp.ones((ONES_ROWS, n_keys), BF16)], axis=0)
            if window:
                mask = _pos_mask_t(key_pos, i * tq + q_lo, s_len, n_keys, q_hi - q_lo)
            s_next = jnp.dot(k, q_ref[0:HEAD_DIM, qs], preferred_element_type=F32)
            for g in range(GROUP):
                s_t = s_next
                if g + 1 < GROUP:
                    s_next = jnp.dot(k, q_ref[(g + 1) * HEAD_DIM:(g + 2) * HEAD_DIM, qs],
                                     preferred_element_type=F32)
                if window:
                    s_t = jnp.where(mask, s_t, NEG)
                m_prev = m_sc[g, :, qs]
                m_new = jnp.maximum(m_prev, jnp.max(s_t, axis=0, keepdims=True))
                alpha = jnp.exp2(m_prev - m_new)
                p_t = jnp.exp2(s_t - m_new)
                acc_sc[g, :, qs] = alpha * acc_sc[g, :, qs] + jnp.dot(v_t, p_t.astype(BF16),
                                                                     preferred_element_type=F32)
                m_sc[g, :, qs] = m_new

        if window:
            @pl.when(t == 0)
            def _():
                tile(k_ref, v_ref, tk, i * tq, 0, tq - BLOCK)
                tile(kc_ref, vc_ref, BLOCK, i * tq - BLOCK, 0, BLOCK)

            @pl.when(t == 1)
            def _():
                tile(k_ref, v_ref, tk, i * tq + tk, BLOCK, tq)
                tile(kc_ref, vc_ref, BLOCK, i * tq + tq, tq - BLOCK, tq)
        else:
            tile(k_ref, v_ref, tk, t * tk, 0, tq)

        @pl.when(t == n_kv - 1)
        def _():
            for g in range(GROUP):
                l = acc_sc[g, HEAD_DIM:HEAD_DIM + 1, :]
                o_ref[g * HEAD_DIM:(g + 1) * HEAD_DIM, :] = (acc_sc[g, 0:HEAD_DIM, :] / l).astype(BF16)
                lse_ref[g * LSE_ROWS:(g + 1) * LSE_ROWS, :] = jnp.broadcast_to(
                    m_sc[g] + jnp.log(l) * LOG2E, (LSE_ROWS, tq))

        if comm:
            @pl.when((h == KV_HEADS - 1) & (i == n_i - 1) & (t == n_kv - 1))
            def _():
                _exchange_finish(c_remote, c_in_refs, c_out_refs, *c_sems)

    kv_blk = (lambda i, t: 2 * i + t) if window else (lambda i, t: t)
    hbm = pl.BlockSpec(memory_space=pl.ANY)
    in_specs = [pl.BlockSpec((GROUP_W, tq), lambda h, i, t: (q_rb + h, i)),
                pl.BlockSpec((None, tk, HEAD_DIM), lambda h, i, t: (k_i + h, kv_blk(i, t), 0)),
                pl.BlockSpec((HEAD_DIM, tk), lambda h, i, t: (v_rb + h, kv_blk(i, t)))]
    args = [qkv_t, kv_tok, qkv_t]
    if window:
        corner = lambda i, t: jnp.clip((tq // BLOCK) * i - 1 + (tq // BLOCK + 1) * t, 0, s_len // BLOCK - 1)
        in_specs = ([pl.BlockSpec(memory_space=pltpu.SMEM)] + in_specs
                    + [pl.BlockSpec((None, BLOCK, HEAD_DIM), lambda h, i, t: (k_i + h, corner(i, t), 0)),
                       pl.BlockSpec((HEAD_DIM, BLOCK), lambda h, i, t: (v_rb + h, corner(i, t)))])
        args = [sink] + args + [kv_tok, qkv_t]
    return pl.pallas_call(
        body, name=name, grid=(KV_HEADS, n_i, n_kv),
        in_specs=in_specs + [hbm] * len(c_ins),
        out_specs=[pl.BlockSpec((GROUP_W, tq), lambda h, i, t: (h, i)),
                   pl.BlockSpec((GROUP * LSE_ROWS, tq), lambda h, i, t: (h, i))] + [hbm] * len(c_outs),
        out_shape=[_sds((KV_HEADS * GROUP_W, s_len), BF16),
                   _sds((KV_HEADS * GROUP * LSE_ROWS, s_len), F32)] + list(c_outs),
        scratch_shapes=[pltpu.VMEM((GROUP, 1, tq), F32),
                        pltpu.VMEM((GROUP, HEAD_DIM + ONES_ROWS, tq), F32)] + _exchange_sems(c_remote),
        compiler_params=_params("arbitrary", "arbitrary", "arbitrary"),
    )(*args, *c_ins)


def flash_bwd_t(qkv_t, kv_tok, o_t, do_t, lse, q_rb, k_i, v_i, k_rb, do_rb, sink, window, name, comm=None):
    s_len = qkv_t.shape[1]
    if window:
        tq = _row_tile(s_len, 512)
        tk = 2 * BLOCK
        assert tq == 2 * tk, "the band parts below are written for query blocks of two key blocks"
        n_q = 1
    else:
        tq, tk = _row_tile(s_len, 2048), _row_tile(s_len, 1024)
        n_q = s_len // tq
    n_qb = s_len // tq
    n_j = s_len // tk
    c_ins, c_outs, c_remote = comm if comm else ([], [], [])
    n_main = 12 if window else 7
    n_out = 4 if window else 3

    def body(*refs):
        main, c_in_refs = refs[:n_main], refs[n_main:n_main + len(c_ins)]
        rest = refs[n_main + len(c_ins):]
        outs, c_out_refs = rest[:n_out], rest[n_out:n_out + len(c_outs)]
        dk_sc, dv_sc = rest[n_out + len(c_outs):n_out + len(c_outs) + 2]
        c_sems = rest[n_out + len(c_outs) + 2:]
        if window:
            sink_ref, q_ref, k_ref, v_ref, kt_ref, o_ref, do_ref, lse_ref = main[:8]
            corner_src = main[8:]
            dq_ref, dk_ref, dv_ref, dsink_ref = outs
        else:
            q_ref, k_ref, v_ref, kt_ref, o_ref, do_ref, lse_ref = main
            dq_ref, dk_ref, dv_ref = outs
        main_src = (q_ref, o_ref, do_ref, lse_ref)
        h, j, t = pl.program_id(0), pl.program_id(1), pl.program_id(2)
        if comm:
            @pl.when((h == 0) & (j == 0) & (t == 0))
            def _():
                _exchange_start(c_remote, c_in_refs, c_out_refs, *c_sems)

        @pl.when((j == 0) & (t == 0))
        def _():
            dq_ref[...] = jnp.zeros(dq_ref.shape, F32)
            if window:
                dsink_ref[...] = jnp.zeros((8, LANES), F32)

        @pl.when(t == 0)
        def _():
            dk_sc[...] = jnp.zeros((tk, HEAD_DIM), F32)
            dv_sc[...] = jnp.zeros((tk, HEAD_DIM), F32)

        def tile(src, q_lo, q_hi, q_pos, k_lo, k_hi, dq_blk, dq_lo, sink_lo=0, sink_hi=0):
            q_src, o_src, do_src, lse_src = src
            ks, qs = slice(k_lo, k_hi), slice(q_lo, q_hi)
            dqs = slice(dq_lo, dq_lo + q_hi - q_lo)
            k, v, k_t = k_ref[ks, :], v_ref[ks, :], kt_ref[:, ks]
            if window:
                mask = _pos_mask_t(j * tk + k_lo, q_pos, s_len, k_hi - k_lo, q_hi - q_lo)
                lane = lax.broadcasted_iota(jnp.int32, (8, LANES), 1)
                sink_tile = jnp.zeros((8, LANES), F32)
            dk_acc = dk_sc[ks, :]
            dv_acc = dv_sc[ks, :]
            for g in range(GROUP):
                rows = slice(g * HEAD_DIM, (g + 1) * HEAD_DIM)
                q_t, o_g, do_g = q_src[rows, qs], o_src[rows, qs], do_src[rows, qs]
                s_t = jnp.dot(k, q_t, preferred_element_type=F32)
                if window:
                    s_t = jnp.where(mask, s_t, NEG)
                lse_row = lse_src[g * LSE_ROWS:g * LSE_ROWS + 1, qs]
                p_t = jnp.exp2(s_t - lse_row)
                delta = jnp.sum(do_g.astype(F32) * o_g.astype(F32), axis=0, keepdims=True)
                dp_t = jnp.dot(v, do_g, preferred_element_type=F32)
                ds_t = (p_t * (dp_t - delta)).astype(BF16)
                dv_acc = dv_acc + lax.dot_general(p_t.astype(BF16), do_g, NT_DIMS, preferred_element_type=F32)
                dk_acc = dk_acc + lax.dot_general(ds_t, q_t, NT_DIMS, preferred_element_type=F32)
                dq_ref[dq_blk, rows, dqs] += jnp.dot(k_t, ds_t, preferred_element_type=F32)
                if sink_hi > sink_lo:
                    at = slice(sink_lo - q_lo, sink_hi - q_lo)
                    p_sink = jnp.exp2(sink_ref[h * GROUP + g] * LOG2E - lse_row[:, at])
                    term = -jnp.sum(p_sink * delta[:, at], axis=1, keepdims=True)
                    sink_tile = jnp.where(lane == g, term, sink_tile)
            dk_sc[ks, :] = dk_acc
            dv_sc[ks, :] = dv_acc
            if sink_hi > sink_lo:
                dsink_ref[...] += sink_tile

        if window:
            m = j // 2

            @pl.when(j % 2 == 0)
            def _():
                tile(main_src, 0, tq - BLOCK, m * tq, 0, tk, m, 0, 0, tq - BLOCK)

            @pl.when((j % 2 == 0) & (j > 0))
            def _():
                tile(corner_src, 0, BLOCK, m * tq - BLOCK, 0, BLOCK, m - 1, tq - BLOCK)

            @pl.when(j % 2 == 1)
            def _():
                tile(main_src, BLOCK, tq, m * tq + BLOCK, 0, tk, m, BLOCK, tq - BLOCK, tq)

            @pl.when((j % 2 == 1) & (j < n_j - 1))
            def _():
                tile(corner_src, 0, BLOCK, (m + 1) * tq, tk - BLOCK, tk, m + 1, 0)
        else:
            tile(main_src, 0, tq, t * tq, 0, tk, t, 0)

        @pl.when(t == n_q - 1)
        def _():
            dk_ref[...] = dk_sc[...] * LN2
            dv_ref[...] = dv_sc[...]

        if comm:
            @pl.when((h == KV_HEADS - 1) & (j == n_j - 1) & (t == n_q - 1))
            def _():
                _exchange_finish(c_remote, c_in_refs, c_out_refs, *c_sems)

    qb = (lambda j, t: j // 2) if window else (lambda j, t: t)
    hbm = pl.BlockSpec(memory_space=pl.ANY)
    in_specs = [pl.BlockSpec((GROUP_W, tq), lambda h, j, t: (q_rb + h, qb(j, t))),
                pl.BlockSpec((None, tk, HEAD_DIM), lambda h, j, t: (k_i + h, j, 0)),
                pl.BlockSpec((None, tk, HEAD_DIM), lambda h, j, t: (v_i + h, j, 0)),
                pl.BlockSpec((HEAD_DIM, tk), lambda h, j, t: (k_rb + h, j)),
                pl.BlockSpec((GROUP_W, tq), lambda h, j, t: (h, qb(j, t))),
                pl.BlockSpec((GROUP_W, tq), lambda h, j, t: (do_rb + h, qb(j, t))),
                pl.BlockSpec((GROUP * LSE_ROWS, tq), lambda h, j, t: (h, qb(j, t)))]
    args = [qkv_t, kv_tok, kv_tok, qkv_t, o_t, do_t, lse]
    kv_out = _sds((KV_HEADS, s_len, HEAD_DIM), F32)
    out_specs = [pl.BlockSpec((n_qb, GROUP_W, tq), lambda h, j, t: (0, h, 0)),
                 pl.BlockSpec((None, tk, HEAD_DIM), lambda h, j, t: (h, j, 0)),
                 pl.BlockSpec((None, tk, HEAD_DIM), lambda h, j, t: (h, j, 0))]
    out_shape = [_sds((n_qb, KV_HEADS * GROUP_W, tq), F32), kv_out, kv_out]
    if window:
        cq = lambda j: jnp.clip(2 * j - 1 + 3 * (j % 2), 0, s_len // BLOCK - 1)
        in_specs = ([pl.BlockSpec(memory_space=pltpu.SMEM)] + in_specs
                    + [pl.BlockSpec((GROUP_W, BLOCK), lambda h, j, t: (q_rb + h, cq(j))),
                       pl.BlockSpec((GROUP_W, BLOCK), lambda h, j, t: (h, cq(j))),
                       pl.BlockSpec((GROUP_W, BLOCK), lambda h, j, t: (do_rb + h, cq(j))),
                       pl.BlockSpec((GROUP * LSE_ROWS, BLOCK), lambda h, j, t: (h, cq(j)))])
        args = [sink] + args + [qkv_t, o_t, do_t, lse]
        out_specs.append(pl.BlockSpec((None, 8, LANES), lambda h, j, t: (h, 0, 0)))
        out_shape.append(_sds((KV_HEADS, 8, LANES), F32))
    return pl.pallas_call(
        body, name=name, grid=(KV_HEADS, n_j, n_q),
        in_specs=in_specs + [hbm] * len(c_ins), out_specs=out_specs + [hbm] * len(c_outs),
        out_shape=out_shape + list(c_outs),
        scratch_shapes=[pltpu.VMEM((tk, HEAD_DIM), F32), pltpu.VMEM((tk, HEAD_DIM), F32)]
        + _exchange_sems(c_remote),
        compiler_params=_params("arbitrary", "arbitrary", "arbitrary"),
    )(*args, *c_ins)


SGU_GROUPS = 8
SGU_CHUNK = 128
GELU_C = float(np.sqrt(2.0 / np.pi))
GELU_A = 0.044715


def _gelu_and_grad(x):
    x2 = x * x
    t = jnp.tanh(x * (GELU_C + (GELU_C * GELU_A) * x2))
    hx = 0.5 * x
    return hx + hx * t, (0.5 + 0.5 * t) + (hx * (1.0 - t * t)) * (GELU_C + (3.0 * GELU_C * GELU_A) * x2)


def _gelu(x):
    t = jnp.tanh(x * (GELU_C + (GELU_C * GELU_A) * (x * x)))
    hx = 0.5 * x
    return hx + hx * t


def _layernorm_stats(v):
    mu = jnp.mean(v, axis=-1, keepdims=True)
    var = jnp.mean(jnp.square(v - mu), axis=-1, keepdims=True)
    rstd = lax.rsqrt(var + EPS)
    return (v - mu) * rstd, rstd


def sgu_mid_fwd(zpre, ln_g, ln_b, ws, bsb, name):
    s_len, width = zpre.shape
    d = width // 2
    ts = _row_tile(s_len, 256)

    def body(z_ref, g_ref, b_ref, ws_ref, bs_ref, y_ref):
        z = _gelu(z_ref[...])
        u, v = z[:, :d], z[:, d:]
        vhat, _ = _layernorm_stats(v)
        vn = (vhat * g_ref[...] + b_ref[...]).astype(BF16)
        for n in range(ts // SGU_CHUNK):
            rows = slice(n * SGU_CHUNK, (n + 1) * SGU_CHUNK)
            for g in range(SGU_GROUPS):
                cols = slice(g * LANES, (g + 1) * LANES)
                mixed = jnp.dot(ws_ref[g], vn[rows, cols], preferred_element_type=F32) + bs_ref[g]
                y_ref[rows, cols] = (u[rows, cols] * mixed).astype(BF16)

    vec = pl.BlockSpec((1, d), lambda i: (0, 0))
    cube = pl.BlockSpec((SGU_GROUPS, SGU_CHUNK, SGU_CHUNK), lambda i: (0, 0, 0))
    return pl.pallas_call(
        body, name=name, grid=(s_len // ts,),
        in_specs=[pl.BlockSpec((ts, width), lambda i: (i, 0)), vec, vec, cube, cube],
        out_specs=pl.BlockSpec((ts, d), lambda i: (i, 0)),
        out_shape=_sds((s_len, d), BF16),
        compiler_params=_params("arbitrary"),
    )(zpre, ln_g, ln_b, ws, bsb)


def sgu_mid_bwd(zpre, dy, ln_g, ln_b, ws, wst, bsb, name):
    s_len, width = zpre.shape
    d = width // 2
    ts = _row_tile(s_len, 256)
    n_steps = s_len // ts

    def body(z_ref, dy_ref, g_ref, b_ref, ws_ref, wst_ref, bs_ref,
             dz_ref, dws_ref, dbs_ref, dg_ref, db_ref, du_sc, dvn_sc):
        i = pl.program_id(0)

        @pl.when(i == 0)
        def _():
            dws_ref[...] = jnp.zeros(dws_ref.shape, F32)
            dbs_ref[...] = jnp.zeros(dbs_ref.shape, F32)
            dg_ref[...] = jnp.zeros(dg_ref.shape, F32)
            db_ref[...] = jnp.zeros(db_ref.shape, F32)

        zp = z_ref[...]
        z, gp = _gelu_and_grad(zp)
        u, v = z[:, :d], z[:, d:]
        vhat, rstd = _layernorm_stats(v)
        gain = g_ref[...]
        vn = (vhat * gain + b_ref[...]).astype(BF16)
        dyf = dy_ref[...].astype(F32)
        for n in range(ts // SGU_CHUNK):
            rows = slice(n * SGU_CHUNK, (n + 1) * SGU_CHUNK)
            for g in range(SGU_GROUPS):
                cols = slice(g * LANES, (g + 1) * LANES)
                vt = vn[rows, cols]
                mixed = jnp.dot(ws_ref[g], vt, preferred_element_type=F32) + bs_ref[g]
                dyt = dyf[rows, cols]
                du_sc[rows, cols] = dyt * mixed
                dmixed = dyt * u[rows, cols]
                dmb = dmixed.astype(BF16)
                dvn_sc[rows, cols] = jnp.dot(wst_ref[g], dmb, preferred_element_type=F32)
                dws_ref[g] += lax.dot_general(dmb, vt, NT_DIMS, preferred_element_type=F32)
                dbs_ref[g] += dmixed
        dvn = dvn_sc[...]
        dg_ref[...] += jnp.sum(dvn * vhat, axis=0, keepdims=True)
        db_ref[...] += jnp.sum(dvn, axis=0, keepdims=True)
        dvh = dvn * gain
        dv = rstd * (dvh - jnp.mean(dvh, axis=-1, keepdims=True)
                     - vhat * jnp.mean(dvh * vhat, axis=-1, keepdims=True))
        dz_ref[:, :d] = (du_sc[...] * gp[:, :d]).astype(BF16)
        dz_ref[:, d:] = (dv * gp[:, d:]).astype(BF16)

        @pl.when(i == n_steps - 1)
        def _():
            for g in range(SGU_GROUPS):
                tot = jnp.sum(dbs_ref[g], axis=1, keepdims=True)
                dbs_ref[g] = jnp.broadcast_to(tot, (SGU_CHUNK, LANES))

    vec = pl.BlockSpec((1, d), lambda i: (0, 0))
    cube = pl.BlockSpec((SGU_GROUPS, SGU_CHUNK, SGU_CHUNK), lambda i: (0, 0, 0))
    cube_shape = _sds((SGU_GROUPS, SGU_CHUNK, SGU_CHUNK), F32)
    return pl.pallas_call(
        body, name=name, grid=(n_steps,),
        in_specs=[pl.BlockSpec((ts, width), lambda i: (i, 0)), pl.BlockSpec((ts, d), lambda i: (i, 0)),
                  vec, vec, cube, cube, cube],
        out_specs=[pl.BlockSpec((ts, width), lambda i: (i, 0)), cube, cube, vec, vec],
        out_shape=[_sds((s_len, width), BF16), cube_shape, cube_shape, _sds((1, d), F32), _sds((1, d), F32)],
        scratch_shapes=[pltpu.VMEM((ts, d), F32), pltpu.VMEM((ts, d), F32)],
        compiler_params=_params("arbitrary"),
    )(zpre, dy, ln_g, ln_b, ws, wst, bsb)


def loss_head(x, g, target, name):
    s_len, d = x.shape
    tm = _row_tile(s_len, 512)

    def body(x_ref, g_ref, t_ref, dx_ref, dxb_ref, dg_ref, loss_ref):
        i = pl.program_id(0)
        xf = x_ref[...]
        gain = g_ref[...]
        r = lax.rsqrt(jnp.mean(xf * xf, axis=-1, keepdims=True) + EPS)
        xhat = xf * r
        err = xhat * gain - t_ref[...]
        row = jnp.mean(err * err, axis=-1, keepdims=True)
        part = 0.5 * jnp.sum(row, axis=0, keepdims=True)
        dy = err * (1.0 / d)
        dg_part = jnp.sum(dy * xhat, axis=0, keepdims=True)

        @pl.when(i == 0)
        def _():
            dg_ref[...] = dg_part
            loss_ref[...] = jnp.broadcast_to(part, (8, LANES))

        @pl.when(i > 0)
        def _():
            dg_ref[...] += dg_part
            loss_ref[...] += jnp.broadcast_to(part, (8, LANES))

        dxh = dy * gain
        dx = r * (dxh - xhat * jnp.mean(dxh * xhat, axis=-1, keepdims=True))
        dx_ref[...] = dx
        dxb_ref[...] = dx.astype(BF16)

    row_spec = pl.BlockSpec((tm, d), lambda i: (i, 0))
    vec = pl.BlockSpec((1, d), lambda i: (0, 0))
    return pl.pallas_call(
        body, name=name, grid=(s_len // tm,),
        in_specs=[row_spec, vec, row_spec],
        out_specs=[row_spec, row_spec, vec, pl.BlockSpec((8, LANES), lambda i: (0, 0))],
        out_shape=[_sds((s_len, d), F32), _sds((s_len, d), BF16), _sds((1, d), F32), _sds((8, LANES), F32)],
        compiler_params=_params("arbitrary"),
    )(x, g, target)


FLIP_BITS = {"c": (0, 0, 1), "x": (1, 0, 0), "y": (0, 1, 0), "xy": (1, 1, 0),
             "xc": (1, 0, 1), "yc": (0, 1, 1), "xyc": (1, 1, 1)}
CHIP_FLIPS = ("x", "y", "xy")


def _flip(pos, name):
    return tuple(1 - p if bit else p for p, bit in zip(pos, FLIP_BITS[name]))


def _chip(pos):
    return 2 * pos[0] + pos[1]


def _me():
    return (lax.axis_index("x"), lax.axis_index("y"), lax.axis_index("c"))


def _exchange_copy(remote, k, in_refs, out_refs, send_sems, recv_sems, sender, receiver):
    ii, src_fn, oi, dst_fn, _ = remote[k]
    return pltpu.make_async_remote_copy(
        src_ref=src_fn(in_refs[ii], sender, receiver), dst_ref=dst_fn(out_refs[oi], sender),
        send_sem=send_sems.at[k], recv_sem=recv_sems.at[k], device_id=receiver, device_id_type=MESH)


def _exchange_start(remote, in_refs, out_refs, send_sems, recv_sems):
    me = _me()
    for k in range(len(remote)):
        _exchange_copy(remote, k, in_refs, out_refs, send_sems, recv_sems, me, _flip(me, remote[k][4])).start()


def _exchange_finish(remote, in_refs, out_refs, send_sems, recv_sems):
    me = _me()
    for k in range(len(remote)):
        _exchange_copy(remote, k, in_refs, out_refs, send_sems, recv_sems, _flip(me, remote[k][4]), me).wait_recv()
    for k in range(len(remote)):
        _exchange_copy(remote, k, in_refs, out_refs, send_sems, recv_sems, me, _flip(me, remote[k][4])).wait_send()


def _exchange_sems(remote):
    n = len(remote)
    return [pltpu.SemaphoreType.DMA((n,)), pltpu.SemaphoreType.DMA((n,))] if n else []


def exchange(name, ins, out_shapes, remote, local):
    n_in, n_out = len(ins), len(out_shapes)

    def body(*refs):
        in_refs, out_refs = refs[:n_in], refs[n_in:n_in + n_out]
        send_sems, recv_sems, local_sems = refs[n_in + n_out:]
        me = _me()
        stays = []
        for k, (ii, src_fn, oi, dst_fn) in enumerate(local):
            cp = pltpu.make_async_copy(src_fn(in_refs[ii], me), dst_fn(out_refs[oi], me), local_sems.at[k])
            cp.start()
            stays.append(cp)
        _exchange_start(remote, in_refs, out_refs, send_sems, recv_sems)
        _exchange_finish(remote, in_refs, out_refs, send_sems, recv_sems)
        for cp in stays:
            cp.wait()

    hbm = pl.BlockSpec(memory_space=pl.ANY)
    return pl.pallas_call(
        body, name=name,
        in_specs=[hbm] * n_in, out_specs=[hbm] * n_out, out_shape=list(out_shapes),
        scratch_shapes=[pltpu.SemaphoreType.DMA((max(len(remote), 1),)),
                        pltpu.SemaphoreType.DMA((max(len(remote), 1),)),
                        pltpu.SemaphoreType.DMA((max(len(local), 1),))],
        compiler_params=pltpu.CompilerParams(has_side_effects=True),
    )(*ins)


def staged_push(name, ins, out_shapes, jobs, n_alias=0):
    n_in, n_out = len(ins), len(out_shapes)
    n_copies = sum(len(dsts) for _, _, dsts in jobs)
    n_remote = sum(1 for _, _, dsts in jobs for d in dsts if d[2] is not None)

    def chunk_of(ii, src_fn):
        probe = _ShapeRef(ins[ii].shape, ins[ii].dtype)
        got = src_fn(probe, (0, 0, 0))
        return tuple(got.shape), got.dtype

    classes = []
    for ii, src_fn, _ in jobs:
        c = chunk_of(ii, src_fn)
        if c not in classes:
            classes.append(c)

    def body(*refs):
        in_refs, out_refs = refs[:n_in], refs[n_in:n_in + n_out]
        bufs = refs[n_in + n_out:n_in + n_out + len(classes)]
        load_sems, out_sems, recv_sems = refs[n_in + n_out + len(classes):]
        me = _me()
        pending = [[[], []] for _ in classes]
        used = [0] * len(classes)
        arrivals = []
        k = r = 0

        def begin_load(job):
            ii, src_fn, _ = job
            cls = classes.index(chunk_of(ii, src_fn))
            slot = used[cls] % 2
            used[cls] += 1
            for kind, cp in pending[cls][slot]:
                cp.wait_send() if kind == "remote" else cp.wait()
            pending[cls][slot] = []
            load = pltpu.make_async_copy(src_fn(in_refs[ii], me), bufs[cls].at[slot], load_sems.at[2 * cls + slot])
            load.start()
            return load, cls, slot

        nxt = begin_load(jobs[0])
        for n, (ii, src_fn, dsts) in enumerate(jobs):
            load, cls, slot = nxt
            load.wait()
            buf = bufs[cls].at[slot]
            sent = []
            for oi, dst_fn, flip in dsts:
                if flip is None:
                    cp = pltpu.make_async_copy(buf, dst_fn(out_refs[oi], me), out_sems.at[k])
                    cp.start()
                    sent.append(("local", cp))
                else:
                    peer = _flip(me, flip)
                    cp = pltpu.make_async_remote_copy(
                        src_ref=buf, dst_ref=dst_fn(out_refs[oi], me), send_sem=out_sems.at[k],
                        recv_sem=recv_sems.at[r], device_id=peer, device_id_type=MESH)
                    cp.start()
                    sent.append(("remote", cp))
                    arrivals.append((r, cls, oi, dst_fn, peer))
                    r += 1
                k += 1
            pending[cls][slot] = sent
            if n + 1 < len(jobs):
                nxt = begin_load(jobs[n + 1])
        for per_class in pending:
            for slot_list in per_class:
                for kind, cp in slot_list:
                    cp.wait_send() if kind == "remote" else cp.wait()
        for r, cls, oi, dst_fn, peer in arrivals:
            pltpu.make_async_remote_copy(
                src_ref=bufs[cls].at[0], dst_ref=dst_fn(out_refs[oi], peer), send_sem=out_sems.at[0],
                recv_sem=recv_sems.at[r], device_id=peer, device_id_type=MESH).wait_recv()

    hbm = pl.BlockSpec(memory_space=pl.ANY)
    return pl.pallas_call(
        body, name=name,
        in_specs=[hbm] * n_in, out_specs=[hbm] * n_out, out_shape=list(out_shapes),
        scratch_shapes=[pltpu.VMEM((2,) + shape, dtype) for shape, dtype in classes]
        + [pltpu.SemaphoreType.DMA((2 * len(classes),)), pltpu.SemaphoreType.DMA((max(n_copies, 1),)),
           pltpu.SemaphoreType.DMA((max(n_remote, 1),))],
        input_output_aliases={i: i for i in range(n_alias)},
        compiler_params=pltpu.CompilerParams(has_side_effects=True, vmem_limit_bytes=VMEM_LIMIT),
    )(*ins)


class _ShapeRef:
    def __init__(self, shape, dtype):
        self.shape, self.dtype = tuple(shape), dtype

    @property
    def at(self):
        return self

    def __getitem__(self, idx):
        idx = idx if isinstance(idx, tuple) else (idx,)
        shape = []
        for dim, i in zip(self.shape, idx):
            if isinstance(i, slice):
                shape.append(len(range(*i.indices(dim))))
            elif hasattr(i, "size") and hasattr(i, "start"):
                shape.append(i.size)
        shape += self.shape[len(idx):]
        return _ShapeRef(shape, self.dtype)


def gather_whole(shards, name):
    whole = lambda ref, sender, receiver=None: ref
    slot = lambda ref, sender: ref.at[_chip(sender)]
    remote = [(t, whole, t, slot, flip) for t in range(len(shards)) for flip in CHIP_FLIPS]
    local = [(t, whole, t, slot) for t in range(len(shards))]
    outs = [_sds((N_CHIPS,) + a.shape, a.dtype) for a in shards]
    return exchange(name, list(shards), outs, remote, local)


def _half_axis(shape):
    return 0 if shape[0] >= 2 else 1


def gather_halves_plan(shards):
    remote = []
    for t, a in enumerate(shards):
        ax = _half_axis(a.shape)
        half = lambda ref, sender, receiver=None, ax=ax: _half(ref, sender[2], ax)
        slot = lambda ref, sender, ax=ax: _half(ref.at[_chip(sender)], sender[2], ax)
        remote += [(t, half, t, slot, flip) for flip in CHIP_FLIPS]
    outs = [_sds((N_CHIPS,) + a.shape, a.dtype) for a in shards]
    return list(shards), outs, remote


def gather_halves_fill(got, shards, name):
    n_t = len(shards)
    jobs = []
    for t, a in enumerate(shards):
        layers = a.shape[0]
        for l in range(layers):
            jobs.append((n_t + t, lambda ref, me, l=l: ref.at[l],
                         [(t, lambda ref, sender, l=l: ref.at[_chip(sender), l], None)]))
        for flip in CHIP_FLIPS:
            if _half_axis(a.shape) == 0:
                n = layers // 2
                for j in range(n):
                    at = lambda ref, pos, flip=flip, j=j, n=n: ref.at[_chip(_flip(pos, flip)), pos[2] * n + j]
                    jobs.append((t, at, [(t, at, "c")]))
            else:
                rows = a.shape[1] // 2
                at = lambda ref, pos, flip=flip, rows=rows: ref.at[
                    _chip(_flip(pos, flip)), 0, pl.ds(pos[2] * rows, rows)]
                jobs.append((t, at, [(t, at, "c")]))
    outs = [_sds(g.shape, g.dtype) for g in got]
    return staged_push(name, list(got) + list(shards), outs, jobs, n_alias=n_t)


def add_to_all(plan, buf):
    ins, outs, remote = plan
    whole = lambda ref, sender, receiver=None: ref
    more = [(len(ins), whole, len(outs), (lambda ref, sender, f=f: ref.at[f]), flip)
            for f, flip in enumerate(FLIPS_BY_INDEX)]
    return list(ins) + [buf], list(outs) + [_sds((len(more),) + buf.shape, buf.dtype)], list(remote) + more


FLIPS_BY_INDEX = ("c", "y", "yc", "x", "xc", "xy", "xyc")


def sum_devices(own, got, name):
    rows = own.shape[0]
    tr = LANES if rows % LANES == 0 else rows
    me = (4 * lax.axis_index("x") + 2 * lax.axis_index("y") + lax.axis_index("c")).astype(jnp.int32).reshape(1)
    everyone = jnp.concatenate([own[None], got], axis=0)

    def body(me_ref, a0, a1, a2, a3, a4, a5, a6, a7, o_ref):
        o_ref[...] = ((a0[...] + a1[...]) + (a2[...] + a3[...])) + ((a4[...] + a5[...]) + (a6[...] + a7[...]))

    return pl.pallas_call(
        body, name=name,
        grid_spec=pltpu.PrefetchScalarGridSpec(
            num_scalar_prefetch=1, grid=(rows // tr,),
            in_specs=[pl.BlockSpec((None, tr, LANES), lambda i, m, k=k: (m[0] ^ k, i, 0)) for k in range(8)],
            out_specs=pl.BlockSpec((tr, LANES), lambda i, m: (i, 0))),
        out_shape=_sds((rows, LANES), F32),
        compiler_params=_params("arbitrary"),
    )(me, *([everyone] * 8))


def _half(ref, core, axis):
    rows = ref.shape[axis] // 2
    idx = (slice(None),) * axis + (pl.ds(core * rows, rows),)
    return ref.at[idx]


def reduce_plan(grads):
    remote, outs = [], []
    for t, g in enumerate(grads):
        outs.append(_sds((len(CHIP_FLIPS),) + g.shape[1:], BF16))
        for f, flip in enumerate(CHIP_FLIPS):
            remote.append((t, lambda ref, sender, receiver: ref.at[_chip(receiver)],
                           t, lambda ref, sender, f=f: ref.at[f], flip))
    return list(grads), outs, remote


def reduce_finish(grads, got, stacks, full_shapes, into, name):
    chip = (2 * lax.axis_index("x") + lax.axis_index("y")).astype(jnp.int32).reshape(1)
    core = lax.axis_index("c").astype(jnp.int32).reshape(1)
    sums = [sum_chips(g, r, chip, f"{name}_sum{t}") for t, (g, r) in enumerate(zip(grads, got))]
    jobs, outs = [], []
    for t, a in enumerate(sums):
        half, cols = a.shape[0] // 2, a.shape[1]
        outs.append(_sds((half, cols), F32))
        pieces = max(1, half * cols * 4 // STAGE_BYTES)
        step = half // pieces
        for q in range(pieces):
            jobs.append((t, lambda ref, me, q=q, step=step, half=half: ref.at[pl.ds((1 - me[2]) * half + q * step, step)],
                         [(t, lambda ref, sender, q=q, step=step: ref.at[pl.ds(q * step, step)], "c")]))
    theirs = staged_push(name + "_swap", sums, outs, jobs)
    totals = [add_rows(a, r, core, f"{name}_add{t}") for t, (a, r) in enumerate(zip(sums, theirs))]
    names = []
    for out_name, _ in stacks:
        if out_name not in names:
            names.append(out_name)
    names = [n for n in names if n in into] + [n for n in names if n not in into]
    kept = [into[n] for n in names if n in into]
    outs = [_sds(full_shapes[n], F32) for n in names]
    jobs = []
    for t, (out_name, layer) in enumerate(stacks):
        oi = names.index(out_name)
        rows, cols = totals[t].shape
        pieces = max(1, rows * cols * 4 // STAGE_BYTES)
        step = rows // pieces
        for q in range(pieces):
            src = lambda ref, me, q=q, step=step: ref.at[pl.ds(q * step, step)]
            place = lambda ref, sender, layer=layer, q=q, step=step, rows=rows: ref.at[
                layer, pl.ds(sender[2] * rows + q * step, step)]
            jobs.append((len(kept) + t, src, [(oi, place, None), (oi, place, "c")]))
    full = staged_push(name + "_share", kept + totals, outs, jobs, n_alias=len(kept))
    return {**into, **dict(zip(names, full))}


STAGE_BYTES = 1024 * 1024


def add_rows(a, theirs, core, name):
    rows, cols = a.shape
    half = rows // 2
    tr = _row_tile(half, 256)
    nb = half // tr

    def body(core_ref, a_ref, t_ref, o_ref):
        o_ref[...] = a_ref[...] + t_ref[...]

    return pl.pallas_call(
        body, name=name,
        grid_spec=pltpu.PrefetchScalarGridSpec(
            num_scalar_prefetch=1, grid=(nb,),
            in_specs=[pl.BlockSpec((tr, cols), lambda i, c: (c[0] * nb + i, 0)),
                      pl.BlockSpec((tr, cols), lambda i, c: (i, 0))],
            out_specs=pl.BlockSpec((tr, cols), lambda i, c: (i, 0))),
        out_shape=_sds((half, cols), F32),
        compiler_params=_params("arbitrary"),
    )(core, a, theirs)


def sum_chips(mine, theirs, chip, name):
    _, half, cols = mine.shape
    tr = _row_tile(half, 256)

    def body(chip_ref, m_ref, a_ref, b_ref, c_ref, o_ref):
        o_ref[...] = ((m_ref[...].astype(F32) + a_ref[...].astype(F32))
                      + b_ref[...].astype(F32)) + c_ref[...].astype(F32)

    got = lambda f: pl.BlockSpec((None, tr, cols), lambda i, ch: (f, i, 0))
    return pl.pallas_call(
        body, name=name,
        grid_spec=pltpu.PrefetchScalarGridSpec(
            num_scalar_prefetch=1, grid=(half // tr,),
            in_specs=[pl.BlockSpec((None, tr, cols), lambda i, ch: (ch[0], i, 0)), got(0), got(1), got(2)],
            out_specs=pl.BlockSpec((tr, cols), lambda i, ch: (i, 0))),
        out_shape=_sds((half, cols), F32),
        compiler_params=_params("arbitrary"),
    )(chip, mine, theirs, theirs, theirs)


def _tok(t):
    return t.transpose(1, 0, 2).reshape(t.shape[1], t.shape[0] * t.shape[2])


def _heads(t):
    return t.reshape(t.shape[0], t.shape[1] // HEAD_DIM, HEAD_DIM).transpose(1, 0, 2)


def _tile2(vec):
    return jnp.tile(vec.reshape(1, HEAD_DIM), (1, 2))


REST = ("att_w_in", "att_w_out", "sgu_w_in", "sgu_w_out", "mlp_w1", "mlp_w2")
LAST_GROUP = (("att_w_in", 0),)
LATE_SMALL = ("att_norm", "att_sink", "att_qnorm", "att_knorm")


def local_step(x, target, first, rest_shards, rep, full_shapes):
    s_len, d = x.shape
    tabs = rope_tables(s_len)
    depth = rep["mlp_norm"].shape[0]
    row = lambda a: a.reshape(1, -1)
    saved = []
    h = x
    gw = {"att_w_in": [first]}

    def wl(name, idx):
        return (gw[name][idx], 0) if name == "att_w_in" else (gw[name], idx)

    for layer in range(depth):
        i = layer // 2
        tag = f"l{layer}"
        if layer % 2 == 0:
            hn, proj = norm_mm(h, row(rep["att_norm"][i]), *wl("att_w_in", i), F32, tag + "_att_proj")
            qkv_t, kv = prep_fwd(proj, tabs, _tile2(rep["att_qnorm"][i]), _tile2(rep["att_knorm"][i]),
                                 tag + "_att_prep")
            kv_tok = _heads(kv)
            oa, lse_a = flash_fwd_t(qkv_t, kv_tok, QA_COL // GROUP_W, 0, (KA_COL + LANES) // HEAD_DIM,
                                    rep["att_sink"][i], True, tag + "_win_fwd")
            plan = gather_halves_plan(rest_shards) if layer == 0 else None
            ob, lse_b, *got = flash_fwd_t(qkv_t, kv_tok, QB_COL // GROUP_W, 4, (KB_COL + LANES) // HEAD_DIM,
                                          None, False, tag + "_grid_fwd", comm=plan)
            if layer == 0:
                rest = dict(zip(REST, gather_halves_fill(got, rest_shards, "gather_rest_fill")))
                gw["att_w_in"].append(rest.pop("att_w_in"))
                gw.update(rest)
            out = mm_res_t([oa, ob], *wl("att_w_out", i), h, tag + "_att_out")
            mix_saved = (h, hn, proj, qkv_t, kv_tok, oa, ob, lse_a, lse_b)
        else:
            hn, zpre = norm_mm(h, row(rep["sgu_norm"][i]), *wl("sgu_w_in", i), F32, tag + "_sgu_in")
            ws = rep["sgu_w_s"][i].astype(BF16)
            bsb = jnp.broadcast_to(rep["sgu_b_s"][i][:, :, None], (SGU_GROUPS, SGU_CHUNK, LANES))
            y = sgu_mid_fwd(zpre, row(rep["sgu_ln_g"][i]), row(rep["sgu_ln_b"][i]), ws, bsb, tag + "_sgu_mid")
            out = mm_res(y, *wl("sgu_w_out", i), h, tag + "_sgu_out")
            mix_saved = (h, hn, zpre, y, ws, bsb)
        hm, a = norm_mm(out, row(rep["mlp_norm"][layer]), *wl("mlp_w1", layer), BF16, tag + "_mlp_up")
        nxt = mm_res(a, *wl("mlp_w2", layer), out, tag + "_mlp_down", relu2=True)
        saved.append((mix_saved, (out, hm, a)))
        h = nxt
    dh, dhb, d_final, loss_tile = loss_head(h, row(rep["final_norm"]), target, "loss_head")
    big, tags = [], []
    small = {k: [jnp.zeros(v.shape[1:], F32)] * v.shape[0] for k, v in rep.items() if k != "final_norm"}
    small["final_norm"] = d_final.reshape(-1)
    stacked = lambda: [small[n] if n == "final_norm" else jnp.stack(small[n]) for n in SMALL]
    for layer in reversed(range(depth)):
        i = layer // 2
        tag = f"l{layer}"
        mix_saved, (xin, hm, a) = saved[layer]
        da = mm_nt_relu2_bwd(dhb, *wl("mlp_w2", layer), a, tag + "_mlp_down_bwd")
        big.append(dw_mm(a, dhb, tag + "_mlp_dw2", col_sharded=False, relu2=True))
        tags.append(("mlp_w2", layer))
        big.append(dw_mm(hm, da, tag + "_mlp_dw1", col_sharded=True))
        tags.append(("mlp_w1", layer))
        dh, dhb, dg = dx_norm(da, *wl("mlp_w1", layer), xin, row(rep["mlp_norm"][layer]), dh, tag + "_mlp_up_bwd")
        small["mlp_norm"][layer] = dg.reshape(-1)
        if layer % 2 == 0:
            xin, hn, proj, qkv_t, kv_tok, oa, ob, lse_a, lse_b = mix_saved
            do_t = mm_nt(dhb, *wl("att_w_out", i), tag + "_att_out_bwd", transposed=True)
            big.append(dw_nn([oa, ob], dhb, tag + "_att_dwout"))
            tags.append(("att_w_out", i))
            dqa, dka, dva, dsink = flash_bwd_t(qkv_t, kv_tok, oa, do_t, lse_a, QA_COL // GROUP_W, 0, 2,
                                               KA_COL // HEAD_DIM, 0, rep["att_sink"][i], True, tag + "_win_bwd")
            plan = None
            if layer == 0:
                early = [k for k, t in enumerate(tags) if t not in LAST_GROUP]
                early_pack = _pack(stacked() + [loss_tile[0, :1]])
                plan = add_to_all(reduce_plan([big[k] for k in early]), early_pack)
            dqb, dkb, dvb, *got = flash_bwd_t(qkv_t, kv_tok, ob, do_t, lse_b, QB_COL // GROUP_W, 4, 6,
                                              KB_COL // HEAD_DIM, 2, None, False, tag + "_grid_bwd", comm=plan)
            if layer == 0:
                grads = reduce_finish([big[k] for k in early], got[:-1], [tags[k] for k in early], full_shapes,
                                      {}, "grads1")
                early_sum = sum_devices(early_pack, got[-1], "sum_small")
            qg, kg = _tile2(rep["att_qnorm"][i]), _tile2(rep["att_knorm"][i])
            dproj, dqg, dkg = prep_bwd(proj, dqa, _tok(dka), _tok(dva), dqb, _tok(dkb), _tok(dvb),
                                       tabs, qg, kg, tag + "_att_prep_bwd")
            big.append(dw_mm(hn, dproj, tag + "_att_dwin", col_sharded=True))
            tags.append(("att_w_in", i))
            dh, dhb, dg = dx_norm(dproj, *wl("att_w_in", i), xin, row(rep["att_norm"][i]), dh, tag + "_att_proj_bwd")
            small["att_norm"][i] = dg.reshape(-1)
            small["att_sink"][i] = dsink[:, 0, :GROUP].reshape(-1)
            small["att_qnorm"][i] = dqg[0, :HEAD_DIM] + dqg[0, HEAD_DIM:]
            small["att_knorm"][i] = dkg[0, :HEAD_DIM] + dkg[0, HEAD_DIM:]
        else:
            xin, hn, zpre, y, ws, bsb = mix_saved
            dy = mm_nt(dhb, *wl("sgu_w_out", i), tag + "_sgu_out_bwd")
            big.append(dw_mm(y, dhb, tag + "_sgu_dwout", col_sharded=False))
            tags.append(("sgu_w_out", i))
            wst = ws.transpose(0, 2, 1)
            dz, dws, dbs, dlg, dlb = sgu_mid_bwd(zpre, dy, row(rep["sgu_ln_g"][i]), row(rep["sgu_ln_b"][i]),
                                                 ws, wst, bsb, tag + "_sgu_mid_bwd")
            big.append(dw_mm(hn, dz, tag + "_sgu_dwin", col_sharded=True))
            tags.append(("sgu_w_in", i))
            dh, dhb, dg = dx_norm(dz, *wl("sgu_w_in", i), xin, row(rep["sgu_norm"][i]), dh, tag + "_sgu_in_bwd")
            small["sgu_norm"][i] = dg.reshape(-1)
            small["sgu_ln_g"][i] = dlg.reshape(-1)
            small["sgu_ln_b"][i] = dlb.reshape(-1)
            small["sgu_w_s"][i] = dws
            small["sgu_b_s"][i] = dbs[:, :, 0]
    late = [k for k, t in enumerate(tags) if t in LAST_GROUP]
    late_pack = _pack([small[n][0] for n in LATE_SMALL])
    got = exchange("grads2_scatter", *add_to_all(reduce_plan([big[k] for k in late]), late_pack), [])
    grads = reduce_finish([big[k] for k in late], got[:-1], [tags[k] for k in late], full_shapes, grads, "grads2")
    late_sum = sum_devices(late_pack, got[-1], "sum_small_late")
    shapes = [a.shape for a in stacked()]
    *small_g, loss = _unpack(early_sum, shapes + [()])
    small_g = dict(zip(SMALL, small_g))
    for n, g in zip(LATE_SMALL, _unpack(late_sum, [small[n][0].shape for n in LATE_SMALL])):
        small_g[n] = small_g[n].at[0].add(g)
    return loss, dh, grads, small_g


BIG = ("att_w_in", "att_w_out", "sgu_w_in", "sgu_w_out", "mlp_w1", "mlp_w2")
SHARDED_VEC = ("sgu_norm", "sgu_ln_g", "sgu_ln_b")
REPLICATED = ("att_norm", "att_sink", "att_qnorm", "att_knorm", "sgu_w_s", "sgu_b_s", "mlp_norm", "final_norm")
WEIGHTS = ("att_norm", "att_w_in", "att_sink", "att_qnorm", "att_knorm", "att_w_out", "sgu_norm", "sgu_w_in",
           "sgu_ln_g", "sgu_ln_b", "sgu_w_s", "sgu_b_s", "sgu_w_out", "mlp_norm", "mlp_w1", "mlp_w2", "final_norm")
SMALL = tuple(n for n in WEIGHTS if n not in BIG)
PACK_ALIGN = 8 * LANES


def _pack(arrays):
    flat = jnp.concatenate([a.reshape(-1) for a in arrays])
    pad = -flat.shape[0] % PACK_ALIGN
    return jnp.pad(flat, (0, pad)).reshape(-1, LANES)


def _unpack(flat2d, shapes):
    flat = flat2d.reshape(-1)
    out, off = [], 0
    for shape in shapes:
        size = int(np.prod(shape))
        out.append(flat[off:off + size].reshape(shape))
        off += size
    return out


def kernel(x, att_norm, att_w_in, att_sink, att_qnorm, att_knorm, att_w_out, sgu_norm, sgu_w_in, sgu_ln_g, sgu_ln_b, sgu_w_s, sgu_b_s, sgu_w_out, mlp_norm, mlp_w1, mlp_w2, final_norm, loss_target, m_att_norm, m_att_w_in, m_att_sink, m_att_qnorm, m_att_knorm, m_att_w_out, m_sgu_norm, m_sgu_w_in, m_sgu_ln_g, m_sgu_ln_b, m_sgu_w_s, m_sgu_b_s, m_sgu_w_out, m_mlp_norm, m_mlp_w1, m_mlp_w2, m_final_norm, v_att_norm, v_att_w_in, v_att_sink, v_att_qnorm, v_att_knorm, v_att_w_out, v_sgu_norm, v_sgu_w_in, v_sgu_ln_g, v_sgu_ln_b, v_sgu_w_s, v_sgu_b_s, v_sgu_w_out, v_mlp_norm, v_mlp_w1, v_mlp_w2, v_final_norm):
    w = dict(att_norm=att_norm, att_w_in=att_w_in, att_sink=att_sink, att_qnorm=att_qnorm, att_knorm=att_knorm,
             att_w_out=att_w_out, sgu_norm=sgu_norm, sgu_w_in=sgu_w_in, sgu_ln_g=sgu_ln_g, sgu_ln_b=sgu_ln_b,
             sgu_w_s=sgu_w_s, sgu_b_s=sgu_b_s, sgu_w_out=sgu_w_out, mlp_norm=mlp_norm, mlp_w1=mlp_w1,
             mlp_w2=mlp_w2, final_norm=final_norm)
    m = dict(att_norm=m_att_norm, att_w_in=m_att_w_in, att_sink=m_att_sink, att_qnorm=m_att_qnorm,
             att_knorm=m_att_knorm, att_w_out=m_att_w_out, sgu_norm=m_sgu_norm, sgu_w_in=m_sgu_w_in,
             sgu_ln_g=m_sgu_ln_g, sgu_ln_b=m_sgu_ln_b, sgu_w_s=m_sgu_w_s, sgu_b_s=m_sgu_b_s,
             sgu_w_out=m_sgu_w_out, mlp_norm=m_mlp_norm, mlp_w1=m_mlp_w1, mlp_w2=m_mlp_w2,
             final_norm=m_final_norm)
    v = dict(att_norm=v_att_norm, att_w_in=v_att_w_in, att_sink=v_att_sink, att_qnorm=v_att_qnorm,
             att_knorm=v_att_knorm, att_w_out=v_att_w_out, sgu_norm=v_sgu_norm, sgu_w_in=v_sgu_w_in,
             sgu_ln_g=v_sgu_ln_g, sgu_ln_b=v_sgu_ln_b, sgu_w_s=v_sgu_w_s, sgu_b_s=v_sgu_b_s,
             sgu_w_out=v_sgu_w_out, mlp_norm=v_mlp_norm, mlp_w1=v_mlp_w1, mlp_w2=v_mlp_w2,
             final_norm=v_final_norm)
    chip = 2 * lax.axis_index("x") + lax.axis_index("y")

    vecs = jnp.stack([w[n] for n in SHARDED_VEC])
    wb = {n: w[n].astype(BF16) for n in BIG}
    first, vec_all = gather_whole([wb["att_w_in"][0:1], vecs], "gather_first")
    rest_shards = [wb[n][1:2] if n == "att_w_in" else wb[n] for n in REST]
    vec_full = vec_all.transpose(1, 2, 0, 3).reshape(vecs.shape[0], vecs.shape[1], -1)
    rep = {n: w[n] for n in REPLICATED}
    rep.update({n: vec_full[k] for k, n in enumerate(SHARDED_VEC)})

    loss, grad_x, grads, small_g = local_step(x[0], loss_target[0], first, rest_shards, rep,
                                              {n: w[n].shape for n in BIG})
    width = w["sgu_norm"].shape[1]
    for n in SHARDED_VEC:
        small_g[n] = lax.dynamic_slice_in_dim(small_g[n], chip * width, width, axis=1)
    grads.update(small_g)
    for n in BIG:
        grads[n] = grads[n].reshape(w[n].shape)

    delta, new_m, new_v = {}, {}, {}
    for n in WEIGHTS:
        shape = w[n].shape
        two_d = (lambda a: a.reshape(1, -1)) if len(shape) == 1 else (lambda a: a)
        dn, mn, vn = adamw(two_d(w[n]), two_d(grads[n]), two_d(m[n]), two_d(v[n]), "adamw_" + n)
        delta[n], new_m[n], new_v[n] = dn.reshape(shape), mn.reshape(shape), vn.reshape(shape)
    return (loss, grad_x[None], *[grads[n] for n in WEIGHTS], *[delta[n] for n in WEIGHTS],
            *[new_m[n] for n in WEIGHTS], *[new_v[n] for n in WEIGHTS])
```

```python
import functools

import numpy as np
import jax
import jax.numpy as jnp
from jax import lax
from jax.experimental import pallas as pl
from jax.experimental.pallas import tpu as pltpu

F32 = jnp.float32
BF16 = jnp.bfloat16
MESH = pl.DeviceIdType.MESH

EPS = 1e-6
HEAD_DIM = 64
BLOCK = 128
GRID_W = 64
ROPE_THETA = 10000.0
N_CHIPS = 4
LANES = 128
V7X_VMEM_BYTES = 64 * 1024 * 1024
VMEM_LIMIT = V7X_VMEM_BYTES - 8 * 1024 * 1024

ADAM_LR = 0.001
ADAM_B1 = 0.9
ADAM_B2 = 0.999
ADAM_EPS = 1e-08
ADAM_WD = 0.01
ADAM_STEP = 10

NT_DIMS = (((1,), (1,)), ((), ()))
TN_DIMS = (((0,), (0,)), ((), ()))


def _params(*sem):
    return pltpu.CompilerParams(dimension_semantics=sem, vmem_limit_bytes=VMEM_LIMIT)


def _sds(shape, dtype):
    return jax.ShapeDtypeStruct(tuple(shape), dtype)


def _row_tile(rows, want):
    t = min(rows, want)
    assert rows % t == 0, (rows, want)
    return t


def norm_mm(x, g, w4, layer, out_dtype, name):
    s_len, d = x.shape
    ns = w4.shape[-1]
    tm = _row_tile(s_len, 512)

    def body(x_ref, g_ref, w_ref, h_ref, y_ref):
        xf = x_ref[...]
        r = lax.rsqrt(jnp.mean(xf * xf, axis=-1, keepdims=True) + EPS)
        h = ((xf * r) * g_ref[...]).astype(BF16)
        h_ref[...] = h
        for s in range(N_CHIPS):
            y_ref[:, s * ns:(s + 1) * ns] = jnp.dot(h, w_ref[s], preferred_element_type=F32).astype(y_ref.dtype)

    return pl.pallas_call(
        body, name=name, grid=(s_len // tm,),
        in_specs=[pl.BlockSpec((tm, d), lambda i: (i, 0)),
                  pl.BlockSpec((1, d), lambda i: (0, 0)),
                  pl.BlockSpec((N_CHIPS, None, d, ns), lambda i: (0, layer, 0, 0))],
        out_specs=[pl.BlockSpec((tm, d), lambda i: (i, 0)),
                   pl.BlockSpec((tm, N_CHIPS * ns), lambda i: (i, 0))],
        out_shape=[_sds((s_len, d), BF16), _sds((s_len, N_CHIPS * ns), out_dtype)],
        compiler_params=_params("arbitrary"),
    )(x, g, w4)


def mm_res(a, w4, layer, res, name, relu2=False):
    s_len, k = a.shape
    kq, n = w4.shape[-2:]
    assert kq * N_CHIPS == k
    tm = _row_tile(s_len, 256 if k > 1024 else 512)

    def body(a_ref, w0, w1, w2, w3, r_ref, o_ref):
        acc = r_ref[...]
        for s, w_ref in enumerate((w0, w1, w2, w3)):
            av = a_ref[:, s * kq:(s + 1) * kq]
            if relu2:
                t = jnp.maximum(av.astype(F32), 0.0)
                av = (t * t).astype(BF16)
            acc = acc + jnp.dot(av, w_ref[...], preferred_element_type=F32)
        o_ref[...] = acc

    def wspec(s):
        return pl.BlockSpec((None, None, kq, n), lambda i: (s, layer, 0, 0))

    return pl.pallas_call(
        body, name=name, grid=(s_len // tm,),
        in_specs=[pl.BlockSpec((tm, k), lambda i: (i, 0)), wspec(0), wspec(1), wspec(2), wspec(3),
                  pl.BlockSpec((tm, n), lambda i: (i, 0))],
        out_specs=pl.BlockSpec((tm, n), lambda i: (i, 0)),
        out_shape=_sds((s_len, n), F32),
        compiler_params=_params("arbitrary"),
    )(a, w4, w4, w4, w4, res)


def mm_res_t(pieces, w4, layer, res, name):
    s_len = res.shape[0]
    kq, n = w4.shape[-2:]
    rows = pieces[0].shape[0]
    assert rows % kq == 0 and rows * len(pieces) == kq * N_CHIPS
    tm = _row_tile(s_len, 512)
    n_p = len(pieces)

    def body(*refs):
        p_refs, w_refs, (r_ref, o_ref) = refs[:n_p], refs[n_p:n_p + N_CHIPS], refs[n_p + N_CHIPS:]
        acc = r_ref[...]
        for s in range(N_CHIPS):
            p, off = divmod(s * kq, rows)
            acc = acc + lax.dot_general(p_refs[p][off:off + kq, :], w_refs[s][...], TN_DIMS,
                                        preferred_element_type=F32)
        o_ref[...] = acc

    def wspec(s):
        return pl.BlockSpec((None, None, kq, n), lambda i: (s, layer, 0, 0))

    return pl.pallas_call(
        body, name=name, grid=(s_len // tm,),
        in_specs=[pl.BlockSpec((rows, tm), lambda i: (0, i))] * n_p + [wspec(s) for s in range(N_CHIPS)]
        + [pl.BlockSpec((tm, n), lambda i: (i, 0))],
        out_specs=pl.BlockSpec((tm, n), lambda i: (i, 0)),
        out_shape=_sds((s_len, n), F32),
        compiler_params=_params("arbitrary"),
    )(*pieces, w4, w4, w4, w4, res)


def dw_nn(pieces, b, name):
    s_len, n = b.shape
    rows = pieces[0].shape[0]
    n_p = len(pieces)
    k = rows * n_p
    ts = _row_tile(s_len, 2048)
    n_s = s_len // ts

    def body(*refs):
        p_refs, (b_ref, o_ref, acc_ref) = refs[:n_p], refs[n_p:]
        s = pl.program_id(0)
        bv = b_ref[...]
        for p in range(n_p):
            part = jnp.dot(p_refs[p][...], bv, preferred_element_type=F32)
            at = slice(p * rows, (p + 1) * rows)
            if n_s == 1:
                o_ref[at, :] = part.astype(BF16)
                continue

            @pl.when(s == 0)
            def _():
                acc_ref[at, :] = part

            @pl.when((s > 0) & (s < n_s - 1))
            def _():
                acc_ref[at, :] += part

            @pl.when(s == n_s - 1)
            def _():
                o_ref[at, :] = (acc_ref[at, :] + part).astype(BF16)

    out = pl.pallas_call(
        body, name=name, grid=(n_s,),
        in_specs=[pl.BlockSpec((rows, ts), lambda s: (0, s))] * n_p + [pl.BlockSpec((ts, n), lambda s: (s, 0))],
        out_specs=pl.BlockSpec((k, n), lambda s: (0, 0)), out_shape=_sds((k, n), BF16),
        scratch_shapes=[pltpu.VMEM((k, n), F32)],
        compiler_params=_params("arbitrary"),
    )(*pieces, b)
    return out.reshape(N_CHIPS, k // N_CHIPS, n)


def mm_nt(dy, w4, layer, name, transposed=False):
    s_len, n = dy.shape
    mq = w4.shape[-2]
    tm = _row_tile(s_len, 512)

    def body(d_ref, w0, w1, w2, w3, o_ref):
        dv = d_ref[...]
        for s, w_ref in enumerate((w0, w1, w2, w3)):
            if transposed:
                o_ref[s * mq:(s + 1) * mq, :] = lax.dot_general(
                    w_ref[...], dv, NT_DIMS, preferred_element_type=F32).astype(BF16)
            else:
                o_ref[:, s * mq:(s + 1) * mq] = lax.dot_general(
                    dv, w_ref[...], NT_DIMS, preferred_element_type=F32).astype(BF16)

    def wspec(s):
        return pl.BlockSpec((None, None, mq, n), lambda i: (s, layer, 0, 0))

    m = N_CHIPS * mq
    return pl.pallas_call(
        body, name=name, grid=(s_len // tm,),
        in_specs=[pl.BlockSpec((tm, n), lambda i: (i, 0)), wspec(0), wspec(1), wspec(2), wspec(3)],
        out_specs=pl.BlockSpec((m, tm), lambda i: (0, i)) if transposed else pl.BlockSpec((tm, m), lambda i: (i, 0)),
        out_shape=_sds((m, s_len) if transposed else (s_len, m), BF16),
        compiler_params=_params("arbitrary"),
    )(dy, w4, w4, w4, w4)


def mm_nt_relu2_bwd(dy, w4, layer, a, name):
    s_len, n = dy.shape
    mq = w4.shape[-2]
    tm = _row_tile(s_len, 512)

    def body(d_ref, w_ref, a_ref, o_ref):
        dv = d_ref[...]
        for s in range(N_CHIPS):
            cols = slice(s * mq, (s + 1) * mq)
            dz = lax.dot_general(dv, w_ref[s], NT_DIMS, preferred_element_type=F32)
            o_ref[:, cols] = (dz * (2.0 * jnp.maximum(a_ref[:, cols].astype(F32), 0.0))).astype(BF16)

    return pl.pallas_call(
        body, name=name, grid=(s_len // tm,),
        in_specs=[pl.BlockSpec((tm, n), lambda i: (i, 0)),
                  pl.BlockSpec((N_CHIPS, None, mq, n), lambda i: (0, layer, 0, 0)),
                  pl.BlockSpec((tm, N_CHIPS * mq), lambda i: (i, 0))],
        out_specs=pl.BlockSpec((tm, N_CHIPS * mq), lambda i: (i, 0)),
        out_shape=_sds((s_len, N_CHIPS * mq), BF16),
        compiler_params=_params("arbitrary"),
    )(dy, w4, a)


def dx_norm(dy, w4, layer, x, g, dres, name):
    s_len, d = x.shape
    ns = w4.shape[-1]
    tm = _row_tile(s_len, 512)

    def body(dy_ref, w_ref, x_ref, g_ref, dr_ref, dx_ref, dxb_ref, dg_ref):
        i = pl.program_id(0)
        dh = lax.dot_general(dy_ref[:, 0:ns], w_ref[0], NT_DIMS, preferred_element_type=F32)
        for s in range(1, N_CHIPS):
            dh = dh + lax.dot_general(dy_ref[:, s * ns:(s + 1) * ns], w_ref[s], NT_DIMS,
                                      preferred_element_type=F32)
        xf = x_ref[...]
        r = lax.rsqrt(jnp.mean(xf * xf, axis=-1, keepdims=True) + EPS)
        xhat = xf * r
        dg_part = jnp.sum(dh * xhat, axis=0, keepdims=True)

        @pl.when(i == 0)
        def _():
            dg_ref[...] = dg_part

        @pl.when(i > 0)
        def _():
            dg_ref[...] += dg_part

        dxh = dh * g_ref[...]
        dx = dr_ref[...] + r * (dxh - xhat * jnp.mean(dxh * xhat, axis=-1, keepdims=True))
        dx_ref[...] = dx
        dxb_ref[...] = dx.astype(BF16)

    row = pl.BlockSpec((tm, d), lambda i: (i, 0))
    vec = pl.BlockSpec((1, d), lambda i: (0, 0))
    return pl.pallas_call(
        body, name=name, grid=(s_len // tm,),
        in_specs=[pl.BlockSpec((tm, N_CHIPS * ns), lambda i: (i, 0)),
                  pl.BlockSpec((N_CHIPS, None, d, ns), lambda i: (0, layer, 0, 0)), row, vec, row],
        out_specs=[row, row, vec],
        out_shape=[_sds((s_len, d), F32), _sds((s_len, d), BF16), _sds((1, d), F32)],
        compiler_params=_params("arbitrary"),
    )(dy, w4, x, g, dres)


def dw_mm(a, b, name, col_sharded, relu2=False):
    s_len, k = a.shape
    n = b.shape[1]
    ts = _row_tile(s_len, 2048)
    tk = min(k, 1024)
    tn = n // N_CHIPS if col_sharded else min(n, 1024)
    n_s = s_len // ts

    def body(a_ref, b_ref, o_ref, acc_ref):
        s = pl.program_id(2)
        av = a_ref[...]
        if relu2:
            t = jnp.maximum(av.astype(F32), 0.0)
            av = (t * t).astype(BF16)
        part = lax.dot_general(av, b_ref[...], TN_DIMS, preferred_element_type=F32)
        if n_s == 1:
            o_ref[...] = part.astype(BF16)
            return

        @pl.when(s == 0)
        def _():
            acc_ref[...] = part

        @pl.when((s > 0) & (s < n_s - 1))
        def _():
            acc_ref[...] += part

        @pl.when(s == n_s - 1)
        def _():
            o_ref[...] = (acc_ref[...] + part).astype(BF16)

    if col_sharded:
        out_shape = _sds((N_CHIPS, k, tn), BF16)
        out_spec = pl.BlockSpec((None, tk, tn), lambda i, j, s: (j, i, 0))
    else:
        out_shape = _sds((N_CHIPS, k // N_CHIPS, n), BF16)
        rows_per = k // N_CHIPS
        assert tk % rows_per == 0 or rows_per % tk == 0
        if tk >= rows_per:
            out_shape = _sds((k, n), BF16)
            out_spec = pl.BlockSpec((tk, tn), lambda i, j, s: (i, j))
        else:
            per = rows_per // tk
            out_spec = pl.BlockSpec((None, tk, tn), lambda i, j, s: (i // per, i % per, j))

    out = pl.pallas_call(
        body, name=name, grid=(k // tk, n // tn, n_s),
        in_specs=[pl.BlockSpec((ts, tk), lambda i, j, s: (s, i)),
                  pl.BlockSpec((ts, tn), lambda i, j, s: (s, j))],
        out_specs=out_spec, out_shape=out_shape,
        scratch_shapes=[pltpu.VMEM((tk, tn), F32)],
        compiler_params=_params("arbitrary", "arbitrary", "arbitrary"),
    )(a, b)
    if not col_sharded:
        out = out.reshape(N_CHIPS, k // N_CHIPS, n)
    return out


def ew(fn, ins, out_dtypes, name, tile_rows=256):
    rows, cols = ins[0].shape
    for a in ins:
        assert a.shape == (rows, cols), (name, a.shape, rows, cols)
    tr = rows if (rows <= tile_rows or rows % tile_rows) else tile_rows
    n_in = len(ins)

    def body(*refs):
        outs = fn(*[r[...] for r in refs[:n_in]])
        for o_ref, val in zip(refs[n_in:], outs):
            o_ref[...] = val.astype(o_ref.dtype)

    spec = pl.BlockSpec((tr, cols), lambda i: (i, 0))
    return pl.pallas_call(
        body, name=name, grid=(rows // tr,),
        in_specs=[spec] * n_in, out_specs=[spec] * len(out_dtypes),
        out_shape=[_sds((rows, cols), dt) for dt in out_dtypes],
        compiler_params=_params("arbitrary"),
    )(*ins)


def adamw(w, g, m, v, name):
    shape = w.shape
    cols = shape[-1]
    two_d = lambda a: a.reshape(-1, cols)

    def fn(wv, gv, mv, vv):
        m_new = ADAM_B1 * mv + (1.0 - ADAM_B1) * gv
        v_new = ADAM_B2 * vv + (1.0 - ADAM_B2) * (gv * gv)
        m_hat = m_new / (1.0 - ADAM_B1 ** ADAM_STEP)
        v_hat = v_new / (1.0 - ADAM_B2 ** ADAM_STEP)
        delta = -ADAM_LR * (m_hat / (jnp.sqrt(v_hat) + ADAM_EPS) + ADAM_WD * wv)
        return delta, m_new, v_new

    d, mn, vn = ew(fn, [two_d(w), two_d(g), two_d(m), two_d(v)], [F32, F32, F32], name)
    return d.reshape(shape), mn.reshape(shape), vn.reshape(shape)


def rope_tables(s_len):
    def angles(pos, dim):
        freqs = ROPE_THETA ** (-jnp.arange(0, dim, 2, dtype=F32) / dim)
        ang = pos.astype(F32)[:, None] * freqs[None, :]
        return jnp.cos(ang), jnp.sin(ang)

    pos = jnp.arange(s_len)
    rows = s_len // GRID_W
    row_idx = jnp.repeat(jnp.arange(rows), GRID_W)
    col_idx = jnp.tile(jnp.arange(GRID_W), rows)
    c1, s1 = angles(pos, HEAD_DIM)
    cr, sr = angles(row_idx, HEAD_DIM // 2)
    cc, sc = angles(col_idx, HEAD_DIM // 2)
    cos1 = jnp.tile(jnp.concatenate([c1, c1], -1), (1, 2))
    sin1 = jnp.tile(jnp.concatenate([-s1, s1], -1), (1, 2))
    cos2 = jnp.tile(jnp.concatenate([cr, cr, cc, cc], -1), (1, 2))
    sin2 = jnp.tile(jnp.concatenate([-sr, sr, -sc, sc], -1), (1, 2))
    return cos1, sin1, cos2, sin2


def _lane_iota(rows):
    return lax.broadcasted_iota(jnp.int32, (rows, LANES), 1)


def _swap(x, dist, lane):
    return jnp.where((lane & dist) != 0, pltpu.roll(x, dist, 1), pltpu.roll(x, LANES - dist, 1))


def _head_ones():
    r = lax.broadcasted_iota(jnp.int32, (LANES, LANES), 0) // HEAD_DIM
    c = lax.broadcasted_iota(jnp.int32, (LANES, LANES), 1) // HEAD_DIM
    return (r == c).astype(BF16)


def _head_sum(t, ones):
    hi = t.astype(BF16)
    lo = (t - hi.astype(F32)).astype(BF16)
    return (jnp.dot(hi, ones, preferred_element_type=F32) + jnp.dot(lo, ones, preferred_element_type=F32))


Q_SCALE = HEAD_DIM ** -0.5
LOG2E = 1.4426950408889634
LN2 = 0.6931471805599453
CHUNK_KIND = ["qa"] * 4 + ["ka", "va"] + ["qb"] * 4 + ["kb", "vb"]
QA_COL, KA_COL, QB_COL, KB_COL = 0, 512, 768, 1280


def prep_fwd(proj, tabs, qn_g, kn_g, name):
    s_len, width = proj.shape
    ts = _row_tile(s_len, 512)
    cos1, sin1, cos2, sin2 = tabs

    def body(p_ref, c1_ref, s1_ref, c2_ref, s2_ref, qg_ref, kg_ref, o_ref, kv_ref):
        lane = _lane_iota(ts)
        ones = _head_ones()
        c1, s1, c2, s2 = c1_ref[...], s1_ref[...], c2_ref[...], s2_ref[...]
        n_kv = 0
        for cb, kind in enumerate(CHUNK_KIND):
            x = p_ref[:, cb * LANES:(cb + 1) * LANES]
            if kind in ("qa", "ka"):
                y = x * c1 + _swap(x, 32, lane) * s1
            elif kind in ("qb", "kb"):
                gain = qg_ref[...] if kind == "qb" else kg_ref[...]
                ms = _head_sum(x * x, ones) * (1.0 / HEAD_DIM)
                xn = (x * lax.rsqrt(ms + EPS)) * gain
                y = xn * c2 + _swap(xn, 16, lane) * s2
            else:
                y = x
            if kind in ("qa", "qb"):
                y = y * (Q_SCALE * LOG2E)
            else:
                kv_ref[:, n_kv * LANES:(n_kv + 1) * LANES] = y.astype(BF16)
                n_kv += 1
            o_ref[cb * LANES:(cb + 1) * LANES, :] = y.T.astype(BF16)

    tab = pl.BlockSpec((ts, LANES), lambda i: (i, 0))
    vec = pl.BlockSpec((1, LANES), lambda i: (0, 0))
    return pl.pallas_call(
        body, name=name, grid=(s_len // ts,),
        in_specs=[pl.BlockSpec((ts, width), lambda i: (i, 0)), tab, tab, tab, tab, vec, vec],
        out_specs=[pl.BlockSpec((width, ts), lambda i: (0, i)), pl.BlockSpec((ts, 4 * LANES), lambda i: (i, 0))],
        out_shape=[_sds((width, s_len), BF16), _sds((s_len, 4 * LANES), BF16)],
        compiler_params=_params("arbitrary"),
    )(proj, cos1, sin1, cos2, sin2, qn_g, kn_g)


def prep_bwd(proj, dqa, dka, dva, dqb, dkb, dvb, tabs, qn_g, kn_g, name):
    s_len, width = proj.shape
    ts = _row_tile(s_len, 256)
    cos1, sin1, cos2, sin2 = tabs

    def body(p_ref, dqa_ref, dka_ref, dva_ref, dqb_ref, dkb_ref, dvb_ref,
             c1_ref, s1_ref, c2_ref, s2_ref, qg_ref, kg_ref, o_ref, dqg_ref, dkg_ref):
        i = pl.program_id(0)
        lane = _lane_iota(ts)
        c1, s1, c2, s2 = c1_ref[...], s1_ref[...], c2_ref[...], s2_ref[...]

        def rope_t(dy, cos, sin, dist):
            return dy * cos + _swap(dy * sin, dist, lane)

        ones = _head_ones()

        def norm_bwd(dy, x, gain):
            r = lax.rsqrt(_head_sum(x * x, ones) * (1.0 / HEAD_DIM) + EPS)
            xhat = x * r
            dgain = jnp.sum(dy * xhat, axis=0, keepdims=True)
            dxh = dy * gain
            dx = r * (dxh - xhat * (_head_sum(dxh * xhat, ones) * (1.0 / HEAD_DIM)))
            return dx, dgain

        dqg = jnp.zeros((1, LANES), F32)
        dkg = jnp.zeros((1, LANES), F32)
        for cb, kind in enumerate(CHUNK_KIND):
            cols = slice(cb * LANES, (cb + 1) * LANES)
            if kind == "qa":
                dx = rope_t(dqa_ref[cols, :].T * Q_SCALE, c1, s1, 32)
            elif kind == "ka":
                dx = rope_t(dka_ref[...], c1, s1, 32)
            elif kind == "va":
                dx = dva_ref[...]
            elif kind == "qb":
                qcols = slice((cb - 6) * LANES, (cb - 5) * LANES)
                dy = rope_t(dqb_ref[qcols, :].T * Q_SCALE, c2, s2, 16)
                dx, dgain = norm_bwd(dy, p_ref[:, cols], qg_ref[...])
                dqg = dqg + dgain
            elif kind == "kb":
                dy = rope_t(dkb_ref[...], c2, s2, 16)
                dx, dgain = norm_bwd(dy, p_ref[:, cols], kg_ref[...])
                dkg = dkg + dgain
            else:
                dx = dvb_ref[...]
            o_ref[:, cols] = dx.astype(BF16)

        @pl.when(i == 0)
        def _():
            dqg_ref[...] = dqg
            dkg_ref[...] = dkg

        @pl.when(i > 0)
        def _():
            dqg_ref[...] += dqg
            dkg_ref[...] += dkg

    tab = pl.BlockSpec((ts, LANES), lambda i: (i, 0))
    vec = pl.BlockSpec((1, LANES), lambda i: (0, 0))

    def dq_spec(dq):
        per = dq.shape[2] // ts
        return pl.BlockSpec((None, 4 * LANES, ts), lambda i: (i // per, 0, i % per))

    return pl.pallas_call(
        body, name=name, grid=(s_len // ts,),
        in_specs=([pl.BlockSpec((ts, width), lambda i: (i, 0)), dq_spec(dqa), tab, tab, dq_spec(dqb), tab, tab]
                  + [tab] * 4 + [vec, vec]),
        out_specs=[pl.BlockSpec((ts, width), lambda i: (i, 0)), vec, vec],
        out_shape=[_sds((s_len, width), BF16), _sds((1, LANES), F32), _sds((1, LANES), F32)],
        compiler_params=_params("arbitrary"),
    )(proj, dqa, dka, dva, dqb, dkb, dvb, cos1, sin1, cos2, sin2, qn_g, kn_g)


NEG = -1e30
GROUP = 4
KV_HEADS = 2
GROUP_W = GROUP * HEAD_DIM
LSE_ROWS = 8
ONES_ROWS = 16


def _pos_mask_t(k_start, q_start, s_len, tk, tq):
    kpos = k_start + lax.broadcasted_iota(jnp.int32, (tk, tq), 0)
    qpos = q_start + lax.broadcasted_iota(jnp.int32, (tk, tq), 1)
    return (jnp.abs(kpos - qpos) <= BLOCK) & (kpos >= 0) & (kpos < s_len)


def flash_fwd_t(qkv_t, kv_tok, q_rb, k_i, v_rb, sink, window, name, comm=None):
    s_len = qkv_t.shape[1]
    if window:
        tq = _row_tile(s_len, 512)
        tk = 2 * BLOCK
        assert tq == 2 * tk, "the band parts below are written for query blocks of two key blocks"
        n_kv = 2
    else:
        tq, tk = _row_tile(s_len, 1024), _row_tile(s_len, 4096)
        n_kv = s_len // tk
    n_i = s_len // tq
    c_ins, c_outs, c_remote = comm if comm else ([], [], [])
    n_main = 6 if window else 3

    def body(*refs):
        main, c_in_refs = refs[:n_main], refs[n_main:n_main + len(c_ins)]
        rest = refs[n_main + len(c_ins):]
        (o_ref, lse_ref), c_out_refs = rest[:2], rest[2:2 + len(c_outs)]
        m_sc, acc_sc = rest[2 + len(c_outs):4 + len(c_outs)]
        c_sems = rest[4 + len(c_outs):]
        if window:
            sink_ref, q_ref, k_ref, v_ref, kc_ref, vc_ref = main
        else:
            q_ref, k_ref, v_ref = main
        h, i, t = pl.program_id(0), pl.program_id(1), pl.program_id(2)
        if comm:
            @pl.when((h == 0) & (i == 0) & (t == 0))
            def _():
                _exchange_start(c_remote, c_in_refs, c_out_refs, *c_sems)

        @pl.when(t == 0)
        def _():
            for g in range(GROUP):
                acc_sc[g, 0:HEAD_DIM, :] = jnp.zeros((HEAD_DIM, tq), F32)
                if window:
                    m_sc[g] = jnp.full((1, tq), sink_ref[h * GROUP + g] * LOG2E, F32)
                    acc_sc[g, HEAD_DIM:, :] = jnp.ones((ONES_ROWS, tq), F32)
                else:
                    m_sc[g] = jnp.full((1, tq), NEG, F32)
                    acc_sc[g, HEAD_DIM:, :] = jnp.zeros((ONES_ROWS, tq), F32)

        def tile(k_src, v_src, n_keys, key_pos, q_lo, q_hi):
            qs = slice(q_lo, q_hi)
            k = k_src[...]
            v_t = jnp.concatenate([v_src[...], jnp.ones((ONES_ROWS, n_keys), BF16)], axis=0)
            if window:
                mask = _pos_mask_t(key_pos, i * tq + q_lo, s_len, n_keys, q_hi - q_lo)
            s_next = jnp.dot(k, q_ref[0:HEAD_DIM, qs], preferred_element_type=F32)
            for g in range(GROUP):
                s_t = s_next
                if g + 1 < GROUP:
                    s_next = jnp.dot(k, q_ref[(g + 1) * HEAD_DIM:(g + 2) * HEAD_DIM, qs],
                                     preferred_element_type=F32)
                if window:
                    s_t = jnp.where(mask, s_t, NEG)
                m_prev = m_sc[g, :, qs]
                m_new = jnp.maximum(m_prev, jnp.max(s_t, axis=0, keepdims=True))
                alpha = jnp.exp2(m_prev - m_new)
                p_t = jnp.exp2(s_t - m_new)
                acc_sc[g, :, qs] = alpha * acc_sc[g, :, qs] + jnp.dot(v_t, p_t.astype(BF16),
                                                                     preferred_element_type=F32)
                m_sc[g, :, qs] = m_new

        if window:
            @pl.when(t == 0)
            def _():
                tile(k_ref, v_ref, tk, i * tq, 0, tq - BLOCK)
                tile(kc_ref, vc_ref, BLOCK, i * tq - BLOCK, 0, BLOCK)

            @pl.when(t == 1)
            def _():
                tile(k_ref, v_ref, tk, i * tq + tk, BLOCK, tq)
                tile(kc_ref, vc_ref, BLOCK, i * tq + tq, tq - BLOCK, tq)
        else:
            tile(k_ref, v_ref, tk, t * tk, 0, tq)

        @pl.when(t == n_kv - 1)
        def _():
            for g in range(GROUP):
                l = acc_sc[g, HEAD_DIM:HEAD_DIM + 1, :]
                o_ref[g * HEAD_DIM:(g + 1) * HEAD_DIM, :] = (acc_sc[g, 0:HEAD_DIM, :] / l).astype(BF16)
                lse_ref[g * LSE_ROWS:(g + 1) * LSE_ROWS, :] = jnp.broadcast_to(
                    m_sc[g] + jnp.log(l) * LOG2E, (LSE_ROWS, tq))

        if comm:
            @pl.when((h == KV_HEADS - 1) & (i == n_i - 1) & (t == n_kv - 1))
            def _():
                _exchange_finish(c_remote, c_in_refs, c_out_refs, *c_sems)

    kv_blk = (lambda i, t: 2 * i + t) if window else (lambda i, t: t)
    hbm = pl.BlockSpec(memory_space=pl.ANY)
    in_specs = [pl.BlockSpec((GROUP_W, tq), lambda h, i, t: (q_rb + h, i)),
                pl.BlockSpec((None, tk, HEAD_DIM), lambda h, i, t: (k_i + h, kv_blk(i, t), 0)),
                pl.BlockSpec((HEAD_DIM, tk), lambda h, i, t: (v_rb + h, kv_blk(i, t)))]
    args = [qkv_t, kv_tok, qkv_t]
    if window:
        corner = lambda i, t: jnp.clip((tq // BLOCK) * i - 1 + (tq // BLOCK + 1) * t, 0, s_len // BLOCK - 1)
        in_specs = ([pl.BlockSpec(memory_space=pltpu.SMEM)] + in_specs
                    + [pl.BlockSpec((None, BLOCK, HEAD_DIM), lambda h, i, t: (k_i + h, corner(i, t), 0)),
                       pl.BlockSpec((HEAD_DIM, BLOCK), lambda h, i, t: (v_rb + h, corner(i, t)))])
        args = [sink] + args + [kv_tok, qkv_t]
    return pl.pallas_call(
        body, name=name, grid=(KV_HEADS, n_i, n_kv),
        in_specs=in_specs + [hbm] * len(c_ins),
        out_specs=[pl.BlockSpec((GROUP_W, tq), lambda h, i, t: (h, i)),
                   pl.BlockSpec((GROUP * LSE_ROWS, tq), lambda h, i, t: (h, i))] + [hbm] * len(c_outs),
        out_shape=[_sds((KV_HEADS * GROUP_W, s_len), BF16),
                   _sds((KV_HEADS * GROUP * LSE_ROWS, s_len), F32)] + list(c_outs),
        scratch_shapes=[pltpu.VMEM((GROUP, 1, tq), F32),
                        pltpu.VMEM((GROUP, HEAD_DIM + ONES_ROWS, tq), F32)] + _exchange_sems(c_remote),
        compiler_params=_params("arbitrary", "arbitrary", "arbitrary"),
    )(*args, *c_ins)


def flash_bwd_t(qkv_t, kv_tok, o_t, do_t, lse, q_rb, k_i, v_i, k_rb, do_rb, sink, window, name, comm=None):
    s_len = qkv_t.shape[1]
    if window:
        tq = _row_tile(s_len, 512)
        tk = 2 * BLOCK
        assert tq == 2 * tk, "the band parts below are written for query blocks of two key blocks"
        n_q = 1
    else:
        tq, tk = _row_tile(s_len, 2048), _row_tile(s_len, 1024)
        n_q = s_len // tq
    n_qb = s_len // tq
    n_j = s_len // tk
    c_ins, c_outs, c_remote = comm if comm else ([], [], [])
    n_main = 12 if window else 7
    n_out = 4 if window else 3

    def body(*refs):
        main, c_in_refs = refs[:n_main], refs[n_main:n_main + len(c_ins)]
        rest = refs[n_main + len(c_ins):]
        outs, c_out_refs = rest[:n_out], rest[n_out:n_out + len(c_outs)]
        dk_sc, dv_sc = rest[n_out + len(c_outs):n_out + len(c_outs) + 2]
        c_sems = rest[n_out + len(c_outs) + 2:]
        if window:
            sink_ref, q_ref, k_ref, v_ref, kt_ref, o_ref, do_ref, lse_ref = main[:8]
            corner_src = main[8:]
            dq_ref, dk_ref, dv_ref, dsink_ref = outs
        else:
            q_ref, k_ref, v_ref, kt_ref, o_ref, do_ref, lse_ref = main
            dq_ref, dk_ref, dv_ref = outs
        main_src = (q_ref, o_ref, do_ref, lse_ref)
        h, j, t = pl.program_id(0), pl.program_id(1), pl.program_id(2)
        if comm:
            @pl.when((h == 0) & (j == 0) & (t == 0))
            def _():
                _exchange_start(c_remote, c_in_refs, c_out_refs, *c_sems)

        @pl.when((j == 0) & (t == 0))
        def _():
            dq_ref[...] = jnp.zeros(dq_ref.shape, F32)
            if window:
                dsink_ref[...] = jnp.zeros((8, LANES), F32)

        @pl.when(t == 0)
        def _():
            dk_sc[...] = jnp.zeros((tk, HEAD_DIM), F32)
            dv_sc[...] = jnp.zeros((tk, HEAD_DIM), F32)

        def tile(src, q_lo, q_hi, q_pos, k_lo, k_hi, dq_blk, dq_lo, sink_lo=0, sink_hi=0):
            q_src, o_src, do_src, lse_src = src
            ks, qs = slice(k_lo, k_hi), slice(q_lo, q_hi)
            dqs = slice(dq_lo, dq_lo + q_hi - q_lo)
            k, v, k_t = k_ref[ks, :], v_ref[ks, :], kt_ref[:, ks]
            if window:
                mask = _pos_mask_t(j * tk + k_lo, q_pos, s_len, k_hi - k_lo, q_hi - q_lo)
                lane = lax.broadcasted_iota(jnp.int32, (8, LANES), 1)
                sink_tile = jnp.zeros((8, LANES), F32)
            dk_acc = dk_sc[ks, :]
            dv_acc = dv_sc[ks, :]
            for g in range(GROUP):
                rows = slice(g * HEAD_DIM, (g + 1) * HEAD_DIM)
                q_t, o_g, do_g = q_src[rows, qs], o_src[rows, qs], do_src[rows, qs]
                s_t = jnp.dot(k, q_t, preferred_element_type=F32)
                if window:
                    s_t = jnp.where(mask, s_t, NEG)
                lse_row = lse_src[g * LSE_ROWS:g * LSE_ROWS + 1, qs]
                p_t = jnp.exp2(s_t - lse_row)
                delta = jnp.sum(do_g.astype(F32) * o_g.astype(F32), axis=0, keepdims=True)
                dp_t = jnp.dot(v, do_g, preferred_element_type=F32)
                ds_t = (p_t * (dp_t - delta)).astype(BF16)
                dv_acc = dv_acc + lax.dot_general(p_t.astype(BF16), do_g, NT_DIMS, preferred_element_type=F32)
                dk_acc = dk_acc + lax.dot_general(ds_t, q_t, NT_DIMS, preferred_element_type=F32)
                dq_ref[dq_blk, rows, dqs] += jnp.dot(k_t, ds_t, preferred_element_type=F32)
                if sink_hi > sink_lo:
                    at = slice(sink_lo - q_lo, sink_hi - q_lo)
                    p_sink = jnp.exp2(sink_ref[h * GROUP + g] * LOG2E - lse_row[:, at])
                    term = -jnp.sum(p_sink * delta[:, at], axis=1, keepdims=True)
                    sink_tile = jnp.where(lane == g, term, sink_tile)
            dk_sc[ks, :] = dk_acc
            dv_sc[ks, :] = dv_acc
            if sink_hi > sink_lo:
                dsink_ref[...] += sink_tile

        if window:
            m = j // 2

            @pl.when(j % 2 == 0)
            def _():
                tile(main_src, 0, tq - BLOCK, m * tq, 0, tk, m, 0, 0, tq - BLOCK)

            @pl.when((j % 2 == 0) & (j > 0))
            def _():
                tile(corner_src, 0, BLOCK, m * tq - BLOCK, 0, BLOCK, m - 1, tq - BLOCK)

            @pl.when(j % 2 == 1)
            def _():
                tile(main_src, BLOCK, tq, m * tq + BLOCK, 0, tk, m, BLOCK, tq - BLOCK, tq)

            @pl.when((j % 2 == 1) & (j < n_j - 1))
            def _():
                tile(corner_src, 0, BLOCK, (m + 1) * tq, tk - BLOCK, tk, m + 1, 0)
        else:
            tile(main_src, 0, tq, t * tq, 0, tk, t, 0)

        @pl.when(t == n_q - 1)
        def _():
            dk_ref[...] = dk_sc[...] * LN2
            dv_ref[...] = dv_sc[...]

        if comm:
            @pl.when((h == KV_HEADS - 1) & (j == n_j - 1) & (t == n_q - 1))
            def _():
                _exchange_finish(c_remote, c_in_refs, c_out_refs, *c_sems)

    qb = (lambda j, t: j // 2) if window else (lambda j, t: t)
    hbm = pl.BlockSpec(memory_space=pl.ANY)
    in_specs = [pl.BlockSpec((GROUP_W, tq), lambda h, j, t: (q_rb + h, qb(j, t))),
                pl.BlockSpec((None, tk, HEAD_DIM), lambda h, j, t: (k_i + h, j, 0)),
                pl.BlockSpec((None, tk, HEAD_DIM), lambda h, j, t: (v_i + h, j, 0)),
                pl.BlockSpec((HEAD_DIM, tk), lambda h, j, t: (k_rb + h, j)),
                pl.BlockSpec((GROUP_W, tq), lambda h, j, t: (h, qb(j, t))),
                pl.BlockSpec((GROUP_W, tq), lambda h, j, t: (do_rb + h, qb(j, t))),
                pl.BlockSpec((GROUP * LSE_ROWS, tq), lambda h, j, t: (h, qb(j, t)))]
    args = [qkv_t, kv_tok, kv_tok, qkv_t, o_t, do_t, lse]
    kv_out = _sds((KV_HEADS, s_len, HEAD_DIM), F32)
    out_specs = [pl.BlockSpec((n_qb, GROUP_W, tq), lambda h, j, t: (0, h, 0)),
                 pl.BlockSpec((None, tk, HEAD_DIM), lambda h, j, t: (h, j, 0)),
                 pl.BlockSpec((None, tk, HEAD_DIM), lambda h, j, t: (h, j, 0))]
    out_shape = [_sds((n_qb, KV_HEADS * GROUP_W, tq), F32), kv_out, kv_out]
    if window:
        cq = lambda j: jnp.clip(2 * j - 1 + 3 * (j % 2), 0, s_len // BLOCK - 1)
        in_specs = ([pl.BlockSpec(memory_space=pltpu.SMEM)] + in_specs
                    + [pl.BlockSpec((GROUP_W, BLOCK), lambda h, j, t: (q_rb + h, cq(j))),
                       pl.BlockSpec((GROUP_W, BLOCK), lambda h, j, t: (h, cq(j))),
                       pl.BlockSpec((GROUP_W, BLOCK), lambda h, j, t: (do_rb + h, cq(j))),
                       pl.BlockSpec((GROUP * LSE_ROWS, BLOCK), lambda h, j, t: (h, cq(j)))])
        args = [sink] + args + [qkv_t, o_t, do_t, lse]
        out_specs.append(pl.BlockSpec((None, 8, LANES), lambda h, j, t: (h, 0, 0)))
        out_shape.append(_sds((KV_HEADS, 8, LANES), F32))
    return pl.pallas_call(
        body, name=name, grid=(KV_HEADS, n_j, n_q),
        in_specs=in_specs + [hbm] * len(c_ins), out_specs=out_specs + [hbm] * len(c_outs),
        out_shape=out_shape + list(c_outs),
        scratch_shapes=[pltpu.VMEM((tk, HEAD_DIM), F32), pltpu.VMEM((tk, HEAD_DIM), F32)]
        + _exchange_sems(c_remote),
        compiler_params=_params("arbitrary", "arbitrary", "arbitrary"),
    )(*args, *c_ins)


SGU_GROUPS = 8
SGU_CHUNK = 128
GELU_C = float(np.sqrt(2.0 / np.pi))
GELU_A = 0.044715


def _gelu_and_grad(x):
    x2 = x * x
    t = jnp.tanh(x * (GELU_C + (GELU_C * GELU_A) * x2))
    hx = 0.5 * x
    return hx + hx * t, (0.5 + 0.5 * t) + (hx * (1.0 - t * t)) * (GELU_C + (3.0 * GELU_C * GELU_A) * x2)


def _gelu(x):
    t = jnp.tanh(x * (GELU_C + (GELU_C * GELU_A) * (x * x)))
    hx = 0.5 * x
    return hx + hx * t


def _layernorm_stats(v):
    mu = jnp.mean(v, axis=-1, keepdims=True)
    var = jnp.mean(jnp.square(v - mu), axis=-1, keepdims=True)
    rstd = lax.rsqrt(var + EPS)
    return (v - mu) * rstd, rstd


def sgu_mid_fwd(zpre, ln_g, ln_b, ws, bsb, name):
    s_len, width = zpre.shape
    d = width // 2
    ts = _row_tile(s_len, 256)

    def body(z_ref, g_ref, b_ref, ws_ref, bs_ref, y_ref):
        z = _gelu(z_ref[...])
        u, v = z[:, :d], z[:, d:]
        vhat, _ = _layernorm_stats(v)
        vn = (vhat * g_ref[...] + b_ref[...]).astype(BF16)
        for n in range(ts // SGU_CHUNK):
            rows = slice(n * SGU_CHUNK, (n + 1) * SGU_CHUNK)
            for g in range(SGU_GROUPS):
                cols = slice(g * LANES, (g + 1) * LANES)
                mixed = jnp.dot(ws_ref[g], vn[rows, cols], preferred_element_type=F32) + bs_ref[g]
                y_ref[rows, cols] = (u[rows, cols] * mixed).astype(BF16)

    vec = pl.BlockSpec((1, d), lambda i: (0, 0))
    cube = pl.BlockSpec((SGU_GROUPS, SGU_CHUNK, SGU_CHUNK), lambda i: (0, 0, 0))
    return pl.pallas_call(
        body, name=name, grid=(s_len // ts,),
        in_specs=[pl.BlockSpec((ts, width), lambda i: (i, 0)), vec, vec, cube, cube],
        out_specs=pl.BlockSpec((ts, d), lambda i: (i, 0)),
        out_shape=_sds((s_len, d), BF16),
        compiler_params=_params("arbitrary"),
    )(zpre, ln_g, ln_b, ws, bsb)


def sgu_mid_bwd(zpre, dy, ln_g, ln_b, ws, wst, bsb, name):
    s_len, width = zpre.shape
    d = width // 2
    ts = _row_tile(s_len, 256)
    n_steps = s_len // ts

    def body(z_ref, dy_ref, g_ref, b_ref, ws_ref, wst_ref, bs_ref,
             dz_ref, dws_ref, dbs_ref, dg_ref, db_ref, du_sc, dvn_sc):
        i = pl.program_id(0)

        @pl.when(i == 0)
        def _():
            dws_ref[...] = jnp.zeros(dws_ref.shape, F32)
            dbs_ref[...] = jnp.zeros(dbs_ref.shape, F32)
            dg_ref[...] = jnp.zeros(dg_ref.shape, F32)
            db_ref[...] = jnp.zeros(db_ref.shape, F32)

        zp = z_ref[...]
        z, gp = _gelu_and_grad(zp)
        u, v = z[:, :d], z[:, d:]
        vhat, rstd = _layernorm_stats(v)
        gain = g_ref[...]
        vn = (vhat * gain + b_ref[...]).astype(BF16)
        dyf = dy_ref[...].astype(F32)
        for n in range(ts // SGU_CHUNK):
            rows = slice(n * SGU_CHUNK, (n + 1) * SGU_CHUNK)
            for g in range(SGU_GROUPS):
                cols = slice(g * LANES, (g + 1) * LANES)
                vt = vn[rows, cols]
                mixed = jnp.dot(ws_ref[g], vt, preferred_element_type=F32) + bs_ref[g]
                dyt = dyf[rows, cols]
                du_sc[rows, cols] = dyt * mixed
                dmixed = dyt * u[rows, cols]
                dmb = dmixed.astype(BF16)
                dvn_sc[rows, cols] = jnp.dot(wst_ref[g], dmb, preferred_element_type=F32)
                dws_ref[g] += lax.dot_general(dmb, vt, NT_DIMS, preferred_element_type=F32)
                dbs_ref[g] += dmixed
        dvn = dvn_sc[...]
        dg_ref[...] += jnp.sum(dvn * vhat, axis=0, keepdims=True)
        db_ref[...] += jnp.sum(dvn, axis=0, keepdims=True)
        dvh = dvn * gain
        dv = rstd * (dvh - jnp.mean(dvh, axis=-1, keepdims=True)
                     - vhat * jnp.mean(dvh * vhat, axis=-1, keepdims=True))
        dz_ref[:, :d] = (du_sc[...] * gp[:, :d]).astype(BF16)
        dz_ref[:, d:] = (dv * gp[:, d:]).astype(BF16)

        @pl.when(i == n_steps - 1)
        def _():
            for g in range(SGU_GROUPS):
                tot = jnp.sum(dbs_ref[g], axis=1, keepdims=True)
                dbs_ref[g] = jnp.broadcast_to(tot, (SGU_CHUNK, LANES))

    vec = pl.BlockSpec((1, d), lambda i: (0, 0))
    cube = pl.BlockSpec((SGU_GROUPS, SGU_CHUNK, SGU_CHUNK), lambda i: (0, 0, 0))
    cube_shape = _sds((SGU_GROUPS, SGU_CHUNK, SGU_CHUNK), F32)
    return pl.pallas_call(
        body, name=name, grid=(n_steps,),
        in_specs=[pl.BlockSpec((ts, width), lambda i: (i, 0)), pl.BlockSpec((ts, d), lambda i: (i, 0)),
                  vec, vec, cube, cube, cube],
        out_specs=[pl.BlockSpec((ts, width), lambda i: (i, 0)), cube, cube, vec, vec],
        out_shape=[_sds((s_len, width), BF16), cube_shape, cube_shape, _sds((1, d), F32), _sds((1, d), F32)],
        scratch_shapes=[pltpu.VMEM((ts, d), F32), pltpu.VMEM((ts, d), F32)],
        compiler_params=_params("arbitrary"),
    )(zpre, dy, ln_g, ln_b, ws, wst, bsb)


def loss_head(x, g, target, name):
    s_len, d = x.shape
    tm = _row_tile(s_len, 512)

    def body(x_ref, g_ref, t_ref, dx_ref, dxb_ref, dg_ref, loss_ref):
        i = pl.program_id(0)
        xf = x_ref[...]
        gain = g_ref[...]
        r = lax.rsqrt(jnp.mean(xf * xf, axis=-1, keepdims=True) + EPS)
        xhat = xf * r
        err = xhat * gain - t_ref[...]
        row = jnp.mean(err * err, axis=-1, keepdims=True)
        part = 0.5 * jnp.sum(row, axis=0, keepdims=True)
        dy = err * (1.0 / d)
        dg_part = jnp.sum(dy * xhat, axis=0, keepdims=True)

        @pl.when(i == 0)
        def _():
            dg_ref[...] = dg_part
            loss_ref[...] = jnp.broadcast_to(part, (8, LANES))

        @pl.when(i > 0)
        def _():
            dg_ref[...] += dg_part
            loss_ref[...] += jnp.broadcast_to(part, (8, LANES))

        dxh = dy * gain
        dx = r * (dxh - xhat * jnp.mean(dxh * xhat, axis=-1, keepdims=True))
        dx_ref[...] = dx
        dxb_ref[...] = dx.astype(BF16)

    row_spec = pl.BlockSpec((tm, d), lambda i: (i, 0))
    vec = pl.BlockSpec((1, d), lambda i: (0, 0))
    return pl.pallas_call(
        body, name=name, grid=(s_len // tm,),
        in_specs=[row_spec, vec, row_spec],
        out_specs=[row_spec, row_spec, vec, pl.BlockSpec((8, LANES), lambda i: (0, 0))],
        out_shape=[_sds((s_len, d), F32), _sds((s_len, d), BF16), _sds((1, d), F32), _sds((8, LANES), F32)],
        compiler_params=_params("arbitrary"),
    )(x, g, target)


FLIP_BITS = {"c": (0, 0, 1), "x": (1, 0, 0), "y": (0, 1, 0), "xy": (1, 1, 0),
             "xc": (1, 0, 1), "yc": (0, 1, 1), "xyc": (1, 1, 1)}
CHIP_FLIPS = ("x", "y", "xy")


def _flip(pos, name):
    return tuple(1 - p if bit else p for p, bit in zip(pos, FLIP_BITS[name]))


def _chip(pos):
    return 2 * pos[0] + pos[1]


def _me():
    return (lax.axis_index("x"), lax.axis_index("y"), lax.axis_index("c"))


def _exchange_copy(remote, k, in_refs, out_refs, send_sems, recv_sems, sender, receiver):
    ii, src_fn, oi, dst_fn, _ = remote[k]
    return pltpu.make_async_remote_copy(
        src_ref=src_fn(in_refs[ii], sender, receiver), dst_ref=dst_fn(out_refs[oi], sender),
        send_sem=send_sems.at[k], recv_sem=recv_sems.at[k], device_id=receiver, device_id_type=MESH)


def _exchange_start(remote, in_refs, out_refs, send_sems, recv_sems):
    me = _me()
    for k in range(len(remote)):
        _exchange_copy(remote, k, in_refs, out_refs, send_sems, recv_sems, me, _flip(me, remote[k][4])).start()


def _exchange_finish(remote, in_refs, out_refs, send_sems, recv_sems):
    me = _me()
    for k in range(len(remote)):
        _exchange_copy(remote, k, in_refs, out_refs, send_sems, recv_sems, _flip(me, remote[k][4]), me).wait_recv()
    for k in range(len(remote)):
        _exchange_copy(remote, k, in_refs, out_refs, send_sems, recv_sems, me, _flip(me, remote[k][4])).wait_send()


def _exchange_sems(remote):
    n = len(remote)
    return [pltpu.SemaphoreType.DMA((n,)), pltpu.SemaphoreType.DMA((n,))] if n else []


def exchange(name, ins, out_shapes, remote, local):
    n_in, n_out = len(ins), len(out_shapes)

    def body(*refs):
        in_refs, out_refs = refs[:n_in], refs[n_in:n_in + n_out]
        send_sems, recv_sems, local_sems = refs[n_in + n_out:]
        me = _me()
        stays = []
        for k, (ii, src_fn, oi, dst_fn) in enumerate(local):
            cp = pltpu.make_async_copy(src_fn(in_refs[ii], me), dst_fn(out_refs[oi], me), local_sems.at[k])
            cp.start()
            stays.append(cp)
        _exchange_start(remote, in_refs, out_refs, send_sems, recv_sems)
        _exchange_finish(remote, in_refs, out_refs, send_sems, recv_sems)
        for cp in stays:
            cp.wait()

    hbm = pl.BlockSpec(memory_space=pl.ANY)
    return pl.pallas_call(
        body, name=name,
        in_specs=[hbm] * n_in, out_specs=[hbm] * n_out, out_shape=list(out_shapes),
        scratch_shapes=[pltpu.SemaphoreType.DMA((max(len(remote), 1),)),
                        pltpu.SemaphoreType.DMA((max(len(remote), 1),)),
                        pltpu.SemaphoreType.DMA((max(len(local), 1),))],
        compiler_params=pltpu.CompilerParams(has_side_effects=True),
    )(*ins)


def staged_push(name, ins, out_shapes, jobs, n_alias=0):
    n_in, n_out = len(ins), len(out_shapes)
    n_copies = sum(len(dsts) for _, _, dsts in jobs)
    n_remote = sum(1 for _, _, dsts in jobs for d in dsts if d[2] is not None)

    def chunk_of(ii, src_fn):
        probe = _ShapeRef(ins[ii].shape, ins[ii].dtype)
        got = src_fn(probe, (0, 0, 0))
        return tuple(got.shape), got.dtype

    classes = []
    for ii, src_fn, _ in jobs:
        c = chunk_of(ii, src_fn)
        if c not in classes:
            classes.append(c)

    def body(*refs):
        in_refs, out_refs = refs[:n_in], refs[n_in:n_in + n_out]
        bufs = refs[n_in + n_out:n_in + n_out + len(classes)]
        load_sems, out_sems, recv_sems = refs[n_in + n_out + len(classes):]
        me = _me()
        pending = [[[], []] for _ in classes]
        used = [0] * len(classes)
        arrivals = []
        k = r = 0

        def begin_load(job):
            ii, src_fn, _ = job
            cls = classes.index(chunk_of(ii, src_fn))
            slot = used[cls] % 2
            used[cls] += 1
            for kind, cp in pending[cls][slot]:
                cp.wait_send() if kind == "remote" else cp.wait()
            pending[cls][slot] = []
            load = pltpu.make_async_copy(src_fn(in_refs[ii], me), bufs[cls].at[slot], load_sems.at[2 * cls + slot])
            load.start()
            return load, cls, slot

        nxt = begin_load(jobs[0])
        for n, (ii, src_fn, dsts) in enumerate(jobs):
            load, cls, slot = nxt
            load.wait()
            buf = bufs[cls].at[slot]
            sent = []
            for oi, dst_fn, flip in dsts:
                if flip is None:
                    cp = pltpu.make_async_copy(buf, dst_fn(out_refs[oi], me), out_sems.at[k])
                    cp.start()
                    sent.append(("local", cp))
                else:
                    peer = _flip(me, flip)
                    cp = pltpu.make_async_remote_copy(
                        src_ref=buf, dst_ref=dst_fn(out_refs[oi], me), send_sem=out_sems.at[k],
                        recv_sem=recv_sems.at[r], device_id=peer, device_id_type=MESH)
                    cp.start()
                    sent.append(("remote", cp))
                    arrivals.append((r, cls, oi, dst_fn, peer))
                    r += 1
                k += 1
            pending[cls][slot] = sent
            if n + 1 < len(jobs):
                nxt = begin_load(jobs[n + 1])
        for per_class in pending:
            for slot_list in per_class:
                for kind, cp in slot_list:
                    cp.wait_send() if kind == "remote" else cp.wait()
        for r, cls, oi, dst_fn, peer in arrivals:
            pltpu.make_async_remote_copy(
                src_ref=bufs[cls].at[0], dst_ref=dst_fn(out_refs[oi], peer), send_sem=out_sems.at[0],
                recv_sem=recv_sems.at[r], device_id=peer, device_id_type=MESH).wait_recv()

    hbm = pl.BlockSpec(memory_space=pl.ANY)
    return pl.pallas_call(
        body, name=name,
        in_specs=[hbm] * n_in, out_specs=[hbm] * n_out, out_shape=list(out_shapes),
        scratch_shapes=[pltpu.VMEM((2,) + shape, dtype) for shape, dtype in classes]
        + [pltpu.SemaphoreType.DMA((2 * len(classes),)), pltpu.SemaphoreType.DMA((max(n_copies, 1),)),
           pltpu.SemaphoreType.DMA((max(n_remote, 1),))],
        input_output_aliases={i: i for i in range(n_alias)},
        compiler_params=pltpu.CompilerParams(has_side_effects=True, vmem_limit_bytes=VMEM_LIMIT),
    )(*ins)


class _ShapeRef:
    def __init__(self, shape, dtype):
        self.shape, self.dtype = tuple(shape), dtype

    @property
    def at(self):
        return self

    def __getitem__(self, idx):
        idx = idx if isinstance(idx, tuple) else (idx,)
        shape = []
        for dim, i in zip(self.shape, idx):
            if isinstance(i, slice):
                shape.append(len(range(*i.indices(dim))))
            elif hasattr(i, "size") and hasattr(i, "start"):
                shape.append(i.size)
        shape += self.shape[len(idx):]
        return _ShapeRef(shape, self.dtype)


def gather_whole(shards, name):
    whole = lambda ref, sender, receiver=None: ref
    slot = lambda ref, sender: ref.at[_chip(sender)]
    remote = [(t, whole, t, slot, flip) for t in range(len(shards)) for flip in CHIP_FLIPS]
    local = [(t, whole, t, slot) for t in range(len(shards))]
    outs = [_sds((N_CHIPS,) + a.shape, a.dtype) for a in shards]
    return exchange(name, list(shards), outs, remote, local)


def _half_axis(shape):
    return 0 if shape[0] >= 2 else 1


def gather_halves_plan(shards):
    remote = []
    for t, a in enumerate(shards):
        ax = _half_axis(a.shape)
        half = lambda ref, sender, receiver=None, ax=ax: _half(ref, sender[2], ax)
        slot = lambda ref, sender, ax=ax: _half(ref.at[_chip(sender)], sender[2], ax)
        remote += [(t, half, t, slot, flip) for flip in CHIP_FLIPS]
    outs = [_sds((N_CHIPS,) + a.shape, a.dtype) for a in shards]
    return list(shards), outs, remote


def gather_halves_fill(got, shards, name):
    n_t = len(shards)
    jobs = []
    for t, a in enumerate(shards):
        layers = a.shape[0]
        for l in range(layers):
            jobs.append((n_t + t, lambda ref, me, l=l: ref.at[l],
                         [(t, lambda ref, sender, l=l: ref.at[_chip(sender), l], None)]))
        for flip in CHIP_FLIPS:
            if _half_axis(a.shape) == 0:
                n = layers // 2
                for j in range(n):
                    at = lambda ref, pos, flip=flip, j=j, n=n: ref.at[_chip(_flip(pos, flip)), pos[2] * n + j]
                    jobs.append((t, at, [(t, at, "c")]))
            else:
                rows = a.shape[1] // 2
                at = lambda ref, pos, flip=flip, rows=rows: ref.at[
                    _chip(_flip(pos, flip)), 0, pl.ds(pos[2] * rows, rows)]
                jobs.append((t, at, [(t, at, "c")]))
    outs = [_sds(g.shape, g.dtype) for g in got]
    return staged_push(name, list(got) + list(shards), outs, jobs, n_alias=n_t)


def add_to_all(plan, buf):
    ins, outs, remote = plan
    whole = lambda ref, sender, receiver=None: ref
    more = [(len(ins), whole, len(outs), (lambda ref, sender, f=f: ref.at[f]), flip)
            for f, flip in enumerate(FLIPS_BY_INDEX)]
    return list(ins) + [buf], list(outs) + [_sds((len(more),) + buf.shape, buf.dtype)], list(remote) + more


FLIPS_BY_INDEX = ("c", "y", "yc", "x", "xc", "xy", "xyc")


def sum_devices(own, got, name):
    rows = own.shape[0]
    tr = LANES if rows % LANES == 0 else rows
    me = (4 * lax.axis_index("x") + 2 * lax.axis_index("y") + lax.axis_index("c")).astype(jnp.int32).reshape(1)
    everyone = jnp.concatenate([own[None], got], axis=0)

    def body(me_ref, a0, a1, a2, a3, a4, a5, a6, a7, o_ref):
        o_ref[...] = ((a0[...] + a1[...]) + (a2[...] + a3[...])) + ((a4[...] + a5[...]) + (a6[...] + a7[...]))

    return pl.pallas_call(
        body, name=name,
        grid_spec=pltpu.PrefetchScalarGridSpec(
            num_scalar_prefetch=1, grid=(rows // tr,),
            in_specs=[pl.BlockSpec((None, tr, LANES), lambda i, m, k=k: (m[0] ^ k, i, 0)) for k in range(8)],
            out_specs=pl.BlockSpec((tr, LANES), lambda i, m: (i, 0))),
        out_shape=_sds((rows, LANES), F32),
        compiler_params=_params("arbitrary"),
    )(me, *([everyone] * 8))


def _half(ref, core, axis):
    rows = ref.shape[axis] // 2
    idx = (slice(None),) * axis + (pl.ds(core * rows, rows),)
    return ref.at[idx]


def reduce_plan(grads):
    remote, outs = [], []
    for t, g in enumerate(grads):
        outs.append(_sds((len(FLIP_BITS), g.shape[1] // 2, g.shape[2]), BF16))
        for f, flip in enumerate(FLIP_BITS):
            remote.append((t, lambda ref, sender, receiver: _half(ref.at[_chip(receiver)], receiver[2], 0),
                           t, lambda ref, sender, f=f: ref.at[f], flip))
    return list(grads), outs, remote


def reduce_finish(grads, got, stacks, full_shapes, into, name):
    chip = (2 * lax.axis_index("x") + lax.axis_index("y")).astype(jnp.int32).reshape(1)
    core = lax.axis_index("c").astype(jnp.int32).reshape(1)
    totals = [sum_partials(g, r, chip, core, f"{name}_sum{t}") for t, (g, r) in enumerate(zip(grads, got))]
    names = []
    for out_name, _ in stacks:
        if out_name not in names:
            names.append(out_name)
    names = [n for n in names if n in into] + [n for n in names if n not in into]
    kept = [into[n] for n in names if n in into]
    outs = [_sds(full_shapes[n], F32) for n in names]
    jobs = []
    for t, (out_name, layer) in enumerate(stacks):
        oi = names.index(out_name)
        rows, cols = totals[t].shape
        pieces = max(1, rows * cols * 4 // STAGE_BYTES)
        step = rows // pieces
        for q in range(pieces):
            src = lambda ref, me, q=q, step=step: ref.at[pl.ds(q * step, step)]
            place = lambda ref, sender, layer=layer, q=q, step=step, rows=rows: ref.at[
                layer, pl.ds(sender[2] * rows + q * step, step)]
            jobs.append((len(kept) + t, src, [(oi, place, None), (oi, place, "c")]))
    full = staged_push(name + "_share", kept + totals, outs, jobs, n_alias=len(kept))
    return {**into, **dict(zip(names, full))}


STAGE_BYTES = 1024 * 1024


def add_rows(a, theirs, core, name):
    rows, cols = a.shape
    half = rows // 2
    tr = _row_tile(half, 256)
    nb = half // tr

    def body(core_ref, a_ref, t_ref, o_ref):
        o_ref[...] = a_ref[...] + t_ref[...]

    return pl.pallas_call(
        body, name=name,
        grid_spec=pltpu.PrefetchScalarGridSpec(
            num_scalar_prefetch=1, grid=(nb,),
            in_specs=[pl.BlockSpec((tr, cols), lambda i, c: (c[0] * nb + i, 0)),
                      pl.BlockSpec((tr, cols), lambda i, c: (i, 0))],
            out_specs=pl.BlockSpec((tr, cols), lambda i, c: (i, 0))),
        out_shape=_sds((half, cols), F32),
        compiler_params=_params("arbitrary"),
    )(core, a, theirs)


def sum_partials(mine, theirs, chip, core, name):
    _, rows, cols = mine.shape
    half = rows // 2
    tr = _row_tile(half, 256)
    nb = half // tr
    n_got = theirs.shape[0]

    def body(chip_ref, core_ref, m_ref, *refs):
        acc = m_ref[...].astype(F32)
        for r in refs[:n_got]:
            acc = acc + r[...].astype(F32)
        refs[n_got][...] = acc

    got = lambda f: pl.BlockSpec((None, tr, cols), lambda i, ch, co: (f, i, 0))
    return pl.pallas_call(
        body, name=name,
        grid_spec=pltpu.PrefetchScalarGridSpec(
            num_scalar_prefetch=2, grid=(nb,),
            in_specs=[pl.BlockSpec((None, tr, cols), lambda i, ch, co: (ch[0], co[0] * nb + i, 0))]
            + [got(f) for f in range(n_got)],
            out_specs=pl.BlockSpec((tr, cols), lambda i, ch, co: (i, 0))),
        out_shape=_sds((half, cols), F32),
        compiler_params=_params("arbitrary"),
    )(chip, core, mine, *([theirs] * n_got))


def sum_chips(mine, theirs, chip, name):
    _, half, cols = mine.shape
    tr = _row_tile(half, 256)

    def body(chip_ref, m_ref, a_ref, b_ref, c_ref, o_ref):
        o_ref[...] = ((m_ref[...].astype(F32) + a_ref[...].astype(F32))
                      + b_ref[...].astype(F32)) + c_ref[...].astype(F32)

    got = lambda f: pl.BlockSpec((None, tr, cols), lambda i, ch: (f, i, 0))
    return pl.pallas_call(
        body, name=name,
        grid_spec=pltpu.PrefetchScalarGridSpec(
            num_scalar_prefetch=1, grid=(half // tr,),
            in_specs=[pl.BlockSpec((None, tr, cols), lambda i, ch: (ch[0], i, 0)), got(0), got(1), got(2)],
            out_specs=pl.BlockSpec((tr, cols), lambda i, ch: (i, 0))),
        out_shape=_sds((half, cols), F32),
        compiler_params=_params("arbitrary"),
    )(chip, mine, theirs, theirs, theirs)


def _tok(t):
    return t.transpose(1, 0, 2).reshape(t.shape[1], t.shape[0] * t.shape[2])


def _heads(t):
    return t.reshape(t.shape[0], t.shape[1] // HEAD_DIM, HEAD_DIM).transpose(1, 0, 2)


def _tile2(vec):
    return jnp.tile(vec.reshape(1, HEAD_DIM), (1, 2))


REST = ("att_w_in", "att_w_out", "sgu_w_in", "sgu_w_out", "mlp_w1", "mlp_w2")
LAST_GROUP = (("att_w_in", 0),)
LATE_SMALL = ("att_norm", "att_sink", "att_qnorm", "att_knorm")


def local_step(x, target, first, rest_shards, rep, full_shapes):
    s_len, d = x.shape
    tabs = rope_tables(s_len)
    depth = rep["mlp_norm"].shape[0]
    row = lambda a: a.reshape(1, -1)
    saved = []
    h = x
    gw = {"att_w_in": [first]}

    def wl(name, idx):
        return (gw[name][idx], 0) if name == "att_w_in" else (gw[name], idx)

    for layer in range(depth):
        i = layer // 2
        tag = f"l{layer}"
        if layer % 2 == 0:
            hn, proj = norm_mm(h, row(rep["att_norm"][i]), *wl("att_w_in", i), F32, tag + "_att_proj")
            qkv_t, kv = prep_fwd(proj, tabs, _tile2(rep["att_qnorm"][i]), _tile2(rep["att_knorm"][i]),
                                 tag + "_att_prep")
            kv_tok = _heads(kv)
            oa, lse_a = flash_fwd_t(qkv_t, kv_tok, QA_COL // GROUP_W, 0, (KA_COL + LANES) // HEAD_DIM,
                                    rep["att_sink"][i], True, tag + "_win_fwd")
            plan = gather_halves_plan(rest_shards) if layer == 0 else None
            ob, lse_b, *got = flash_fwd_t(qkv_t, kv_tok, QB_COL // GROUP_W, 4, (KB_COL + LANES) // HEAD_DIM,
                                          None, False, tag + "_grid_fwd", comm=plan)
            if layer == 0:
                rest = dict(zip(REST, gather_halves_fill(got, rest_shards, "gather_rest_fill")))
                gw["att_w_in"].append(rest.pop("att_w_in"))
                gw.update(rest)
            out = mm_res_t([oa, ob], *wl("att_w_out", i), h, tag + "_att_out")
            mix_saved = (h, hn, proj, qkv_t, kv_tok, oa, ob, lse_a, lse_b)
        else:
            hn, zpre = norm_mm(h, row(rep["sgu_norm"][i]), *wl("sgu_w_in", i), F32, tag + "_sgu_in")
            ws = rep["sgu_w_s"][i].astype(BF16)
            bsb = jnp.broadcast_to(rep["sgu_b_s"][i][:, :, None], (SGU_GROUPS, SGU_CHUNK, LANES))
            y = sgu_mid_fwd(zpre, row(rep["sgu_ln_g"][i]), row(rep["sgu_ln_b"][i]), ws, bsb, tag + "_sgu_mid")
            out = mm_res(y, *wl("sgu_w_out", i), h, tag + "_sgu_out")
            mix_saved = (h, hn, zpre, y, ws, bsb)
        hm, a = norm_mm(out, row(rep["mlp_norm"][layer]), *wl("mlp_w1", layer), BF16, tag + "_mlp_up")
        nxt = mm_res(a, *wl("mlp_w2", layer), out, tag + "_mlp_down", relu2=True)
        saved.append((mix_saved, (out, hm, a)))
        h = nxt
    dh, dhb, d_final, loss_tile = loss_head(h, row(rep["final_norm"]), target, "loss_head")
    big, tags = [], []
    small = {k: [jnp.zeros(v.shape[1:], F32)] * v.shape[0] for k, v in rep.items() if k != "final_norm"}
    small["final_norm"] = d_final.reshape(-1)
    stacked = lambda: [small[n] if n == "final_norm" else jnp.stack(small[n]) for n in SMALL]
    for layer in reversed(range(depth)):
        i = layer // 2
        tag = f"l{layer}"
        mix_saved, (xin, hm, a) = saved[layer]
        da = mm_nt_relu2_bwd(dhb, *wl("mlp_w2", layer), a, tag + "_mlp_down_bwd")
        big.append(dw_mm(a, dhb, tag + "_mlp_dw2", col_sharded=False, relu2=True))
        tags.append(("mlp_w2", layer))
        big.append(dw_mm(hm, da, tag + "_mlp_dw1", col_sharded=True))
        tags.append(("mlp_w1", layer))
        dh, dhb, dg = dx_norm(da, *wl("mlp_w1", layer), xin, row(rep["mlp_norm"][layer]), dh, tag + "_mlp_up_bwd")
        small["mlp_norm"][layer] = dg.reshape(-1)
        if layer % 2 == 0:
            xin, hn, proj, qkv_t, kv_tok, oa, ob, lse_a, lse_b = mix_saved
            do_t = mm_nt(dhb, *wl("att_w_out", i), tag + "_att_out_bwd", transposed=True)
            big.append(dw_nn([oa, ob], dhb, tag + "_att_dwout"))
            tags.append(("att_w_out", i))
            dqa, dka, dva, dsink = flash_bwd_t(qkv_t, kv_tok, oa, do_t, lse_a, QA_COL // GROUP_W, 0, 2,
                                               KA_COL // HEAD_DIM, 0, rep["att_sink"][i], True, tag + "_win_bwd")
            plan = None
            if layer == 0:
                early = [k for k, t in enumerate(tags) if t not in LAST_GROUP]
                early_pack = _pack(stacked() + [loss_tile[0, :1]])
                plan = add_to_all(reduce_plan([big[k] for k in early]), early_pack)
            dqb, dkb, dvb, *got = flash_bwd_t(qkv_t, kv_tok, ob, do_t, lse_b, QB_COL // GROUP_W, 4, 6,
                                              KB_COL // HEAD_DIM, 2, None, False, tag + "_grid_bwd", comm=plan)
            if layer == 0:
                grads = reduce_finish([big[k] for k in early], got[:-1], [tags[k] for k in early], full_shapes,
                                      {}, "grads1")
                early_sum = sum_devices(early_pack, got[-1], "sum_small")
            qg, kg = _tile2(rep["att_qnorm"][i]), _tile2(rep["att_knorm"][i])
            dproj, dqg, dkg = prep_bwd(proj, dqa, _tok(dka), _tok(dva), dqb, _tok(dkb), _tok(dvb),
                                       tabs, qg, kg, tag + "_att_prep_bwd")
            big.append(dw_mm(hn, dproj, tag + "_att_dwin", col_sharded=True))
            tags.append(("att_w_in", i))
            dh, dhb, dg = dx_norm(dproj, *wl("att_w_in", i), xin, row(rep["att_norm"][i]), dh, tag + "_att_proj_bwd")
            small["att_norm"][i] = dg.reshape(-1)
            small["att_sink"][i] = dsink[:, 0, :GROUP].reshape(-1)
            small["att_qnorm"][i] = dqg[0, :HEAD_DIM] + dqg[0, HEAD_DIM:]
            small["att_knorm"][i] = dkg[0, :HEAD_DIM] + dkg[0, HEAD_DIM:]
        else:
            xin, hn, zpre, y, ws, bsb = mix_saved
            dy = mm_nt(dhb, *wl("sgu_w_out", i), tag + "_sgu_out_bwd")
            big.append(dw_mm(y, dhb, tag + "_sgu_dwout", col_sharded=False))
            tags.append(("sgu_w_out", i))
            wst = ws.transpose(0, 2, 1)
            dz, dws, dbs, dlg, dlb = sgu_mid_bwd(zpre, dy, row(rep["sgu_ln_g"][i]), row(rep["sgu_ln_b"][i]),
                                                 ws, wst, bsb, tag + "_sgu_mid_bwd")
            big.append(dw_mm(hn, dz, tag + "_sgu_dwin", col_sharded=True))
            tags.append(("sgu_w_in", i))
            dh, dhb, dg = dx_norm(dz, *wl("sgu_w_in", i), xin, row(rep["sgu_norm"][i]), dh, tag + "_sgu_in_bwd")
            small["sgu_norm"][i] = dg.reshape(-1)
            small["sgu_ln_g"][i] = dlg.reshape(-1)
            small["sgu_ln_b"][i] = dlb.reshape(-1)
            small["sgu_w_s"][i] = dws
            small["sgu_b_s"][i] = dbs[:, :, 0]
    late = [k for k, t in enumerate(tags) if t in LAST_GROUP]
    late_pack = _pack([small[n][0] for n in LATE_SMALL])
    got = exchange("grads2_scatter", *add_to_all(reduce_plan([big[k] for k in late]), late_pack), [])
    grads = reduce_finish([big[k] for k in late], got[:-1], [tags[k] for k in late], full_shapes, grads, "grads2")
    late_sum = sum_devices(late_pack, got[-1], "sum_small_late")
    shapes = [a.shape for a in stacked()]
    *small_g, loss = _unpack(early_sum, shapes + [()])
    small_g = dict(zip(SMALL, small_g))
    for n, g in zip(LATE_SMALL, _unpack(late_sum, [small[n][0].shape for n in LATE_SMALL])):
        small_g[n] = small_g[n].at[0].add(g)
    return loss, dh, grads, small_g


BIG = ("att_w_in", "att_w_out", "sgu_w_in", "sgu_w_out", "mlp_w1", "mlp_w2")
SHARDED_VEC = ("sgu_norm", "sgu_ln_g", "sgu_ln_b")
REPLICATED = ("att_norm", "att_sink", "att_qnorm", "att_knorm", "sgu_w_s", "sgu_b_s", "mlp_norm", "final_norm")
WEIGHTS = ("att_norm", "att_w_in", "att_sink", "att_qnorm", "att_knorm", "att_w_out", "sgu_norm", "sgu_w_in",
           "sgu_ln_g", "sgu_ln_b", "sgu_w_s", "sgu_b_s", "sgu_w_out", "mlp_norm", "mlp_w1", "mlp_w2", "final_norm")
SMALL = tuple(n for n in WEIGHTS if n not in BIG)
PACK_ALIGN = 8 * LANES


def _pack(arrays):
    flat = jnp.concatenate([a.reshape(-1) for a in arrays])
    pad = -flat.shape[0] % PACK_ALIGN
    return jnp.pad(flat, (0, pad)).reshape(-1, LANES)


def _unpack(flat2d, shapes):
    flat = flat2d.reshape(-1)
    out, off = [], 0
    for shape in shapes:
        size = int(np.prod(shape))
        out.append(flat[off:off + size].reshape(shape))
        off += size
    return out


def kernel(x, att_norm, att_w_in, att_sink, att_qnorm, att_knorm, att_w_out, sgu_norm, sgu_w_in, sgu_ln_g, sgu_ln_b, sgu_w_s, sgu_b_s, sgu_w_out, mlp_norm, mlp_w1, mlp_w2, final_norm, loss_target, m_att_norm, m_att_w_in, m_att_sink, m_att_qnorm, m_att_knorm, m_att_w_out, m_sgu_norm, m_sgu_w_in, m_sgu_ln_g, m_sgu_ln_b, m_sgu_w_s, m_sgu_b_s, m_sgu_w_out, m_mlp_norm, m_mlp_w1, m_mlp_w2, m_final_norm, v_att_norm, v_att_w_in, v_att_sink, v_att_qnorm, v_att_knorm, v_att_w_out, v_sgu_norm, v_sgu_w_in, v_sgu_ln_g, v_sgu_ln_b, v_sgu_w_s, v_sgu_b_s, v_sgu_w_out, v_mlp_norm, v_mlp_w1, v_mlp_w2, v_final_norm):
    w = dict(att_norm=att_norm, att_w_in=att_w_in, att_sink=att_sink, att_qnorm=att_qnorm, att_knorm=att_knorm,
             att_w_out=att_w_out, sgu_norm=sgu_norm, sgu_w_in=sgu_w_in, sgu_ln_g=sgu_ln_g, sgu_ln_b=sgu_ln_b,
             sgu_w_s=sgu_w_s, sgu_b_s=sgu_b_s, sgu_w_out=sgu_w_out, mlp_norm=mlp_norm, mlp_w1=mlp_w1,
             mlp_w2=mlp_w2, final_norm=final_norm)
    m = dict(att_norm=m_att_norm, att_w_in=m_att_w_in, att_sink=m_att_sink, att_qnorm=m_att_qnorm,
             att_knorm=m_att_knorm, att_w_out=m_att_w_out, sgu_norm=m_sgu_norm, sgu_w_in=m_sgu_w_in,
             sgu_ln_g=m_sgu_ln_g, sgu_ln_b=m_sgu_ln_b, sgu_w_s=m_sgu_w_s, sgu_b_s=m_sgu_b_s,
             sgu_w_out=m_sgu_w_out, mlp_norm=m_mlp_norm, mlp_w1=m_mlp_w1, mlp_w2=m_mlp_w2,
             final_norm=m_final_norm)
    v = dict(att_norm=v_att_norm, att_w_in=v_att_w_in, att_sink=v_att_sink, att_qnorm=v_att_qnorm,
             att_knorm=v_att_knorm, att_w_out=v_att_w_out, sgu_norm=v_sgu_norm, sgu_w_in=v_sgu_w_in,
             sgu_ln_g=v_sgu_ln_g, sgu_ln_b=v_sgu_ln_b, sgu_w_s=v_sgu_w_s, sgu_b_s=v_sgu_b_s,
             sgu_w_out=v_sgu_w_out, mlp_norm=v_mlp_norm, mlp_w1=v_mlp_w1, mlp_w2=v_mlp_w2,
             final_norm=v_final_norm)
    chip = 2 * lax.axis_index("x") + lax.axis_index("y")

    vecs = jnp.stack([w[n] for n in SHARDED_VEC])
    wb = {n: w[n].astype(BF16) for n in BIG}
    first, vec_all = gather_whole([wb["att_w_in"][0:1], vecs], "gather_first")
    rest_shards = [wb[n][1:2] if n == "att_w_in" else wb[n] for n in REST]
    vec_full = vec_all.transpose(1, 2, 0, 3).reshape(vecs.shape[0], vecs.shape[1], -1)
    rep = {n: w[n] for n in REPLICATED}
    rep.update({n: vec_full[k] for k, n in enumerate(SHARDED_VEC)})

    loss, grad_x, grads, small_g = local_step(x[0], loss_target[0], first, rest_shards, rep,
                                              {n: w[n].shape for n in BIG})
    width = w["sgu_norm"].shape[1]
    for n in SHARDED_VEC:
        small_g[n] = lax.dynamic_slice_in_dim(small_g[n], chip * width, width, axis=1)
    grads.update(small_g)
    for n in BIG:
        grads[n] = grads[n].reshape(w[n].shape)

    delta, new_m, new_v = {}, {}, {}
    for n in WEIGHTS:
        shape = w[n].shape
        two_d = (lambda a: a.reshape(1, -1)) if len(shape) == 1 else (lambda a: a)
        dn, mn, vn = adamw(two_d(w[n]), two_d(grads[n]), two_d(m[n]), two_d(v[n]), "adamw_" + n)
        delta[n], new_m[n], new_v[n] = dn.reshape(shape), mn.reshape(shape), vn.reshape(shape)
    return (loss, grad_x[None], *[grads[n] for n in WEIGHTS], *[delta[n] for n in WEIGHTS],
            *[new_m[n] for n in WEIGHTS], *[new_v[n] for n in WEIGHTS])
```

```python
import functools

import numpy as np
import jax
import jax.numpy as jnp
from jax import lax
from jax.experimental import pallas as pl
from jax.experimental.pallas import tpu as pltpu

F32 = jnp.float32
BF16 = jnp.bfloat16
MESH = pl.DeviceIdType.MESH

EPS = 1e-6
HEAD_DIM = 64
BLOCK = 128
GRID_W = 64
ROPE_THETA = 10000.0
N_CHIPS = 4
LANES = 128
V7X_VMEM_BYTES = 64 * 1024 * 1024
VMEM_LIMIT = V7X_VMEM_BYTES - 8 * 1024 * 1024

ADAM_LR = 0.001
ADAM_B1 = 0.9
ADAM_B2 = 0.999
ADAM_EPS = 1e-08
ADAM_WD = 0.01
ADAM_STEP = 10

NT_DIMS = (((1,), (1,)), ((), ()))
TN_DIMS = (((0,), (0,)), ((), ()))


def _params(*sem):
    return pltpu.CompilerParams(dimension_semantics=sem, vmem_limit_bytes=VMEM_LIMIT)


def _sds(shape, dtype):
    return jax.ShapeDtypeStruct(tuple(shape), dtype)


def _row_tile(rows, want):
    t = min(rows, want)
    assert rows % t == 0, (rows, want)
    return t


def norm_mm(x, g, w4, layer, out_dtype, name):
    s_len, d = x.shape
    ns = w4.shape[-1]
    tm = _row_tile(s_len, 512)

    def body(x_ref, g_ref, w_ref, h_ref, y_ref):
        xf = x_ref[...]
        r = lax.rsqrt(jnp.mean(xf * xf, axis=-1, keepdims=True) + EPS)
        h = ((xf * r) * g_ref[...]).astype(BF16)
        h_ref[...] = h
        for s in range(N_CHIPS):
            y_ref[:, s * ns:(s + 1) * ns] = jnp.dot(h, w_ref[s], preferred_element_type=F32).astype(y_ref.dtype)

    return pl.pallas_call(
        body, name=name, grid=(s_len // tm,),
        in_specs=[pl.BlockSpec((tm, d), lambda i: (i, 0)),
                  pl.BlockSpec((1, d), lambda i: (0, 0)),
                  pl.BlockSpec((N_CHIPS, None, d, ns), lambda i: (0, layer, 0, 0))],
        out_specs=[pl.BlockSpec((tm, d), lambda i: (i, 0)),
                   pl.BlockSpec((tm, N_CHIPS * ns), lambda i: (i, 0))],
        out_shape=[_sds((s_len, d), BF16), _sds((s_len, N_CHIPS * ns), out_dtype)],
        compiler_params=_params("arbitrary"),
    )(x, g, w4)


def mm_res(a, w4, layer, res, name, relu2=False):
    s_len, k = a.shape
    kq, n = w4.shape[-2:]
    assert kq * N_CHIPS == k
    tm = _row_tile(s_len, 256 if k > 1024 else 512)

    def body(a_ref, w0, w1, w2, w3, r_ref, o_ref):
        acc = r_ref[...]
        for s, w_ref in enumerate((w0, w1, w2, w3)):
            av = a_ref[:, s * kq:(s + 1) * kq]
            if relu2:
                t = jnp.maximum(av.astype(F32), 0.0)
                av = (t * t).astype(BF16)
            acc = acc + jnp.dot(av, w_ref[...], preferred_element_type=F32)
        o_ref[...] = acc

    def wspec(s):
        return pl.BlockSpec((None, None, kq, n), lambda i: (s, layer, 0, 0))

    return pl.pallas_call(
        body, name=name, grid=(s_len // tm,),
        in_specs=[pl.BlockSpec((tm, k), lambda i: (i, 0)), wspec(0), wspec(1), wspec(2), wspec(3),
                  pl.BlockSpec((tm, n), lambda i: (i, 0))],
        out_specs=pl.BlockSpec((tm, n), lambda i: (i, 0)),
        out_shape=_sds((s_len, n), F32),
        compiler_params=_params("arbitrary"),
    )(a, w4, w4, w4, w4, res)


def mm_res_t(pieces, w4, layer, res, name):
    s_len = res.shape[0]
    kq, n = w4.shape[-2:]
    rows = pieces[0].shape[0]
    assert rows % kq == 0 and rows * len(pieces) == kq * N_CHIPS
    tm = _row_tile(s_len, 512)
    n_p = len(pieces)

    def body(*refs):
        p_refs, w_refs, (r_ref, o_ref) = refs[:n_p], refs[n_p:n_p + N_CHIPS], refs[n_p + N_CHIPS:]
        acc = r_ref[...]
        for s in range(N_CHIPS):
            p, off = divmod(s * kq, rows)
            acc = acc + lax.dot_general(p_refs[p][off:off + kq, :], w_refs[s][...], TN_DIMS,
                                        preferred_element_type=F32)
        o_ref[...] = acc

    def wspec(s):
        return pl.BlockSpec((None, None, kq, n), lambda i: (s, layer, 0, 0))

    return pl.pallas_call(
        body, name=name, grid=(s_len // tm,),
        in_specs=[pl.BlockSpec((rows, tm), lambda i: (0, i))] * n_p + [wspec(s) for s in range(N_CHIPS)]
        + [pl.BlockSpec((tm, n), lambda i: (i, 0))],
        out_specs=pl.BlockSpec((tm, n), lambda i: (i, 0)),
        out_shape=_sds((s_len, n), F32),
        compiler_params=_params("arbitrary"),
    )(*pieces, w4, w4, w4, w4, res)


def dw_nn(pieces, b, name):
    s_len, n = b.shape
    rows = pieces[0].shape[0]
    n_p = len(pieces)
    k = rows * n_p
    ts = _row_tile(s_len, 2048)
    n_s = s_len // ts

    def body(*refs):
        p_refs, (b_ref, o_ref, acc_ref) = refs[:n_p], refs[n_p:]
        s = pl.program_id(0)
        bv = b_ref[...]
        for p in range(n_p):
            part = jnp.dot(p_refs[p][...], bv, preferred_element_type=F32)
            at = slice(p * rows, (p + 1) * rows)
            if n_s == 1:
                o_ref[at, :] = part.astype(BF16)
                continue

            @pl.when(s == 0)
            def _():
                acc_ref[at, :] = part

            @pl.when((s > 0) & (s < n_s - 1))
            def _():
                acc_ref[at, :] += part

            @pl.when(s == n_s - 1)
            def _():
                o_ref[at, :] = (acc_ref[at, :] + part).astype(BF16)

    out = pl.pallas_call(
        body, name=name, grid=(n_s,),
        in_specs=[pl.BlockSpec((rows, ts), lambda s: (0, s))] * n_p + [pl.BlockSpec((ts, n), lambda s: (s, 0))],
        out_specs=pl.BlockSpec((k, n), lambda s: (0, 0)), out_shape=_sds((k, n), BF16),
        scratch_shapes=[pltpu.VMEM((k, n), F32)],
        compiler_params=_params("arbitrary"),
    )(*pieces, b)
    return out.reshape(N_CHIPS, k // N_CHIPS, n)


def mm_nt(dy, w4, layer, name, transposed=False):
    s_len, n = dy.shape
    mq = w4.shape[-2]
    tm = _row_tile(s_len, 512)

    def body(d_ref, w0, w1, w2, w3, o_ref):
        dv = d_ref[...]
        for s, w_ref in enumerate((w0, w1, w2, w3)):
            if transposed:
                o_ref[s * mq:(s + 1) * mq, :] = lax.dot_general(
                    w_ref[...], dv, NT_DIMS, preferred_element_type=F32).astype(BF16)
            else:
                o_ref[:, s * mq:(s + 1) * mq] = lax.dot_general(
                    dv, w_ref[...], NT_DIMS, preferred_element_type=F32).astype(BF16)

    def wspec(s):
        return pl.BlockSpec((None, None, mq, n), lambda i: (s, layer, 0, 0))

    m = N_CHIPS * mq
    return pl.pallas_call(
        body, name=name, grid=(s_len // tm,),
        in_specs=[pl.BlockSpec((tm, n), lambda i: (i, 0)), wspec(0), wspec(1), wspec(2), wspec(3)],
        out_specs=pl.BlockSpec((m, tm), lambda i: (0, i)) if transposed else pl.BlockSpec((tm, m), lambda i: (i, 0)),
        out_shape=_sds((m, s_len) if transposed else (s_len, m), BF16),
        compiler_params=_params("arbitrary"),
    )(dy, w4, w4, w4, w4)


def mm_nt_relu2_bwd(dy, w4, layer, a, name):
    s_len, n = dy.shape
    mq = w4.shape[-2]
    tm = _row_tile(s_len, 512)

    def body(d_ref, w_ref, a_ref, o_ref):
        dv = d_ref[...]
        for s in range(N_CHIPS):
            cols = slice(s * mq, (s + 1) * mq)
            dz = lax.dot_general(dv, w_ref[s], NT_DIMS, preferred_element_type=F32)
            o_ref[:, cols] = (dz * (2.0 * jnp.maximum(a_ref[:, cols].astype(F32), 0.0))).astype(BF16)

    return pl.pallas_call(
        body, name=name, grid=(s_len // tm,),
        in_specs=[pl.BlockSpec((tm, n), lambda i: (i, 0)),
                  pl.BlockSpec((N_CHIPS, None, mq, n), lambda i: (0, layer, 0, 0)),
                  pl.BlockSpec((tm, N_CHIPS * mq), lambda i: (i, 0))],
        out_specs=pl.BlockSpec((tm, N_CHIPS * mq), lambda i: (i, 0)),
        out_shape=_sds((s_len, N_CHIPS * mq), BF16),
        compiler_params=_params("arbitrary"),
    )(dy, w4, a)


def dx_norm(dy, w4, layer, x, g, dres, name):
    s_len, d = x.shape
    ns = w4.shape[-1]
    tm = _row_tile(s_len, 512)

    def body(dy_ref, w_ref, x_ref, g_ref, dr_ref, dx_ref, dxb_ref, dg_ref):
        i = pl.program_id(0)
        dh = lax.dot_general(dy_ref[:, 0:ns], w_ref[0], NT_DIMS, preferred_element_type=F32)
        for s in range(1, N_CHIPS):
            dh = dh + lax.dot_general(dy_ref[:, s * ns:(s + 1) * ns], w_ref[s], NT_DIMS,
                                      preferred_element_type=F32)
        xf = x_ref[...]
        r = lax.rsqrt(jnp.mean(xf * xf, axis=-1, keepdims=True) + EPS)
        xhat = xf * r
        dg_part = jnp.sum(dh * xhat, axis=0, keepdims=True)

        @pl.when(i == 0)
        def _():
            dg_ref[...] = dg_part

        @pl.when(i > 0)
        def _():
            dg_ref[...] += dg_part

        dxh = dh * g_ref[...]
        dx = dr_ref[...] + r * (dxh - xhat * jnp.mean(dxh * xhat, axis=-1, keepdims=True))
        dx_ref[...] = dx
        dxb_ref[...] = dx.astype(BF16)

    row = pl.BlockSpec((tm, d), lambda i: (i, 0))
    vec = pl.BlockSpec((1, d), lambda i: (0, 0))
    return pl.pallas_call(
        body, name=name, grid=(s_len // tm,),
        in_specs=[pl.BlockSpec((tm, N_CHIPS * ns), lambda i: (i, 0)),
                  pl.BlockSpec((N_CHIPS, None, d, ns), lambda i: (0, layer, 0, 0)), row, vec, row],
        out_specs=[row, row, vec],
        out_shape=[_sds((s_len, d), F32), _sds((s_len, d), BF16), _sds((1, d), F32)],
        compiler_params=_params("arbitrary"),
    )(dy, w4, x, g, dres)


def dw_mm(a, b, name, col_sharded, relu2=False):
    s_len, k = a.shape
    n = b.shape[1]
    ts = _row_tile(s_len, 2048)
    tk = min(k, 1024)
    tn = n // N_CHIPS if col_sharded else min(n, 1024)
    n_s = s_len // ts

    def body(a_ref, b_ref, o_ref, acc_ref):
        s = pl.program_id(2)
        av = a_ref[...]
        if relu2:
            t = jnp.maximum(av.astype(F32), 0.0)
            av = (t * t).astype(BF16)
        part = lax.dot_general(av, b_ref[...], TN_DIMS, preferred_element_type=F32)
        if n_s == 1:
            o_ref[...] = part.astype(BF16)
            return

        @pl.when(s == 0)
        def _():
            acc_ref[...] = part

        @pl.when((s > 0) & (s < n_s - 1))
        def _():
            acc_ref[...] += part

        @pl.when(s == n_s - 1)
        def _():
            o_ref[...] = (acc_ref[...] + part).astype(BF16)

    if col_sharded:
        out_shape = _sds((N_CHIPS, k, tn), BF16)
        out_spec = pl.BlockSpec((None, tk, tn), lambda i, j, s: (j, i, 0))
    else:
        out_shape = _sds((N_CHIPS, k // N_CHIPS, n), BF16)
        rows_per = k // N_CHIPS
        assert tk % rows_per == 0 or rows_per % tk == 0
        if tk >= rows_per:
            out_shape = _sds((k, n), BF16)
            out_spec = pl.BlockSpec((tk, tn), lambda i, j, s: (i, j))
        else:
            per = rows_per // tk
            out_spec = pl.BlockSpec((None, tk, tn), lambda i, j, s: (i // per, i % per, j))

    out = pl.pallas_call(
        body, name=name, grid=(k // tk, n // tn, n_s),
        in_specs=[pl.BlockSpec((ts, tk), lambda i, j, s: (s, i)),
                  pl.BlockSpec((ts, tn), lambda i, j, s: (s, j))],
        out_specs=out_spec, out_shape=out_shape,
        scratch_shapes=[pltpu.VMEM((tk, tn), F32)],
        compiler_params=_params("arbitrary", "arbitrary", "arbitrary"),
    )(a, b)
    if not col_sharded:
        out = out.reshape(N_CHIPS, k // N_CHIPS, n)
    return out


def ew(fn, ins, out_dtypes, name, tile_rows=256):
    rows, cols = ins[0].shape
    for a in ins:
        assert a.shape == (rows, cols), (name, a.shape, rows, cols)
    tr = rows if (rows <= tile_rows or rows % tile_rows) else tile_rows
    n_in = len(ins)

    def body(*refs):
        outs = fn(*[r[...] for r in refs[:n_in]])
        for o_ref, val in zip(refs[n_in:], outs):
            o_ref[...] = val.astype(o_ref.dtype)

    spec = pl.BlockSpec((tr, cols), lambda i: (i, 0))
    return pl.pallas_call(
        body, name=name, grid=(rows // tr,),
        in_specs=[spec] * n_in, out_specs=[spec] * len(out_dtypes),
        out_shape=[_sds((rows, cols), dt) for dt in out_dtypes],
        compiler_params=_params("arbitrary"),
    )(*ins)


def adamw(w, g, m, v, name):
    shape = w.shape
    cols = shape[-1]
    two_d = lambda a: a.reshape(-1, cols)

    def fn(wv, gv, mv, vv):
        m_new = ADAM_B1 * mv + (1.0 - ADAM_B1) * gv
        v_new = ADAM_B2 * vv + (1.0 - ADAM_B2) * (gv * gv)
        m_hat = m_new / (1.0 - ADAM_B1 ** ADAM_STEP)
        v_hat = v_new / (1.0 - ADAM_B2 ** ADAM_STEP)
        delta = -ADAM_LR * (m_hat / (jnp.sqrt(v_hat) + ADAM_EPS) + ADAM_WD * wv)
        return delta, m_new, v_new

    d, mn, vn = ew(fn, [two_d(w), two_d(g), two_d(m), two_d(v)], [F32, F32, F32], name)
    return d.reshape(shape), mn.reshape(shape), vn.reshape(shape)


def rope_tables(s_len):
    def angles(pos, dim):
        freqs = ROPE_THETA ** (-jnp.arange(0, dim, 2, dtype=F32) / dim)
        ang = pos.astype(F32)[:, None] * freqs[None, :]
        return jnp.cos(ang), jnp.sin(ang)

    pos = jnp.arange(s_len)
    rows = s_len // GRID_W
    row_idx = jnp.repeat(jnp.arange(rows), GRID_W)
    col_idx = jnp.tile(jnp.arange(GRID_W), rows)
    c1, s1 = angles(pos, HEAD_DIM)
    cr, sr = angles(row_idx, HEAD_DIM // 2)
    cc, sc = angles(col_idx, HEAD_DIM // 2)
    cos1 = jnp.tile(jnp.concatenate([c1, c1], -1), (1, 2))
    sin1 = jnp.tile(jnp.concatenate([-s1, s1], -1), (1, 2))
    cos2 = jnp.tile(jnp.concatenate([cr, cr, cc, cc], -1), (1, 2))
    sin2 = jnp.tile(jnp.concatenate([-sr, sr, -sc, sc], -1), (1, 2))
    return cos1, sin1, cos2, sin2


def _lane_iota(rows):
    return lax.broadcasted_iota(jnp.int32, (rows, LANES), 1)


def _swap(x, dist, lane):
    return jnp.where((lane & dist) != 0, pltpu.roll(x, dist, 1), pltpu.roll(x, LANES - dist, 1))


def _head_ones():
    r = lax.broadcasted_iota(jnp.int32, (LANES, LANES), 0) // HEAD_DIM
    c = lax.broadcasted_iota(jnp.int32, (LANES, LANES), 1) // HEAD_DIM
    return (r == c).astype(BF16)


def _head_sum(t, ones):
    hi = t.astype(BF16)
    lo = (t - hi.astype(F32)).astype(BF16)
    return (jnp.dot(hi, ones, preferred_element_type=F32) + jnp.dot(lo, ones, preferred_element_type=F32))


Q_SCALE = HEAD_DIM ** -0.5
LOG2E = 1.4426950408889634
LN2 = 0.6931471805599453
CHUNK_KIND = ["qa"] * 4 + ["ka", "va"] + ["qb"] * 4 + ["kb", "vb"]
QA_COL, KA_COL, QB_COL, KB_COL = 0, 512, 768, 1280


def prep_fwd(proj, tabs, qn_g, kn_g, name):
    s_len, width = proj.shape
    ts = _row_tile(s_len, 512)
    cos1, sin1, cos2, sin2 = tabs

    def body(p_ref, c1_ref, s1_ref, c2_ref, s2_ref, qg_ref, kg_ref, o_ref, kv_ref):
        lane = _lane_iota(ts)
        ones = _head_ones()
        c1, s1, c2, s2 = c1_ref[...], s1_ref[...], c2_ref[...], s2_ref[...]
        n_kv = 0
        for cb, kind in enumerate(CHUNK_KIND):
            x = p_ref[:, cb * LANES:(cb + 1) * LANES]
            if kind in ("qa", "ka"):
                y = x * c1 + _swap(x, 32, lane) * s1
            elif kind in ("qb", "kb"):
                gain = qg_ref[...] if kind == "qb" else kg_ref[...]
                ms = _head_sum(x * x, ones) * (1.0 / HEAD_DIM)
                xn = (x * lax.rsqrt(ms + EPS)) * gain
                y = xn * c2 + _swap(xn, 16, lane) * s2
            else:
                y = x
            if kind in ("qa", "qb"):
                y = y * (Q_SCALE * LOG2E)
            else:
                kv_ref[:, n_kv * LANES:(n_kv + 1) * LANES] = y.astype(BF16)
                n_kv += 1
            o_ref[cb * LANES:(cb + 1) * LANES, :] = y.T.astype(BF16)

    tab = pl.BlockSpec((ts, LANES), lambda i: (i, 0))
    vec = pl.BlockSpec((1, LANES), lambda i: (0, 0))
    return pl.pallas_call(
        body, name=name, grid=(s_len // ts,),
        in_specs=[pl.BlockSpec((ts, width), lambda i: (i, 0)), tab, tab, tab, tab, vec, vec],
        out_specs=[pl.BlockSpec((width, ts), lambda i: (0, i)), pl.BlockSpec((ts, 4 * LANES), lambda i: (i, 0))],
        out_shape=[_sds((width, s_len), BF16), _sds((s_len, 4 * LANES), BF16)],
        compiler_params=_params("arbitrary"),
    )(proj, cos1, sin1, cos2, sin2, qn_g, kn_g)


def prep_bwd(proj, dqa, dka, dva, dqb, dkb, dvb, tabs, qn_g, kn_g, name):
    s_len, width = proj.shape
    ts = _row_tile(s_len, 256)
    cos1, sin1, cos2, sin2 = tabs

    def body(p_ref, dqa_ref, dka_ref, dva_ref, dqb_ref, dkb_ref, dvb_ref,
             c1_ref, s1_ref, c2_ref, s2_ref, qg_ref, kg_ref, o_ref, dqg_ref, dkg_ref):
        i = pl.program_id(0)
        lane = _lane_iota(ts)
        c1, s1, c2, s2 = c1_ref[...], s1_ref[...], c2_ref[...], s2_ref[...]

        def rope_t(dy, cos, sin, dist):
            return dy * cos + _swap(dy * sin, dist, lane)

        ones = _head_ones()

        def norm_bwd(dy, x, gain):
            r = lax.rsqrt(_head_sum(x * x, ones) * (1.0 / HEAD_DIM) + EPS)
            xhat = x * r
            dgain = jnp.sum(dy * xhat, axis=0, keepdims=True)
            dxh = dy * gain
            dx = r * (dxh - xhat * (_head_sum(dxh * xhat, ones) * (1.0 / HEAD_DIM)))
            return dx, dgain

        dqg = jnp.zeros((1, LANES), F32)
        dkg = jnp.zeros((1, LANES), F32)
        for cb, kind in enumerate(CHUNK_KIND):
            cols = slice(cb * LANES, (cb + 1) * LANES)
            if kind == "qa":
                dx = rope_t(dqa_ref[cols, :].T * Q_SCALE, c1, s1, 32)
            elif kind == "ka":
                dx = rope_t(dka_ref[...], c1, s1, 32)
            elif kind == "va":
                dx = dva_ref[...]
            elif kind == "qb":
                qcols = slice((cb - 6) * LANES, (cb - 5) * LANES)
                dy = rope_t(dqb_ref[qcols, :].T * Q_SCALE, c2, s2, 16)
                dx, dgain = norm_bwd(dy, p_ref[:, cols], qg_ref[...])
                dqg = dqg + dgain
            elif kind == "kb":
                dy = rope_t(dkb_ref[...], c2, s2, 16)
                dx, dgain = norm_bwd(dy, p_ref[:, cols], kg_ref[...])
                dkg = dkg + dgain
            else:
                dx = dvb_ref[...]
            o_ref[:, cols] = dx.astype(BF16)

        @pl.when(i == 0)
        def _():
            dqg_ref[...] = dqg
            dkg_ref[...] = dkg

        @pl.when(i > 0)
        def _():
            dqg_ref[...] += dqg
            dkg_ref[...] += dkg

    tab = pl.BlockSpec((ts, LANES), lambda i: (i, 0))
    vec = pl.BlockSpec((1, LANES), lambda i: (0, 0))

    def dq_spec(dq):
        per = dq.shape[2] // ts
        return pl.BlockSpec((None, 4 * LANES, ts), lambda i: (i // per, 0, i % per))

    return pl.pallas_call(
        body, name=name, grid=(s_len // ts,),
        in_specs=([pl.BlockSpec((ts, width), lambda i: (i, 0)), dq_spec(dqa), tab, tab, dq_spec(dqb), tab, tab]
                  + [tab] * 4 + [vec, vec]),
        out_specs=[pl.BlockSpec((ts, width), lambda i: (i, 0)), vec, vec],
        out_shape=[_sds((s_len, width), BF16), _sds((1, LANES), F32), _sds((1, LANES), F32)],
        compiler_params=_params("arbitrary"),
    )(proj, dqa, dka, dva, dqb, dkb, dvb, cos1, sin1, cos2, sin2, qn_g, kn_g)


NEG = -1e30
GROUP = 4
KV_HEADS = 2
GROUP_W = GROUP * HEAD_DIM
LSE_ROWS = 8
ONES_ROWS = 16


def _pos_mask_t(k_start, q_start, s_len, tk, tq):
    kpos = k_start + lax.broadcasted_iota(jnp.int32, (tk, tq), 0)
    qpos = q_start + lax.broadcasted_iota(jnp.int32, (tk, tq), 1)
    return (jnp.abs(kpos - qpos) <= BLOCK) & (kpos >= 0) & (kpos < s_len)


def flash_fwd_t(qkv_t, kv_tok, q_rb, k_i, v_rb, sink, window, name, comm=None):
    s_len = qkv_t.shape[1]
    if window:
        tq = _row_tile(s_len, 512)
        tk = 2 * BLOCK
        assert tq == 2 * tk, "the band parts below are written for query blocks of two key blocks"
        n_kv = 2
    else:
        tq, tk = _row_tile(s_len, 1024), _row_tile(s_len, 4096)
        n_kv = s_len // tk
    n_i = s_len // tq
    c_ins, c_outs, c_remote = comm if comm else ([], [], [])
    n_main = 6 if window else 3

    def body(*refs):
        main, c_in_refs = refs[:n_main], refs[n_main:n_main + len(c_ins)]
        rest = refs[n_main + len(c_ins):]
        (o_ref, lse_ref), c_out_refs = rest[:2], rest[2:2 + len(c_outs)]
        m_sc, acc_sc = rest[2 + len(c_outs):4 + len(c_outs)]
        c_sems = rest[4 + len(c_outs):]
        if window:
            sink_ref, q_ref, k_ref, v_ref, kc_ref, vc_ref = main
        else:
            q_ref, k_ref, v_ref = main
        h, i, t = pl.program_id(0), pl.program_id(1), pl.program_id(2)
        if comm:
            @pl.when((h == 0) & (i == 0) & (t == 0))
            def _():
                _exchange_start(c_remote, c_in_refs, c_out_refs, *c_sems)

        @pl.when(t == 0)
        def _():
            for g in range(GROUP):
                acc_sc[g, 0:HEAD_DIM, :] = jnp.zeros((HEAD_DIM, tq), F32)
                if window:
                    m_sc[g] = jnp.full((1, tq), sink_ref[h * GROUP + g] * LOG2E, F32)
                    acc_sc[g, HEAD_DIM:, :] = jnp.ones((ONES_ROWS, tq), F32)
                else:
                    m_sc[g] = jnp.full((1, tq), NEG, F32)
                    acc_sc[g, HEAD_DIM:, :] = jnp.zeros((ONES_ROWS, tq), F32)

        def tile(k_src, v_src, n_keys, key_pos, q_lo, q_hi):
            qs = slice(q_lo, q_hi)
            k = k_src[...]
            v_t = jnp.concatenate([v_src[...], jnp.ones((ONES_ROWS, n_keys), BF16)], axis=0)
            if window:
                mask = _pos_mask_t(key_pos, i * tq + q_lo, s_len, n_keys, q_hi - q_lo)
            s_next = jnp.dot(k, q_ref[0:HEAD_DIM, qs], preferred_element_type=F32)
            for g in range(GROUP):
                s_t = s_next
                if g + 1 < GROUP:
                    s_next = jnp.dot(k, q_ref[(g + 1) * HEAD_DIM:(g + 2) * HEAD_DIM, qs],
                                     preferred_element_type=F32)
                if window:
                    s_t = jnp.where(mask, s_t, NEG)
                m_prev = m_sc[g, :, qs]
                m_new = jnp.maximum(m_prev, jnp.max(s_t, axis=0, keepdims=True))
                alpha = jnp.exp2(m_prev - m_new)
                p_t = jnp.exp2(s_t - m_new)
                acc_sc[g, :, qs] = alpha * acc_sc[g, :, qs] + jnp.dot(v_t, p_t.astype(BF16),
                                                                     preferred_element_type=F32)
                m_sc[g, :, qs] = m_new

        if window:
            @pl.when(t == 0)
            def _():
                tile(k_ref, v_ref, tk, i * tq, 0, tq - BLOCK)
                tile(kc_ref, vc_ref, BLOCK, i * tq - BLOCK, 0, BLOCK)

            @pl.when(t == 1)
            def _():
                tile(k_ref, v_ref, tk, i * tq + tk, BLOCK, tq)
                tile(kc_ref, vc_ref, BLOCK, i * tq + tq, tq - BLOCK, tq)
        else:
            tile(k_ref, v_ref, tk, t * tk, 0, tq)

        @pl.when(t == n_kv - 1)
        def _():
            for g in range(GROUP):
                l = acc_sc[g, HEAD_DIM:HEAD_DIM + 1, :]
                o_ref[g * HEAD_DIM:(g + 1) * HEAD_DIM, :] = (acc_sc[g, 0:HEAD_DIM, :] / l).astype(BF16)
                lse_ref[g * LSE_ROWS:(g + 1) * LSE_ROWS, :] = jnp.broadcast_to(
                    m_sc[g] + jnp.log(l) * LOG2E, (LSE_ROWS, tq))

        if comm:
            @pl.when((h == KV_HEADS - 1) & (i == n_i - 1) & (t == n_kv - 1))
            def _():
                _exchange_finish(c_remote, c_in_refs, c_out_refs, *c_sems)

    kv_blk = (lambda i, t: 2 * i + t) if window else (lambda i, t: t)
    hbm = pl.BlockSpec(memory_space=pl.ANY)
    in_specs = [pl.BlockSpec((GROUP_W, tq), lambda h, i, t: (q_rb + h, i)),
                pl.BlockSpec((None, tk, HEAD_DIM), lambda h, i, t: (k_i + h, kv_blk(i, t), 0)),
                pl.BlockSpec((HEAD_DIM, tk), lambda h, i, t: (v_rb + h, kv_blk(i, t)))]
    args = [qkv_t, kv_tok, qkv_t]
    if window:
        corner = lambda i, t: jnp.clip((tq // BLOCK) * i - 1 + (tq // BLOCK + 1) * t, 0, s_len // BLOCK - 1)
        in_specs = ([pl.BlockSpec(memory_space=pltpu.SMEM)] + in_specs
                    + [pl.BlockSpec((None, BLOCK, HEAD_DIM), lambda h, i, t: (k_i + h, corner(i, t), 0)),
                       pl.BlockSpec((HEAD_DIM, BLOCK), lambda h, i, t: (v_rb + h, corner(i, t)))])
        args = [sink] + args + [kv_tok, qkv_t]
    return pl.pallas_call(
        body, name=name, grid=(KV_HEADS, n_i, n_kv),
        in_specs=in_specs + [hbm] * len(c_ins),
        out_specs=[pl.BlockSpec((GROUP_W, tq), lambda h, i, t: (h, i)),
                   pl.BlockSpec((GROUP * LSE_ROWS, tq), lambda h, i, t: (h, i))] + [hbm] * len(c_outs),
        out_shape=[_sds((KV_HEADS * GROUP_W, s_len), BF16),
                   _sds((KV_HEADS * GROUP * LSE_ROWS, s_len), F32)] + list(c_outs),
        scratch_shapes=[pltpu.VMEM((GROUP, 1, tq), F32),
                        pltpu.VMEM((GROUP, HEAD_DIM + ONES_ROWS, tq), F32)] + _exchange_sems(c_remote),
        compiler_params=_params("arbitrary", "arbitrary", "arbitrary"),
    )(*args, *c_ins)


def flash_bwd_t(qkv_t, kv_tok, o_t, do_t, lse, q_rb, k_i, v_i, k_rb, do_rb, sink, window, name, comm=None):
    s_len = qkv_t.shape[1]
    if window:
        tq = _row_tile(s_len, 512)
        tk = 2 * BLOCK
        assert tq == 2 * tk, "the band parts below are written for query blocks of two key blocks"
        n_q = 1
    else:
        tq, tk = _row_tile(s_len, 2048), _row_tile(s_len, 1024)
        n_q = s_len // tq
    n_qb = s_len // tq
    n_j = s_len // tk
    c_ins, c_outs, c_remote = comm if comm else ([], [], [])
    n_main = 12 if window else 7
    n_out = 4 if window else 3

    def body(*refs):
        main, c_in_refs = refs[:n_main], refs[n_main:n_main + len(c_ins)]
        rest = refs[n_main + len(c_ins):]
        outs, c_out_refs = rest[:n_out], rest[n_out:n_out + len(c_outs)]
        dk_sc, dv_sc = rest[n_out + len(c_outs):n_out + len(c_outs) + 2]
        c_sems = rest[n_out + len(c_outs) + 2:]
        if window:
            sink_ref, q_ref, k_ref, v_ref, kt_ref, o_ref, do_ref, lse_ref = main[:8]
            corner_src = main[8:]
            dq_ref, dk_ref, dv_ref, dsink_ref = outs
        else:
            q_ref, k_ref, v_ref, kt_ref, o_ref, do_ref, lse_ref = main
            dq_ref, dk_ref, dv_ref = outs
        main_src = (q_ref, o_ref, do_ref, lse_ref)
        h, j, t = pl.program_id(0), pl.program_id(1), pl.program_id(2)
        if comm:
            @pl.when((h == 0) & (j == 0) & (t == 0))
            def _():
                _exchange_start(c_remote, c_in_refs, c_out_refs, *c_sems)

        @pl.when((j == 0) & (t == 0))
        def _():
            dq_ref[...] = jnp.zeros(dq_ref.shape, F32)
            if window:
                dsink_ref[...] = jnp.zeros((8, LANES), F32)

        @pl.when(t == 0)
        def _():
            dk_sc[...] = jnp.zeros((tk, HEAD_DIM), F32)
            dv_sc[...] = jnp.zeros((tk, HEAD_DIM), F32)

        def tile(src, q_lo, q_hi, q_pos, k_lo, k_hi, dq_blk, dq_lo, sink_lo=0, sink_hi=0):
            q_src, o_src, do_src, lse_src = src
            ks, qs = slice(k_lo, k_hi), slice(q_lo, q_hi)
            dqs = slice(dq_lo, dq_lo + q_hi - q_lo)
            k, v, k_t = k_ref[ks, :], v_ref[ks, :], kt_ref[:, ks]
            if window:
                mask = _pos_mask_t(j * tk + k_lo, q_pos, s_len, k_hi - k_lo, q_hi - q_lo)
                lane = lax.broadcasted_iota(jnp.int32, (8, LANES), 1)
                sink_tile = jnp.zeros((8, LANES), F32)
            dk_acc = dk_sc[ks, :]
            dv_acc = dv_sc[ks, :]
            for g in range(GROUP):
                rows = slice(g * HEAD_DIM, (g + 1) * HEAD_DIM)
                q_t, o_g, do_g = q_src[rows, qs], o_src[rows, qs], do_src[rows, qs]
                s_t = jnp.dot(k, q_t, preferred_element_type=F32)
                if window:
                    s_t = jnp.where(mask, s_t, NEG)
                lse_row = lse_src[g * LSE_ROWS:g * LSE_ROWS + 1, qs]
                p_t = jnp.exp2(s_t - lse_row)
                delta = jnp.sum(do_g.astype(F32) * o_g.astype(F32), axis=0, keepdims=True)
                dp_t = jnp.dot(v, do_g, preferred_element_type=F32)
                ds_t = (p_t * (dp_t - delta)).astype(BF16)
                dv_acc = dv_acc + lax.dot_general(p_t.astype(BF16), do_g, NT_DIMS, preferred_element_type=F32)
                dk_acc = dk_acc + lax.dot_general(ds_t, q_t, NT_DIMS, preferred_element_type=F32)
                dq_ref[dq_blk, rows, dqs] += jnp.dot(k_t, ds_t, preferred_element_type=F32)
                if sink_hi > sink_lo:
                    at = slice(sink_lo - q_lo, sink_hi - q_lo)
                    p_sink = jnp.exp2(sink_ref[h * GROUP + g] * LOG2E - lse_row[:, at])
                    term = -jnp.sum(p_sink * delta[:, at], axis=1, keepdims=True)
                    sink_tile = jnp.where(lane == g, term, sink_tile)
            dk_sc[ks, :] = dk_acc
            dv_sc[ks, :] = dv_acc
            if sink_hi > sink_lo:
                dsink_ref[...] += sink_tile

        if window:
            m = j // 2

            @pl.when(j % 2 == 0)
            def _():
                tile(main_src, 0, tq - BLOCK, m * tq, 0, tk, m, 0, 0, tq - BLOCK)

            @pl.when((j % 2 == 0) & (j > 0))
            def _():
                tile(corner_src, 0, BLOCK, m * tq - BLOCK, 0, BLOCK, m - 1, tq - BLOCK)

            @pl.when(j % 2 == 1)
            def _():
                tile(main_src, BLOCK, tq, m * tq + BLOCK, 0, tk, m, BLOCK, tq - BLOCK, tq)

            @pl.when((j % 2 == 1) & (j < n_j - 1))
            def _():
                tile(corner_src, 0, BLOCK, (m + 1) * tq, tk - BLOCK, tk, m + 1, 0)
        else:
            tile(main_src, 0, tq, t * tq, 0, tk, t, 0)

        @pl.when(t == n_q - 1)
        def _():
            dk_ref[...] = dk_sc[...] * LN2
            dv_ref[...] = dv_sc[...]

        if comm:
            @pl.when((h == KV_HEADS - 1) & (j == n_j - 1) & (t == n_q - 1))
            def _():
                _exchange_finish(c_remote, c_in_refs, c_out_refs, *c_sems)

    qb = (lambda j, t: j // 2) if window else (lambda j, t: t)
    hbm = pl.BlockSpec(memory_space=pl.ANY)
    in_specs = [pl.BlockSpec((GROUP_W, tq), lambda h, j, t: (q_rb + h, qb(j, t))),
                pl.BlockSpec((None, tk, HEAD_DIM), lambda h, j, t: (k_i + h, j, 0)),
                pl.BlockSpec((None, tk, HEAD_DIM), lambda h, j, t: (v_i + h, j, 0)),
                pl.BlockSpec((HEAD_DIM, tk), lambda h, j, t: (k_rb + h, j)),
                pl.BlockSpec((GROUP_W, tq), lambda h, j, t: (h, qb(j, t))),
                pl.BlockSpec((GROUP_W, tq), lambda h, j, t: (do_rb + h, qb(j, t))),
                pl.BlockSpec((GROUP * LSE_ROWS, tq), lambda h, j, t: (h, qb(j, t)))]
    args = [qkv_t, kv_tok, kv_tok, qkv_t, o_t, do_t, lse]
    kv_out = _sds((KV_HEADS, s_len, HEAD_DIM), F32)
    out_specs = [pl.BlockSpec((n_qb, GROUP_W, tq), lambda h, j, t: (0, h, 0)),
                 pl.BlockSpec((None, tk, HEAD_DIM), lambda h, j, t: (h, j, 0)),
                 pl.BlockSpec((None, tk, HEAD_DIM), lambda h, j, t: (h, j, 0))]
    out_shape = [_sds((n_qb, KV_HEADS * GROUP_W, tq), F32), kv_out, kv_out]
    if window:
        cq = lambda j: jnp.clip(2 * j - 1 + 3 * (j % 2), 0, s_len // BLOCK - 1)
        in_specs = ([pl.BlockSpec(memory_space=pltpu.SMEM)] + in_specs
                    + [pl.BlockSpec((GROUP_W, BLOCK), lambda h, j, t: (q_rb + h, cq(j))),
                       pl.BlockSpec((GROUP_W, BLOCK), lambda h, j, t: (h, cq(j))),
                       pl.BlockSpec((GROUP_W, BLOCK), lambda h, j, t: (do_rb + h, cq(j))),
                       pl.BlockSpec((GROUP * LSE_ROWS, BLOCK), lambda h, j, t: (h, cq(j)))])
        args = [sink] + args + [qkv_t, o_t, do_t, lse]
        out_specs.append(pl.BlockSpec((None, 8, LANES), lambda h, j, t: (h, 0, 0)))
        out_shape.append(_sds((KV_HEADS, 8, LANES), F32))
    return pl.pallas_call(
        body, name=name, grid=(KV_HEADS, n_j, n_q),
        in_specs=in_specs + [hbm] * len(c_ins), out_specs=out_specs + [hbm] * len(c_outs),
        out_shape=out_shape + list(c_outs),
        scratch_shapes=[pltpu.VMEM((tk, HEAD_DIM), F32), pltpu.VMEM((tk, HEAD_DIM), F32)]
        + _exchange_sems(c_remote),
        compiler_params=_params("arbitrary", "arbitrary", "arbitrary"),
    )(*args, *c_ins)


SGU_GROUPS = 8
SGU_CHUNK = 128
GELU_C = float(np.sqrt(2.0 / np.pi))
GELU_A = 0.044715


def _gelu_and_grad(x):
    x2 = x * x
    t = jnp.tanh(x * (GELU_C + (GELU_C * GELU_A) * x2))
    hx = 0.5 * x
    return hx + hx * t, (0.5 + 0.5 * t) + (hx * (1.0 - t * t)) * (GELU_C + (3.0 * GELU_C * GELU_A) * x2)


def _gelu(x):
    t = jnp.tanh(x * (GELU_C + (GELU_C * GELU_A) * (x * x)))
    hx = 0.5 * x
    return hx + hx * t


def _layernorm_stats(v):
    mu = jnp.mean(v, axis=-1, keepdims=True)
    var = jnp.mean(jnp.square(v - mu), axis=-1, keepdims=True)
    rstd = lax.rsqrt(var + EPS)
    return (v - mu) * rstd, rstd


def sgu_mid_fwd(zpre, ln_g, ln_b, ws, bsb, name):
    s_len, width = zpre.shape
    d = width // 2
    ts = _row_tile(s_len, 256)

    def body(z_ref, g_ref, b_ref, ws_ref, bs_ref, y_ref):
        z = _gelu(z_ref[...])
        u, v = z[:, :d], z[:, d:]
        vhat, _ = _layernorm_stats(v)
        vn = (vhat * g_ref[...] + b_ref[...]).astype(BF16)
        for n in range(ts // SGU_CHUNK):
            rows = slice(n * SGU_CHUNK, (n + 1) * SGU_CHUNK)
            for g in range(SGU_GROUPS):
                cols = slice(g * LANES, (g + 1) * LANES)
                mixed = jnp.dot(ws_ref[g], vn[rows, cols], preferred_element_type=F32) + bs_ref[g]
                y_ref[rows, cols] = (u[rows, cols] * mixed).astype(BF16)

    vec = pl.BlockSpec((1, d), lambda i: (0, 0))
    cube = pl.BlockSpec((SGU_GROUPS, SGU_CHUNK, SGU_CHUNK), lambda i: (0, 0, 0))
    return pl.pallas_call(
        body, name=name, grid=(s_len // ts,),
        in_specs=[pl.BlockSpec((ts, width), lambda i: (i, 0)), vec, vec, cube, cube],
        out_specs=pl.BlockSpec((ts, d), lambda i: (i, 0)),
        out_shape=_sds((s_len, d), BF16),
        compiler_params=_params("arbitrary"),
    )(zpre, ln_g, ln_b, ws, bsb)


def sgu_mid_bwd(zpre, dy, ln_g, ln_b, ws, wst, bsb, name):
    s_len, width = zpre.shape
    d = width // 2
    ts = _row_tile(s_len, 256)
    n_steps = s_len // ts

    def body(z_ref, dy_ref, g_ref, b_ref, ws_ref, wst_ref, bs_ref,
             dz_ref, dws_ref, dbs_ref, dg_ref, db_ref, du_sc, dvn_sc):
        i = pl.program_id(0)

        @pl.when(i == 0)
        def _():
            dws_ref[...] = jnp.zeros(dws_ref.shape, F32)
            dbs_ref[...] = jnp.zeros(dbs_ref.shape, F32)
            dg_ref[...] = jnp.zeros(dg_ref.shape, F32)
            db_ref[...] = jnp.zeros(db_ref.shape, F32)

        zp = z_ref[...]
        z, gp = _gelu_and_grad(zp)
        u, v = z[:, :d], z[:, d:]
        vhat, rstd = _layernorm_stats(v)
        gain = g_ref[...]
        vn = (vhat * gain + b_ref[...]).astype(BF16)
        dyf = dy_ref[...].astype(F32)
        for n in range(ts // SGU_CHUNK):
            rows = slice(n * SGU_CHUNK, (n + 1) * SGU_CHUNK)
            for g in range(SGU_GROUPS):
                cols = slice(g * LANES, (g + 1) * LANES)
                vt = vn[rows, cols]
                mixed = jnp.dot(ws_ref[g], vt, preferred_element_type=F32) + bs_ref[g]
                dyt = dyf[rows, cols]
                du_sc[rows, cols] = dyt * mixed
                dmixed = dyt * u[rows, cols]
                dmb = dmixed.astype(BF16)
                dvn_sc[rows, cols] = jnp.dot(wst_ref[g], dmb, preferred_element_type=F32)
                dws_ref[g] += lax.dot_general(dmb, vt, NT_DIMS, preferred_element_type=F32)
                dbs_ref[g] += dmixed
        dvn = dvn_sc[...]
        dg_ref[...] += jnp.sum(dvn * vhat, axis=0, keepdims=True)
        db_ref[...] += jnp.sum(dvn, axis=0, keepdims=True)
        dvh = dvn * gain
        dv = rstd * (dvh - jnp.mean(dvh, axis=-1, keepdims=True)
                     - vhat * jnp.mean(dvh * vhat, axis=-1, keepdims=True))
        dz_ref[:, :d] = (du_sc[...] * gp[:, :d]).astype(BF16)
        dz_ref[:, d:] = (dv * gp[:, d:]).astype(BF16)

        @pl.when(i == n_steps - 1)
        def _():
            for g in range(SGU_GROUPS):
                tot = jnp.sum(dbs_ref[g], axis=1, keepdims=True)
                dbs_ref[g] = jnp.broadcast_to(tot, (SGU_CHUNK, LANES))

    vec = pl.BlockSpec((1, d), lambda i: (0, 0))
    cube = pl.BlockSpec((SGU_GROUPS, SGU_CHUNK, SGU_CHUNK), lambda i: (0, 0, 0))
    cube_shape = _sds((SGU_GROUPS, SGU_CHUNK, SGU_CHUNK), F32)
    return pl.pallas_call(
        body, name=name, grid=(n_steps,),
        in_specs=[pl.BlockSpec((ts, width), lambda i: (i, 0)), pl.BlockSpec((ts, d), lambda i: (i, 0)),
                  vec, vec, cube, cube, cube],
        out_specs=[pl.BlockSpec((ts, width), lambda i: (i, 0)), cube, cube, vec, vec],
        out_shape=[_sds((s_len, width), BF16), cube_shape, cube_shape, _sds((1, d), F32), _sds((1, d), F32)],
        scratch_shapes=[pltpu.VMEM((ts, d), F32), pltpu.VMEM((ts, d), F32)],
        compiler_params=_params("arbitrary"),
    )(zpre, dy, ln_g, ln_b, ws, wst, bsb)


def loss_head(x, g, target, name):
    s_len, d = x.shape
    tm = _row_tile(s_len, 512)

    def body(x_ref, g_ref, t_ref, dx_ref, dxb_ref, dg_ref, loss_ref):
        i = pl.program_id(0)
        xf = x_ref[...]
        gain = g_ref[...]
        r = lax.rsqrt(jnp.mean(xf * xf, axis=-1, keepdims=True) + EPS)
        xhat = xf * r
        err = xhat * gain - t_ref[...]
        row = jnp.mean(err * err, axis=-1, keepdims=True)
        part = 0.5 * jnp.sum(row, axis=0, keepdims=True)
        dy = err * (1.0 / d)
        dg_part = jnp.sum(dy * xhat, axis=0, keepdims=True)

        @pl.when(i == 0)
        def _():
            dg_ref[...] = dg_part
            loss_ref[...] = jnp.broadcast_to(part, (8, LANES))

        @pl.when(i > 0)
        def _():
            dg_ref[...] += dg_part
            loss_ref[...] += jnp.broadcast_to(part, (8, LANES))

        dxh = dy * gain
        dx = r * (dxh - xhat * jnp.mean(dxh * xhat, axis=-1, keepdims=True))
        dx_ref[...] = dx
        dxb_ref[...] = dx.astype(BF16)

    row_spec = pl.BlockSpec((tm, d), lambda i: (i, 0))
    vec = pl.BlockSpec((1, d), lambda i: (0, 0))
    return pl.pallas_call(
        body, name=name, grid=(s_len // tm,),
        in_specs=[row_spec, vec, row_spec],
        out_specs=[row_spec, row_spec, vec, pl.BlockSpec((8, LANES), lambda i: (0, 0))],
        out_shape=[_sds((s_len, d), F32), _sds((s_len, d), BF16), _sds((1, d), F32), _sds((8, LANES), F32)],
        compiler_params=_params("arbitrary"),
    )(x, g, target)


FLIP_BITS = {"c": (0, 0, 1), "x": (1, 0, 0), "y": (0, 1, 0), "xy": (1, 1, 0),
             "xc": (1, 0, 1), "yc": (0, 1, 1), "xyc": (1, 1, 1)}
CHIP_FLIPS = ("x", "y", "xy")


def _flip(pos, name):
    return tuple(1 - p if bit else p for p, bit in zip(pos, FLIP_BITS[name]))


def _chip(pos):
    return 2 * pos[0] + pos[1]


def _me():
    return (lax.axis_index("x"), lax.axis_index("y"), lax.axis_index("c"))


def _exchange_copy(remote, k, in_refs, out_refs, send_sems, recv_sems, sender, receiver):
    ii, src_fn, oi, dst_fn, _ = remote[k]
    return pltpu.make_async_remote_copy(
        src_ref=src_fn(in_refs[ii], sender, receiver), dst_ref=dst_fn(out_refs[oi], sender),
        send_sem=send_sems.at[k], recv_sem=recv_sems.at[k], device_id=receiver, device_id_type=MESH)


def _exchange_start(remote, in_refs, out_refs, send_sems, recv_sems):
    me = _me()
    for k in range(len(remote)):
        _exchange_copy(remote, k, in_refs, out_refs, send_sems, recv_sems, me, _flip(me, remote[k][4])).start()


def _exchange_finish(remote, in_refs, out_refs, send_sems, recv_sems):
    me = _me()
    for k in range(len(remote)):
        _exchange_copy(remote, k, in_refs, out_refs, send_sems, recv_sems, _flip(me, remote[k][4]), me).wait_recv()
    for k in range(len(remote)):
        _exchange_copy(remote, k, in_refs, out_refs, send_sems, recv_sems, me, _flip(me, remote[k][4])).wait_send()


def _exchange_sems(remote):
    n = len(remote)
    return [pltpu.SemaphoreType.DMA((n,)), pltpu.SemaphoreType.DMA((n,))] if n else []


def exchange(name, ins, out_shapes, remote, local):
    n_in, n_out = len(ins), len(out_shapes)

    def body(*refs):
        in_refs, out_refs = refs[:n_in], refs[n_in:n_in + n_out]
        send_sems, recv_sems, local_sems = refs[n_in + n_out:]
        me = _me()
        stays = []
        for k, (ii, src_fn, oi, dst_fn) in enumerate(local):
            cp = pltpu.make_async_copy(src_fn(in_refs[ii], me), dst_fn(out_refs[oi], me), local_sems.at[k])
            cp.start()
            stays.append(cp)
        _exchange_start(remote, in_refs, out_refs, send_sems, recv_sems)
        _exchange_finish(remote, in_refs, out_refs, send_sems, recv_sems)
        for cp in stays:
            cp.wait()

    hbm = pl.BlockSpec(memory_space=pl.ANY)
    return pl.pallas_call(
        body, name=name,
        in_specs=[hbm] * n_in, out_specs=[hbm] * n_out, out_shape=list(out_shapes),
        scratch_shapes=[pltpu.SemaphoreType.DMA((max(len(remote), 1),)),
                        pltpu.SemaphoreType.DMA((max(len(remote), 1),)),
                        pltpu.SemaphoreType.DMA((max(len(local), 1),))],
        compiler_params=pltpu.CompilerParams(has_side_effects=True),
    )(*ins)


def staged_push(name, ins, out_shapes, jobs, n_alias=0):
    n_in, n_out = len(ins), len(out_shapes)
    n_copies = sum(len(dsts) for _, _, dsts in jobs)
    n_remote = sum(1 for _, _, dsts in jobs for d in dsts if d[2] is not None)

    def chunk_of(ii, src_fn):
        probe = _ShapeRef(ins[ii].shape, ins[ii].dtype)
        got = src_fn(probe, (0, 0, 0))
        return tuple(got.shape), got.dtype

    classes = []
    for ii, src_fn, _ in jobs:
        c = chunk_of(ii, src_fn)
        if c not in classes:
            classes.append(c)

    def body(*refs):
        in_refs, out_refs = refs[:n_in], refs[n_in:n_in + n_out]
        bufs = refs[n_in + n_out:n_in + n_out + len(classes)]
        load_sems, out_sems, recv_sems = refs[n_in + n_out + len(classes):]
        me = _me()
        pending = [[[], []] for _ in classes]
        used = [0] * len(classes)
        arrivals = []
        k = r = 0

        def begin_load(job):
            ii, src_fn, _ = job
            cls = classes.index(chunk_of(ii, src_fn))
            slot = used[cls] % 2
            used[cls] += 1
            for kind, cp in pending[cls][slot]:
                cp.wait_send() if kind == "remote" else cp.wait()
            pending[cls][slot] = []
            load = pltpu.make_async_copy(src_fn(in_refs[ii], me), bufs[cls].at[slot], load_sems.at[2 * cls + slot])
            load.start()
            return load, cls, slot

        nxt = begin_load(jobs[0])
        for n, (ii, src_fn, dsts) in enumerate(jobs):
            load, cls, slot = nxt
            load.wait()
            buf = bufs[cls].at[slot]
            sent = []
            for oi, dst_fn, flip in dsts:
                if flip is None:
                    cp = pltpu.make_async_copy(buf, dst_fn(out_refs[oi], me), out_sems.at[k])
                    cp.start()
                    sent.append(("local", cp))
                else:
                    peer = _flip(me, flip)
                    cp = pltpu.make_async_remote_copy(
                        src_ref=buf, dst_ref=dst_fn(out_refs[oi], me), send_sem=out_sems.at[k],
                        recv_sem=recv_sems.at[r], device_id=peer, device_id_type=MESH)
                    cp.start()
                    sent.append(("remote", cp))
                    arrivals.append((r, cls, oi, dst_fn, peer))
                    r += 1
                k += 1
            pending[cls][slot] = sent
            if n + 1 < len(jobs):
                nxt = begin_load(jobs[n + 1])
        for per_class in pending:
            for slot_list in per_class:
                for kind, cp in slot_list:
                    cp.wait_send() if kind == "remote" else cp.wait()
        for r, cls, oi, dst_fn, peer in arrivals:
            pltpu.make_async_remote_copy(
                src_ref=bufs[cls].at[0], dst_ref=dst_fn(out_refs[oi], peer), send_sem=out_sems.at[0],
                recv_sem=recv_sems.at[r], device_id=peer, device_id_type=MESH).wait_recv()

    hbm = pl.BlockSpec(memory_space=pl.ANY)
    return pl.pallas_call(
        body, name=name,
        in_specs=[hbm] * n_in, out_specs=[hbm] * n_out, out_shape=list(out_shapes),
        scratch_shapes=[pltpu.VMEM((2,) + shape, dtype) for shape, dtype in classes]
        + [pltpu.SemaphoreType.DMA((2 * len(classes),)), pltpu.SemaphoreType.DMA((max(n_copies, 1),)),
           pltpu.SemaphoreType.DMA((max(n_remote, 1),))],
        input_output_aliases={i: i for i in range(n_alias)},
        compiler_params=pltpu.CompilerParams(has_side_effects=True, vmem_limit_bytes=VMEM_LIMIT),
    )(*ins)


class _ShapeRef:
    def __init__(self, shape, dtype):
        self.shape, self.dtype = tuple(shape), dtype

    @property
    def at(self):
        return self

    def __getitem__(self, idx):
        idx = idx if isinstance(idx, tuple) else (idx,)
        shape = []
        for dim, i in zip(self.shape, idx):
            if isinstance(i, slice):
                shape.append(len(range(*i.indices(dim))))
            elif hasattr(i, "size") and hasattr(i, "start"):
                shape.append(i.size)
        shape += self.shape[len(idx):]
        return _ShapeRef(shape, self.dtype)


def gather_whole(shards, name):
    whole = lambda ref, sender, receiver=None: ref
    slot = lambda ref, sender: ref.at[_chip(sender)]
    remote = [(t, whole, t, slot, flip) for t in range(len(shards)) for flip in CHIP_FLIPS]
    local = [(t, whole, t, slot) for t in range(len(shards))]
    outs = [_sds((N_CHIPS,) + a.shape, a.dtype) for a in shards]
    return exchange(name, list(shards), outs, remote, local)


def _half_axis(shape):
    return 0 if shape[0] >= 2 else 1


def gather_halves_plan(shards):
    remote = []
    for t, a in enumerate(shards):
        ax = _half_axis(a.shape)
        half = lambda ref, sender, receiver=None, ax=ax: _half(ref, sender[2], ax)
        slot = lambda ref, sender, ax=ax: _half(ref.at[_chip(sender)], sender[2], ax)
        remote += [(t, half, t, slot, flip) for flip in CHIP_FLIPS]
    outs = [_sds((N_CHIPS,) + a.shape, a.dtype) for a in shards]
    return list(shards), outs, remote


def gather_halves_fill(got, shards, name):
    n_t = len(shards)
    jobs = []
    for t, a in enumerate(shards):
        layers = a.shape[0]
        for l in range(layers):
            jobs.append((n_t + t, lambda ref, me, l=l: ref.at[l],
                         [(t, lambda ref, sender, l=l: ref.at[_chip(sender), l], None)]))
        for flip in CHIP_FLIPS:
            if _half_axis(a.shape) == 0:
                n = layers // 2
                for j in range(n):
                    at = lambda ref, pos, flip=flip, j=j, n=n: ref.at[_chip(_flip(pos, flip)), pos[2] * n + j]
                    jobs.append((t, at, [(t, at, "c")]))
            else:
                rows = a.shape[1] // 2
                at = lambda ref, pos, flip=flip, rows=rows: ref.at[
                    _chip(_flip(pos, flip)), 0, pl.ds(pos[2] * rows, rows)]
                jobs.append((t, at, [(t, at, "c")]))
    outs = [_sds(g.shape, g.dtype) for g in got]
    return staged_push(name, list(got) + list(shards), outs, jobs, n_alias=n_t)


def add_to_all(plan, buf):
    ins, outs, remote = plan
    whole = lambda ref, sender, receiver=None: ref
    more = [(len(ins), whole, len(outs), (lambda ref, sender, f=f: ref.at[f]), flip)
            for f, flip in enumerate(FLIPS_BY_INDEX)]
    return list(ins) + [buf], list(outs) + [_sds((len(more),) + buf.shape, buf.dtype)], list(remote) + more


FLIPS_BY_INDEX = ("c", "y", "yc", "x", "xc", "xy", "xyc")


def sum_devices(own, got, name):
    rows = own.shape[0]
    tr = LANES if rows % LANES == 0 else rows
    me = (4 * lax.axis_index("x") + 2 * lax.axis_index("y") + lax.axis_index("c")).astype(jnp.int32).reshape(1)
    everyone = jnp.concatenate([own[None], got], axis=0)

    def body(me_ref, a0, a1, a2, a3, a4, a5, a6, a7, o_ref):
        o_ref[...] = ((a0[...] + a1[...]) + (a2[...] + a3[...])) + ((a4[...] + a5[...]) + (a6[...] + a7[...]))

    return pl.pallas_call(
        body, name=name,
        grid_spec=pltpu.PrefetchScalarGridSpec(
            num_scalar_prefetch=1, grid=(rows // tr,),
            in_specs=[pl.BlockSpec((None, tr, LANES), lambda i, m, k=k: (m[0] ^ k, i, 0)) for k in range(8)],
            out_specs=pl.BlockSpec((tr, LANES), lambda i, m: (i, 0))),
        out_shape=_sds((rows, LANES), F32),
        compiler_params=_params("arbitrary"),
    )(me, *([everyone] * 8))


def _half(ref, core, axis):
    rows = ref.shape[axis] // 2
    idx = (slice(None),) * axis + (pl.ds(core * rows, rows),)
    return ref.at[idx]


def reduce_plan(grads):
    remote, outs = [], []
    for t, g in enumerate(grads):
        outs.append(_sds((len(FLIP_BITS), g.shape[1] // 2, g.shape[2]), BF16))
        for f, flip in enumerate(FLIP_BITS):
            remote.append((t, lambda ref, sender, receiver: _half(ref.at[_chip(receiver)], receiver[2], 0),
                           t, lambda ref, sender, f=f: ref.at[f], flip))
    return list(grads), outs, remote


def reduce_finish(grads, got, stacks, full_shapes, into, name):
    chip = (2 * lax.axis_index("x") + lax.axis_index("y")).astype(jnp.int32).reshape(1)
    core = lax.axis_index("c").astype(jnp.int32).reshape(1)
    totals = [sum_partials(g, r, chip, core, f"{name}_sum{t}") for t, (g, r) in enumerate(zip(grads, got))]
    names = []
    for out_name, _ in stacks:
        if out_name not in names:
            names.append(out_name)
    names = [n for n in names if n in into] + [n for n in names if n not in into]
    kept = [into[n] for n in names if n in into]
    outs = [_sds(full_shapes[n], F32) for n in names]
    jobs = []
    for t, (out_name, layer) in enumerate(stacks):
        oi = names.index(out_name)
        rows, cols = totals[t].shape
        pieces = max(1, rows * cols * 4 // STAGE_BYTES)
        step = rows // pieces
        for q in range(pieces):
            src = lambda ref, me, q=q, step=step: ref.at[pl.ds(q * step, step)]
            place = lambda ref, sender, layer=layer, q=q, step=step, rows=rows: ref.at[
                layer, pl.ds(sender[2] * rows + q * step, step)]
            jobs.append((len(kept) + t, src, [(oi, place, None), (oi, place, "c")]))
    full = staged_push(name + "_share", kept + totals, outs, jobs, n_alias=len(kept))
    return {**into, **dict(zip(names, full))}


STAGE_BYTES = 2 * 1024 * 1024


def sum_partials(mine, theirs, chip, core, name):
    _, rows, cols = mine.shape
    half = rows // 2
    tr = _row_tile(half, 256)
    nb = half // tr
    n_got = theirs.shape[0]

    def body(chip_ref, core_ref, m_ref, *refs):
        acc = m_ref[...].astype(F32)
        for r in refs[:n_got]:
            acc = acc + r[...].astype(F32)
        refs[n_got][...] = acc

    got = lambda f: pl.BlockSpec((None, tr, cols), lambda i, ch, co: (f, i, 0))
    return pl.pallas_call(
        body, name=name,
        grid_spec=pltpu.PrefetchScalarGridSpec(
            num_scalar_prefetch=2, grid=(nb,),
            in_specs=[pl.BlockSpec((None, tr, cols), lambda i, ch, co: (ch[0], co[0] * nb + i, 0))]
            + [got(f) for f in range(n_got)],
            out_specs=pl.BlockSpec((tr, cols), lambda i, ch, co: (i, 0))),
        out_shape=_sds((half, cols), F32),
        compiler_params=_params("arbitrary"),
    )(chip, core, mine, *([theirs] * n_got))


def _tok(t):
    return t.transpose(1, 0, 2).reshape(t.shape[1], t.shape[0] * t.shape[2])


def _heads(t):
    return t.reshape(t.shape[0], t.shape[1] // HEAD_DIM, HEAD_DIM).transpose(1, 0, 2)


def _tile2(vec):
    return jnp.tile(vec.reshape(1, HEAD_DIM), (1, 2))


REST = ("att_w_in", "att_w_out", "sgu_w_in", "sgu_w_out", "mlp_w1", "mlp_w2")
LAST_GROUP = (("att_w_in", 0),)
LATE_SMALL = ("att_norm", "att_sink", "att_qnorm", "att_knorm")


def local_step(x, target, first, rest_shards, rep, full_shapes):
    s_len, d = x.shape
    tabs = rope_tables(s_len)
    depth = rep["mlp_norm"].shape[0]
    row = lambda a: a.reshape(1, -1)
    saved = []
    h = x
    gw = {"att_w_in": [first]}

    def wl(name, idx):
        return (gw[name][idx], 0) if name == "att_w_in" else (gw[name], idx)

    for layer in range(depth):
        i = layer // 2
        tag = f"l{layer}"
        if layer % 2 == 0:
            hn, proj = norm_mm(h, row(rep["att_norm"][i]), *wl("att_w_in", i), F32, tag + "_att_proj")
            qkv_t, kv = prep_fwd(proj, tabs, _tile2(rep["att_qnorm"][i]), _tile2(rep["att_knorm"][i]),
                                 tag + "_att_prep")
            kv_tok = _heads(kv)
            oa, lse_a = flash_fwd_t(qkv_t, kv_tok, QA_COL // GROUP_W, 0, (KA_COL + LANES) // HEAD_DIM,
                                    rep["att_sink"][i], True, tag + "_win_fwd")
            plan = gather_halves_plan(rest_shards) if layer == 0 else None
            ob, lse_b, *got = flash_fwd_t(qkv_t, kv_tok, QB_COL // GROUP_W, 4, (KB_COL + LANES) // HEAD_DIM,
                                          None, False, tag + "_grid_fwd", comm=plan)
            if layer == 0:
                rest = dict(zip(REST, gather_halves_fill(got, rest_shards, "gather_rest_fill")))
                gw["att_w_in"].append(rest.pop("att_w_in"))
                gw.update(rest)
            out = mm_res_t([oa, ob], *wl("att_w_out", i), h, tag + "_att_out")
            mix_saved = (h, hn, proj, qkv_t, kv_tok, oa, ob, lse_a, lse_b)
        else:
            hn, zpre = norm_mm(h, row(rep["sgu_norm"][i]), *wl("sgu_w_in", i), F32, tag + "_sgu_in")
            ws = rep["sgu_w_s"][i].astype(BF16)
            bsb = jnp.broadcast_to(rep["sgu_b_s"][i][:, :, None], (SGU_GROUPS, SGU_CHUNK, LANES))
            y = sgu_mid_fwd(zpre, row(rep["sgu_ln_g"][i]), row(rep["sgu_ln_b"][i]), ws, bsb, tag + "_sgu_mid")
            out = mm_res(y, *wl("sgu_w_out", i), h, tag + "_sgu_out")
            mix_saved = (h, hn, zpre, y, ws, bsb)
        hm, a = norm_mm(out, row(rep["mlp_norm"][layer]), *wl("mlp_w1", layer), BF16, tag + "_mlp_up")
        nxt = mm_res(a, *wl("mlp_w2", layer), out, tag + "_mlp_down", relu2=True)
        saved.append((mix_saved, (out, hm, a)))
        h = nxt
    dh, dhb, d_final, loss_tile = loss_head(h, row(rep["final_norm"]), target, "loss_head")
    big, tags = [], []
    small = {k: [jnp.zeros(v.shape[1:], F32)] * v.shape[0] for k, v in rep.items() if k != "final_norm"}
    small["final_norm"] = d_final.reshape(-1)
    stacked = lambda: [small[n] if n == "final_norm" else jnp.stack(small[n]) for n in SMALL]
    for layer in reversed(range(depth)):
        i = layer // 2
        tag = f"l{layer}"
        mix_saved, (xin, hm, a) = saved[layer]
        da = mm_nt_relu2_bwd(dhb, *wl("mlp_w2", layer), a, tag + "_mlp_down_bwd")
        big.append(dw_mm(a, dhb, tag + "_mlp_dw2", col_sharded=False, relu2=True))
        tags.append(("mlp_w2", layer))
        big.append(dw_mm(hm, da, tag + "_mlp_dw1", col_sharded=True))
        tags.append(("mlp_w1", layer))
        dh, dhb, dg = dx_norm(da, *wl("mlp_w1", layer), xin, row(rep["mlp_norm"][layer]), dh, tag + "_mlp_up_bwd")
        small["mlp_norm"][layer] = dg.reshape(-1)
        if layer % 2 == 0:
            xin, hn, proj, qkv_t, kv_tok, oa, ob, lse_a, lse_b = mix_saved
            do_t = mm_nt(dhb, *wl("att_w_out", i), tag + "_att_out_bwd", transposed=True)
            big.append(dw_nn([oa, ob], dhb, tag + "_att_dwout"))
            tags.append(("att_w_out", i))
            dqa, dka, dva, dsink = flash_bwd_t(qkv_t, kv_tok, oa, do_t, lse_a, QA_COL // GROUP_W, 0, 2,
                                               KA_COL // HEAD_DIM, 0, rep["att_sink"][i], True, tag + "_win_bwd")
            plan = None
            if layer == 0:
                early = [k for k, t in enumerate(tags) if t not in LAST_GROUP]
                early_pack = _pack(stacked() + [loss_tile[0, :1]])
                plan = add_to_all(reduce_plan([big[k] for k in early]), early_pack)
            dqb, dkb, dvb, *got = flash_bwd_t(qkv_t, kv_tok, ob, do_t, lse_b, QB_COL // GROUP_W, 4, 6,
                                              KB_COL // HEAD_DIM, 2, None, False, tag + "_grid_bwd", comm=plan)
            if layer == 0:
                grads = reduce_finish([big[k] for k in early], got[:-1], [tags[k] for k in early], full_shapes,
                                      {}, "grads1")
                early_sum = sum_devices(early_pack, got[-1], "sum_small")
            qg, kg = _tile2(rep["att_qnorm"][i]), _tile2(rep["att_knorm"][i])
            dproj, dqg, dkg = prep_bwd(proj, dqa, _tok(dka), _tok(dva), dqb, _tok(dkb), _tok(dvb),
                                       tabs, qg, kg, tag + "_att_prep_bwd")
            big.append(dw_mm(hn, dproj, tag + "_att_dwin", col_sharded=True))
            tags.append(("att_w_in", i))
            dh, dhb, dg = dx_norm(dproj, *wl("att_w_in", i), xin, row(rep["att_norm"][i]), dh, tag + "_att_proj_bwd")
            small["att_norm"][i] = dg.reshape(-1)
            small["att_sink"][i] = dsink[:, 0, :GROUP].reshape(-1)
            small["att_qnorm"][i] = dqg[0, :HEAD_DIM] + dqg[0, HEAD_DIM:]
            small["att_knorm"][i] = dkg[0, :HEAD_DIM] + dkg[0, HEAD_DIM:]
        else:
            xin, hn, zpre, y, ws, bsb = mix_saved
            dy = mm_nt(dhb, *wl("sgu_w_out", i), tag + "_sgu_out_bwd")
            big.append(dw_mm(y, dhb, tag + "_sgu_dwout", col_sharded=False))
            tags.append(("sgu_w_out", i))
            wst = ws.transpose(0, 2, 1)
            dz, dws, dbs, dlg, dlb = sgu_mid_bwd(zpre, dy, row(rep["sgu_ln_g"][i]), row(rep["sgu_ln_b"][i]),
                                                 ws, wst, bsb, tag + "_sgu_mid_bwd")
            big.append(dw_mm(hn, dz, tag + "_sgu_dwin", col_sharded=True))
            tags.append(("sgu_w_in", i))
            dh, dhb, dg = dx_norm(dz, *wl("sgu_w_in", i), xin, row(rep["sgu_norm"][i]), dh, tag + "_sgu_in_bwd")
            small["sgu_norm"][i] = dg.reshape(-1)
            small["sgu_ln_g"][i] = dlg.reshape(-1)
            small["sgu_ln_b"][i] = dlb.reshape(-1)
            small["sgu_w_s"][i] = dws
            small["sgu_b_s"][i] = dbs[:, :, 0]
    late = [k for k, t in enumerate(tags) if t in LAST_GROUP]
    late_pack = _pack([small[n][0] for n in LATE_SMALL])
    got = exchange("grads2_scatter", *add_to_all(reduce_plan([big[k] for k in late]), late_pack), [])
    grads = reduce_finish([big[k] for k in late], got[:-1], [tags[k] for k in late], full_shapes, grads, "grads2")
    late_sum = sum_devices(late_pack, got[-1], "sum_small_late")
    shapes = [a.shape for a in stacked()]
    *small_g, loss = _unpack(early_sum, shapes + [()])
    small_g = dict(zip(SMALL, small_g))
    for n, g in zip(LATE_SMALL, _unpack(late_sum, [small[n][0].shape for n in LATE_SMALL])):
        small_g[n] = small_g[n].at[0].add(g)
    return loss, dh, grads, small_g


BIG = ("att_w_in", "att_w_out", "sgu_w_in", "sgu_w_out", "mlp_w1", "mlp_w2")
SHARDED_VEC = ("sgu_norm", "sgu_ln_g", "sgu_ln_b")
REPLICATED = ("att_norm", "att_sink", "att_qnorm", "att_knorm", "sgu_w_s", "sgu_b_s", "mlp_norm", "final_norm")
WEIGHTS = ("att_norm", "att_w_in", "att_sink", "att_qnorm", "att_knorm", "att_w_out", "sgu_norm", "sgu_w_in",
           "sgu_ln_g", "sgu_ln_b", "sgu_w_s", "sgu_b_s", "sgu_w_out", "mlp_norm", "mlp_w1", "mlp_w2", "final_norm")
SMALL = tuple(n for n in WEIGHTS if n not in BIG)
PACK_ALIGN = 8 * LANES


def _pack(arrays):
    flat = jnp.concatenate([a.reshape(-1) for a in arrays])
    pad = -flat.shape[0] % PACK_ALIGN
    return jnp.pad(flat, (0, pad)).reshape(-1, LANES)


def _unpack(flat2d, shapes):
    flat = flat2d.reshape(-1)
    out, off = [], 0
    for shape in shapes:
        size = int(np.prod(shape))
        out.append(flat[off:off + size].reshape(shape))
        off += size
    return out


def kernel(x, att_norm, att_w_in, att_sink, att_qnorm, att_knorm, att_w_out, sgu_norm, sgu_w_in, sgu_ln_g, sgu_ln_b, sgu_w_s, sgu_b_s, sgu_w_out, mlp_norm, mlp_w1, mlp_w2, final_norm, loss_target, m_att_norm, m_att_w_in, m_att_sink, m_att_qnorm, m_att_knorm, m_att_w_out, m_sgu_norm, m_sgu_w_in, m_sgu_ln_g, m_sgu_ln_b, m_sgu_w_s, m_sgu_b_s, m_sgu_w_out, m_mlp_norm, m_mlp_w1, m_mlp_w2, m_final_norm, v_att_norm, v_att_w_in, v_att_sink, v_att_qnorm, v_att_knorm, v_att_w_out, v_sgu_norm, v_sgu_w_in, v_sgu_ln_g, v_sgu_ln_b, v_sgu_w_s, v_sgu_b_s, v_sgu_w_out, v_mlp_norm, v_mlp_w1, v_mlp_w2, v_final_norm):
    w = dict(att_norm=att_norm, att_w_in=att_w_in, att_sink=att_sink, att_qnorm=att_qnorm, att_knorm=att_knorm,
             att_w_out=att_w_out, sgu_norm=sgu_norm, sgu_w_in=sgu_w_in, sgu_ln_g=sgu_ln_g, sgu_ln_b=sgu_ln_b,
             sgu_w_s=sgu_w_s, sgu_b_s=sgu_b_s, sgu_w_out=sgu_w_out, mlp_norm=mlp_norm, mlp_w1=mlp_w1,
             mlp_w2=mlp_w2, final_norm=final_norm)
    m = dict(att_norm=m_att_norm, att_w_in=m_att_w_in, att_sink=m_att_sink, att_qnorm=m_att_qnorm,
             att_knorm=m_att_knorm, att_w_out=m_att_w_out, sgu_norm=m_sgu_norm, sgu_w_in=m_sgu_w_in,
             sgu_ln_g=m_sgu_ln_g, sgu_ln_b=m_sgu_ln_b, sgu_w_s=m_sgu_w_s, sgu_b_s=m_sgu_b_s,
             sgu_w_out=m_sgu_w_out, mlp_norm=m_mlp_norm, mlp_w1=m_mlp_w1, mlp_w2=m_mlp_w2,
             final_norm=m_final_norm)
    v = dict(att_norm=v_att_norm, att_w_in=v_att_w_in, att_sink=v_att_sink, att_qnorm=v_att_qnorm,
             att_knorm=v_att_knorm, att_w_out=v_att_w_out, sgu_norm=v_sgu_norm, sgu_w_in=v_sgu_w_in,
             sgu_ln_g=v_sgu_ln_g, sgu_ln_b=v_sgu_ln_b, sgu_w_s=v_sgu_w_s, sgu_b_s=v_sgu_b_s,
             sgu_w_out=v_sgu_w_out, mlp_norm=v_mlp_norm, mlp_w1=v_mlp_w1, mlp_w2=v_mlp_w2,
             final_norm=v_final_norm)
    chip = 2 * lax.axis_index("x") + lax.axis_index("y")

    vecs = jnp.stack([w[n] for n in SHARDED_VEC])
    wb = {n: w[n].astype(BF16) for n in BIG}
    first, vec_all = gather_whole([wb["att_w_in"][0:1], vecs], "gather_first")
    rest_shards = [wb[n][1:2] if n == "att_w_in" else wb[n] for n in REST]
    vec_full = vec_all.transpose(1, 2, 0, 3).reshape(vecs.shape[0], vecs.shape[1], -1)
    rep = {n: w[n] for n in REPLICATED}
    rep.update({n: vec_full[k] for k, n in enumerate(SHARDED_VEC)})

    loss, grad_x, grads, small_g = local_step(x[0], loss_target[0], first, rest_shards, rep,
                                              {n: w[n].shape for n in BIG})
    width = w["sgu_norm"].shape[1]
    for n in SHARDED_VEC:
        small_g[n] = lax.dynamic_slice_in_dim(small_g[n], chip * width, width, axis=1)
    grads.update(small_g)
    for n in BIG:
        grads[n] = grads[n].reshape(w[n].shape)

    delta, new_m, new_v = {}, {}, {}
    for n in WEIGHTS:
        shape = w[n].shape
        two_d = (lambda a: a.reshape(1, -1)) if len(shape) == 1 else (lambda a: a)
        dn, mn, vn = adamw(two_d(w[n]), two_d(grads[n]), two_d(m[n]), two_d(v[n]), "adamw_" + n)
        delta[n], new_m[n], new_v[n] = dn.reshape(shape), mn.reshape(shape), vn.reshape(shape)
    return (loss, grad_x[None], *[grads[n] for n in WEIGHTS], *[delta[n] for n in WEIGHTS],
            *[new_m[n] for n in WEIGHTS], *[new_v[n] for n in WEIGHTS])
```

```python
import functools

import numpy as np
import jax
import jax.numpy as jnp
from jax import lax
from jax.experimental import pallas as pl
from jax.experimental.pallas import tpu as pltpu

F32 = jnp.float32
BF16 = jnp.bfloat16
MESH = pl.DeviceIdType.MESH

EPS = 1e-6
HEAD_DIM = 64
BLOCK = 128
GRID_W = 64
ROPE_THETA = 10000.0
N_CHIPS = 4
LANES = 128
V7X_VMEM_BYTES = 64 * 1024 * 1024
VMEM_LIMIT = V7X_VMEM_BYTES - 8 * 1024 * 1024

ADAM_LR = 0.001
ADAM_B1 = 0.9
ADAM_B2 = 0.999
ADAM_EPS = 1e-08
ADAM_WD = 0.01
ADAM_STEP = 10

NT_DIMS = (((1,), (1,)), ((), ()))
TN_DIMS = (((0,), (0,)), ((), ()))


def _params(*sem):
    return pltpu.CompilerParams(dimension_semantics=sem, vmem_limit_bytes=VMEM_LIMIT)


def _sds(shape, dtype):
    return jax.ShapeDtypeStruct(tuple(shape), dtype)


def _row_tile(rows, want):
    t = min(rows, want)
    assert rows % t == 0, (rows, want)
    return t


def norm_mm(x, g, w4, layer, out_dtype, name):
    s_len, d = x.shape
    ns = w4.shape[-1]
    tm = _row_tile(s_len, 512)

    def body(x_ref, g_ref, w_ref, h_ref, y_ref):
        xf = x_ref[...]
        r = lax.rsqrt(jnp.mean(xf * xf, axis=-1, keepdims=True) + EPS)
        h = ((xf * r) * g_ref[...]).astype(BF16)
        h_ref[...] = h
        for s in range(N_CHIPS):
            y_ref[:, s * ns:(s + 1) * ns] = jnp.dot(h, w_ref[s], preferred_element_type=F32).astype(y_ref.dtype)

    return pl.pallas_call(
        body, name=name, grid=(s_len // tm,),
        in_specs=[pl.BlockSpec((tm, d), lambda i: (i, 0)),
                  pl.BlockSpec((1, d), lambda i: (0, 0)),
                  pl.BlockSpec((N_CHIPS, None, d, ns), lambda i: (0, layer, 0, 0))],
        out_specs=[pl.BlockSpec((tm, d), lambda i: (i, 0)),
                   pl.BlockSpec((tm, N_CHIPS * ns), lambda i: (i, 0))],
        out_shape=[_sds((s_len, d), BF16), _sds((s_len, N_CHIPS * ns), out_dtype)],
        compiler_params=_params("arbitrary"),
    )(x, g, w4)


def mm_res(a, w4, layer, res, name, relu2=False):
    s_len, k = a.shape
    kq, n = w4.shape[-2:]
    assert kq * N_CHIPS == k
    tm = _row_tile(s_len, 256 if k > 1024 else 512)

    def body(a_ref, w0, w1, w2, w3, r_ref, o_ref):
        acc = r_ref[...]
        for s, w_ref in enumerate((w0, w1, w2, w3)):
            av = a_ref[:, s * kq:(s + 1) * kq]
            if relu2:
                t = jnp.maximum(av.astype(F32), 0.0)
                av = (t * t).astype(BF16)
            acc = acc + jnp.dot(av, w_ref[...], preferred_element_type=F32)
        o_ref[...] = acc

    def wspec(s):
        return pl.BlockSpec((None, None, kq, n), lambda i: (s, layer, 0, 0))

    return pl.pallas_call(
        body, name=name, grid=(s_len // tm,),
        in_specs=[pl.BlockSpec((tm, k), lambda i: (i, 0)), wspec(0), wspec(1), wspec(2), wspec(3),
                  pl.BlockSpec((tm, n), lambda i: (i, 0))],
        out_specs=pl.BlockSpec((tm, n), lambda i: (i, 0)),
        out_shape=_sds((s_len, n), F32),
        compiler_params=_params("arbitrary"),
    )(a, w4, w4, w4, w4, res)


def mm_res_t(pieces, w4, layer, res, name):
    s_len = res.shape[0]
    kq, n = w4.shape[-2:]
    rows = pieces[0].shape[0]
    assert rows % kq == 0 and rows * len(pieces) == kq * N_CHIPS
    tm = _row_tile(s_len, 512)
    n_p = len(pieces)

    def body(*refs):
        p_refs, w_refs, (r_ref, o_ref) = refs[:n_p], refs[n_p:n_p + N_CHIPS], refs[n_p + N_CHIPS:]
        acc = r_ref[...]
        for s in range(N_CHIPS):
            p, off = divmod(s * kq, rows)
            acc = acc + lax.dot_general(p_refs[p][off:off + kq, :], w_refs[s][...], TN_DIMS,
                                        preferred_element_type=F32)
        o_ref[...] = acc

    def wspec(s):
        return pl.BlockSpec((None, None, kq, n), lambda i: (s, layer, 0, 0))

    return pl.pallas_call(
        body, name=name, grid=(s_len // tm,),
        in_specs=[pl.BlockSpec((rows, tm), lambda i: (0, i))] * n_p + [wspec(s) for s in range(N_CHIPS)]
        + [pl.BlockSpec((tm, n), lambda i: (i, 0))],
        out_specs=pl.BlockSpec((tm, n), lambda i: (i, 0)),
        out_shape=_sds((s_len, n), F32),
        compiler_params=_params("arbitrary"),
    )(*pieces, w4, w4, w4, w4, res)


def dw_nn(pieces, b, name):
    s_len, n = b.shape
    rows = pieces[0].shape[0]
    n_p = len(pieces)
    k = rows * n_p
    ts = _row_tile(s_len, 2048)
    n_s = s_len // ts

    def body(*refs):
        p_refs, (b_ref, o_ref, acc_ref) = refs[:n_p], refs[n_p:]
        s = pl.program_id(0)
        bv = b_ref[...]
        for p in range(n_p):
            part = jnp.dot(p_refs[p][...], bv, preferred_element_type=F32)
            at = slice(p * rows, (p + 1) * rows)
            if n_s == 1:
                o_ref[at, :] = part.astype(BF16)
                continue

            @pl.when(s == 0)
            def _():
                acc_ref[at, :] = part

            @pl.when((s > 0) & (s < n_s - 1))
            def _():
                acc_ref[at, :] += part

            @pl.when(s == n_s - 1)
            def _():
                o_ref[at, :] = (acc_ref[at, :] + part).astype(BF16)

    out = pl.pallas_call(
        body, name=name, grid=(n_s,),
        in_specs=[pl.BlockSpec((rows, ts), lambda s: (0, s))] * n_p + [pl.BlockSpec((ts, n), lambda s: (s, 0))],
        out_specs=pl.BlockSpec((k, n), lambda s: (0, 0)), out_shape=_sds((k, n), BF16),
        scratch_shapes=[pltpu.VMEM((k, n), F32)],
        compiler_params=_params("arbitrary"),
    )(*pieces, b)
    return out.reshape(N_CHIPS, k // N_CHIPS, n)


def mm_nt(dy, w4, layer, name, transposed=False):
    s_len, n = dy.shape
    mq = w4.shape[-2]
    tm = _row_tile(s_len, 512)

    def body(d_ref, w0, w1, w2, w3, o_ref):
        dv = d_ref[...]
        for s, w_ref in enumerate((w0, w1, w2, w3)):
            if transposed:
                o_ref[s * mq:(s + 1) * mq, :] = lax.dot_general(
                    w_ref[...], dv, NT_DIMS, preferred_element_type=F32).astype(BF16)
            else:
                o_ref[:, s * mq:(s + 1) * mq] = lax.dot_general(
                    dv, w_ref[...], NT_DIMS, preferred_element_type=F32).astype(BF16)

    def wspec(s):
        return pl.BlockSpec((None, None, mq, n), lambda i: (s, layer, 0, 0))

    m = N_CHIPS * mq
    return pl.pallas_call(
        body, name=name, grid=(s_len // tm,),
        in_specs=[pl.BlockSpec((tm, n), lambda i: (i, 0)), wspec(0), wspec(1), wspec(2), wspec(3)],
        out_specs=pl.BlockSpec((m, tm), lambda i: (0, i)) if transposed else pl.BlockSpec((tm, m), lambda i: (i, 0)),
        out_shape=_sds((m, s_len) if transposed else (s_len, m), BF16),
        compiler_params=_params("arbitrary"),
    )(dy, w4, w4, w4, w4)


def mm_nt_relu2_bwd(dy, w4, layer, a, name):
    s_len, n = dy.shape
    mq = w4.shape[-2]
    tm = _row_tile(s_len, 512)

    def body(d_ref, w_ref, a_ref, o_ref):
        dv = d_ref[...]
        for s in range(N_CHIPS):
            cols = slice(s * mq, (s + 1) * mq)
            dz = lax.dot_general(dv, w_ref[s], NT_DIMS, preferred_element_type=F32)
            o_ref[:, cols] = (dz * (2.0 * jnp.maximum(a_ref[:, cols].astype(F32), 0.0))).astype(BF16)

    return pl.pallas_call(
        body, name=name, grid=(s_len // tm,),
        in_specs=[pl.BlockSpec((tm, n), lambda i: (i, 0)),
                  pl.BlockSpec((N_CHIPS, None, mq, n), lambda i: (0, layer, 0, 0)),
                  pl.BlockSpec((tm, N_CHIPS * mq), lambda i: (i, 0))],
        out_specs=pl.BlockSpec((tm, N_CHIPS * mq), lambda i: (i, 0)),
        out_shape=_sds((s_len, N_CHIPS * mq), BF16),
        compiler_params=_params("arbitrary"),
    )(dy, w4, a)


def dx_norm(dy, w4, layer, x, g, dres, name):
    s_len, d = x.shape
    ns = w4.shape[-1]
    tm = _row_tile(s_len, 512)

    def body(dy_ref, w_ref, x_ref, g_ref, dr_ref, dx_ref, dxb_ref, dg_ref):
        i = pl.program_id(0)
        dh = lax.dot_general(dy_ref[:, 0:ns], w_ref[0], NT_DIMS, preferred_element_type=F32)
        for s in range(1, N_CHIPS):
            dh = dh + lax.dot_general(dy_ref[:, s * ns:(s + 1) * ns], w_ref[s], NT_DIMS,
                                      preferred_element_type=F32)
        xf = x_ref[...]
        r = lax.rsqrt(jnp.mean(xf * xf, axis=-1, keepdims=True) + EPS)
        xhat = xf * r
        dg_part = jnp.sum(dh * xhat, axis=0, keepdims=True)

        @pl.when(i == 0)
        def _():
            dg_ref[...] = dg_part

        @pl.when(i > 0)
        def _():
            dg_ref[...] += dg_part

        dxh = dh * g_ref[...]
        dx = dr_ref[...] + r * (dxh - xhat * jnp.mean(dxh * xhat, axis=-1, keepdims=True))
        dx_ref[...] = dx
        dxb_ref[...] = dx.astype(BF16)

    row = pl.BlockSpec((tm, d), lambda i: (i, 0))
    vec = pl.BlockSpec((1, d), lambda i: (0, 0))
    return pl.pallas_call(
        body, name=name, grid=(s_len // tm,),
        in_specs=[pl.BlockSpec((tm, N_CHIPS * ns), lambda i: (i, 0)),
                  pl.BlockSpec((N_CHIPS, None, d, ns), lambda i: (0, layer, 0, 0)), row, vec, row],
        out_specs=[row, row, vec],
        out_shape=[_sds((s_len, d), F32), _sds((s_len, d), BF16), _sds((1, d), F32)],
        compiler_params=_params("arbitrary"),
    )(dy, w4, x, g, dres)


def dw_mm(a, b, name, col_sharded, relu2=False):
    s_len, k = a.shape
    n = b.shape[1]
    ts = _row_tile(s_len, 2048)
    tk = min(k, 1024)
    tn = n // N_CHIPS if col_sharded else min(n, 1024)
    n_s = s_len // ts

    def body(a_ref, b_ref, o_ref, acc_ref):
        s = pl.program_id(2)
        av = a_ref[...]
        if relu2:
            t = jnp.maximum(av.astype(F32), 0.0)
            av = (t * t).astype(BF16)
        part = lax.dot_general(av, b_ref[...], TN_DIMS, preferred_element_type=F32)
        if n_s == 1:
            o_ref[...] = part.astype(BF16)
            return

        @pl.when(s == 0)
        def _():
            acc_ref[...] = part

        @pl.when((s > 0) & (s < n_s - 1))
        def _():
            acc_ref[...] += part

        @pl.when(s == n_s - 1)
        def _():
            o_ref[...] = (acc_ref[...] + part).astype(BF16)

    if col_sharded:
        out_shape = _sds((N_CHIPS, k, tn), BF16)
        out_spec = pl.BlockSpec((None, tk, tn), lambda i, j, s: (j, i, 0))
    else:
        out_shape = _sds((N_CHIPS, k // N_CHIPS, n), BF16)
        rows_per = k // N_CHIPS
        assert tk % rows_per == 0 or rows_per % tk == 0
        if tk >= rows_per:
            out_shape = _sds((k, n), BF16)
            out_spec = pl.BlockSpec((tk, tn), lambda i, j, s: (i, j))
        else:
            per = rows_per // tk
            out_spec = pl.BlockSpec((None, tk, tn), lambda i, j, s: (i // per, i % per, j))

    out = pl.pallas_call(
        body, name=name, grid=(k // tk, n // tn, n_s),
        in_specs=[pl.BlockSpec((ts, tk), lambda i, j, s: (s, i)),
                  pl.BlockSpec((ts, tn), lambda i, j, s: (s, j))],
        out_specs=out_spec, out_shape=out_shape,
        scratch_shapes=[pltpu.VMEM((tk, tn), F32)],
        compiler_params=_params("arbitrary", "arbitrary", "arbitrary"),
    )(a, b)
    if not col_sharded:
        out = out.reshape(N_CHIPS, k // N_CHIPS, n)
    return out


def ew(fn, ins, out_dtypes, name, tile_rows=256):
    rows, cols = ins[0].shape
    for a in ins:
        assert a.shape == (rows, cols), (name, a.shape, rows, cols)
    tr = rows if (rows <= tile_rows or rows % tile_rows) else tile_rows
    n_in = len(ins)

    def body(*refs):
        outs = fn(*[r[...] for r in refs[:n_in]])
        for o_ref, val in zip(refs[n_in:], outs):
            o_ref[...] = val.astype(o_ref.dtype)

    spec = pl.BlockSpec((tr, cols), lambda i: (i, 0))
    return pl.pallas_call(
        body, name=name, grid=(rows // tr,),
        in_specs=[spec] * n_in, out_specs=[spec] * len(out_dtypes),
        out_shape=[_sds((rows, cols), dt) for dt in out_dtypes],
        compiler_params=_params("arbitrary"),
    )(*ins)


def adamw(w, g, m, v, name):
    shape = w.shape
    cols = shape[-1]
    two_d = lambda a: a.reshape(-1, cols)

    def fn(wv, gv, mv, vv):
        m_new = ADAM_B1 * mv + (1.0 - ADAM_B1) * gv
        v_new = ADAM_B2 * vv + (1.0 - ADAM_B2) * (gv * gv)
        m_hat = m_new / (1.0 - ADAM_B1 ** ADAM_STEP)
        v_hat = v_new / (1.0 - ADAM_B2 ** ADAM_STEP)
        delta = -ADAM_LR * (m_hat / (jnp.sqrt(v_hat) + ADAM_EPS) + ADAM_WD * wv)
        return delta, m_new, v_new

    d, mn, vn = ew(fn, [two_d(w), two_d(g), two_d(m), two_d(v)], [F32, F32, F32], name)
    return d.reshape(shape), mn.reshape(shape), vn.reshape(shape)


def rope_tables(s_len):
    def angles(pos, dim):
        freqs = ROPE_THETA ** (-jnp.arange(0, dim, 2, dtype=F32) / dim)
        ang = pos.astype(F32)[:, None] * freqs[None, :]
        return jnp.cos(ang), jnp.sin(ang)

    pos = jnp.arange(s_len)
    rows = s_len // GRID_W
    row_idx = jnp.repeat(jnp.arange(rows), GRID_W)
    col_idx = jnp.tile(jnp.arange(GRID_W), rows)
    c1, s1 = angles(pos, HEAD_DIM)
    cr, sr = angles(row_idx, HEAD_DIM // 2)
    cc, sc = angles(col_idx, HEAD_DIM // 2)
    cos1 = jnp.tile(jnp.concatenate([c1, c1], -1), (1, 2))
    sin1 = jnp.tile(jnp.concatenate([-s1, s1], -1), (1, 2))
    cos2 = jnp.tile(jnp.concatenate([cr, cr, cc, cc], -1), (1, 2))
    sin2 = jnp.tile(jnp.concatenate([-sr, sr, -sc, sc], -1), (1, 2))
    return cos1, sin1, cos2, sin2


def _lane_iota(rows):
    return lax.broadcasted_iota(jnp.int32, (rows, LANES), 1)


def _swap(x, dist, lane):
    return jnp.where((lane & dist) != 0, pltpu.roll(x, dist, 1), pltpu.roll(x, LANES - dist, 1))


def _head_ones():
    r = lax.broadcasted_iota(jnp.int32, (LANES, LANES), 0) // HEAD_DIM
    c = lax.broadcasted_iota(jnp.int32, (LANES, LANES), 1) // HEAD_DIM
    return (r == c).astype(BF16)


def _head_sum(t, ones):
    hi = t.astype(BF16)
    lo = (t - hi.astype(F32)).astype(BF16)
    return (jnp.dot(hi, ones, preferred_element_type=F32) + jnp.dot(lo, ones, preferred_element_type=F32))


Q_SCALE = HEAD_DIM ** -0.5
LOG2E = 1.4426950408889634
LN2 = 0.6931471805599453
CHUNK_KIND = ["qa"] * 4 + ["ka", "va"] + ["qb"] * 4 + ["kb", "vb"]
QA_COL, KA_COL, QB_COL, KB_COL = 0, 512, 768, 1280


def prep_fwd(proj, tabs, qn_g, kn_g, name):
    s_len, width = proj.shape
    ts = _row_tile(s_len, 512)
    cos1, sin1, cos2, sin2 = tabs

    def body(p_ref, c1_ref, s1_ref, c2_ref, s2_ref, qg_ref, kg_ref, o_ref, kv_ref):
        lane = _lane_iota(ts)
        ones = _head_ones()
        c1, s1, c2, s2 = c1_ref[...], s1_ref[...], c2_ref[...], s2_ref[...]
        n_kv = 0
        for cb, kind in enumerate(CHUNK_KIND):
            x = p_ref[:, cb * LANES:(cb + 1) * LANES]
            if kind in ("qa", "ka"):
                y = x * c1 + _swap(x, 32, lane) * s1
            elif kind in ("qb", "kb"):
                gain = qg_ref[...] if kind == "qb" else kg_ref[...]
                ms = _head_sum(x * x, ones) * (1.0 / HEAD_DIM)
                xn = (x * lax.rsqrt(ms + EPS)) * gain
                y = xn * c2 + _swap(xn, 16, lane) * s2
            else:
                y = x
            if kind in ("qa", "qb"):
                y = y * (Q_SCALE * LOG2E)
            else:
                kv_ref[:, n_kv * LANES:(n_kv + 1) * LANES] = y.astype(BF16)
                n_kv += 1
            o_ref[cb * LANES:(cb + 1) * LANES, :] = y.T.astype(BF16)

    tab = pl.BlockSpec((ts, LANES), lambda i: (i, 0))
    vec = pl.BlockSpec((1, LANES), lambda i: (0, 0))
    return pl.pallas_call(
        body, name=name, grid=(s_len // ts,),
        in_specs=[pl.BlockSpec((ts, width), lambda i: (i, 0)), tab, tab, tab, tab, vec, vec],
        out_specs=[pl.BlockSpec((width, ts), lambda i: (0, i)), pl.BlockSpec((ts, 4 * LANES), lambda i: (i, 0))],
        out_shape=[_sds((width, s_len), BF16), _sds((s_len, 4 * LANES), BF16)],
        compiler_params=_params("arbitrary"),
    )(proj, cos1, sin1, cos2, sin2, qn_g, kn_g)


def prep_bwd(proj, dqa, dka, dva, dqb, dkb, dvb, tabs, qn_g, kn_g, name):
    s_len, width = proj.shape
    ts = _row_tile(s_len, 256)
    cos1, sin1, cos2, sin2 = tabs

    def body(p_ref, dqa_ref, dka_ref, dva_ref, dqb_ref, dkb_ref, dvb_ref,
             c1_ref, s1_ref, c2_ref, s2_ref, qg_ref, kg_ref, o_ref, dqg_ref, dkg_ref):
        i = pl.program_id(0)
        lane = _lane_iota(ts)
        c1, s1, c2, s2 = c1_ref[...], s1_ref[...], c2_ref[...], s2_ref[...]

        def rope_t(dy, cos, sin, dist):
            return dy * cos + _swap(dy * sin, dist, lane)

        ones = _head_ones()

        def norm_bwd(dy, x, gain):
            r = lax.rsqrt(_head_sum(x * x, ones) * (1.0 / HEAD_DIM) + EPS)
            xhat = x * r
            dgain = jnp.sum(dy * xhat, axis=0, keepdims=True)
            dxh = dy * gain
            dx = r * (dxh - xhat * (_head_sum(dxh * xhat, ones) * (1.0 / HEAD_DIM)))
            return dx, dgain

        dqg = jnp.zeros((1, LANES), F32)
        dkg = jnp.zeros((1, LANES), F32)
        for cb, kind in enumerate(CHUNK_KIND):
            cols = slice(cb * LANES, (cb + 1) * LANES)
            if kind == "qa":
                dx = rope_t(dqa_ref[cols, :].T * Q_SCALE, c1, s1, 32)
            elif kind == "ka":
                dx = rope_t(dka_ref[...], c1, s1, 32)
            elif kind == "va":
                dx = dva_ref[...]
            elif kind == "qb":
                qcols = slice((cb - 6) * LANES, (cb - 5) * LANES)
                dy = rope_t(dqb_ref[qcols, :].T * Q_SCALE, c2, s2, 16)
                dx, dgain = norm_bwd(dy, p_ref[:, cols], qg_ref[...])
                dqg = dqg + dgain
            elif kind == "kb":
                dy = rope_t(dkb_ref[...], c2, s2, 16)
                dx, dgain = norm_bwd(dy, p_ref[:, cols], kg_ref[...])
                dkg = dkg + dgain
            else:
                dx = dvb_ref[...]
            o_ref[:, cols] = dx.astype(BF16)

        @pl.when(i == 0)
        def _():
            dqg_ref[...] = dqg
            dkg_ref[...] = dkg

        @pl.when(i > 0)
        def _():
            dqg_ref[...] += dqg
            dkg_ref[...] += dkg

    tab = pl.BlockSpec((ts, LANES), lambda i: (i, 0))
    vec = pl.BlockSpec((1, LANES), lambda i: (0, 0))

    def dq_spec(dq):
        per = dq.shape[2] // ts
        return pl.BlockSpec((None, 4 * LANES, ts), lambda i: (i // per, 0, i % per))

    return pl.pallas_call(
        body, name=name, grid=(s_len // ts,),
        in_specs=([pl.BlockSpec((ts, width), lambda i: (i, 0)), dq_spec(dqa), tab, tab, dq_spec(dqb), tab, tab]
                  + [tab] * 4 + [vec, vec]),
        out_specs=[pl.BlockSpec((ts, width), lambda i: (i, 0)), vec, vec],
        out_shape=[_sds((s_len, width), BF16), _sds((1, LANES), F32), _sds((1, LANES), F32)],
        compiler_params=_params("arbitrary"),
    )(proj, dqa, dka, dva, dqb, dkb, dvb, cos1, sin1, cos2, sin2, qn_g, kn_g)


NEG = -1e30
GROUP = 4
KV_HEADS = 2
GROUP_W = GROUP * HEAD_DIM
LSE_ROWS = 8
ONES_ROWS = 16


def _pos_mask_t(k_start, q_start, s_len, tk, tq):
    kpos = k_start + lax.broadcasted_iota(jnp.int32, (tk, tq), 0)
    qpos = q_start + lax.broadcasted_iota(jnp.int32, (tk, tq), 1)
    return (jnp.abs(kpos - qpos) <= BLOCK) & (kpos >= 0) & (kpos < s_len)


def flash_fwd_t(qkv_t, kv_tok, q_rb, k_i, v_rb, sink, window, name, comm=None):
    s_len = qkv_t.shape[1]
    if window:
        tq = _row_tile(s_len, 512)
        tk = 2 * BLOCK
        assert tq == 2 * tk, "the band parts below are written for query blocks of two key blocks"
        n_kv = 2
    else:
        tq, tk = _row_tile(s_len, 512), _row_tile(s_len, 8192)
        n_kv = s_len // tk
    n_i = s_len // tq
    c_ins, c_outs, c_remote = comm if comm else ([], [], [])
    n_main = 6 if window else 3

    def body(*refs):
        main, c_in_refs = refs[:n_main], refs[n_main:n_main + len(c_ins)]
        rest = refs[n_main + len(c_ins):]
        (o_ref, lse_ref), c_out_refs = rest[:2], rest[2:2 + len(c_outs)]
        m_sc, acc_sc = rest[2 + len(c_outs):4 + len(c_outs)]
        c_sems = rest[4 + len(c_outs):]
        if window:
            sink_ref, q_ref, k_ref, v_ref, kc_ref, vc_ref = main
        else:
            q_ref, k_ref, v_ref = main
        h, i, t = pl.program_id(0), pl.program_id(1), pl.program_id(2)
        if comm:
            @pl.when((h == 0) & (i == 0) & (t == 0))
            def _():
                _exchange_start(c_remote, c_in_refs, c_out_refs, *c_sems)

        @pl.when(t == 0)
        def _():
            for g in range(GROUP):
                acc_sc[g, 0:HEAD_DIM, :] = jnp.zeros((HEAD_DIM, tq), F32)
                if window:
                    m_sc[g] = jnp.full((1, tq), sink_ref[h * GROUP + g] * LOG2E, F32)
                    acc_sc[g, HEAD_DIM:, :] = jnp.ones((ONES_ROWS, tq), F32)
                else:
                    m_sc[g] = jnp.full((1, tq), NEG, F32)
                    acc_sc[g, HEAD_DIM:, :] = jnp.zeros((ONES_ROWS, tq), F32)

        def tile(k_src, v_src, n_keys, key_pos, q_lo, q_hi):
            qs = slice(q_lo, q_hi)
            k = k_src[...]
            v_t = jnp.concatenate([v_src[...], jnp.ones((ONES_ROWS, n_keys), BF16)], axis=0)
            if window:
                mask = _pos_mask_t(key_pos, i * tq + q_lo, s_len, n_keys, q_hi - q_lo)
            s_next = jnp.dot(k, q_ref[0:HEAD_DIM, qs], preferred_element_type=F32)
            for g in range(GROUP):
                s_t = s_next
                if g + 1 < GROUP:
                    s_next = jnp.dot(k, q_ref[(g + 1) * HEAD_DIM:(g + 2) * HEAD_DIM, qs],
                                     preferred_element_type=F32)
                if window:
                    s_t = jnp.where(mask, s_t, NEG)
                m_prev = m_sc[g, :, qs]
                m_new = jnp.maximum(m_prev, jnp.max(s_t, axis=0, keepdims=True))
                alpha = jnp.exp2(m_prev - m_new)
                p_t = jnp.exp2(s_t - m_new)
                acc_sc[g, :, qs] = alpha * acc_sc[g, :, qs] + jnp.dot(v_t, p_t.astype(BF16),
                                                                     preferred_element_type=F32)
                m_sc[g, :, qs] = m_new

        if window:
            @pl.when(t == 0)
            def _():
                tile(k_ref, v_ref, tk, i * tq, 0, tq - BLOCK)
                tile(kc_ref, vc_ref, BLOCK, i * tq - BLOCK, 0, BLOCK)

            @pl.when(t == 1)
            def _():
                tile(k_ref, v_ref, tk, i * tq + tk, BLOCK, tq)
                tile(kc_ref, vc_ref, BLOCK, i * tq + tq, tq - BLOCK, tq)
        else:
            tile(k_ref, v_ref, tk, t * tk, 0, tq)

        @pl.when(t == n_kv - 1)
        def _():
            for g in range(GROUP):
                l = acc_sc[g, HEAD_DIM:HEAD_DIM + 1, :]
                o_ref[g * HEAD_DIM:(g + 1) * HEAD_DIM, :] = (acc_sc[g, 0:HEAD_DIM, :] / l).astype(BF16)
                lse_ref[g * LSE_ROWS:(g + 1) * LSE_ROWS, :] = jnp.broadcast_to(
                    m_sc[g] + jnp.log(l) * LOG2E, (LSE_ROWS, tq))

        if comm:
            @pl.when((h == KV_HEADS - 1) & (i == n_i - 1) & (t == n_kv - 1))
            def _():
                _exchange_finish(c_remote, c_in_refs, c_out_refs, *c_sems)

    kv_blk = (lambda i, t: 2 * i + t) if window else (lambda i, t: t)
    hbm = pl.BlockSpec(memory_space=pl.ANY)
    in_specs = [pl.BlockSpec((GROUP_W, tq), lambda h, i, t: (q_rb + h, i)),
                pl.BlockSpec((None, tk, HEAD_DIM), lambda h, i, t: (k_i + h, kv_blk(i, t), 0)),
                pl.BlockSpec((HEAD_DIM, tk), lambda h, i, t: (v_rb + h, kv_blk(i, t)))]
    args = [qkv_t, kv_tok, qkv_t]
    if window:
        corner = lambda i, t: jnp.clip((tq // BLOCK) * i - 1 + (tq // BLOCK + 1) * t, 0, s_len // BLOCK - 1)
        in_specs = ([pl.BlockSpec(memory_space=pltpu.SMEM)] + in_specs
                    + [pl.BlockSpec((None, BLOCK, HEAD_DIM), lambda h, i, t: (k_i + h, corner(i, t), 0)),
                       pl.BlockSpec((HEAD_DIM, BLOCK), lambda h, i, t: (v_rb + h, corner(i, t)))])
        args = [sink] + args + [kv_tok, qkv_t]
    return pl.pallas_call(
        body, name=name, grid=(KV_HEADS, n_i, n_kv),
        in_specs=in_specs + [hbm] * len(c_ins),
        out_specs=[pl.BlockSpec((GROUP_W, tq), lambda h, i, t: (h, i)),
                   pl.BlockSpec((GROUP * LSE_ROWS, tq), lambda h, i, t: (h, i))] + [hbm] * len(c_outs),
        out_shape=[_sds((KV_HEADS * GROUP_W, s_len), BF16),
                   _sds((KV_HEADS * GROUP * LSE_ROWS, s_len), F32)] + list(c_outs),
        scratch_shapes=[pltpu.VMEM((GROUP, 1, tq), F32),
                        pltpu.VMEM((GROUP, HEAD_DIM + ONES_ROWS, tq), F32)] + _exchange_sems(c_remote),
        compiler_params=_params("arbitrary", "arbitrary", "arbitrary"),
    )(*args, *c_ins)


def flash_bwd_t(qkv_t, kv_tok, o_t, do_t, lse, q_rb, k_i, v_i, k_rb, do_rb, sink, window, name, comm=None):
    s_len = qkv_t.shape[1]
    if window:
        tq = _row_tile(s_len, 512)
        tk = 2 * BLOCK
        assert tq == 2 * tk, "the band parts below are written for query blocks of two key blocks"
        n_q = 1
    else:
        tq, tk = _row_tile(s_len, 2048), _row_tile(s_len, 1024)
        n_q = s_len // tq
    n_qb = s_len // tq
    n_j = s_len // tk
    c_ins, c_outs, c_remote = comm if comm else ([], [], [])
    n_main = 12 if window else 7
    n_out = 4 if window else 3

    def body(*refs):
        main, c_in_refs = refs[:n_main], refs[n_main:n_main + len(c_ins)]
        rest = refs[n_main + len(c_ins):]
        outs, c_out_refs = rest[:n_out], rest[n_out:n_out + len(c_outs)]
        dk_sc, dv_sc = rest[n_out + len(c_outs):n_out + len(c_outs) + 2]
        c_sems = rest[n_out + len(c_outs) + 2:]
        if window:
            sink_ref, q_ref, k_ref, v_ref, kt_ref, o_ref, do_ref, lse_ref = main[:8]
            corner_src = main[8:]
            dq_ref, dk_ref, dv_ref, dsink_ref = outs
        else:
            q_ref, k_ref, v_ref, kt_ref, o_ref, do_ref, lse_ref = main
            dq_ref, dk_ref, dv_ref = outs
        main_src = (q_ref, o_ref, do_ref, lse_ref)
        h, j, t = pl.program_id(0), pl.program_id(1), pl.program_id(2)
        if comm:
            @pl.when((h == 0) & (j == 0) & (t == 0))
            def _():
                _exchange_start(c_remote, c_in_refs, c_out_refs, *c_sems)

        @pl.when((j == 0) & (t == 0))
        def _():
            dq_ref[...] = jnp.zeros(dq_ref.shape, F32)
            if window:
                dsink_ref[...] = jnp.zeros((8, LANES), F32)

        @pl.when(t == 0)
        def _():
            dk_sc[...] = jnp.zeros((tk, HEAD_DIM), F32)
            dv_sc[...] = jnp.zeros((tk, HEAD_DIM), F32)

        def tile(src, q_lo, q_hi, q_pos, k_lo, k_hi, dq_blk, dq_lo, sink_lo=0, sink_hi=0):
            q_src, o_src, do_src, lse_src = src
            ks, qs = slice(k_lo, k_hi), slice(q_lo, q_hi)
            dqs = slice(dq_lo, dq_lo + q_hi - q_lo)
            k, v, k_t = k_ref[ks, :], v_ref[ks, :], kt_ref[:, ks]
            if window:
                mask = _pos_mask_t(j * tk + k_lo, q_pos, s_len, k_hi - k_lo, q_hi - q_lo)
                lane = lax.broadcasted_iota(jnp.int32, (8, LANES), 1)
                sink_tile = jnp.zeros((8, LANES), F32)
            dk_acc = dk_sc[ks, :]
            dv_acc = dv_sc[ks, :]
            for g in range(GROUP):
                rows = slice(g * HEAD_DIM, (g + 1) * HEAD_DIM)
                q_t, o_g, do_g = q_src[rows, qs], o_src[rows, qs], do_src[rows, qs]
                s_t = jnp.dot(k, q_t, preferred_element_type=F32)
                if window:
                    s_t = jnp.where(mask, s_t, NEG)
                lse_row = lse_src[g * LSE_ROWS:g * LSE_ROWS + 1, qs]
                p_t = jnp.exp2(s_t - lse_row)
                delta = jnp.sum(do_g.astype(F32) * o_g.astype(F32), axis=0, keepdims=True)
                dp_t = jnp.dot(v, do_g, preferred_element_type=F32)
                ds_t = (p_t * (dp_t - delta)).astype(BF16)
                dv_acc = dv_acc + lax.dot_general(p_t.astype(BF16), do_g, NT_DIMS, preferred_element_type=F32)
                dk_acc = dk_acc + lax.dot_general(ds_t, q_t, NT_DIMS, preferred_element_type=F32)
                dq_ref[dq_blk, rows, dqs] += jnp.dot(k_t, ds_t, preferred_element_type=F32)
                if sink_hi > sink_lo:
                    at = slice(sink_lo - q_lo, sink_hi - q_lo)
                    p_sink = jnp.exp2(sink_ref[h * GROUP + g] * LOG2E - lse_row[:, at])
                    term = -jnp.sum(p_sink * delta[:, at], axis=1, keepdims=True)
                    sink_tile = jnp.where(lane == g, term, sink_tile)
            dk_sc[ks, :] = dk_acc
            dv_sc[ks, :] = dv_acc
            if sink_hi > sink_lo:
                dsink_ref[...] += sink_tile

        if window:
            m = j // 2

            @pl.when(j % 2 == 0)
            def _():
                tile(main_src, 0, tq - BLOCK, m * tq, 0, tk, m, 0, 0, tq - BLOCK)

            @pl.when((j % 2 == 0) & (j > 0))
            def _():
                tile(corner_src, 0, BLOCK, m * tq - BLOCK, 0, BLOCK, m - 1, tq - BLOCK)

            @pl.when(j % 2 == 1)
            def _():
                tile(main_src, BLOCK, tq, m * tq + BLOCK, 0, tk, m, BLOCK, tq - BLOCK, tq)

            @pl.when((j % 2 == 1) & (j < n_j - 1))
            def _():
                tile(corner_src, 0, BLOCK, (m + 1) * tq, tk - BLOCK, tk, m + 1, 0)
        else:
            tile(main_src, 0, tq, t * tq, 0, tk, t, 0)

        @pl.when(t == n_q - 1)
        def _():
            dk_ref[...] = dk_sc[...] * LN2
            dv_ref[...] = dv_sc[...]

        if comm:
            @pl.when((h == KV_HEADS - 1) & (j == n_j - 1) & (t == n_q - 1))
            def _():
                _exchange_finish(c_remote, c_in_refs, c_out_refs, *c_sems)

    qb = (lambda j, t: j // 2) if window else (lambda j, t: t)
    hbm = pl.BlockSpec(memory_space=pl.ANY)
    in_specs = [pl.BlockSpec((GROUP_W, tq), lambda h, j, t: (q_rb + h, qb(j, t))),
                pl.BlockSpec((None, tk, HEAD_DIM), lambda h, j, t: (k_i + h, j, 0)),
                pl.BlockSpec((None, tk, HEAD_DIM), lambda h, j, t: (v_i + h, j, 0)),
                pl.BlockSpec((HEAD_DIM, tk), lambda h, j, t: (k_rb + h, j)),
                pl.BlockSpec((GROUP_W, tq), lambda h, j, t: (h, qb(j, t))),
                pl.BlockSpec((GROUP_W, tq), lambda h, j, t: (do_rb + h, qb(j, t))),
                pl.BlockSpec((GROUP * LSE_ROWS, tq), lambda h, j, t: (h, qb(j, t)))]
    args = [qkv_t, kv_tok, kv_tok, qkv_t, o_t, do_t, lse]
    kv_out = _sds((KV_HEADS, s_len, HEAD_DIM), F32)
    out_specs = [pl.BlockSpec((n_qb, GROUP_W, tq), lambda h, j, t: (0, h, 0)),
                 pl.BlockSpec((None, tk, HEAD_DIM), lambda h, j, t: (h, j, 0)),
                 pl.BlockSpec((None, tk, HEAD_DIM), lambda h, j, t: (h, j, 0))]
    out_shape = [_sds((n_qb, KV_HEADS * GROUP_W, tq), F32), kv_out, kv_out]
    if window:
        cq = lambda j: jnp.clip(2 * j - 1 + 3 * (j % 2), 0, s_len // BLOCK - 1)
        in_specs = ([pl.BlockSpec(memory_space=pltpu.SMEM)] + in_specs
                    + [pl.BlockSpec((GROUP_W, BLOCK), lambda h, j, t: (q_rb + h, cq(j))),
                       pl.BlockSpec((GROUP_W, BLOCK), lambda h, j, t: (h, cq(j))),
                       pl.BlockSpec((GROUP_W, BLOCK), lambda h, j, t: (do_rb + h, cq(j))),
                       pl.BlockSpec((GROUP * LSE_ROWS, BLOCK), lambda h, j, t: (h, cq(j)))])
        args = [sink] + args + [qkv_t, o_t, do_t, lse]
        out_specs.append(pl.BlockSpec((None, 8, LANES), lambda h, j, t: (h, 0, 0)))
        out_shape.append(_sds((KV_HEADS, 8, LANES), F32))
    return pl.pallas_call(
        body, name=name, grid=(KV_HEADS, n_j, n_q),
        in_specs=in_specs + [hbm] * len(c_ins), out_specs=out_specs + [hbm] * len(c_outs),
        out_shape=out_shape + list(c_outs),
        scratch_shapes=[pltpu.VMEM((tk, HEAD_DIM), F32), pltpu.VMEM((tk, HEAD_DIM), F32)]
        + _exchange_sems(c_remote),
        compiler_params=_params("arbitrary", "arbitrary", "arbitrary"),
    )(*args, *c_ins)


SGU_GROUPS = 8
SGU_CHUNK = 128
GELU_C = float(np.sqrt(2.0 / np.pi))
GELU_A = 0.044715


def _gelu_and_grad(x):
    x2 = x * x
    t = jnp.tanh(x * (GELU_C + (GELU_C * GELU_A) * x2))
    hx = 0.5 * x
    return hx + hx * t, (0.5 + 0.5 * t) + (hx * (1.0 - t * t)) * (GELU_C + (3.0 * GELU_C * GELU_A) * x2)


def _gelu(x):
    t = jnp.tanh(x * (GELU_C + (GELU_C * GELU_A) * (x * x)))
    hx = 0.5 * x
    return hx + hx * t


def _layernorm_stats(v):
    mu = jnp.mean(v, axis=-1, keepdims=True)
    var = jnp.mean(jnp.square(v - mu), axis=-1, keepdims=True)
    rstd = lax.rsqrt(var + EPS)
    return (v - mu) * rstd, rstd


def sgu_mid_fwd(zpre, ln_g, ln_b, ws, bsb, name):
    s_len, width = zpre.shape
    d = width // 2
    ts = _row_tile(s_len, 256)

    def body(z_ref, g_ref, b_ref, ws_ref, bs_ref, y_ref):
        z = _gelu(z_ref[...])
        u, v = z[:, :d], z[:, d:]
        vhat, _ = _layernorm_stats(v)
        vn = (vhat * g_ref[...] + b_ref[...]).astype(BF16)
        for n in range(ts // SGU_CHUNK):
            rows = slice(n * SGU_CHUNK, (n + 1) * SGU_CHUNK)
            for g in range(SGU_GROUPS):
                cols = slice(g * LANES, (g + 1) * LANES)
                mixed = jnp.dot(ws_ref[g], vn[rows, cols], preferred_element_type=F32) + bs_ref[g]
                y_ref[rows, cols] = (u[rows, cols] * mixed).astype(BF16)

    vec = pl.BlockSpec((1, d), lambda i: (0, 0))
    cube = pl.BlockSpec((SGU_GROUPS, SGU_CHUNK, SGU_CHUNK), lambda i: (0, 0, 0))
    return pl.pallas_call(
        body, name=name, grid=(s_len // ts,),
        in_specs=[pl.BlockSpec((ts, width), lambda i: (i, 0)), vec, vec, cube, cube],
        out_specs=pl.BlockSpec((ts, d), lambda i: (i, 0)),
        out_shape=_sds((s_len, d), BF16),
        compiler_params=_params("arbitrary"),
    )(zpre, ln_g, ln_b, ws, bsb)


def sgu_mid_bwd(zpre, dy, ln_g, ln_b, ws, wst, bsb, name):
    s_len, width = zpre.shape
    d = width // 2
    ts = _row_tile(s_len, 256)
    n_steps = s_len // ts

    def body(z_ref, dy_ref, g_ref, b_ref, ws_ref, wst_ref, bs_ref,
             dz_ref, dws_ref, dbs_ref, dg_ref, db_ref, du_sc, dvn_sc):
        i = pl.program_id(0)

        @pl.when(i == 0)
        def _():
            dws_ref[...] = jnp.zeros(dws_ref.shape, F32)
            dbs_ref[...] = jnp.zeros(dbs_ref.shape, F32)
            dg_ref[...] = jnp.zeros(dg_ref.shape, F32)
            db_ref[...] = jnp.zeros(db_ref.shape, F32)

        zp = z_ref[...]
        z, gp = _gelu_and_grad(zp)
        u, v = z[:, :d], z[:, d:]
        vhat, rstd = _layernorm_stats(v)
        gain = g_ref[...]
        vn = (vhat * gain + b_ref[...]).astype(BF16)
        dyf = dy_ref[...].astype(F32)
        for n in range(ts // SGU_CHUNK):
            rows = slice(n * SGU_CHUNK, (n + 1) * SGU_CHUNK)
            for g in range(SGU_GROUPS):
                cols = slice(g * LANES, (g + 1) * LANES)
                vt = vn[rows, cols]
                mixed = jnp.dot(ws_ref[g], vt, preferred_element_type=F32) + bs_ref[g]
                dyt = dyf[rows, cols]
                du_sc[rows, cols] = dyt * mixed
                dmixed = dyt * u[rows, cols]
                dmb = dmixed.astype(BF16)
                dvn_sc[rows, cols] = jnp.dot(wst_ref[g], dmb, preferred_element_type=F32)
                dws_ref[g] += lax.dot_general(dmb, vt, NT_DIMS, preferred_element_type=F32)
                dbs_ref[g] += dmixed
        dvn = dvn_sc[...]
        dg_ref[...] += jnp.sum(dvn * vhat, axis=0, keepdims=True)
        db_ref[...] += jnp.sum(dvn, axis=0, keepdims=True)
        dvh = dvn * gain
        dv = rstd * (dvh - jnp.mean(dvh, axis=-1, keepdims=True)
                     - vhat * jnp.mean(dvh * vhat, axis=-1, keepdims=True))
        dz_ref[:, :d] = (du_sc[...] * gp[:, :d]).astype(BF16)
        dz_ref[:, d:] = (dv * gp[:, d:]).astype(BF16)

        @pl.when(i == n_steps - 1)
        def _():
            for g in range(SGU_GROUPS):
                tot = jnp.sum(dbs_ref[g], axis=1, keepdims=True)
                dbs_ref[g] = jnp.broadcast_to(tot, (SGU_CHUNK, LANES))

    vec = pl.BlockSpec((1, d), lambda i: (0, 0))
    cube = pl.BlockSpec((SGU_GROUPS, SGU_CHUNK, SGU_CHUNK), lambda i: (0, 0, 0))
    cube_shape = _sds((SGU_GROUPS, SGU_CHUNK, SGU_CHUNK), F32)
    return pl.pallas_call(
        body, name=name, grid=(n_steps,),
        in_specs=[pl.BlockSpec((ts, width), lambda i: (i, 0)), pl.BlockSpec((ts, d), lambda i: (i, 0)),
                  vec, vec, cube, cube, cube],
        out_specs=[pl.BlockSpec((ts, width), lambda i: (i, 0)), cube, cube, vec, vec],
        out_shape=[_sds((s_len, width), BF16), cube_shape, cube_shape, _sds((1, d), F32), _sds((1, d), F32)],
        scratch_shapes=[pltpu.VMEM((ts, d), F32), pltpu.VMEM((ts, d), F32)],
        compiler_params=_params("arbitrary"),
    )(zpre, dy, ln_g, ln_b, ws, wst, bsb)


def loss_head(x, g, target, name):
    s_len, d = x.shape
    tm = _row_tile(s_len, 512)

    def body(x_ref, g_ref, t_ref, dx_ref, dxb_ref, dg_ref, loss_ref):
        i = pl.program_id(0)
        xf = x_ref[...]
        gain = g_ref[...]
        r = lax.rsqrt(jnp.mean(xf * xf, axis=-1, keepdims=True) + EPS)
        xhat = xf * r
        err = xhat * gain - t_ref[...]
        row = jnp.mean(err * err, axis=-1, keepdims=True)
        part = 0.5 * jnp.sum(row, axis=0, keepdims=True)
        dy = err * (1.0 / d)
        dg_part = jnp.sum(dy * xhat, axis=0, keepdims=True)

        @pl.when(i == 0)
        def _():
            dg_ref[...] = dg_part
            loss_ref[...] = jnp.broadcast_to(part, (8, LANES))

        @pl.when(i > 0)
        def _():
            dg_ref[...] += dg_part
            loss_ref[...] += jnp.broadcast_to(part, (8, LANES))

        dxh = dy * gain
        dx = r * (dxh - xhat * jnp.mean(dxh * xhat, axis=-1, keepdims=True))
        dx_ref[...] = dx
        dxb_ref[...] = dx.astype(BF16)

    row_spec = pl.BlockSpec((tm, d), lambda i: (i, 0))
    vec = pl.BlockSpec((1, d), lambda i: (0, 0))
    return pl.pallas_call(
        body, name=name, grid=(s_len // tm,),
        in_specs=[row_spec, vec, row_spec],
        out_specs=[row_spec, row_spec, vec, pl.BlockSpec((8, LANES), lambda i: (0, 0))],
        out_shape=[_sds((s_len, d), F32), _sds((s_len, d), BF16), _sds((1, d), F32), _sds((8, LANES), F32)],
        compiler_params=_params("arbitrary"),
    )(x, g, target)


FLIP_BITS = {"c": (0, 0, 1), "x": (1, 0, 0), "y": (0, 1, 0), "xy": (1, 1, 0),
             "xc": (1, 0, 1), "yc": (0, 1, 1), "xyc": (1, 1, 1)}
CHIP_FLIPS = ("x", "y", "xy")


def _flip(pos, name):
    return tuple(1 - p if bit else p for p, bit in zip(pos, FLIP_BITS[name]))


def _chip(pos):
    return 2 * pos[0] + pos[1]


def _me():
    return (lax.axis_index("x"), lax.axis_index("y"), lax.axis_index("c"))


def _exchange_copy(remote, k, in_refs, out_refs, send_sems, recv_sems, sender, receiver):
    ii, src_fn, oi, dst_fn, _ = remote[k]
    return pltpu.make_async_remote_copy(
        src_ref=src_fn(in_refs[ii], sender, receiver), dst_ref=dst_fn(out_refs[oi], sender),
        send_sem=send_sems.at[k], recv_sem=recv_sems.at[k], device_id=receiver, device_id_type=MESH)


def _exchange_start(remote, in_refs, out_refs, send_sems, recv_sems):
    me = _me()
    for k in range(len(remote)):
        _exchange_copy(remote, k, in_refs, out_refs, send_sems, recv_sems, me, _flip(me, remote[k][4])).start()


def _exchange_finish(remote, in_refs, out_refs, send_sems, recv_sems):
    me = _me()
    for k in range(len(remote)):
        _exchange_copy(remote, k, in_refs, out_refs, send_sems, recv_sems, _flip(me, remote[k][4]), me).wait_recv()
    for k in range(len(remote)):
        _exchange_copy(remote, k, in_refs, out_refs, send_sems, recv_sems, me, _flip(me, remote[k][4])).wait_send()


def _exchange_sems(remote):
    n = len(remote)
    return [pltpu.SemaphoreType.DMA((n,)), pltpu.SemaphoreType.DMA((n,))] if n else []


def exchange(name, ins, out_shapes, remote, local):
    n_in, n_out = len(ins), len(out_shapes)

    def body(*refs):
        in_refs, out_refs = refs[:n_in], refs[n_in:n_in + n_out]
        send_sems, recv_sems, local_sems = refs[n_in + n_out:]
        me = _me()
        stays = []
        for k, (ii, src_fn, oi, dst_fn) in enumerate(local):
            cp = pltpu.make_async_copy(src_fn(in_refs[ii], me), dst_fn(out_refs[oi], me), local_sems.at[k])
            cp.start()
            stays.append(cp)
        _exchange_start(remote, in_refs, out_refs, send_sems, recv_sems)
        _exchange_finish(remote, in_refs, out_refs, send_sems, recv_sems)
        for cp in stays:
            cp.wait()

    hbm = pl.BlockSpec(memory_space=pl.ANY)
    return pl.pallas_call(
        body, name=name,
        in_specs=[hbm] * n_in, out_specs=[hbm] * n_out, out_shape=list(out_shapes),
        scratch_shapes=[pltpu.SemaphoreType.DMA((max(len(remote), 1),)),
                        pltpu.SemaphoreType.DMA((max(len(remote), 1),)),
                        pltpu.SemaphoreType.DMA((max(len(local), 1),))],
        compiler_params=pltpu.CompilerParams(has_side_effects=True),
    )(*ins)


def staged_push(name, ins, out_shapes, jobs, n_alias=0):
    n_in, n_out = len(ins), len(out_shapes)
    n_copies = sum(len(dsts) for _, _, dsts in jobs)
    n_remote = sum(1 for _, _, dsts in jobs for d in dsts if d[2] is not None)

    def chunk_of(ii, src_fn):
        probe = _ShapeRef(ins[ii].shape, ins[ii].dtype)
        got = src_fn(probe, (0, 0, 0))
        return tuple(got.shape), got.dtype

    classes = []
    for ii, src_fn, _ in jobs:
        c = chunk_of(ii, src_fn)
        if c not in classes:
            classes.append(c)

    def body(*refs):
        in_refs, out_refs = refs[:n_in], refs[n_in:n_in + n_out]
        bufs = refs[n_in + n_out:n_in + n_out + len(classes)]
        load_sems, out_sems, recv_sems = refs[n_in + n_out + len(classes):]
        me = _me()
        pending = [[[], []] for _ in classes]
        used = [0] * len(classes)
        arrivals = []
        k = r = 0

        def begin_load(job):
            ii, src_fn, _ = job
            cls = classes.index(chunk_of(ii, src_fn))
            slot = used[cls] % 2
            used[cls] += 1
            for kind, cp in pending[cls][slot]:
                cp.wait_send() if kind == "remote" else cp.wait()
            pending[cls][slot] = []
            load = pltpu.make_async_copy(src_fn(in_refs[ii], me), bufs[cls].at[slot], load_sems.at[2 * cls + slot])
            load.start()
            return load, cls, slot

        nxt = begin_load(jobs[0])
        for n, (ii, src_fn, dsts) in enumerate(jobs):
            load, cls, slot = nxt
            load.wait()
            buf = bufs[cls].at[slot]
            sent = []
            for oi, dst_fn, flip in dsts:
                if flip is None:
                    cp = pltpu.make_async_copy(buf, dst_fn(out_refs[oi], me), out_sems.at[k])
                    cp.start()
                    sent.append(("local", cp))
                else:
                    peer = _flip(me, flip)
                    cp = pltpu.make_async_remote_copy(
                        src_ref=buf, dst_ref=dst_fn(out_refs[oi], me), send_sem=out_sems.at[k],
                        recv_sem=recv_sems.at[r], device_id=peer, device_id_type=MESH)
                    cp.start()
                    sent.append(("remote", cp))
                    arrivals.append((r, cls, oi, dst_fn, peer))
                    r += 1
                k += 1
            pending[cls][slot] = sent
            if n + 1 < len(jobs):
                nxt = begin_load(jobs[n + 1])
        for per_class in pending:
            for slot_list in per_class:
                for kind, cp in slot_list:
                    cp.wait_send() if kind == "remote" else cp.wait()
        for r, cls, oi, dst_fn, peer in arrivals:
            pltpu.make_async_remote_copy(
                src_ref=bufs[cls].at[0], dst_ref=dst_fn(out_refs[oi], peer), send_sem=out_sems.at[0],
                recv_sem=recv_sems.at[r], device_id=peer, device_id_type=MESH).wait_recv()

    hbm = pl.BlockSpec(memory_space=pl.ANY)
    return pl.pallas_call(
        body, name=name,
        in_specs=[hbm] * n_in, out_specs=[hbm] * n_out, out_shape=list(out_shapes),
        scratch_shapes=[pltpu.VMEM((2,) + shape, dtype) for shape, dtype in classes]
        + [pltpu.SemaphoreType.DMA((2 * len(classes),)), pltpu.SemaphoreType.DMA((max(n_copies, 1),)),
           pltpu.SemaphoreType.DMA((max(n_remote, 1),))],
        input_output_aliases={i: i for i in range(n_alias)},
        compiler_params=pltpu.CompilerParams(has_side_effects=True, vmem_limit_bytes=VMEM_LIMIT),
    )(*ins)


class _ShapeRef:
    def __init__(self, shape, dtype):
        self.shape, self.dtype = tuple(shape), dtype

    @property
    def at(self):
        return self

    def __getitem__(self, idx):
        idx = idx if isinstance(idx, tuple) else (idx,)
        shape = []
        for dim, i in zip(self.shape, idx):
            if isinstance(i, slice):
                shape.append(len(range(*i.indices(dim))))
            elif hasattr(i, "size") and hasattr(i, "start"):
                shape.append(i.size)
        shape += self.shape[len(idx):]
        return _ShapeRef(shape, self.dtype)


def gather_whole(shards, name):
    whole = lambda ref, sender, receiver=None: ref
    slot = lambda ref, sender: ref.at[_chip(sender)]
    remote = [(t, whole, t, slot, flip) for t in range(len(shards)) for flip in CHIP_FLIPS]
    local = [(t, whole, t, slot) for t in range(len(shards))]
    outs = [_sds((N_CHIPS,) + a.shape, a.dtype) for a in shards]
    return exchange(name, list(shards), outs, remote, local)


def _half_axis(shape):
    return 0 if shape[0] >= 2 else 1


def gather_halves_plan(shards):
    remote = []
    for t, a in enumerate(shards):
        ax = _half_axis(a.shape)
        half = lambda ref, sender, receiver=None, ax=ax: _half(ref, sender[2], ax)
        slot = lambda ref, sender, ax=ax: _half(ref.at[_chip(sender)], sender[2], ax)
        remote += [(t, half, t, slot, flip) for flip in CHIP_FLIPS]
    outs = [_sds((N_CHIPS,) + a.shape, a.dtype) for a in shards]
    return list(shards), outs, remote


def gather_halves_fill(got, shards, name):
    n_t = len(shards)
    jobs = []
    for t, a in enumerate(shards):
        layers = a.shape[0]
        for l in range(layers):
            jobs.append((n_t + t, lambda ref, me, l=l: ref.at[l],
                         [(t, lambda ref, sender, l=l: ref.at[_chip(sender), l], None)]))
        for flip in CHIP_FLIPS:
            if _half_axis(a.shape) == 0:
                n = layers // 2
                for j in range(n):
                    at = lambda ref, pos, flip=flip, j=j, n=n: ref.at[_chip(_flip(pos, flip)), pos[2] * n + j]
                    jobs.append((t, at, [(t, at, "c")]))
            else:
                rows = a.shape[1] // 2
                at = lambda ref, pos, flip=flip, rows=rows: ref.at[
                    _chip(_flip(pos, flip)), 0, pl.ds(pos[2] * rows, rows)]
                jobs.append((t, at, [(t, at, "c")]))
    outs = [_sds(g.shape, g.dtype) for g in got]
    return staged_push(name, list(got) + list(shards), outs, jobs, n_alias=n_t)


def add_to_all(plan, buf):
    ins, outs, remote = plan
    whole = lambda ref, sender, receiver=None: ref
    more = [(len(ins), whole, len(outs), (lambda ref, sender, f=f: ref.at[f]), flip)
            for f, flip in enumerate(FLIPS_BY_INDEX)]
    return list(ins) + [buf], list(outs) + [_sds((len(more),) + buf.shape, buf.dtype)], list(remote) + more


FLIPS_BY_INDEX = ("c", "y", "yc", "x", "xc", "xy", "xyc")


def sum_devices(own, got, name):
    rows = own.shape[0]
    tr = LANES if rows % LANES == 0 else rows
    me = (4 * lax.axis_index("x") + 2 * lax.axis_index("y") + lax.axis_index("c")).astype(jnp.int32).reshape(1)
    everyone = jnp.concatenate([own[None], got], axis=0)

    def body(me_ref, a0, a1, a2, a3, a4, a5, a6, a7, o_ref):
        o_ref[...] = ((a0[...] + a1[...]) + (a2[...] + a3[...])) + ((a4[...] + a5[...]) + (a6[...] + a7[...]))

    return pl.pallas_call(
        body, name=name,
        grid_spec=pltpu.PrefetchScalarGridSpec(
            num_scalar_prefetch=1, grid=(rows // tr,),
            in_specs=[pl.BlockSpec((None, tr, LANES), lambda i, m, k=k: (m[0] ^ k, i, 0)) for k in range(8)],
            out_specs=pl.BlockSpec((tr, LANES), lambda i, m: (i, 0))),
        out_shape=_sds((rows, LANES), F32),
        compiler_params=_params("arbitrary"),
    )(me, *([everyone] * 8))


def _half(ref, core, axis):
    rows = ref.shape[axis] // 2
    idx = (slice(None),) * axis + (pl.ds(core * rows, rows),)
    return ref.at[idx]


def reduce_plan(grads):
    remote, outs = [], []
    for t, g in enumerate(grads):
        outs.append(_sds((len(FLIP_BITS), g.shape[1] // 2, g.shape[2]), BF16))
        for f, flip in enumerate(FLIP_BITS):
            remote.append((t, lambda ref, sender, receiver: _half(ref.at[_chip(receiver)], receiver[2], 0),
                           t, lambda ref, sender, f=f: ref.at[f], flip))
    return list(grads), outs, remote


def reduce_finish(grads, got, stacks, full_shapes, into, name):
    chip = (2 * lax.axis_index("x") + lax.axis_index("y")).astype(jnp.int32).reshape(1)
    core = lax.axis_index("c").astype(jnp.int32).reshape(1)
    totals = [sum_partials(g, r, chip, core, f"{name}_sum{t}") for t, (g, r) in enumerate(zip(grads, got))]
    names = []
    for out_name, _ in stacks:
        if out_name not in names:
            names.append(out_name)
    names = [n for n in names if n in into] + [n for n in names if n not in into]
    kept = [into[n] for n in names if n in into]
    outs = [_sds(full_shapes[n], F32) for n in names]
    jobs = []
    for t, (out_name, layer) in enumerate(stacks):
        oi = names.index(out_name)
        rows, cols = totals[t].shape
        pieces = max(1, rows * cols * 4 // STAGE_BYTES)
        step = rows // pieces
        for q in range(pieces):
            src = lambda ref, me, q=q, step=step: ref.at[pl.ds(q * step, step)]
            place = lambda ref, sender, layer=layer, q=q, step=step, rows=rows: ref.at[
                layer, pl.ds(sender[2] * rows + q * step, step)]
            jobs.append((len(kept) + t, src, [(oi, place, None), (oi, place, "c")]))
    full = staged_push(name + "_share", kept + totals, outs, jobs, n_alias=len(kept))
    return {**into, **dict(zip(names, full))}


STAGE_BYTES = 2 * 1024 * 1024


def sum_partials(mine, theirs, chip, core, name):
    _, rows, cols = mine.shape
    half = rows // 2
    tr = _row_tile(half, 256)
    nb = half // tr
    n_got = theirs.shape[0]

    def body(chip_ref, core_ref, m_ref, *refs):
        acc = m_ref[...].astype(F32)
        for r in refs[:n_got]:
            acc = acc + r[...].astype(F32)
        refs[n_got][...] = acc

    got = lambda f: pl.BlockSpec((None, tr, cols), lambda i, ch, co: (f, i, 0))
    return pl.pallas_call(
        body, name=name,
        grid_spec=pltpu.PrefetchScalarGridSpec(
            num_scalar_prefetch=2, grid=(nb,),
            in_specs=[pl.BlockSpec((None, tr, cols), lambda i, ch, co: (ch[0], co[0] * nb + i, 0))]
            + [got(f) for f in range(n_got)],
            out_specs=pl.BlockSpec((tr, cols), lambda i, ch, co: (i, 0))),
        out_shape=_sds((half, cols), F32),
        compiler_params=_params("arbitrary"),
    )(chip, core, mine, *([theirs] * n_got))


def _tok(t):
    return t.transpose(1, 0, 2).reshape(t.shape[1], t.shape[0] * t.shape[2])


def _heads(t):
    return t.reshape(t.shape[0], t.shape[1] // HEAD_DIM, HEAD_DIM).transpose(1, 0, 2)


def _tile2(vec):
    return jnp.tile(vec.reshape(1, HEAD_DIM), (1, 2))


REST = ("att_w_in", "att_w_out", "sgu_w_in", "sgu_w_out", "mlp_w1", "mlp_w2")
LAST_GROUP = (("att_w_in", 0),)
LATE_SMALL = ("att_norm", "att_sink", "att_qnorm", "att_knorm")


def local_step(x, target, first, rest_shards, rep, full_shapes):
    s_len, d = x.shape
    tabs = rope_tables(s_len)
    depth = rep["mlp_norm"].shape[0]
    row = lambda a: a.reshape(1, -1)
    saved = []
    h = x
    gw = {"att_w_in": [first]}

    def wl(name, idx):
        return (gw[name][idx], 0) if name == "att_w_in" else (gw[name], idx)

    for layer in range(depth):
        i = layer // 2
        tag = f"l{layer}"
        if layer % 2 == 0:
            hn, proj = norm_mm(h, row(rep["att_norm"][i]), *wl("att_w_in", i), F32, tag + "_att_proj")
            qkv_t, kv = prep_fwd(proj, tabs, _tile2(rep["att_qnorm"][i]), _tile2(rep["att_knorm"][i]),
                                 tag + "_att_prep")
            kv_tok = _heads(kv)
            oa, lse_a = flash_fwd_t(qkv_t, kv_tok, QA_COL // GROUP_W, 0, (KA_COL + LANES) // HEAD_DIM,
                                    rep["att_sink"][i], True, tag + "_win_fwd")
            plan = gather_halves_plan(rest_shards) if layer == 0 else None
            ob, lse_b, *got = flash_fwd_t(qkv_t, kv_tok, QB_COL // GROUP_W, 4, (KB_COL + LANES) // HEAD_DIM,
                                          None, False, tag + "_grid_fwd", comm=plan)
            if layer == 0:
                rest = dict(zip(REST, gather_halves_fill(got, rest_shards, "gather_rest_fill")))
                gw["att_w_in"].append(rest.pop("att_w_in"))
                gw.update(rest)
            out = mm_res_t([oa, ob], *wl("att_w_out", i), h, tag + "_att_out")
            mix_saved = (h, hn, proj, qkv_t, kv_tok, oa, ob, lse_a, lse_b)
        else:
            hn, zpre = norm_mm(h, row(rep["sgu_norm"][i]), *wl("sgu_w_in", i), F32, tag + "_sgu_in")
            ws = rep["sgu_w_s"][i].astype(BF16)
            bsb = jnp.broadcast_to(rep["sgu_b_s"][i][:, :, None], (SGU_GROUPS, SGU_CHUNK, LANES))
            y = sgu_mid_fwd(zpre, row(rep["sgu_ln_g"][i]), row(rep["sgu_ln_b"][i]), ws, bsb, tag + "_sgu_mid")
            out = mm_res(y, *wl("sgu_w_out", i), h, tag + "_sgu_out")
            mix_saved = (h, hn, zpre, y, ws, bsb)
        hm, a = norm_mm(out, row(rep["mlp_norm"][layer]), *wl("mlp_w1", layer), BF16, tag + "_mlp_up")
        nxt = mm_res(a, *wl("mlp_w2", layer), out, tag + "_mlp_down", relu2=True)
        saved.append((mix_saved, (out, hm, a)))
        h = nxt
    dh, dhb, d_final, loss_tile = loss_head(h, row(rep["final_norm"]), target, "loss_head")
    big, tags = [], []
    small = {k: [jnp.zeros(v.shape[1:], F32)] * v.shape[0] for k, v in rep.items() if k != "final_norm"}
    small["final_norm"] = d_final.reshape(-1)
    stacked = lambda: [small[n] if n == "final_norm" else jnp.stack(small[n]) for n in SMALL]
    for layer in reversed(range(depth)):
        i = layer // 2
        tag = f"l{layer}"
        mix_saved, (xin, hm, a) = saved[layer]
        da = mm_nt_relu2_bwd(dhb, *wl("mlp_w2", layer), a, tag + "_mlp_down_bwd")
        big.append(dw_mm(a, dhb, tag + "_mlp_dw2", col_sharded=False, relu2=True))
        tags.append(("mlp_w2", layer))
        big.append(dw_mm(hm, da, tag + "_mlp_dw1", col_sharded=True))
        tags.append(("mlp_w1", layer))
        dh, dhb, dg = dx_norm(da, *wl("mlp_w1", layer), xin, row(rep["mlp_norm"][layer]), dh, tag + "_mlp_up_bwd")
        small["mlp_norm"][layer] = dg.reshape(-1)
        if layer % 2 == 0:
            xin, hn, proj, qkv_t, kv_tok, oa, ob, lse_a, lse_b = mix_saved
            do_t = mm_nt(dhb, *wl("att_w_out", i), tag + "_att_out_bwd", transposed=True)
            big.append(dw_nn([oa, ob], dhb, tag + "_att_dwout"))
            tags.append(("att_w_out", i))
            dqa, dka, dva, dsink = flash_bwd_t(qkv_t, kv_tok, oa, do_t, lse_a, QA_COL // GROUP_W, 0, 2,
                                               KA_COL // HEAD_DIM, 0, rep["att_sink"][i], True, tag + "_win_bwd")
            plan = None
            if layer == 0:
                early = [k for k, t in enumerate(tags) if t not in LAST_GROUP]
                early_pack = _pack(stacked() + [loss_tile[0, :1]])
                plan = add_to_all(reduce_plan([big[k] for k in early]), early_pack)
            dqb, dkb, dvb, *got = flash_bwd_t(qkv_t, kv_tok, ob, do_t, lse_b, QB_COL // GROUP_W, 4, 6,
                                              KB_COL // HEAD_DIM, 2, None, False, tag + "_grid_bwd", comm=plan)
            if layer == 0:
                grads = reduce_finish([big[k] for k in early], got[:-1], [tags[k] for k in early], full_shapes,
                                      {}, "grads1")
                early_sum = sum_devices(early_pack, got[-1], "sum_small")
            qg, kg = _tile2(rep["att_qnorm"][i]), _tile2(rep["att_knorm"][i])
            dproj, dqg, dkg = prep_bwd(proj, dqa, _tok(dka), _tok(dva), dqb, _tok(dkb), _tok(dvb),
                                       tabs, qg, kg, tag + "_att_prep_bwd")
            big.append(dw_mm(hn, dproj, tag + "_att_dwin", col_sharded=True))
            tags.append(("att_w_in", i))
            dh, dhb, dg = dx_norm(dproj, *wl("att_w_in", i), xin, row(rep["att_norm"][i]), dh, tag + "_att_proj_bwd")
            small["att_norm"][i] = dg.reshape(-1)
            small["att_sink"][i] = dsink[:, 0, :GROUP].reshape(-1)
            small["att_qnorm"][i] = dqg[0, :HEAD_DIM] + dqg[0, HEAD_DIM:]
            small["att_knorm"][i] = dkg[0, :HEAD_DIM] + dkg[0, HEAD_DIM:]
        else:
            xin, hn, zpre, y, ws, bsb = mix_saved
            dy = mm_nt(dhb, *wl("sgu_w_out", i), tag + "_sgu_out_bwd")
            big.append(dw_mm(y, dhb, tag + "_sgu_dwout", col_sharded=False))
            tags.append(("sgu_w_out", i))
            wst = ws.transpose(0, 2, 1)
            dz, dws, dbs, dlg, dlb = sgu_mid_bwd(zpre, dy, row(rep["sgu_ln_g"][i]), row(rep["sgu_ln_b"][i]),
                                                 ws, wst, bsb, tag + "_sgu_mid_bwd")
            big.append(dw_mm(hn, dz, tag + "_sgu_dwin", col_sharded=True))
            tags.append(("sgu_w_in", i))
            dh, dhb, dg = dx_norm(dz, *wl("sgu_w_in", i), xin, row(rep["sgu_norm"][i]), dh, tag + "_sgu_in_bwd")
            small["sgu_norm"][i] = dg.reshape(-1)
            small["sgu_ln_g"][i] = dlg.reshape(-1)
            small["sgu_ln_b"][i] = dlb.reshape(-1)
            small["sgu_w_s"][i] = dws
            small["sgu_b_s"][i] = dbs[:, :, 0]
    late = [k for k, t in enumerate(tags) if t in LAST_GROUP]
    late_pack = _pack([small[n][0] for n in LATE_SMALL])
    got = exchange("grads2_scatter", *add_to_all(reduce_plan([big[k] for k in late]), late_pack), [])
    grads = reduce_finish([big[k] for k in late], got[:-1], [tags[k] for k in late], full_shapes, grads, "grads2")
    late_sum = sum_devices(late_pack, got[-1], "sum_small_late")
    shapes = [a.shape for a in stacked()]
    *small_g, loss = _unpack(early_sum, shapes + [()])
    small_g = dict(zip(SMALL, small_g))
    for n, g in zip(LATE_SMALL, _unpack(late_sum, [small[n][0].shape for n in LATE_SMALL])):
        small_g[n] = small_g[n].at[0].add(g)
    return loss, dh, grads, small_g


BIG = ("att_w_in", "att_w_out", "sgu_w_in", "sgu_w_out", "mlp_w1", "mlp_w2")
SHARDED_VEC = ("sgu_norm", "sgu_ln_g", "sgu_ln_b")
REPLICATED = ("att_norm", "att_sink", "att_qnorm", "att_knorm", "sgu_w_s", "sgu_b_s", "mlp_norm", "final_norm")
WEIGHTS = ("att_norm", "att_w_in", "att_sink", "att_qnorm", "att_knorm", "att_w_out", "sgu_norm", "sgu_w_in",
           "sgu_ln_g", "sgu_ln_b", "sgu_w_s", "sgu_b_s", "sgu_w_out", "mlp_norm", "mlp_w1", "mlp_w2", "final_norm")
SMALL = tuple(n for n in WEIGHTS if n not in BIG)
PACK_ALIGN = 8 * LANES


def _pack(arrays):
    flat = jnp.concatenate([a.reshape(-1) for a in arrays])
    pad = -flat.shape[0] % PACK_ALIGN
    return jnp.pad(flat, (0, pad)).reshape(-1, LANES)


def _unpack(flat2d, shapes):
    flat = flat2d.reshape(-1)
    out, off = [], 0
    for shape in shapes:
        size = int(np.prod(shape))
        out.append(flat[off:off + size].reshape(shape))
        off += size
    return out


def kernel(x, att_norm, att_w_in, att_sink, att_qnorm, att_knorm, att_w_out, sgu_norm, sgu_w_in, sgu_ln_g, sgu_ln_b, sgu_w_s, sgu_b_s, sgu_w_out, mlp_norm, mlp_w1, mlp_w2, final_norm, loss_target, m_att_norm, m_att_w_in, m_att_sink, m_att_qnorm, m_att_knorm, m_att_w_out, m_sgu_norm, m_sgu_w_in, m_sgu_ln_g, m_sgu_ln_b, m_sgu_w_s, m_sgu_b_s, m_sgu_w_out, m_mlp_norm, m_mlp_w1, m_mlp_w2, m_final_norm, v_att_norm, v_att_w_in, v_att_sink, v_att_qnorm, v_att_knorm, v_att_w_out, v_sgu_norm, v_sgu_w_in, v_sgu_ln_g, v_sgu_ln_b, v_sgu_w_s, v_sgu_b_s, v_sgu_w_out, v_mlp_norm, v_mlp_w1, v_mlp_w2, v_final_norm):
    w = dict(att_norm=att_norm, att_w_in=att_w_in, att_sink=att_sink, att_qnorm=att_qnorm, att_knorm=att_knorm,
             att_w_out=att_w_out, sgu_norm=sgu_norm, sgu_w_in=sgu_w_in, sgu_ln_g=sgu_ln_g, sgu_ln_b=sgu_ln_b,
             sgu_w_s=sgu_w_s, sgu_b_s=sgu_b_s, sgu_w_out=sgu_w_out, mlp_norm=mlp_norm, mlp_w1=mlp_w1,
             mlp_w2=mlp_w2, final_norm=final_norm)
    m = dict(att_norm=m_att_norm, att_w_in=m_att_w_in, att_sink=m_att_sink, att_qnorm=m_att_qnorm,
             att_knorm=m_att_knorm, att_w_out=m_att_w_out, sgu_norm=m_sgu_norm, sgu_w_in=m_sgu_w_in,
             sgu_ln_g=m_sgu_ln_g, sgu_ln_b=m_sgu_ln_b, sgu_w_s=m_sgu_w_s, sgu_b_s=m_sgu_b_s,
             sgu_w_out=m_sgu_w_out, mlp_norm=m_mlp_norm, mlp_w1=m_mlp_w1, mlp_w2=m_mlp_w2,
             final_norm=m_final_norm)
    v = dict(att_norm=v_att_norm, att_w_in=v_att_w_in, att_sink=v_att_sink, att_qnorm=v_att_qnorm,
             att_knorm=v_att_knorm, att_w_out=v_att_w_out, sgu_norm=v_sgu_norm, sgu_w_in=v_sgu_w_in,
             sgu_ln_g=v_sgu_ln_g, sgu_ln_b=v_sgu_ln_b, sgu_w_s=v_sgu_w_s, sgu_b_s=v_sgu_b_s,
             sgu_w_out=v_sgu_w_out, mlp_norm=v_mlp_norm, mlp_w1=v_mlp_w1, mlp_w2=v_mlp_w2,
             final_norm=v_final_norm)
    chip = 2 * lax.axis_index("x") + lax.axis_index("y")

    vecs = jnp.stack([w[n] for n in SHARDED_VEC])
    wb = {n: w[n].astype(BF16) for n in BIG}
    first, vec_all = gather_whole([wb["att_w_in"][0:1], vecs], "gather_first")
    rest_shards = [wb[n][1:2] if n == "att_w_in" else wb[n] for n in REST]
    vec_full = vec_all.transpose(1, 2, 0, 3).reshape(vecs.shape[0], vecs.shape[1], -1)
    rep = {n: w[n] for n in REPLICATED}
    rep.update({n: vec_full[k] for k, n in enumerate(SHARDED_VEC)})

    loss, grad_x, grads, small_g = local_step(x[0], loss_target[0], first, rest_shards, rep,
                                              {n: w[n].shape for n in BIG})
    width = w["sgu_norm"].shape[1]
    for n in SHARDED_VEC:
        small_g[n] = lax.dynamic_slice_in_dim(small_g[n], chip * width, width, axis=1)
    grads.update(small_g)
    for n in BIG:
        grads[n] = grads[n].reshape(w[n].shape)

    delta, new_m, new_v = {}, {}, {}
    for n in WEIGHTS:
        shape = w[n].shape
        two_d = (lambda a: a.reshape(1, -1)) if len(shape) == 1 else (lambda a: a)
        dn, mn, vn = adamw(two_d(w[n]), two_d(grads[n]), two_d(m[n]), two_d(v[n]), "adamw_" + n)
        delta[n], new_m[n], new_v[n] = dn.reshape(shape), mn.reshape(shape), vn.reshape(shape)
    return (loss, grad_x[None], *[grads[n] for n in WEIGHTS], *[delta[n] for n in WEIGHTS],
            *[new_m[n] for n in WEIGHTS], *[new_v[n] for n in WEIGHTS])
```
